```python
import math
import jax
import jax.numpy as jnp
from jax import lax
import numpy as np

D_MODEL = 1024
BATCH = 8
SEQ = 8192
DEPTH = 2

HEAD_DIM = 64
BRANCH_WIDTH = D_MODEL // 2
N_BRANCH = 3
CONV_WIDTH = BRANCH_WIDTH
CONV_K = 3
FOX_HEADS = BRANCH_WIDTH // HEAD_DIM
SWA_HEADS = BRANCH_WIDTH // HEAD_DIM
SWA_KV_HEADS = 2
SWA_GROUP = SWA_HEADS // SWA_KV_HEADS
WINDOW = 128
BLOCK = 128
N_BUCKETS = 32
MAX_DISTANCE = WINDOW
MEM_LEN = 256
X_HEADS = 4
X_HEAD_DIM = D_MODEL // X_HEADS
_FF_RAW = -(-8 * D_MODEL // 3)
D_FF = -(-_FF_RAW // 256) * 256
IN_COLS = (3 * CONV_WIDTH + 3 * FOX_HEADS * HEAD_DIM + FOX_HEADS
           + (SWA_HEADS + 2 * SWA_KV_HEADS) * HEAD_DIM + N_BRANCH * D_MODEL)
RMS_EPS = 1e-6
NEG_INF = -1e30

kernel_name = "hybrid_conv_fox_swa_block"


def rms_norm(x, g):
    xf = x.astype(jnp.float32)
    y = xf * lax.rsqrt(jnp.mean(xf * xf, axis=-1, keepdims=True) + RMS_EPS)
    return (y * g.astype(jnp.float32)).astype(x.dtype)


def split_proj(proj):
    sizes = ([CONV_WIDTH] * 3 + [FOX_HEADS * HEAD_DIM] * 3 + [FOX_HEADS]
             + [SWA_HEADS * HEAD_DIM, SWA_KV_HEADS * HEAD_DIM, SWA_KV_HEADS * HEAD_DIM]
             + [N_BRANCH * D_MODEL])
    parts, off = [], 0
    for s in sizes:
        parts.append(proj[..., off:off + s])
        off += s
    return parts


def short_conv_branch(gate_b, gate_c, u, conv_w):
    z = gate_c * u
    y = lax.conv_general_dilated(
        z, conv_w[:, None, :].astype(z.dtype), window_strides=(1,),
        padding=[(CONV_K - 1, 0)], dimension_numbers=('NWC', 'WIO', 'NWC'),
        feature_group_count=CONV_WIDTH)
    return gate_b * y


def fox_branch(q, k, v, f_logit, f_bias):
    b, s = q.shape[0], q.shape[1]
    nb = s // BLOCK
    q = q.reshape(b, s, FOX_HEADS, HEAD_DIM)
    k = k.reshape(b, s, FOX_HEADS, HEAD_DIM)
    v = v.reshape(b, s, FOX_HEADS, HEAD_DIM)
    log_f = jax.nn.log_sigmoid(f_logit.astype(jnp.float32) + f_bias.astype(jnp.float32))
    c = jnp.cumsum(log_f, axis=1)
    c_k = c.transpose(0, 2, 1)
    q_blocks = q.reshape(b, nb, BLOCK, FOX_HEADS, HEAD_DIM).transpose(1, 0, 2, 3, 4)
    c_blocks = c.reshape(b, nb, BLOCK, FOX_HEADS).transpose(1, 0, 2, 3)
    starts = jnp.arange(nb, dtype=jnp.int32) * BLOCK
    k_pos = jnp.arange(s, dtype=jnp.int32)
    scale = HEAD_DIM ** -0.5

    def one_block(args):
        qi, ci, start = args
        logits = jnp.einsum('bqhd,bkhd->bhqk', qi, k).astype(jnp.float32) * scale
        logits = logits + ci.transpose(0, 2, 1)[..., None] - c_k[:, :, None, :]
        q_pos = start + jnp.arange(BLOCK, dtype=jnp.int32)
        causal = k_pos[None, :] <= q_pos[:, None]
        logits = jnp.where(causal[None, None], logits, NEG_INF)
        p = jax.nn.softmax(logits, axis=-1)
        return jnp.einsum('bhqk,bkhd->bqhd', p.astype(v.dtype), v)

    out = lax.map(one_block, (q_blocks, c_blocks, starts))
    return out.transpose(1, 0, 2, 3, 4).reshape(b, s, FOX_HEADS * HEAD_DIM)


def t5_bucket(n):
    n = jnp.maximum(n, 0)
    max_exact = N_BUCKETS // 2
    large = max_exact + (
        jnp.log(jnp.maximum(n, 1).astype(jnp.float32) / max_exact)
        / math.log(MAX_DISTANCE / max_exact) * (N_BUCKETS - max_exact)).astype(jnp.int32)
    large = jnp.minimum(large, N_BUCKETS - 1)
    return jnp.where(n < max_exact, n, large)


def swa_sink_branch(q, k, v, rel_bias, sink):
    b, s = q.shape[0], q.shape[1]
    nb = s // BLOCK
    qb = q.reshape(b, nb, BLOCK, SWA_KV_HEADS, SWA_GROUP, HEAD_DIM)
    kb = k.reshape(b, nb, BLOCK, SWA_KV_HEADS, HEAD_DIM)
    vb = v.reshape(b, nb, BLOCK, SWA_KV_HEADS, HEAD_DIM)

    def band(t):
        prev = jnp.concatenate([jnp.zeros_like(t[:, :1]), t[:, :-1]], axis=1)
        return jnp.concatenate([prev, t], axis=2)

    k_band, v_band = band(kb), band(vb)
    tq = jnp.arange(BLOCK, dtype=jnp.int32)
    sk = jnp.arange(2 * BLOCK, dtype=jnp.int32)
    dist = BLOCK + tq[:, None] - sk[None, :]
    in_window = (dist >= 0) & (dist < WINDOW)
    key_pos = jnp.arange(nb, dtype=jnp.int32)[:, None] * BLOCK - BLOCK + sk[None, :]
    mask = in_window[None] & (key_pos >= 0)[:, None, :]
    bias = rel_bias.astype(jnp.float32)[t5_bucket(dist)]
    bias = bias.reshape(BLOCK, 2 * BLOCK, SWA_KV_HEADS, SWA_GROUP).transpose(2, 3, 0, 1)
    logits = jnp.einsum('bnqkgd,bnskd->bnkgqs', qb, k_band).astype(jnp.float32) * HEAD_DIM ** -0.5
    logits = jnp.where(mask[None, :, None, None], logits + bias, NEG_INF)
    sink_l = sink.astype(jnp.float32).reshape(SWA_KV_HEADS, SWA_GROUP)[None, None, :, :, None]
    m = jnp.maximum(logits.max(axis=-1), sink_l)
    p = jnp.exp(logits - m[..., None])
    denom = p.sum(axis=-1) + jnp.exp(sink_l - m)
    p = p / denom[..., None]
    out = jnp.einsum('bnkgqs,bnskd->bnqkgd', p.astype(v.dtype), v_band)
    return out.reshape(b, s, SWA_HEADS * HEAD_DIM)


def cross_attention(xn, mem_n, w_q, w_kv, w_o):
    b, s = xn.shape[0], xn.shape[1]
    q = (xn @ w_q).reshape(b, s, X_HEADS, X_HEAD_DIM)
    kv = mem_n @ w_kv
    k = kv[..., :X_HEADS * X_HEAD_DIM].reshape(b, -1, X_HEADS, X_HEAD_DIM)
    v = kv[..., X_HEADS * X_HEAD_DIM:].reshape(b, -1, X_HEADS, X_HEAD_DIM)
    logits = jnp.einsum('bshd,bmhd->bhsm', q, k).astype(jnp.float32) * X_HEAD_DIM ** -0.5
    p = jax.nn.softmax(logits, axis=-1)
    o = jnp.einsum('bhsm,bmhd->bshd', p.astype(v.dtype), v).reshape(b, s, X_HEADS * X_HEAD_DIM)
    return o @ w_o


def swiglu(xn, w_gate, w_up, w_down):
    return (jax.nn.silu(xn @ w_gate) * (xn @ w_up)) @ w_down


def _fwd_setup_inputs(seed: int = 0) -> dict:
    key = jax.random.key(seed)
    ks = jax.random.split(key, 22)
    f32 = jnp.float32

    def nrm(k, shape, scale):
        return scale * jax.random.normal(k, shape, f32)

    def gain(k, shape):
        return 1.0 + 0.1 * jax.random.normal(k, shape, f32)

    return {
        "x": nrm(ks[0], (BATCH, SEQ, D_MODEL), 1.0),
        "mem": nrm(ks[1], (BATCH, MEM_LEN, D_MODEL), 1.0),
        "mix_norm_g": gain(ks[2], (DEPTH, D_MODEL)),
        "w_in": nrm(ks[3], (DEPTH, D_MODEL, IN_COLS), D_MODEL ** -0.5),
        "forget_bias": 4.0 + 0.5 * jax.random.normal(ks[4], (DEPTH, FOX_HEADS), f32),
        "conv_w": nrm(ks[5], (DEPTH, CONV_K, CONV_WIDTH), CONV_K ** -0.5),
        "sink": nrm(ks[6], (DEPTH, SWA_HEADS), 0.5),
        "w_branch": nrm(ks[7], (DEPTH, N_BRANCH, BRANCH_WIDTH, D_MODEL), BRANCH_WIDTH ** -0.5),
        "w_mix_out": nrm(ks[8], (DEPTH, D_MODEL, D_MODEL), D_MODEL ** -0.5),
        "rel_bias": nrm(ks[9], (N_BUCKETS, SWA_HEADS), 0.5),
        "xattn_norm_g": gain(ks[10], (DEPTH, D_MODEL)),
        "mem_norm_g": gain(ks[11], (DEPTH, D_MODEL)),
        "w_xq": nrm(ks[12], (DEPTH, D_MODEL, X_HEADS * X_HEAD_DIM), D_MODEL ** -0.5),
        "w_xkv": nrm(ks[13], (DEPTH, D_MODEL, 2 * X_HEADS * X_HEAD_DIM), D_MODEL ** -0.5),
        "w_xo": nrm(ks[14], (DEPTH, X_HEADS * X_HEAD_DIM, D_MODEL), (X_HEADS * X_HEAD_DIM) ** -0.5),
        "ffn_norm_g": gain(ks[15], (DEPTH, D_MODEL)),
        "w_ffn_gate": nrm(ks[16], (DEPTH, D_MODEL, D_FF), D_MODEL ** -0.5),
        "w_ffn_up": nrm(ks[17], (DEPTH, D_MODEL, D_FF), D_MODEL ** -0.5),
        "w_ffn_down": nrm(ks[18], (DEPTH, D_FF, D_MODEL), D_FF ** -0.5),
        "final_norm_g": gain(ks[19], (D_MODEL,)),
    }


def _fwd_reference(x, mem, mix_norm_g, w_in, forget_bias, conv_w, sink, w_branch, w_mix_out,
              rel_bias, xattn_norm_g, mem_norm_g, w_xq, w_xkv, w_xo, ffn_norm_g,
              w_ffn_gate, w_ffn_up, w_ffn_down, final_norm_g):
    b, s = x.shape[0], x.shape[1]
    for l in range(DEPTH):
        h = rms_norm(x, mix_norm_g[l])
        (c_b, c_c, c_u, f_q, f_k, f_v, f_g, s_q, s_k, s_v, gate_logits) = split_proj(h @ w_in[l])
        y_conv = short_conv_branch(c_b, c_c, c_u, conv_w[l])
        y_fox = fox_branch(f_q, f_k, f_v, f_g, forget_bias[l])
        y_swa = swa_sink_branch(s_q, s_k, s_v, rel_bias, sink[l])
        gates = jax.nn.sigmoid(gate_logits.reshape(b, s, N_BRANCH, D_MODEL))
        merged = (gates[:, :, 0] * (y_conv @ w_branch[l, 0])
                  + gates[:, :, 1] * (y_fox @ w_branch[l, 1])
                  + gates[:, :, 2] * (y_swa @ w_branch[l, 2]))
        x = x + merged @ w_mix_out[l]
        x = x + cross_attention(rms_norm(x, xattn_norm_g[l]), rms_norm(mem, mem_norm_g[l]),
                                w_xq[l], w_xkv[l], w_xo[l])
        x = x + swiglu(rms_norm(x, ffn_norm_g[l]), w_ffn_gate[l], w_ffn_up[l], w_ffn_down[l])
    return rms_norm(x, final_norm_g)


import jax as _jax
import jax.numpy as _jnp

TWIN_FORMAT = 'train_step'
FWD_PARAMS = ['x', 'mem', 'mix_norm_g', 'w_in', 'forget_bias', 'conv_w', 'sink', 'w_branch', 'w_mix_out', 'rel_bias', 'xattn_norm_g', 'mem_norm_g', 'w_xq', 'w_xkv', 'w_xo', 'ffn_norm_g', 'w_ffn_gate', 'w_ffn_up', 'w_ffn_down', 'final_norm_g']
TWIN_WEIGHTS = ['mix_norm_g', 'w_in', 'forget_bias', 'conv_w', 'sink', 'w_branch', 'w_mix_out', 'rel_bias', 'xattn_norm_g', 'mem_norm_g', 'w_xq', 'w_xkv', 'w_xo', 'ffn_norm_g', 'w_ffn_gate', 'w_ffn_up', 'w_ffn_down', 'final_norm_g']
TWIN_DIFF_INPUT = 'x'
TWIN_INPUTS = ['x', 'mem', 'mix_norm_g', 'w_in', 'forget_bias', 'conv_w', 'sink', 'w_branch', 'w_mix_out', 'rel_bias', 'xattn_norm_g', 'mem_norm_g', 'w_xq', 'w_xkv', 'w_xo', 'ffn_norm_g', 'w_ffn_gate', 'w_ffn_up', 'w_ffn_down', 'final_norm_g', 'loss_target', 'm_mix_norm_g', 'm_w_in', 'm_forget_bias', 'm_conv_w', 'm_sink', 'm_w_branch', 'm_w_mix_out', 'm_rel_bias', 'm_xattn_norm_g', 'm_mem_norm_g', 'm_w_xq', 'm_w_xkv', 'm_w_xo', 'm_ffn_norm_g', 'm_w_ffn_gate', 'm_w_ffn_up', 'm_w_ffn_down', 'm_final_norm_g', 'v_mix_norm_g', 'v_w_in', 'v_forget_bias', 'v_conv_w', 'v_sink', 'v_w_branch', 'v_w_mix_out', 'v_rel_bias', 'v_xattn_norm_g', 'v_mem_norm_g', 'v_w_xq', 'v_w_xkv', 'v_w_xo', 'v_ffn_norm_g', 'v_w_ffn_gate', 'v_w_ffn_up', 'v_w_ffn_down', 'v_final_norm_g']
TWIN_OUTPUTS = ['loss', 'grad_x', 'grad_mix_norm_g', 'grad_w_in', 'grad_forget_bias', 'grad_conv_w', 'grad_sink', 'grad_w_branch', 'grad_w_mix_out', 'grad_rel_bias', 'grad_xattn_norm_g', 'grad_mem_norm_g', 'grad_w_xq', 'grad_w_xkv', 'grad_w_xo', 'grad_ffn_norm_g', 'grad_w_ffn_gate', 'grad_w_ffn_up', 'grad_w_ffn_down', 'grad_final_norm_g', 'delta_mix_norm_g', 'delta_w_in', 'delta_forget_bias', 'delta_conv_w', 'delta_sink', 'delta_w_branch', 'delta_w_mix_out', 'delta_rel_bias', 'delta_xattn_norm_g', 'delta_mem_norm_g', 'delta_w_xq', 'delta_w_xkv', 'delta_w_xo', 'delta_ffn_norm_g', 'delta_w_ffn_gate', 'delta_w_ffn_up', 'delta_w_ffn_down', 'delta_final_norm_g', 'new_m_mix_norm_g', 'new_m_w_in', 'new_m_forget_bias', 'new_m_conv_w', 'new_m_sink', 'new_m_w_branch', 'new_m_w_mix_out', 'new_m_rel_bias', 'new_m_xattn_norm_g', 'new_m_mem_norm_g', 'new_m_w_xq', 'new_m_w_xkv', 'new_m_w_xo', 'new_m_ffn_norm_g', 'new_m_w_ffn_gate', 'new_m_w_ffn_up', 'new_m_w_ffn_down', 'new_m_final_norm_g', 'new_v_mix_norm_g', 'new_v_w_in', 'new_v_forget_bias', 'new_v_conv_w', 'new_v_sink', 'new_v_w_branch', 'new_v_w_mix_out', 'new_v_rel_bias', 'new_v_xattn_norm_g', 'new_v_mem_norm_g', 'new_v_w_xq', 'new_v_w_xkv', 'new_v_w_xo', 'new_v_ffn_norm_g', 'new_v_w_ffn_gate', 'new_v_w_ffn_up', 'new_v_w_ffn_down', 'new_v_final_norm_g']
TWIN_LEAF_KINDS = {'loss': 'loss', 'grad_x': 'grad_x', 'grad_mix_norm_g': 'grad_w', 'grad_w_in': 'grad_w', 'grad_forget_bias': 'grad_w', 'grad_conv_w': 'grad_w', 'grad_sink': 'grad_w', 'grad_w_branch': 'grad_w', 'grad_w_mix_out': 'grad_w', 'grad_rel_bias': 'grad_w', 'grad_xattn_norm_g': 'grad_w', 'grad_mem_norm_g': 'grad_w', 'grad_w_xq': 'grad_w', 'grad_w_xkv': 'grad_w', 'grad_w_xo': 'grad_w', 'grad_ffn_norm_g': 'grad_w', 'grad_w_ffn_gate': 'grad_w', 'grad_w_ffn_up': 'grad_w', 'grad_w_ffn_down': 'grad_w', 'grad_final_norm_g': 'grad_w', 'delta_mix_norm_g': 'delta_w', 'delta_w_in': 'delta_w', 'delta_forget_bias': 'delta_w', 'delta_conv_w': 'delta_w', 'delta_sink': 'delta_w', 'delta_w_branch': 'delta_w', 'delta_w_mix_out': 'delta_w', 'delta_rel_bias': 'delta_w', 'delta_xattn_norm_g': 'delta_w', 'delta_mem_norm_g': 'delta_w', 'delta_w_xq': 'delta_w', 'delta_w_xkv': 'delta_w', 'delta_w_xo': 'delta_w', 'delta_ffn_norm_g': 'delta_w', 'delta_w_ffn_gate': 'delta_w', 'delta_w_ffn_up': 'delta_w', 'delta_w_ffn_down': 'delta_w', 'delta_final_norm_g': 'delta_w', 'new_m_mix_norm_g': 'new_m', 'new_m_w_in': 'new_m', 'new_m_forget_bias': 'new_m', 'new_m_conv_w': 'new_m', 'new_m_sink': 'new_m', 'new_m_w_branch': 'new_m', 'new_m_w_mix_out': 'new_m', 'new_m_rel_bias': 'new_m', 'new_m_xattn_norm_g': 'new_m', 'new_m_mem_norm_g': 'new_m', 'new_m_w_xq': 'new_m', 'new_m_w_xkv': 'new_m', 'new_m_w_xo': 'new_m', 'new_m_ffn_norm_g': 'new_m', 'new_m_w_ffn_gate': 'new_m', 'new_m_w_ffn_up': 'new_m', 'new_m_w_ffn_down': 'new_m', 'new_m_final_norm_g': 'new_m', 'new_v_mix_norm_g': 'new_v', 'new_v_w_in': 'new_v', 'new_v_forget_bias': 'new_v', 'new_v_conv_w': 'new_v', 'new_v_sink': 'new_v', 'new_v_w_branch': 'new_v', 'new_v_w_mix_out': 'new_v', 'new_v_rel_bias': 'new_v', 'new_v_xattn_norm_g': 'new_v', 'new_v_mem_norm_g': 'new_v', 'new_v_w_xq': 'new_v', 'new_v_w_xkv': 'new_v', 'new_v_w_xo': 'new_v', 'new_v_ffn_norm_g': 'new_v', 'new_v_w_ffn_gate': 'new_v', 'new_v_w_ffn_up': 'new_v', 'new_v_w_ffn_down': 'new_v', 'new_v_final_norm_g': 'new_v'}


def _forward(args):
    return _fwd_reference(*[args[k] for k in FWD_PARAMS])


def _output_shape():
    def fwd():
        inp = _fwd_setup_inputs(0)
        return _fwd_reference(*[inp[k] for k in FWD_PARAMS])
    out = _jax.eval_shape(fwd)
    return out.shape, out.dtype

N_MICROBATCH = 1
ADAM_LR = 0.001
ADAM_B1 = 0.9
ADAM_B2 = 0.999
ADAM_EPS = 1e-08
ADAM_WD = 0.01
ADAM_STEP = 10
PER_EXAMPLE_BATCH_AXIS = {'x': 0, 'mem': 0, 'loss_target': 0}
SHARED_INPUTS = []
_WEIGHT_DTYPES = {'mix_norm_g': _jnp.float32, 'w_in': _jnp.float32, 'forget_bias': _jnp.float32, 'conv_w': _jnp.float32, 'sink': _jnp.float32, 'w_branch': _jnp.float32, 'w_mix_out': _jnp.float32, 'rel_bias': _jnp.float32, 'xattn_norm_g': _jnp.float32, 'mem_norm_g': _jnp.float32, 'w_xq': _jnp.float32, 'w_xkv': _jnp.float32, 'w_xo': _jnp.float32, 'ffn_norm_g': _jnp.float32, 'w_ffn_gate': _jnp.float32, 'w_ffn_up': _jnp.float32, 'w_ffn_down': _jnp.float32, 'final_norm_g': _jnp.float32}
MOMENT_SCALE = {'mix_norm_g': 2.684210e-01, 'w_in': 9.952049e-02, 'forget_bias': 1.522034e-01, 'conv_w': 2.113000e-01, 'sink': 3.180257e-02, 'w_branch': 8.613275e-02, 'w_mix_out': 1.518410e-01, 'rel_bias': 7.315095e-02, 'xattn_norm_g': 2.750027e-02, 'mem_norm_g': 4.362313e-02, 'w_xq': 2.786435e-02, 'w_xkv': 2.864814e-02, 'w_xo': 2.937972e-02, 'ffn_norm_g': 1.851630e-01, 'w_ffn_gate': 7.807728e-02, 'w_ffn_up': 7.744290e-02, 'w_ffn_down': 1.295093e-01, 'final_norm_g': 6.418222e+01}


def _to_microbatches(a, axis):
    t = _jnp.moveaxis(a, axis, 0)
    t = t.reshape((N_MICROBATCH, t.shape[0] // N_MICROBATCH) + t.shape[1:])
    return _jnp.moveaxis(t, 1, axis + 1)


def setup_inputs(seed: int = 0) -> dict:
    inp = _fwd_setup_inputs(seed)
    key = _jax.random.fold_in(_jax.random.key(seed), 7919)
    shape, _ = _output_shape()
    out = dict(inp)
    out["loss_target"] = _jax.random.normal(_jax.random.fold_in(key, 0), shape, _jnp.float32)
    for i, name in enumerate(TWIN_WEIGHTS):
        w = inp[name].astype(_jnp.float32)
        if MOMENT_SCALE is None:
            s = _jnp.sqrt(_jnp.mean(_jnp.square(w)) + 1e-30)
        else:
            s = MOMENT_SCALE[name]
        km, kv = _jax.random.split(_jax.random.fold_in(key, i + 1))
        out[name] = w
        out["m_" + name] = s * _jax.random.normal(km, w.shape, _jnp.float32)
        out["v_" + name] = (s * s) * _jax.random.uniform(kv, w.shape, _jnp.float32, 0.5, 1.5)
    if N_MICROBATCH > 1:
        for name, axis in PER_EXAMPLE_BATCH_AXIS.items():
            out[name] = _to_microbatches(out[name], axis)
    return {'x': out['x'], 'mem': out['mem'], 'mix_norm_g': out['mix_norm_g'], 'w_in': out['w_in'], 'forget_bias': out['forget_bias'], 'conv_w': out['conv_w'], 'sink': out['sink'], 'w_branch': out['w_branch'], 'w_mix_out': out['w_mix_out'], 'rel_bias': out['rel_bias'], 'xattn_norm_g': out['xattn_norm_g'], 'mem_norm_g': out['mem_norm_g'], 'w_xq': out['w_xq'], 'w_xkv': out['w_xkv'], 'w_xo': out['w_xo'], 'ffn_norm_g': out['ffn_norm_g'], 'w_ffn_gate': out['w_ffn_gate'], 'w_ffn_up': out['w_ffn_up'], 'w_ffn_down': out['w_ffn_down'], 'final_norm_g': out['final_norm_g'], 'loss_target': out['loss_target'], 'm_mix_norm_g': out['m_mix_norm_g'], 'm_w_in': out['m_w_in'], 'm_forget_bias': out['m_forget_bias'], 'm_conv_w': out['m_conv_w'], 'm_sink': out['m_sink'], 'm_w_branch': out['m_w_branch'], 'm_w_mix_out': out['m_w_mix_out'], 'm_rel_bias': out['m_rel_bias'], 'm_xattn_norm_g': out['m_xattn_norm_g'], 'm_mem_norm_g': out['m_mem_norm_g'], 'm_w_xq': out['m_w_xq'], 'm_w_xkv': out['m_w_xkv'], 'm_w_xo': out['m_w_xo'], 'm_ffn_norm_g': out['m_ffn_norm_g'], 'm_w_ffn_gate': out['m_w_ffn_gate'], 'm_w_ffn_up': out['m_w_ffn_up'], 'm_w_ffn_down': out['m_w_ffn_down'], 'm_final_norm_g': out['m_final_norm_g'], 'v_mix_norm_g': out['v_mix_norm_g'], 'v_w_in': out['v_w_in'], 'v_forget_bias': out['v_forget_bias'], 'v_conv_w': out['v_conv_w'], 'v_sink': out['v_sink'], 'v_w_branch': out['v_w_branch'], 'v_w_mix_out': out['v_w_mix_out'], 'v_rel_bias': out['v_rel_bias'], 'v_xattn_norm_g': out['v_xattn_norm_g'], 'v_mem_norm_g': out['v_mem_norm_g'], 'v_w_xq': out['v_w_xq'], 'v_w_xkv': out['v_w_xkv'], 'v_w_xo': out['v_w_xo'], 'v_ffn_norm_g': out['v_ffn_norm_g'], 'v_w_ffn_gate': out['v_w_ffn_gate'], 'v_w_ffn_up': out['v_w_ffn_up'], 'v_w_ffn_down': out['v_w_ffn_down'], 'v_final_norm_g': out['v_final_norm_g']}


def _loss(weights, diff, rest, loss_target):
    with _jax.named_scope("forward"):
        args = {**rest, TWIN_DIFF_INPUT: diff, **{k: w.astype(_WEIGHT_DTYPES[k]) for k, w in weights.items()}}
        y = _forward(args)
    with _jax.named_scope("loss_head"):
        err = _jnp.square(y.astype(_jnp.float32) - loss_target)
        return 0.5 * _jnp.sum(_jnp.mean(err, axis=-1)) if err.ndim else 0.5 * err


def _adamw(w, g, m, v):
    m = ADAM_B1 * m + (1.0 - ADAM_B1) * g
    v = ADAM_B2 * v + (1.0 - ADAM_B2) * _jnp.square(g)
    m_hat = m / (1.0 - ADAM_B1 ** ADAM_STEP)
    v_hat = v / (1.0 - ADAM_B2 ** ADAM_STEP)
    delta = -ADAM_LR * (m_hat / (_jnp.sqrt(v_hat) + ADAM_EPS) + ADAM_WD * w)
    return delta, m, v


def reference(x, mem, mix_norm_g, w_in, forget_bias, conv_w, sink, w_branch, w_mix_out, rel_bias, xattn_norm_g, mem_norm_g, w_xq, w_xkv, w_xo, ffn_norm_g, w_ffn_gate, w_ffn_up, w_ffn_down, final_norm_g, loss_target, m_mix_norm_g, m_w_in, m_forget_bias, m_conv_w, m_sink, m_w_branch, m_w_mix_out, m_rel_bias, m_xattn_norm_g, m_mem_norm_g, m_w_xq, m_w_xkv, m_w_xo, m_ffn_norm_g, m_w_ffn_gate, m_w_ffn_up, m_w_ffn_down, m_final_norm_g, v_mix_norm_g, v_w_in, v_forget_bias, v_conv_w, v_sink, v_w_branch, v_w_mix_out, v_rel_bias, v_xattn_norm_g, v_mem_norm_g, v_w_xq, v_w_xkv, v_w_xo, v_ffn_norm_g, v_w_ffn_gate, v_w_ffn_up, v_w_ffn_down, v_final_norm_g):
    given = dict(x=x, mem=mem, mix_norm_g=mix_norm_g, w_in=w_in, forget_bias=forget_bias, conv_w=conv_w, sink=sink, w_branch=w_branch, w_mix_out=w_mix_out, rel_bias=rel_bias, xattn_norm_g=xattn_norm_g, mem_norm_g=mem_norm_g, w_xq=w_xq, w_xkv=w_xkv, w_xo=w_xo, ffn_norm_g=ffn_norm_g, w_ffn_gate=w_ffn_gate, w_ffn_up=w_ffn_up, w_ffn_down=w_ffn_down, final_norm_g=final_norm_g, loss_target=loss_target, m_mix_norm_g=m_mix_norm_g, m_w_in=m_w_in, m_forget_bias=m_forget_bias, m_conv_w=m_conv_w, m_sink=m_sink, m_w_branch=m_w_branch, m_w_mix_out=m_w_mix_out, m_rel_bias=m_rel_bias, m_xattn_norm_g=m_xattn_norm_g, m_mem_norm_g=m_mem_norm_g, m_w_xq=m_w_xq, m_w_xkv=m_w_xkv, m_w_xo=m_w_xo, m_ffn_norm_g=m_ffn_norm_g, m_w_ffn_gate=m_w_ffn_gate, m_w_ffn_up=m_w_ffn_up, m_w_ffn_down=m_w_ffn_down, m_final_norm_g=m_final_norm_g, v_mix_norm_g=v_mix_norm_g, v_w_in=v_w_in, v_forget_bias=v_forget_bias, v_conv_w=v_conv_w, v_sink=v_sink, v_w_branch=v_w_branch, v_w_mix_out=v_w_mix_out, v_rel_bias=v_rel_bias, v_xattn_norm_g=v_xattn_norm_g, v_mem_norm_g=v_mem_norm_g, v_w_xq=v_w_xq, v_w_xkv=v_w_xkv, v_w_xo=v_w_xo, v_ffn_norm_g=v_ffn_norm_g, v_w_ffn_gate=v_w_ffn_gate, v_w_ffn_up=v_w_ffn_up, v_w_ffn_down=v_w_ffn_down, v_final_norm_g=v_final_norm_g)
    weights = {n: given[n] for n in TWIN_WEIGHTS}
    shared = {n: given[n] for n in SHARED_INPUTS}
    per_example = {n: given[n] for n in ['x', 'mem']}
    grad_fn = _jax.value_and_grad(_loss, argnums=(0, 1))

    def one_microbatch(ex, loss_target):
        ex = dict(ex)
        diff = ex.pop(TWIN_DIFF_INPUT)
        return grad_fn(weights, diff, {**shared, **ex}, loss_target)

    if N_MICROBATCH == 1:
        loss, (grad_w, grad_x) = one_microbatch(per_example, given["loss_target"])
    else:
        def body(carry, xs):
            loss_sum, grad_sum = carry
            l_k, (gw_k, gx_k) = one_microbatch(xs[0], xs[1])
            with _jax.named_scope("update"):
                return (loss_sum + l_k, _jax.tree.map(_jnp.add, grad_sum, gw_k)), gx_k

        init = (_jnp.zeros((), _jnp.float32), _jax.tree.map(_jnp.zeros_like, weights))
        (loss, grad_w), grad_x = _jax.lax.scan(body, init, (per_example, given["loss_target"]))
    with _jax.named_scope("update"):
        delta_w, new_m, new_v = {}, {}, {}
        for n in TWIN_WEIGHTS:
            delta_w[n], new_m[n], new_v[n] = _adamw(weights[n], grad_w[n], given["m_" + n], given["v_" + n])
    return (loss, grad_x, *[grad_w[n] for n in TWIN_WEIGHTS], *[delta_w[n] for n in TWIN_WEIGHTS],
            *[new_m[n] for n in TWIN_WEIGHTS], *[new_v[n] for n in TWIN_WEIGHTS])
```

```python
import math

import numpy as np
import jax
import jax.numpy as jnp
from jax import lax
from jax.experimental import pallas as pl
from jax.experimental.pallas import tpu as pltpu

F32 = jnp.float32
BF16 = jnp.bfloat16
I32 = jnp.int32

D_MODEL = 1024
DEPTH = 2
HEAD_DIM = 64
BRANCH = 512
N_BUCKETS = 32
WINDOW = 128
MEM_LEN = 256
X_HEADS = 4
X_HEAD_DIM = 256
D_FF = 2816
IN_COLS = 6920
PROJ_MAIN = 6912
PROJ_PAD = 7040
RMS_EPS = 1e-6
NEG = -1e30
ATT_SCALE = 0.125
X_SCALE = 0.0625

ADAM_LR = 0.001
ADAM_B1 = 0.9
ADAM_B2 = 0.999
ADAM_EPS = 1e-08
ADAM_WD = 0.01
ADAM_STEP = 10

VMEM_LIMIT = 48 * 1024 * 1024
MESH = pl.DeviceIdType.MESH

CB_GATE = (0, 1, 2)
CB_B, CB_C, CB_U, CB_FQ, CB_FK, CB_FV, CB_SQ = 6, 7, 8, 9, 10, 11, 12
CB_SK, CB_SV = 52, 53


def _cp(sem):
    return pltpu.CompilerParams(dimension_semantics=sem, vmem_limit_bytes=VMEM_LIMIT)


def _pick(n, prefs):
    for p in prefs:
        if p <= n and n % p == 0:
            return p
    return n


def _dot(a, b, dims):
    return lax.dot_general(a, b, (dims, ((), ())), preferred_element_type=F32)


def _dot_nn(a, b):
    return _dot(a, b, ((1,), (0,)))


def _dot_nt(a, b):
    return _dot(a, b, ((1,), (1,)))


def _dot_tn(a, b):
    return _dot(a, b, ((0,), (0,)))


def _mm(a, b, mode, out_dtype, name, res=None, bm=1024, bn=1024, bk=1024):
    if mode == "nn":
        (M, K), (K2, N) = a.shape, b.shape
    elif mode == "nt":
        (M, K), (N, K2) = a.shape, b.shape
    else:
        (K, M), (K2, N) = a.shape, b.shape
    assert K == K2, (name, a.shape, b.shape)
    bm = _pick(M, (bm, 1024, 512, 256, 128))
    bn = _pick(N, (bn, 1024, 768, 640, 512, 384, 256, 128))
    bk = _pick(K, (bk, 1024, 768, 640, 512, 384, 256, 128))
    nk = K // bk
    if mode == "tn":
        a_spec = pl.BlockSpec((bk, bm), lambda i, j, k: (k, i))
    else:
        a_spec = pl.BlockSpec((bm, bk), lambda i, j, k: (i, k))
    if mode == "nt":
        b_spec = pl.BlockSpec((bn, bk), lambda i, j, k: (j, k))
    else:
        b_spec = pl.BlockSpec((bk, bn), lambda i, j, k: (k, j))
    dims = {"nn": ((1,), (0,)), "nt": ((1,), (1,)), "tn": ((0,), (0,))}[mode]
    o_spec = pl.BlockSpec((bm, bn), lambda i, j, k: (i, j))
    has_res = res is not None

    def body(*refs):
        if has_res:
            a_ref, b_ref, r_ref, o_ref = refs[:4]
            scr = refs[4:]
        else:
            a_ref, b_ref, o_ref = refs[:3]
            r_ref = None
            scr = refs[3:]
        p = _dot(a_ref[...].astype(BF16), b_ref[...].astype(BF16), dims)
        if nk == 1:
            if has_res:
                p = p + r_ref[...]
            o_ref[...] = p.astype(out_dtype)
        else:
            acc = scr[0]
            k = pl.program_id(2)

            @pl.when(k == 0)
            def _():
                acc[...] = p

            @pl.when(k > 0)
            def _():
                acc[...] += p

            @pl.when(k == nk - 1)
            def _():
                r = acc[...]
                if has_res:
                    r = r + r_ref[...]
                o_ref[...] = r.astype(out_dtype)

    ins = [a, b] + ([res] if has_res else [])
    in_specs = [a_spec, b_spec] + ([o_spec] if has_res else [])
    return pl.pallas_call(
        body, name=name, grid=(M // bm, N // bn, nk),
        in_specs=in_specs, out_specs=o_spec,
        out_shape=jax.ShapeDtypeStruct((M, N), out_dtype),
        scratch_shapes=[pltpu.VMEM((bm, bn), F32)] if nk > 1 else [],
        compiler_params=_cp(("parallel", "parallel", "arbitrary")),
    )(*ins)


def _rms_fwd(x, g, name):
    T, Dm = x.shape
    bt = _pick(T, (512, 256))

    def body(x_ref, g_ref, o_ref):
        xv = x_ref[...]
        r = lax.rsqrt(jnp.mean(xv * xv, axis=-1, keepdims=True) + RMS_EPS)
        o_ref[...] = ((xv * r) * g_ref[...]).astype(BF16)

    return pl.pallas_call(
        body, name=name, grid=(T // bt,),
        in_specs=[pl.BlockSpec((bt, Dm), lambda i: (i, 0)), pl.BlockSpec((1, Dm), lambda i: (0, 0))],
        out_specs=pl.BlockSpec((bt, Dm), lambda i: (i, 0)),
        out_shape=jax.ShapeDtypeStruct((T, Dm), BF16),
        compiler_params=_cp(("parallel",)),
    )(x, g)


def _rms_bwd(x, g, dh, dres, name):
    T, Dm = x.shape
    bt = _pick(T, (512, 256))
    want_dx = dres is not None

    def body(*refs):
        if want_dx:
            x_ref, g_ref, dh_ref, dr_ref, dx_ref, dg_ref = refs
        else:
            x_ref, g_ref, dh_ref, dg_ref = refs
        xv = x_ref[...]
        r = lax.rsqrt(jnp.mean(xv * xv, axis=-1, keepdims=True) + RMS_EPS)
        xh = xv * r
        dhv = dh_ref[...].astype(F32)

        @pl.when(pl.program_id(0) == 0)
        def _():
            dg_ref[...] = jnp.zeros_like(dg_ref)

        dg_ref[...] += jnp.sum(dhv * xh, axis=0, keepdims=True)
        if want_dx:
            dyg = dhv * g_ref[...]
            dx_ref[...] = dr_ref[...] + r * (dyg - xh * jnp.mean(dyg * xh, axis=-1, keepdims=True))

    row = pl.BlockSpec((bt, Dm), lambda i: (i, 0))
    vec = pl.BlockSpec((1, Dm), lambda i: (0, 0))
    if want_dx:
        return pl.pallas_call(
            body, name=name, grid=(T // bt,),
            in_specs=[row, vec, row, row], out_specs=[row, vec],
            out_shape=[jax.ShapeDtypeStruct((T, Dm), F32), jax.ShapeDtypeStruct((1, Dm), F32)],
            compiler_params=_cp(("arbitrary",)),
        )(x, g, dh, dres)
    return None, pl.pallas_call(
        body, name=name, grid=(T // bt,),
        in_specs=[row, vec, row], out_specs=vec,
        out_shape=jax.ShapeDtypeStruct((1, Dm), F32),
        compiler_params=_cp(("arbitrary",)),
    )(x, g, dh)


def _final_loss(x, g, tgt, name):
    T, Dm = x.shape
    bt = _pick(T, (512, 256))

    def body(x_ref, g_ref, t_ref, loss_ref, dx_ref, dg_ref):
        xv = x_ref[...]
        r = lax.rsqrt(jnp.mean(xv * xv, axis=-1, keepdims=True) + RMS_EPS)
        xh = xv * r
        gv = g_ref[...]
        err = xh * gv - t_ref[...]

        @pl.when(pl.program_id(0) == 0)
        def _():
            dg_ref[...] = jnp.zeros_like(dg_ref)
            loss_ref[...] = jnp.zeros_like(loss_ref)

        loss_ref[...] += jnp.sum(err * err) * (0.5 / Dm)
        dy = err * (1.0 / Dm)
        dg_ref[...] += jnp.sum(dy * xh, axis=0, keepdims=True)
        dyg = dy * gv
        dx_ref[...] = r * (dyg - xh * jnp.mean(dyg * xh, axis=-1, keepdims=True))

    row = pl.BlockSpec((bt, Dm), lambda i: (i, 0))
    vec = pl.BlockSpec((1, Dm), lambda i: (0, 0))
    return pl.pallas_call(
        body, name=name, grid=(T // bt,),
        in_specs=[row, vec, row],
        out_specs=[pl.BlockSpec((1, 128), lambda i: (0, 0)), row, vec],
        out_shape=[jax.ShapeDtypeStruct((1, 128), F32), jax.ShapeDtypeStruct((T, Dm), F32),
                   jax.ShapeDtypeStruct((1, Dm), F32)],
        compiler_params=_cp(("arbitrary",)),
    )(x, g, tgt)


HALO = 16


def _shift_down(z, zprev, s):
    rolled = pltpu.roll(z, s, 0)
    hp = pltpu.roll(zprev, s, 0)
    row = lax.broadcasted_iota(I32, hp.shape, 0)
    top = jnp.where(row < s, hp, rolled[:HALO])
    return jnp.concatenate([top, rolled[HALO:]], axis=0)


def _shift_up(z, znext, s):
    n = z.shape[0]
    rolled = pltpu.roll(z, n - s, 0)
    hn = pltpu.roll(znext, HALO - s, 0)
    row = lax.broadcasted_iota(I32, hn.shape, 0)
    bot = jnp.where(row >= HALO - s, hn, rolled[n - HALO:])
    return jnp.concatenate([rolled[:n - HALO], bot], axis=0)


def _conv_fwd(pm, cw, name):
    T = pm.shape[0]
    bt = _pick(T, (512, 256))
    hb = bt // HALO

    def body(b_ref, c_ref, u_ref, cp_ref, up_ref, w_ref, o_ref):
        i = pl.program_id(0)
        z = c_ref[...].astype(F32) * u_ref[...].astype(F32)
        zp = cp_ref[...].astype(F32) * up_ref[...].astype(F32)
        zp = jnp.where(i > 0, zp, 0.0)
        w = w_ref[...]
        y = w[2:3] * z + w[1:2] * _shift_down(z, zp, 1) + w[0:1] * _shift_down(z, zp, 2)
        o_ref[...] = (b_ref[...].astype(F32) * y).astype(BF16)

    def col(cb):
        return pl.BlockSpec((bt, BRANCH), lambda i: (i, cb))

    def prev(cb):
        return pl.BlockSpec((HALO, BRANCH), lambda i: (jnp.maximum(i * hb - 1, 0), cb))

    return pl.pallas_call(
        body, name=name, grid=(T // bt,),
        in_specs=[col(CB_B), col(CB_C), col(CB_U), prev(CB_C), prev(CB_U),
                  pl.BlockSpec((8, BRANCH), lambda i: (0, 0))],
        out_specs=pl.BlockSpec((bt, BRANCH), lambda i: (i, 0)),
        out_shape=jax.ShapeDtypeStruct((T, BRANCH), BF16),
        compiler_params=_cp(("parallel",)),
    )(pm, pm, pm, pm, pm, cw)


def _conv_bwd(pm, cw, dy, name):
    T = pm.shape[0]
    bt = _pick(T, (512, 256))
    hb = bt // HALO
    nb = T // bt
    last_h = T // HALO - 1

    def body(b_ref, c_ref, u_ref, cp_ref, up_ref, bn_ref, dy_ref, dyn_ref, w_ref,
             db_ref, dc_ref, du_ref, dw_ref):
        i = pl.program_id(0)
        cv = c_ref[...].astype(F32)
        uv = u_ref[...].astype(F32)
        bv = b_ref[...].astype(F32)
        z = cv * uv
        zp = jnp.where(i > 0, cp_ref[...].astype(F32) * up_ref[...].astype(F32), 0.0)
        w = w_ref[...]
        z1 = _shift_down(z, zp, 1)
        z2 = _shift_down(z, zp, 2)
        yc = w[2:3] * z + w[1:2] * z1 + w[0:1] * z2
        dyv = dy_ref[...].astype(F32)
        db_ref[...] = (dyv * yc).astype(BF16)
        g = dyv * bv
        gn = jnp.where(i < nb - 1, dyn_ref[...].astype(F32) * bn_ref[...].astype(F32), 0.0)
        dz = w[2:3] * g + w[1:2] * _shift_up(g, gn, 1) + w[0:1] * _shift_up(g, gn, 2)
        dc_ref[...] = (dz * uv).astype(BF16)
        du_ref[...] = (dz * cv).astype(BF16)

        @pl.when(i == 0)
        def _():
            dw_ref[...] = jnp.zeros_like(dw_ref)

        dw_ref[0:1, :] += jnp.sum(g * z2, axis=0, keepdims=True)
        dw_ref[1:2, :] += jnp.sum(g * z1, axis=0, keepdims=True)
        dw_ref[2:3, :] += jnp.sum(g * z, axis=0, keepdims=True)

    def col(cb):
        return pl.BlockSpec((bt, BRANCH), lambda i: (i, cb))

    def prev(cb):
        return pl.BlockSpec((HALO, BRANCH), lambda i: (jnp.maximum(i * hb - 1, 0), cb))

    def nxt(cb):
        return pl.BlockSpec((HALO, BRANCH), lambda i: (jnp.minimum((i + 1) * hb, last_h), cb))

    own = pl.BlockSpec((bt, BRANCH), lambda i: (i, 0))
    w_spec = pl.BlockSpec((8, BRANCH), lambda i: (0, 0))
    act = jax.ShapeDtypeStruct((T, BRANCH), BF16)
    return pl.pallas_call(
        body, name=name, grid=(nb,),
        in_specs=[col(CB_B), col(CB_C), col(CB_U), prev(CB_C), prev(CB_U), nxt(CB_B), own,
                  pl.BlockSpec((HALO, BRANCH), lambda i: (jnp.minimum((i + 1) * hb, last_h), 0)), w_spec],
        out_specs=[own, own, own, w_spec],
        out_shape=[act, act, act, jax.ShapeDtypeStruct((8, BRANCH), F32)],
        compiler_params=_cp(("arbitrary",)),
    )(pm, pm, pm, pm, pm, pm, dy, dy, cw)


def _log_sigmoid(z):
    return jnp.minimum(z, 0.0) - jnp.log(1.0 + jnp.exp(-jnp.abs(z)))


def _fox_gate_fwd(fg, fb, name):
    T = fg.shape[0]
    bt = _pick(T, (256,))

    def body(f_ref, b_ref, c_ref, carry):
        @pl.when(pl.program_id(0) == 0)
        def _():
            carry[...] = jnp.zeros_like(carry)

        xv = _log_sigmoid(f_ref[...] + b_ref[...])
        row = lax.broadcasted_iota(I32, xv.shape, 0)
        s = 1
        while s < bt:
            xv = xv + jnp.where(row >= s, pltpu.roll(xv, s, 0), 0.0)
            s *= 2
        xv = xv + carry[...]
        c_ref[...] = xv
        carry[...] = xv[bt - 1:bt, :]

    blk = pl.BlockSpec((bt, 128), lambda i: (i, 0))
    return pl.pallas_call(
        body, name=name, grid=(T // bt,),
        in_specs=[blk, pl.BlockSpec((1, 128), lambda i: (0, 0))],
        out_specs=blk, out_shape=jax.ShapeDtypeStruct((T, 128), F32),
        scratch_shapes=[pltpu.VMEM((1, 128), F32)],
        compiler_params=_cp(("arbitrary",)),
    )(fg, fb)


def _fox_gate_bwd(dc, fg, fb, name):
    T = fg.shape[0]
    bt = _pick(T, (256,))
    nb = T // bt

    def body(d_ref, f_ref, b_ref, o_ref, db_ref, carry):
        @pl.when(pl.program_id(0) == 0)
        def _():
            carry[...] = jnp.zeros_like(carry)
            db_ref[...] = jnp.zeros_like(db_ref)

        xv = d_ref[...]
        row = lax.broadcasted_iota(I32, xv.shape, 0)
        s = 1
        while s < bt:
            xv = xv + jnp.where(row < bt - s, pltpu.roll(xv, bt - s, 0), 0.0)
            s *= 2
        xv = xv + carry[...]
        carry[...] = xv[0:1, :]
        z = f_ref[...] + b_ref[...]
        dz = xv * (1.0 / (1.0 + jnp.exp(z)))
        o_ref[...] = dz
        db_ref[...] += jnp.sum(dz, axis=0, keepdims=True)

    blk = pl.BlockSpec((bt, 128), lambda i: (nb - 1 - i, 0))
    vec = pl.BlockSpec((1, 128), lambda i: (0, 0))
    return pl.pallas_call(
        body, name=name, grid=(nb,),
        in_specs=[blk, blk, vec], out_specs=[blk, vec],
        out_shape=[jax.ShapeDtypeStruct((T, 128), F32), jax.ShapeDtypeStruct((1, 128), F32)],
        scratch_shapes=[pltpu.VMEM((1, 128), F32)],
        compiler_params=_cp(("arbitrary",)),
    )(dc, fg, fb)


def _lane_lo(shape):
    return lax.broadcasted_iota(I32, shape, 1) < HEAD_DIM


def _put_col(shape, h, col):
    lane = lax.broadcasted_iota(I32, shape, 1)
    return jnp.where(lane == h, col, 0.0)


def _fox_fwd(pm, c_col, c_row, name):
    T = pm.shape[0]
    bq = _pick(T, (512, 256))
    bk = bq
    nq = T // bq

    def body(q_ref, k_ref, v_ref, cq_ref, ck_ref, o_ref, lse_ref, acc, m_s, l_s):
        qi = pl.program_id(0)
        ki = pl.program_id(1)

        @pl.when(ki == 0)
        def _():
            acc[...] = jnp.zeros_like(acc)
            m_s[...] = jnp.full_like(m_s, NEG)
            l_s[...] = jnp.zeros_like(l_s)

        @pl.when(ki <= qi)
        def _():
            row = lax.broadcasted_iota(I32, (bq, bk), 0) + qi * bq
            colv = lax.broadcasted_iota(I32, (bq, bk), 1) + ki * bk
            causal = colv <= row
            klo = _lane_lo((bk, 128))
            qlo = _lane_lo((bq, 128))
            cq = cq_ref[...]
            ck = ck_ref[...]
            for p in range(4):
                sl = slice(128 * p, 128 * p + 128)
                qp = q_ref[:, sl] * ATT_SCALE
                kp = k_ref[:, sl]
                vp = v_ref[:, sl]
                kz = jnp.zeros_like(kp)
                ks = (jnp.where(klo, kp, kz), jnp.where(klo, kz, kp))
                alphas, pvs = [], []
                for j in range(2):
                    h = 2 * p + j
                    s = _dot_nt(qp, ks[j]) + (cq[:, h:h + 1] - ck[h:h + 1, :])
                    s = jnp.where(causal, s, NEG)
                    m_old = m_s[h][:, 0:1]
                    m_new = jnp.maximum(m_old, jnp.max(s, axis=-1, keepdims=True))
                    alpha = jnp.exp(m_old - m_new)
                    pe = jnp.exp(s - m_new)
                    l_new = alpha * l_s[h][:, 0:1] + jnp.sum(pe, axis=-1, keepdims=True)
                    m_s[h] = jnp.broadcast_to(m_new, (bq, 128))
                    l_s[h] = jnp.broadcast_to(l_new, (bq, 128))
                    alphas.append(alpha)
                    pvs.append(_dot_nn(pe.astype(BF16), vp))
                a = jnp.where(qlo, alphas[0], alphas[1])
                acc[:, sl] = a * acc[:, sl] + jnp.where(qlo, pvs[0], pvs[1])

        @pl.when(ki == nq - 1)
        def _():
            qlo = _lane_lo((bq, 128))
            lse = jnp.zeros((bq, 128), F32)
            for p in range(4):
                sl = slice(128 * p, 128 * p + 128)
                l0 = l_s[2 * p][:, 0:1]
                l1 = l_s[2 * p + 1][:, 0:1]
                o_ref[:, sl] = (acc[:, sl] / jnp.where(qlo, l0, l1)).astype(BF16)
                lse = lse + _put_col((bq, 128), 2 * p, m_s[2 * p][:, 0:1] + jnp.log(l0))
                lse = lse + _put_col((bq, 128), 2 * p + 1, m_s[2 * p + 1][:, 0:1] + jnp.log(l1))
            lse_ref[...] = lse

    return pl.pallas_call(
        body, name=name, grid=(nq, nq),
        in_specs=[pl.BlockSpec((bq, BRANCH), lambda i, k: (i, CB_FQ)),
                  pl.BlockSpec((bk, BRANCH), lambda i, k: (jnp.minimum(k, i), CB_FK)),
                  pl.BlockSpec((bk, BRANCH), lambda i, k: (jnp.minimum(k, i), CB_FV)),
                  pl.BlockSpec((bq, 128), lambda i, k: (i, 0)),
                  pl.BlockSpec((8, bk), lambda i, k: (0, jnp.minimum(k, i)))],
        out_specs=[pl.BlockSpec((bq, BRANCH), lambda i, k: (i, 0)),
                   pl.BlockSpec((bq, 128), lambda i, k: (i, 0))],
        out_shape=[jax.ShapeDtypeStruct((T, BRANCH), BF16), jax.ShapeDtypeStruct((T, 128), F32)],
        scratch_shapes=[pltpu.VMEM((bq, BRANCH), F32), pltpu.VMEM((8, bq, 128), F32),
                        pltpu.VMEM((8, bq, 128), F32)],
        compiler_params=_cp(("parallel", "arbitrary")),
    )(pm, pm, pm, c_col, c_row)


def _fox_delta(o, do, name):
    T = o.shape[0]
    bt = _pick(T, (512, 256))

    def body(o_ref, d_ref, out_ref):
        prod = o_ref[...].astype(F32) * d_ref[...].astype(F32)
        out = jnp.zeros((bt, 128), F32)
        for h in range(8):
            out = out + _put_col((bt, 128), h, jnp.sum(prod[:, 64 * h:64 * h + 64], axis=-1, keepdims=True))
        out_ref[...] = out

    blk = pl.BlockSpec((bt, BRANCH), lambda i: (i, 0))
    return pl.pallas_call(
        body, name=name, grid=(T // bt,), in_specs=[blk, blk],
        out_specs=pl.BlockSpec((bt, 128), lambda i: (i, 0)),
        out_shape=jax.ShapeDtypeStruct((T, 128), F32),
        compiler_params=_cp(("parallel",)),
    )(o, do)


def _fox_bwd_dq(pm, do, c_col, c_row, lse, delta, name):
    T = pm.shape[0]
    bq = _pick(T, (512, 256))
    bk = bq
    nq = T // bq

    def body(q_ref, k_ref, v_ref, do_ref, cq_ref, ck_ref, lse_ref, dl_ref, dq_ref, dl2_ref, acc, esum):
        qi = pl.program_id(0)
        ki = pl.program_id(1)

        @pl.when(ki == 0)
        def _():
            acc[...] = jnp.zeros_like(acc)
            esum[...] = jnp.zeros_like(esum)

        @pl.when(ki <= qi)
        def _():
            row = lax.broadcasted_iota(I32, (bq, bk), 0) + qi * bq
            colv = lax.broadcasted_iota(I32, (bq, bk), 1) + ki * bk
            causal = colv <= row
            klo = _lane_lo((bk, 128))
            qlo = _lane_lo((bq, 128))
            cq = cq_ref[...]
            ck = ck_ref[...]
            lse_v = lse_ref[...]
            dl_v = dl_ref[...]
            es = jnp.zeros((bq, 128), F32)
            for p in range(4):
                sl = slice(128 * p, 128 * p + 128)
                qp = q_ref[:, sl] * ATT_SCALE
                kp = k_ref[:, sl]
                vp = v_ref[:, sl]
                dop = do_ref[:, sl]
                kz = jnp.zeros_like(kp)
                ks = (jnp.where(klo, kp, kz), jnp.where(klo, kz, kp))
                vs = (jnp.where(klo, vp, kz), jnp.where(klo, kz, vp))
                dqs = []
                for j in range(2):
                    h = 2 * p + j
                    s = _dot_nt(qp, ks[j]) + (cq[:, h:h + 1] - ck[h:h + 1, :])
                    s = jnp.where(causal, s, NEG)
                    pr = jnp.exp(s - lse_v[:, h:h + 1])
                    dp = _dot_nt(dop, vs[j])
                    ds = pr * (dp - dl_v[:, h:h + 1])
                    es = es + _put_col((bq, 128), h, jnp.sum(ds, axis=-1, keepdims=True))
                    dqs.append(_dot_nn(ds.astype(BF16), kp))
                acc[:, sl] += jnp.where(qlo, dqs[0], dqs[1])
            esum[...] += es

        @pl.when(ki == nq - 1)
        def _():
            dq_ref[...] = (acc[...] * ATT_SCALE).astype(BF16)
            dl2_ref[...] = dl_ref[...] + esum[...]

    qb = pl.BlockSpec((bq, 128), lambda i, k: (i, 0))
    return pl.pallas_call(
        body, name=name, grid=(nq, nq),
        in_specs=[pl.BlockSpec((bq, BRANCH), lambda i, k: (i, CB_FQ)),
                  pl.BlockSpec((bk, BRANCH), lambda i, k: (jnp.minimum(k, i), CB_FK)),
                  pl.BlockSpec((bk, BRANCH), lambda i, k: (jnp.minimum(k, i), CB_FV)),
                  pl.BlockSpec((bq, BRANCH), lambda i, k: (i, 0)),
                  qb, pl.BlockSpec((8, bk), lambda i, k: (0, jnp.minimum(k, i))), qb, qb],
        out_specs=[pl.BlockSpec((bq, BRANCH), lambda i, k: (i, 0)), qb],
        out_shape=[jax.ShapeDtypeStruct((T, BRANCH), BF16), jax.ShapeDtypeStruct((T, 128), F32)],
        scratch_shapes=[pltpu.VMEM((bq, BRANCH), F32), pltpu.VMEM((bq, 128), F32)],
        compiler_params=_cp(("parallel", "arbitrary")),
    )(pm, pm, pm, do, c_col, c_row, lse, delta)


def _fox_bwd_dkv(pm, do, c_col, c_row, lse_row, delta_row, name):
    T = pm.shape[0]
    bk = _pick(T, (512, 256))
    bq = bk
    nk = T // bk

    def body(q_ref, k_ref, v_ref, do_ref, cq_ref, ck_ref, lse_ref, dl_ref,
             dk_ref, dv_ref, dc_ref, dk_acc, dv_acc, dc_acc):
        ki = pl.program_id(0)
        qi = pl.program_id(1)

        @pl.when(qi == 0)
        def _():
            dk_acc[...] = jnp.zeros_like(dk_acc)
            dv_acc[...] = jnp.zeros_like(dv_acc)
            dc_acc[...] = jnp.zeros_like(dc_acc)

        @pl.when(qi >= ki)
        def _():
            krow = lax.broadcasted_iota(I32, (bk, bq), 0) + ki * bk
            qcol = lax.broadcasted_iota(I32, (bk, bq), 1) + qi * bq
            causal = krow <= qcol
            qlo = _lane_lo((bq, 128))
            klo = _lane_lo((bk, 128))
            cq = cq_ref[...]
            ck = ck_ref[...]
            lse_v = lse_ref[...]
            dl_v = dl_ref[...]
            dcs = jnp.zeros((bk, 128), F32)
            for p in range(4):
                sl = slice(128 * p, 128 * p + 128)
                qp = q_ref[:, sl]
                kp = k_ref[:, sl] * ATT_SCALE
                vp = v_ref[:, sl]
                dop = do_ref[:, sl]
                qz = jnp.zeros_like(qp)
                qs = (jnp.where(qlo, qp, qz), jnp.where(qlo, qz, qp))
                dos = (jnp.where(qlo, dop, qz), jnp.where(qlo, qz, dop))
                dks, dvs = [], []
                for j in range(2):
                    h = 2 * p + j
                    st = _dot_nt(kp, qs[j]) + (cq[h:h + 1, :] - ck[:, h:h + 1])
                    st = jnp.where(causal, st, NEG)
                    pt = jnp.exp(st - lse_v[h:h + 1, :])
                    dvs.append(_dot_nn(pt.astype(BF16), dop))
                    dpt = _dot_nt(vp, dos[j])
                    dst = pt * (dpt - dl_v[h:h + 1, :])
                    dks.append(_dot_nn(dst.astype(BF16), qp))
                    dcs = dcs - _put_col((bk, 128), h, jnp.sum(dst, axis=-1, keepdims=True))
                dk_acc[:, sl] += jnp.where(klo, dks[0], dks[1])
                dv_acc[:, sl] += jnp.where(klo, dvs[0], dvs[1])
            dc_acc[...] += dcs

        @pl.when(qi == nk - 1)
        def _():
            dk_ref[...] = (dk_acc[...] * ATT_SCALE).astype(BF16)
            dv_ref[...] = dv_acc[...].astype(BF16)
            dc_ref[...] = dc_acc[...]

    qrow = pl.BlockSpec((8, bq), lambda k, i: (0, jnp.maximum(i, k)))
    kb = pl.BlockSpec((bk, BRANCH), lambda k, i: (k, 0))
    return pl.pallas_call(
        body, name=name, grid=(nk, nk),
        in_specs=[pl.BlockSpec((bq, BRANCH), lambda k, i: (jnp.maximum(i, k), CB_FQ)),
                  pl.BlockSpec((bk, BRANCH), lambda k, i: (k, CB_FK)),
                  pl.BlockSpec((bk, BRANCH), lambda k, i: (k, CB_FV)),
                  pl.BlockSpec((bq, BRANCH), lambda k, i: (jnp.maximum(i, k), 0)),
                  qrow, pl.BlockSpec((bk, 128), lambda k, i: (k, 0)), qrow, qrow],
        out_specs=[kb, kb, pl.BlockSpec((bk, 128), lambda k, i: (k, 0))],
        out_shape=[jax.ShapeDtypeStruct((T, BRANCH), BF16), jax.ShapeDtypeStruct((T, BRANCH), BF16),
                   jax.ShapeDtypeStruct((T, 128), F32)],
        scratch_shapes=[pltpu.VMEM((bk, BRANCH), F32), pltpu.VMEM((bk, BRANCH), F32),
                        pltpu.VMEM((bk, 128), F32)],
        compiler_params=_cp(("parallel", "arbitrary")),
    )(pm, pm, pm, do, c_row, c_col, lse_row, delta_row)


def _bucket_table():
    tq = np.arange(WINDOW, dtype=np.int32)[:, None]
    sk = np.arange(2 * WINDOW, dtype=np.int32)[None, :]
    n = np.maximum(WINDOW + tq - sk, 0)
    max_exact = N_BUCKETS // 2
    ratio = np.maximum(n, 1).astype(np.float32) / np.float32(max_exact)
    large = max_exact + (np.log(ratio) / np.float32(math.log(WINDOW / max_exact))
                         * np.float32(N_BUCKETS - max_exact)).astype(np.int32)
    large = np.minimum(large, N_BUCKETS - 1)
    return np.where(n < max_exact, n, large).astype(np.int32)


def _swa_bias(rel_bias, bucket, name):
    def body(rb_ref, bk_ref, o_ref):
        bkt = bk_ref[...]
        for h in range(8):
            def step(b, a):
                return a + jnp.where(bkt == b, rb_ref[b, h], 0.0)
            o_ref[h] = lax.fori_loop(0, N_BUCKETS, step, jnp.zeros(bkt.shape, F32))

    return pl.pallas_call(
        body, name=name,
        in_specs=[pl.BlockSpec(memory_space=pltpu.SMEM), pl.BlockSpec(memory_space=pltpu.VMEM)],
        out_specs=pl.BlockSpec(memory_space=pltpu.VMEM),
        out_shape=jax.ShapeDtypeStruct((8, WINDOW, 2 * WINDOW), F32),
    )(rel_bias, bucket)


def _swa_dbias_reduce(dbias, bucket, name):
    def body(d_ref, bk_ref, o_ref):
        bkt = bk_ref[...]
        rowi = lax.broadcasted_iota(I32, (N_BUCKETS, 128), 0)
        lane = lax.broadcasted_iota(I32, (N_BUCKETS, 128), 1)
        out = jnp.zeros((N_BUCKETS, 128), F32)
        for h in range(8):
            dv = d_ref[h]

            def step(b, a):
                tot = jnp.sum(jnp.where(bkt == b, dv, 0.0), keepdims=True)
                return a + jnp.where((rowi == b) & (lane == h), tot, 0.0)
            out = lax.fori_loop(0, N_BUCKETS, step, out)
        o_ref[...] = out

    return pl.pallas_call(
        body, name=name,
        in_specs=[pl.BlockSpec(memory_space=pltpu.VMEM), pl.BlockSpec(memory_space=pltpu.VMEM)],
        out_specs=pl.BlockSpec(memory_space=pltpu.VMEM),
        out_shape=jax.ShapeDtypeStruct((N_BUCKETS, 128), F32),
    )(dbias, bucket)


def _swap_halves(x):
    return pltpu.roll(x.astype(F32), HEAD_DIM, 1).astype(x.dtype)


def _kv_variants(t):
    lo = _lane_lo(t.shape)
    z = jnp.zeros_like(t)
    a0 = jnp.where(lo, t, z)
    b1 = jnp.where(lo, z, t)
    b0 = _swap_halves(a0)
    a1 = _swap_halves(b1)
    return (a0, a1), (b0, b1), (a0 + b0, a1 + b1)


def _swa_masks(i):
    tq = lax.broadcasted_iota(I32, (WINDOW, WINDOW), 0)
    jj = lax.broadcasted_iota(I32, (WINDOW, WINDOW), 1)
    return (jj > tq) & (i > 0), jj <= tq


def _swa_specs():
    q = pl.BlockSpec((WINDOW, BRANCH), lambda i: (i, CB_SQ))
    kc = pl.BlockSpec((WINDOW, 128), lambda i: (i, CB_SK))
    kp = pl.BlockSpec((WINDOW, 128), lambda i: (jnp.maximum(i - 1, 0), CB_SK))
    vc = pl.BlockSpec((WINDOW, 128), lambda i: (i, CB_SV))
    vp = pl.BlockSpec((WINDOW, 128), lambda i: (jnp.maximum(i - 1, 0), CB_SV))
    bias = pl.BlockSpec((8, WINDOW, 2 * WINDOW), lambda i: (0, 0, 0))
    vec = pl.BlockSpec((1, 128), lambda i: (0, 0))
    return q, kc, kp, vc, vp, bias, vec


def _swa_fwd(pm, bias, sink, name):
    T = pm.shape[0]
    nb = T // WINDOW

    def body(q_ref, kc_ref, kp_ref, vc_ref, vp_ref, b_ref, s_ref, o_ref, m_ref):
        i = pl.program_id(0)
        mprev, mcur = _swa_masks(i)
        kcA, kcB, _ = _kv_variants(kc_ref[...])
        kpA, kpB, _ = _kv_variants(kp_ref[...])
        _, _, vcD = _kv_variants(vc_ref[...])
        _, _, vpD = _kv_variants(vp_ref[...])
        lo = _lane_lo((WINDOW, 128))
        sink_v = s_ref[...]
        mout = jnp.zeros((WINDOW, 128), F32)
        for p in range(4):
            jv = p // 2
            sl = slice(128 * p, 128 * p + 128)
            qp = q_ref[:, sl] * ATT_SCALE
            outs = []
            for par in range(2):
                h = 2 * p + par
                kpx = (kpA, kpB)[par][jv]
                kcx = (kcA, kcB)[par][jv]
                sp = jnp.where(mprev, _dot_nt(qp, kpx) + b_ref[h, :, 0:WINDOW], NEG)
                sc = jnp.where(mcur, _dot_nt(qp, kcx) + b_ref[h, :, WINDOW:2 * WINDOW], NEG)
                sk_h = sink_v[:, h:h + 1]
                m = jnp.maximum(jnp.maximum(jnp.max(sp, axis=-1, keepdims=True),
                                            jnp.max(sc, axis=-1, keepdims=True)), sk_h)
                ep = jnp.exp(sp - m)
                ec = jnp.exp(sc - m)
                den = (jnp.sum(ep, axis=-1, keepdims=True) + jnp.sum(ec, axis=-1, keepdims=True)
                       + jnp.exp(sk_h - m))
                inv = 1.0 / den
                outs.append(_dot_nn((ep * inv).astype(BF16), vpD[jv])
                            + _dot_nn((ec * inv).astype(BF16), vcD[jv]))
                mout = mout + _put_col((WINDOW, 128), h, m + jnp.log(den))
            o_ref[:, sl] = jnp.where(lo, outs[0], outs[1]).astype(BF16)
        m_ref[...] = mout

    q, kc, kp, vc, vp, bs, vec = _swa_specs()
    return pl.pallas_call(
        body, name=name, grid=(nb,),
        in_specs=[q, kc, kp, vc, vp, bs, vec],
        out_specs=[pl.BlockSpec((WINDOW, BRANCH), lambda i: (i, 0)),
                   pl.BlockSpec((WINDOW, 128), lambda i: (i, 0))],
        out_shape=[jax.ShapeDtypeStruct((T, BRANCH), BF16), jax.ShapeDtypeStruct((T, 128), F32)],
        compiler_params=_cp(("parallel",)),
    )(pm, pm, pm, pm, pm, bias, sink)


def _swa_bwd(pm, bias, sink, do, mlse, name):
    T = pm.shape[0]
    nb = T // WINDOW

    def fold(zz):
        return zz + pltpu.roll(zz, HEAD_DIM, 1)

    def body(q_ref, kc_ref, kp_ref, vc_ref, vp_ref, b_ref, s_ref, do_ref, m_ref,
             dq_ref, dkc_ref, dkp_ref, dvc_ref, dvp_ref, db_ref, ds_ref):
        i = pl.program_id(0)

        @pl.when(i == 0)
        def _():
            db_ref[...] = jnp.zeros_like(db_ref)
            ds_ref[...] = jnp.zeros_like(ds_ref)

        mprev, mcur = _swa_masks(i)
        kcA, kcB, kcD = _kv_variants(kc_ref[...])
        kpA, kpB, kpD = _kv_variants(kp_ref[...])
        vcA, vcB, _ = _kv_variants(vc_ref[...])
        vpA, vpB, _ = _kv_variants(vp_ref[...])
        lo = _lane_lo((WINDOW, 128))
        sink_v = s_ref[...]
        mv = m_ref[...]
        zk = jnp.zeros((WINDOW, 128), F32)
        zkp, zkc, zvp, zvc = [zk, zk], [zk, zk], [zk, zk], [zk, zk]
        dsink = jnp.zeros((1, 128), F32)
        for p in range(4):
            jv = p // 2
            sl = slice(128 * p, 128 * p + 128)
            qraw = q_ref[:, sl]
            qp = qraw * ATT_SCALE
            dop = do_ref[:, sl]
            dqs, mkp, mkc, mvp, mvc = [], [], [], [], []
            for par in range(2):
                h = 2 * p + par
                kpx = (kpA, kpB)[par][jv]
                kcx = (kcA, kcB)[par][jv]
                vpx = (vpA, vpB)[par][jv]
                vcx = (vcA, vcB)[par][jv]
                sp = jnp.where(mprev, _dot_nt(qp, kpx) + b_ref[h, :, 0:WINDOW], NEG)
                sc = jnp.where(mcur, _dot_nt(qp, kcx) + b_ref[h, :, WINDOW:2 * WINDOW], NEG)
                m_h = mv[:, h:h + 1]
                pp = jnp.exp(sp - m_h)
                pc = jnp.exp(sc - m_h)
                psink = jnp.exp(sink_v[:, h:h + 1] - m_h)
                dpp = _dot_nt(dop, vpx)
                dpc = _dot_nt(dop, vcx)
                delta = jnp.sum(pp * dpp, axis=-1, keepdims=True) + jnp.sum(pc * dpc, axis=-1, keepdims=True)
                dsp = pp * (dpp - delta)
                dsc = pc * (dpc - delta)
                db_ref[h, :, 0:WINDOW] += dsp
                db_ref[h, :, WINDOW:2 * WINDOW] += dsc
                dsink = dsink - _put_col((1, 128), h, jnp.sum(psink * delta, keepdims=True))
                dsp_b = dsp.astype(BF16)
                dsc_b = dsc.astype(BF16)
                dqs.append(_dot_nn(dsp_b, kpD[jv]) + _dot_nn(dsc_b, kcD[jv]))
                mkp.append(_dot_tn(dsp_b, qraw))
                mkc.append(_dot_tn(dsc_b, qraw))
                mvp.append(_dot_tn(pp.astype(BF16), dop))
                mvc.append(_dot_tn(pc.astype(BF16), dop))
            dq_ref[:, sl] = (jnp.where(lo, dqs[0], dqs[1]) * ATT_SCALE).astype(BF16)
            zkp[jv] = zkp[jv] + jnp.where(lo, mkp[0], mkp[1])
            zkc[jv] = zkc[jv] + jnp.where(lo, mkc[0], mkc[1])
            zvp[jv] = zvp[jv] + jnp.where(lo, mvp[0], mvp[1])
            zvc[jv] = zvc[jv] + jnp.where(lo, mvc[0], mvc[1])
        dkc_ref[...] = jnp.where(lo, fold(zkc[0]), fold(zkc[1])) * ATT_SCALE
        dkp_ref[...] = jnp.where(lo, fold(zkp[0]), fold(zkp[1])) * ATT_SCALE
        dvc_ref[...] = jnp.where(lo, fold(zvc[0]), fold(zvc[1]))
        dvp_ref[...] = jnp.where(lo, fold(zvp[0]), fold(zvp[1]))
        ds_ref[...] += dsink

    q, kc, kp, vc, vp, bs, vec = _swa_specs()
    own = pl.BlockSpec((WINDOW, BRANCH), lambda i: (i, 0))
    sm = pl.BlockSpec((WINDOW, 128), lambda i: (i, 0))
    f128 = jax.ShapeDtypeStruct((T, 128), F32)
    return pl.pallas_call(
        body, name=name, grid=(nb,),
        in_specs=[q, kc, kp, vc, vp, bs, vec, own, sm],
        out_specs=[own, sm, sm, sm, sm, bs, vec],
        out_shape=[jax.ShapeDtypeStruct((T, BRANCH), BF16), f128, f128, f128, f128,
                   jax.ShapeDtypeStruct((8, WINDOW, 2 * WINDOW), F32), jax.ShapeDtypeStruct((1, 128), F32)],
        compiler_params=_cp(("arbitrary",)),
    )(pm, pm, pm, pm, pm, bias, sink, do, mlse)


def _merge_fwd(pm, us, name):
    T = pm.shape[0]
    bt = _pick(T, (512, 256))

    def body(g0, g1, g2, u0, u1, u2, o_ref):
        acc = jax.nn.sigmoid(g0[...].astype(F32)) * u0[...].astype(F32)
        acc = acc + jax.nn.sigmoid(g1[...].astype(F32)) * u1[...].astype(F32)
        acc = acc + jax.nn.sigmoid(g2[...].astype(F32)) * u2[...].astype(F32)
        o_ref[...] = acc.astype(BF16)

    own = pl.BlockSpec((bt, D_MODEL), lambda i: (i, 0))
    gs = [pl.BlockSpec((bt, D_MODEL), lambda i, cb=cb: (i, cb)) for cb in CB_GATE]
    return pl.pallas_call(
        body, name=name, grid=(T // bt,), in_specs=gs + [own, own, own], out_specs=own,
        out_shape=jax.ShapeDtypeStruct((T, D_MODEL), BF16),
        compiler_params=_cp(("parallel",)),
    )(pm, pm, pm, *us)


def _merge_bwd(pm, us, dm, name):
    T = pm.shape[0]
    bt = _pick(T, (256,))

    def body(g0, g1, g2, u0, u1, u2, dm_ref, du0, du1, du2, dg_ref):
        dmv = dm_ref[...].astype(F32)
        for b, (g, u, du) in enumerate(((g0, u0, du0), (g1, u1, du1), (g2, u2, du2))):
            s = jax.nn.sigmoid(g[...].astype(F32))
            du[...] = (dmv * s).astype(BF16)
            dg_ref[:, D_MODEL * b:D_MODEL * (b + 1)] = (dmv * u[...].astype(F32) * s * (1.0 - s)).astype(BF16)

    own = pl.BlockSpec((bt, D_MODEL), lambda i: (i, 0))
    gs = [pl.BlockSpec((bt, D_MODEL), lambda i, cb=cb: (i, cb)) for cb in CB_GATE]
    act = jax.ShapeDtypeStruct((T, D_MODEL), BF16)
    return pl.pallas_call(
        body, name=name, grid=(T // bt,), in_specs=gs + [own, own, own, own],
        out_specs=[own, own, own, pl.BlockSpec((bt, 3 * D_MODEL), lambda i: (i, 0))],
        out_shape=[act, act, act, jax.ShapeDtypeStruct((T, 3 * D_MODEL), BF16)],
        compiler_params=_cp(("parallel",)),
    )(pm, pm, pm, *us, dm)


def _swiglu_fwd(ab, name):
    T = ab.shape[0]
    bt = _pick(T, (512, 256))

    def body(a_ref, b_ref, o_ref):
        a = a_ref[...].astype(F32)
        o_ref[...] = (a * jax.nn.sigmoid(a) * b_ref[...].astype(F32)).astype(BF16)

    return pl.pallas_call(
        body, name=name, grid=(T // bt,),
        in_specs=[pl.BlockSpec((bt, D_FF), lambda i: (i, 0)), pl.BlockSpec((bt, D_FF), lambda i: (i, 1))],
        out_specs=pl.BlockSpec((bt, D_FF), lambda i: (i, 0)),
        out_shape=jax.ShapeDtypeStruct((T, D_FF), BF16),
        compiler_params=_cp(("parallel",)),
    )(ab, ab)


def _swiglu_bwd(ab, dh, name):
    T = ab.shape[0]
    bt = _pick(T, (256,))

    def body(a_ref, b_ref, d_ref, o_ref):
        a = a_ref[...].astype(F32)
        b = b_ref[...].astype(F32)
        d = d_ref[...].astype(F32)
        s = jax.nn.sigmoid(a)
        o_ref[:, 0:D_FF] = (d * b * (s + a * s * (1.0 - s))).astype(BF16)
        o_ref[:, D_FF:2 * D_FF] = (d * a * s).astype(BF16)

    return pl.pallas_call(
        body, name=name, grid=(T // bt,),
        in_specs=[pl.BlockSpec((bt, D_FF), lambda i: (i, 0)), pl.BlockSpec((bt, D_FF), lambda i: (i, 1)),
                  pl.BlockSpec((bt, D_FF), lambda i: (i, 0))],
        out_specs=pl.BlockSpec((bt, 2 * D_FF), lambda i: (i, 0)),
        out_shape=jax.ShapeDtypeStruct((T, 2 * D_FF), BF16),
        compiler_params=_cp(("parallel",)),
    )(ab, ab, dh)


def _xattn_probs(q_ref, kv_ref, h):
    sl = slice(X_HEAD_DIM * h, X_HEAD_DIM * (h + 1))
    qh = q_ref[:, sl]
    kh = kv_ref[:, sl]
    vh = kv_ref[:, D_MODEL + X_HEAD_DIM * h:D_MODEL + X_HEAD_DIM * (h + 1)]
    s = _dot_nt(qh, kh) * X_SCALE
    e = jnp.exp(s - jnp.max(s, axis=-1, keepdims=True))
    return qh, kh, vh, e * (1.0 / jnp.sum(e, axis=-1, keepdims=True))


def _xattn_fwd(q, kv, name):
    T = q.shape[0]
    bq = _pick(T, (512, 256))

    def body(q_ref, kv_ref, o_ref):
        for h in range(X_HEADS):
            _, _, vh, p = _xattn_probs(q_ref, kv_ref, h)
            o_ref[:, X_HEAD_DIM * h:X_HEAD_DIM * (h + 1)] = _dot_nn(p.astype(BF16), vh).astype(BF16)

    own = pl.BlockSpec((bq, D_MODEL), lambda i: (i, 0))
    return pl.pallas_call(
        body, name=name, grid=(T // bq,),
        in_specs=[own, pl.BlockSpec((MEM_LEN, 2 * D_MODEL), lambda i: (0, 0))], out_specs=own,
        out_shape=jax.ShapeDtypeStruct((T, D_MODEL), BF16),
        compiler_params=_cp(("parallel",)),
    )(q, kv)


def _xattn_bwd(q, kv, do, name):
    T = q.shape[0]
    bq = _pick(T, (512, 256))

    def body(q_ref, kv_ref, do_ref, dq_ref, dkv_ref):
        @pl.when(pl.program_id(0) == 0)
        def _():
            dkv_ref[...] = jnp.zeros_like(dkv_ref)

        for h in range(X_HEADS):
            sl = slice(X_HEAD_DIM * h, X_HEAD_DIM * (h + 1))
            qh, kh, vh, p = _xattn_probs(q_ref, kv_ref, h)
            doh = do_ref[:, sl]
            dp = _dot_nt(doh, vh)
            ds = (p * (dp - jnp.sum(p * dp, axis=-1, keepdims=True)) * X_SCALE).astype(BF16)
            dq_ref[:, sl] = _dot_nn(ds, kh).astype(BF16)
            dkv_ref[:, sl] += _dot_tn(ds, qh)
            dkv_ref[:, D_MODEL + X_HEAD_DIM * h:D_MODEL + X_HEAD_DIM * (h + 1)] += _dot_tn(p.astype(BF16), doh)

    own = pl.BlockSpec((bq, D_MODEL), lambda i: (i, 0))
    kvs = pl.BlockSpec((MEM_LEN, 2 * D_MODEL), lambda i: (0, 0))
    return pl.pallas_call(
        body, name=name, grid=(T // bq,), in_specs=[own, kvs, own], out_specs=[own, kvs],
        out_shape=[jax.ShapeDtypeStruct((T, D_MODEL), BF16), jax.ShapeDtypeStruct((MEM_LEN, 2 * D_MODEL), F32)],
        compiler_params=_cp(("arbitrary",)),
    )(q, kv, do)


def _adamw(w, g, m, v, name):
    R, C = w.shape
    cpad = -(-C // 128) * 128
    bt = R
    for cand in (1024, 512, 256, 128, 64, 32, 16, 8):
        if R % cand == 0 and cand * cpad * 4 <= (1 << 20):
            bt = cand
            break

    def body(w_ref, g_ref, m_ref, v_ref, d_ref, nm_ref, nv_ref):
        gv = g_ref[...]
        mn = ADAM_B1 * m_ref[...] + (1.0 - ADAM_B1) * gv
        vn = ADAM_B2 * v_ref[...] + (1.0 - ADAM_B2) * (gv * gv)
        m_hat = mn / (1.0 - ADAM_B1 ** ADAM_STEP)
        v_hat = vn / (1.0 - ADAM_B2 ** ADAM_STEP)
        d_ref[...] = -ADAM_LR * (m_hat / (jnp.sqrt(v_hat) + ADAM_EPS) + ADAM_WD * w_ref[...])
        nm_ref[...] = mn
        nv_ref[...] = vn

    blk = pl.BlockSpec((bt, C), lambda i: (i, 0))
    out = jax.ShapeDtypeStruct((R, C), F32)
    return pl.pallas_call(
        body, name=name, grid=(R // bt,), in_specs=[blk] * 4, out_specs=[blk] * 3,
        out_shape=[out, out, out], compiler_params=_cp(("parallel",)),
    )(w, g, m, v)


ANY = pl.BlockSpec(memory_space=pl.ANY)


def _place():
    x, y, c = lax.axis_index("x"), lax.axis_index("y"), lax.axis_index("c")
    chips = [(1 - x, y), (x, 1 - y), (1 - x, 1 - y)]
    return x, y, c, chips


def _ag_packs(pack):
    R, Wd = pack.shape
    hrows = R // 2

    def body(p_ref, o_ref, send_sems, recv_sems, local_sem):
        x, y, c, chips = _place()
        me = 2 * x + y
        mine = pl.ds(c * hrows, hrows)
        theirs = pl.ds((1 - c) * hrows, hrows)
        local = pltpu.make_async_copy(p_ref, o_ref.at[me], local_sem)
        local.start()

        def copy(k, slab, rows, to, src=None):
            dst = o_ref.at[slab, rows]
            return pltpu.make_async_remote_copy(
                src_ref=dst if src is None else src, dst_ref=dst,
                send_sem=send_sems.at[k], recv_sem=recv_sems.at[k], device_id=to, device_id_type=MESH)

        first = [copy(k, me, mine, (px, py, c), src=p_ref.at[mine]) for k, (px, py) in enumerate(chips)]
        for cp in first:
            cp.start()
        passed = [copy(3 + k, 2 * px + py, mine, (x, y, 1 - c)) for k, (px, py) in enumerate(chips)]
        for k, (px, py) in enumerate(chips):
            copy(k, 2 * px + py, mine, (x, y, c)).wait_recv()
            passed[k].start()
        for k, (px, py) in enumerate(chips):
            copy(3 + k, 2 * px + py, theirs, (x, y, c)).wait_recv()
        for cp in first + passed:
            cp.wait_send()
        local.wait()

    return pl.pallas_call(
        body, name="ag_weights", in_specs=[ANY], out_specs=ANY,
        out_shape=jax.ShapeDtypeStruct((4, R, Wd), pack.dtype),
        scratch_shapes=[pltpu.SemaphoreType.DMA((6,)), pltpu.SemaphoreType.DMA((6,)), pltpu.SemaphoreType.DMA],
    )(pack)


def _rs_sibling(g4):
    _, R, Wd = g4.shape
    hrows = R // 2

    def body(g_ref, o_ref, send_sem, recv_sem):
        x, y, c, _ = _place()
        cp = pltpu.make_async_remote_copy(
            src_ref=g_ref.at[:, pl.ds((1 - c) * hrows, hrows)], dst_ref=o_ref,
            send_sem=send_sem, recv_sem=recv_sem, device_id=(x, y, 1 - c), device_id_type=MESH)
        cp.start()
        cp.wait()

    return pl.pallas_call(
        body, name="rs_sibling", in_specs=[ANY], out_specs=ANY,
        out_shape=jax.ShapeDtypeStruct((4, hrows, Wd), g4.dtype),
        scratch_shapes=[pltpu.SemaphoreType.DMA, pltpu.SemaphoreType.DMA],
    )(g4)


def _rs_add_pair(g4, sib, cidx):
    _, R, Wd = g4.shape
    hrows = R // 2
    bt = _pick(hrows, (240, 120, 16))
    nb = hrows // bt

    def body(c_ref, a_ref, b_ref, o_ref):
        o_ref[...] = (a_ref[...].astype(F32) + b_ref[...].astype(F32)).astype(o_ref.dtype)

    grid_spec = pltpu.PrefetchScalarGridSpec(
        num_scalar_prefetch=1, grid=(4, nb),
        in_specs=[pl.BlockSpec((1, bt, Wd), lambda j, i, c: (j, c[0] * nb + i, 0)),
                  pl.BlockSpec((1, bt, Wd), lambda j, i, c: (j, i, 0))],
        out_specs=pl.BlockSpec((1, bt, Wd), lambda j, i, c: (j, i, 0)))
    return pl.pallas_call(
        body, name="rs_add_pair", grid_spec=grid_spec,
        out_shape=jax.ShapeDtypeStruct((4, hrows, Wd), g4.dtype),
        compiler_params=_cp(("parallel", "parallel")),
    )(cidx, g4, sib)


def _rs_chips(r4):
    _, hrows, Wd = r4.shape

    def body(r_ref, o_ref, send_sems, recv_sems, local_sem):
        x, y, c, chips = _place()
        me = 2 * x + y
        local = pltpu.make_async_copy(r_ref.at[me], o_ref.at[me], local_sem)
        local.start()
        sends = []
        for k, (px, py) in enumerate(chips):
            sends.append(pltpu.make_async_remote_copy(
                src_ref=r_ref.at[2 * px + py], dst_ref=o_ref.at[me],
                send_sem=send_sems.at[k], recv_sem=recv_sems.at[k], device_id=(px, py, c), device_id_type=MESH))
        for cp in sends:
            cp.start()
        for k, (px, py) in enumerate(chips):
            pltpu.make_async_remote_copy(
                src_ref=r_ref.at[me], dst_ref=o_ref.at[2 * px + py],
                send_sem=send_sems.at[k], recv_sem=recv_sems.at[k], device_id=(x, y, c),
                device_id_type=MESH).wait_recv()
        for cp in sends:
            cp.wait_send()
        local.wait()

    return pl.pallas_call(
        body, name="rs_chips", in_specs=[ANY], out_specs=ANY,
        out_shape=jax.ShapeDtypeStruct((4, hrows, Wd), r4.dtype),
        scratch_shapes=[pltpu.SemaphoreType.DMA((3,)), pltpu.SemaphoreType.DMA((3,)), pltpu.SemaphoreType.DMA],
    )(r4)


def _rs_add_chips(q4):
    _, hrows, Wd = q4.shape
    bt = _pick(hrows, (240, 120, 16))

    def body(q_ref, o_ref):
        o_ref[...] = ((q_ref[0].astype(F32) + q_ref[1].astype(F32)) + q_ref[2].astype(F32)) + q_ref[3].astype(F32)

    return pl.pallas_call(
        body, name="rs_add_chips", grid=(hrows // bt,),
        in_specs=[pl.BlockSpec((4, bt, Wd), lambda i: (0, i, 0))],
        out_specs=pl.BlockSpec((bt, Wd), lambda i: (i, 0)),
        out_shape=jax.ShapeDtypeStruct((hrows, Wd), F32),
        compiler_params=_cp(("parallel",)),
    )(q4)


def _rs_share(half):
    hrows, Wd = half.shape

    def body(h_ref, o_ref, send_sem, recv_sem, local_sem):
        x, y, c, _ = _place()
        mine = pl.ds(c * hrows, hrows)
        local = pltpu.make_async_copy(h_ref, o_ref.at[mine], local_sem)
        local.start()
        cp = pltpu.make_async_remote_copy(
            src_ref=h_ref, dst_ref=o_ref.at[mine], send_sem=send_sem, recv_sem=recv_sem,
            device_id=(x, y, 1 - c), device_id_type=MESH)
        cp.start()
        pltpu.make_async_remote_copy(
            src_ref=h_ref, dst_ref=o_ref.at[pl.ds((1 - c) * hrows, hrows)], send_sem=send_sem,
            recv_sem=recv_sem, device_id=(x, y, c), device_id_type=MESH).wait_recv()
        cp.wait_send()
        local.wait()

    return pl.pallas_call(
        body, name="rs_share", in_specs=[ANY], out_specs=ANY,
        out_shape=jax.ShapeDtypeStruct((2 * hrows, Wd), half.dtype),
        scratch_shapes=[pltpu.SemaphoreType.DMA, pltpu.SemaphoreType.DMA, pltpu.SemaphoreType.DMA],
    )(half)


def _allreduce_small(v):
    R, Wd = v.shape

    def body(v_ref, o_ref, buf, send_sems, recv_sems):
        x, y, c, _ = _place()
        me = 4 * x + 2 * y + c
        buf[me] = v_ref[...]
        sends = []
        for k in range(1, 8):
            peer = ((x + (k >> 2)) % 2, (y + ((k >> 1) & 1)) % 2, (c + (k & 1)) % 2)
            sends.append(pltpu.make_async_remote_copy(
                src_ref=v_ref, dst_ref=buf.at[me], send_sem=send_sems.at[k - 1], recv_sem=recv_sems.at[k - 1],
                device_id=peer, device_id_type=MESH))
        for cp in sends:
            cp.start()
        for k in range(1, 8):
            px, py, pc = (x + (k >> 2)) % 2, (y + ((k >> 1) & 1)) % 2, (c + (k & 1)) % 2
            pltpu.make_async_remote_copy(
                src_ref=v_ref, dst_ref=buf.at[4 * px + 2 * py + pc], send_sem=send_sems.at[k - 1],
                recv_sem=recv_sems.at[k - 1], device_id=(x, y, c), device_id_type=MESH).wait_recv()
        acc = buf[0]
        for d in range(1, 8):
            acc = acc + buf[d]
        o_ref[...] = acc
        for cp in sends:
            cp.wait_send()

    vm = pl.BlockSpec(memory_space=pltpu.VMEM)
    return pl.pallas_call(
        body, name="allreduce_small", in_specs=[vm], out_specs=vm,
        out_shape=jax.ShapeDtypeStruct((R, Wd), F32),
        scratch_shapes=[pltpu.VMEM((8, R, Wd), F32), pltpu.SemaphoreType.DMA((7,)), pltpu.SemaphoreType.DMA((7,))],
    )(v)


SHARDED = (
    ("w_in", (2, 1024, 1730), 2),
    ("w_branch", (2, 3, 512, 256), 3),
    ("w_mix_out", (2, 256, 1024), 1),
    ("w_xq", (2, 256, 1024), 1),
    ("w_xkv", (2, 1024, 512), 2),
    ("w_xo", (2, 256, 1024), 1),
    ("w_ffn_gate", (2, 1024, 704), 2),
    ("w_ffn_up", (2, 1024, 704), 2),
    ("w_ffn_down", (2, 704, 1024), 1),
    ("conv_w", (2, 3, 128), 2),
)
PACK_W = 1024
PACK_ELEMS = sum(int(np.prod(s)) for _, s, _ in SHARDED)
PACK_ROWS = -(-PACK_ELEMS // (PACK_W * 32)) * 32


def _pack(parts, dtype):
    flat = jnp.concatenate([p.astype(dtype).reshape(-1) for p in parts]
                           + [jnp.zeros((PACK_ROWS * PACK_W - PACK_ELEMS,), dtype)])
    return flat.reshape(PACK_ROWS, PACK_W)


def _unpack(pack):
    flat = pack.reshape(-1)
    out, off = {}, 0
    for name, shape, _ in SHARDED:
        n = int(np.prod(shape))
        out[name] = flat[off:off + n].reshape(shape)
        off += n
    return out


SMALL = (
    ("mix_norm_g", (2, 1024)), ("xattn_norm_g", (2, 1024)), ("mem_norm_g", (2, 1024)),
    ("ffn_norm_g", (2, 1024)), ("final_norm_g", (1024,)),
    ("forget_bias", (2, 8)), ("sink", (2, 8)), ("rel_bias", (32, 8)),
)
SMALL_ROWS = 112


def _pack_small(vals):
    rows = []
    for name, shape in SMALL:
        v = vals[name].astype(F32)
        if shape[-1] == 1024:
            rows.append(v.reshape(-1, 128))
        else:
            rows.append(jnp.pad(v, ((0, 0), (0, 120))))
    rows = jnp.concatenate(rows, axis=0)
    return jnp.pad(rows, ((0, SMALL_ROWS - rows.shape[0]), (0, 0)))


def _unpack_small(pack):
    out, off = {}, 0
    for name, shape in SMALL:
        if shape[-1] == 1024:
            n = int(np.prod(shape)) // 128
            out[name] = pack[off:off + n].reshape(shape)
        else:
            n = shape[0]
            out[name] = pack[off:off + n, 0:8]
        off += n
    return out


W_IN_PERM = ((3848, 6920), (0, 3072), (3080, 3848), (3072, 3080))


def _perm_w_in(w):
    parts = [w[:, a:b] for a, b in W_IN_PERM]
    return jnp.concatenate(parts + [jnp.zeros((w.shape[0], PROJ_PAD - IN_COLS), w.dtype)], axis=1)


def _unperm_w_in(p):
    return jnp.concatenate([p[:, 3072:6144], p[:, 6912:6920], p[:, 6144:6912], p[:, 0:3072]], axis=1)


def _pad_row8(v):
    return jnp.pad(v.astype(F32).reshape(1, 8), ((0, 0), (0, 120)))


def _local_step(x, mem, tgt, W, rel_bias):
    T = x.shape[0]
    bucket = jnp.asarray(_bucket_table())
    bias = _swa_bias(rel_bias, bucket, "swa_bias")
    saved = []
    for l in range(DEPTH):
        n = "l%d_" % l
        s = {"x0": x}
        wcat = W["w_in_p"][l]
        h = _rms_fwd(x, W["mix_norm_g"][l:l + 1], n + "mix_norm")
        pm = _mm(h, wcat[:, :PROJ_MAIN], "nn", BF16, n + "proj", bn=768)
        fg = _mm(h, wcat[:, PROJ_MAIN:], "nn", F32, n + "proj_fg")
        fb = _pad_row8(W["forget_bias"][l])
        c_col = _fox_gate_fwd(fg, fb, n + "fox_gate")
        c_row = c_col[:, 0:8].T
        cw = jnp.pad(W["conv_w"][l], ((0, 5), (0, 0)))
        y_conv = _conv_fwd(pm, cw, n + "conv")
        y_fox, lse = _fox_fwd(pm, c_col, c_row, n + "fox")
        sink = _pad_row8(W["sink"][l])
        y_swa, mlse = _swa_fwd(pm, bias, sink, n + "swa")
        ys = (y_conv, y_fox, y_swa)
        us = tuple(_mm(ys[b], W["w_branch"][l, b], "nn", BF16, n + "branch%d" % b) for b in range(3))
        merged = _merge_fwd(pm, us, n + "merge")
        x1 = _mm(merged, W["w_mix_out"][l], "nn", F32, n + "mix_out", res=x)
        xn1 = _rms_fwd(x1, W["xattn_norm_g"][l:l + 1], n + "xattn_norm")
        memn = _rms_fwd(mem, W["mem_norm_g"][l:l + 1], n + "mem_norm")
        qx = _mm(xn1, W["w_xq"][l], "nn", BF16, n + "xq")
        kv = _mm(memn, W["w_xkv"][l], "nn", BF16, n + "xkv")
        ox = _xattn_fwd(qx, kv, n + "xattn")
        x2 = _mm(ox, W["w_xo"][l], "nn", F32, n + "xo", res=x1)
        xn2 = _rms_fwd(x2, W["ffn_norm_g"][l:l + 1], n + "ffn_norm")
        ab = _mm(xn2, W["w_gu"][l], "nn", BF16, n + "ffn_in", bn=512)
        hm = _swiglu_fwd(ab, n + "swiglu")
        x3 = _mm(hm, W["w_ffn_down"][l], "nn", F32, n + "ffn_out", res=x2, bk=1408)
        s.update(h=h, pm=pm, fg=fg, fb=fb, c_col=c_col, c_row=c_row, cw=cw, ys=ys, lse=lse, sink=sink,
                 mlse=mlse, us=us, merged=merged, x1=x1, xn1=xn1, memn=memn, qx=qx, kv=kv, ox=ox,
                 x2=x2, xn2=xn2, ab=ab, hm=hm)
        saved.append(s)
        x = x3

    loss_row, dx, dg_final = _final_loss(x, W["final_norm_g"].reshape(1, D_MODEL), tgt, "final_loss")
    G = {name: [None] * DEPTH for name in
         ("mix_norm_g", "w_in_p", "forget_bias", "conv_w", "sink", "w_branch", "w_mix_out", "xattn_norm_g",
          "mem_norm_g", "w_xq", "w_xkv", "w_xo", "ffn_norm_g", "w_gu", "w_ffn_down")}
    dbias_tot = None
    for l in reversed(range(DEPTH)):
        n = "l%d_" % l
        s = saved[l]
        dhm = _mm(dx, W["w_ffn_down"][l], "nt", BF16, n + "d_hm", bn=1408)
        G["w_ffn_down"][l] = _mm(s["hm"], dx, "tn", F32, n + "dw_down", bm=1408, bk=512)
        dab = _swiglu_bwd(s["ab"], dhm, n + "d_swiglu")
        dxn2 = _mm(dab, W["w_gu"][l], "nt", BF16, n + "d_xn2", bk=1408)
        G["w_gu"][l] = _mm(s["xn2"], dab, "tn", F32, n + "dw_gu", bn=512, bk=512)
        dx, G["ffn_norm_g"][l] = _rms_bwd(s["x2"], W["ffn_norm_g"][l:l + 1], dxn2, dx, n + "d_ffn_norm")
        dox = _mm(dx, W["w_xo"][l], "nt", BF16, n + "d_ox")
        G["w_xo"][l] = _mm(s["ox"], dx, "tn", F32, n + "dw_xo", bk=512)
        dqx, dkv = _xattn_bwd(s["qx"], s["kv"], dox, n + "d_xattn")
        dxn1 = _mm(dqx, W["w_xq"][l], "nt", BF16, n + "d_xn1")
        G["w_xq"][l] = _mm(s["xn1"], dqx, "tn", F32, n + "dw_xq", bk=512)
        dmemn = _mm(dkv, W["w_xkv"][l], "nt", BF16, n + "d_memn")
        G["w_xkv"][l] = _mm(s["memn"], dkv, "tn", F32, n + "dw_xkv")
        _, G["mem_norm_g"][l] = _rms_bwd(mem, W["mem_norm_g"][l:l + 1], dmemn, None, n + "d_mem_norm")
        dx, G["xattn_norm_g"][l] = _rms_bwd(s["x1"], W["xattn_norm_g"][l:l + 1], dxn1, dx, n + "d_xattn_norm")
        dmerged = _mm(dx, W["w_mix_out"][l], "nt", BF16, n + "d_merged")
        G["w_mix_out"][l] = _mm(s["merged"], dx, "tn", F32, n + "dw_mix_out", bk=512)
        du0, du1, du2, dgates = _merge_bwd(s["pm"], s["us"], dmerged, n + "d_merge")
        dus = (du0, du1, du2)
        dys = [_mm(dus[b], W["w_branch"][l, b], "nt", BF16, n + "d_y%d" % b) for b in range(3)]
        G["w_branch"][l] = jnp.stack(
            [_mm(s["ys"][b], dus[b], "tn", F32, n + "dw_branch%d" % b, bk=512) for b in range(3)])
        dcb, dcc, dcu, dcw = _conv_bwd(s["pm"], s["cw"], dys[0], n + "d_conv")
        G["conv_w"][l] = dcw[0:3]
        delta = _fox_delta(s["ys"][1], dys[1], n + "fox_delta")
        dfq, delta = _fox_bwd_dq(s["pm"], dys[1], s["c_col"], s["c_row"], s["lse"], delta, n + "d_fox_q")
        dfk, dfv, dc = _fox_bwd_dkv(s["pm"], dys[1], s["c_col"], s["c_row"], s["lse"][:, 0:8].T,
                                    delta[:, 0:8].T, n + "d_fox_kv")
        dfg, dfb = _fox_gate_bwd(dc, s["fg"], s["fb"], n + "d_fox_gate")
        G["forget_bias"][l] = dfb[0, 0:8]
        dsq, dkc, dkp, dvc, dvp, dbias, dsink = _swa_bwd(s["pm"], bias, s["sink"], dys[2], s["mlse"],
                                                        n + "d_swa")
        G["sink"][l] = dsink[0, 0:8]
        dbias_tot = dbias if dbias_tot is None else dbias_tot + dbias
        zpad = jnp.zeros((WINDOW, 128), F32)
        dsk = dkc + jnp.concatenate([dkp[WINDOW:], zpad], axis=0)
        dsv = dvc + jnp.concatenate([dvp[WINDOW:], zpad], axis=0)
        dproj = jnp.concatenate([dgates, dcb, dcc, dcu, dfq, dfk, dfv, dsq, dsk.astype(BF16),
                                 dsv.astype(BF16), dfg.astype(BF16)], axis=1)
        dh = _mm(dproj, W["w_in_p"][l], "nt", BF16, n + "d_h", bk=1408)
        G["w_in_p"][l] = _mm(s["h"], dproj, "tn", F32, n + "dw_in", bn=640, bk=512)
        dx, G["mix_norm_g"][l] = _rms_bwd(s["x0"], W["mix_norm_g"][l:l + 1], dh, dx, n + "d_mix_norm")
    drb = _swa_dbias_reduce(dbias_tot, bucket, "swa_dbias")
    G["rel_bias"] = drb[:, 0:8]
    G["final_norm_g"] = dg_final.reshape(D_MODEL)
    return loss_row, dx, G


def kernel(x, mem, mix_norm_g, w_in, forget_bias, conv_w, sink, w_branch, w_mix_out, rel_bias, xattn_norm_g, mem_norm_g, w_xq, w_xkv, w_xo, ffn_norm_g, w_ffn_gate, w_ffn_up, w_ffn_down, final_norm_g, loss_target, m_mix_norm_g, m_w_in, m_forget_bias, m_conv_w, m_sink, m_w_branch, m_w_mix_out, m_rel_bias, m_xattn_norm_g, m_mem_norm_g, m_w_xq, m_w_xkv, m_w_xo, m_ffn_norm_g, m_w_ffn_gate, m_w_ffn_up, m_w_ffn_down, m_final_norm_g, v_mix_norm_g, v_w_in, v_forget_bias, v_conv_w, v_sink, v_w_branch, v_w_mix_out, v_rel_bias, v_xattn_norm_g, v_mem_norm_g, v_w_xq, v_w_xkv, v_w_xo, v_ffn_norm_g, v_w_ffn_gate, v_w_ffn_up, v_w_ffn_down, v_final_norm_g):
    order = ("mix_norm_g", "w_in", "forget_bias", "conv_w", "sink", "w_branch", "w_mix_out", "rel_bias",
             "xattn_norm_g", "mem_norm_g", "w_xq", "w_xkv", "w_xo", "ffn_norm_g", "w_ffn_gate", "w_ffn_up",
             "w_ffn_down", "final_norm_g")
    w_sh = dict(zip(order, (mix_norm_g, w_in, forget_bias, conv_w, sink, w_branch, w_mix_out, rel_bias,
                            xattn_norm_g, mem_norm_g, w_xq, w_xkv, w_xo, ffn_norm_g, w_ffn_gate, w_ffn_up,
                            w_ffn_down, final_norm_g)))
    m_sh = dict(zip(order, (m_mix_norm_g, m_w_in, m_forget_bias, m_conv_w, m_sink, m_w_branch, m_w_mix_out,
                            m_rel_bias, m_xattn_norm_g, m_mem_norm_g, m_w_xq, m_w_xkv, m_w_xo, m_ffn_norm_g,
                            m_w_ffn_gate, m_w_ffn_up, m_w_ffn_down, m_final_norm_g)))
    v_sh = dict(zip(order, (v_mix_norm_g, v_w_in, v_forget_bias, v_conv_w, v_sink, v_w_branch, v_w_mix_out,
                            v_rel_bias, v_xattn_norm_g, v_mem_norm_g, v_w_xq, v_w_xkv, v_w_xo, v_ffn_norm_g,
                            v_w_ffn_gate, v_w_ffn_up, v_w_ffn_down, v_final_norm_g)))

    gathered = _ag_packs(_pack([w_sh[name] for name, _, _ in SHARDED], BF16))
    per_chip = [_unpack(gathered[j]) for j in range(4)]
    full = {name: jnp.concatenate([per_chip[j][name] for j in range(4)], axis=ax) for name, _, ax in SHARDED}
    W = {k: w_sh[k] for k in ("mix_norm_g", "forget_bias", "sink", "xattn_norm_g", "mem_norm_g",
                              "ffn_norm_g", "final_norm_g")}
    W["w_in_p"] = jnp.stack([_perm_w_in(full["w_in"][l]) for l in range(DEPTH)])
    W["w_gu"] = jnp.concatenate([full["w_ffn_gate"], full["w_ffn_up"]], axis=2)
    W["conv_w"] = full["conv_w"].astype(F32)
    for k in ("w_branch", "w_mix_out", "w_xq", "w_xkv", "w_xo", "w_ffn_down"):
        W[k] = full[k]
    loss_row, dx, G = _local_step(x[0], mem[0], loss_target[0], W, rel_bias)

    gfull = {
        "w_in": jnp.stack([_unperm_w_in(G["w_in_p"][l]) for l in range(DEPTH)]),
        "w_branch": jnp.stack(G["w_branch"]),
        "w_mix_out": jnp.stack(G["w_mix_out"]),
        "w_xq": jnp.stack(G["w_xq"]),
        "w_xkv": jnp.stack(G["w_xkv"]),
        "w_xo": jnp.stack(G["w_xo"]),
        "w_ffn_gate": jnp.stack([G["w_gu"][l][:, :D_FF] for l in range(DEPTH)]),
        "w_ffn_up": jnp.stack([G["w_gu"][l][:, D_FF:] for l in range(DEPTH)]),
        "w_ffn_down": jnp.stack(G["w_ffn_down"]),
        "conv_w": jnp.stack(G["conv_w"]),
    }
    slabs = []
    for j in range(4):
        parts = []
        for name, shape, ax in SHARDED:
            n = shape[ax]
            parts.append(lax.slice_in_dim(gfull[name], j * n, (j + 1) * n, axis=ax))
        slabs.append(_pack(parts, BF16))
    g4 = jnp.stack(slabs)
    cidx = lax.axis_index("c").astype(I32).reshape(1)
    pair = _rs_add_pair(g4, _rs_sibling(g4), cidx)
    half = _rs_add_chips(_rs_chips(pair))
    gsh = _unpack(_rs_share(half))

    small = _unpack_small(_allreduce_small(_pack_small({
        "mix_norm_g": jnp.concatenate(G["mix_norm_g"], axis=0),
        "xattn_norm_g": jnp.concatenate(G["xattn_norm_g"], axis=0),
        "mem_norm_g": jnp.concatenate(G["mem_norm_g"], axis=0),
        "ffn_norm_g": jnp.concatenate(G["ffn_norm_g"], axis=0),
        "final_norm_g": G["final_norm_g"],
        "forget_bias": jnp.stack(G["forget_bias"]),
        "sink": jnp.stack(G["sink"]),
        "rel_bias": G["rel_bias"],
    })))
    grads = dict(gsh)
    grads.update(small)

    sm_names = [name for name, _ in SMALL]
    sd, sm_, sv_ = _adamw(_pack_small({k: w_sh[k] for k in sm_names}), _pack_small({k: grads[k] for k in sm_names}),
                          _pack_small({k: m_sh[k] for k in sm_names}), _pack_small({k: v_sh[k] for k in sm_names}),
                          "adamw_small")
    delta, new_m, new_v = _unpack_small(sd), _unpack_small(sm_), _unpack_small(sv_)
    for name, shape, _ in SHARDED:
        two_d = (-1, shape[-1])
        d, nm, nv = _adamw(w_sh[name].reshape(two_d), grads[name].reshape(two_d), m_sh[name].reshape(two_d),
                           v_sh[name].reshape(two_d), "adamw_" + name)
        delta[name], new_m[name], new_v[name] = d.reshape(shape), nm.reshape(shape), nv.reshape(shape)

    loss = lax.psum(loss_row[0, 0], ("x", "y", "c"))
    return (loss, dx[None], *[grads[k] for k in order], *[delta[k] for k in order],
            *[new_m[k] for k in order], *[new_v[k] for k in order])
```

```python
import math

import numpy as np
import jax
import jax.numpy as jnp
from jax import lax
from jax.experimental import pallas as pl
from jax.experimental.pallas import tpu as pltpu

F32 = jnp.float32
BF16 = jnp.bfloat16
I32 = jnp.int32

D_MODEL = 1024
DEPTH = 2
HEAD_DIM = 64
BRANCH = 512
N_BUCKETS = 32
WINDOW = 128
MEM_LEN = 256
X_HEADS = 4
X_HEAD_DIM = 256
D_FF = 2816
IN_COLS = 6920
PROJ_MAIN = 6912
PROJ_PAD = 7040
RMS_EPS = 1e-6
NEG = -1e30
ATT_SCALE = 0.125
X_SCALE = 0.0625

ADAM_LR = 0.001
ADAM_B1 = 0.9
ADAM_B2 = 0.999
ADAM_EPS = 1e-08
ADAM_WD = 0.01
ADAM_STEP = 10

VMEM_LIMIT = 48 * 1024 * 1024
MESH = pl.DeviceIdType.MESH

CB_GATE = (0, 1, 2)
CB_B, CB_C, CB_U, CB_FQ, CB_FK, CB_FV, CB_SQ = 6, 7, 8, 9, 10, 11, 12
CB_SK, CB_SV = 52, 53


def _cp(sem):
    return pltpu.CompilerParams(dimension_semantics=sem, vmem_limit_bytes=VMEM_LIMIT)


def _pick(n, prefs):
    for p in prefs:
        if p <= n and n % p == 0:
            return p
    return n


def _dot(a, b, dims):
    return lax.dot_general(a, b, (dims, ((), ())), preferred_element_type=F32)


def _dot_nn(a, b):
    return _dot(a, b, ((1,), (0,)))


def _dot_nt(a, b):
    return _dot(a, b, ((1,), (1,)))


def _dot_tn(a, b):
    return _dot(a, b, ((0,), (0,)))


def _mm(a, b, mode, out_dtype, name, res=None, bm=1024, bn=1024, bk=1024):
    if mode == "nn":
        (M, K), (K2, N) = a.shape, b.shape
    elif mode == "nt":
        (M, K), (N, K2) = a.shape, b.shape
    else:
        (K, M), (K2, N) = a.shape, b.shape
    assert K == K2, (name, a.shape, b.shape)
    bm = _pick(M, (bm, 1024, 512, 256, 128))
    bn = _pick(N, (bn, 1024, 768, 640, 512, 384, 256, 128))
    bk = _pick(K, (bk, 1024, 768, 640, 512, 384, 256, 128))
    nk = K // bk
    if mode == "tn":
        a_spec = pl.BlockSpec((bk, bm), lambda i, j, k: (k, i))
    else:
        a_spec = pl.BlockSpec((bm, bk), lambda i, j, k: (i, k))
    if mode == "nt":
        b_spec = pl.BlockSpec((bn, bk), lambda i, j, k: (j, k))
    else:
        b_spec = pl.BlockSpec((bk, bn), lambda i, j, k: (k, j))
    dims = {"nn": ((1,), (0,)), "nt": ((1,), (1,)), "tn": ((0,), (0,))}[mode]
    o_spec = pl.BlockSpec((bm, bn), lambda i, j, k: (i, j))
    has_res = res is not None

    def body(*refs):
        if has_res:
            a_ref, b_ref, r_ref, o_ref = refs[:4]
            scr = refs[4:]
        else:
            a_ref, b_ref, o_ref = refs[:3]
            r_ref = None
            scr = refs[3:]
        p = _dot(a_ref[...].astype(BF16), b_ref[...].astype(BF16), dims)
        if nk == 1:
            if has_res:
                p = p + r_ref[...]
            o_ref[...] = p.astype(out_dtype)
        else:
            acc = scr[0]
            k = pl.program_id(2)

            @pl.when(k == 0)
            def _():
                acc[...] = p

            @pl.when(k > 0)
            def _():
                acc[...] += p

            @pl.when(k == nk - 1)
            def _():
                r = acc[...]
                if has_res:
                    r = r + r_ref[...]
                o_ref[...] = r.astype(out_dtype)

    ins = [a, b] + ([res] if has_res else [])
    in_specs = [a_spec, b_spec] + ([o_spec] if has_res else [])
    return pl.pallas_call(
        body, name=name, grid=(M // bm, N // bn, nk),
        in_specs=in_specs, out_specs=o_spec,
        out_shape=jax.ShapeDtypeStruct((M, N), out_dtype),
        scratch_shapes=[pltpu.VMEM((bm, bn), F32)] if nk > 1 else [],
        compiler_params=_cp(("parallel", "parallel", "arbitrary")),
    )(*ins)


def _rms_fwd(x, g, name):
    T, Dm = x.shape
    bt = _pick(T, (512, 256))

    def body(x_ref, g_ref, o_ref):
        xv = x_ref[...]
        r = lax.rsqrt(jnp.mean(xv * xv, axis=-1, keepdims=True) + RMS_EPS)
        o_ref[...] = ((xv * r) * g_ref[...]).astype(BF16)

    return pl.pallas_call(
        body, name=name, grid=(T // bt,),
        in_specs=[pl.BlockSpec((bt, Dm), lambda i: (i, 0)), pl.BlockSpec((1, Dm), lambda i: (0, 0))],
        out_specs=pl.BlockSpec((bt, Dm), lambda i: (i, 0)),
        out_shape=jax.ShapeDtypeStruct((T, Dm), BF16),
        compiler_params=_cp(("parallel",)),
    )(x, g)


def _rms_bwd(x, g, dh, dres, name):
    T, Dm = x.shape
    bt = _pick(T, (512, 256))
    want_dx = dres is not None

    def body(*refs):
        if want_dx:
            x_ref, g_ref, dh_ref, dr_ref, dx_ref, dg_ref = refs
        else:
            x_ref, g_ref, dh_ref, dg_ref = refs
        xv = x_ref[...]
        r = lax.rsqrt(jnp.mean(xv * xv, axis=-1, keepdims=True) + RMS_EPS)
        xh = xv * r
        dhv = dh_ref[...].astype(F32)

        @pl.when(pl.program_id(0) == 0)
        def _():
            dg_ref[...] = jnp.zeros_like(dg_ref)

        dg_ref[...] += jnp.sum(dhv * xh, axis=0, keepdims=True)
        if want_dx:
            dyg = dhv * g_ref[...]
            dx_ref[...] = dr_ref[...] + r * (dyg - xh * jnp.mean(dyg * xh, axis=-1, keepdims=True))

    row = pl.BlockSpec((bt, Dm), lambda i: (i, 0))
    vec = pl.BlockSpec((1, Dm), lambda i: (0, 0))
    if want_dx:
        return pl.pallas_call(
            body, name=name, grid=(T // bt,),
            in_specs=[row, vec, row, row], out_specs=[row, vec],
            out_shape=[jax.ShapeDtypeStruct((T, Dm), F32), jax.ShapeDtypeStruct((1, Dm), F32)],
            compiler_params=_cp(("arbitrary",)),
        )(x, g, dh, dres)
    return None, pl.pallas_call(
        body, name=name, grid=(T // bt,),
        in_specs=[row, vec, row], out_specs=vec,
        out_shape=jax.ShapeDtypeStruct((1, Dm), F32),
        compiler_params=_cp(("arbitrary",)),
    )(x, g, dh)


def _final_loss(x, g, tgt, name):
    T, Dm = x.shape
    bt = _pick(T, (512, 256))

    def body(x_ref, g_ref, t_ref, loss_ref, dx_ref, dg_ref):
        xv = x_ref[...]
        r = lax.rsqrt(jnp.mean(xv * xv, axis=-1, keepdims=True) + RMS_EPS)
        xh = xv * r
        gv = g_ref[...]
        err = xh * gv - t_ref[...]

        @pl.when(pl.program_id(0) == 0)
        def _():
            dg_ref[...] = jnp.zeros_like(dg_ref)
            loss_ref[...] = jnp.zeros_like(loss_ref)

        loss_ref[...] += jnp.sum(err * err) * (0.5 / Dm)
        dy = err * (1.0 / Dm)
        dg_ref[...] += jnp.sum(dy * xh, axis=0, keepdims=True)
        dyg = dy * gv
        dx_ref[...] = r * (dyg - xh * jnp.mean(dyg * xh, axis=-1, keepdims=True))

    row = pl.BlockSpec((bt, Dm), lambda i: (i, 0))
    vec = pl.BlockSpec((1, Dm), lambda i: (0, 0))
    return pl.pallas_call(
        body, name=name, grid=(T // bt,),
        in_specs=[row, vec, row],
        out_specs=[pl.BlockSpec((1, 128), lambda i: (0, 0)), row, vec],
        out_shape=[jax.ShapeDtypeStruct((1, 128), F32), jax.ShapeDtypeStruct((T, Dm), F32),
                   jax.ShapeDtypeStruct((1, Dm), F32)],
        compiler_params=_cp(("arbitrary",)),
    )(x, g, tgt)


HALO = 16


def _shift_down(z, zprev, s):
    rolled = pltpu.roll(z, s, 0)
    hp = pltpu.roll(zprev, s, 0)
    row = lax.broadcasted_iota(I32, hp.shape, 0)
    top = jnp.where(row < s, hp, rolled[:HALO])
    return jnp.concatenate([top, rolled[HALO:]], axis=0)


def _shift_up(z, znext, s):
    n = z.shape[0]
    rolled = pltpu.roll(z, n - s, 0)
    hn = pltpu.roll(znext, HALO - s, 0)
    row = lax.broadcasted_iota(I32, hn.shape, 0)
    bot = jnp.where(row >= HALO - s, hn, rolled[n - HALO:])
    return jnp.concatenate([rolled[:n - HALO], bot], axis=0)


def _conv_fwd(pm, cw, name):
    T = pm.shape[0]
    bt = _pick(T, (512, 256))
    hb = bt // HALO

    def body(b_ref, c_ref, u_ref, cp_ref, up_ref, w_ref, o_ref):
        i = pl.program_id(0)
        z = c_ref[...].astype(F32) * u_ref[...].astype(F32)
        zp = cp_ref[...].astype(F32) * up_ref[...].astype(F32)
        zp = jnp.where(i > 0, zp, 0.0)
        w = w_ref[...]
        y = w[2:3] * z + w[1:2] * _shift_down(z, zp, 1) + w[0:1] * _shift_down(z, zp, 2)
        o_ref[...] = (b_ref[...].astype(F32) * y).astype(BF16)

    def col(cb):
        return pl.BlockSpec((bt, BRANCH), lambda i: (i, cb))

    def prev(cb):
        return pl.BlockSpec((HALO, BRANCH), lambda i: (jnp.maximum(i * hb - 1, 0), cb))

    return pl.pallas_call(
        body, name=name, grid=(T // bt,),
        in_specs=[col(CB_B), col(CB_C), col(CB_U), prev(CB_C), prev(CB_U),
                  pl.BlockSpec((8, BRANCH), lambda i: (0, 0))],
        out_specs=pl.BlockSpec((bt, BRANCH), lambda i: (i, 0)),
        out_shape=jax.ShapeDtypeStruct((T, BRANCH), BF16),
        compiler_params=_cp(("parallel",)),
    )(pm, pm, pm, pm, pm, cw)


def _conv_bwd(pm, cw, dy, name):
    T = pm.shape[0]
    bt = _pick(T, (512, 256))
    hb = bt // HALO
    nb = T // bt
    last_h = T // HALO - 1

    def body(b_ref, c_ref, u_ref, cp_ref, up_ref, bn_ref, dy_ref, dyn_ref, w_ref,
             db_ref, dc_ref, du_ref, dw_ref):
        i = pl.program_id(0)
        cv = c_ref[...].astype(F32)
        uv = u_ref[...].astype(F32)
        bv = b_ref[...].astype(F32)
        z = cv * uv
        zp = jnp.where(i > 0, cp_ref[...].astype(F32) * up_ref[...].astype(F32), 0.0)
        w = w_ref[...]
        z1 = _shift_down(z, zp, 1)
        z2 = _shift_down(z, zp, 2)
        yc = w[2:3] * z + w[1:2] * z1 + w[0:1] * z2
        dyv = dy_ref[...].astype(F32)
        db_ref[...] = (dyv * yc).astype(BF16)
        g = dyv * bv
        gn = jnp.where(i < nb - 1, dyn_ref[...].astype(F32) * bn_ref[...].astype(F32), 0.0)
        dz = w[2:3] * g + w[1:2] * _shift_up(g, gn, 1) + w[0:1] * _shift_up(g, gn, 2)
        dc_ref[...] = (dz * uv).astype(BF16)
        du_ref[...] = (dz * cv).astype(BF16)

        @pl.when(i == 0)
        def _():
            dw_ref[...] = jnp.zeros_like(dw_ref)

        dw_ref[0:1, :] += jnp.sum(g * z2, axis=0, keepdims=True)
        dw_ref[1:2, :] += jnp.sum(g * z1, axis=0, keepdims=True)
        dw_ref[2:3, :] += jnp.sum(g * z, axis=0, keepdims=True)

    def col(cb):
        return pl.BlockSpec((bt, BRANCH), lambda i: (i, cb))

    def prev(cb):
        return pl.BlockSpec((HALO, BRANCH), lambda i: (jnp.maximum(i * hb - 1, 0), cb))

    def nxt(cb):
        return pl.BlockSpec((HALO, BRANCH), lambda i: (jnp.minimum((i + 1) * hb, last_h), cb))

    own = pl.BlockSpec((bt, BRANCH), lambda i: (i, 0))
    w_spec = pl.BlockSpec((8, BRANCH), lambda i: (0, 0))
    act = jax.ShapeDtypeStruct((T, BRANCH), BF16)
    return pl.pallas_call(
        body, name=name, grid=(nb,),
        in_specs=[col(CB_B), col(CB_C), col(CB_U), prev(CB_C), prev(CB_U), nxt(CB_B), own,
                  pl.BlockSpec((HALO, BRANCH), lambda i: (jnp.minimum((i + 1) * hb, last_h), 0)), w_spec],
        out_specs=[own, own, own, w_spec],
        out_shape=[act, act, act, jax.ShapeDtypeStruct((8, BRANCH), F32)],
        compiler_params=_cp(("arbitrary",)),
    )(pm, pm, pm, pm, pm, pm, dy, dy, cw)


def _log_sigmoid(z):
    return jnp.minimum(z, 0.0) - jnp.log(1.0 + jnp.exp(-jnp.abs(z)))


def _fox_gate_fwd(fg, fb, name):
    T = fg.shape[0]
    bt = _pick(T, (256,))

    def body(f_ref, b_ref, c_ref, carry):
        @pl.when(pl.program_id(0) == 0)
        def _():
            carry[...] = jnp.zeros_like(carry)

        xv = _log_sigmoid(f_ref[...] + b_ref[...])
        row = lax.broadcasted_iota(I32, xv.shape, 0)
        s = 1
        while s < bt:
            xv = xv + jnp.where(row >= s, pltpu.roll(xv, s, 0), 0.0)
            s *= 2
        xv = xv + carry[...]
        c_ref[...] = xv
        carry[...] = xv[bt - 1:bt, :]

    blk = pl.BlockSpec((bt, 128), lambda i: (i, 0))
    return pl.pallas_call(
        body, name=name, grid=(T // bt,),
        in_specs=[blk, pl.BlockSpec((1, 128), lambda i: (0, 0))],
        out_specs=blk, out_shape=jax.ShapeDtypeStruct((T, 128), F32),
        scratch_shapes=[pltpu.VMEM((1, 128), F32)],
        compiler_params=_cp(("arbitrary",)),
    )(fg, fb)


def _fox_gate_bwd(dc, fg, fb, name):
    T = fg.shape[0]
    bt = _pick(T, (256,))
    nb = T // bt

    def body(d_ref, f_ref, b_ref, o_ref, db_ref, carry):
        @pl.when(pl.program_id(0) == 0)
        def _():
            carry[...] = jnp.zeros_like(carry)
            db_ref[...] = jnp.zeros_like(db_ref)

        xv = d_ref[...]
        row = lax.broadcasted_iota(I32, xv.shape, 0)
        s = 1
        while s < bt:
            xv = xv + jnp.where(row < bt - s, pltpu.roll(xv, bt - s, 0), 0.0)
            s *= 2
        xv = xv + carry[...]
        carry[...] = xv[0:1, :]
        z = f_ref[...] + b_ref[...]
        dz = xv * (1.0 / (1.0 + jnp.exp(z)))
        o_ref[...] = dz
        db_ref[...] += jnp.sum(dz, axis=0, keepdims=True)

    blk = pl.BlockSpec((bt, 128), lambda i: (nb - 1 - i, 0))
    vec = pl.BlockSpec((1, 128), lambda i: (0, 0))
    return pl.pallas_call(
        body, name=name, grid=(nb,),
        in_specs=[blk, blk, vec], out_specs=[blk, vec],
        out_shape=[jax.ShapeDtypeStruct((T, 128), F32), jax.ShapeDtypeStruct((1, 128), F32)],
        scratch_shapes=[pltpu.VMEM((1, 128), F32)],
        compiler_params=_cp(("arbitrary",)),
    )(dc, fg, fb)


def _lane_lo(shape):
    return lax.broadcasted_iota(I32, shape, 1) < HEAD_DIM


def _put_col(shape, h, col):
    lane = lax.broadcasted_iota(I32, shape, 1)
    return jnp.where(lane == h, col, 0.0)


def _fox_fwd(pm, c_col, c_row, name):
    T = pm.shape[0]
    bq = _pick(T, (512, 256))
    bk = bq
    nq = T // bq

    def body(q_ref, k_ref, v_ref, cq_ref, ck_ref, o_ref, lse_ref, acc, m_s, l_s):
        qi = pl.program_id(0)
        ki = pl.program_id(1)

        @pl.when(ki == 0)
        def _():
            acc[...] = jnp.zeros_like(acc)
            m_s[...] = jnp.full_like(m_s, NEG)
            l_s[...] = jnp.zeros_like(l_s)

        @pl.when(ki <= qi)
        def _():
            row = lax.broadcasted_iota(I32, (bq, bk), 0) + qi * bq
            colv = lax.broadcasted_iota(I32, (bq, bk), 1) + ki * bk
            causal = colv <= row
            klo = _lane_lo((bk, 128))
            qlo = _lane_lo((bq, 128))
            cq = cq_ref[...]
            ck = ck_ref[...]
            for p in range(4):
                sl = slice(128 * p, 128 * p + 128)
                qp = q_ref[:, sl] * ATT_SCALE
                kp = k_ref[:, sl]
                vp = v_ref[:, sl]
                kz = jnp.zeros_like(kp)
                ks = (jnp.where(klo, kp, kz), jnp.where(klo, kz, kp))
                alphas, pvs = [], []
                for j in range(2):
                    h = 2 * p + j
                    s = _dot_nt(qp, ks[j]) + (cq[:, h:h + 1] - ck[h:h + 1, :])
                    s = jnp.where(causal, s, NEG)
                    m_old = m_s[h][:, 0:1]
                    m_new = jnp.maximum(m_old, jnp.max(s, axis=-1, keepdims=True))
                    alpha = jnp.exp(m_old - m_new)
                    pe = jnp.exp(s - m_new)
                    l_new = alpha * l_s[h][:, 0:1] + jnp.sum(pe, axis=-1, keepdims=True)
                    m_s[h] = jnp.broadcast_to(m_new, (bq, 128))
                    l_s[h] = jnp.broadcast_to(l_new, (bq, 128))
                    alphas.append(alpha)
                    pvs.append(_dot_nn(pe.astype(BF16), vp))
                a = jnp.where(qlo, alphas[0], alphas[1])
                acc[:, sl] = a * acc[:, sl] + jnp.where(qlo, pvs[0], pvs[1])

        @pl.when(ki == nq - 1)
        def _():
            qlo = _lane_lo((bq, 128))
            lse = jnp.zeros((bq, 128), F32)
            for p in range(4):
                sl = slice(128 * p, 128 * p + 128)
                l0 = l_s[2 * p][:, 0:1]
                l1 = l_s[2 * p + 1][:, 0:1]
                o_ref[:, sl] = (acc[:, sl] / jnp.where(qlo, l0, l1)).astype(BF16)
                lse = lse + _put_col((bq, 128), 2 * p, m_s[2 * p][:, 0:1] + jnp.log(l0))
                lse = lse + _put_col((bq, 128), 2 * p + 1, m_s[2 * p + 1][:, 0:1] + jnp.log(l1))
            lse_ref[...] = lse

    return pl.pallas_call(
        body, name=name, grid=(nq, nq),
        in_specs=[pl.BlockSpec((bq, BRANCH), lambda i, k: (i, CB_FQ)),
                  pl.BlockSpec((bk, BRANCH), lambda i, k: (jnp.minimum(k, i), CB_FK)),
                  pl.BlockSpec((bk, BRANCH), lambda i, k: (jnp.minimum(k, i), CB_FV)),
                  pl.BlockSpec((bq, 128), lambda i, k: (i, 0)),
                  pl.BlockSpec((8, bk), lambda i, k: (0, jnp.minimum(k, i)))],
        out_specs=[pl.BlockSpec((bq, BRANCH), lambda i, k: (i, 0)),
                   pl.BlockSpec((bq, 128), lambda i, k: (i, 0))],
        out_shape=[jax.ShapeDtypeStruct((T, BRANCH), BF16), jax.ShapeDtypeStruct((T, 128), F32)],
        scratch_shapes=[pltpu.VMEM((bq, BRANCH), F32), pltpu.VMEM((8, bq, 128), F32),
                        pltpu.VMEM((8, bq, 128), F32)],
        compiler_params=_cp(("parallel", "arbitrary")),
    )(pm, pm, pm, c_col, c_row)


def _fox_delta(o, do, name):
    T = o.shape[0]
    bt = _pick(T, (512, 256))

    def body(o_ref, d_ref, out_ref):
        prod = o_ref[...].astype(F32) * d_ref[...].astype(F32)
        out = jnp.zeros((bt, 128), F32)
        for h in range(8):
            out = out + _put_col((bt, 128), h, jnp.sum(prod[:, 64 * h:64 * h + 64], axis=-1, keepdims=True))
        out_ref[...] = out

    blk = pl.BlockSpec((bt, BRANCH), lambda i: (i, 0))
    return pl.pallas_call(
        body, name=name, grid=(T // bt,), in_specs=[blk, blk],
        out_specs=pl.BlockSpec((bt, 128), lambda i: (i, 0)),
        out_shape=jax.ShapeDtypeStruct((T, 128), F32),
        compiler_params=_cp(("parallel",)),
    )(o, do)


def _fox_bwd_dq(pm, do, c_col, c_row, lse, delta, name):
    T = pm.shape[0]
    bq = _pick(T, (512, 256))
    bk = bq
    nq = T // bq

    def body(q_ref, k_ref, v_ref, do_ref, cq_ref, ck_ref, lse_ref, dl_ref, dq_ref, dl2_ref, acc, esum):
        qi = pl.program_id(0)
        ki = pl.program_id(1)

        @pl.when(ki == 0)
        def _():
            acc[...] = jnp.zeros_like(acc)
            esum[...] = jnp.zeros_like(esum)

        @pl.when(ki <= qi)
        def _():
            row = lax.broadcasted_iota(I32, (bq, bk), 0) + qi * bq
            colv = lax.broadcasted_iota(I32, (bq, bk), 1) + ki * bk
            causal = colv <= row
            klo = _lane_lo((bk, 128))
            qlo = _lane_lo((bq, 128))
            cq = cq_ref[...]
            ck = ck_ref[...]
            lse_v = lse_ref[...]
            dl_v = dl_ref[...]
            es = jnp.zeros((bq, 128), F32)
            for p in range(4):
                sl = slice(128 * p, 128 * p + 128)
                qp = q_ref[:, sl] * ATT_SCALE
                kp = k_ref[:, sl]
                vp = v_ref[:, sl]
                dop = do_ref[:, sl]
                kz = jnp.zeros_like(kp)
                ks = (jnp.where(klo, kp, kz), jnp.where(klo, kz, kp))
                vs = (jnp.where(klo, vp, kz), jnp.where(klo, kz, vp))
                dqs = []
                for j in range(2):
                    h = 2 * p + j
                    s = _dot_nt(qp, ks[j]) + (cq[:, h:h + 1] - ck[h:h + 1, :])
                    s = jnp.where(causal, s, NEG)
                    pr = jnp.exp(s - lse_v[:, h:h + 1])
                    dp = _dot_nt(dop, vs[j])
                    ds = pr * (dp - dl_v[:, h:h + 1])
                    es = es + _put_col((bq, 128), h, jnp.sum(ds, axis=-1, keepdims=True))
                    dqs.append(_dot_nn(ds.astype(BF16), kp))
                acc[:, sl] += jnp.where(qlo, dqs[0], dqs[1])
            esum[...] += es

        @pl.when(ki == nq - 1)
        def _():
            dq_ref[...] = (acc[...] * ATT_SCALE).astype(BF16)
            dl2_ref[...] = dl_ref[...] + esum[...]

    qb = pl.BlockSpec((bq, 128), lambda i, k: (i, 0))
    return pl.pallas_call(
        body, name=name, grid=(nq, nq),
        in_specs=[pl.BlockSpec((bq, BRANCH), lambda i, k: (i, CB_FQ)),
                  pl.BlockSpec((bk, BRANCH), lambda i, k: (jnp.minimum(k, i), CB_FK)),
                  pl.BlockSpec((bk, BRANCH), lambda i, k: (jnp.minimum(k, i), CB_FV)),
                  pl.BlockSpec((bq, BRANCH), lambda i, k: (i, 0)),
                  qb, pl.BlockSpec((8, bk), lambda i, k: (0, jnp.minimum(k, i))), qb, qb],
        out_specs=[pl.BlockSpec((bq, BRANCH), lambda i, k: (i, 0)), qb],
        out_shape=[jax.ShapeDtypeStruct((T, BRANCH), BF16), jax.ShapeDtypeStruct((T, 128), F32)],
        scratch_shapes=[pltpu.VMEM((bq, BRANCH), F32), pltpu.VMEM((bq, 128), F32)],
        compiler_params=_cp(("parallel", "arbitrary")),
    )(pm, pm, pm, do, c_col, c_row, lse, delta)


def _fox_bwd_dkv(pm, do, c_col, c_row, lse_row, delta_row, name):
    T = pm.shape[0]
    bk = _pick(T, (512, 256))
    bq = bk
    nk = T // bk

    def body(q_ref, k_ref, v_ref, do_ref, cq_ref, ck_ref, lse_ref, dl_ref,
             dk_ref, dv_ref, dc_ref, dk_acc, dv_acc, dc_acc):
        ki = pl.program_id(0)
        qi = pl.program_id(1)

        @pl.when(qi == 0)
        def _():
            dk_acc[...] = jnp.zeros_like(dk_acc)
            dv_acc[...] = jnp.zeros_like(dv_acc)
            dc_acc[...] = jnp.zeros_like(dc_acc)

        @pl.when(qi >= ki)
        def _():
            krow = lax.broadcasted_iota(I32, (bk, bq), 0) + ki * bk
            qcol = lax.broadcasted_iota(I32, (bk, bq), 1) + qi * bq
            causal = krow <= qcol
            qlo = _lane_lo((bq, 128))
            klo = _lane_lo((bk, 128))
            cq = cq_ref[...]
            ck = ck_ref[...]
            lse_v = lse_ref[...]
            dl_v = dl_ref[...]
            dcs = jnp.zeros((bk, 128), F32)
            for p in range(4):
                sl = slice(128 * p, 128 * p + 128)
                qp = q_ref[:, sl]
                kp = k_ref[:, sl] * ATT_SCALE
                vp = v_ref[:, sl]
                dop = do_ref[:, sl]
                qz = jnp.zeros_like(qp)
                qs = (jnp.where(qlo, qp, qz), jnp.where(qlo, qz, qp))
                dos = (jnp.where(qlo, dop, qz), jnp.where(qlo, qz, dop))
                dks, dvs = [], []
                for j in range(2):
                    h = 2 * p + j
                    st = _dot_nt(kp, qs[j]) + (cq[h:h + 1, :] - ck[:, h:h + 1])
                    st = jnp.where(causal, st, NEG)
                    pt = jnp.exp(st - lse_v[h:h + 1, :])
                    dvs.append(_dot_nn(pt.astype(BF16), dop))
                    dpt = _dot_nt(vp, dos[j])
                    dst = pt * (dpt - dl_v[h:h + 1, :])
                    dks.append(_dot_nn(dst.astype(BF16), qp))
                    dcs = dcs - _put_col((bk, 128), h, jnp.sum(dst, axis=-1, keepdims=True))
                dk_acc[:, sl] += jnp.where(klo, dks[0], dks[1])
                dv_acc[:, sl] += jnp.where(klo, dvs[0], dvs[1])
            dc_acc[...] += dcs

        @pl.when(qi == nk - 1)
        def _():
            dk_ref[...] = (dk_acc[...] * ATT_SCALE).astype(BF16)
            dv_ref[...] = dv_acc[...].astype(BF16)
            dc_ref[...] = dc_acc[...]

    qrow = pl.BlockSpec((8, bq), lambda k, i: (0, jnp.maximum(i, k)))
    kb = pl.BlockSpec((bk, BRANCH), lambda k, i: (k, 0))
    return pl.pallas_call(
        body, name=name, grid=(nk, nk),
        in_specs=[pl.BlockSpec((bq, BRANCH), lambda k, i: (jnp.maximum(i, k), CB_FQ)),
                  pl.BlockSpec((bk, BRANCH), lambda k, i: (k, CB_FK)),
                  pl.BlockSpec((bk, BRANCH), lambda k, i: (k, CB_FV)),
                  pl.BlockSpec((bq, BRANCH), lambda k, i: (jnp.maximum(i, k), 0)),
                  qrow, pl.BlockSpec((bk, 128), lambda k, i: (k, 0)), qrow, qrow],
        out_specs=[kb, kb, pl.BlockSpec((bk, 128), lambda k, i: (k, 0))],
        out_shape=[jax.ShapeDtypeStruct((T, BRANCH), BF16), jax.ShapeDtypeStruct((T, BRANCH), BF16),
                   jax.ShapeDtypeStruct((T, 128), F32)],
        scratch_shapes=[pltpu.VMEM((bk, BRANCH), F32), pltpu.VMEM((bk, BRANCH), F32),
                        pltpu.VMEM((bk, 128), F32)],
        compiler_params=_cp(("parallel", "arbitrary")),
    )(pm, pm, pm, do, c_row, c_col, lse_row, delta_row)


FOX_ROWS = 32


FOX_UNROLL = 16


def _row_start(r, rows):
    return r * rows if isinstance(r, int) else pl.multiple_of(r * rows, rows)


def _chunk_loop(n, chunk):
    if n <= FOX_UNROLL:
        for u in range(n):
            chunk(u, 0)
        return

    def outer(i, carry):
        for u in range(FOX_UNROLL):
            chunk(i * FOX_UNROLL + u, carry)
        return carry

    lax.fori_loop(0, n // FOX_UNROLL, outer, 0)


def _tree(op, xs):
    xs = list(xs)
    while len(xs) > 1:
        xs = [op(xs[i], xs[i + 1]) if i + 1 < len(xs) else xs[i] for i in range(0, len(xs), 2)]
    return xs[0]


def _masked_halves(t):
    lo = _lane_lo(t.shape)
    z = jnp.zeros_like(t)
    return jnp.where(lo, t, z), jnp.where(lo, z, t)


def _fox2_fwd(pm, c_row, name):
    T = pm.shape[0]
    bq = _pick(T, (512, 256))
    bk = bq
    nq = T // bq
    R = FOX_ROWS
    ng = bk // 128

    def body(q_ref, k_ref, v_ref, ck_ref, o_ref, lse_ref, acc, m_s, l_s, a_s, s_scr, p_scr):
        qi = pl.program_id(0)
        ki = pl.program_id(1)

        @pl.when(ki == 0)
        def _():
            acc[...] = jnp.zeros_like(acc)
            m_s[...] = jnp.full_like(m_s, NEG)
            l_s[...] = jnp.zeros_like(l_s)

        def block(masked):
            qlo = _lane_lo((bq, 128))
            for p in range(4):
                sl = slice(128 * p, 128 * p + 128)
                qp = q_ref[:, sl] * ATT_SCALE
                vp = v_ref[:, sl]
                ks = _masked_halves(k_ref[:, sl])
                pvs = []
                for j in range(2):
                    h = 2 * p + j
                    s_scr[j] = _dot_nt(qp, ks[j])

                    def chunk(r, carry, h=h, j=j):
                        r0 = _row_start(r, R)
                        rows = pl.ds(r0, R)
                        sc = [s_scr[j, rows, 128 * g:128 * g + 128] - ck_ref[h:h + 1, 128 * g:128 * g + 128]
                              for g in range(ng)]
                        if masked:
                            rid = lax.broadcasted_iota(I32, (R, 128), 0) + r0
                            cid = lax.broadcasted_iota(I32, (R, 128), 1)
                            sc = [jnp.where(cid + 128 * g <= rid, sc[g], NEG) for g in range(ng)]
                        m_old = m_s[h, rows, :]
                        m_new = jnp.maximum(m_old, jnp.max(_tree(jnp.maximum, sc), axis=-1, keepdims=True))
                        alpha = jnp.exp(m_old - m_new)
                        pe = [jnp.exp(sc[g] - m_new) for g in range(ng)]
                        l_s[h, rows, :] = alpha * l_s[h, rows, :] + _tree(jnp.add, pe)
                        m_s[h, rows, :] = m_new
                        a_s[j, rows, :] = alpha
                        for g in range(ng):
                            p_scr[j, rows, 128 * g:128 * g + 128] = pe[g].astype(BF16)
                        return carry

                    _chunk_loop(bq // R, chunk)
                    pvs.append(_dot_nn(p_scr[j], vp))
                acc[:, sl] = jnp.where(qlo, a_s[0], a_s[1]) * acc[:, sl] + jnp.where(qlo, pvs[0], pvs[1])

        @pl.when(ki < qi)
        def _():
            block(False)

        @pl.when(ki == qi)
        def _():
            block(True)

        @pl.when(ki == nq - 1)
        def _():
            qlo = _lane_lo((bq, 128))
            lse = jnp.zeros((bq, 128), F32)
            for p in range(4):
                sl = slice(128 * p, 128 * p + 128)
                l0 = jnp.sum(l_s[2 * p], axis=-1, keepdims=True)
                l1 = jnp.sum(l_s[2 * p + 1], axis=-1, keepdims=True)
                o_ref[:, sl] = (acc[:, sl] / jnp.where(qlo, l0, l1)).astype(BF16)
                lse = lse + _put_col((bq, 128), 2 * p, m_s[2 * p][:, 0:1] + jnp.log(l0))
                lse = lse + _put_col((bq, 128), 2 * p + 1, m_s[2 * p + 1][:, 0:1] + jnp.log(l1))
            lse_ref[...] = lse

    return pl.pallas_call(
        body, name=name, grid=(nq, nq),
        in_specs=[pl.BlockSpec((bq, BRANCH), lambda i, k: (i, CB_FQ)),
                  pl.BlockSpec((bk, BRANCH), lambda i, k: (jnp.minimum(k, i), CB_FK)),
                  pl.BlockSpec((bk, BRANCH), lambda i, k: (jnp.minimum(k, i), CB_FV)),
                  pl.BlockSpec((8, bk), lambda i, k: (0, jnp.minimum(k, i)))],
        out_specs=[pl.BlockSpec((bq, BRANCH), lambda i, k: (i, 0)),
                   pl.BlockSpec((bq, 128), lambda i, k: (i, 0))],
        out_shape=[jax.ShapeDtypeStruct((T, BRANCH), BF16), jax.ShapeDtypeStruct((T, 128), F32)],
        scratch_shapes=[pltpu.VMEM((bq, BRANCH), F32), pltpu.VMEM((8, bq, 128), F32),
                        pltpu.VMEM((8, bq, 128), F32), pltpu.VMEM((2, bq, 128), F32),
                        pltpu.VMEM((2, bq, bk), F32), pltpu.VMEM((2, bq, bk), BF16)],
        compiler_params=_cp(("parallel", "arbitrary")),
    )(pm, pm, pm, c_row)


def _fox2_bwd_dq(pm, do, c_row, lse, delta, name):
    T = pm.shape[0]
    bq = _pick(T, (512, 256))
    bk = bq
    nq = T // bq
    R = FOX_ROWS
    ng = bk // 128

    def body(q_ref, k_ref, v_ref, do_ref, ck_ref, lse_ref, dl_ref, dq_ref, dl2_ref,
             acc, e_s, s_scr, dp_scr, ds_scr):
        qi = pl.program_id(0)
        ki = pl.program_id(1)

        @pl.when(ki == 0)
        def _():
            acc[...] = jnp.zeros_like(acc)
            e_s[...] = jnp.zeros_like(e_s)

        def block(masked):
            qlo = _lane_lo((bq, 128))
            for p in range(4):
                sl = slice(128 * p, 128 * p + 128)
                qp = q_ref[:, sl] * ATT_SCALE
                kp = k_ref[:, sl]
                dop = do_ref[:, sl]
                ks = _masked_halves(kp)
                vs = _masked_halves(v_ref[:, sl])
                dqs = []
                for j in range(2):
                    h = 2 * p + j
                    s_scr[...] = _dot_nt(qp, ks[j])
                    dp_scr[...] = _dot_nt(dop, vs[j])

                    def chunk(r, carry, h=h):
                        r0 = _row_start(r, R)
                        rows = pl.ds(r0, R)
                        lse_c = lse_ref[rows, h:h + 1]
                        dl_c = dl_ref[rows, h:h + 1]
                        if masked:
                            rid = lax.broadcasted_iota(I32, (R, 128), 0) + r0
                            cid = lax.broadcasted_iota(I32, (R, 128), 1)
                        dss = []
                        for g in range(ng):
                            gs = slice(128 * g, 128 * g + 128)
                            sc = s_scr[rows, gs] - ck_ref[h:h + 1, gs]
                            if masked:
                                sc = jnp.where(cid + 128 * g <= rid, sc, NEG)
                            ds = jnp.exp(sc - lse_c) * (dp_scr[rows, gs] - dl_c)
                            ds_scr[rows, gs] = ds.astype(BF16)
                            dss.append(ds)
                        e_s[h, rows, :] += _tree(jnp.add, dss)
                        return carry

                    _chunk_loop(bq // R, chunk)
                    dqs.append(_dot_nn(ds_scr[...], kp))
                acc[:, sl] += jnp.where(qlo, dqs[0], dqs[1])

        @pl.when(ki < qi)
        def _():
            block(False)

        @pl.when(ki == qi)
        def _():
            block(True)

        @pl.when(ki == nq - 1)
        def _():
            dq_ref[...] = (acc[...] * ATT_SCALE).astype(BF16)
            out = dl_ref[...]
            for h in range(8):
                out = out + _put_col((bq, 128), h, jnp.sum(e_s[h], axis=-1, keepdims=True))
            dl2_ref[...] = out

    qb = pl.BlockSpec((bq, 128), lambda i, k: (i, 0))
    return pl.pallas_call(
        body, name=name, grid=(nq, nq),
        in_specs=[pl.BlockSpec((bq, BRANCH), lambda i, k: (i, CB_FQ)),
                  pl.BlockSpec((bk, BRANCH), lambda i, k: (jnp.minimum(k, i), CB_FK)),
                  pl.BlockSpec((bk, BRANCH), lambda i, k: (jnp.minimum(k, i), CB_FV)),
                  pl.BlockSpec((bq, BRANCH), lambda i, k: (i, 0)),
                  pl.BlockSpec((8, bk), lambda i, k: (0, jnp.minimum(k, i))), qb, qb],
        out_specs=[pl.BlockSpec((bq, BRANCH), lambda i, k: (i, 0)), qb],
        out_shape=[jax.ShapeDtypeStruct((T, BRANCH), BF16), jax.ShapeDtypeStruct((T, 128), F32)],
        scratch_shapes=[pltpu.VMEM((bq, BRANCH), F32), pltpu.VMEM((8, bq, 128), F32),
                        pltpu.VMEM((bq, bk), F32), pltpu.VMEM((bq, bk), F32), pltpu.VMEM((bq, bk), BF16)],
        compiler_params=_cp(("parallel", "arbitrary")),
    )(pm, pm, pm, do, c_row, lse, delta)


def _fox2_bwd_dkv(pm, do, c_col, lse_row, delta_row, name):
    T = pm.shape[0]
    bk = _pick(T, (512, 256))
    bq = bk
    nk = T // bk
    R = FOX_ROWS
    ng = bq // 128

    def body(q_ref, k_ref, v_ref, do_ref, ck_ref, lse_ref, dl_ref, dk_ref, dv_ref, dc_ref,
             dk_acc, dv_acc, dc_s, st_scr, dpt_scr, pt_scr, dst_scr):
        ki = pl.program_id(0)
        qi = pl.program_id(1)

        @pl.when(qi == 0)
        def _():
            dk_acc[...] = jnp.zeros_like(dk_acc)
            dv_acc[...] = jnp.zeros_like(dv_acc)
            dc_s[...] = jnp.zeros_like(dc_s)

        def block(masked):
            klo = _lane_lo((bk, 128))
            for p in range(4):
                sl = slice(128 * p, 128 * p + 128)
                qp = q_ref[:, sl]
                kp = k_ref[:, sl] * ATT_SCALE
                vp = v_ref[:, sl]
                dop = do_ref[:, sl]
                qs = _masked_halves(qp)
                dos = _masked_halves(dop)
                dks, dvs = [], []
                for j in range(2):
                    h = 2 * p + j
                    st_scr[...] = _dot_nt(kp, qs[j])
                    dpt_scr[...] = _dot_nt(vp, dos[j])

                    def chunk(r, carry, h=h):
                        r0 = _row_start(r, R)
                        rows = pl.ds(r0, R)
                        ck_c = ck_ref[rows, h:h + 1]
                        if masked:
                            kid = lax.broadcasted_iota(I32, (R, 128), 0) + r0
                            qid = lax.broadcasted_iota(I32, (R, 128), 1)
                        dss = []
                        for g in range(ng):
                            gs = slice(128 * g, 128 * g + 128)
                            st = st_scr[rows, gs] - (ck_c + lse_ref[h:h + 1, gs])
                            if masked:
                                st = jnp.where(kid <= qid + 128 * g, st, NEG)
                            pt = jnp.exp(st)
                            dst = pt * (dpt_scr[rows, gs] - dl_ref[h:h + 1, gs])
                            pt_scr[rows, gs] = pt.astype(BF16)
                            dst_scr[rows, gs] = dst.astype(BF16)
                            dss.append(dst)
                        dc_s[h, rows, :] -= _tree(jnp.add, dss)
                        return carry

                    _chunk_loop(bk // R, chunk)
                    dvs.append(_dot_nn(pt_scr[...], dop))
                    dks.append(_dot_nn(dst_scr[...], qp))
                dk_acc[:, sl] += jnp.where(klo, dks[0], dks[1])
                dv_acc[:, sl] += jnp.where(klo, dvs[0], dvs[1])

        @pl.when(qi > ki)
        def _():
            block(False)

        @pl.when(qi == ki)
        def _():
            block(True)

        @pl.when(qi == nk - 1)
        def _():
            dk_ref[...] = (dk_acc[...] * ATT_SCALE).astype(BF16)
            dv_ref[...] = dv_acc[...].astype(BF16)
            out = jnp.zeros((bk, 128), F32)
            for h in range(8):
                out = out + _put_col((bk, 128), h, jnp.sum(dc_s[h], axis=-1, keepdims=True))
            dc_ref[...] = out

    qrow = pl.BlockSpec((8, bq), lambda k, i: (0, jnp.maximum(i, k)))
    kb = pl.BlockSpec((bk, BRANCH), lambda k, i: (k, 0))
    return pl.pallas_call(
        body, name=name, grid=(nk, nk),
        in_specs=[pl.BlockSpec((bq, BRANCH), lambda k, i: (jnp.maximum(i, k), CB_FQ)),
                  pl.BlockSpec((bk, BRANCH), lambda k, i: (k, CB_FK)),
                  pl.BlockSpec((bk, BRANCH), lambda k, i: (k, CB_FV)),
                  pl.BlockSpec((bq, BRANCH), lambda k, i: (jnp.maximum(i, k), 0)),
                  pl.BlockSpec((bk, 128), lambda k, i: (k, 0)), qrow, qrow],
        out_specs=[kb, kb, pl.BlockSpec((bk, 128), lambda k, i: (k, 0))],
        out_shape=[jax.ShapeDtypeStruct((T, BRANCH), BF16), jax.ShapeDtypeStruct((T, BRANCH), BF16),
                   jax.ShapeDtypeStruct((T, 128), F32)],
        scratch_shapes=[pltpu.VMEM((bk, BRANCH), F32), pltpu.VMEM((bk, BRANCH), F32),
                        pltpu.VMEM((8, bk, 128), F32), pltpu.VMEM((bk, bq), F32), pltpu.VMEM((bk, bq), F32),
                        pltpu.VMEM((bk, bq), BF16), pltpu.VMEM((bk, bq), BF16)],
        compiler_params=_cp(("parallel", "arbitrary")),
    )(pm, pm, pm, do, c_col, lse_row, delta_row)


def _bucket_table():
    tq = np.arange(WINDOW, dtype=np.int32)[:, None]
    sk = np.arange(2 * WINDOW, dtype=np.int32)[None, :]
    n = np.maximum(WINDOW + tq - sk, 0)
    max_exact = N_BUCKETS // 2
    ratio = np.maximum(n, 1).astype(np.float32) / np.float32(max_exact)
    large = max_exact + (np.log(ratio) / np.float32(math.log(WINDOW / max_exact))
                         * np.float32(N_BUCKETS - max_exact)).astype(np.int32)
    large = np.minimum(large, N_BUCKETS - 1)
    return np.where(n < max_exact, n, large).astype(np.int32)


def _swa_bias(rel_bias, bucket, name):
    def body(rb_ref, bk_ref, o_ref):
        bkt = bk_ref[...]
        for h in range(8):
            def step(b, a):
                return a + jnp.where(bkt == b, rb_ref[b, h], 0.0)
            o_ref[h] = lax.fori_loop(0, N_BUCKETS, step, jnp.zeros(bkt.shape, F32))

    return pl.pallas_call(
        body, name=name,
        in_specs=[pl.BlockSpec(memory_space=pltpu.SMEM), pl.BlockSpec(memory_space=pltpu.VMEM)],
        out_specs=pl.BlockSpec(memory_space=pltpu.VMEM),
        out_shape=jax.ShapeDtypeStruct((8, WINDOW, 2 * WINDOW), F32),
    )(rel_bias, bucket)


def _swa_dbias_reduce(dbias, bucket, name):
    def body(d_ref, bk_ref, o_ref):
        bkt = bk_ref[...]
        rowi = lax.broadcasted_iota(I32, (N_BUCKETS, 128), 0)
        lane = lax.broadcasted_iota(I32, (N_BUCKETS, 128), 1)
        out = jnp.zeros((N_BUCKETS, 128), F32)
        for h in range(8):
            dv = d_ref[h]

            def step(b, a):
                tot = jnp.sum(jnp.where(bkt == b, dv, 0.0), keepdims=True)
                return a + jnp.where((rowi == b) & (lane == h), tot, 0.0)
            out = lax.fori_loop(0, N_BUCKETS, step, out)
        o_ref[...] = out

    return pl.pallas_call(
        body, name=name,
        in_specs=[pl.BlockSpec(memory_space=pltpu.VMEM), pl.BlockSpec(memory_space=pltpu.VMEM)],
        out_specs=pl.BlockSpec(memory_space=pltpu.VMEM),
        out_shape=jax.ShapeDtypeStruct((N_BUCKETS, 128), F32),
    )(dbias, bucket)


def _swap_halves(x):
    return pltpu.roll(x.astype(F32), HEAD_DIM, 1).astype(x.dtype)


def _kv_variants(t):
    lo = _lane_lo(t.shape)
    z = jnp.zeros_like(t)
    a0 = jnp.where(lo, t, z)
    b1 = jnp.where(lo, z, t)
    b0 = _swap_halves(a0)
    a1 = _swap_halves(b1)
    return (a0, a1), (b0, b1), (a0 + b0, a1 + b1)


def _swa_masks(i):
    tq = lax.broadcasted_iota(I32, (WINDOW, WINDOW), 0)
    jj = lax.broadcasted_iota(I32, (WINDOW, WINDOW), 1)
    return (jj > tq) & (i > 0), jj <= tq


def _swa_specs():
    q = pl.BlockSpec((WINDOW, BRANCH), lambda i: (i, CB_SQ))
    kc = pl.BlockSpec((WINDOW, 128), lambda i: (i, CB_SK))
    kp = pl.BlockSpec((WINDOW, 128), lambda i: (jnp.maximum(i - 1, 0), CB_SK))
    vc = pl.BlockSpec((WINDOW, 128), lambda i: (i, CB_SV))
    vp = pl.BlockSpec((WINDOW, 128), lambda i: (jnp.maximum(i - 1, 0), CB_SV))
    bias = pl.BlockSpec((8, WINDOW, 2 * WINDOW), lambda i: (0, 0, 0))
    vec = pl.BlockSpec((1, 128), lambda i: (0, 0))
    return q, kc, kp, vc, vp, bias, vec


def _swa_fwd(pm, bias, sink, name):
    T = pm.shape[0]
    nb = T // WINDOW

    def body(q_ref, kc_ref, kp_ref, vc_ref, vp_ref, b_ref, s_ref, o_ref, m_ref):
        i = pl.program_id(0)
        mprev, mcur = _swa_masks(i)
        kcA, kcB, _ = _kv_variants(kc_ref[...])
        kpA, kpB, _ = _kv_variants(kp_ref[...])
        _, _, vcD = _kv_variants(vc_ref[...])
        _, _, vpD = _kv_variants(vp_ref[...])
        lo = _lane_lo((WINDOW, 128))
        sink_v = s_ref[...]
        mout = jnp.zeros((WINDOW, 128), F32)
        for p in range(4):
            jv = p // 2
            sl = slice(128 * p, 128 * p + 128)
            qp = q_ref[:, sl] * ATT_SCALE
            outs = []
            for par in range(2):
                h = 2 * p + par
                kpx = (kpA, kpB)[par][jv]
                kcx = (kcA, kcB)[par][jv]
                sp = jnp.where(mprev, _dot_nt(qp, kpx) + b_ref[h, :, 0:WINDOW], NEG)
                sc = jnp.where(mcur, _dot_nt(qp, kcx) + b_ref[h, :, WINDOW:2 * WINDOW], NEG)
                sk_h = sink_v[:, h:h + 1]
                m = jnp.maximum(jnp.maximum(jnp.max(sp, axis=-1, keepdims=True),
                                            jnp.max(sc, axis=-1, keepdims=True)), sk_h)
                ep = jnp.exp(sp - m)
                ec = jnp.exp(sc - m)
                den = (jnp.sum(ep, axis=-1, keepdims=True) + jnp.sum(ec, axis=-1, keepdims=True)
                       + jnp.exp(sk_h - m))
                inv = 1.0 / den
                outs.append(_dot_nn((ep * inv).astype(BF16), vpD[jv])
                            + _dot_nn((ec * inv).astype(BF16), vcD[jv]))
                mout = mout + _put_col((WINDOW, 128), h, m + jnp.log(den))
            o_ref[:, sl] = jnp.where(lo, outs[0], outs[1]).astype(BF16)
        m_ref[...] = mout

    q, kc, kp, vc, vp, bs, vec = _swa_specs()
    return pl.pallas_call(
        body, name=name, grid=(nb,),
        in_specs=[q, kc, kp, vc, vp, bs, vec],
        out_specs=[pl.BlockSpec((WINDOW, BRANCH), lambda i: (i, 0)),
                   pl.BlockSpec((WINDOW, 128), lambda i: (i, 0))],
        out_shape=[jax.ShapeDtypeStruct((T, BRANCH), BF16), jax.ShapeDtypeStruct((T, 128), F32)],
        compiler_params=_cp(("parallel",)),
    )(pm, pm, pm, pm, pm, bias, sink)


def _swa_bwd(pm, bias, sink, do, mlse, name):
    T = pm.shape[0]
    nb = T // WINDOW

    def fold(zz):
        return zz + pltpu.roll(zz, HEAD_DIM, 1)

    def body(q_ref, kc_ref, kp_ref, vc_ref, vp_ref, b_ref, s_ref, do_ref, m_ref,
             dq_ref, dkc_ref, dkp_ref, dvc_ref, dvp_ref, db_ref, ds_ref):
        i = pl.program_id(0)

        @pl.when(i == 0)
        def _():
            db_ref[...] = jnp.zeros_like(db_ref)
            ds_ref[...] = jnp.zeros_like(ds_ref)

        mprev, mcur = _swa_masks(i)
        kcA, kcB, kcD = _kv_variants(kc_ref[...])
        kpA, kpB, kpD = _kv_variants(kp_ref[...])
        vcA, vcB, _ = _kv_variants(vc_ref[...])
        vpA, vpB, _ = _kv_variants(vp_ref[...])
        lo = _lane_lo((WINDOW, 128))
        sink_v = s_ref[...]
        mv = m_ref[...]
        zk = jnp.zeros((WINDOW, 128), F32)
        zkp, zkc, zvp, zvc = [zk, zk], [zk, zk], [zk, zk], [zk, zk]
        dsink = jnp.zeros((1, 128), F32)
        for p in range(4):
            jv = p // 2
            sl = slice(128 * p, 128 * p + 128)
            qraw = q_ref[:, sl]
            qp = qraw * ATT_SCALE
            dop = do_ref[:, sl]
            dqs, mkp, mkc, mvp, mvc = [], [], [], [], []
            for par in range(2):
                h = 2 * p + par
                kpx = (kpA, kpB)[par][jv]
                kcx = (kcA, kcB)[par][jv]
                vpx = (vpA, vpB)[par][jv]
                vcx = (vcA, vcB)[par][jv]
                sp = jnp.where(mprev, _dot_nt(qp, kpx) + b_ref[h, :, 0:WINDOW], NEG)
                sc = jnp.where(mcur, _dot_nt(qp, kcx) + b_ref[h, :, WINDOW:2 * WINDOW], NEG)
                m_h = mv[:, h:h + 1]
                pp = jnp.exp(sp - m_h)
                pc = jnp.exp(sc - m_h)
                psink = jnp.exp(sink_v[:, h:h + 1] - m_h)
                dpp = _dot_nt(dop, vpx)
                dpc = _dot_nt(dop, vcx)
                delta = jnp.sum(pp * dpp, axis=-1, keepdims=True) + jnp.sum(pc * dpc, axis=-1, keepdims=True)
                dsp = pp * (dpp - delta)
                dsc = pc * (dpc - delta)
                db_ref[h, :, 0:WINDOW] += dsp
                db_ref[h, :, WINDOW:2 * WINDOW] += dsc
                dsink = dsink - _put_col((1, 128), h, jnp.sum(psink * delta, keepdims=True))
                dsp_b = dsp.astype(BF16)
                dsc_b = dsc.astype(BF16)
                dqs.append(_dot_nn(dsp_b, kpD[jv]) + _dot_nn(dsc_b, kcD[jv]))
                mkp.append(_dot_tn(dsp_b, qraw))
                mkc.append(_dot_tn(dsc_b, qraw))
                mvp.append(_dot_tn(pp.astype(BF16), dop))
                mvc.append(_dot_tn(pc.astype(BF16), dop))
            dq_ref[:, sl] = (jnp.where(lo, dqs[0], dqs[1]) * ATT_SCALE).astype(BF16)
            zkp[jv] = zkp[jv] + jnp.where(lo, mkp[0], mkp[1])
            zkc[jv] = zkc[jv] + jnp.where(lo, mkc[0], mkc[1])
            zvp[jv] = zvp[jv] + jnp.where(lo, mvp[0], mvp[1])
            zvc[jv] = zvc[jv] + jnp.where(lo, mvc[0], mvc[1])
        dkc_ref[...] = jnp.where(lo, fold(zkc[0]), fold(zkc[1])) * ATT_SCALE
        dkp_ref[...] = jnp.where(lo, fold(zkp[0]), fold(zkp[1])) * ATT_SCALE
        dvc_ref[...] = jnp.where(lo, fold(zvc[0]), fold(zvc[1]))
        dvp_ref[...] = jnp.where(lo, fold(zvp[0]), fold(zvp[1]))
        ds_ref[...] += dsink

    q, kc, kp, vc, vp, bs, vec = _swa_specs()
    own = pl.BlockSpec((WINDOW, BRANCH), lambda i: (i, 0))
    sm = pl.BlockSpec((WINDOW, 128), lambda i: (i, 0))
    f128 = jax.ShapeDtypeStruct((T, 128), F32)
    return pl.pallas_call(
        body, name=name, grid=(nb,),
        in_specs=[q, kc, kp, vc, vp, bs, vec, own, sm],
        out_specs=[own, sm, sm, sm, sm, bs, vec],
        out_shape=[jax.ShapeDtypeStruct((T, BRANCH), BF16), f128, f128, f128, f128,
                   jax.ShapeDtypeStruct((8, WINDOW, 2 * WINDOW), F32), jax.ShapeDtypeStruct((1, 128), F32)],
        compiler_params=_cp(("arbitrary",)),
    )(pm, pm, pm, pm, pm, bias, sink, do, mlse)


def _merge_fwd(pm, us, name):
    T = pm.shape[0]
    bt = _pick(T, (512, 256))

    def body(g0, g1, g2, u0, u1, u2, o_ref):
        acc = jax.nn.sigmoid(g0[...].astype(F32)) * u0[...].astype(F32)
        acc = acc + jax.nn.sigmoid(g1[...].astype(F32)) * u1[...].astype(F32)
        acc = acc + jax.nn.sigmoid(g2[...].astype(F32)) * u2[...].astype(F32)
        o_ref[...] = acc.astype(BF16)

    own = pl.BlockSpec((bt, D_MODEL), lambda i: (i, 0))
    gs = [pl.BlockSpec((bt, D_MODEL), lambda i, cb=cb: (i, cb)) for cb in CB_GATE]
    return pl.pallas_call(
        body, name=name, grid=(T // bt,), in_specs=gs + [own, own, own], out_specs=own,
        out_shape=jax.ShapeDtypeStruct((T, D_MODEL), BF16),
        compiler_params=_cp(("parallel",)),
    )(pm, pm, pm, *us)


def _merge_bwd(pm, us, dm, name):
    T = pm.shape[0]
    bt = _pick(T, (256,))

    def body(g0, g1, g2, u0, u1, u2, dm_ref, du0, du1, du2, dg_ref):
        dmv = dm_ref[...].astype(F32)
        for b, (g, u, du) in enumerate(((g0, u0, du0), (g1, u1, du1), (g2, u2, du2))):
            s = jax.nn.sigmoid(g[...].astype(F32))
            du[...] = (dmv * s).astype(BF16)
            dg_ref[:, D_MODEL * b:D_MODEL * (b + 1)] = (dmv * u[...].astype(F32) * s * (1.0 - s)).astype(BF16)

    own = pl.BlockSpec((bt, D_MODEL), lambda i: (i, 0))
    gs = [pl.BlockSpec((bt, D_MODEL), lambda i, cb=cb: (i, cb)) for cb in CB_GATE]
    act = jax.ShapeDtypeStruct((T, D_MODEL), BF16)
    return pl.pallas_call(
        body, name=name, grid=(T // bt,), in_specs=gs + [own, own, own, own],
        out_specs=[own, own, own, pl.BlockSpec((bt, 3 * D_MODEL), lambda i: (i, 0))],
        out_shape=[act, act, act, jax.ShapeDtypeStruct((T, 3 * D_MODEL), BF16)],
        compiler_params=_cp(("parallel",)),
    )(pm, pm, pm, *us, dm)


def _swiglu_fwd(ab, name):
    T = ab.shape[0]
    bt = _pick(T, (512, 256))

    def body(a_ref, b_ref, o_ref):
        a = a_ref[...].astype(F32)
        o_ref[...] = (a * jax.nn.sigmoid(a) * b_ref[...].astype(F32)).astype(BF16)

    return pl.pallas_call(
        body, name=name, grid=(T // bt,),
        in_specs=[pl.BlockSpec((bt, D_FF), lambda i: (i, 0)), pl.BlockSpec((bt, D_FF), lambda i: (i, 1))],
        out_specs=pl.BlockSpec((bt, D_FF), lambda i: (i, 0)),
        out_shape=jax.ShapeDtypeStruct((T, D_FF), BF16),
        compiler_params=_cp(("parallel",)),
    )(ab, ab)


def _swiglu_bwd(ab, dh, name):
    T = ab.shape[0]
    bt = _pick(T, (256,))

    def body(a_ref, b_ref, d_ref, o_ref):
        a = a_ref[...].astype(F32)
        b = b_ref[...].astype(F32)
        d = d_ref[...].astype(F32)
        s = jax.nn.sigmoid(a)
        o_ref[:, 0:D_FF] = (d * b * (s + a * s * (1.0 - s))).astype(BF16)
        o_ref[:, D_FF:2 * D_FF] = (d * a * s).astype(BF16)

    return pl.pallas_call(
        body, name=name, grid=(T // bt,),
        in_specs=[pl.BlockSpec((bt, D_FF), lambda i: (i, 0)), pl.BlockSpec((bt, D_FF), lambda i: (i, 1)),
                  pl.BlockSpec((bt, D_FF), lambda i: (i, 0))],
        out_specs=pl.BlockSpec((bt, 2 * D_FF), lambda i: (i, 0)),
        out_shape=jax.ShapeDtypeStruct((T, 2 * D_FF), BF16),
        compiler_params=_cp(("parallel",)),
    )(ab, ab, dh)


def _xattn_probs(q_ref, kv_ref, h):
    sl = slice(X_HEAD_DIM * h, X_HEAD_DIM * (h + 1))
    qh = q_ref[:, sl]
    kh = kv_ref[:, sl]
    vh = kv_ref[:, D_MODEL + X_HEAD_DIM * h:D_MODEL + X_HEAD_DIM * (h + 1)]
    s = _dot_nt(qh, kh) * X_SCALE
    e = jnp.exp(s - jnp.max(s, axis=-1, keepdims=True))
    return qh, kh, vh, e * (1.0 / jnp.sum(e, axis=-1, keepdims=True))


def _xattn_fwd(q, kv, name):
    T = q.shape[0]
    bq = _pick(T, (512, 256))

    def body(q_ref, kv_ref, o_ref):
        for h in range(X_HEADS):
            _, _, vh, p = _xattn_probs(q_ref, kv_ref, h)
            o_ref[:, X_HEAD_DIM * h:X_HEAD_DIM * (h + 1)] = _dot_nn(p.astype(BF16), vh).astype(BF16)

    own = pl.BlockSpec((bq, D_MODEL), lambda i: (i, 0))
    return pl.pallas_call(
        body, name=name, grid=(T // bq,),
        in_specs=[own, pl.BlockSpec((MEM_LEN, 2 * D_MODEL), lambda i: (0, 0))], out_specs=own,
        out_shape=jax.ShapeDtypeStruct((T, D_MODEL), BF16),
        compiler_params=_cp(("parallel",)),
    )(q, kv)


def _xattn_bwd(q, kv, do, name):
    T = q.shape[0]
    bq = _pick(T, (512, 256))

    def body(q_ref, kv_ref, do_ref, dq_ref, dkv_ref):
        @pl.when(pl.program_id(0) == 0)
        def _():
            dkv_ref[...] = jnp.zeros_like(dkv_ref)

        for h in range(X_HEADS):
            sl = slice(X_HEAD_DIM * h, X_HEAD_DIM * (h + 1))
            qh, kh, vh, p = _xattn_probs(q_ref, kv_ref, h)
            doh = do_ref[:, sl]
            dp = _dot_nt(doh, vh)
            ds = (p * (dp - jnp.sum(p * dp, axis=-1, keepdims=True)) * X_SCALE).astype(BF16)
            dq_ref[:, sl] = _dot_nn(ds, kh).astype(BF16)
            dkv_ref[:, sl] += _dot_tn(ds, qh)
            dkv_ref[:, D_MODEL + X_HEAD_DIM * h:D_MODEL + X_HEAD_DIM * (h + 1)] += _dot_tn(p.astype(BF16), doh)

    own = pl.BlockSpec((bq, D_MODEL), lambda i: (i, 0))
    kvs = pl.BlockSpec((MEM_LEN, 2 * D_MODEL), lambda i: (0, 0))
    return pl.pallas_call(
        body, name=name, grid=(T // bq,), in_specs=[own, kvs, own], out_specs=[own, kvs],
        out_shape=[jax.ShapeDtypeStruct((T, D_MODEL), BF16), jax.ShapeDtypeStruct((MEM_LEN, 2 * D_MODEL), F32)],
        compiler_params=_cp(("arbitrary",)),
    )(q, kv, do)


def _adamw(w, g, m, v, name):
    R, C = w.shape
    cpad = -(-C // 128) * 128
    bt = R
    for cand in (1024, 512, 256, 128, 64, 32, 16, 8):
        if R % cand == 0 and cand * cpad * 4 <= (1 << 20):
            bt = cand
            break

    def body(w_ref, g_ref, m_ref, v_ref, d_ref, nm_ref, nv_ref):
        gv = g_ref[...]
        mn = ADAM_B1 * m_ref[...] + (1.0 - ADAM_B1) * gv
        vn = ADAM_B2 * v_ref[...] + (1.0 - ADAM_B2) * (gv * gv)
        m_hat = mn / (1.0 - ADAM_B1 ** ADAM_STEP)
        v_hat = vn / (1.0 - ADAM_B2 ** ADAM_STEP)
        d_ref[...] = -ADAM_LR * (m_hat / (jnp.sqrt(v_hat) + ADAM_EPS) + ADAM_WD * w_ref[...])
        nm_ref[...] = mn
        nv_ref[...] = vn

    blk = pl.BlockSpec((bt, C), lambda i: (i, 0))
    out = jax.ShapeDtypeStruct((R, C), F32)
    return pl.pallas_call(
        body, name=name, grid=(R // bt,), in_specs=[blk] * 4, out_specs=[blk] * 3,
        out_shape=[out, out, out], compiler_params=_cp(("parallel",)),
    )(w, g, m, v)


ANY = pl.BlockSpec(memory_space=pl.ANY)


def _place():
    x, y, c = lax.axis_index("x"), lax.axis_index("y"), lax.axis_index("c")
    chips = [(1 - x, y), (x, 1 - y), (1 - x, 1 - y)]
    return x, y, c, chips


def _ag_packs(pack):
    R, Wd = pack.shape
    hrows = R // 2

    def body(p_ref, o_ref, send_sems, recv_sems, local_sem):
        x, y, c, chips = _place()
        me = 2 * x + y
        mine = pl.ds(c * hrows, hrows)
        theirs = pl.ds((1 - c) * hrows, hrows)
        local = pltpu.make_async_copy(p_ref, o_ref.at[me], local_sem)
        local.start()

        def copy(k, slab, rows, to, src=None):
            dst = o_ref.at[slab, rows]
            return pltpu.make_async_remote_copy(
                src_ref=dst if src is None else src, dst_ref=dst,
                send_sem=send_sems.at[k], recv_sem=recv_sems.at[k], device_id=to, device_id_type=MESH)

        first = [copy(k, me, mine, (px, py, c), src=p_ref.at[mine]) for k, (px, py) in enumerate(chips)]
        for cp in first:
            cp.start()
        passed = [copy(3 + k, 2 * px + py, mine, (x, y, 1 - c)) for k, (px, py) in enumerate(chips)]
        for k, (px, py) in enumerate(chips):
            copy(k, 2 * px + py, mine, (x, y, c)).wait_recv()
            passed[k].start()
        for k, (px, py) in enumerate(chips):
            copy(3 + k, 2 * px + py, theirs, (x, y, c)).wait_recv()
        for cp in first + passed:
            cp.wait_send()
        local.wait()

    return pl.pallas_call(
        body, name="ag_weights", in_specs=[ANY], out_specs=ANY,
        out_shape=jax.ShapeDtypeStruct((4, R, Wd), pack.dtype),
        scratch_shapes=[pltpu.SemaphoreType.DMA((6,)), pltpu.SemaphoreType.DMA((6,)), pltpu.SemaphoreType.DMA],
    )(pack)


def _rs_sibling(g4):
    _, R, Wd = g4.shape
    hrows = R // 2

    def body(g_ref, o_ref, send_sem, recv_sem):
        x, y, c, _ = _place()
        cp = pltpu.make_async_remote_copy(
            src_ref=g_ref.at[:, pl.ds((1 - c) * hrows, hrows)], dst_ref=o_ref,
            send_sem=send_sem, recv_sem=recv_sem, device_id=(x, y, 1 - c), device_id_type=MESH)
        cp.start()
        cp.wait()

    return pl.pallas_call(
        body, name="rs_sibling", in_specs=[ANY], out_specs=ANY,
        out_shape=jax.ShapeDtypeStruct((4, hrows, Wd), g4.dtype),
        scratch_shapes=[pltpu.SemaphoreType.DMA, pltpu.SemaphoreType.DMA],
    )(g4)


def _rs_add_pair(g4, sib, cidx):
    _, R, Wd = g4.shape
    hrows = R // 2
    bt = _pick(hrows, (240, 120, 16))
    nb = hrows // bt

    def body(c_ref, a_ref, b_ref, o_ref):
        o_ref[...] = (a_ref[...].astype(F32) + b_ref[...].astype(F32)).astype(o_ref.dtype)

    grid_spec = pltpu.PrefetchScalarGridSpec(
        num_scalar_prefetch=1, grid=(4, nb),
        in_specs=[pl.BlockSpec((1, bt, Wd), lambda j, i, c: (j, c[0] * nb + i, 0)),
                  pl.BlockSpec((1, bt, Wd), lambda j, i, c: (j, i, 0))],
        out_specs=pl.BlockSpec((1, bt, Wd), lambda j, i, c: (j, i, 0)))
    return pl.pallas_call(
        body, name="rs_add_pair", grid_spec=grid_spec,
        out_shape=jax.ShapeDtypeStruct((4, hrows, Wd), g4.dtype),
        compiler_params=_cp(("parallel", "parallel")),
    )(cidx, g4, sib)


def _rs_chips(r4):
    _, hrows, Wd = r4.shape

    def body(r_ref, o_ref, send_sems, recv_sems, local_sem):
        x, y, c, chips = _place()
        me = 2 * x + y
        local = pltpu.make_async_copy(r_ref.at[me], o_ref.at[me], local_sem)
        local.start()
        sends = []
        for k, (px, py) in enumerate(chips):
            sends.append(pltpu.make_async_remote_copy(
                src_ref=r_ref.at[2 * px + py], dst_ref=o_ref.at[me],
                send_sem=send_sems.at[k], recv_sem=recv_sems.at[k], device_id=(px, py, c), device_id_type=MESH))
        for cp in sends:
            cp.start()
        for k, (px, py) in enumerate(chips):
            pltpu.make_async_remote_copy(
                src_ref=r_ref.at[me], dst_ref=o_ref.at[2 * px + py],
                send_sem=send_sems.at[k], recv_sem=recv_sems.at[k], device_id=(x, y, c),
                device_id_type=MESH).wait_recv()
        for cp in sends:
            cp.wait_send()
        local.wait()

    return pl.pallas_call(
        body, name="rs_chips", in_specs=[ANY], out_specs=ANY,
        out_shape=jax.ShapeDtypeStruct((4, hrows, Wd), r4.dtype),
        scratch_shapes=[pltpu.SemaphoreType.DMA((3,)), pltpu.SemaphoreType.DMA((3,)), pltpu.SemaphoreType.DMA],
    )(r4)


def _rs_add_chips(q4):
    _, hrows, Wd = q4.shape
    bt = _pick(hrows, (240, 120, 16))

    def body(q_ref, o_ref):
        o_ref[...] = ((q_ref[0].astype(F32) + q_ref[1].astype(F32)) + q_ref[2].astype(F32)) + q_ref[3].astype(F32)

    return pl.pallas_call(
        body, name="rs_add_chips", grid=(hrows // bt,),
        in_specs=[pl.BlockSpec((4, bt, Wd), lambda i: (0, i, 0))],
        out_specs=pl.BlockSpec((bt, Wd), lambda i: (i, 0)),
        out_shape=jax.ShapeDtypeStruct((hrows, Wd), F32),
        compiler_params=_cp(("parallel",)),
    )(q4)


def _rs_share(half):
    hrows, Wd = half.shape

    def body(h_ref, o_ref, send_sem, recv_sem, local_sem):
        x, y, c, _ = _place()
        mine = pl.ds(c * hrows, hrows)
        local = pltpu.make_async_copy(h_ref, o_ref.at[mine], local_sem)
        local.start()
        cp = pltpu.make_async_remote_copy(
            src_ref=h_ref, dst_ref=o_ref.at[mine], send_sem=send_sem, recv_sem=recv_sem,
            device_id=(x, y, 1 - c), device_id_type=MESH)
        cp.start()
        pltpu.make_async_remote_copy(
            src_ref=h_ref, dst_ref=o_ref.at[pl.ds((1 - c) * hrows, hrows)], send_sem=send_sem,
            recv_sem=recv_sem, device_id=(x, y, c), device_id_type=MESH).wait_recv()
        cp.wait_send()
        local.wait()

    return pl.pallas_call(
        body, name="rs_share", in_specs=[ANY], out_specs=ANY,
        out_shape=jax.ShapeDtypeStruct((2 * hrows, Wd), half.dtype),
        scratch_shapes=[pltpu.SemaphoreType.DMA, pltpu.SemaphoreType.DMA, pltpu.SemaphoreType.DMA],
    )(half)


def _allreduce_small(v):
    R, Wd = v.shape

    def body(v_ref, o_ref, buf, send_sems, recv_sems):
        x, y, c, _ = _place()
        me = 4 * x + 2 * y + c
        buf[me] = v_ref[...]
        sends = []
        for k in range(1, 8):
            peer = ((x + (k >> 2)) % 2, (y + ((k >> 1) & 1)) % 2, (c + (k & 1)) % 2)
            sends.append(pltpu.make_async_remote_copy(
                src_ref=v_ref, dst_ref=buf.at[me], send_sem=send_sems.at[k - 1], recv_sem=recv_sems.at[k - 1],
                device_id=peer, device_id_type=MESH))
        for cp in sends:
            cp.start()
        for k in range(1, 8):
            px, py, pc = (x + (k >> 2)) % 2, (y + ((k >> 1) & 1)) % 2, (c + (k & 1)) % 2
            pltpu.make_async_remote_copy(
                src_ref=v_ref, dst_ref=buf.at[4 * px + 2 * py + pc], send_sem=send_sems.at[k - 1],
                recv_sem=recv_sems.at[k - 1], device_id=(x, y, c), device_id_type=MESH).wait_recv()
        acc = buf[0]
        for d in range(1, 8):
            acc = acc + buf[d]
        o_ref[...] = acc
        for cp in sends:
            cp.wait_send()

    vm = pl.BlockSpec(memory_space=pltpu.VMEM)
    return pl.pallas_call(
        body, name="allreduce_small", in_specs=[vm], out_specs=vm,
        out_shape=jax.ShapeDtypeStruct((R, Wd), F32),
        scratch_shapes=[pltpu.VMEM((8, R, Wd), F32), pltpu.SemaphoreType.DMA((7,)), pltpu.SemaphoreType.DMA((7,))],
    )(v)


SHARDED = (
    ("w_in", (2, 1024, 1730), 2),
    ("w_branch", (2, 3, 512, 256), 3),
    ("w_mix_out", (2, 256, 1024), 1),
    ("w_xq", (2, 256, 1024), 1),
    ("w_xkv", (2, 1024, 512), 2),
    ("w_xo", (2, 256, 1024), 1),
    ("w_ffn_gate", (2, 1024, 704), 2),
    ("w_ffn_up", (2, 1024, 704), 2),
    ("w_ffn_down", (2, 704, 1024), 1),
    ("conv_w", (2, 3, 128), 2),
)
PACK_W = 1024
PACK_ELEMS = sum(int(np.prod(s)) for _, s, _ in SHARDED)
PACK_ROWS = -(-PACK_ELEMS // (PACK_W * 32)) * 32


def _pack(parts, dtype):
    flat = jnp.concatenate([p.astype(dtype).reshape(-1) for p in parts]
                           + [jnp.zeros((PACK_ROWS * PACK_W - PACK_ELEMS,), dtype)])
    return flat.reshape(PACK_ROWS, PACK_W)


def _unpack(pack):
    flat = pack.reshape(-1)
    out, off = {}, 0
    for name, shape, _ in SHARDED:
        n = int(np.prod(shape))
        out[name] = flat[off:off + n].reshape(shape)
        off += n
    return out


SMALL = (
    ("mix_norm_g", (2, 1024)), ("xattn_norm_g", (2, 1024)), ("mem_norm_g", (2, 1024)),
    ("ffn_norm_g", (2, 1024)), ("final_norm_g", (1024,)),
    ("forget_bias", (2, 8)), ("sink", (2, 8)), ("rel_bias", (32, 8)),
)
SMALL_ROWS = 112


def _pack_small(vals):
    rows = []
    for name, shape in SMALL:
        v = vals[name].astype(F32)
        if shape[-1] == 1024:
            rows.append(v.reshape(-1, 128))
        else:
            rows.append(jnp.pad(v, ((0, 0), (0, 120))))
    rows = jnp.concatenate(rows, axis=0)
    return jnp.pad(rows, ((0, SMALL_ROWS - rows.shape[0]), (0, 0)))


def _unpack_small(pack):
    out, off = {}, 0
    for name, shape in SMALL:
        if shape[-1] == 1024:
            n = int(np.prod(shape)) // 128
            out[name] = pack[off:off + n].reshape(shape)
        else:
            n = shape[0]
            out[name] = pack[off:off + n, 0:8]
        off += n
    return out


W_IN_PERM = ((3848, 6920), (0, 3072), (3080, 3848), (3072, 3080))


def _perm_w_in(w):
    parts = [w[:, a:b] for a, b in W_IN_PERM]
    return jnp.concatenate(parts + [jnp.zeros((w.shape[0], PROJ_PAD - IN_COLS), w.dtype)], axis=1)


def _unperm_w_in(p):
    return jnp.concatenate([p[:, 3072:6144], p[:, 6912:6920], p[:, 6144:6912], p[:, 0:3072]], axis=1)


def _pad_row8(v):
    return jnp.pad(v.astype(F32).reshape(1, 8), ((0, 0), (0, 120)))


def _local_step(x, mem, tgt, W, rel_bias):
    T = x.shape[0]
    bucket = jnp.asarray(_bucket_table())
    bias = _swa_bias(rel_bias, bucket, "swa_bias")
    saved = []
    for l in range(DEPTH):
        n = "l%d_" % l
        s = {"x0": x}
        wcat = W["w_in_p"][l]
        h = _rms_fwd(x, W["mix_norm_g"][l:l + 1], n + "mix_norm")
        pm = _mm(h, wcat[:, :PROJ_MAIN], "nn", BF16, n + "proj", bn=768)
        fg = _mm(h, wcat[:, PROJ_MAIN:], "nn", F32, n + "proj_fg")
        fb = _pad_row8(W["forget_bias"][l])
        c_col = _fox_gate_fwd(fg, fb, n + "fox_gate")
        c_row = c_col[:, 0:8].T
        cw = jnp.pad(W["conv_w"][l], ((0, 5), (0, 0)))
        y_conv = _conv_fwd(pm, cw, n + "conv")
        y_fox, lse = _fox2_fwd(pm, c_row, n + "fox")
        sink = _pad_row8(W["sink"][l])
        y_swa, mlse = _swa_fwd(pm, bias, sink, n + "swa")
        ys = (y_conv, y_fox, y_swa)
        us = tuple(_mm(ys[b], W["w_branch"][l, b], "nn", BF16, n + "branch%d" % b) for b in range(3))
        merged = _merge_fwd(pm, us, n + "merge")
        x1 = _mm(merged, W["w_mix_out"][l], "nn", F32, n + "mix_out", res=x)
        xn1 = _rms_fwd(x1, W["xattn_norm_g"][l:l + 1], n + "xattn_norm")
        memn = _rms_fwd(mem, W["mem_norm_g"][l:l + 1], n + "mem_norm")
        qx = _mm(xn1, W["w_xq"][l], "nn", BF16, n + "xq")
        kv = _mm(memn, W["w_xkv"][l], "nn", BF16, n + "xkv")
        ox = _xattn_fwd(qx, kv, n + "xattn")
        x2 = _mm(ox, W["w_xo"][l], "nn", F32, n + "xo", res=x1)
        xn2 = _rms_fwd(x2, W["ffn_norm_g"][l:l + 1], n + "ffn_norm")
        ab = _mm(xn2, W["w_gu"][l], "nn", BF16, n + "ffn_in", bn=512)
        hm = _swiglu_fwd(ab, n + "swiglu")
        x3 = _mm(hm, W["w_ffn_down"][l], "nn", F32, n + "ffn_out", res=x2, bk=1408)
        s.update(h=h, pm=pm, fg=fg, fb=fb, c_col=c_col, c_row=c_row, cw=cw, ys=ys, lse=lse, sink=sink,
                 mlse=mlse, us=us, merged=merged, x1=x1, xn1=xn1, memn=memn, qx=qx, kv=kv, ox=ox,
                 x2=x2, xn2=xn2, ab=ab, hm=hm)
        saved.append(s)
        x = x3

    loss_row, dx, dg_final = _final_loss(x, W["final_norm_g"].reshape(1, D_MODEL), tgt, "final_loss")
    G = {name: [None] * DEPTH for name in
         ("mix_norm_g", "w_in_p", "forget_bias", "conv_w", "sink", "w_branch", "w_mix_out", "xattn_norm_g",
          "mem_norm_g", "w_xq", "w_xkv", "w_xo", "ffn_norm_g", "w_gu", "w_ffn_down")}
    dbias_tot = None
    for l in reversed(range(DEPTH)):
        n = "l%d_" % l
        s = saved[l]
        dhm = _mm(dx, W["w_ffn_down"][l], "nt", BF16, n + "d_hm", bn=1408)
        G["w_ffn_down"][l] = _mm(s["hm"], dx, "tn", F32, n + "dw_down", bm=1408, bk=512)
        dab = _swiglu_bwd(s["ab"], dhm, n + "d_swiglu")
        dxn2 = _mm(dab, W["w_gu"][l], "nt", BF16, n + "d_xn2", bk=1408)
        G["w_gu"][l] = _mm(s["xn2"], dab, "tn", F32, n + "dw_gu", bn=512, bk=512)
        dx, G["ffn_norm_g"][l] = _rms_bwd(s["x2"], W["ffn_norm_g"][l:l + 1], dxn2, dx, n + "d_ffn_norm")
        dox = _mm(dx, W["w_xo"][l], "nt", BF16, n + "d_ox")
        G["w_xo"][l] = _mm(s["ox"], dx, "tn", F32, n + "dw_xo", bk=512)
        dqx, dkv = _xattn_bwd(s["qx"], s["kv"], dox, n + "d_xattn")
        dxn1 = _mm(dqx, W["w_xq"][l], "nt", BF16, n + "d_xn1")
        G["w_xq"][l] = _mm(s["xn1"], dqx, "tn", F32, n + "dw_xq", bk=512)
        dmemn = _mm(dkv, W["w_xkv"][l], "nt", BF16, n + "d_memn")
        G["w_xkv"][l] = _mm(s["memn"], dkv, "tn", F32, n + "dw_xkv")
        _, G["mem_norm_g"][l] = _rms_bwd(mem, W["mem_norm_g"][l:l + 1], dmemn, None, n + "d_mem_norm")
        dx, G["xattn_norm_g"][l] = _rms_bwd(s["x1"], W["xattn_norm_g"][l:l + 1], dxn1, dx, n + "d_xattn_norm")
        dmerged = _mm(dx, W["w_mix_out"][l], "nt", BF16, n + "d_merged")
        G["w_mix_out"][l] = _mm(s["merged"], dx, "tn", F32, n + "dw_mix_out", bk=512)
        du0, du1, du2, dgates = _merge_bwd(s["pm"], s["us"], dmerged, n + "d_merge")
        dus = (du0, du1, du2)
        dys = [_mm(dus[b], W["w_branch"][l, b], "nt", BF16, n + "d_y%d" % b) for b in range(3)]
        G["w_branch"][l] = jnp.stack(
            [_mm(s["ys"][b], dus[b], "tn", F32, n + "dw_branch%d" % b, bk=512) for b in range(3)])
        dcb, dcc, dcu, dcw = _conv_bwd(s["pm"], s["cw"], dys[0], n + "d_conv")
        G["conv_w"][l] = dcw[0:3]
        delta = _fox_delta(s["ys"][1], dys[1], n + "fox_delta")
        dfq, delta = _fox2_bwd_dq(s["pm"], dys[1], s["c_row"], s["lse"], delta, n + "d_fox_q")
        dfk, dfv, dc = _fox2_bwd_dkv(s["pm"], dys[1], s["c_col"], s["lse"][:, 0:8].T, delta[:, 0:8].T,
                                     n + "d_fox_kv")
        dfg, dfb = _fox_gate_bwd(dc, s["fg"], s["fb"], n + "d_fox_gate")
        G["forget_bias"][l] = dfb[0, 0:8]
        dsq, dkc, dkp, dvc, dvp, dbias, dsink = _swa_bwd(s["pm"], bias, s["sink"], dys[2], s["mlse"],
                                                        n + "d_swa")
        G["sink"][l] = dsink[0, 0:8]
        dbias_tot = dbias if dbias_tot is None else dbias_tot + dbias
        zpad = jnp.zeros((WINDOW, 128), F32)
        dsk = dkc + jnp.concatenate([dkp[WINDOW:], zpad], axis=0)
        dsv = dvc + jnp.concatenate([dvp[WINDOW:], zpad], axis=0)
        dproj = jnp.concatenate([dgates, dcb, dcc, dcu, dfq, dfk, dfv, dsq, dsk.astype(BF16),
                                 dsv.astype(BF16), dfg.astype(BF16)], axis=1)
        dh = _mm(dproj, W["w_in_p"][l], "nt", BF16, n + "d_h", bk=1408)
        G["w_in_p"][l] = _mm(s["h"], dproj, "tn", F32, n + "dw_in", bn=640, bk=512)
        dx, G["mix_norm_g"][l] = _rms_bwd(s["x0"], W["mix_norm_g"][l:l + 1], dh, dx, n + "d_mix_norm")
    drb = _swa_dbias_reduce(dbias_tot, bucket, "swa_dbias")
    G["rel_bias"] = drb[:, 0:8]
    G["final_norm_g"] = dg_final.reshape(D_MODEL)
    return loss_row, dx, G


def kernel(x, mem, mix_norm_g, w_in, forget_bias, conv_w, sink, w_branch, w_mix_out, rel_bias, xattn_norm_g, mem_norm_g, w_xq, w_xkv, w_xo, ffn_norm_g, w_ffn_gate, w_ffn_up, w_ffn_down, final_norm_g, loss_target, m_mix_norm_g, m_w_in, m_forget_bias, m_conv_w, m_sink, m_w_branch, m_w_mix_out, m_rel_bias, m_xattn_norm_g, m_mem_norm_g, m_w_xq, m_w_xkv, m_w_xo, m_ffn_norm_g, m_w_ffn_gate, m_w_ffn_up, m_w_ffn_down, m_final_norm_g, v_mix_norm_g, v_w_in, v_forget_bias, v_conv_w, v_sink, v_w_branch, v_w_mix_out, v_rel_bias, v_xattn_norm_g, v_mem_norm_g, v_w_xq, v_w_xkv, v_w_xo, v_ffn_norm_g, v_w_ffn_gate, v_w_ffn_up, v_w_ffn_down, v_final_norm_g):
    order = ("mix_norm_g", "w_in", "forget_bias", "conv_w", "sink", "w_branch", "w_mix_out", "rel_bias",
             "xattn_norm_g", "mem_norm_g", "w_xq", "w_xkv", "w_xo", "ffn_norm_g", "w_ffn_gate", "w_ffn_up",
             "w_ffn_down", "final_norm_g")
    w_sh = dict(zip(order, (mix_norm_g, w_in, forget_bias, conv_w, sink, w_branch, w_mix_out, rel_bias,
                            xattn_norm_g, mem_norm_g, w_xq, w_xkv, w_xo, ffn_norm_g, w_ffn_gate, w_ffn_up,
                            w_ffn_down, final_norm_g)))
    m_sh = dict(zip(order, (m_mix_norm_g, m_w_in, m_forget_bias, m_conv_w, m_sink, m_w_branch, m_w_mix_out,
                            m_rel_bias, m_xattn_norm_g, m_mem_norm_g, m_w_xq, m_w_xkv, m_w_xo, m_ffn_norm_g,
                            m_w_ffn_gate, m_w_ffn_up, m_w_ffn_down, m_final_norm_g)))
    v_sh = dict(zip(order, (v_mix_norm_g, v_w_in, v_forget_bias, v_conv_w, v_sink, v_w_branch, v_w_mix_out,
                            v_rel_bias, v_xattn_norm_g, v_mem_norm_g, v_w_xq, v_w_xkv, v_w_xo, v_ffn_norm_g,
                            v_w_ffn_gate, v_w_ffn_up, v_w_ffn_down, v_final_norm_g)))

    gathered = _ag_packs(_pack([w_sh[name] for name, _, _ in SHARDED], BF16))
    per_chip = [_unpack(gathered[j]) for j in range(4)]
    full = {name: jnp.concatenate([per_chip[j][name] for j in range(4)], axis=ax) for name, _, ax in SHARDED}
    W = {k: w_sh[k] for k in ("mix_norm_g", "forget_bias", "sink", "xattn_norm_g", "mem_norm_g",
                              "ffn_norm_g", "final_norm_g")}
    W["w_in_p"] = jnp.stack([_perm_w_in(full["w_in"][l]) for l in range(DEPTH)])
    W["w_gu"] = jnp.concatenate([full["w_ffn_gate"], full["w_ffn_up"]], axis=2)
    W["conv_w"] = full["conv_w"].astype(F32)
    for k in ("w_branch", "w_mix_out", "w_xq", "w_xkv", "w_xo", "w_ffn_down"):
        W[k] = full[k]
    loss_row, dx, G = _local_step(x[0], mem[0], loss_target[0], W, rel_bias)

    gfull = {
        "w_in": jnp.stack([_unperm_w_in(G["w_in_p"][l]) for l in range(DEPTH)]),
        "w_branch": jnp.stack(G["w_branch"]),
        "w_mix_out": jnp.stack(G["w_mix_out"]),
        "w_xq": jnp.stack(G["w_xq"]),
        "w_xkv": jnp.stack(G["w_xkv"]),
        "w_xo": jnp.stack(G["w_xo"]),
        "w_ffn_gate": jnp.stack([G["w_gu"][l][:, :D_FF] for l in range(DEPTH)]),
        "w_ffn_up": jnp.stack([G["w_gu"][l][:, D_FF:] for l in range(DEPTH)]),
        "w_ffn_down": jnp.stack(G["w_ffn_down"]),
        "conv_w": jnp.stack(G["conv_w"]),
    }
    slabs = []
    for j in range(4):
        parts = []
        for name, shape, ax in SHARDED:
            n = shape[ax]
            parts.append(lax.slice_in_dim(gfull[name], j * n, (j + 1) * n, axis=ax))
        slabs.append(_pack(parts, BF16))
    g4 = jnp.stack(slabs)
    cidx = lax.axis_index("c").astype(I32).reshape(1)
    pair = _rs_add_pair(g4, _rs_sibling(g4), cidx)
    half = _rs_add_chips(_rs_chips(pair))
    gsh = _unpack(_rs_share(half))

    small = _unpack_small(_allreduce_small(_pack_small({
        "mix_norm_g": jnp.concatenate(G["mix_norm_g"], axis=0),
        "xattn_norm_g": jnp.concatenate(G["xattn_norm_g"], axis=0),
        "mem_norm_g": jnp.concatenate(G["mem_norm_g"], axis=0),
        "ffn_norm_g": jnp.concatenate(G["ffn_norm_g"], axis=0),
        "final_norm_g": G["final_norm_g"],
        "forget_bias": jnp.stack(G["forget_bias"]),
        "sink": jnp.stack(G["sink"]),
        "rel_bias": G["rel_bias"],
    })))
    grads = dict(gsh)
    grads.update(small)

    sm_names = [name for name, _ in SMALL]
    sd, sm_, sv_ = _adamw(_pack_small({k: w_sh[k] for k in sm_names}), _pack_small({k: grads[k] for k in sm_names}),
                          _pack_small({k: m_sh[k] for k in sm_names}), _pack_small({k: v_sh[k] for k in sm_names}),
                          "adamw_small")
    delta, new_m, new_v = _unpack_small(sd), _unpack_small(sm_), _unpack_small(sv_)
    for name, shape, _ in SHARDED:
        two_d = (-1, shape[-1])
        d, nm, nv = _adamw(w_sh[name].reshape(two_d), grads[name].reshape(two_d), m_sh[name].reshape(two_d),
                           v_sh[name].reshape(two_d), "adamw_" + name)
        delta[name], new_m[name], new_v[name] = d.reshape(shape), nm.reshape(shape), nv.reshape(shape)

    loss = lax.psum(loss_row[0, 0], ("x", "y", "c"))
    return (loss, dx[None], *[grads[k] for k in order], *[delta[k] for k in order],
            *[new_m[k] for k in order], *[new_v[k] for k in order])
```

```python
import math

import numpy as np
import jax
import jax.numpy as jnp
from jax import lax
from jax.experimental import pallas as pl
from jax.experimental.pallas import tpu as pltpu

F32 = jnp.float32
BF16 = jnp.bfloat16
I32 = jnp.int32

D_MODEL = 1024
DEPTH = 2
HEAD_DIM = 64
BRANCH = 512
N_BUCKETS = 32
WINDOW = 128
MEM_LEN = 256
X_HEADS = 4
X_HEAD_DIM = 256
D_FF = 2816
IN_COLS = 6920
PROJ_MAIN = 6912
PROJ_PAD = 7040
RMS_EPS = 1e-6
NEG = -1e30
ATT_SCALE = 0.125
X_SCALE = 0.0625

ADAM_LR = 0.001
ADAM_B1 = 0.9
ADAM_B2 = 0.999
ADAM_EPS = 1e-08
ADAM_WD = 0.01
ADAM_STEP = 10

VMEM_LIMIT = 48 * 1024 * 1024
MESH = pl.DeviceIdType.MESH

CB_GATE = (0, 1, 2)
CB_B, CB_C, CB_U, CB_FQ, CB_FK, CB_FV, CB_SQ = 6, 7, 8, 9, 10, 11, 12
CB_SK, CB_SV = 52, 53


def _cp(sem):
    return pltpu.CompilerParams(dimension_semantics=sem, vmem_limit_bytes=VMEM_LIMIT)


def _pick(n, prefs):
    for p in prefs:
        if p <= n and n % p == 0:
            return p
    return n


def _dot(a, b, dims):
    return lax.dot_general(a, b, (dims, ((), ())), preferred_element_type=F32)


def _dot_nn(a, b):
    return _dot(a, b, ((1,), (0,)))


def _dot_nt(a, b):
    return _dot(a, b, ((1,), (1,)))


def _dot_tn(a, b):
    return _dot(a, b, ((0,), (0,)))


def _mm(a, b, mode, out_dtype, name, res=None, bm=1024, bn=1024, bk=1024):
    if mode == "nn":
        (M, K), (K2, N) = a.shape, b.shape
    elif mode == "nt":
        (M, K), (N, K2) = a.shape, b.shape
    else:
        (K, M), (K2, N) = a.shape, b.shape
    assert K == K2, (name, a.shape, b.shape)
    bm = _pick(M, (bm, 1024, 512, 256, 128))
    bn = _pick(N, (bn, 1024, 768, 640, 512, 384, 256, 128))
    bk = _pick(K, (bk, 1024, 768, 640, 512, 384, 256, 128))
    nk = K // bk
    if mode == "tn":
        a_spec = pl.BlockSpec((bk, bm), lambda i, j, k: (k, i))
    else:
        a_spec = pl.BlockSpec((bm, bk), lambda i, j, k: (i, k))
    if mode == "nt":
        b_spec = pl.BlockSpec((bn, bk), lambda i, j, k: (j, k))
    else:
        b_spec = pl.BlockSpec((bk, bn), lambda i, j, k: (k, j))
    dims = {"nn": ((1,), (0,)), "nt": ((1,), (1,)), "tn": ((0,), (0,))}[mode]
    o_spec = pl.BlockSpec((bm, bn), lambda i, j, k: (i, j))
    has_res = res is not None

    def body(*refs):
        if has_res:
            a_ref, b_ref, r_ref, o_ref = refs[:4]
            scr = refs[4:]
        else:
            a_ref, b_ref, o_ref = refs[:3]
            r_ref = None
            scr = refs[3:]
        p = _dot(a_ref[...].astype(BF16), b_ref[...].astype(BF16), dims)
        if nk == 1:
            if has_res:
                p = p + r_ref[...]
            o_ref[...] = p.astype(out_dtype)
        else:
            acc = scr[0]
            k = pl.program_id(2)

            @pl.when(k == 0)
            def _():
                acc[...] = p

            @pl.when(k > 0)
            def _():
                acc[...] += p

            @pl.when(k == nk - 1)
            def _():
                r = acc[...]
                if has_res:
                    r = r + r_ref[...]
                o_ref[...] = r.astype(out_dtype)

    ins = [a, b] + ([res] if has_res else [])
    in_specs = [a_spec, b_spec] + ([o_spec] if has_res else [])
    return pl.pallas_call(
        body, name=name, grid=(M // bm, N // bn, nk),
        in_specs=in_specs, out_specs=o_spec,
        out_shape=jax.ShapeDtypeStruct((M, N), out_dtype),
        scratch_shapes=[pltpu.VMEM((bm, bn), F32)] if nk > 1 else [],
        compiler_params=_cp(("parallel", "parallel", "arbitrary")),
    )(*ins)


def _rms_fwd(x, g, name):
    T, Dm = x.shape
    bt = _pick(T, (512, 256))

    def body(x_ref, g_ref, o_ref):
        xv = x_ref[...]
        r = lax.rsqrt(jnp.mean(xv * xv, axis=-1, keepdims=True) + RMS_EPS)
        o_ref[...] = ((xv * r) * g_ref[...]).astype(BF16)

    return pl.pallas_call(
        body, name=name, grid=(T // bt,),
        in_specs=[pl.BlockSpec((bt, Dm), lambda i: (i, 0)), pl.BlockSpec((1, Dm), lambda i: (0, 0))],
        out_specs=pl.BlockSpec((bt, Dm), lambda i: (i, 0)),
        out_shape=jax.ShapeDtypeStruct((T, Dm), BF16),
        compiler_params=_cp(("parallel",)),
    )(x, g)


def _rms_bwd(x, g, dh, dres, name):
    T, Dm = x.shape
    bt = _pick(T, (512, 256))
    want_dx = dres is not None

    def body(*refs):
        if want_dx:
            x_ref, g_ref, dh_ref, dr_ref, dx_ref, dg_ref = refs
        else:
            x_ref, g_ref, dh_ref, dg_ref = refs
        xv = x_ref[...]
        r = lax.rsqrt(jnp.mean(xv * xv, axis=-1, keepdims=True) + RMS_EPS)
        xh = xv * r
        dhv = dh_ref[...].astype(F32)

        @pl.when(pl.program_id(0) == 0)
        def _():
            dg_ref[...] = jnp.zeros_like(dg_ref)

        dg_ref[...] += jnp.sum(dhv * xh, axis=0, keepdims=True)
        if want_dx:
            dyg = dhv * g_ref[...]
            dx_ref[...] = dr_ref[...] + r * (dyg - xh * jnp.mean(dyg * xh, axis=-1, keepdims=True))

    row = pl.BlockSpec((bt, Dm), lambda i: (i, 0))
    vec = pl.BlockSpec((1, Dm), lambda i: (0, 0))
    if want_dx:
        return pl.pallas_call(
            body, name=name, grid=(T // bt,),
            in_specs=[row, vec, row, row], out_specs=[row, vec],
            out_shape=[jax.ShapeDtypeStruct((T, Dm), F32), jax.ShapeDtypeStruct((1, Dm), F32)],
            compiler_params=_cp(("arbitrary",)),
        )(x, g, dh, dres)
    return None, pl.pallas_call(
        body, name=name, grid=(T // bt,),
        in_specs=[row, vec, row], out_specs=vec,
        out_shape=jax.ShapeDtypeStruct((1, Dm), F32),
        compiler_params=_cp(("arbitrary",)),
    )(x, g, dh)


def _final_loss(x, g, tgt, name):
    T, Dm = x.shape
    bt = _pick(T, (512, 256))

    def body(x_ref, g_ref, t_ref, loss_ref, dx_ref, dg_ref):
        xv = x_ref[...]
        r = lax.rsqrt(jnp.mean(xv * xv, axis=-1, keepdims=True) + RMS_EPS)
        xh = xv * r
        gv = g_ref[...]
        err = xh * gv - t_ref[...]

        @pl.when(pl.program_id(0) == 0)
        def _():
            dg_ref[...] = jnp.zeros_like(dg_ref)
            loss_ref[...] = jnp.zeros_like(loss_ref)

        loss_ref[...] += jnp.sum(err * err) * (0.5 / Dm)
        dy = err * (1.0 / Dm)
        dg_ref[...] += jnp.sum(dy * xh, axis=0, keepdims=True)
        dyg = dy * gv
        dx_ref[...] = r * (dyg - xh * jnp.mean(dyg * xh, axis=-1, keepdims=True))

    row = pl.BlockSpec((bt, Dm), lambda i: (i, 0))
    vec = pl.BlockSpec((1, Dm), lambda i: (0, 0))
    return pl.pallas_call(
        body, name=name, grid=(T // bt,),
        in_specs=[row, vec, row],
        out_specs=[pl.BlockSpec((1, 128), lambda i: (0, 0)), row, vec],
        out_shape=[jax.ShapeDtypeStruct((1, 128), F32), jax.ShapeDtypeStruct((T, Dm), F32),
                   jax.ShapeDtypeStruct((1, Dm), F32)],
        compiler_params=_cp(("arbitrary",)),
    )(x, g, tgt)


HALO = 16


def _shift_down(z, zprev, s):
    rolled = pltpu.roll(z, s, 0)
    hp = pltpu.roll(zprev, s, 0)
    row = lax.broadcasted_iota(I32, hp.shape, 0)
    top = jnp.where(row < s, hp, rolled[:HALO])
    return jnp.concatenate([top, rolled[HALO:]], axis=0)


def _shift_up(z, znext, s):
    n = z.shape[0]
    rolled = pltpu.roll(z, n - s, 0)
    hn = pltpu.roll(znext, HALO - s, 0)
    row = lax.broadcasted_iota(I32, hn.shape, 0)
    bot = jnp.where(row >= HALO - s, hn, rolled[n - HALO:])
    return jnp.concatenate([rolled[:n - HALO], bot], axis=0)


def _conv_fwd(pm, cw, name):
    T = pm.shape[0]
    bt = _pick(T, (512, 256))
    hb = bt // HALO

    def body(b_ref, c_ref, u_ref, cp_ref, up_ref, w_ref, o_ref):
        i = pl.program_id(0)
        z = c_ref[...].astype(F32) * u_ref[...].astype(F32)
        zp = cp_ref[...].astype(F32) * up_ref[...].astype(F32)
        zp = jnp.where(i > 0, zp, 0.0)
        w = w_ref[...]
        y = w[2:3] * z + w[1:2] * _shift_down(z, zp, 1) + w[0:1] * _shift_down(z, zp, 2)
        o_ref[...] = (b_ref[...].astype(F32) * y).astype(BF16)

    def col(cb):
        return pl.BlockSpec((bt, BRANCH), lambda i: (i, cb))

    def prev(cb):
        return pl.BlockSpec((HALO, BRANCH), lambda i: (jnp.maximum(i * hb - 1, 0), cb))

    return pl.pallas_call(
        body, name=name, grid=(T // bt,),
        in_specs=[col(CB_B), col(CB_C), col(CB_U), prev(CB_C), prev(CB_U),
                  pl.BlockSpec((8, BRANCH), lambda i: (0, 0))],
        out_specs=pl.BlockSpec((bt, BRANCH), lambda i: (i, 0)),
        out_shape=jax.ShapeDtypeStruct((T, BRANCH), BF16),
        compiler_params=_cp(("parallel",)),
    )(pm, pm, pm, pm, pm, cw)


def _conv_bwd(pm, cw, dy, name):
    T = pm.shape[0]
    bt = _pick(T, (512, 256))
    hb = bt // HALO
    nb = T // bt
    last_h = T // HALO - 1

    def body(b_ref, c_ref, u_ref, cp_ref, up_ref, bn_ref, dy_ref, dyn_ref, w_ref,
             db_ref, dc_ref, du_ref, dw_ref):
        i = pl.program_id(0)
        cv = c_ref[...].astype(F32)
        uv = u_ref[...].astype(F32)
        bv = b_ref[...].astype(F32)
        z = cv * uv
        zp = jnp.where(i > 0, cp_ref[...].astype(F32) * up_ref[...].astype(F32), 0.0)
        w = w_ref[...]
        z1 = _shift_down(z, zp, 1)
        z2 = _shift_down(z, zp, 2)
        yc = w[2:3] * z + w[1:2] * z1 + w[0:1] * z2
        dyv = dy_ref[...].astype(F32)
        db_ref[...] = (dyv * yc).astype(BF16)
        g = dyv * bv
        gn = jnp.where(i < nb - 1, dyn_ref[...].astype(F32) * bn_ref[...].astype(F32), 0.0)
        dz = w[2:3] * g + w[1:2] * _shift_up(g, gn, 1) + w[0:1] * _shift_up(g, gn, 2)
        dc_ref[...] = (dz * uv).astype(BF16)
        du_ref[...] = (dz * cv).astype(BF16)

        @pl.when(i == 0)
        def _():
            dw_ref[...] = jnp.zeros_like(dw_ref)

        dw_ref[0:1, :] += jnp.sum(g * z2, axis=0, keepdims=True)
        dw_ref[1:2, :] += jnp.sum(g * z1, axis=0, keepdims=True)
        dw_ref[2:3, :] += jnp.sum(g * z, axis=0, keepdims=True)

    def col(cb):
        return pl.BlockSpec((bt, BRANCH), lambda i: (i, cb))

    def prev(cb):
        return pl.BlockSpec((HALO, BRANCH), lambda i: (jnp.maximum(i * hb - 1, 0), cb))

    def nxt(cb):
        return pl.BlockSpec((HALO, BRANCH), lambda i: (jnp.minimum((i + 1) * hb, last_h), cb))

    own = pl.BlockSpec((bt, BRANCH), lambda i: (i, 0))
    w_spec = pl.BlockSpec((8, BRANCH), lambda i: (0, 0))
    act = jax.ShapeDtypeStruct((T, BRANCH), BF16)
    return pl.pallas_call(
        body, name=name, grid=(nb,),
        in_specs=[col(CB_B), col(CB_C), col(CB_U), prev(CB_C), prev(CB_U), nxt(CB_B), own,
                  pl.BlockSpec((HALO, BRANCH), lambda i: (jnp.minimum((i + 1) * hb, last_h), 0)), w_spec],
        out_specs=[own, own, own, w_spec],
        out_shape=[act, act, act, jax.ShapeDtypeStruct((8, BRANCH), F32)],
        compiler_params=_cp(("arbitrary",)),
    )(pm, pm, pm, pm, pm, pm, dy, dy, cw)


def _log_sigmoid(z):
    return jnp.minimum(z, 0.0) - jnp.log(1.0 + jnp.exp(-jnp.abs(z)))


def _fox_gate_fwd(fg, fb, name):
    T = fg.shape[0]
    bt = _pick(T, (256,))

    def body(f_ref, b_ref, c_ref, carry):
        @pl.when(pl.program_id(0) == 0)
        def _():
            carry[...] = jnp.zeros_like(carry)

        xv = _log_sigmoid(f_ref[...] + b_ref[...])
        row = lax.broadcasted_iota(I32, xv.shape, 0)
        s = 1
        while s < bt:
            xv = xv + jnp.where(row >= s, pltpu.roll(xv, s, 0), 0.0)
            s *= 2
        xv = xv + carry[...]
        c_ref[...] = xv
        carry[...] = xv[bt - 1:bt, :]

    blk = pl.BlockSpec((bt, 128), lambda i: (i, 0))
    return pl.pallas_call(
        body, name=name, grid=(T // bt,),
        in_specs=[blk, pl.BlockSpec((1, 128), lambda i: (0, 0))],
        out_specs=blk, out_shape=jax.ShapeDtypeStruct((T, 128), F32),
        scratch_shapes=[pltpu.VMEM((1, 128), F32)],
        compiler_params=_cp(("arbitrary",)),
    )(fg, fb)


def _fox_gate_bwd(dc, fg, fb, name):
    T = fg.shape[0]
    bt = _pick(T, (256,))
    nb = T // bt

    def body(d_ref, f_ref, b_ref, o_ref, db_ref, carry):
        @pl.when(pl.program_id(0) == 0)
        def _():
            carry[...] = jnp.zeros_like(carry)
            db_ref[...] = jnp.zeros_like(db_ref)

        xv = d_ref[...]
        row = lax.broadcasted_iota(I32, xv.shape, 0)
        s = 1
        while s < bt:
            xv = xv + jnp.where(row < bt - s, pltpu.roll(xv, bt - s, 0), 0.0)
            s *= 2
        xv = xv + carry[...]
        carry[...] = xv[0:1, :]
        z = f_ref[...] + b_ref[...]
        dz = xv * (1.0 / (1.0 + jnp.exp(z)))
        o_ref[...] = dz
        db_ref[...] += jnp.sum(dz, axis=0, keepdims=True)

    blk = pl.BlockSpec((bt, 128), lambda i: (nb - 1 - i, 0))
    vec = pl.BlockSpec((1, 128), lambda i: (0, 0))
    return pl.pallas_call(
        body, name=name, grid=(nb,),
        in_specs=[blk, blk, vec], out_specs=[blk, vec],
        out_shape=[jax.ShapeDtypeStruct((T, 128), F32), jax.ShapeDtypeStruct((1, 128), F32)],
        scratch_shapes=[pltpu.VMEM((1, 128), F32)],
        compiler_params=_cp(("arbitrary",)),
    )(dc, fg, fb)


def _lane_lo(shape):
    return lax.broadcasted_iota(I32, shape, 1) < HEAD_DIM


def _put_col(shape, h, col):
    lane = lax.broadcasted_iota(I32, shape, 1)
    return jnp.where(lane == h, col, 0.0)


def _fox_fwd(pm, c_col, c_row, name):
    T = pm.shape[0]
    bq = _pick(T, (512, 256))
    bk = bq
    nq = T // bq

    def body(q_ref, k_ref, v_ref, cq_ref, ck_ref, o_ref, lse_ref, acc, m_s, l_s):
        qi = pl.program_id(0)
        ki = pl.program_id(1)

        @pl.when(ki == 0)
        def _():
            acc[...] = jnp.zeros_like(acc)
            m_s[...] = jnp.full_like(m_s, NEG)
            l_s[...] = jnp.zeros_like(l_s)

        @pl.when(ki <= qi)
        def _():
            row = lax.broadcasted_iota(I32, (bq, bk), 0) + qi * bq
            colv = lax.broadcasted_iota(I32, (bq, bk), 1) + ki * bk
            causal = colv <= row
            klo = _lane_lo((bk, 128))
            qlo = _lane_lo((bq, 128))
            cq = cq_ref[...]
            ck = ck_ref[...]
            for p in range(4):
                sl = slice(128 * p, 128 * p + 128)
                qp = q_ref[:, sl] * ATT_SCALE
                kp = k_ref[:, sl]
                vp = v_ref[:, sl]
                kz = jnp.zeros_like(kp)
                ks = (jnp.where(klo, kp, kz), jnp.where(klo, kz, kp))
                alphas, pvs = [], []
                for j in range(2):
                    h = 2 * p + j
                    s = _dot_nt(qp, ks[j]) + (cq[:, h:h + 1] - ck[h:h + 1, :])
                    s = jnp.where(causal, s, NEG)
                    m_old = m_s[h][:, 0:1]
                    m_new = jnp.maximum(m_old, jnp.max(s, axis=-1, keepdims=True))
                    alpha = jnp.exp(m_old - m_new)
                    pe = jnp.exp(s - m_new)
                    l_new = alpha * l_s[h][:, 0:1] + jnp.sum(pe, axis=-1, keepdims=True)
                    m_s[h] = jnp.broadcast_to(m_new, (bq, 128))
                    l_s[h] = jnp.broadcast_to(l_new, (bq, 128))
                    alphas.append(alpha)
                    pvs.append(_dot_nn(pe.astype(BF16), vp))
                a = jnp.where(qlo, alphas[0], alphas[1])
                acc[:, sl] = a * acc[:, sl] + jnp.where(qlo, pvs[0], pvs[1])

        @pl.when(ki == nq - 1)
        def _():
            qlo = _lane_lo((bq, 128))
            lse = jnp.zeros((bq, 128), F32)
            for p in range(4):
                sl = slice(128 * p, 128 * p + 128)
                l0 = l_s[2 * p][:, 0:1]
                l1 = l_s[2 * p + 1][:, 0:1]
                o_ref[:, sl] = (acc[:, sl] / jnp.where(qlo, l0, l1)).astype(BF16)
                lse = lse + _put_col((bq, 128), 2 * p, m_s[2 * p][:, 0:1] + jnp.log(l0))
                lse = lse + _put_col((bq, 128), 2 * p + 1, m_s[2 * p + 1][:, 0:1] + jnp.log(l1))
            lse_ref[...] = lse

    return pl.pallas_call(
        body, name=name, grid=(nq, nq),
        in_specs=[pl.BlockSpec((bq, BRANCH), lambda i, k: (i, CB_FQ)),
                  pl.BlockSpec((bk, BRANCH), lambda i, k: (jnp.minimum(k, i), CB_FK)),
                  pl.BlockSpec((bk, BRANCH), lambda i, k: (jnp.minimum(k, i), CB_FV)),
                  pl.BlockSpec((bq, 128), lambda i, k: (i, 0)),
                  pl.BlockSpec((8, bk), lambda i, k: (0, jnp.minimum(k, i)))],
        out_specs=[pl.BlockSpec((bq, BRANCH), lambda i, k: (i, 0)),
                   pl.BlockSpec((bq, 128), lambda i, k: (i, 0))],
        out_shape=[jax.ShapeDtypeStruct((T, BRANCH), BF16), jax.ShapeDtypeStruct((T, 128), F32)],
        scratch_shapes=[pltpu.VMEM((bq, BRANCH), F32), pltpu.VMEM((8, bq, 128), F32),
                        pltpu.VMEM((8, bq, 128), F32)],
        compiler_params=_cp(("parallel", "arbitrary")),
    )(pm, pm, pm, c_col, c_row)


def _fox_delta(o, do, name):
    T = o.shape[0]
    bt = _pick(T, (512, 256))

    def body(o_ref, d_ref, out_ref):
        prod = o_ref[...].astype(F32) * d_ref[...].astype(F32)
        out = jnp.zeros((bt, 128), F32)
        for h in range(8):
            out = out + _put_col((bt, 128), h, jnp.sum(prod[:, 64 * h:64 * h + 64], axis=-1, keepdims=True))
        out_ref[...] = out

    blk = pl.BlockSpec((bt, BRANCH), lambda i: (i, 0))
    return pl.pallas_call(
        body, name=name, grid=(T // bt,), in_specs=[blk, blk],
        out_specs=pl.BlockSpec((bt, 128), lambda i: (i, 0)),
        out_shape=jax.ShapeDtypeStruct((T, 128), F32),
        compiler_params=_cp(("parallel",)),
    )(o, do)


def _fox_bwd_dq(pm, do, c_col, c_row, lse, delta, name):
    T = pm.shape[0]
    bq = _pick(T, (512, 256))
    bk = bq
    nq = T // bq

    def body(q_ref, k_ref, v_ref, do_ref, cq_ref, ck_ref, lse_ref, dl_ref, dq_ref, dl2_ref, acc, esum):
        qi = pl.program_id(0)
        ki = pl.program_id(1)

        @pl.when(ki == 0)
        def _():
            acc[...] = jnp.zeros_like(acc)
            esum[...] = jnp.zeros_like(esum)

        @pl.when(ki <= qi)
        def _():
            row = lax.broadcasted_iota(I32, (bq, bk), 0) + qi * bq
            colv = lax.broadcasted_iota(I32, (bq, bk), 1) + ki * bk
            causal = colv <= row
            klo = _lane_lo((bk, 128))
            qlo = _lane_lo((bq, 128))
            cq = cq_ref[...]
            ck = ck_ref[...]
            lse_v = lse_ref[...]
            dl_v = dl_ref[...]
            es = jnp.zeros((bq, 128), F32)
            for p in range(4):
                sl = slice(128 * p, 128 * p + 128)
                qp = q_ref[:, sl] * ATT_SCALE
                kp = k_ref[:, sl]
                vp = v_ref[:, sl]
                dop = do_ref[:, sl]
                kz = jnp.zeros_like(kp)
                ks = (jnp.where(klo, kp, kz), jnp.where(klo, kz, kp))
                vs = (jnp.where(klo, vp, kz), jnp.where(klo, kz, vp))
                dqs = []
                for j in range(2):
                    h = 2 * p + j
                    s = _dot_nt(qp, ks[j]) + (cq[:, h:h + 1] - ck[h:h + 1, :])
                    s = jnp.where(causal, s, NEG)
                    pr = jnp.exp(s - lse_v[:, h:h + 1])
                    dp = _dot_nt(dop, vs[j])
                    ds = pr * (dp - dl_v[:, h:h + 1])
                    es = es + _put_col((bq, 128), h, jnp.sum(ds, axis=-1, keepdims=True))
                    dqs.append(_dot_nn(ds.astype(BF16), kp))
                acc[:, sl] += jnp.where(qlo, dqs[0], dqs[1])
            esum[...] += es

        @pl.when(ki == nq - 1)
        def _():
            dq_ref[...] = (acc[...] * ATT_SCALE).astype(BF16)
            dl2_ref[...] = dl_ref[...] + esum[...]

    qb = pl.BlockSpec((bq, 128), lambda i, k: (i, 0))
    return pl.pallas_call(
        body, name=name, grid=(nq, nq),
        in_specs=[pl.BlockSpec((bq, BRANCH), lambda i, k: (i, CB_FQ)),
                  pl.BlockSpec((bk, BRANCH), lambda i, k: (jnp.minimum(k, i), CB_FK)),
                  pl.BlockSpec((bk, BRANCH), lambda i, k: (jnp.minimum(k, i), CB_FV)),
                  pl.BlockSpec((bq, BRANCH), lambda i, k: (i, 0)),
                  qb, pl.BlockSpec((8, bk), lambda i, k: (0, jnp.minimum(k, i))), qb, qb],
        out_specs=[pl.BlockSpec((bq, BRANCH), lambda i, k: (i, 0)), qb],
        out_shape=[jax.ShapeDtypeStruct((T, BRANCH), BF16), jax.ShapeDtypeStruct((T, 128), F32)],
        scratch_shapes=[pltpu.VMEM((bq, BRANCH), F32), pltpu.VMEM((bq, 128), F32)],
        compiler_params=_cp(("parallel", "arbitrary")),
    )(pm, pm, pm, do, c_col, c_row, lse, delta)


def _fox_bwd_dkv(pm, do, c_col, c_row, lse_row, delta_row, name):
    T = pm.shape[0]
    bk = _pick(T, (512, 256))
    bq = bk
    nk = T // bk

    def body(q_ref, k_ref, v_ref, do_ref, cq_ref, ck_ref, lse_ref, dl_ref,
             dk_ref, dv_ref, dc_ref, dk_acc, dv_acc, dc_acc):
        ki = pl.program_id(0)
        qi = pl.program_id(1)

        @pl.when(qi == 0)
        def _():
            dk_acc[...] = jnp.zeros_like(dk_acc)
            dv_acc[...] = jnp.zeros_like(dv_acc)
            dc_acc[...] = jnp.zeros_like(dc_acc)

        @pl.when(qi >= ki)
        def _():
            krow = lax.broadcasted_iota(I32, (bk, bq), 0) + ki * bk
            qcol = lax.broadcasted_iota(I32, (bk, bq), 1) + qi * bq
            causal = krow <= qcol
            qlo = _lane_lo((bq, 128))
            klo = _lane_lo((bk, 128))
            cq = cq_ref[...]
            ck = ck_ref[...]
            lse_v = lse_ref[...]
            dl_v = dl_ref[...]
            dcs = jnp.zeros((bk, 128), F32)
            for p in range(4):
                sl = slice(128 * p, 128 * p + 128)
                qp = q_ref[:, sl]
                kp = k_ref[:, sl] * ATT_SCALE
                vp = v_ref[:, sl]
                dop = do_ref[:, sl]
                qz = jnp.zeros_like(qp)
                qs = (jnp.where(qlo, qp, qz), jnp.where(qlo, qz, qp))
                dos = (jnp.where(qlo, dop, qz), jnp.where(qlo, qz, dop))
                dks, dvs = [], []
                for j in range(2):
                    h = 2 * p + j
                    st = _dot_nt(kp, qs[j]) + (cq[h:h + 1, :] - ck[:, h:h + 1])
                    st = jnp.where(causal, st, NEG)
                    pt = jnp.exp(st - lse_v[h:h + 1, :])
                    dvs.append(_dot_nn(pt.astype(BF16), dop))
                    dpt = _dot_nt(vp, dos[j])
                    dst = pt * (dpt - dl_v[h:h + 1, :])
                    dks.append(_dot_nn(dst.astype(BF16), qp))
                    dcs = dcs - _put_col((bk, 128), h, jnp.sum(dst, axis=-1, keepdims=True))
                dk_acc[:, sl] += jnp.where(klo, dks[0], dks[1])
                dv_acc[:, sl] += jnp.where(klo, dvs[0], dvs[1])
            dc_acc[...] += dcs

        @pl.when(qi == nk - 1)
        def _():
            dk_ref[...] = (dk_acc[...] * ATT_SCALE).astype(BF16)
            dv_ref[...] = dv_acc[...].astype(BF16)
            dc_ref[...] = dc_acc[...]

    qrow = pl.BlockSpec((8, bq), lambda k, i: (0, jnp.maximum(i, k)))
    kb = pl.BlockSpec((bk, BRANCH), lambda k, i: (k, 0))
    return pl.pallas_call(
        body, name=name, grid=(nk, nk),
        in_specs=[pl.BlockSpec((bq, BRANCH), lambda k, i: (jnp.maximum(i, k), CB_FQ)),
                  pl.BlockSpec((bk, BRANCH), lambda k, i: (k, CB_FK)),
                  pl.BlockSpec((bk, BRANCH), lambda k, i: (k, CB_FV)),
                  pl.BlockSpec((bq, BRANCH), lambda k, i: (jnp.maximum(i, k), 0)),
                  qrow, pl.BlockSpec((bk, 128), lambda k, i: (k, 0)), qrow, qrow],
        out_specs=[kb, kb, pl.BlockSpec((bk, 128), lambda k, i: (k, 0))],
        out_shape=[jax.ShapeDtypeStruct((T, BRANCH), BF16), jax.ShapeDtypeStruct((T, BRANCH), BF16),
                   jax.ShapeDtypeStruct((T, 128), F32)],
        scratch_shapes=[pltpu.VMEM((bk, BRANCH), F32), pltpu.VMEM((bk, BRANCH), F32),
                        pltpu.VMEM((bk, 128), F32)],
        compiler_params=_cp(("parallel", "arbitrary")),
    )(pm, pm, pm, do, c_row, c_col, lse_row, delta_row)


FOX_ROWS = 32


FOX_UNROLL = 16


def _row_start(r, rows):
    return r * rows if isinstance(r, int) else pl.multiple_of(r * rows, rows)


def _chunk_loop(n, chunk):
    if n <= FOX_UNROLL:
        for u in range(n):
            chunk(u, 0)
        return

    def outer(i, carry):
        for u in range(FOX_UNROLL):
            chunk(i * FOX_UNROLL + u, carry)
        return carry

    lax.fori_loop(0, n // FOX_UNROLL, outer, 0)


def _tree(op, xs):
    xs = list(xs)
    while len(xs) > 1:
        xs = [op(xs[i], xs[i + 1]) if i + 1 < len(xs) else xs[i] for i in range(0, len(xs), 2)]
    return xs[0]


def _masked_halves(t):
    lo = _lane_lo(t.shape)
    z = jnp.zeros_like(t)
    return jnp.where(lo, t, z), jnp.where(lo, z, t)


def _fox2_fwd(pm, c_row, name):
    T = pm.shape[0]
    bq = _pick(T, (512, 256))
    bk = bq
    nq = T // bq
    R = FOX_ROWS
    ng = bk // 128

    def body(q_ref, k_ref, v_ref, ck_ref, o_ref, lse_ref, acc, m_s, l_s, a_s, s_scr, p_scr):
        qi = pl.program_id(0)
        ki = pl.program_id(1)

        @pl.when(ki == 0)
        def _():
            acc[...] = jnp.zeros_like(acc)
            m_s[...] = jnp.full_like(m_s, NEG)
            l_s[...] = jnp.zeros_like(l_s)

        def block(masked):
            qlo = _lane_lo((bq, 128))
            for p in range(4):
                sl = slice(128 * p, 128 * p + 128)
                qp = q_ref[:, sl] * ATT_SCALE
                vp = v_ref[:, sl]
                ks = _masked_halves(k_ref[:, sl])
                pvs = []
                for j in range(2):
                    h = 2 * p + j
                    s_scr[j] = _dot_nt(qp, ks[j])

                    def chunk(r, carry, h=h, j=j):
                        r0 = _row_start(r, R)
                        rows = pl.ds(r0, R)
                        sc = [s_scr[j, rows, 128 * g:128 * g + 128] - ck_ref[h:h + 1, 128 * g:128 * g + 128]
                              for g in range(ng)]
                        if masked:
                            rid = lax.broadcasted_iota(I32, (R, 128), 0) + r0
                            cid = lax.broadcasted_iota(I32, (R, 128), 1)
                            sc = [jnp.where(cid + 128 * g <= rid, sc[g], NEG) for g in range(ng)]
                        m_old = m_s[h, rows, :]
                        m_new = jnp.maximum(m_old, jnp.max(_tree(jnp.maximum, sc), axis=-1, keepdims=True))
                        alpha = jnp.exp(m_old - m_new)
                        pe = [jnp.exp(sc[g] - m_new) for g in range(ng)]
                        l_s[h, rows, :] = alpha * l_s[h, rows, :] + _tree(jnp.add, pe)
                        m_s[h, rows, :] = m_new
                        a_s[j, rows, :] = alpha
                        for g in range(ng):
                            p_scr[j, rows, 128 * g:128 * g + 128] = pe[g].astype(BF16)
                        return carry

                    _chunk_loop(bq // R, chunk)
                    pvs.append(_dot_nn(p_scr[j], vp))
                acc[:, sl] = jnp.where(qlo, a_s[0], a_s[1]) * acc[:, sl] + jnp.where(qlo, pvs[0], pvs[1])

        @pl.when(ki < qi)
        def _():
            block(False)

        @pl.when(ki == qi)
        def _():
            block(True)

        @pl.when(ki == nq - 1)
        def _():
            qlo = _lane_lo((bq, 128))
            lse = jnp.zeros((bq, 128), F32)
            for p in range(4):
                sl = slice(128 * p, 128 * p + 128)
                l0 = jnp.sum(l_s[2 * p], axis=-1, keepdims=True)
                l1 = jnp.sum(l_s[2 * p + 1], axis=-1, keepdims=True)
                o_ref[:, sl] = (acc[:, sl] / jnp.where(qlo, l0, l1)).astype(BF16)
                lse = lse + _put_col((bq, 128), 2 * p, m_s[2 * p][:, 0:1] + jnp.log(l0))
                lse = lse + _put_col((bq, 128), 2 * p + 1, m_s[2 * p + 1][:, 0:1] + jnp.log(l1))
            lse_ref[...] = lse

    return pl.pallas_call(
        body, name=name, grid=(nq, nq),
        in_specs=[pl.BlockSpec((bq, BRANCH), lambda i, k: (i, CB_FQ)),
                  pl.BlockSpec((bk, BRANCH), lambda i, k: (jnp.minimum(k, i), CB_FK)),
                  pl.BlockSpec((bk, BRANCH), lambda i, k: (jnp.minimum(k, i), CB_FV)),
                  pl.BlockSpec((8, bk), lambda i, k: (0, jnp.minimum(k, i)))],
        out_specs=[pl.BlockSpec((bq, BRANCH), lambda i, k: (i, 0)),
                   pl.BlockSpec((bq, 128), lambda i, k: (i, 0))],
        out_shape=[jax.ShapeDtypeStruct((T, BRANCH), BF16), jax.ShapeDtypeStruct((T, 128), F32)],
        scratch_shapes=[pltpu.VMEM((bq, BRANCH), F32), pltpu.VMEM((8, bq, 128), F32),
                        pltpu.VMEM((8, bq, 128), F32), pltpu.VMEM((2, bq, 128), F32),
                        pltpu.VMEM((2, bq, bk), F32), pltpu.VMEM((2, bq, bk), BF16)],
        compiler_params=_cp(("parallel", "arbitrary")),
    )(pm, pm, pm, c_row)


def _fox2_bwd_dq(pm, do, c_row, lse, delta, name):
    T = pm.shape[0]
    bq = _pick(T, (512, 256))
    bk = bq
    nq = T // bq
    R = FOX_ROWS
    ng = bk // 128

    def body(q_ref, k_ref, v_ref, do_ref, ck_ref, lse_ref, dl_ref, dq_ref, dl2_ref,
             acc, e_s, s_scr, dp_scr, ds_scr):
        qi = pl.program_id(0)
        ki = pl.program_id(1)

        @pl.when(ki == 0)
        def _():
            acc[...] = jnp.zeros_like(acc)
            e_s[...] = jnp.zeros_like(e_s)

        def block(masked):
            qlo = _lane_lo((bq, 128))
            for p in range(4):
                sl = slice(128 * p, 128 * p + 128)
                qp = q_ref[:, sl] * ATT_SCALE
                kp = k_ref[:, sl]
                dop = do_ref[:, sl]
                ks = _masked_halves(kp)
                vs = _masked_halves(v_ref[:, sl])
                dqs = []
                for j in range(2):
                    h = 2 * p + j
                    s_scr[...] = _dot_nt(qp, ks[j])
                    dp_scr[...] = _dot_nt(dop, vs[j])

                    def chunk(r, carry, h=h):
                        r0 = _row_start(r, R)
                        rows = pl.ds(r0, R)
                        lse_c = lse_ref[rows, h:h + 1]
                        dl_c = dl_ref[rows, h:h + 1]
                        if masked:
                            rid = lax.broadcasted_iota(I32, (R, 128), 0) + r0
                            cid = lax.broadcasted_iota(I32, (R, 128), 1)
                        dss = []
                        for g in range(ng):
                            gs = slice(128 * g, 128 * g + 128)
                            sc = s_scr[rows, gs] - ck_ref[h:h + 1, gs]
                            if masked:
                                sc = jnp.where(cid + 128 * g <= rid, sc, NEG)
                            ds = jnp.exp(sc - lse_c) * (dp_scr[rows, gs] - dl_c)
                            ds_scr[rows, gs] = ds.astype(BF16)
                            dss.append(ds)
                        e_s[h, rows, :] += _tree(jnp.add, dss)
                        return carry

                    _chunk_loop(bq // R, chunk)
                    dqs.append(_dot_nn(ds_scr[...], kp))
                acc[:, sl] += jnp.where(qlo, dqs[0], dqs[1])

        @pl.when(ki < qi)
        def _():
            block(False)

        @pl.when(ki == qi)
        def _():
            block(True)

        @pl.when(ki == nq - 1)
        def _():
            dq_ref[...] = (acc[...] * ATT_SCALE).astype(BF16)
            out = dl_ref[...]
            for h in range(8):
                out = out + _put_col((bq, 128), h, jnp.sum(e_s[h], axis=-1, keepdims=True))
            dl2_ref[...] = out

    qb = pl.BlockSpec((bq, 128), lambda i, k: (i, 0))
    return pl.pallas_call(
        body, name=name, grid=(nq, nq),
        in_specs=[pl.BlockSpec((bq, BRANCH), lambda i, k: (i, CB_FQ)),
                  pl.BlockSpec((bk, BRANCH), lambda i, k: (jnp.minimum(k, i), CB_FK)),
                  pl.BlockSpec((bk, BRANCH), lambda i, k: (jnp.minimum(k, i), CB_FV)),
                  pl.BlockSpec((bq, BRANCH), lambda i, k: (i, 0)),
                  pl.BlockSpec((8, bk), lambda i, k: (0, jnp.minimum(k, i))), qb, qb],
        out_specs=[pl.BlockSpec((bq, BRANCH), lambda i, k: (i, 0)), qb],
        out_shape=[jax.ShapeDtypeStruct((T, BRANCH), BF16), jax.ShapeDtypeStruct((T, 128), F32)],
        scratch_shapes=[pltpu.VMEM((bq, BRANCH), F32), pltpu.VMEM((8, bq, 128), F32),
                        pltpu.VMEM((bq, bk), F32), pltpu.VMEM((bq, bk), F32), pltpu.VMEM((bq, bk), BF16)],
        compiler_params=_cp(("parallel", "arbitrary")),
    )(pm, pm, pm, do, c_row, lse, delta)


def _fox2_bwd_dkv(pm, do, c_col, lse_row, delta_row, name):
    T = pm.shape[0]
    bk = _pick(T, (512, 256))
    bq = bk
    nk = T // bk
    R = FOX_ROWS
    ng = bq // 128

    def body(q_ref, k_ref, v_ref, do_ref, ck_ref, lse_ref, dl_ref, dk_ref, dv_ref, dc_ref,
             dk_acc, dv_acc, dc_s, st_scr, dpt_scr, pt_scr, dst_scr):
        ki = pl.program_id(0)
        qi = pl.program_id(1)

        @pl.when(qi == 0)
        def _():
            dk_acc[...] = jnp.zeros_like(dk_acc)
            dv_acc[...] = jnp.zeros_like(dv_acc)
            dc_s[...] = jnp.zeros_like(dc_s)

        def block(masked):
            klo = _lane_lo((bk, 128))
            for p in range(4):
                sl = slice(128 * p, 128 * p + 128)
                qp = q_ref[:, sl]
                kp = k_ref[:, sl] * ATT_SCALE
                vp = v_ref[:, sl]
                dop = do_ref[:, sl]
                qs = _masked_halves(qp)
                dos = _masked_halves(dop)
                dks, dvs = [], []
                for j in range(2):
                    h = 2 * p + j
                    st_scr[...] = _dot_nt(kp, qs[j])
                    dpt_scr[...] = _dot_nt(vp, dos[j])

                    def chunk(r, carry, h=h):
                        r0 = _row_start(r, R)
                        rows = pl.ds(r0, R)
                        ck_c = ck_ref[rows, h:h + 1]
                        if masked:
                            kid = lax.broadcasted_iota(I32, (R, 128), 0) + r0
                            qid = lax.broadcasted_iota(I32, (R, 128), 1)
                        dss = []
                        for g in range(ng):
                            gs = slice(128 * g, 128 * g + 128)
                            st = st_scr[rows, gs] - (ck_c + lse_ref[h:h + 1, gs])
                            if masked:
                                st = jnp.where(kid <= qid + 128 * g, st, NEG)
                            pt = jnp.exp(st)
                            dst = pt * (dpt_scr[rows, gs] - dl_ref[h:h + 1, gs])
                            pt_scr[rows, gs] = pt.astype(BF16)
                            dst_scr[rows, gs] = dst.astype(BF16)
                            dss.append(dst)
                        dc_s[h, rows, :] -= _tree(jnp.add, dss)
                        return carry

                    _chunk_loop(bk // R, chunk)
                    dvs.append(_dot_nn(pt_scr[...], dop))
                    dks.append(_dot_nn(dst_scr[...], qp))
                dk_acc[:, sl] += jnp.where(klo, dks[0], dks[1])
                dv_acc[:, sl] += jnp.where(klo, dvs[0], dvs[1])

        @pl.when(qi > ki)
        def _():
            block(False)

        @pl.when(qi == ki)
        def _():
            block(True)

        @pl.when(qi == nk - 1)
        def _():
            dk_ref[...] = (dk_acc[...] * ATT_SCALE).astype(BF16)
            dv_ref[...] = dv_acc[...].astype(BF16)
            out = jnp.zeros((bk, 128), F32)
            for h in range(8):
                out = out + _put_col((bk, 128), h, jnp.sum(dc_s[h], axis=-1, keepdims=True))
            dc_ref[...] = out

    qrow = pl.BlockSpec((8, bq), lambda k, i: (0, jnp.maximum(i, k)))
    kb = pl.BlockSpec((bk, BRANCH), lambda k, i: (k, 0))
    return pl.pallas_call(
        body, name=name, grid=(nk, nk),
        in_specs=[pl.BlockSpec((bq, BRANCH), lambda k, i: (jnp.maximum(i, k), CB_FQ)),
                  pl.BlockSpec((bk, BRANCH), lambda k, i: (k, CB_FK)),
                  pl.BlockSpec((bk, BRANCH), lambda k, i: (k, CB_FV)),
                  pl.BlockSpec((bq, BRANCH), lambda k, i: (jnp.maximum(i, k), 0)),
                  pl.BlockSpec((bk, 128), lambda k, i: (k, 0)), qrow, qrow],
        out_specs=[kb, kb, pl.BlockSpec((bk, 128), lambda k, i: (k, 0))],
        out_shape=[jax.ShapeDtypeStruct((T, BRANCH), BF16), jax.ShapeDtypeStruct((T, BRANCH), BF16),
                   jax.ShapeDtypeStruct((T, 128), F32)],
        scratch_shapes=[pltpu.VMEM((bk, BRANCH), F32), pltpu.VMEM((bk, BRANCH), F32),
                        pltpu.VMEM((8, bk, 128), F32), pltpu.VMEM((bk, bq), F32), pltpu.VMEM((bk, bq), F32),
                        pltpu.VMEM((bk, bq), BF16), pltpu.VMEM((bk, bq), BF16)],
        compiler_params=_cp(("parallel", "arbitrary")),
    )(pm, pm, pm, do, c_col, lse_row, delta_row)


def _bucket_table():
    tq = np.arange(WINDOW, dtype=np.int32)[:, None]
    sk = np.arange(2 * WINDOW, dtype=np.int32)[None, :]
    n = np.maximum(WINDOW + tq - sk, 0)
    max_exact = N_BUCKETS // 2
    ratio = np.maximum(n, 1).astype(np.float32) / np.float32(max_exact)
    large = max_exact + (np.log(ratio) / np.float32(math.log(WINDOW / max_exact))
                         * np.float32(N_BUCKETS - max_exact)).astype(np.int32)
    large = np.minimum(large, N_BUCKETS - 1)
    return np.where(n < max_exact, n, large).astype(np.int32)


def _swa_bias(rel_bias, bucket, name):
    def body(rb_ref, bk_ref, o_ref):
        bkt = bk_ref[...]
        for h in range(8):
            def step(b, a):
                return a + jnp.where(bkt == b, rb_ref[b, h], 0.0)
            o_ref[h] = lax.fori_loop(0, N_BUCKETS, step, jnp.zeros(bkt.shape, F32))

    return pl.pallas_call(
        body, name=name,
        in_specs=[pl.BlockSpec(memory_space=pltpu.SMEM), pl.BlockSpec(memory_space=pltpu.VMEM)],
        out_specs=pl.BlockSpec(memory_space=pltpu.VMEM),
        out_shape=jax.ShapeDtypeStruct((8, WINDOW, 2 * WINDOW), F32),
    )(rel_bias, bucket)


def _swa_dbias_reduce(dbias, bucket, name):
    def body(d_ref, bk_ref, o_ref):
        bkt = bk_ref[...]
        rowi = lax.broadcasted_iota(I32, (N_BUCKETS, 128), 0)
        lane = lax.broadcasted_iota(I32, (N_BUCKETS, 128), 1)
        out = jnp.zeros((N_BUCKETS, 128), F32)
        for h in range(8):
            dv = d_ref[h]

            def step(b, a):
                tot = jnp.sum(jnp.where(bkt == b, dv, 0.0), keepdims=True)
                return a + jnp.where((rowi == b) & (lane == h), tot, 0.0)
            out = lax.fori_loop(0, N_BUCKETS, step, out)
        o_ref[...] = out

    return pl.pallas_call(
        body, name=name,
        in_specs=[pl.BlockSpec(memory_space=pltpu.VMEM), pl.BlockSpec(memory_space=pltpu.VMEM)],
        out_specs=pl.BlockSpec(memory_space=pltpu.VMEM),
        out_shape=jax.ShapeDtypeStruct((N_BUCKETS, 128), F32),
    )(dbias, bucket)


def _swap_halves(x):
    return pltpu.roll(x.astype(F32), HEAD_DIM, 1).astype(x.dtype)


def _kv_variants(t):
    lo = _lane_lo(t.shape)
    z = jnp.zeros_like(t)
    a0 = jnp.where(lo, t, z)
    b1 = jnp.where(lo, z, t)
    b0 = _swap_halves(a0)
    a1 = _swap_halves(b1)
    return (a0, a1), (b0, b1), (a0 + b0, a1 + b1)


def _swa_masks(i):
    tq = lax.broadcasted_iota(I32, (WINDOW, WINDOW), 0)
    jj = lax.broadcasted_iota(I32, (WINDOW, WINDOW), 1)
    return (jj > tq) & (i > 0), jj <= tq


def _swa_specs():
    q = pl.BlockSpec((WINDOW, BRANCH), lambda i: (i, CB_SQ))
    kc = pl.BlockSpec((WINDOW, 128), lambda i: (i, CB_SK))
    kp = pl.BlockSpec((WINDOW, 128), lambda i: (jnp.maximum(i - 1, 0), CB_SK))
    vc = pl.BlockSpec((WINDOW, 128), lambda i: (i, CB_SV))
    vp = pl.BlockSpec((WINDOW, 128), lambda i: (jnp.maximum(i - 1, 0), CB_SV))
    bias = pl.BlockSpec((8, WINDOW, 2 * WINDOW), lambda i: (0, 0, 0))
    vec = pl.BlockSpec((1, 128), lambda i: (0, 0))
    return q, kc, kp, vc, vp, bias, vec


def _swa_fwd(pm, bias, sink, name):
    T = pm.shape[0]
    nb = T // WINDOW

    def body(q_ref, kc_ref, kp_ref, vc_ref, vp_ref, b_ref, s_ref, o_ref, m_ref):
        i = pl.program_id(0)
        mprev, mcur = _swa_masks(i)
        kcA, kcB, _ = _kv_variants(kc_ref[...])
        kpA, kpB, _ = _kv_variants(kp_ref[...])
        _, _, vcD = _kv_variants(vc_ref[...])
        _, _, vpD = _kv_variants(vp_ref[...])
        lo = _lane_lo((WINDOW, 128))
        sink_v = s_ref[...]
        mout = jnp.zeros((WINDOW, 128), F32)
        for p in range(4):
            jv = p // 2
            sl = slice(128 * p, 128 * p + 128)
            qp = q_ref[:, sl] * ATT_SCALE
            outs = []
            for par in range(2):
                h = 2 * p + par
                kpx = (kpA, kpB)[par][jv]
                kcx = (kcA, kcB)[par][jv]
                sp = jnp.where(mprev, _dot_nt(qp, kpx) + b_ref[h, :, 0:WINDOW], NEG)
                sc = jnp.where(mcur, _dot_nt(qp, kcx) + b_ref[h, :, WINDOW:2 * WINDOW], NEG)
                sk_h = sink_v[:, h:h + 1]
                m = jnp.maximum(jnp.maximum(jnp.max(sp, axis=-1, keepdims=True),
                                            jnp.max(sc, axis=-1, keepdims=True)), sk_h)
                ep = jnp.exp(sp - m)
                ec = jnp.exp(sc - m)
                den = (jnp.sum(ep, axis=-1, keepdims=True) + jnp.sum(ec, axis=-1, keepdims=True)
                       + jnp.exp(sk_h - m))
                inv = 1.0 / den
                outs.append(_dot_nn((ep * inv).astype(BF16), vpD[jv])
                            + _dot_nn((ec * inv).astype(BF16), vcD[jv]))
                mout = mout + _put_col((WINDOW, 128), h, m + jnp.log(den))
            o_ref[:, sl] = jnp.where(lo, outs[0], outs[1]).astype(BF16)
        m_ref[...] = mout

    q, kc, kp, vc, vp, bs, vec = _swa_specs()
    return pl.pallas_call(
        body, name=name, grid=(nb,),
        in_specs=[q, kc, kp, vc, vp, bs, vec],
        out_specs=[pl.BlockSpec((WINDOW, BRANCH), lambda i: (i, 0)),
                   pl.BlockSpec((WINDOW, 128), lambda i: (i, 0))],
        out_shape=[jax.ShapeDtypeStruct((T, BRANCH), BF16), jax.ShapeDtypeStruct((T, 128), F32)],
        compiler_params=_cp(("parallel",)),
    )(pm, pm, pm, pm, pm, bias, sink)


def _swa_bwd(pm, bias, sink, do, mlse, name):
    T = pm.shape[0]
    nb = T // WINDOW

    def fold(zz):
        return zz + pltpu.roll(zz, HEAD_DIM, 1)

    def body(q_ref, kc_ref, kp_ref, vc_ref, vp_ref, b_ref, s_ref, do_ref, m_ref,
             dq_ref, dkc_ref, dkp_ref, dvc_ref, dvp_ref, db_ref, ds_ref):
        i = pl.program_id(0)

        @pl.when(i == 0)
        def _():
            db_ref[...] = jnp.zeros_like(db_ref)
            ds_ref[...] = jnp.zeros_like(ds_ref)

        mprev, mcur = _swa_masks(i)
        kcA, kcB, kcD = _kv_variants(kc_ref[...])
        kpA, kpB, kpD = _kv_variants(kp_ref[...])
        vcA, vcB, _ = _kv_variants(vc_ref[...])
        vpA, vpB, _ = _kv_variants(vp_ref[...])
        lo = _lane_lo((WINDOW, 128))
        sink_v = s_ref[...]
        mv = m_ref[...]
        zk = jnp.zeros((WINDOW, 128), F32)
        zkp, zkc, zvp, zvc = [zk, zk], [zk, zk], [zk, zk], [zk, zk]
        dsink = jnp.zeros((1, 128), F32)
        for p in range(4):
            jv = p // 2
            sl = slice(128 * p, 128 * p + 128)
            qraw = q_ref[:, sl]
            qp = qraw * ATT_SCALE
            dop = do_ref[:, sl]
            dqs, mkp, mkc, mvp, mvc = [], [], [], [], []
            for par in range(2):
                h = 2 * p + par
                kpx = (kpA, kpB)[par][jv]
                kcx = (kcA, kcB)[par][jv]
                vpx = (vpA, vpB)[par][jv]
                vcx = (vcA, vcB)[par][jv]
                sp = jnp.where(mprev, _dot_nt(qp, kpx) + b_ref[h, :, 0:WINDOW], NEG)
                sc = jnp.where(mcur, _dot_nt(qp, kcx) + b_ref[h, :, WINDOW:2 * WINDOW], NEG)
                m_h = mv[:, h:h + 1]
                pp = jnp.exp(sp - m_h)
                pc = jnp.exp(sc - m_h)
                psink = jnp.exp(sink_v[:, h:h + 1] - m_h)
                dpp = _dot_nt(dop, vpx)
                dpc = _dot_nt(dop, vcx)
                delta = jnp.sum(pp * dpp, axis=-1, keepdims=True) + jnp.sum(pc * dpc, axis=-1, keepdims=True)
                dsp = pp * (dpp - delta)
                dsc = pc * (dpc - delta)
                db_ref[h, :, 0:WINDOW] += dsp
                db_ref[h, :, WINDOW:2 * WINDOW] += dsc
                dsink = dsink - _put_col((1, 128), h, jnp.sum(psink * delta, keepdims=True))
                dsp_b = dsp.astype(BF16)
                dsc_b = dsc.astype(BF16)
                dqs.append(_dot_nn(dsp_b, kpD[jv]) + _dot_nn(dsc_b, kcD[jv]))
                mkp.append(_dot_tn(dsp_b, qraw))
                mkc.append(_dot_tn(dsc_b, qraw))
                mvp.append(_dot_tn(pp.astype(BF16), dop))
                mvc.append(_dot_tn(pc.astype(BF16), dop))
            dq_ref[:, sl] = (jnp.where(lo, dqs[0], dqs[1]) * ATT_SCALE).astype(BF16)
            zkp[jv] = zkp[jv] + jnp.where(lo, mkp[0], mkp[1])
            zkc[jv] = zkc[jv] + jnp.where(lo, mkc[0], mkc[1])
            zvp[jv] = zvp[jv] + jnp.where(lo, mvp[0], mvp[1])
            zvc[jv] = zvc[jv] + jnp.where(lo, mvc[0], mvc[1])
        dkc_ref[...] = jnp.where(lo, fold(zkc[0]), fold(zkc[1])) * ATT_SCALE
        dkp_ref[...] = jnp.where(lo, fold(zkp[0]), fold(zkp[1])) * ATT_SCALE
        dvc_ref[...] = jnp.where(lo, fold(zvc[0]), fold(zvc[1]))
        dvp_ref[...] = jnp.where(lo, fold(zvp[0]), fold(zvp[1]))
        ds_ref[...] += dsink

    q, kc, kp, vc, vp, bs, vec = _swa_specs()
    own = pl.BlockSpec((WINDOW, BRANCH), lambda i: (i, 0))
    sm = pl.BlockSpec((WINDOW, 128), lambda i: (i, 0))
    f128 = jax.ShapeDtypeStruct((T, 128), F32)
    return pl.pallas_call(
        body, name=name, grid=(nb,),
        in_specs=[q, kc, kp, vc, vp, bs, vec, own, sm],
        out_specs=[own, sm, sm, sm, sm, bs, vec],
        out_shape=[jax.ShapeDtypeStruct((T, BRANCH), BF16), f128, f128, f128, f128,
                   jax.ShapeDtypeStruct((8, WINDOW, 2 * WINDOW), F32), jax.ShapeDtypeStruct((1, 128), F32)],
        compiler_params=_cp(("arbitrary",)),
    )(pm, pm, pm, pm, pm, bias, sink, do, mlse)


def _merge_fwd(pm, us, name):
    T = pm.shape[0]
    bt = _pick(T, (512, 256))

    def body(g0, g1, g2, u0, u1, u2, o_ref):
        acc = jax.nn.sigmoid(g0[...].astype(F32)) * u0[...].astype(F32)
        acc = acc + jax.nn.sigmoid(g1[...].astype(F32)) * u1[...].astype(F32)
        acc = acc + jax.nn.sigmoid(g2[...].astype(F32)) * u2[...].astype(F32)
        o_ref[...] = acc.astype(BF16)

    own = pl.BlockSpec((bt, D_MODEL), lambda i: (i, 0))
    gs = [pl.BlockSpec((bt, D_MODEL), lambda i, cb=cb: (i, cb)) for cb in CB_GATE]
    return pl.pallas_call(
        body, name=name, grid=(T // bt,), in_specs=gs + [own, own, own], out_specs=own,
        out_shape=jax.ShapeDtypeStruct((T, D_MODEL), BF16),
        compiler_params=_cp(("parallel",)),
    )(pm, pm, pm, *us)


def _merge_bwd(pm, us, dm, name):
    T = pm.shape[0]
    bt = _pick(T, (256,))

    def body(g0, g1, g2, u0, u1, u2, dm_ref, du0, du1, du2, dg_ref):
        dmv = dm_ref[...].astype(F32)
        for b, (g, u, du) in enumerate(((g0, u0, du0), (g1, u1, du1), (g2, u2, du2))):
            s = jax.nn.sigmoid(g[...].astype(F32))
            du[...] = (dmv * s).astype(BF16)
            dg_ref[:, D_MODEL * b:D_MODEL * (b + 1)] = (dmv * u[...].astype(F32) * s * (1.0 - s)).astype(BF16)

    own = pl.BlockSpec((bt, D_MODEL), lambda i: (i, 0))
    gs = [pl.BlockSpec((bt, D_MODEL), lambda i, cb=cb: (i, cb)) for cb in CB_GATE]
    act = jax.ShapeDtypeStruct((T, D_MODEL), BF16)
    return pl.pallas_call(
        body, name=name, grid=(T // bt,), in_specs=gs + [own, own, own, own],
        out_specs=[own, own, own, pl.BlockSpec((bt, 3 * D_MODEL), lambda i: (i, 0))],
        out_shape=[act, act, act, jax.ShapeDtypeStruct((T, 3 * D_MODEL), BF16)],
        compiler_params=_cp(("parallel",)),
    )(pm, pm, pm, *us, dm)


def _swiglu_fwd(ab, name):
    T = ab.shape[0]
    bt = _pick(T, (512, 256))

    def body(a_ref, b_ref, o_ref):
        a = a_ref[...].astype(F32)
        o_ref[...] = (a * jax.nn.sigmoid(a) * b_ref[...].astype(F32)).astype(BF16)

    return pl.pallas_call(
        body, name=name, grid=(T // bt,),
        in_specs=[pl.BlockSpec((bt, D_FF), lambda i: (i, 0)), pl.BlockSpec((bt, D_FF), lambda i: (i, 1))],
        out_specs=pl.BlockSpec((bt, D_FF), lambda i: (i, 0)),
        out_shape=jax.ShapeDtypeStruct((T, D_FF), BF16),
        compiler_params=_cp(("parallel",)),
    )(ab, ab)


def _swiglu_bwd(ab, dh, name):
    T = ab.shape[0]
    bt = _pick(T, (256,))

    def body(a_ref, b_ref, d_ref, o_ref):
        a = a_ref[...].astype(F32)
        b = b_ref[...].astype(F32)
        d = d_ref[...].astype(F32)
        s = jax.nn.sigmoid(a)
        o_ref[:, 0:D_FF] = (d * b * (s + a * s * (1.0 - s))).astype(BF16)
        o_ref[:, D_FF:2 * D_FF] = (d * a * s).astype(BF16)

    return pl.pallas_call(
        body, name=name, grid=(T // bt,),
        in_specs=[pl.BlockSpec((bt, D_FF), lambda i: (i, 0)), pl.BlockSpec((bt, D_FF), lambda i: (i, 1)),
                  pl.BlockSpec((bt, D_FF), lambda i: (i, 0))],
        out_specs=pl.BlockSpec((bt, 2 * D_FF), lambda i: (i, 0)),
        out_shape=jax.ShapeDtypeStruct((T, 2 * D_FF), BF16),
        compiler_params=_cp(("parallel",)),
    )(ab, ab, dh)


def _xattn_probs(q_ref, kv_ref, h):
    sl = slice(X_HEAD_DIM * h, X_HEAD_DIM * (h + 1))
    qh = q_ref[:, sl]
    kh = kv_ref[:, sl]
    vh = kv_ref[:, D_MODEL + X_HEAD_DIM * h:D_MODEL + X_HEAD_DIM * (h + 1)]
    s = _dot_nt(qh, kh) * X_SCALE
    e = jnp.exp(s - jnp.max(s, axis=-1, keepdims=True))
    return qh, kh, vh, e * (1.0 / jnp.sum(e, axis=-1, keepdims=True))


def _xattn_fwd(q, kv, name):
    T = q.shape[0]
    bq = _pick(T, (512, 256))

    def body(q_ref, kv_ref, o_ref):
        for h in range(X_HEADS):
            _, _, vh, p = _xattn_probs(q_ref, kv_ref, h)
            o_ref[:, X_HEAD_DIM * h:X_HEAD_DIM * (h + 1)] = _dot_nn(p.astype(BF16), vh).astype(BF16)

    own = pl.BlockSpec((bq, D_MODEL), lambda i: (i, 0))
    return pl.pallas_call(
        body, name=name, grid=(T // bq,),
        in_specs=[own, pl.BlockSpec((MEM_LEN, 2 * D_MODEL), lambda i: (0, 0))], out_specs=own,
        out_shape=jax.ShapeDtypeStruct((T, D_MODEL), BF16),
        compiler_params=_cp(("parallel",)),
    )(q, kv)


def _xattn_bwd(q, kv, do, name):
    T = q.shape[0]
    bq = _pick(T, (512, 256))

    def body(q_ref, kv_ref, do_ref, dq_ref, dkv_ref):
        @pl.when(pl.program_id(0) == 0)
        def _():
            dkv_ref[...] = jnp.zeros_like(dkv_ref)

        for h in range(X_HEADS):
            sl = slice(X_HEAD_DIM * h, X_HEAD_DIM * (h + 1))
            qh, kh, vh, p = _xattn_probs(q_ref, kv_ref, h)
            doh = do_ref[:, sl]
            dp = _dot_nt(doh, vh)
            ds = (p * (dp - jnp.sum(p * dp, axis=-1, keepdims=True)) * X_SCALE).astype(BF16)
            dq_ref[:, sl] = _dot_nn(ds, kh).astype(BF16)
            dkv_ref[:, sl] += _dot_tn(ds, qh)
            dkv_ref[:, D_MODEL + X_HEAD_DIM * h:D_MODEL + X_HEAD_DIM * (h + 1)] += _dot_tn(p.astype(BF16), doh)

    own = pl.BlockSpec((bq, D_MODEL), lambda i: (i, 0))
    kvs = pl.BlockSpec((MEM_LEN, 2 * D_MODEL), lambda i: (0, 0))
    return pl.pallas_call(
        body, name=name, grid=(T // bq,), in_specs=[own, kvs, own], out_specs=[own, kvs],
        out_shape=[jax.ShapeDtypeStruct((T, D_MODEL), BF16), jax.ShapeDtypeStruct((MEM_LEN, 2 * D_MODEL), F32)],
        compiler_params=_cp(("arbitrary",)),
    )(q, kv, do)


def _adamw(w, g, m, v, name):
    R, C = w.shape
    cpad = -(-C // 128) * 128
    bt = R
    for cand in (1024, 512, 256, 128, 64, 32, 16, 8):
        if R % cand == 0 and cand * cpad * 4 <= (1 << 20):
            bt = cand
            break

    def body(w_ref, g_ref, m_ref, v_ref, d_ref, nm_ref, nv_ref):
        gv = g_ref[...]
        mn = ADAM_B1 * m_ref[...] + (1.0 - ADAM_B1) * gv
        vn = ADAM_B2 * v_ref[...] + (1.0 - ADAM_B2) * (gv * gv)
        m_hat = mn / (1.0 - ADAM_B1 ** ADAM_STEP)
        v_hat = vn / (1.0 - ADAM_B2 ** ADAM_STEP)
        d_ref[...] = -ADAM_LR * (m_hat / (jnp.sqrt(v_hat) + ADAM_EPS) + ADAM_WD * w_ref[...])
        nm_ref[...] = mn
        nv_ref[...] = vn

    blk = pl.BlockSpec((bt, C), lambda i: (i, 0))
    out = jax.ShapeDtypeStruct((R, C), F32)
    return pl.pallas_call(
        body, name=name, grid=(R // bt,), in_specs=[blk] * 4, out_specs=[blk] * 3,
        out_shape=[out, out, out], compiler_params=_cp(("parallel",)),
    )(w, g, m, v)


ANY = pl.BlockSpec(memory_space=pl.ANY)


def _place():
    x, y, c = lax.axis_index("x"), lax.axis_index("y"), lax.axis_index("c")
    chips = [(1 - x, y), (x, 1 - y), (1 - x, 1 - y)]
    return x, y, c, chips


def _ag_packs(pack):
    R, Wd = pack.shape
    hrows = R // 2

    def body(p_ref, o_ref, send_sems, recv_sems, local_sem):
        x, y, c, chips = _place()
        me = 2 * x + y
        mine = pl.ds(c * hrows, hrows)
        theirs = pl.ds((1 - c) * hrows, hrows)
        local = pltpu.make_async_copy(p_ref, o_ref.at[me], local_sem)
        local.start()

        def copy(k, slab, rows, to, src=None):
            dst = o_ref.at[slab, rows]
            return pltpu.make_async_remote_copy(
                src_ref=dst if src is None else src, dst_ref=dst,
                send_sem=send_sems.at[k], recv_sem=recv_sems.at[k], device_id=to, device_id_type=MESH)

        first = [copy(k, me, mine, (px, py, c), src=p_ref.at[mine]) for k, (px, py) in enumerate(chips)]
        for cp in first:
            cp.start()
        passed = [copy(3 + k, 2 * px + py, mine, (x, y, 1 - c)) for k, (px, py) in enumerate(chips)]
        for k, (px, py) in enumerate(chips):
            copy(k, 2 * px + py, mine, (x, y, c)).wait_recv()
            passed[k].start()
        for k, (px, py) in enumerate(chips):
            copy(3 + k, 2 * px + py, theirs, (x, y, c)).wait_recv()
        for cp in first + passed:
            cp.wait_send()
        local.wait()

    return pl.pallas_call(
        body, name="ag_weights", in_specs=[ANY], out_specs=ANY,
        out_shape=jax.ShapeDtypeStruct((4, R, Wd), pack.dtype),
        scratch_shapes=[pltpu.SemaphoreType.DMA((6,)), pltpu.SemaphoreType.DMA((6,)), pltpu.SemaphoreType.DMA],
    )(pack)


def _rs_sibling(g4):
    _, R, Wd = g4.shape
    hrows = R // 2

    def body(g_ref, o_ref, send_sem, recv_sem):
        x, y, c, _ = _place()
        cp = pltpu.make_async_remote_copy(
            src_ref=g_ref.at[:, pl.ds((1 - c) * hrows, hrows)], dst_ref=o_ref,
            send_sem=send_sem, recv_sem=recv_sem, device_id=(x, y, 1 - c), device_id_type=MESH)
        cp.start()
        cp.wait()

    return pl.pallas_call(
        body, name="rs_sibling", in_specs=[ANY], out_specs=ANY,
        out_shape=jax.ShapeDtypeStruct((4, hrows, Wd), g4.dtype),
        scratch_shapes=[pltpu.SemaphoreType.DMA, pltpu.SemaphoreType.DMA],
    )(g4)


def _rs_add_pair(g4, sib, cidx):
    _, R, Wd = g4.shape
    hrows = R // 2
    bt = _pick(hrows, (512, 256, 16))
    nb = hrows // bt

    def body(c_ref, a_ref, b_ref, o_ref):
        o_ref[...] = (a_ref[...].astype(F32) + b_ref[...].astype(F32)).astype(o_ref.dtype)

    grid_spec = pltpu.PrefetchScalarGridSpec(
        num_scalar_prefetch=1, grid=(4, nb),
        in_specs=[pl.BlockSpec((1, bt, Wd), lambda j, i, c: (j, c[0] * nb + i, 0)),
                  pl.BlockSpec((1, bt, Wd), lambda j, i, c: (j, i, 0))],
        out_specs=pl.BlockSpec((1, bt, Wd), lambda j, i, c: (j, i, 0)))
    return pl.pallas_call(
        body, name="rs_add_pair", grid_spec=grid_spec,
        out_shape=jax.ShapeDtypeStruct((4, hrows, Wd), g4.dtype),
        compiler_params=_cp(("parallel", "parallel")),
    )(cidx, g4, sib)


def _rs_chips(r4):
    _, hrows, Wd = r4.shape

    def body(r_ref, o_ref, send_sems, recv_sems, local_sem):
        x, y, c, chips = _place()
        me = 2 * x + y
        local = pltpu.make_async_copy(r_ref.at[me], o_ref.at[me], local_sem)
        local.start()
        sends = []
        for k, (px, py) in enumerate(chips):
            sends.append(pltpu.make_async_remote_copy(
                src_ref=r_ref.at[2 * px + py], dst_ref=o_ref.at[me],
                send_sem=send_sems.at[k], recv_sem=recv_sems.at[k], device_id=(px, py, c), device_id_type=MESH))
        for cp in sends:
            cp.start()
        for k, (px, py) in enumerate(chips):
            pltpu.make_async_remote_copy(
                src_ref=r_ref.at[me], dst_ref=o_ref.at[2 * px + py],
                send_sem=send_sems.at[k], recv_sem=recv_sems.at[k], device_id=(x, y, c),
                device_id_type=MESH).wait_recv()
        for cp in sends:
            cp.wait_send()
        local.wait()

    return pl.pallas_call(
        body, name="rs_chips", in_specs=[ANY], out_specs=ANY,
        out_shape=jax.ShapeDtypeStruct((4, hrows, Wd), r4.dtype),
        scratch_shapes=[pltpu.SemaphoreType.DMA((3,)), pltpu.SemaphoreType.DMA((3,)), pltpu.SemaphoreType.DMA],
    )(r4)


def _rs_add_chips(q4):
    _, hrows, Wd = q4.shape
    bt = _pick(hrows, (240, 120, 16))

    def body(q_ref, o_ref):
        o_ref[...] = ((q_ref[0].astype(F32) + q_ref[1].astype(F32)) + q_ref[2].astype(F32)) + q_ref[3].astype(F32)

    return pl.pallas_call(
        body, name="rs_add_chips", grid=(hrows // bt,),
        in_specs=[pl.BlockSpec((4, bt, Wd), lambda i: (0, i, 0))],
        out_specs=pl.BlockSpec((bt, Wd), lambda i: (i, 0)),
        out_shape=jax.ShapeDtypeStruct((hrows, Wd), F32),
        compiler_params=_cp(("parallel",)),
    )(q4)


def _rs_share(half):
    hrows, Wd = half.shape

    def body(h_ref, o_ref, send_sem, recv_sem, local_sem):
        x, y, c, _ = _place()
        mine = pl.ds(c * hrows, hrows)
        local = pltpu.make_async_copy(h_ref, o_ref.at[mine], local_sem)
        local.start()
        cp = pltpu.make_async_remote_copy(
            src_ref=h_ref, dst_ref=o_ref.at[mine], send_sem=send_sem, recv_sem=recv_sem,
            device_id=(x, y, 1 - c), device_id_type=MESH)
        cp.start()
        pltpu.make_async_remote_copy(
            src_ref=h_ref, dst_ref=o_ref.at[pl.ds((1 - c) * hrows, hrows)], send_sem=send_sem,
            recv_sem=recv_sem, device_id=(x, y, c), device_id_type=MESH).wait_recv()
        cp.wait_send()
        local.wait()

    return pl.pallas_call(
        body, name="rs_share", in_specs=[ANY], out_specs=ANY,
        out_shape=jax.ShapeDtypeStruct((2 * hrows, Wd), half.dtype),
        scratch_shapes=[pltpu.SemaphoreType.DMA, pltpu.SemaphoreType.DMA, pltpu.SemaphoreType.DMA],
    )(half)


def _allreduce_small(v):
    R, Wd = v.shape

    def body(v_ref, o_ref, buf, send_sems, recv_sems):
        x, y, c, _ = _place()
        me = 4 * x + 2 * y + c
        buf[me] = v_ref[...]
        sends = []
        for k in range(1, 8):
            peer = ((x + (k >> 2)) % 2, (y + ((k >> 1) & 1)) % 2, (c + (k & 1)) % 2)
            sends.append(pltpu.make_async_remote_copy(
                src_ref=v_ref, dst_ref=buf.at[me], send_sem=send_sems.at[k - 1], recv_sem=recv_sems.at[k - 1],
                device_id=peer, device_id_type=MESH))
        for cp in sends:
            cp.start()
        for k in range(1, 8):
            px, py, pc = (x + (k >> 2)) % 2, (y + ((k >> 1) & 1)) % 2, (c + (k & 1)) % 2
            pltpu.make_async_remote_copy(
                src_ref=v_ref, dst_ref=buf.at[4 * px + 2 * py + pc], send_sem=send_sems.at[k - 1],
                recv_sem=recv_sems.at[k - 1], device_id=(x, y, c), device_id_type=MESH).wait_recv()
        acc = buf[0]
        for d in range(1, 8):
            acc = acc + buf[d]
        o_ref[...] = acc
        for cp in sends:
            cp.wait_send()

    vm = pl.BlockSpec(memory_space=pltpu.VMEM)
    return pl.pallas_call(
        body, name="allreduce_small", in_specs=[vm], out_specs=vm,
        out_shape=jax.ShapeDtypeStruct((R, Wd), F32),
        scratch_shapes=[pltpu.VMEM((8, R, Wd), F32), pltpu.SemaphoreType.DMA((7,)), pltpu.SemaphoreType.DMA((7,))],
    )(v)


def _neighbours():
    x, y, c = lax.axis_index("x"), lax.axis_index("y"), lax.axis_index("c")
    idx = (2 * x + y, 2 * (1 - x) + y, 2 * x + (1 - y), 2 * (1 - x) + (1 - y))
    return idx, (x, y, c), (1 - x, y, c), (x, 1 - y, c), (x, y, 1 - c)


def _ag_ring(pack):
    R, Wd = pack.shape
    hrows = R // 2
    qrows = hrows // 2

    def body(p_ref, o_ref, send_sems, recv_sems, local_sem):
        (me, ix, iy, idg), here, xn, yn, sib = _neighbours()
        c = here[2]
        base = c * hrows
        half = pl.ds(base, hrows)
        q0 = pl.ds(base, qrows)
        q1 = pl.ds(base + qrows, qrows)
        obase = (1 - c) * hrows
        local = pltpu.make_async_copy(p_ref, o_ref.at[me], local_sem)
        local.start()

        def copy(k, slab, rows, to, src=None):
            dst = o_ref.at[slab, rows]
            return pltpu.make_async_remote_copy(
                src_ref=dst if src is None else src, dst_ref=dst,
                send_sem=send_sems.at[k], recv_sem=recv_sems.at[k], device_id=to, device_id_type=MESH)

        sends = [copy(0, me, half, xn, src=p_ref.at[half]), copy(1, me, half, yn, src=p_ref.at[half])]
        for cp in sends:
            cp.start()
        landed = [(0, ix, half), (1, iy, half), (2, idg, q0), (3, idg, q1)]
        onward = {0: copy(2, ix, q0, yn), 1: copy(3, iy, q1, xn)}
        for k, slab, rows in landed:
            copy(k, slab, rows, here).wait_recv()
            if k in onward:
                onward[k].start()
                sends.append(onward[k])
            cp = copy(4 + k, slab, rows, sib)
            cp.start()
            sends.append(cp)
        theirs = [(4, ix, pl.ds(obase, hrows)), (5, iy, pl.ds(obase, hrows)),
                  (6, idg, pl.ds(obase, qrows)), (7, idg, pl.ds(obase + qrows, qrows))]
        for k, slab, rows in theirs:
            copy(k, slab, rows, here).wait_recv()
        for cp in sends:
            cp.wait_send()
        local.wait()

    return pl.pallas_call(
        body, name="ag_weights", in_specs=[ANY], out_specs=ANY,
        out_shape=jax.ShapeDtypeStruct((4, R, Wd), pack.dtype),
        scratch_shapes=[pltpu.SemaphoreType.DMA((8,)), pltpu.SemaphoreType.DMA((8,)), pltpu.SemaphoreType.DMA],
    )(pack)


def _rs_diag(r4):
    _, hrows, Wd = r4.shape
    qrows = hrows // 2

    def body(r_ref, o_ref, send_sems, recv_sems):
        (me, ix, iy, idg), here, xn, yn, sib = _neighbours()
        pieces = [(0, pl.ds(0, qrows), xn), (1, pl.ds(qrows, qrows), yn)]
        sends = [pltpu.make_async_remote_copy(
            src_ref=r_ref.at[idg, rows], dst_ref=o_ref.at[k], send_sem=send_sems.at[k],
            recv_sem=recv_sems.at[k], device_id=to, device_id_type=MESH) for k, rows, to in pieces]
        for cp in sends:
            cp.start()
        for k, rows, to in pieces:
            pltpu.make_async_remote_copy(
                src_ref=r_ref.at[idg, rows], dst_ref=o_ref.at[k], send_sem=send_sems.at[k],
                recv_sem=recv_sems.at[k], device_id=here, device_id_type=MESH).wait_recv()
        for cp in sends:
            cp.wait_send()

    return pl.pallas_call(
        body, name="rs_diag", in_specs=[ANY], out_specs=ANY,
        out_shape=jax.ShapeDtypeStruct((2, qrows, Wd), r4.dtype),
        scratch_shapes=[pltpu.SemaphoreType.DMA((2,)), pltpu.SemaphoreType.DMA((2,))],
    )(r4)


def _rs_merge(r4, dg, nbr_idx):
    _, hrows, Wd = r4.shape
    bt = _pick(hrows // 2, (256, 128, 16))
    nb = hrows // bt
    nq = nb // 2

    def body(i_ref, r_ref, d_ref, o_ref):
        w = pl.program_id(0)
        i = pl.program_id(1)
        merged = jnp.where(w == 0, i >= nq, i < nq)
        add = jnp.where(merged, d_ref[...].astype(F32), 0.0)
        o_ref[...] = (r_ref[...].astype(F32) + add).astype(o_ref.dtype)

    grid_spec = pltpu.PrefetchScalarGridSpec(
        num_scalar_prefetch=1, grid=(2, nb),
        in_specs=[pl.BlockSpec((1, bt, Wd), lambda w, i, idx: (idx[w], i, 0)),
                  pl.BlockSpec((1, bt, Wd), lambda w, i, idx: (1 - w, jnp.clip(i - (1 - w) * nq, 0, nq - 1), 0))],
        out_specs=pl.BlockSpec((1, bt, Wd), lambda w, i, idx: (w, i, 0)))
    return pl.pallas_call(
        body, name="rs_merge", grid_spec=grid_spec,
        out_shape=jax.ShapeDtypeStruct((2, hrows, Wd), r4.dtype),
        compiler_params=_cp(("parallel", "parallel")),
    )(nbr_idx, r4, dg)


def _rs_direct(m2):
    _, hrows, Wd = m2.shape

    def body(m_ref, o_ref, send_sems, recv_sems):
        _, here, xn, yn, sib = _neighbours()
        sends = [pltpu.make_async_remote_copy(
            src_ref=m_ref.at[k], dst_ref=o_ref.at[k], send_sem=send_sems.at[k], recv_sem=recv_sems.at[k],
            device_id=to, device_id_type=MESH) for k, to in ((0, xn), (1, yn))]
        for cp in sends:
            cp.start()
        for k in range(2):
            pltpu.make_async_remote_copy(
                src_ref=m_ref.at[k], dst_ref=o_ref.at[k], send_sem=send_sems.at[k], recv_sem=recv_sems.at[k],
                device_id=here, device_id_type=MESH).wait_recv()
        for cp in sends:
            cp.wait_send()

    return pl.pallas_call(
        body, name="rs_direct", in_specs=[ANY], out_specs=ANY,
        out_shape=jax.ShapeDtypeStruct((2, hrows, Wd), m2.dtype),
        scratch_shapes=[pltpu.SemaphoreType.DMA((2,)), pltpu.SemaphoreType.DMA((2,))],
    )(m2)


def _rs_final(r4, got, me_idx):
    _, hrows, Wd = r4.shape
    bt = _pick(hrows, (512, 256, 16))

    def body(i_ref, r_ref, g_ref, o_ref):
        o_ref[...] = (r_ref[0].astype(F32) + g_ref[0].astype(F32)) + g_ref[1].astype(F32)

    grid_spec = pltpu.PrefetchScalarGridSpec(
        num_scalar_prefetch=1, grid=(hrows // bt,),
        in_specs=[pl.BlockSpec((1, bt, Wd), lambda i, idx: (idx[0], i, 0)),
                  pl.BlockSpec((2, bt, Wd), lambda i, idx: (0, i, 0))],
        out_specs=pl.BlockSpec((bt, Wd), lambda i, idx: (i, 0)))
    return pl.pallas_call(
        body, name="rs_final", grid_spec=grid_spec,
        out_shape=jax.ShapeDtypeStruct((hrows, Wd), F32),
        compiler_params=_cp(("parallel",)),
    )(me_idx, r4, got)


SHARDED = (
    ("w_in", (2, 1024, 1730), 2),
    ("w_branch", (2, 3, 512, 256), 3),
    ("w_mix_out", (2, 256, 1024), 1),
    ("w_xq", (2, 256, 1024), 1),
    ("w_xkv", (2, 1024, 512), 2),
    ("w_xo", (2, 256, 1024), 1),
    ("w_ffn_gate", (2, 1024, 704), 2),
    ("w_ffn_up", (2, 1024, 704), 2),
    ("w_ffn_down", (2, 704, 1024), 1),
    ("conv_w", (2, 3, 128), 2),
)
PACK_W = 1024
PACK_ELEMS = sum(int(np.prod(s)) for _, s, _ in SHARDED)
PACK_ROWS = -(-PACK_ELEMS // (PACK_W * 1024)) * 1024


def _pack(parts, dtype):
    flat = jnp.concatenate([p.astype(dtype).reshape(-1) for p in parts]
                           + [jnp.zeros((PACK_ROWS * PACK_W - PACK_ELEMS,), dtype)])
    return flat.reshape(PACK_ROWS, PACK_W)


def _unpack(pack):
    flat = pack.reshape(-1)
    out, off = {}, 0
    for name, shape, _ in SHARDED:
        n = int(np.prod(shape))
        out[name] = flat[off:off + n].reshape(shape)
        off += n
    return out


SMALL = (
    ("mix_norm_g", (2, 1024)), ("xattn_norm_g", (2, 1024)), ("mem_norm_g", (2, 1024)),
    ("ffn_norm_g", (2, 1024)), ("final_norm_g", (1024,)),
    ("forget_bias", (2, 8)), ("sink", (2, 8)), ("rel_bias", (32, 8)),
)
SMALL_ROWS = 112


def _pack_small(vals):
    rows = []
    for name, shape in SMALL:
        v = vals[name].astype(F32)
        if shape[-1] == 1024:
            rows.append(v.reshape(-1, 128))
        else:
            rows.append(jnp.pad(v, ((0, 0), (0, 120))))
    rows = jnp.concatenate(rows, axis=0)
    return jnp.pad(rows, ((0, SMALL_ROWS - rows.shape[0]), (0, 0)))


def _unpack_small(pack):
    out, off = {}, 0
    for name, shape in SMALL:
        if shape[-1] == 1024:
            n = int(np.prod(shape)) // 128
            out[name] = pack[off:off + n].reshape(shape)
        else:
            n = shape[0]
            out[name] = pack[off:off + n, 0:8]
        off += n
    return out


W_IN_PERM = ((3848, 6920), (0, 3072), (3080, 3848), (3072, 3080))


def _perm_w_in(w):
    parts = [w[:, a:b] for a, b in W_IN_PERM]
    return jnp.concatenate(parts + [jnp.zeros((w.shape[0], PROJ_PAD - IN_COLS), w.dtype)], axis=1)


def _unperm_w_in(p):
    return jnp.concatenate([p[:, 3072:6144], p[:, 6912:6920], p[:, 6144:6912], p[:, 0:3072]], axis=1)


def _pad_row8(v):
    return jnp.pad(v.astype(F32).reshape(1, 8), ((0, 0), (0, 120)))


def _local_step(x, mem, tgt, W, rel_bias):
    T = x.shape[0]
    bucket = jnp.asarray(_bucket_table())
    bias = _swa_bias(rel_bias, bucket, "swa_bias")
    saved = []
    for l in range(DEPTH):
        n = "l%d_" % l
        s = {"x0": x}
        wcat = W["w_in_p"][l]
        h = _rms_fwd(x, W["mix_norm_g"][l:l + 1], n + "mix_norm")
        pm = _mm(h, wcat[:, :PROJ_MAIN], "nn", BF16, n + "proj", bn=768)
        fg = _mm(h, wcat[:, PROJ_MAIN:], "nn", F32, n + "proj_fg")
        fb = _pad_row8(W["forget_bias"][l])
        c_col = _fox_gate_fwd(fg, fb, n + "fox_gate")
        c_row = c_col[:, 0:8].T
        cw = jnp.pad(W["conv_w"][l], ((0, 5), (0, 0)))
        y_conv = _conv_fwd(pm, cw, n + "conv")
        y_fox, lse = _fox2_fwd(pm, c_row, n + "fox")
        sink = _pad_row8(W["sink"][l])
        y_swa, mlse = _swa_fwd(pm, bias, sink, n + "swa")
        ys = (y_conv, y_fox, y_swa)
        us = tuple(_mm(ys[b], W["w_branch"][l, b], "nn", BF16, n + "branch%d" % b) for b in range(3))
        merged = _merge_fwd(pm, us, n + "merge")
        x1 = _mm(merged, W["w_mix_out"][l], "nn", F32, n + "mix_out", res=x)
        xn1 = _rms_fwd(x1, W["xattn_norm_g"][l:l + 1], n + "xattn_norm")
        memn = _rms_fwd(mem, W["mem_norm_g"][l:l + 1], n + "mem_norm")
        qx = _mm(xn1, W["w_xq"][l], "nn", BF16, n + "xq")
        kv = _mm(memn, W["w_xkv"][l], "nn", BF16, n + "xkv")
        ox = _xattn_fwd(qx, kv, n + "xattn")
        x2 = _mm(ox, W["w_xo"][l], "nn", F32, n + "xo", res=x1)
        xn2 = _rms_fwd(x2, W["ffn_norm_g"][l:l + 1], n + "ffn_norm")
        ab = _mm(xn2, W["w_gu"][l], "nn", BF16, n + "ffn_in", bn=512)
        hm = _swiglu_fwd(ab, n + "swiglu")
        x3 = _mm(hm, W["w_ffn_down"][l], "nn", F32, n + "ffn_out", res=x2, bk=1408)
        s.update(h=h, pm=pm, fg=fg, fb=fb, c_col=c_col, c_row=c_row, cw=cw, ys=ys, lse=lse, sink=sink,
                 mlse=mlse, us=us, merged=merged, x1=x1, xn1=xn1, memn=memn, qx=qx, kv=kv, ox=ox,
                 x2=x2, xn2=xn2, ab=ab, hm=hm)
        saved.append(s)
        x = x3

    loss_row, dx, dg_final = _final_loss(x, W["final_norm_g"].reshape(1, D_MODEL), tgt, "final_loss")
    G = {name: [None] * DEPTH for name in
         ("mix_norm_g", "w_in_p", "forget_bias", "conv_w", "sink", "w_branch", "w_mix_out", "xattn_norm_g",
          "mem_norm_g", "w_xq", "w_xkv", "w_xo", "ffn_norm_g", "w_gu", "w_ffn_down")}
    dbias_tot = None
    for l in reversed(range(DEPTH)):
        n = "l%d_" % l
        s = saved[l]
        dhm = _mm(dx, W["w_ffn_down"][l], "nt", BF16, n + "d_hm", bn=1408)
        G["w_ffn_down"][l] = _mm(s["hm"], dx, "tn", F32, n + "dw_down", bm=1408, bk=1024)
        dab = _swiglu_bwd(s["ab"], dhm, n + "d_swiglu")
        dxn2 = _mm(dab, W["w_gu"][l], "nt", BF16, n + "d_xn2", bk=1408)
        G["w_gu"][l] = _mm(s["xn2"], dab, "tn", F32, n + "dw_gu", bn=512, bk=2048)
        dx, G["ffn_norm_g"][l] = _rms_bwd(s["x2"], W["ffn_norm_g"][l:l + 1], dxn2, dx, n + "d_ffn_norm")
        dox = _mm(dx, W["w_xo"][l], "nt", BF16, n + "d_ox")
        G["w_xo"][l] = _mm(s["ox"], dx, "tn", F32, n + "dw_xo", bk=1024)
        dqx, dkv = _xattn_bwd(s["qx"], s["kv"], dox, n + "d_xattn")
        dxn1 = _mm(dqx, W["w_xq"][l], "nt", BF16, n + "d_xn1")
        G["w_xq"][l] = _mm(s["xn1"], dqx, "tn", F32, n + "dw_xq", bk=2048)
        dmemn = _mm(dkv, W["w_xkv"][l], "nt", BF16, n + "d_memn")
        G["w_xkv"][l] = _mm(s["memn"], dkv, "tn", F32, n + "dw_xkv")
        _, G["mem_norm_g"][l] = _rms_bwd(mem, W["mem_norm_g"][l:l + 1], dmemn, None, n + "d_mem_norm")
        dx, G["xattn_norm_g"][l] = _rms_bwd(s["x1"], W["xattn_norm_g"][l:l + 1], dxn1, dx, n + "d_xattn_norm")
        dmerged = _mm(dx, W["w_mix_out"][l], "nt", BF16, n + "d_merged")
        G["w_mix_out"][l] = _mm(s["merged"], dx, "tn", F32, n + "dw_mix_out", bk=1024)
        du0, du1, du2, dgates = _merge_bwd(s["pm"], s["us"], dmerged, n + "d_merge")
        dus = (du0, du1, du2)
        dys = [_mm(dus[b], W["w_branch"][l, b], "nt", BF16, n + "d_y%d" % b) for b in range(3)]
        G["w_branch"][l] = jnp.stack(
            [_mm(s["ys"][b], dus[b], "tn", F32, n + "dw_branch%d" % b, bk=2048) for b in range(3)])
        dcb, dcc, dcu, dcw = _conv_bwd(s["pm"], s["cw"], dys[0], n + "d_conv")
        G["conv_w"][l] = dcw[0:3]
        delta = _fox_delta(s["ys"][1], dys[1], n + "fox_delta")
        dfq, delta = _fox2_bwd_dq(s["pm"], dys[1], s["c_row"], s["lse"], delta, n + "d_fox_q")
        dfk, dfv, dc = _fox2_bwd_dkv(s["pm"], dys[1], s["c_col"], s["lse"][:, 0:8].T, delta[:, 0:8].T,
                                     n + "d_fox_kv")
        dfg, dfb = _fox_gate_bwd(dc, s["fg"], s["fb"], n + "d_fox_gate")
        G["forget_bias"][l] = dfb[0, 0:8]
        dsq, dkc, dkp, dvc, dvp, dbias, dsink = _swa_bwd(s["pm"], bias, s["sink"], dys[2], s["mlse"],
                                                        n + "d_swa")
        G["sink"][l] = dsink[0, 0:8]
        dbias_tot = dbias if dbias_tot is None else dbias_tot + dbias
        zpad = jnp.zeros((WINDOW, 128), F32)
        dsk = dkc + jnp.concatenate([dkp[WINDOW:], zpad], axis=0)
        dsv = dvc + jnp.concatenate([dvp[WINDOW:], zpad], axis=0)
        dproj = jnp.concatenate([dgates, dcb, dcc, dcu, dfq, dfk, dfv, dsq, dsk.astype(BF16),
                                 dsv.astype(BF16), dfg.astype(BF16)], axis=1)
        dh = _mm(dproj, W["w_in_p"][l], "nt", BF16, n + "d_h", bk=1408)
        G["w_in_p"][l] = _mm(s["h"], dproj, "tn", F32, n + "dw_in", bn=640, bk=2048)
        dx, G["mix_norm_g"][l] = _rms_bwd(s["x0"], W["mix_norm_g"][l:l + 1], dh, dx, n + "d_mix_norm")
    drb = _swa_dbias_reduce(dbias_tot, bucket, "swa_dbias")
    G["rel_bias"] = drb[:, 0:8]
    G["final_norm_g"] = dg_final.reshape(D_MODEL)
    return loss_row, dx, G


def kernel(x, mem, mix_norm_g, w_in, forget_bias, conv_w, sink, w_branch, w_mix_out, rel_bias, xattn_norm_g, mem_norm_g, w_xq, w_xkv, w_xo, ffn_norm_g, w_ffn_gate, w_ffn_up, w_ffn_down, final_norm_g, loss_target, m_mix_norm_g, m_w_in, m_forget_bias, m_conv_w, m_sink, m_w_branch, m_w_mix_out, m_rel_bias, m_xattn_norm_g, m_mem_norm_g, m_w_xq, m_w_xkv, m_w_xo, m_ffn_norm_g, m_w_ffn_gate, m_w_ffn_up, m_w_ffn_down, m_final_norm_g, v_mix_norm_g, v_w_in, v_forget_bias, v_conv_w, v_sink, v_w_branch, v_w_mix_out, v_rel_bias, v_xattn_norm_g, v_mem_norm_g, v_w_xq, v_w_xkv, v_w_xo, v_ffn_norm_g, v_w_ffn_gate, v_w_ffn_up, v_w_ffn_down, v_final_norm_g):
    order = ("mix_norm_g", "w_in", "forget_bias", "conv_w", "sink", "w_branch", "w_mix_out", "rel_bias",
             "xattn_norm_g", "mem_norm_g", "w_xq", "w_xkv", "w_xo", "ffn_norm_g", "w_ffn_gate", "w_ffn_up",
             "w_ffn_down", "final_norm_g")
    w_sh = dict(zip(order, (mix_norm_g, w_in, forget_bias, conv_w, sink, w_branch, w_mix_out, rel_bias,
                            xattn_norm_g, mem_norm_g, w_xq, w_xkv, w_xo, ffn_norm_g, w_ffn_gate, w_ffn_up,
                            w_ffn_down, final_norm_g)))
    m_sh = dict(zip(order, (m_mix_norm_g, m_w_in, m_forget_bias, m_conv_w, m_sink, m_w_branch, m_w_mix_out,
                            m_rel_bias, m_xattn_norm_g, m_mem_norm_g, m_w_xq, m_w_xkv, m_w_xo, m_ffn_norm_g,
                            m_w_ffn_gate, m_w_ffn_up, m_w_ffn_down, m_final_norm_g)))
    v_sh = dict(zip(order, (v_mix_norm_g, v_w_in, v_forget_bias, v_conv_w, v_sink, v_w_branch, v_w_mix_out,
                            v_rel_bias, v_xattn_norm_g, v_mem_norm_g, v_w_xq, v_w_xkv, v_w_xo, v_ffn_norm_g,
                            v_w_ffn_gate, v_w_ffn_up, v_w_ffn_down, v_final_norm_g)))

    gathered = _ag_ring(_pack([w_sh[name] for name, _, _ in SHARDED], BF16))
    per_chip = [_unpack(gathered[j]) for j in range(4)]
    full = {name: jnp.concatenate([per_chip[j][name] for j in range(4)], axis=ax) for name, _, ax in SHARDED}
    W = {k: w_sh[k] for k in ("mix_norm_g", "forget_bias", "sink", "xattn_norm_g", "mem_norm_g",
                              "ffn_norm_g", "final_norm_g")}
    W["w_in_p"] = [_perm_w_in(full["w_in"][l]) for l in range(DEPTH)]
    W["w_gu"] = jnp.concatenate([full["w_ffn_gate"], full["w_ffn_up"]], axis=2)
    W["conv_w"] = full["conv_w"].astype(F32)
    for k in ("w_branch", "w_mix_out", "w_xq", "w_xkv", "w_xo", "w_ffn_down"):
        W[k] = full[k]
    loss_row, dx, G = _local_step(x[0], mem[0], loss_target[0], W, rel_bias)

    gfull = {
        "w_in": [_unperm_w_in(G["w_in_p"][l]) for l in range(DEPTH)],
        "w_branch": G["w_branch"],
        "w_mix_out": G["w_mix_out"],
        "w_xq": G["w_xq"],
        "w_xkv": G["w_xkv"],
        "w_xo": G["w_xo"],
        "w_ffn_gate": [G["w_gu"][l][:, :D_FF] for l in range(DEPTH)],
        "w_ffn_up": [G["w_gu"][l][:, D_FF:] for l in range(DEPTH)],
        "w_ffn_down": G["w_ffn_down"],
        "conv_w": G["conv_w"],
    }
    slabs = []
    for j in range(4):
        parts = []
        for name, shape, ax in SHARDED:
            n = shape[ax]
            parts += [lax.slice_in_dim(gfull[name][l], j * n, (j + 1) * n, axis=ax - 1) for l in range(DEPTH)]
        slabs.append(_pack(parts, BF16))
    g4 = jnp.stack(slabs)
    xi, yi, ci = lax.axis_index("x"), lax.axis_index("y"), lax.axis_index("c")
    as_idx = lambda *v: jnp.stack([jnp.asarray(t, I32) for t in v])
    pair = _rs_add_pair(g4, _rs_sibling(g4), as_idx(ci))
    merged = _rs_merge(pair, _rs_diag(pair), as_idx(2 * (1 - xi) + yi, 2 * xi + (1 - yi)))
    half = _rs_final(pair, _rs_direct(merged), as_idx(2 * xi + yi))
    gsh = _unpack(_rs_share(half))

    small = _unpack_small(_allreduce_small(_pack_small({
        "mix_norm_g": jnp.concatenate(G["mix_norm_g"], axis=0),
        "xattn_norm_g": jnp.concatenate(G["xattn_norm_g"], axis=0),
        "mem_norm_g": jnp.concatenate(G["mem_norm_g"], axis=0),
        "ffn_norm_g": jnp.concatenate(G["ffn_norm_g"], axis=0),
        "final_norm_g": G["final_norm_g"],
        "forget_bias": jnp.stack(G["forget_bias"]),
        "sink": jnp.stack(G["sink"]),
        "rel_bias": G["rel_bias"],
    })))
    grads = dict(gsh)
    grads.update(small)

    sm_names = [name for name, _ in SMALL]
    sd, sm_, sv_ = _adamw(_pack_small({k: w_sh[k] for k in sm_names}), _pack_small({k: grads[k] for k in sm_names}),
                          _pack_small({k: m_sh[k] for k in sm_names}), _pack_small({k: v_sh[k] for k in sm_names}),
                          "adamw_small")
    delta, new_m, new_v = _unpack_small(sd), _unpack_small(sm_), _unpack_small(sv_)
    for name, shape, _ in SHARDED:
        two_d = (-1, shape[-1])
        d, nm, nv = _adamw(w_sh[name].reshape(two_d), grads[name].reshape(two_d), m_sh[name].reshape(two_d),
                           v_sh[name].reshape(two_d), "adamw_" + name)
        delta[name], new_m[name], new_v[name] = d.reshape(shape), nm.reshape(shape), nv.reshape(shape)

    loss = lax.psum(loss_row[0, 0], ("x", "y", "c"))
    return (loss, dx[None], *[grads[k] for k in order], *[delta[k] for k in order],
            *[new_m[k] for k in order], *[new_v[k] for k in order])
```

```python
import math

import numpy as np
import jax
import jax.numpy as jnp
from jax import lax
from jax.experimental import pallas as pl
from jax.experimental.pallas import tpu as pltpu

F32 = jnp.float32
BF16 = jnp.bfloat16
I32 = jnp.int32

D_MODEL = 1024
DEPTH = 2
HEAD_DIM = 64
BRANCH = 512
N_BUCKETS = 32
WINDOW = 128
MEM_LEN = 256
X_HEADS = 4
X_HEAD_DIM = 256
D_FF = 2816
IN_COLS = 6920
PROJ_MAIN = 6912
PROJ_PAD = 7040
RMS_EPS = 1e-6
NEG = -1e30
ATT_SCALE = 0.125
X_SCALE = 0.0625

ADAM_LR = 0.001
ADAM_B1 = 0.9
ADAM_B2 = 0.999
ADAM_EPS = 1e-08
ADAM_WD = 0.01
ADAM_STEP = 10

VMEM_LIMIT = 48 * 1024 * 1024
MESH = pl.DeviceIdType.MESH

CB_GATE = (0, 1, 2)
CB_B, CB_C, CB_U, CB_FQ, CB_FK, CB_FV, CB_SQ = 6, 7, 8, 9, 10, 11, 12
CB_SK, CB_SV = 52, 53


def _cp(sem):
    return pltpu.CompilerParams(dimension_semantics=sem, vmem_limit_bytes=VMEM_LIMIT)


def _pick(n, prefs):
    for p in prefs:
        if p <= n and n % p == 0:
            return p
    return n


def _dot(a, b, dims):
    return lax.dot_general(a, b, (dims, ((), ())), preferred_element_type=F32)


def _dot_nn(a, b):
    return _dot(a, b, ((1,), (0,)))


def _dot_nt(a, b):
    return _dot(a, b, ((1,), (1,)))


def _dot_tn(a, b):
    return _dot(a, b, ((0,), (0,)))


def _mm(a, b, mode, out_dtype, name, res=None, bm=1024, bn=1024, bk=1024):
    if mode == "nn":
        (M, K), (K2, N) = a.shape, b.shape
    elif mode == "nt":
        (M, K), (N, K2) = a.shape, b.shape
    else:
        (K, M), (K2, N) = a.shape, b.shape
    assert K == K2, (name, a.shape, b.shape)
    bm = _pick(M, (bm, 1024, 512, 256, 128))
    bn = _pick(N, (bn, 1024, 768, 640, 512, 384, 256, 128))
    bk = _pick(K, (bk, 1024, 768, 640, 512, 384, 256, 128))
    nk = K // bk
    if mode == "tn":
        a_spec = pl.BlockSpec((bk, bm), lambda i, j, k: (k, i))
    else:
        a_spec = pl.BlockSpec((bm, bk), lambda i, j, k: (i, k))
    if mode == "nt":
        b_spec = pl.BlockSpec((bn, bk), lambda i, j, k: (j, k))
    else:
        b_spec = pl.BlockSpec((bk, bn), lambda i, j, k: (k, j))
    dims = {"nn": ((1,), (0,)), "nt": ((1,), (1,)), "tn": ((0,), (0,))}[mode]
    o_spec = pl.BlockSpec((bm, bn), lambda i, j, k: (i, j))
    has_res = res is not None

    def body(*refs):
        if has_res:
            a_ref, b_ref, r_ref, o_ref = refs[:4]
            scr = refs[4:]
        else:
            a_ref, b_ref, o_ref = refs[:3]
            r_ref = None
            scr = refs[3:]
        p = _dot(a_ref[...].astype(BF16), b_ref[...].astype(BF16), dims)
        if nk == 1:
            if has_res:
                p = p + r_ref[...]
            o_ref[...] = p.astype(out_dtype)
        else:
            acc = scr[0]
            k = pl.program_id(2)

            @pl.when(k == 0)
            def _():
                acc[...] = p

            @pl.when(k > 0)
            def _():
                acc[...] += p

            @pl.when(k == nk - 1)
            def _():
                r = acc[...]
                if has_res:
                    r = r + r_ref[...]
                o_ref[...] = r.astype(out_dtype)

    ins = [a, b] + ([res] if has_res else [])
    in_specs = [a_spec, b_spec] + ([o_spec] if has_res else [])
    return pl.pallas_call(
        body, name=name, grid=(M // bm, N // bn, nk),
        in_specs=in_specs, out_specs=o_spec,
        out_shape=jax.ShapeDtypeStruct((M, N), out_dtype),
        scratch_shapes=[pltpu.VMEM((bm, bn), F32)] if nk > 1 else [],
        compiler_params=_cp(("parallel", "parallel", "arbitrary")),
    )(*ins)


def _rms_fwd(x, g, name):
    T, Dm = x.shape
    bt = _pick(T, (512, 256))

    def body(x_ref, g_ref, o_ref):
        xv = x_ref[...]
        r = lax.rsqrt(jnp.mean(xv * xv, axis=-1, keepdims=True) + RMS_EPS)
        o_ref[...] = ((xv * r) * g_ref[...]).astype(BF16)

    return pl.pallas_call(
        body, name=name, grid=(T // bt,),
        in_specs=[pl.BlockSpec((bt, Dm), lambda i: (i, 0)), pl.BlockSpec((1, Dm), lambda i: (0, 0))],
        out_specs=pl.BlockSpec((bt, Dm), lambda i: (i, 0)),
        out_shape=jax.ShapeDtypeStruct((T, Dm), BF16),
        compiler_params=_cp(("parallel",)),
    )(x, g)


def _rms_bwd(x, g, dh, dres, name):
    T, Dm = x.shape
    bt = _pick(T, (512, 256))
    want_dx = dres is not None

    def body(*refs):
        if want_dx:
            x_ref, g_ref, dh_ref, dr_ref, dx_ref, dg_ref = refs
        else:
            x_ref, g_ref, dh_ref, dg_ref = refs
        xv = x_ref[...]
        r = lax.rsqrt(jnp.mean(xv * xv, axis=-1, keepdims=True) + RMS_EPS)
        xh = xv * r
        dhv = dh_ref[...].astype(F32)

        @pl.when(pl.program_id(0) == 0)
        def _():
            dg_ref[...] = jnp.zeros_like(dg_ref)

        dg_ref[...] += jnp.sum(dhv * xh, axis=0, keepdims=True)
        if want_dx:
            dyg = dhv * g_ref[...]
            dx_ref[...] = dr_ref[...] + r * (dyg - xh * jnp.mean(dyg * xh, axis=-1, keepdims=True))

    row = pl.BlockSpec((bt, Dm), lambda i: (i, 0))
    vec = pl.BlockSpec((1, Dm), lambda i: (0, 0))
    if want_dx:
        return pl.pallas_call(
            body, name=name, grid=(T // bt,),
            in_specs=[row, vec, row, row], out_specs=[row, vec],
            out_shape=[jax.ShapeDtypeStruct((T, Dm), F32), jax.ShapeDtypeStruct((1, Dm), F32)],
            compiler_params=_cp(("arbitrary",)),
        )(x, g, dh, dres)
    return None, pl.pallas_call(
        body, name=name, grid=(T // bt,),
        in_specs=[row, vec, row], out_specs=vec,
        out_shape=jax.ShapeDtypeStruct((1, Dm), F32),
        compiler_params=_cp(("arbitrary",)),
    )(x, g, dh)


def _final_loss(x, g, tgt, name):
    T, Dm = x.shape
    bt = _pick(T, (512, 256))

    def body(x_ref, g_ref, t_ref, loss_ref, dx_ref, dg_ref):
        xv = x_ref[...]
        r = lax.rsqrt(jnp.mean(xv * xv, axis=-1, keepdims=True) + RMS_EPS)
        xh = xv * r
        gv = g_ref[...]
        err = xh * gv - t_ref[...]

        @pl.when(pl.program_id(0) == 0)
        def _():
            dg_ref[...] = jnp.zeros_like(dg_ref)
            loss_ref[...] = jnp.zeros_like(loss_ref)

        loss_ref[...] += jnp.sum(err * err) * (0.5 / Dm)
        dy = err * (1.0 / Dm)
        dg_ref[...] += jnp.sum(dy * xh, axis=0, keepdims=True)
        dyg = dy * gv
        dx_ref[...] = r * (dyg - xh * jnp.mean(dyg * xh, axis=-1, keepdims=True))

    row = pl.BlockSpec((bt, Dm), lambda i: (i, 0))
    vec = pl.BlockSpec((1, Dm), lambda i: (0, 0))
    return pl.pallas_call(
        body, name=name, grid=(T // bt,),
        in_specs=[row, vec, row],
        out_specs=[pl.BlockSpec((1, 128), lambda i: (0, 0)), row, vec],
        out_shape=[jax.ShapeDtypeStruct((1, 128), F32), jax.ShapeDtypeStruct((T, Dm), F32),
                   jax.ShapeDtypeStruct((1, Dm), F32)],
        compiler_params=_cp(("arbitrary",)),
    )(x, g, tgt)


HALO = 16


def _shift_down(z, zprev, s):
    rolled = pltpu.roll(z, s, 0)
    hp = pltpu.roll(zprev, s, 0)
    row = lax.broadcasted_iota(I32, hp.shape, 0)
    top = jnp.where(row < s, hp, rolled[:HALO])
    return jnp.concatenate([top, rolled[HALO:]], axis=0)


def _shift_up(z, znext, s):
    n = z.shape[0]
    rolled = pltpu.roll(z, n - s, 0)
    hn = pltpu.roll(znext, HALO - s, 0)
    row = lax.broadcasted_iota(I32, hn.shape, 0)
    bot = jnp.where(row >= HALO - s, hn, rolled[n - HALO:])
    return jnp.concatenate([rolled[:n - HALO], bot], axis=0)


def _conv_fwd(pm, cw, name):
    T = pm.shape[0]
    bt = _pick(T, (512, 256))
    hb = bt // HALO

    def body(b_ref, c_ref, u_ref, cp_ref, up_ref, w_ref, o_ref):
        i = pl.program_id(0)
        z = c_ref[...].astype(F32) * u_ref[...].astype(F32)
        zp = cp_ref[...].astype(F32) * up_ref[...].astype(F32)
        zp = jnp.where(i > 0, zp, 0.0)
        w = w_ref[...]
        y = w[2:3] * z + w[1:2] * _shift_down(z, zp, 1) + w[0:1] * _shift_down(z, zp, 2)
        o_ref[...] = (b_ref[...].astype(F32) * y).astype(BF16)

    def col(cb):
        return pl.BlockSpec((bt, BRANCH), lambda i: (i, cb))

    def prev(cb):
        return pl.BlockSpec((HALO, BRANCH), lambda i: (jnp.maximum(i * hb - 1, 0), cb))

    return pl.pallas_call(
        body, name=name, grid=(T // bt,),
        in_specs=[col(CB_B), col(CB_C), col(CB_U), prev(CB_C), prev(CB_U),
                  pl.BlockSpec((8, BRANCH), lambda i: (0, 0))],
        out_specs=pl.BlockSpec((bt, BRANCH), lambda i: (i, 0)),
        out_shape=jax.ShapeDtypeStruct((T, BRANCH), BF16),
        compiler_params=_cp(("parallel",)),
    )(pm, pm, pm, pm, pm, cw)


def _conv_bwd(pm, cw, dy, name):
    T = pm.shape[0]
    bt = _pick(T, (512, 256))
    hb = bt // HALO
    nb = T // bt
    last_h = T // HALO - 1

    def body(b_ref, c_ref, u_ref, cp_ref, up_ref, bn_ref, dy_ref, dyn_ref, w_ref,
             db_ref, dc_ref, du_ref, dw_ref):
        i = pl.program_id(0)
        cv = c_ref[...].astype(F32)
        uv = u_ref[...].astype(F32)
        bv = b_ref[...].astype(F32)
        z = cv * uv
        zp = jnp.where(i > 0, cp_ref[...].astype(F32) * up_ref[...].astype(F32), 0.0)
        w = w_ref[...]
        z1 = _shift_down(z, zp, 1)
        z2 = _shift_down(z, zp, 2)
        yc = w[2:3] * z + w[1:2] * z1 + w[0:1] * z2
        dyv = dy_ref[...].astype(F32)
        db_ref[...] = (dyv * yc).astype(BF16)
        g = dyv * bv
        gn = jnp.where(i < nb - 1, dyn_ref[...].astype(F32) * bn_ref[...].astype(F32), 0.0)
        dz = w[2:3] * g + w[1:2] * _shift_up(g, gn, 1) + w[0:1] * _shift_up(g, gn, 2)
        dc_ref[...] = (dz * uv).astype(BF16)
        du_ref[...] = (dz * cv).astype(BF16)

        @pl.when(i == 0)
        def _():
            dw_ref[...] = jnp.zeros_like(dw_ref)

        dw_ref[0:1, :] += jnp.sum(g * z2, axis=0, keepdims=True)
        dw_ref[1:2, :] += jnp.sum(g * z1, axis=0, keepdims=True)
        dw_ref[2:3, :] += jnp.sum(g * z, axis=0, keepdims=True)

    def col(cb):
        return pl.BlockSpec((bt, BRANCH), lambda i: (i, cb))

    def prev(cb):
        return pl.BlockSpec((HALO, BRANCH), lambda i: (jnp.maximum(i * hb - 1, 0), cb))

    def nxt(cb):
        return pl.BlockSpec((HALO, BRANCH), lambda i: (jnp.minimum((i + 1) * hb, last_h), cb))

    own = pl.BlockSpec((bt, BRANCH), lambda i: (i, 0))
    w_spec = pl.BlockSpec((8, BRANCH), lambda i: (0, 0))
    act = jax.ShapeDtypeStruct((T, BRANCH), BF16)
    return pl.pallas_call(
        body, name=name, grid=(nb,),
        in_specs=[col(CB_B), col(CB_C), col(CB_U), prev(CB_C), prev(CB_U), nxt(CB_B), own,
                  pl.BlockSpec((HALO, BRANCH), lambda i: (jnp.minimum((i + 1) * hb, last_h), 0)), w_spec],
        out_specs=[own, own, own, w_spec],
        out_shape=[act, act, act, jax.ShapeDtypeStruct((8, BRANCH), F32)],
        compiler_params=_cp(("arbitrary",)),
    )(pm, pm, pm, pm, pm, pm, dy, dy, cw)


def _log_sigmoid(z):
    return jnp.minimum(z, 0.0) - jnp.log(1.0 + jnp.exp(-jnp.abs(z)))


def _fox_gate_fwd(fg, fb, name):
    T = fg.shape[0]
    bt = _pick(T, (256,))

    def body(f_ref, b_ref, c_ref, carry):
        @pl.when(pl.program_id(0) == 0)
        def _():
            carry[...] = jnp.zeros_like(carry)

        xv = _log_sigmoid(f_ref[...] + b_ref[...])
        row = lax.broadcasted_iota(I32, xv.shape, 0)
        s = 1
        while s < bt:
            xv = xv + jnp.where(row >= s, pltpu.roll(xv, s, 0), 0.0)
            s *= 2
        xv = xv + carry[...]
        c_ref[...] = xv
        carry[...] = xv[bt - 1:bt, :]

    blk = pl.BlockSpec((bt, 128), lambda i: (i, 0))
    return pl.pallas_call(
        body, name=name, grid=(T // bt,),
        in_specs=[blk, pl.BlockSpec((1, 128), lambda i: (0, 0))],
        out_specs=blk, out_shape=jax.ShapeDtypeStruct((T, 128), F32),
        scratch_shapes=[pltpu.VMEM((1, 128), F32)],
        compiler_params=_cp(("arbitrary",)),
    )(fg, fb)


def _fox_gate_bwd(dc, fg, fb, name):
    T = fg.shape[0]
    bt = _pick(T, (256,))
    nb = T // bt

    def body(d_ref, f_ref, b_ref, o_ref, db_ref, carry):
        @pl.when(pl.program_id(0) == 0)
        def _():
            carry[...] = jnp.zeros_like(carry)
            db_ref[...] = jnp.zeros_like(db_ref)

        xv = d_ref[...]
        row = lax.broadcasted_iota(I32, xv.shape, 0)
        s = 1
        while s < bt:
            xv = xv + jnp.where(row < bt - s, pltpu.roll(xv, bt - s, 0), 0.0)
            s *= 2
        xv = xv + carry[...]
        carry[...] = xv[0:1, :]
        z = f_ref[...] + b_ref[...]
        dz = xv * (1.0 / (1.0 + jnp.exp(z)))
        o_ref[...] = dz
        db_ref[...] += jnp.sum(dz, axis=0, keepdims=True)

    blk = pl.BlockSpec((bt, 128), lambda i: (nb - 1 - i, 0))
    vec = pl.BlockSpec((1, 128), lambda i: (0, 0))
    return pl.pallas_call(
        body, name=name, grid=(nb,),
        in_specs=[blk, blk, vec], out_specs=[blk, vec],
        out_shape=[jax.ShapeDtypeStruct((T, 128), F32), jax.ShapeDtypeStruct((1, 128), F32)],
        scratch_shapes=[pltpu.VMEM((1, 128), F32)],
        compiler_params=_cp(("arbitrary",)),
    )(dc, fg, fb)


def _lane_lo(shape):
    return lax.broadcasted_iota(I32, shape, 1) < HEAD_DIM


def _put_col(shape, h, col):
    lane = lax.broadcasted_iota(I32, shape, 1)
    return jnp.where(lane == h, col, 0.0)


def _fox_fwd(pm, c_col, c_row, name):
    T = pm.shape[0]
    bq = _pick(T, (512, 256))
    bk = bq
    nq = T // bq

    def body(q_ref, k_ref, v_ref, cq_ref, ck_ref, o_ref, lse_ref, acc, m_s, l_s):
        qi = pl.program_id(0)
        ki = pl.program_id(1)

        @pl.when(ki == 0)
        def _():
            acc[...] = jnp.zeros_like(acc)
            m_s[...] = jnp.full_like(m_s, NEG)
            l_s[...] = jnp.zeros_like(l_s)

        @pl.when(ki <= qi)
        def _():
            row = lax.broadcasted_iota(I32, (bq, bk), 0) + qi * bq
            colv = lax.broadcasted_iota(I32, (bq, bk), 1) + ki * bk
            causal = colv <= row
            klo = _lane_lo((bk, 128))
            qlo = _lane_lo((bq, 128))
            cq = cq_ref[...]
            ck = ck_ref[...]
            for p in range(4):
                sl = slice(128 * p, 128 * p + 128)
                qp = q_ref[:, sl] * ATT_SCALE
                kp = k_ref[:, sl]
                vp = v_ref[:, sl]
                kz = jnp.zeros_like(kp)
                ks = (jnp.where(klo, kp, kz), jnp.where(klo, kz, kp))
                alphas, pvs = [], []
                for j in range(2):
                    h = 2 * p + j
                    s = _dot_nt(qp, ks[j]) + (cq[:, h:h + 1] - ck[h:h + 1, :])
                    s = jnp.where(causal, s, NEG)
                    m_old = m_s[h][:, 0:1]
                    m_new = jnp.maximum(m_old, jnp.max(s, axis=-1, keepdims=True))
                    alpha = jnp.exp(m_old - m_new)
                    pe = jnp.exp(s - m_new)
                    l_new = alpha * l_s[h][:, 0:1] + jnp.sum(pe, axis=-1, keepdims=True)
                    m_s[h] = jnp.broadcast_to(m_new, (bq, 128))
                    l_s[h] = jnp.broadcast_to(l_new, (bq, 128))
                    alphas.append(alpha)
                    pvs.append(_dot_nn(pe.astype(BF16), vp))
                a = jnp.where(qlo, alphas[0], alphas[1])
                acc[:, sl] = a * acc[:, sl] + jnp.where(qlo, pvs[0], pvs[1])

        @pl.when(ki == nq - 1)
        def _():
            qlo = _lane_lo((bq, 128))
            lse = jnp.zeros((bq, 128), F32)
            for p in range(4):
                sl = slice(128 * p, 128 * p + 128)
                l0 = l_s[2 * p][:, 0:1]
                l1 = l_s[2 * p + 1][:, 0:1]
                o_ref[:, sl] = (acc[:, sl] / jnp.where(qlo, l0, l1)).astype(BF16)
                lse = lse + _put_col((bq, 128), 2 * p, m_s[2 * p][:, 0:1] + jnp.log(l0))
                lse = lse + _put_col((bq, 128), 2 * p + 1, m_s[2 * p + 1][:, 0:1] + jnp.log(l1))
            lse_ref[...] = lse

    return pl.pallas_call(
        body, name=name, grid=(nq, nq),
        in_specs=[pl.BlockSpec((bq, BRANCH), lambda i, k: (i, CB_FQ)),
                  pl.BlockSpec((bk, BRANCH), lambda i, k: (jnp.minimum(k, i), CB_FK)),
                  pl.BlockSpec((bk, BRANCH), lambda i, k: (jnp.minimum(k, i), CB_FV)),
                  pl.BlockSpec((bq, 128), lambda i, k: (i, 0)),
                  pl.BlockSpec((8, bk), lambda i, k: (0, jnp.minimum(k, i)))],
        out_specs=[pl.BlockSpec((bq, BRANCH), lambda i, k: (i, 0)),
                   pl.BlockSpec((bq, 128), lambda i, k: (i, 0))],
        out_shape=[jax.ShapeDtypeStruct((T, BRANCH), BF16), jax.ShapeDtypeStruct((T, 128), F32)],
        scratch_shapes=[pltpu.VMEM((bq, BRANCH), F32), pltpu.VMEM((8, bq, 128), F32),
                        pltpu.VMEM((8, bq, 128), F32)],
        compiler_params=_cp(("parallel", "arbitrary")),
    )(pm, pm, pm, c_col, c_row)


def _fox_delta(o, do, name):
    T = o.shape[0]
    bt = _pick(T, (512, 256))

    def body(o_ref, d_ref, out_ref):
        prod = o_ref[...].astype(F32) * d_ref[...].astype(F32)
        out = jnp.zeros((bt, 128), F32)
        for h in range(8):
            out = out + _put_col((bt, 128), h, jnp.sum(prod[:, 64 * h:64 * h + 64], axis=-1, keepdims=True))
        out_ref[...] = out

    blk = pl.BlockSpec((bt, BRANCH), lambda i: (i, 0))
    return pl.pallas_call(
        body, name=name, grid=(T // bt,), in_specs=[blk, blk],
        out_specs=pl.BlockSpec((bt, 128), lambda i: (i, 0)),
        out_shape=jax.ShapeDtypeStruct((T, 128), F32),
        compiler_params=_cp(("parallel",)),
    )(o, do)


def _fox_bwd_dq(pm, do, c_col, c_row, lse, delta, name):
    T = pm.shape[0]
    bq = _pick(T, (512, 256))
    bk = bq
    nq = T // bq

    def body(q_ref, k_ref, v_ref, do_ref, cq_ref, ck_ref, lse_ref, dl_ref, dq_ref, dl2_ref, acc, esum):
        qi = pl.program_id(0)
        ki = pl.program_id(1)

        @pl.when(ki == 0)
        def _():
            acc[...] = jnp.zeros_like(acc)
            esum[...] = jnp.zeros_like(esum)

        @pl.when(ki <= qi)
        def _():
            row = lax.broadcasted_iota(I32, (bq, bk), 0) + qi * bq
            colv = lax.broadcasted_iota(I32, (bq, bk), 1) + ki * bk
            causal = colv <= row
            klo = _lane_lo((bk, 128))
            qlo = _lane_lo((bq, 128))
            cq = cq_ref[...]
            ck = ck_ref[...]
            lse_v = lse_ref[...]
            dl_v = dl_ref[...]
            es = jnp.zeros((bq, 128), F32)
            for p in range(4):
                sl = slice(128 * p, 128 * p + 128)
                qp = q_ref[:, sl] * ATT_SCALE
                kp = k_ref[:, sl]
                vp = v_ref[:, sl]
                dop = do_ref[:, sl]
                kz = jnp.zeros_like(kp)
                ks = (jnp.where(klo, kp, kz), jnp.where(klo, kz, kp))
                vs = (jnp.where(klo, vp, kz), jnp.where(klo, kz, vp))
                dqs = []
                for j in range(2):
                    h = 2 * p + j
                    s = _dot_nt(qp, ks[j]) + (cq[:, h:h + 1] - ck[h:h + 1, :])
                    s = jnp.where(causal, s, NEG)
                    pr = jnp.exp(s - lse_v[:, h:h + 1])
                    dp = _dot_nt(dop, vs[j])
                    ds = pr * (dp - dl_v[:, h:h + 1])
                    es = es + _put_col((bq, 128), h, jnp.sum(ds, axis=-1, keepdims=True))
                    dqs.append(_dot_nn(ds.astype(BF16), kp))
                acc[:, sl] += jnp.where(qlo, dqs[0], dqs[1])
            esum[...] += es

        @pl.when(ki == nq - 1)
        def _():
            dq_ref[...] = (acc[...] * ATT_SCALE).astype(BF16)
            dl2_ref[...] = dl_ref[...] + esum[...]

    qb = pl.BlockSpec((bq, 128), lambda i, k: (i, 0))
    return pl.pallas_call(
        body, name=name, grid=(nq, nq),
        in_specs=[pl.BlockSpec((bq, BRANCH), lambda i, k: (i, CB_FQ)),
                  pl.BlockSpec((bk, BRANCH), lambda i, k: (jnp.minimum(k, i), CB_FK)),
                  pl.BlockSpec((bk, BRANCH), lambda i, k: (jnp.minimum(k, i), CB_FV)),
                  pl.BlockSpec((bq, BRANCH), lambda i, k: (i, 0)),
                  qb, pl.BlockSpec((8, bk), lambda i, k: (0, jnp.minimum(k, i))), qb, qb],
        out_specs=[pl.BlockSpec((bq, BRANCH), lambda i, k: (i, 0)), qb],
        out_shape=[jax.ShapeDtypeStruct((T, BRANCH), BF16), jax.ShapeDtypeStruct((T, 128), F32)],
        scratch_shapes=[pltpu.VMEM((bq, BRANCH), F32), pltpu.VMEM((bq, 128), F32)],
        compiler_params=_cp(("parallel", "arbitrary")),
    )(pm, pm, pm, do, c_col, c_row, lse, delta)


def _fox_bwd_dkv(pm, do, c_col, c_row, lse_row, delta_row, name):
    T = pm.shape[0]
    bk = _pick(T, (512, 256))
    bq = bk
    nk = T // bk

    def body(q_ref, k_ref, v_ref, do_ref, cq_ref, ck_ref, lse_ref, dl_ref,
             dk_ref, dv_ref, dc_ref, dk_acc, dv_acc, dc_acc):
        ki = pl.program_id(0)
        qi = pl.program_id(1)

        @pl.when(qi == 0)
        def _():
            dk_acc[...] = jnp.zeros_like(dk_acc)
            dv_acc[...] = jnp.zeros_like(dv_acc)
            dc_acc[...] = jnp.zeros_like(dc_acc)

        @pl.when(qi >= ki)
        def _():
            krow = lax.broadcasted_iota(I32, (bk, bq), 0) + ki * bk
            qcol = lax.broadcasted_iota(I32, (bk, bq), 1) + qi * bq
            causal = krow <= qcol
            qlo = _lane_lo((bq, 128))
            klo = _lane_lo((bk, 128))
            cq = cq_ref[...]
            ck = ck_ref[...]
            lse_v = lse_ref[...]
            dl_v = dl_ref[...]
            dcs = jnp.zeros((bk, 128), F32)
            for p in range(4):
                sl = slice(128 * p, 128 * p + 128)
                qp = q_ref[:, sl]
                kp = k_ref[:, sl] * ATT_SCALE
                vp = v_ref[:, sl]
                dop = do_ref[:, sl]
                qz = jnp.zeros_like(qp)
                qs = (jnp.where(qlo, qp, qz), jnp.where(qlo, qz, qp))
                dos = (jnp.where(qlo, dop, qz), jnp.where(qlo, qz, dop))
                dks, dvs = [], []
                for j in range(2):
                    h = 2 * p + j
                    st = _dot_nt(kp, qs[j]) + (cq[h:h + 1, :] - ck[:, h:h + 1])
                    st = jnp.where(causal, st, NEG)
                    pt = jnp.exp(st - lse_v[h:h + 1, :])
                    dvs.append(_dot_nn(pt.astype(BF16), dop))
                    dpt = _dot_nt(vp, dos[j])
                    dst = pt * (dpt - dl_v[h:h + 1, :])
                    dks.append(_dot_nn(dst.astype(BF16), qp))
                    dcs = dcs - _put_col((bk, 128), h, jnp.sum(dst, axis=-1, keepdims=True))
                dk_acc[:, sl] += jnp.where(klo, dks[0], dks[1])
                dv_acc[:, sl] += jnp.where(klo, dvs[0], dvs[1])
            dc_acc[...] += dcs

        @pl.when(qi == nk - 1)
        def _():
            dk_ref[...] = (dk_acc[...] * ATT_SCALE).astype(BF16)
            dv_ref[...] = dv_acc[...].astype(BF16)
            dc_ref[...] = dc_acc[...]

    qrow = pl.BlockSpec((8, bq), lambda k, i: (0, jnp.maximum(i, k)))
    kb = pl.BlockSpec((bk, BRANCH), lambda k, i: (k, 0))
    return pl.pallas_call(
        body, name=name, grid=(nk, nk),
        in_specs=[pl.BlockSpec((bq, BRANCH), lambda k, i: (jnp.maximum(i, k), CB_FQ)),
                  pl.BlockSpec((bk, BRANCH), lambda k, i: (k, CB_FK)),
                  pl.BlockSpec((bk, BRANCH), lambda k, i: (k, CB_FV)),
                  pl.BlockSpec((bq, BRANCH), lambda k, i: (jnp.maximum(i, k), 0)),
                  qrow, pl.BlockSpec((bk, 128), lambda k, i: (k, 0)), qrow, qrow],
        out_specs=[kb, kb, pl.BlockSpec((bk, 128), lambda k, i: (k, 0))],
        out_shape=[jax.ShapeDtypeStruct((T, BRANCH), BF16), jax.ShapeDtypeStruct((T, BRANCH), BF16),
                   jax.ShapeDtypeStruct((T, 128), F32)],
        scratch_shapes=[pltpu.VMEM((bk, BRANCH), F32), pltpu.VMEM((bk, BRANCH), F32),
                        pltpu.VMEM((bk, 128), F32)],
        compiler_params=_cp(("parallel", "arbitrary")),
    )(pm, pm, pm, do, c_row, c_col, lse_row, delta_row)


FOX_ROWS = 32


FOX_UNROLL = 16


def _row_start(r, rows):
    return r * rows if isinstance(r, int) else pl.multiple_of(r * rows, rows)


def _chunk_loop(n, chunk):
    if n <= FOX_UNROLL:
        for u in range(n):
            chunk(u, 0)
        return

    def outer(i, carry):
        for u in range(FOX_UNROLL):
            chunk(i * FOX_UNROLL + u, carry)
        return carry

    lax.fori_loop(0, n // FOX_UNROLL, outer, 0)


def _tree(op, xs):
    xs = list(xs)
    while len(xs) > 1:
        xs = [op(xs[i], xs[i + 1]) if i + 1 < len(xs) else xs[i] for i in range(0, len(xs), 2)]
    return xs[0]


def _masked_halves(t):
    lo = _lane_lo(t.shape)
    z = jnp.zeros_like(t)
    return jnp.where(lo, t, z), jnp.where(lo, z, t)


def _fox2_fwd(pm, c_row, name):
    T = pm.shape[0]
    bq = _pick(T, (512, 256))
    bk = bq
    nq = T // bq
    R = FOX_ROWS
    ng = bk // 128

    def body(q_ref, k_ref, v_ref, ck_ref, o_ref, lse_ref, acc, m_s, l_s, a_s, s_scr, p_scr):
        qi = pl.program_id(0)
        ki = pl.program_id(1)

        @pl.when(ki == 0)
        def _():
            acc[...] = jnp.zeros_like(acc)
            m_s[...] = jnp.full_like(m_s, NEG)
            l_s[...] = jnp.zeros_like(l_s)

        def block(masked):
            qlo = _lane_lo((bq, 128))
            for p in range(4):
                sl = slice(128 * p, 128 * p + 128)
                qp = q_ref[:, sl] * ATT_SCALE
                vp = v_ref[:, sl]
                ks = _masked_halves(k_ref[:, sl])
                pvs = []
                for j in range(2):
                    h = 2 * p + j
                    s_scr[j] = _dot_nt(qp, ks[j])

                    def chunk(r, carry, h=h, j=j):
                        r0 = _row_start(r, R)
                        rows = pl.ds(r0, R)
                        sc = [s_scr[j, rows, 128 * g:128 * g + 128] - ck_ref[h:h + 1, 128 * g:128 * g + 128]
                              for g in range(ng)]
                        if masked:
                            rid = lax.broadcasted_iota(I32, (R, 128), 0) + r0
                            cid = lax.broadcasted_iota(I32, (R, 128), 1)
                            sc = [jnp.where(cid + 128 * g <= rid, sc[g], NEG) for g in range(ng)]
                        m_old = m_s[h, rows, :]
                        m_new = jnp.maximum(m_old, jnp.max(_tree(jnp.maximum, sc), axis=-1, keepdims=True))
                        alpha = jnp.exp(m_old - m_new)
                        pe = [jnp.exp(sc[g] - m_new) for g in range(ng)]
                        l_s[h, rows, :] = alpha * l_s[h, rows, :] + _tree(jnp.add, pe)
                        m_s[h, rows, :] = m_new
                        a_s[j, rows, :] = alpha
                        for g in range(ng):
                            p_scr[j, rows, 128 * g:128 * g + 128] = pe[g].astype(BF16)
                        return carry

                    _chunk_loop(bq // R, chunk)
                    pvs.append(_dot_nn(p_scr[j], vp))
                acc[:, sl] = jnp.where(qlo, a_s[0], a_s[1]) * acc[:, sl] + jnp.where(qlo, pvs[0], pvs[1])

        @pl.when(ki < qi)
        def _():
            block(False)

        @pl.when(ki == qi)
        def _():
            block(True)

        @pl.when(ki == nq - 1)
        def _():
            qlo = _lane_lo((bq, 128))
            lse = jnp.zeros((bq, 128), F32)
            for p in range(4):
                sl = slice(128 * p, 128 * p + 128)
                l0 = jnp.sum(l_s[2 * p], axis=-1, keepdims=True)
                l1 = jnp.sum(l_s[2 * p + 1], axis=-1, keepdims=True)
                o_ref[:, sl] = (acc[:, sl] / jnp.where(qlo, l0, l1)).astype(BF16)
                lse = lse + _put_col((bq, 128), 2 * p, m_s[2 * p][:, 0:1] + jnp.log(l0))
                lse = lse + _put_col((bq, 128), 2 * p + 1, m_s[2 * p + 1][:, 0:1] + jnp.log(l1))
            lse_ref[...] = lse

    return pl.pallas_call(
        body, name=name, grid=(nq, nq),
        in_specs=[pl.BlockSpec((bq, BRANCH), lambda i, k: (i, CB_FQ)),
                  pl.BlockSpec((bk, BRANCH), lambda i, k: (jnp.minimum(k, i), CB_FK)),
                  pl.BlockSpec((bk, BRANCH), lambda i, k: (jnp.minimum(k, i), CB_FV)),
                  pl.BlockSpec((8, bk), lambda i, k: (0, jnp.minimum(k, i)))],
        out_specs=[pl.BlockSpec((bq, BRANCH), lambda i, k: (i, 0)),
                   pl.BlockSpec((bq, 128), lambda i, k: (i, 0))],
        out_shape=[jax.ShapeDtypeStruct((T, BRANCH), BF16), jax.ShapeDtypeStruct((T, 128), F32)],
        scratch_shapes=[pltpu.VMEM((bq, BRANCH), F32), pltpu.VMEM((8, bq, 128), F32),
                        pltpu.VMEM((8, bq, 128), F32), pltpu.VMEM((2, bq, 128), F32),
                        pltpu.VMEM((2, bq, bk), F32), pltpu.VMEM((2, bq, bk), BF16)],
        compiler_params=_cp(("parallel", "arbitrary")),
    )(pm, pm, pm, c_row)


def _fox2_bwd_dq(pm, do, c_row, lse, delta, name):
    T = pm.shape[0]
    bq = _pick(T, (512, 256))
    bk = bq
    nq = T // bq
    R = FOX_ROWS
    ng = bk // 128

    def body(q_ref, k_ref, v_ref, do_ref, ck_ref, lse_ref, dl_ref, dq_ref, dl2_ref,
             acc, e_s, s_scr, dp_scr, ds_scr):
        qi = pl.program_id(0)
        ki = pl.program_id(1)

        @pl.when(ki == 0)
        def _():
            acc[...] = jnp.zeros_like(acc)
            e_s[...] = jnp.zeros_like(e_s)

        def block(masked):
            qlo = _lane_lo((bq, 128))
            for p in range(4):
                sl = slice(128 * p, 128 * p + 128)
                qp = q_ref[:, sl] * ATT_SCALE
                kp = k_ref[:, sl]
                dop = do_ref[:, sl]
                ks = _masked_halves(kp)
                vs = _masked_halves(v_ref[:, sl])
                dqs = []
                for j in range(2):
                    h = 2 * p + j
                    s_scr[...] = _dot_nt(qp, ks[j])
                    dp_scr[...] = _dot_nt(dop, vs[j])

                    def chunk(r, carry, h=h):
                        r0 = _row_start(r, R)
                        rows = pl.ds(r0, R)
                        lse_c = lse_ref[rows, h:h + 1]
                        dl_c = dl_ref[rows, h:h + 1]
                        if masked:
                            rid = lax.broadcasted_iota(I32, (R, 128), 0) + r0
                            cid = lax.broadcasted_iota(I32, (R, 128), 1)
                        dss = []
                        for g in range(ng):
                            gs = slice(128 * g, 128 * g + 128)
                            sc = s_scr[rows, gs] - ck_ref[h:h + 1, gs]
                            if masked:
                                sc = jnp.where(cid + 128 * g <= rid, sc, NEG)
                            ds = jnp.exp(sc - lse_c) * (dp_scr[rows, gs] - dl_c)
                            ds_scr[rows, gs] = ds.astype(BF16)
                            dss.append(ds)
                        e_s[h, rows, :] += _tree(jnp.add, dss)
                        return carry

                    _chunk_loop(bq // R, chunk)
                    dqs.append(_dot_nn(ds_scr[...], kp))
                acc[:, sl] += jnp.where(qlo, dqs[0], dqs[1])

        @pl.when(ki < qi)
        def _():
            block(False)

        @pl.when(ki == qi)
        def _():
            block(True)

        @pl.when(ki == nq - 1)
        def _():
            dq_ref[...] = (acc[...] * ATT_SCALE).astype(BF16)
            out = dl_ref[...]
            for h in range(8):
                out = out + _put_col((bq, 128), h, jnp.sum(e_s[h], axis=-1, keepdims=True))
            dl2_ref[...] = out

    qb = pl.BlockSpec((bq, 128), lambda i, k: (i, 0))
    return pl.pallas_call(
        body, name=name, grid=(nq, nq),
        in_specs=[pl.BlockSpec((bq, BRANCH), lambda i, k: (i, CB_FQ)),
                  pl.BlockSpec((bk, BRANCH), lambda i, k: (jnp.minimum(k, i), CB_FK)),
                  pl.BlockSpec((bk, BRANCH), lambda i, k: (jnp.minimum(k, i), CB_FV)),
                  pl.BlockSpec((bq, BRANCH), lambda i, k: (i, 0)),
                  pl.BlockSpec((8, bk), lambda i, k: (0, jnp.minimum(k, i))), qb, qb],
        out_specs=[pl.BlockSpec((bq, BRANCH), lambda i, k: (i, 0)), qb],
        out_shape=[jax.ShapeDtypeStruct((T, BRANCH), BF16), jax.ShapeDtypeStruct((T, 128), F32)],
        scratch_shapes=[pltpu.VMEM((bq, BRANCH), F32), pltpu.VMEM((8, bq, 128), F32),
                        pltpu.VMEM((bq, bk), F32), pltpu.VMEM((bq, bk), F32), pltpu.VMEM((bq, bk), BF16)],
        compiler_params=_cp(("parallel", "arbitrary")),
    )(pm, pm, pm, do, c_row, lse, delta)


def _fox2_bwd_dkv(pm, do, c_col, lse_row, delta_row, name):
    T = pm.shape[0]
    bk = _pick(T, (512, 256))
    bq = bk
    nk = T // bk
    R = FOX_ROWS
    ng = bq // 128

    def body(q_ref, k_ref, v_ref, do_ref, ck_ref, lse_ref, dl_ref, dk_ref, dv_ref, dc_ref,
             dk_acc, dv_acc, dc_s, st_scr, dpt_scr, pt_scr, dst_scr):
        ki = pl.program_id(0)
        qi = pl.program_id(1)

        @pl.when(qi == 0)
        def _():
            dk_acc[...] = jnp.zeros_like(dk_acc)
            dv_acc[...] = jnp.zeros_like(dv_acc)
            dc_s[...] = jnp.zeros_like(dc_s)

        def block(masked):
            klo = _lane_lo((bk, 128))
            for p in range(4):
                sl = slice(128 * p, 128 * p + 128)
                qp = q_ref[:, sl]
                kp = k_ref[:, sl] * ATT_SCALE
                vp = v_ref[:, sl]
                dop = do_ref[:, sl]
                qs = _masked_halves(qp)
                dos = _masked_halves(dop)
                dks, dvs = [], []
                for j in range(2):
                    h = 2 * p + j
                    st_scr[...] = _dot_nt(kp, qs[j])
                    dpt_scr[...] = _dot_nt(vp, dos[j])

                    def chunk(r, carry, h=h):
                        r0 = _row_start(r, R)
                        rows = pl.ds(r0, R)
                        ck_c = ck_ref[rows, h:h + 1]
                        if masked:
                            kid = lax.broadcasted_iota(I32, (R, 128), 0) + r0
                            qid = lax.broadcasted_iota(I32, (R, 128), 1)
                        dss = []
                        for g in range(ng):
                            gs = slice(128 * g, 128 * g + 128)
                            st = st_scr[rows, gs] - (ck_c + lse_ref[h:h + 1, gs])
                            if masked:
                                st = jnp.where(kid <= qid + 128 * g, st, NEG)
                            pt = jnp.exp(st)
                            dst = pt * (dpt_scr[rows, gs] - dl_ref[h:h + 1, gs])
                            pt_scr[rows, gs] = pt.astype(BF16)
                            dst_scr[rows, gs] = dst.astype(BF16)
                            dss.append(dst)
                        dc_s[h, rows, :] -= _tree(jnp.add, dss)
                        return carry

                    _chunk_loop(bk // R, chunk)
                    dvs.append(_dot_nn(pt_scr[...], dop))
                    dks.append(_dot_nn(dst_scr[...], qp))
                dk_acc[:, sl] += jnp.where(klo, dks[0], dks[1])
                dv_acc[:, sl] += jnp.where(klo, dvs[0], dvs[1])

        @pl.when(qi > ki)
        def _():
            block(False)

        @pl.when(qi == ki)
        def _():
            block(True)

        @pl.when(qi == nk - 1)
        def _():
            dk_ref[...] = (dk_acc[...] * ATT_SCALE).astype(BF16)
            dv_ref[...] = dv_acc[...].astype(BF16)
            out = jnp.zeros((bk, 128), F32)
            for h in range(8):
                out = out + _put_col((bk, 128), h, jnp.sum(dc_s[h], axis=-1, keepdims=True))
            dc_ref[...] = out

    qrow = pl.BlockSpec((8, bq), lambda k, i: (0, jnp.maximum(i, k)))
    kb = pl.BlockSpec((bk, BRANCH), lambda k, i: (k, 0))
    return pl.pallas_call(
        body, name=name, grid=(nk, nk),
        in_specs=[pl.BlockSpec((bq, BRANCH), lambda k, i: (jnp.maximum(i, k), CB_FQ)),
                  pl.BlockSpec((bk, BRANCH), lambda k, i: (k, CB_FK)),
                  pl.BlockSpec((bk, BRANCH), lambda k, i: (k, CB_FV)),
                  pl.BlockSpec((bq, BRANCH), lambda k, i: (jnp.maximum(i, k), 0)),
                  pl.BlockSpec((bk, 128), lambda k, i: (k, 0)), qrow, qrow],
        out_specs=[kb, kb, pl.BlockSpec((bk, 128), lambda k, i: (k, 0))],
        out_shape=[jax.ShapeDtypeStruct((T, BRANCH), BF16), jax.ShapeDtypeStruct((T, BRANCH), BF16),
                   jax.ShapeDtypeStruct((T, 128), F32)],
        scratch_shapes=[pltpu.VMEM((bk, BRANCH), F32), pltpu.VMEM((bk, BRANCH), F32),
                        pltpu.VMEM((8, bk, 128), F32), pltpu.VMEM((bk, bq), F32), pltpu.VMEM((bk, bq), F32),
                        pltpu.VMEM((bk, bq), BF16), pltpu.VMEM((bk, bq), BF16)],
        compiler_params=_cp(("parallel", "arbitrary")),
    )(pm, pm, pm, do, c_col, lse_row, delta_row)


def _bucket_table():
    tq = np.arange(WINDOW, dtype=np.int32)[:, None]
    sk = np.arange(2 * WINDOW, dtype=np.int32)[None, :]
    n = np.maximum(WINDOW + tq - sk, 0)
    max_exact = N_BUCKETS // 2
    ratio = np.maximum(n, 1).astype(np.float32) / np.float32(max_exact)
    large = max_exact + (np.log(ratio) / np.float32(math.log(WINDOW / max_exact))
                         * np.float32(N_BUCKETS - max_exact)).astype(np.int32)
    large = np.minimum(large, N_BUCKETS - 1)
    return np.where(n < max_exact, n, large).astype(np.int32)


def _swa_bias(rel_bias, bucket, name):
    def body(rb_ref, bk_ref, o_ref):
        bkt = bk_ref[...]
        for h in range(8):
            def step(b, a):
                return a + jnp.where(bkt == b, rb_ref[b, h], 0.0)
            o_ref[h] = lax.fori_loop(0, N_BUCKETS, step, jnp.zeros(bkt.shape, F32))

    return pl.pallas_call(
        body, name=name,
        in_specs=[pl.BlockSpec(memory_space=pltpu.SMEM), pl.BlockSpec(memory_space=pltpu.VMEM)],
        out_specs=pl.BlockSpec(memory_space=pltpu.VMEM),
        out_shape=jax.ShapeDtypeStruct((8, WINDOW, 2 * WINDOW), F32),
    )(rel_bias, bucket)


def _swa_dbias_reduce(dbias, bucket, name):
    def body(d_ref, bk_ref, o_ref):
        bkt = bk_ref[...]
        rowi = lax.broadcasted_iota(I32, (N_BUCKETS, 128), 0)
        lane = lax.broadcasted_iota(I32, (N_BUCKETS, 128), 1)
        out = jnp.zeros((N_BUCKETS, 128), F32)
        for h in range(8):
            dv = d_ref[h]

            def step(b, a):
                tot = jnp.sum(jnp.where(bkt == b, dv, 0.0), keepdims=True)
                return a + jnp.where((rowi == b) & (lane == h), tot, 0.0)
            out = lax.fori_loop(0, N_BUCKETS, step, out)
        o_ref[...] = out

    return pl.pallas_call(
        body, name=name,
        in_specs=[pl.BlockSpec(memory_space=pltpu.VMEM), pl.BlockSpec(memory_space=pltpu.VMEM)],
        out_specs=pl.BlockSpec(memory_space=pltpu.VMEM),
        out_shape=jax.ShapeDtypeStruct((N_BUCKETS, 128), F32),
    )(dbias, bucket)


def _swap_halves(x):
    return pltpu.roll(x.astype(F32), HEAD_DIM, 1).astype(x.dtype)


def _kv_variants(t):
    lo = _lane_lo(t.shape)
    z = jnp.zeros_like(t)
    a0 = jnp.where(lo, t, z)
    b1 = jnp.where(lo, z, t)
    b0 = _swap_halves(a0)
    a1 = _swap_halves(b1)
    return (a0, a1), (b0, b1), (a0 + b0, a1 + b1)


def _swa_masks(i):
    tq = lax.broadcasted_iota(I32, (WINDOW, WINDOW), 0)
    jj = lax.broadcasted_iota(I32, (WINDOW, WINDOW), 1)
    return (jj > tq) & (i > 0), jj <= tq


def _swa_specs():
    q = pl.BlockSpec((WINDOW, BRANCH), lambda i: (i, CB_SQ))
    kc = pl.BlockSpec((WINDOW, 128), lambda i: (i, CB_SK))
    kp = pl.BlockSpec((WINDOW, 128), lambda i: (jnp.maximum(i - 1, 0), CB_SK))
    vc = pl.BlockSpec((WINDOW, 128), lambda i: (i, CB_SV))
    vp = pl.BlockSpec((WINDOW, 128), lambda i: (jnp.maximum(i - 1, 0), CB_SV))
    bias = pl.BlockSpec((8, WINDOW, 2 * WINDOW), lambda i: (0, 0, 0))
    vec = pl.BlockSpec((1, 128), lambda i: (0, 0))
    return q, kc, kp, vc, vp, bias, vec


def _swa_fwd(pm, bias, sink, name):
    T = pm.shape[0]
    nb = T // WINDOW

    def body(q_ref, kc_ref, kp_ref, vc_ref, vp_ref, b_ref, s_ref, o_ref, m_ref):
        i = pl.program_id(0)
        mprev, mcur = _swa_masks(i)
        kcA, kcB, _ = _kv_variants(kc_ref[...])
        kpA, kpB, _ = _kv_variants(kp_ref[...])
        _, _, vcD = _kv_variants(vc_ref[...])
        _, _, vpD = _kv_variants(vp_ref[...])
        lo = _lane_lo((WINDOW, 128))
        sink_v = s_ref[...]
        mout = jnp.zeros((WINDOW, 128), F32)
        for p in range(4):
            jv = p // 2
            sl = slice(128 * p, 128 * p + 128)
            qp = q_ref[:, sl] * ATT_SCALE
            outs = []
            for par in range(2):
                h = 2 * p + par
                kpx = (kpA, kpB)[par][jv]
                kcx = (kcA, kcB)[par][jv]
                sp = jnp.where(mprev, _dot_nt(qp, kpx) + b_ref[h, :, 0:WINDOW], NEG)
                sc = jnp.where(mcur, _dot_nt(qp, kcx) + b_ref[h, :, WINDOW:2 * WINDOW], NEG)
                sk_h = sink_v[:, h:h + 1]
                m = jnp.maximum(jnp.maximum(jnp.max(sp, axis=-1, keepdims=True),
                                            jnp.max(sc, axis=-1, keepdims=True)), sk_h)
                ep = jnp.exp(sp - m)
                ec = jnp.exp(sc - m)
                den = (jnp.sum(ep, axis=-1, keepdims=True) + jnp.sum(ec, axis=-1, keepdims=True)
                       + jnp.exp(sk_h - m))
                inv = 1.0 / den
                outs.append(_dot_nn((ep * inv).astype(BF16), vpD[jv])
                            + _dot_nn((ec * inv).astype(BF16), vcD[jv]))
                mout = mout + _put_col((WINDOW, 128), h, m + jnp.log(den))
            o_ref[:, sl] = jnp.where(lo, outs[0], outs[1]).astype(BF16)
        m_ref[...] = mout

    q, kc, kp, vc, vp, bs, vec = _swa_specs()
    return pl.pallas_call(
        body, name=name, grid=(nb,),
        in_specs=[q, kc, kp, vc, vp, bs, vec],
        out_specs=[pl.BlockSpec((WINDOW, BRANCH), lambda i: (i, 0)),
                   pl.BlockSpec((WINDOW, 128), lambda i: (i, 0))],
        out_shape=[jax.ShapeDtypeStruct((T, BRANCH), BF16), jax.ShapeDtypeStruct((T, 128), F32)],
        compiler_params=_cp(("parallel",)),
    )(pm, pm, pm, pm, pm, bias, sink)


def _swa_bwd(pm, bias, sink, do, mlse, name):
    T = pm.shape[0]
    nb = T // WINDOW

    def fold(zz):
        return zz + pltpu.roll(zz, HEAD_DIM, 1)

    def body(q_ref, kc_ref, kp_ref, vc_ref, vp_ref, b_ref, s_ref, do_ref, m_ref,
             dq_ref, dkc_ref, dkp_ref, dvc_ref, dvp_ref, db_ref, ds_ref):
        i = pl.program_id(0)

        @pl.when(i == 0)
        def _():
            db_ref[...] = jnp.zeros_like(db_ref)
            ds_ref[...] = jnp.zeros_like(ds_ref)

        mprev, mcur = _swa_masks(i)
        kcA, kcB, kcD = _kv_variants(kc_ref[...])
        kpA, kpB, kpD = _kv_variants(kp_ref[...])
        vcA, vcB, _ = _kv_variants(vc_ref[...])
        vpA, vpB, _ = _kv_variants(vp_ref[...])
        lo = _lane_lo((WINDOW, 128))
        sink_v = s_ref[...]
        mv = m_ref[...]
        zk = jnp.zeros((WINDOW, 128), F32)
        zkp, zkc, zvp, zvc = [zk, zk], [zk, zk], [zk, zk], [zk, zk]
        dsink = jnp.zeros((1, 128), F32)
        for p in range(4):
            jv = p // 2
            sl = slice(128 * p, 128 * p + 128)
            qraw = q_ref[:, sl]
            qp = qraw * ATT_SCALE
            dop = do_ref[:, sl]
            dqs, mkp, mkc, mvp, mvc = [], [], [], [], []
            for par in range(2):
                h = 2 * p + par
                kpx = (kpA, kpB)[par][jv]
                kcx = (kcA, kcB)[par][jv]
                vpx = (vpA, vpB)[par][jv]
                vcx = (vcA, vcB)[par][jv]
                sp = jnp.where(mprev, _dot_nt(qp, kpx) + b_ref[h, :, 0:WINDOW], NEG)
                sc = jnp.where(mcur, _dot_nt(qp, kcx) + b_ref[h, :, WINDOW:2 * WINDOW], NEG)
                m_h = mv[:, h:h + 1]
                pp = jnp.exp(sp - m_h)
                pc = jnp.exp(sc - m_h)
                psink = jnp.exp(sink_v[:, h:h + 1] - m_h)
                dpp = _dot_nt(dop, vpx)
                dpc = _dot_nt(dop, vcx)
                delta = jnp.sum(pp * dpp, axis=-1, keepdims=True) + jnp.sum(pc * dpc, axis=-1, keepdims=True)
                dsp = pp * (dpp - delta)
                dsc = pc * (dpc - delta)
                db_ref[h, :, 0:WINDOW] += dsp
                db_ref[h, :, WINDOW:2 * WINDOW] += dsc
                dsink = dsink - _put_col((1, 128), h, jnp.sum(psink * delta, keepdims=True))
                dsp_b = dsp.astype(BF16)
                dsc_b = dsc.astype(BF16)
                dqs.append(_dot_nn(dsp_b, kpD[jv]) + _dot_nn(dsc_b, kcD[jv]))
                mkp.append(_dot_tn(dsp_b, qraw))
                mkc.append(_dot_tn(dsc_b, qraw))
                mvp.append(_dot_tn(pp.astype(BF16), dop))
                mvc.append(_dot_tn(pc.astype(BF16), dop))
            dq_ref[:, sl] = (jnp.where(lo, dqs[0], dqs[1]) * ATT_SCALE).astype(BF16)
            zkp[jv] = zkp[jv] + jnp.where(lo, mkp[0], mkp[1])
            zkc[jv] = zkc[jv] + jnp.where(lo, mkc[0], mkc[1])
            zvp[jv] = zvp[jv] + jnp.where(lo, mvp[0], mvp[1])
            zvc[jv] = zvc[jv] + jnp.where(lo, mvc[0], mvc[1])
        dkc_ref[...] = jnp.where(lo, fold(zkc[0]), fold(zkc[1])) * ATT_SCALE
        dkp_ref[...] = jnp.where(lo, fold(zkp[0]), fold(zkp[1])) * ATT_SCALE
        dvc_ref[...] = jnp.where(lo, fold(zvc[0]), fold(zvc[1]))
        dvp_ref[...] = jnp.where(lo, fold(zvp[0]), fold(zvp[1]))
        ds_ref[...] += dsink

    q, kc, kp, vc, vp, bs, vec = _swa_specs()
    own = pl.BlockSpec((WINDOW, BRANCH), lambda i: (i, 0))
    sm = pl.BlockSpec((WINDOW, 128), lambda i: (i, 0))
    f128 = jax.ShapeDtypeStruct((T, 128), F32)
    return pl.pallas_call(
        body, name=name, grid=(nb,),
        in_specs=[q, kc, kp, vc, vp, bs, vec, own, sm],
        out_specs=[own, sm, sm, sm, sm, bs, vec],
        out_shape=[jax.ShapeDtypeStruct((T, BRANCH), BF16), f128, f128, f128, f128,
                   jax.ShapeDtypeStruct((8, WINDOW, 2 * WINDOW), F32), jax.ShapeDtypeStruct((1, 128), F32)],
        compiler_params=_cp(("arbitrary",)),
    )(pm, pm, pm, pm, pm, bias, sink, do, mlse)


def _merge_fwd(pm, us, name):
    T = pm.shape[0]
    bt = _pick(T, (512, 256))

    def body(g0, g1, g2, u0, u1, u2, o_ref):
        acc = jax.nn.sigmoid(g0[...].astype(F32)) * u0[...].astype(F32)
        acc = acc + jax.nn.sigmoid(g1[...].astype(F32)) * u1[...].astype(F32)
        acc = acc + jax.nn.sigmoid(g2[...].astype(F32)) * u2[...].astype(F32)
        o_ref[...] = acc.astype(BF16)

    own = pl.BlockSpec((bt, D_MODEL), lambda i: (i, 0))
    gs = [pl.BlockSpec((bt, D_MODEL), lambda i, cb=cb: (i, cb)) for cb in CB_GATE]
    return pl.pallas_call(
        body, name=name, grid=(T // bt,), in_specs=gs + [own, own, own], out_specs=own,
        out_shape=jax.ShapeDtypeStruct((T, D_MODEL), BF16),
        compiler_params=_cp(("parallel",)),
    )(pm, pm, pm, *us)


def _merge_bwd(pm, us, dm, name):
    T = pm.shape[0]
    bt = _pick(T, (256,))

    def body(g0, g1, g2, u0, u1, u2, dm_ref, du0, du1, du2, dg_ref):
        dmv = dm_ref[...].astype(F32)
        for b, (g, u, du) in enumerate(((g0, u0, du0), (g1, u1, du1), (g2, u2, du2))):
            s = jax.nn.sigmoid(g[...].astype(F32))
            du[...] = (dmv * s).astype(BF16)
            dg_ref[:, D_MODEL * b:D_MODEL * (b + 1)] = (dmv * u[...].astype(F32) * s * (1.0 - s)).astype(BF16)

    own = pl.BlockSpec((bt, D_MODEL), lambda i: (i, 0))
    gs = [pl.BlockSpec((bt, D_MODEL), lambda i, cb=cb: (i, cb)) for cb in CB_GATE]
    act = jax.ShapeDtypeStruct((T, D_MODEL), BF16)
    return pl.pallas_call(
        body, name=name, grid=(T // bt,), in_specs=gs + [own, own, own, own],
        out_specs=[own, own, own, pl.BlockSpec((bt, 3 * D_MODEL), lambda i: (i, 0))],
        out_shape=[act, act, act, jax.ShapeDtypeStruct((T, 3 * D_MODEL), BF16)],
        compiler_params=_cp(("parallel",)),
    )(pm, pm, pm, *us, dm)


def _swiglu_fwd(ab, name):
    T = ab.shape[0]
    bt = _pick(T, (512, 256))

    def body(a_ref, b_ref, o_ref):
        a = a_ref[...].astype(F32)
        o_ref[...] = (a * jax.nn.sigmoid(a) * b_ref[...].astype(F32)).astype(BF16)

    return pl.pallas_call(
        body, name=name, grid=(T // bt,),
        in_specs=[pl.BlockSpec((bt, D_FF), lambda i: (i, 0)), pl.BlockSpec((bt, D_FF), lambda i: (i, 1))],
        out_specs=pl.BlockSpec((bt, D_FF), lambda i: (i, 0)),
        out_shape=jax.ShapeDtypeStruct((T, D_FF), BF16),
        compiler_params=_cp(("parallel",)),
    )(ab, ab)


def _swiglu_bwd(ab, dh, name):
    T = ab.shape[0]
    bt = _pick(T, (256,))

    def body(a_ref, b_ref, d_ref, o_ref):
        a = a_ref[...].astype(F32)
        b = b_ref[...].astype(F32)
        d = d_ref[...].astype(F32)
        s = jax.nn.sigmoid(a)
        o_ref[:, 0:D_FF] = (d * b * (s + a * s * (1.0 - s))).astype(BF16)
        o_ref[:, D_FF:2 * D_FF] = (d * a * s).astype(BF16)

    return pl.pallas_call(
        body, name=name, grid=(T // bt,),
        in_specs=[pl.BlockSpec((bt, D_FF), lambda i: (i, 0)), pl.BlockSpec((bt, D_FF), lambda i: (i, 1)),
                  pl.BlockSpec((bt, D_FF), lambda i: (i, 0))],
        out_specs=pl.BlockSpec((bt, 2 * D_FF), lambda i: (i, 0)),
        out_shape=jax.ShapeDtypeStruct((T, 2 * D_FF), BF16),
        compiler_params=_cp(("parallel",)),
    )(ab, ab, dh)


def _xattn_probs(q_ref, kv_ref, h):
    sl = slice(X_HEAD_DIM * h, X_HEAD_DIM * (h + 1))
    qh = q_ref[:, sl]
    kh = kv_ref[:, sl]
    vh = kv_ref[:, D_MODEL + X_HEAD_DIM * h:D_MODEL + X_HEAD_DIM * (h + 1)]
    s = _dot_nt(qh, kh) * X_SCALE
    e = jnp.exp(s - jnp.max(s, axis=-1, keepdims=True))
    return qh, kh, vh, e * (1.0 / jnp.sum(e, axis=-1, keepdims=True))


def _xattn_fwd(q, kv, name):
    T = q.shape[0]
    bq = _pick(T, (512, 256))

    def body(q_ref, kv_ref, o_ref):
        for h in range(X_HEADS):
            _, _, vh, p = _xattn_probs(q_ref, kv_ref, h)
            o_ref[:, X_HEAD_DIM * h:X_HEAD_DIM * (h + 1)] = _dot_nn(p.astype(BF16), vh).astype(BF16)

    own = pl.BlockSpec((bq, D_MODEL), lambda i: (i, 0))
    return pl.pallas_call(
        body, name=name, grid=(T // bq,),
        in_specs=[own, pl.BlockSpec((MEM_LEN, 2 * D_MODEL), lambda i: (0, 0))], out_specs=own,
        out_shape=jax.ShapeDtypeStruct((T, D_MODEL), BF16),
        compiler_params=_cp(("parallel",)),
    )(q, kv)


def _xattn_bwd(q, kv, do, name):
    T = q.shape[0]
    bq = _pick(T, (512, 256))

    def body(q_ref, kv_ref, do_ref, dq_ref, dkv_ref):
        @pl.when(pl.program_id(0) == 0)
        def _():
            dkv_ref[...] = jnp.zeros_like(dkv_ref)

        for h in range(X_HEADS):
            sl = slice(X_HEAD_DIM * h, X_HEAD_DIM * (h + 1))
            qh, kh, vh, p = _xattn_probs(q_ref, kv_ref, h)
            doh = do_ref[:, sl]
            dp = _dot_nt(doh, vh)
            ds = (p * (dp - jnp.sum(p * dp, axis=-1, keepdims=True)) * X_SCALE).astype(BF16)
            dq_ref[:, sl] = _dot_nn(ds, kh).astype(BF16)
            dkv_ref[:, sl] += _dot_tn(ds, qh)
            dkv_ref[:, D_MODEL + X_HEAD_DIM * h:D_MODEL + X_HEAD_DIM * (h + 1)] += _dot_tn(p.astype(BF16), doh)

    own = pl.BlockSpec((bq, D_MODEL), lambda i: (i, 0))
    kvs = pl.BlockSpec((MEM_LEN, 2 * D_MODEL), lambda i: (0, 0))
    return pl.pallas_call(
        body, name=name, grid=(T // bq,), in_specs=[own, kvs, own], out_specs=[own, kvs],
        out_shape=[jax.ShapeDtypeStruct((T, D_MODEL), BF16), jax.ShapeDtypeStruct((MEM_LEN, 2 * D_MODEL), F32)],
        compiler_params=_cp(("arbitrary",)),
    )(q, kv, do)


def _adamw(w, g, m, v, name):
    R, C = w.shape
    cpad = -(-C // 128) * 128
    bt = R
    for cand in (1024, 512, 256, 128, 64, 32, 16, 8):
        if R % cand == 0 and cand * cpad * 4 <= (1 << 20):
            bt = cand
            break

    def body(w_ref, g_ref, m_ref, v_ref, d_ref, nm_ref, nv_ref):
        gv = g_ref[...]
        mn = ADAM_B1 * m_ref[...] + (1.0 - ADAM_B1) * gv
        vn = ADAM_B2 * v_ref[...] + (1.0 - ADAM_B2) * (gv * gv)
        m_hat = mn / (1.0 - ADAM_B1 ** ADAM_STEP)
        v_hat = vn / (1.0 - ADAM_B2 ** ADAM_STEP)
        d_ref[...] = -ADAM_LR * (m_hat / (jnp.sqrt(v_hat) + ADAM_EPS) + ADAM_WD * w_ref[...])
        nm_ref[...] = mn
        nv_ref[...] = vn

    blk = pl.BlockSpec((bt, C), lambda i: (i, 0))
    out = jax.ShapeDtypeStruct((R, C), F32)
    return pl.pallas_call(
        body, name=name, grid=(R // bt,), in_specs=[blk] * 4, out_specs=[blk] * 3,
        out_shape=[out, out, out], compiler_params=_cp(("parallel",)),
    )(w, g, m, v)


ANY = pl.BlockSpec(memory_space=pl.ANY)


def _place():
    x, y, c = lax.axis_index("x"), lax.axis_index("y"), lax.axis_index("c")
    chips = [(1 - x, y), (x, 1 - y), (1 - x, 1 - y)]
    return x, y, c, chips


def _ag_packs(pack):
    R, Wd = pack.shape
    hrows = R // 2

    def body(p_ref, o_ref, send_sems, recv_sems, local_sem):
        x, y, c, chips = _place()
        me = 2 * x + y
        mine = pl.ds(c * hrows, hrows)
        theirs = pl.ds((1 - c) * hrows, hrows)
        local = pltpu.make_async_copy(p_ref, o_ref.at[me], local_sem)
        local.start()

        def copy(k, slab, rows, to, src=None):
            dst = o_ref.at[slab, rows]
            return pltpu.make_async_remote_copy(
                src_ref=dst if src is None else src, dst_ref=dst,
                send_sem=send_sems.at[k], recv_sem=recv_sems.at[k], device_id=to, device_id_type=MESH)

        first = [copy(k, me, mine, (px, py, c), src=p_ref.at[mine]) for k, (px, py) in enumerate(chips)]
        for cp in first:
            cp.start()
        passed = [copy(3 + k, 2 * px + py, mine, (x, y, 1 - c)) for k, (px, py) in enumerate(chips)]
        for k, (px, py) in enumerate(chips):
            copy(k, 2 * px + py, mine, (x, y, c)).wait_recv()
            passed[k].start()
        for k, (px, py) in enumerate(chips):
            copy(3 + k, 2 * px + py, theirs, (x, y, c)).wait_recv()
        for cp in first + passed:
            cp.wait_send()
        local.wait()

    return pl.pallas_call(
        body, name="ag_weights", in_specs=[ANY], out_specs=ANY,
        out_shape=jax.ShapeDtypeStruct((4, R, Wd), pack.dtype),
        scratch_shapes=[pltpu.SemaphoreType.DMA((6,)), pltpu.SemaphoreType.DMA((6,)), pltpu.SemaphoreType.DMA],
    )(pack)


def _rs_sibling(g4):
    _, R, Wd = g4.shape
    hrows = R // 2

    def body(g_ref, o_ref, send_sem, recv_sem):
        x, y, c, _ = _place()
        cp = pltpu.make_async_remote_copy(
            src_ref=g_ref.at[:, pl.ds((1 - c) * hrows, hrows)], dst_ref=o_ref,
            send_sem=send_sem, recv_sem=recv_sem, device_id=(x, y, 1 - c), device_id_type=MESH)
        cp.start()
        cp.wait()

    return pl.pallas_call(
        body, name="rs_sibling", in_specs=[ANY], out_specs=ANY,
        out_shape=jax.ShapeDtypeStruct((4, hrows, Wd), g4.dtype),
        scratch_shapes=[pltpu.SemaphoreType.DMA, pltpu.SemaphoreType.DMA],
    )(g4)


def _rs_add_pair(g4, sib, cidx):
    _, R, Wd = g4.shape
    hrows = R // 2
    bt = _pick(hrows, (512, 256, 16))
    nb = hrows // bt

    def body(c_ref, a_ref, b_ref, o_ref):
        o_ref[...] = (a_ref[...].astype(F32) + b_ref[...].astype(F32)).astype(o_ref.dtype)

    grid_spec = pltpu.PrefetchScalarGridSpec(
        num_scalar_prefetch=1, grid=(4, nb),
        in_specs=[pl.BlockSpec((1, bt, Wd), lambda j, i, c: (j, c[0] * nb + i, 0)),
                  pl.BlockSpec((1, bt, Wd), lambda j, i, c: (j, i, 0))],
        out_specs=pl.BlockSpec((1, bt, Wd), lambda j, i, c: (j, i, 0)))
    return pl.pallas_call(
        body, name="rs_add_pair", grid_spec=grid_spec,
        out_shape=jax.ShapeDtypeStruct((4, hrows, Wd), g4.dtype),
        compiler_params=_cp(("parallel", "parallel")),
    )(cidx, g4, sib)


def _rs_chips(r4):
    _, hrows, Wd = r4.shape

    def body(r_ref, o_ref, send_sems, recv_sems, local_sem):
        x, y, c, chips = _place()
        me = 2 * x + y
        local = pltpu.make_async_copy(r_ref.at[me], o_ref.at[me], local_sem)
        local.start()
        sends = []
        for k, (px, py) in enumerate(chips):
            sends.append(pltpu.make_async_remote_copy(
                src_ref=r_ref.at[2 * px + py], dst_ref=o_ref.at[me],
                send_sem=send_sems.at[k], recv_sem=recv_sems.at[k], device_id=(px, py, c), device_id_type=MESH))
        for cp in sends:
            cp.start()
        for k, (px, py) in enumerate(chips):
            pltpu.make_async_remote_copy(
                src_ref=r_ref.at[me], dst_ref=o_ref.at[2 * px + py],
                send_sem=send_sems.at[k], recv_sem=recv_sems.at[k], device_id=(x, y, c),
                device_id_type=MESH).wait_recv()
        for cp in sends:
            cp.wait_send()
        local.wait()

    return pl.pallas_call(
        body, name="rs_chips", in_specs=[ANY], out_specs=ANY,
        out_shape=jax.ShapeDtypeStruct((4, hrows, Wd), r4.dtype),
        scratch_shapes=[pltpu.SemaphoreType.DMA((3,)), pltpu.SemaphoreType.DMA((3,)), pltpu.SemaphoreType.DMA],
    )(r4)


def _rs_add_chips(q4):
    _, hrows, Wd = q4.shape
    bt = _pick(hrows, (240, 120, 16))

    def body(q_ref, o_ref):
        o_ref[...] = ((q_ref[0].astype(F32) + q_ref[1].astype(F32)) + q_ref[2].astype(F32)) + q_ref[3].astype(F32)

    return pl.pallas_call(
        body, name="rs_add_chips", grid=(hrows // bt,),
        in_specs=[pl.BlockSpec((4, bt, Wd), lambda i: (0, i, 0))],
        out_specs=pl.BlockSpec((bt, Wd), lambda i: (i, 0)),
        out_shape=jax.ShapeDtypeStruct((hrows, Wd), F32),
        compiler_params=_cp(("parallel",)),
    )(q4)


def _rs_share(buf):
    R, Wd = buf.shape
    hrows = R // 2

    def body(b_ref, o_ref, send_sem, recv_sem):
        del b_ref
        x, y, c, _ = _place()
        mine = o_ref.at[pl.ds(c * hrows, hrows)]
        cp = pltpu.make_async_remote_copy(
            src_ref=mine, dst_ref=mine, send_sem=send_sem, recv_sem=recv_sem,
            device_id=(x, y, 1 - c), device_id_type=MESH)
        cp.start()
        theirs = o_ref.at[pl.ds((1 - c) * hrows, hrows)]
        pltpu.make_async_remote_copy(
            src_ref=theirs, dst_ref=theirs, send_sem=send_sem, recv_sem=recv_sem,
            device_id=(x, y, c), device_id_type=MESH).wait_recv()
        cp.wait_send()

    return pl.pallas_call(
        body, name="rs_share", in_specs=[ANY], out_specs=ANY, input_output_aliases={0: 0},
        out_shape=jax.ShapeDtypeStruct((R, Wd), buf.dtype),
        scratch_shapes=[pltpu.SemaphoreType.DMA, pltpu.SemaphoreType.DMA],
    )(buf)


def _allreduce_small(v):
    R, Wd = v.shape

    def body(v_ref, o_ref, buf, send_sems, recv_sems):
        x, y, c, _ = _place()
        me = 4 * x + 2 * y + c
        buf[me] = v_ref[...]
        sends = []
        for k in range(1, 8):
            peer = ((x + (k >> 2)) % 2, (y + ((k >> 1) & 1)) % 2, (c + (k & 1)) % 2)
            sends.append(pltpu.make_async_remote_copy(
                src_ref=v_ref, dst_ref=buf.at[me], send_sem=send_sems.at[k - 1], recv_sem=recv_sems.at[k - 1],
                device_id=peer, device_id_type=MESH))
        for cp in sends:
            cp.start()
        for k in range(1, 8):
            px, py, pc = (x + (k >> 2)) % 2, (y + ((k >> 1) & 1)) % 2, (c + (k & 1)) % 2
            pltpu.make_async_remote_copy(
                src_ref=v_ref, dst_ref=buf.at[4 * px + 2 * py + pc], send_sem=send_sems.at[k - 1],
                recv_sem=recv_sems.at[k - 1], device_id=(x, y, c), device_id_type=MESH).wait_recv()
        acc = buf[0]
        for d in range(1, 8):
            acc = acc + buf[d]
        o_ref[...] = acc
        for cp in sends:
            cp.wait_send()

    vm = pl.BlockSpec(memory_space=pltpu.VMEM)
    return pl.pallas_call(
        body, name="allreduce_small", in_specs=[vm], out_specs=vm,
        out_shape=jax.ShapeDtypeStruct((R, Wd), F32),
        scratch_shapes=[pltpu.VMEM((8, R, Wd), F32), pltpu.SemaphoreType.DMA((7,)), pltpu.SemaphoreType.DMA((7,))],
    )(v)


def _neighbours():
    x, y, c = lax.axis_index("x"), lax.axis_index("y"), lax.axis_index("c")
    idx = (2 * x + y, 2 * (1 - x) + y, 2 * x + (1 - y), 2 * (1 - x) + (1 - y))
    return idx, (x, y, c), (1 - x, y, c), (x, 1 - y, c), (x, y, 1 - c)


def _place_own(pack, me_idx):
    R, Wd = pack.shape
    bt = _pick(R, (512, 256))

    def body(i_ref, p_ref, o_ref):
        o_ref[0] = p_ref[...]

    grid_spec = pltpu.PrefetchScalarGridSpec(
        num_scalar_prefetch=1, grid=(R // bt,),
        in_specs=[pl.BlockSpec((bt, Wd), lambda i, idx: (i, 0))],
        out_specs=pl.BlockSpec((1, bt, Wd), lambda i, idx: (idx[0], i, 0)))
    return pl.pallas_call(
        body, name="place_own", grid_spec=grid_spec,
        out_shape=jax.ShapeDtypeStruct((4, R, Wd), pack.dtype),
        compiler_params=_cp(("parallel",)),
    )(me_idx, pack)


def _ag_ring(buf):
    _, R, Wd = buf.shape
    hrows = R // 2
    qrows = hrows // 2

    def body(b_ref, o_ref, send_sems, recv_sems):
        del b_ref
        (me, ix, iy, idg), here, xn, yn, sib = _neighbours()
        c = here[2]
        base = c * hrows
        half = pl.ds(base, hrows)
        q0 = pl.ds(base, qrows)
        q1 = pl.ds(base + qrows, qrows)
        obase = (1 - c) * hrows

        def copy(k, slab, rows, to):
            dst = o_ref.at[slab, rows]
            return pltpu.make_async_remote_copy(
                src_ref=dst, dst_ref=dst,
                send_sem=send_sems.at[k], recv_sem=recv_sems.at[k], device_id=to, device_id_type=MESH)

        sends = [copy(0, me, half, xn), copy(1, me, half, yn)]
        for cp in sends:
            cp.start()
        landed = [(0, ix, half), (1, iy, half), (2, idg, q0), (3, idg, q1)]
        onward = {0: copy(2, ix, q0, yn), 1: copy(3, iy, q1, xn)}
        for k, slab, rows in landed:
            copy(k, slab, rows, here).wait_recv()
            if k in onward:
                onward[k].start()
                sends.append(onward[k])
            cp = copy(4 + k, slab, rows, sib)
            cp.start()
            sends.append(cp)
        theirs = [(4, ix, pl.ds(obase, hrows)), (5, iy, pl.ds(obase, hrows)),
                  (6, idg, pl.ds(obase, qrows)), (7, idg, pl.ds(obase + qrows, qrows))]
        for k, slab, rows in theirs:
            copy(k, slab, rows, here).wait_recv()
        for cp in sends:
            cp.wait_send()

    return pl.pallas_call(
        body, name="ag_weights", in_specs=[ANY], out_specs=ANY, input_output_aliases={0: 0},
        out_shape=jax.ShapeDtypeStruct((4, R, Wd), buf.dtype),
        scratch_shapes=[pltpu.SemaphoreType.DMA((8,)), pltpu.SemaphoreType.DMA((8,))],
    )(buf)


def _rs_diag(r4):
    _, hrows, Wd = r4.shape
    qrows = hrows // 2

    def body(r_ref, o_ref, send_sems, recv_sems):
        (me, ix, iy, idg), here, xn, yn, sib = _neighbours()
        pieces = [(0, pl.ds(0, qrows), xn), (1, pl.ds(qrows, qrows), yn)]
        sends = [pltpu.make_async_remote_copy(
            src_ref=r_ref.at[idg, rows], dst_ref=o_ref.at[k], send_sem=send_sems.at[k],
            recv_sem=recv_sems.at[k], device_id=to, device_id_type=MESH) for k, rows, to in pieces]
        for cp in sends:
            cp.start()
        for k, rows, to in pieces:
            pltpu.make_async_remote_copy(
                src_ref=r_ref.at[idg, rows], dst_ref=o_ref.at[k], send_sem=send_sems.at[k],
                recv_sem=recv_sems.at[k], device_id=here, device_id_type=MESH).wait_recv()
        for cp in sends:
            cp.wait_send()

    return pl.pallas_call(
        body, name="rs_diag", in_specs=[ANY], out_specs=ANY,
        out_shape=jax.ShapeDtypeStruct((2, qrows, Wd), r4.dtype),
        scratch_shapes=[pltpu.SemaphoreType.DMA((2,)), pltpu.SemaphoreType.DMA((2,))],
    )(r4)


def _rs_merge(r4, dg, nbr_idx):
    _, hrows, Wd = r4.shape
    bt = _pick(hrows // 2, (256, 128, 16))
    nb = hrows // bt
    nq = nb // 2

    def body(i_ref, r_ref, d_ref, o_ref):
        w = pl.program_id(0)
        i = pl.program_id(1)
        merged = jnp.where(w == 0, i >= nq, i < nq)
        add = jnp.where(merged, d_ref[...].astype(F32), 0.0)
        o_ref[...] = (r_ref[...].astype(F32) + add).astype(o_ref.dtype)

    grid_spec = pltpu.PrefetchScalarGridSpec(
        num_scalar_prefetch=1, grid=(2, nb),
        in_specs=[pl.BlockSpec((1, bt, Wd), lambda w, i, idx: (idx[w], i, 0)),
                  pl.BlockSpec((1, bt, Wd), lambda w, i, idx: (1 - w, jnp.clip(i - (1 - w) * nq, 0, nq - 1), 0))],
        out_specs=pl.BlockSpec((1, bt, Wd), lambda w, i, idx: (w, i, 0)))
    return pl.pallas_call(
        body, name="rs_merge", grid_spec=grid_spec,
        out_shape=jax.ShapeDtypeStruct((2, hrows, Wd), r4.dtype),
        compiler_params=_cp(("parallel", "parallel")),
    )(nbr_idx, r4, dg)


def _rs_direct(m2):
    _, hrows, Wd = m2.shape

    def body(m_ref, o_ref, send_sems, recv_sems):
        _, here, xn, yn, sib = _neighbours()
        sends = [pltpu.make_async_remote_copy(
            src_ref=m_ref.at[k], dst_ref=o_ref.at[k], send_sem=send_sems.at[k], recv_sem=recv_sems.at[k],
            device_id=to, device_id_type=MESH) for k, to in ((0, xn), (1, yn))]
        for cp in sends:
            cp.start()
        for k in range(2):
            pltpu.make_async_remote_copy(
                src_ref=m_ref.at[k], dst_ref=o_ref.at[k], send_sem=send_sems.at[k], recv_sem=recv_sems.at[k],
                device_id=here, device_id_type=MESH).wait_recv()
        for cp in sends:
            cp.wait_send()

    return pl.pallas_call(
        body, name="rs_direct", in_specs=[ANY], out_specs=ANY,
        out_shape=jax.ShapeDtypeStruct((2, hrows, Wd), m2.dtype),
        scratch_shapes=[pltpu.SemaphoreType.DMA((2,)), pltpu.SemaphoreType.DMA((2,))],
    )(m2)


def _rs_final(r4, got, me_c):
    _, hrows, Wd = r4.shape
    bt = _pick(hrows, (512, 256, 16))
    nb = hrows // bt

    def body(i_ref, r_ref, g_ref, o_ref):
        o_ref[...] = (r_ref[0].astype(F32) + g_ref[0].astype(F32)) + g_ref[1].astype(F32)

    grid_spec = pltpu.PrefetchScalarGridSpec(
        num_scalar_prefetch=1, grid=(nb,),
        in_specs=[pl.BlockSpec((1, bt, Wd), lambda i, idx: (idx[0], i, 0)),
                  pl.BlockSpec((2, bt, Wd), lambda i, idx: (0, i, 0))],
        out_specs=pl.BlockSpec((bt, Wd), lambda i, idx: (idx[1] * nb + i, 0)))
    return pl.pallas_call(
        body, name="rs_final", grid_spec=grid_spec,
        out_shape=jax.ShapeDtypeStruct((2 * hrows, Wd), F32),
        compiler_params=_cp(("parallel",)),
    )(me_c, r4, got)


SHARDED = (
    ("w_in", (2, 1024, 1730), 2),
    ("w_branch", (2, 3, 512, 256), 3),
    ("w_mix_out", (2, 256, 1024), 1),
    ("w_xq", (2, 256, 1024), 1),
    ("w_xkv", (2, 1024, 512), 2),
    ("w_xo", (2, 256, 1024), 1),
    ("w_ffn_gate", (2, 1024, 704), 2),
    ("w_ffn_up", (2, 1024, 704), 2),
    ("w_ffn_down", (2, 704, 1024), 1),
    ("conv_w", (2, 3, 128), 2),
)
PACK_W = 1024
PACK_ELEMS = sum(int(np.prod(s)) for _, s, _ in SHARDED)
PACK_ROWS = -(-PACK_ELEMS // (PACK_W * 1024)) * 1024


def _pack(parts, dtype):
    flat = jnp.concatenate([p.astype(dtype).reshape(-1) for p in parts]
                           + [jnp.zeros((PACK_ROWS * PACK_W - PACK_ELEMS,), dtype)])
    return flat.reshape(PACK_ROWS, PACK_W)


def _unpack(pack):
    flat = pack.reshape(-1)
    out, off = {}, 0
    for name, shape, _ in SHARDED:
        n = int(np.prod(shape))
        out[name] = flat[off:off + n].reshape(shape)
        off += n
    return out


SMALL = (
    ("mix_norm_g", (2, 1024)), ("xattn_norm_g", (2, 1024)), ("mem_norm_g", (2, 1024)),
    ("ffn_norm_g", (2, 1024)), ("final_norm_g", (1024,)),
    ("forget_bias", (2, 8)), ("sink", (2, 8)), ("rel_bias", (32, 8)),
)
SMALL_ROWS = 112


def _pack_small(vals):
    rows = []
    for name, shape in SMALL:
        v = vals[name].astype(F32)
        if shape[-1] == 1024:
            rows.append(v.reshape(-1, 128))
        else:
            rows.append(jnp.pad(v, ((0, 0), (0, 120))))
    rows = jnp.concatenate(rows, axis=0)
    return jnp.pad(rows, ((0, SMALL_ROWS - rows.shape[0]), (0, 0)))


def _unpack_small(pack):
    out, off = {}, 0
    for name, shape in SMALL:
        if shape[-1] == 1024:
            n = int(np.prod(shape)) // 128
            out[name] = pack[off:off + n].reshape(shape)
        else:
            n = shape[0]
            out[name] = pack[off:off + n, 0:8]
        off += n
    return out


W_IN_PERM = ((3848, 6920), (0, 3072), (3080, 3848), (3072, 3080))


def _perm_w_in(w):
    parts = [w[:, a:b] for a, b in W_IN_PERM]
    return jnp.concatenate(parts + [jnp.zeros((w.shape[0], PROJ_PAD - IN_COLS), w.dtype)], axis=1)


def _unperm_w_in(p):
    return jnp.concatenate([p[:, 3072:6144], p[:, 6912:6920], p[:, 6144:6912], p[:, 0:3072]], axis=1)


def _pad_row8(v):
    return jnp.pad(v.astype(F32).reshape(1, 8), ((0, 0), (0, 120)))


def _local_step(x, mem, tgt, W, rel_bias):
    T = x.shape[0]
    bucket = jnp.asarray(_bucket_table())
    bias = _swa_bias(rel_bias, bucket, "swa_bias")
    saved = []
    for l in range(DEPTH):
        n = "l%d_" % l
        s = {"x0": x}
        wcat = W["w_in_p"][l]
        h = _rms_fwd(x, W["mix_norm_g"][l:l + 1], n + "mix_norm")
        pm = _mm(h, wcat[:, :PROJ_MAIN], "nn", BF16, n + "proj", bn=768)
        fg = _mm(h, wcat[:, PROJ_MAIN:], "nn", F32, n + "proj_fg")
        fb = _pad_row8(W["forget_bias"][l])
        c_col = _fox_gate_fwd(fg, fb, n + "fox_gate")
        c_row = c_col[:, 0:8].T
        cw = jnp.pad(W["conv_w"][l], ((0, 5), (0, 0)))
        y_conv = _conv_fwd(pm, cw, n + "conv")
        y_fox, lse = _fox2_fwd(pm, c_row, n + "fox")
        sink = _pad_row8(W["sink"][l])
        y_swa, mlse = _swa_fwd(pm, bias, sink, n + "swa")
        ys = (y_conv, y_fox, y_swa)
        us = tuple(_mm(ys[b], W["w_branch"][l, b], "nn", BF16, n + "branch%d" % b) for b in range(3))
        merged = _merge_fwd(pm, us, n + "merge")
        x1 = _mm(merged, W["w_mix_out"][l], "nn", F32, n + "mix_out", res=x)
        xn1 = _rms_fwd(x1, W["xattn_norm_g"][l:l + 1], n + "xattn_norm")
        memn = _rms_fwd(mem, W["mem_norm_g"][l:l + 1], n + "mem_norm")
        qx = _mm(xn1, W["w_xq"][l], "nn", BF16, n + "xq")
        kv = _mm(memn, W["w_xkv"][l], "nn", BF16, n + "xkv")
        ox = _xattn_fwd(qx, kv, n + "xattn")
        x2 = _mm(ox, W["w_xo"][l], "nn", F32, n + "xo", res=x1)
        xn2 = _rms_fwd(x2, W["ffn_norm_g"][l:l + 1], n + "ffn_norm")
        ab = _mm(xn2, W["w_gu"][l], "nn", BF16, n + "ffn_in", bn=512)
        hm = _swiglu_fwd(ab, n + "swiglu")
        x3 = _mm(hm, W["w_ffn_down"][l], "nn", F32, n + "ffn_out", res=x2, bk=1408)
        s.update(h=h, pm=pm, fg=fg, fb=fb, c_col=c_col, c_row=c_row, cw=cw, ys=ys, lse=lse, sink=sink,
                 mlse=mlse, us=us, merged=merged, x1=x1, xn1=xn1, memn=memn, qx=qx, kv=kv, ox=ox,
                 x2=x2, xn2=xn2, ab=ab, hm=hm)
        saved.append(s)
        x = x3

    loss_row, dx, dg_final = _final_loss(x, W["final_norm_g"].reshape(1, D_MODEL), tgt, "final_loss")
    G = {name: [None] * DEPTH for name in
         ("mix_norm_g", "w_in_p", "forget_bias", "conv_w", "sink", "w_branch", "w_mix_out", "xattn_norm_g",
          "mem_norm_g", "w_xq", "w_xkv", "w_xo", "ffn_norm_g", "w_gu", "w_ffn_down")}
    dbias_tot = None
    for l in reversed(range(DEPTH)):
        n = "l%d_" % l
        s = saved[l]
        dhm = _mm(dx, W["w_ffn_down"][l], "nt", BF16, n + "d_hm", bn=1408)
        G["w_ffn_down"][l] = _mm(s["hm"], dx, "tn", F32, n + "dw_down", bm=1408, bk=1024)
        dab = _swiglu_bwd(s["ab"], dhm, n + "d_swiglu")
        dxn2 = _mm(dab, W["w_gu"][l], "nt", BF16, n + "d_xn2", bk=1408)
        G["w_gu"][l] = _mm(s["xn2"], dab, "tn", F32, n + "dw_gu", bn=512, bk=2048)
        dx, G["ffn_norm_g"][l] = _rms_bwd(s["x2"], W["ffn_norm_g"][l:l + 1], dxn2, dx, n + "d_ffn_norm")
        dox = _mm(dx, W["w_xo"][l], "nt", BF16, n + "d_ox")
        G["w_xo"][l] = _mm(s["ox"], dx, "tn", F32, n + "dw_xo", bk=1024)
        dqx, dkv = _xattn_bwd(s["qx"], s["kv"], dox, n + "d_xattn")
        dxn1 = _mm(dqx, W["w_xq"][l], "nt", BF16, n + "d_xn1")
        G["w_xq"][l] = _mm(s["xn1"], dqx, "tn", F32, n + "dw_xq", bk=2048)
        dmemn = _mm(dkv, W["w_xkv"][l], "nt", BF16, n + "d_memn")
        G["w_xkv"][l] = _mm(s["memn"], dkv, "tn", F32, n + "dw_xkv")
        _, G["mem_norm_g"][l] = _rms_bwd(mem, W["mem_norm_g"][l:l + 1], dmemn, None, n + "d_mem_norm")
        dx, G["xattn_norm_g"][l] = _rms_bwd(s["x1"], W["xattn_norm_g"][l:l + 1], dxn1, dx, n + "d_xattn_norm")
        dmerged = _mm(dx, W["w_mix_out"][l], "nt", BF16, n + "d_merged")
        G["w_mix_out"][l] = _mm(s["merged"], dx, "tn", F32, n + "dw_mix_out", bk=1024)
        du0, du1, du2, dgates = _merge_bwd(s["pm"], s["us"], dmerged, n + "d_merge")
        dus = (du0, du1, du2)
        dys = [_mm(dus[b], W["w_branch"][l, b], "nt", BF16, n + "d_y%d" % b) for b in range(3)]
        G["w_branch"][l] = jnp.stack(
            [_mm(s["ys"][b], dus[b], "tn", F32, n + "dw_branch%d" % b, bk=2048) for b in range(3)])
        dcb, dcc, dcu, dcw = _conv_bwd(s["pm"], s["cw"], dys[0], n + "d_conv")
        G["conv_w"][l] = dcw[0:3]
        delta = _fox_delta(s["ys"][1], dys[1], n + "fox_delta")
        dfq, delta = _fox2_bwd_dq(s["pm"], dys[1], s["c_row"], s["lse"], delta, n + "d_fox_q")
        dfk, dfv, dc = _fox2_bwd_dkv(s["pm"], dys[1], s["c_col"], s["lse"][:, 0:8].T, delta[:, 0:8].T,
                                     n + "d_fox_kv")
        dfg, dfb = _fox_gate_bwd(dc, s["fg"], s["fb"], n + "d_fox_gate")
        G["forget_bias"][l] = dfb[0, 0:8]
        dsq, dkc, dkp, dvc, dvp, dbias, dsink = _swa_bwd(s["pm"], bias, s["sink"], dys[2], s["mlse"],
                                                        n + "d_swa")
        G["sink"][l] = dsink[0, 0:8]
        dbias_tot = dbias if dbias_tot is None else dbias_tot + dbias
        zpad = jnp.zeros((WINDOW, 128), F32)
        dsk = dkc + jnp.concatenate([dkp[WINDOW:], zpad], axis=0)
        dsv = dvc + jnp.concatenate([dvp[WINDOW:], zpad], axis=0)
        dproj = jnp.concatenate([dgates, dcb, dcc, dcu, dfq, dfk, dfv, dsq, dsk.astype(BF16),
                                 dsv.astype(BF16), dfg.astype(BF16)], axis=1)
        dh = _mm(dproj, W["w_in_p"][l], "nt", BF16, n + "d_h", bk=1408)
        G["w_in_p"][l] = _mm(s["h"], dproj, "tn", F32, n + "dw_in", bn=640, bk=2048)
        dx, G["mix_norm_g"][l] = _rms_bwd(s["x0"], W["mix_norm_g"][l:l + 1], dh, dx, n + "d_mix_norm")
    drb = _swa_dbias_reduce(dbias_tot, bucket, "swa_dbias")
    G["rel_bias"] = drb[:, 0:8]
    G["final_norm_g"] = dg_final.reshape(D_MODEL)
    return loss_row, dx, G


def kernel(x, mem, mix_norm_g, w_in, forget_bias, conv_w, sink, w_branch, w_mix_out, rel_bias, xattn_norm_g, mem_norm_g, w_xq, w_xkv, w_xo, ffn_norm_g, w_ffn_gate, w_ffn_up, w_ffn_down, final_norm_g, loss_target, m_mix_norm_g, m_w_in, m_forget_bias, m_conv_w, m_sink, m_w_branch, m_w_mix_out, m_rel_bias, m_xattn_norm_g, m_mem_norm_g, m_w_xq, m_w_xkv, m_w_xo, m_ffn_norm_g, m_w_ffn_gate, m_w_ffn_up, m_w_ffn_down, m_final_norm_g, v_mix_norm_g, v_w_in, v_forget_bias, v_conv_w, v_sink, v_w_branch, v_w_mix_out, v_rel_bias, v_xattn_norm_g, v_mem_norm_g, v_w_xq, v_w_xkv, v_w_xo, v_ffn_norm_g, v_w_ffn_gate, v_w_ffn_up, v_w_ffn_down, v_final_norm_g):
    order = ("mix_norm_g", "w_in", "forget_bias", "conv_w", "sink", "w_branch", "w_mix_out", "rel_bias",
             "xattn_norm_g", "mem_norm_g", "w_xq", "w_xkv", "w_xo", "ffn_norm_g", "w_ffn_gate", "w_ffn_up",
             "w_ffn_down", "final_norm_g")
    w_sh = dict(zip(order, (mix_norm_g, w_in, forget_bias, conv_w, sink, w_branch, w_mix_out, rel_bias,
                            xattn_norm_g, mem_norm_g, w_xq, w_xkv, w_xo, ffn_norm_g, w_ffn_gate, w_ffn_up,
                            w_ffn_down, final_norm_g)))
    m_sh = dict(zip(order, (m_mix_norm_g, m_w_in, m_forget_bias, m_conv_w, m_sink, m_w_branch, m_w_mix_out,
                            m_rel_bias, m_xattn_norm_g, m_mem_norm_g, m_w_xq, m_w_xkv, m_w_xo, m_ffn_norm_g,
                            m_w_ffn_gate, m_w_ffn_up, m_w_ffn_down, m_final_norm_g)))
    v_sh = dict(zip(order, (v_mix_norm_g, v_w_in, v_forget_bias, v_conv_w, v_sink, v_w_branch, v_w_mix_out,
                            v_rel_bias, v_xattn_norm_g, v_mem_norm_g, v_w_xq, v_w_xkv, v_w_xo, v_ffn_norm_g,
                            v_w_ffn_gate, v_w_ffn_up, v_w_ffn_down, v_final_norm_g)))

    xi, yi, ci = lax.axis_index("x"), lax.axis_index("y"), lax.axis_index("c")
    as_idx = lambda *v: jnp.stack([jnp.asarray(t, I32) for t in v])
    gathered = _ag_ring(_place_own(_pack([w_sh[name] for name, _, _ in SHARDED], BF16), as_idx(2 * xi + yi)))
    per_chip = [_unpack(gathered[j]) for j in range(4)]
    full = {name: jnp.concatenate([per_chip[j][name] for j in range(4)], axis=ax) for name, _, ax in SHARDED}
    W = {k: w_sh[k] for k in ("mix_norm_g", "forget_bias", "sink", "xattn_norm_g", "mem_norm_g",
                              "ffn_norm_g", "final_norm_g")}
    W["w_in_p"] = [_perm_w_in(full["w_in"][l]) for l in range(DEPTH)]
    W["w_gu"] = jnp.concatenate([full["w_ffn_gate"], full["w_ffn_up"]], axis=2)
    W["conv_w"] = full["conv_w"].astype(F32)
    for k in ("w_branch", "w_mix_out", "w_xq", "w_xkv", "w_xo", "w_ffn_down"):
        W[k] = full[k]
    loss_row, dx, G = _local_step(x[0], mem[0], loss_target[0], W, rel_bias)

    gfull = {
        "w_in": [_unperm_w_in(G["w_in_p"][l]) for l in range(DEPTH)],
        "w_branch": G["w_branch"],
        "w_mix_out": G["w_mix_out"],
        "w_xq": G["w_xq"],
        "w_xkv": G["w_xkv"],
        "w_xo": G["w_xo"],
        "w_ffn_gate": [G["w_gu"][l][:, :D_FF] for l in range(DEPTH)],
        "w_ffn_up": [G["w_gu"][l][:, D_FF:] for l in range(DEPTH)],
        "w_ffn_down": G["w_ffn_down"],
        "conv_w": G["conv_w"],
    }
    slabs = []
    for j in range(4):
        parts = []
        for name, shape, ax in SHARDED:
            n = shape[ax]
            parts += [lax.slice_in_dim(gfull[name][l], j * n, (j + 1) * n, axis=ax - 1) for l in range(DEPTH)]
        slabs.append(_pack(parts, BF16))
    g4 = jnp.stack(slabs)
    pair = _rs_add_pair(g4, _rs_sibling(g4), as_idx(ci))
    merged = _rs_merge(pair, _rs_diag(pair), as_idx(2 * (1 - xi) + yi, 2 * xi + (1 - yi)))
    gsh = _unpack(_rs_share(_rs_final(pair, _rs_direct(merged), as_idx(2 * xi + yi, ci))))

    small = _unpack_small(_allreduce_small(_pack_small({
        "mix_norm_g": jnp.concatenate(G["mix_norm_g"], axis=0),
        "xattn_norm_g": jnp.concatenate(G["xattn_norm_g"], axis=0),
        "mem_norm_g": jnp.concatenate(G["mem_norm_g"], axis=0),
        "ffn_norm_g": jnp.concatenate(G["ffn_norm_g"], axis=0),
        "final_norm_g": G["final_norm_g"],
        "forget_bias": jnp.stack(G["forget_bias"]),
        "sink": jnp.stack(G["sink"]),
        "rel_bias": G["rel_bias"],
    })))
    grads = dict(gsh)
    grads.update(small)

    sm_names = [name for name, _ in SMALL]
    sd, sm_, sv_ = _adamw(_pack_small({k: w_sh[k] for k in sm_names}), _pack_small({k: grads[k] for k in sm_names}),
                          _pack_small({k: m_sh[k] for k in sm_names}), _pack_small({k: v_sh[k] for k in sm_names}),
                          "adamw_small")
    delta, new_m, new_v = _unpack_small(sd), _unpack_small(sm_), _unpack_small(sv_)
    for name, shape, _ in SHARDED:
        two_d = (-1, shape[-1])
        d, nm, nv = _adamw(w_sh[name].reshape(two_d), grads[name].reshape(two_d), m_sh[name].reshape(two_d),
                           v_sh[name].reshape(two_d), "adamw_" + name)
        delta[name], new_m[name], new_v[name] = d.reshape(shape), nm.reshape(shape), nv.reshape(shape)

    loss = lax.psum(loss_row[0, 0], ("x", "y", "c"))
    return (loss, dx[None], *[grads[k] for k in order], *[delta[k] for k in order],
            *[new_m[k] for k in order], *[new_v[k] for k in order])
```

```python
import math

import numpy as np
import jax
import jax.numpy as jnp
from jax import lax
from jax.experimental import pallas as pl
from jax.experimental.pallas import tpu as pltpu

F32 = jnp.float32
BF16 = jnp.bfloat16
I32 = jnp.int32

D_MODEL = 1024
DEPTH = 2
HEAD_DIM = 64
BRANCH = 512
N_BUCKETS = 32
WINDOW = 128
MEM_LEN = 256
X_HEADS = 4
X_HEAD_DIM = 256
D_FF = 2816
IN_COLS = 6920
PROJ_MAIN = 6912
PROJ_PAD = 7040
RMS_EPS = 1e-6
NEG = -1e30
ATT_SCALE = 0.125
X_SCALE = 0.0625

ADAM_LR = 0.001
ADAM_B1 = 0.9
ADAM_B2 = 0.999
ADAM_EPS = 1e-08
ADAM_WD = 0.01
ADAM_STEP = 10

VMEM_LIMIT = 48 * 1024 * 1024
MESH = pl.DeviceIdType.MESH

CB_GATE = (0, 1, 2)
CB_B, CB_C, CB_U, CB_FQ, CB_FK, CB_FV, CB_SQ = 6, 7, 8, 9, 10, 11, 12
CB_SK, CB_SV = 52, 53


def _cp(sem):
    return pltpu.CompilerParams(dimension_semantics=sem, vmem_limit_bytes=VMEM_LIMIT)


def _pick(n, prefs):
    for p in prefs:
        if p <= n and n % p == 0:
            return p
    return n


def _dot(a, b, dims):
    return lax.dot_general(a, b, (dims, ((), ())), preferred_element_type=F32)


def _dot_nn(a, b):
    return _dot(a, b, ((1,), (0,)))


def _dot_nt(a, b):
    return _dot(a, b, ((1,), (1,)))


def _dot_tn(a, b):
    return _dot(a, b, ((0,), (0,)))


def _mm(a, b, mode, out_dtype, name, res=None, bm=1024, bn=1024, bk=1024):
    if mode == "nn":
        (M, K), (K2, N) = a.shape, b.shape
    elif mode == "nt":
        (M, K), (N, K2) = a.shape, b.shape
    else:
        (K, M), (K2, N) = a.shape, b.shape
    assert K == K2, (name, a.shape, b.shape)
    bm = _pick(M, (bm, 1024, 512, 256, 128))
    bn = _pick(N, (bn, 1024, 768, 640, 512, 384, 256, 128))
    bk = _pick(K, (bk, 1024, 768, 640, 512, 384, 256, 128))
    nk = K // bk
    if mode == "tn":
        a_spec = pl.BlockSpec((bk, bm), lambda i, j, k: (k, i))
    else:
        a_spec = pl.BlockSpec((bm, bk), lambda i, j, k: (i, k))
    if mode == "nt":
        b_spec = pl.BlockSpec((bn, bk), lambda i, j, k: (j, k))
    else:
        b_spec = pl.BlockSpec((bk, bn), lambda i, j, k: (k, j))
    dims = {"nn": ((1,), (0,)), "nt": ((1,), (1,)), "tn": ((0,), (0,))}[mode]
    o_spec = pl.BlockSpec((bm, bn), lambda i, j, k: (i, j))
    has_res = res is not None

    def body(*refs):
        if has_res:
            a_ref, b_ref, r_ref, o_ref = refs[:4]
            scr = refs[4:]
        else:
            a_ref, b_ref, o_ref = refs[:3]
            r_ref = None
            scr = refs[3:]
        p = _dot(a_ref[...].astype(BF16), b_ref[...].astype(BF16), dims)
        if nk == 1:
            if has_res:
                p = p + r_ref[...]
            o_ref[...] = p.astype(out_dtype)
        else:
            acc = scr[0]
            k = pl.program_id(2)

            @pl.when(k == 0)
            def _():
                acc[...] = p

            @pl.when(k > 0)
            def _():
                acc[...] += p

            @pl.when(k == nk - 1)
            def _():
                r = acc[...]
                if has_res:
                    r = r + r_ref[...]
                o_ref[...] = r.astype(out_dtype)

    ins = [a, b] + ([res] if has_res else [])
    in_specs = [a_spec, b_spec] + ([o_spec] if has_res else [])
    return pl.pallas_call(
        body, name=name, grid=(M // bm, N // bn, nk),
        in_specs=in_specs, out_specs=o_spec,
        out_shape=jax.ShapeDtypeStruct((M, N), out_dtype),
        scratch_shapes=[pltpu.VMEM((bm, bn), F32)] if nk > 1 else [],
        compiler_params=_cp(("parallel", "parallel", "arbitrary")),
    )(*ins)


def _rms_fwd(x, g, name):
    T, Dm = x.shape
    bt = _pick(T, (512, 256))

    def body(x_ref, g_ref, o_ref):
        xv = x_ref[...]
        r = lax.rsqrt(jnp.mean(xv * xv, axis=-1, keepdims=True) + RMS_EPS)
        o_ref[...] = ((xv * r) * g_ref[...]).astype(BF16)

    return pl.pallas_call(
        body, name=name, grid=(T // bt,),
        in_specs=[pl.BlockSpec((bt, Dm), lambda i: (i, 0)), pl.BlockSpec((1, Dm), lambda i: (0, 0))],
        out_specs=pl.BlockSpec((bt, Dm), lambda i: (i, 0)),
        out_shape=jax.ShapeDtypeStruct((T, Dm), BF16),
        compiler_params=_cp(("parallel",)),
    )(x, g)


def _rms_bwd(x, g, dh, dres, name):
    T, Dm = x.shape
    bt = _pick(T, (512, 256))
    want_dx = dres is not None

    def body(*refs):
        if want_dx:
            x_ref, g_ref, dh_ref, dr_ref, dx_ref, dg_ref = refs
        else:
            x_ref, g_ref, dh_ref, dg_ref = refs
        xv = x_ref[...]
        r = lax.rsqrt(jnp.mean(xv * xv, axis=-1, keepdims=True) + RMS_EPS)
        xh = xv * r
        dhv = dh_ref[...].astype(F32)

        @pl.when(pl.program_id(0) == 0)
        def _():
            dg_ref[...] = jnp.zeros_like(dg_ref)

        dg_ref[...] += jnp.sum(dhv * xh, axis=0, keepdims=True)
        if want_dx:
            dyg = dhv * g_ref[...]
            dx_ref[...] = dr_ref[...] + r * (dyg - xh * jnp.mean(dyg * xh, axis=-1, keepdims=True))

    row = pl.BlockSpec((bt, Dm), lambda i: (i, 0))
    vec = pl.BlockSpec((1, Dm), lambda i: (0, 0))
    if want_dx:
        return pl.pallas_call(
            body, name=name, grid=(T // bt,),
            in_specs=[row, vec, row, row], out_specs=[row, vec],
            out_shape=[jax.ShapeDtypeStruct((T, Dm), F32), jax.ShapeDtypeStruct((1, Dm), F32)],
            compiler_params=_cp(("arbitrary",)),
        )(x, g, dh, dres)
    return None, pl.pallas_call(
        body, name=name, grid=(T // bt,),
        in_specs=[row, vec, row], out_specs=vec,
        out_shape=jax.ShapeDtypeStruct((1, Dm), F32),
        compiler_params=_cp(("arbitrary",)),
    )(x, g, dh)


def _final_loss(x, g, tgt, name):
    T, Dm = x.shape
    bt = _pick(T, (512, 256))

    def body(x_ref, g_ref, t_ref, loss_ref, dx_ref, dg_ref):
        xv = x_ref[...]
        r = lax.rsqrt(jnp.mean(xv * xv, axis=-1, keepdims=True) + RMS_EPS)
        xh = xv * r
        gv = g_ref[...]
        err = xh * gv - t_ref[...]

        @pl.when(pl.program_id(0) == 0)
        def _():
            dg_ref[...] = jnp.zeros_like(dg_ref)
            loss_ref[...] = jnp.zeros_like(loss_ref)

        loss_ref[...] += jnp.sum(err * err) * (0.5 / Dm)
        dy = err * (1.0 / Dm)
        dg_ref[...] += jnp.sum(dy * xh, axis=0, keepdims=True)
        dyg = dy * gv
        dx_ref[...] = r * (dyg - xh * jnp.mean(dyg * xh, axis=-1, keepdims=True))

    row = pl.BlockSpec((bt, Dm), lambda i: (i, 0))
    vec = pl.BlockSpec((1, Dm), lambda i: (0, 0))
    return pl.pallas_call(
        body, name=name, grid=(T // bt,),
        in_specs=[row, vec, row],
        out_specs=[pl.BlockSpec((1, 128), lambda i: (0, 0)), row, vec],
        out_shape=[jax.ShapeDtypeStruct((1, 128), F32), jax.ShapeDtypeStruct((T, Dm), F32),
                   jax.ShapeDtypeStruct((1, Dm), F32)],
        compiler_params=_cp(("arbitrary",)),
    )(x, g, tgt)


HALO = 16


def _shift_down(z, zprev, s):
    rolled = pltpu.roll(z, s, 0)
    hp = pltpu.roll(zprev, s, 0)
    row = lax.broadcasted_iota(I32, hp.shape, 0)
    top = jnp.where(row < s, hp, rolled[:HALO])
    return jnp.concatenate([top, rolled[HALO:]], axis=0)


def _shift_up(z, znext, s):
    n = z.shape[0]
    rolled = pltpu.roll(z, n - s, 0)
    hn = pltpu.roll(znext, HALO - s, 0)
    row = lax.broadcasted_iota(I32, hn.shape, 0)
    bot = jnp.where(row >= HALO - s, hn, rolled[n - HALO:])
    return jnp.concatenate([rolled[:n - HALO], bot], axis=0)


def _conv_fwd(pm, cw, name):
    T = pm.shape[0]
    bt = _pick(T, (512, 256))
    hb = bt // HALO

    def body(b_ref, c_ref, u_ref, cp_ref, up_ref, w_ref, o_ref):
        i = pl.program_id(0)
        z = c_ref[...].astype(F32) * u_ref[...].astype(F32)
        zp = cp_ref[...].astype(F32) * up_ref[...].astype(F32)
        zp = jnp.where(i > 0, zp, 0.0)
        w = w_ref[...]
        y = w[2:3] * z + w[1:2] * _shift_down(z, zp, 1) + w[0:1] * _shift_down(z, zp, 2)
        o_ref[...] = (b_ref[...].astype(F32) * y).astype(BF16)

    def col(cb):
        return pl.BlockSpec((bt, BRANCH), lambda i: (i, cb))

    def prev(cb):
        return pl.BlockSpec((HALO, BRANCH), lambda i: (jnp.maximum(i * hb - 1, 0), cb))

    return pl.pallas_call(
        body, name=name, grid=(T // bt,),
        in_specs=[col(CB_B), col(CB_C), col(CB_U), prev(CB_C), prev(CB_U),
                  pl.BlockSpec((8, BRANCH), lambda i: (0, 0))],
        out_specs=pl.BlockSpec((bt, BRANCH), lambda i: (i, 0)),
        out_shape=jax.ShapeDtypeStruct((T, BRANCH), BF16),
        compiler_params=_cp(("parallel",)),
    )(pm, pm, pm, pm, pm, cw)


def _conv_bwd(pm, cw, dy, name):
    T = pm.shape[0]
    bt = _pick(T, (512, 256))
    hb = bt // HALO
    nb = T // bt
    last_h = T // HALO - 1

    def body(b_ref, c_ref, u_ref, cp_ref, up_ref, bn_ref, dy_ref, dyn_ref, w_ref,
             db_ref, dc_ref, du_ref, dw_ref):
        i = pl.program_id(0)
        cv = c_ref[...].astype(F32)
        uv = u_ref[...].astype(F32)
        bv = b_ref[...].astype(F32)
        z = cv * uv
        zp = jnp.where(i > 0, cp_ref[...].astype(F32) * up_ref[...].astype(F32), 0.0)
        w = w_ref[...]
        z1 = _shift_down(z, zp, 1)
        z2 = _shift_down(z, zp, 2)
        yc = w[2:3] * z + w[1:2] * z1 + w[0:1] * z2
        dyv = dy_ref[...].astype(F32)
        db_ref[...] = (dyv * yc).astype(BF16)
        g = dyv * bv
        gn = jnp.where(i < nb - 1, dyn_ref[...].astype(F32) * bn_ref[...].astype(F32), 0.0)
        dz = w[2:3] * g + w[1:2] * _shift_up(g, gn, 1) + w[0:1] * _shift_up(g, gn, 2)
        dc_ref[...] = (dz * uv).astype(BF16)
        du_ref[...] = (dz * cv).astype(BF16)

        @pl.when(i == 0)
        def _():
            dw_ref[...] = jnp.zeros_like(dw_ref)

        dw_ref[0:1, :] += jnp.sum(g * z2, axis=0, keepdims=True)
        dw_ref[1:2, :] += jnp.sum(g * z1, axis=0, keepdims=True)
        dw_ref[2:3, :] += jnp.sum(g * z, axis=0, keepdims=True)

    def col(cb):
        return pl.BlockSpec((bt, BRANCH), lambda i: (i, cb))

    def prev(cb):
        return pl.BlockSpec((HALO, BRANCH), lambda i: (jnp.maximum(i * hb - 1, 0), cb))

    def nxt(cb):
        return pl.BlockSpec((HALO, BRANCH), lambda i: (jnp.minimum((i + 1) * hb, last_h), cb))

    own = pl.BlockSpec((bt, BRANCH), lambda i: (i, 0))
    w_spec = pl.BlockSpec((8, BRANCH), lambda i: (0, 0))
    act = jax.ShapeDtypeStruct((T, BRANCH), BF16)
    return pl.pallas_call(
        body, name=name, grid=(nb,),
        in_specs=[col(CB_B), col(CB_C), col(CB_U), prev(CB_C), prev(CB_U), nxt(CB_B), own,
                  pl.BlockSpec((HALO, BRANCH), lambda i: (jnp.minimum((i + 1) * hb, last_h), 0)), w_spec],
        out_specs=[own, own, own, w_spec],
        out_shape=[act, act, act, jax.ShapeDtypeStruct((8, BRANCH), F32)],
        compiler_params=_cp(("arbitrary",)),
    )(pm, pm, pm, pm, pm, pm, dy, dy, cw)


def _log_sigmoid(z):
    return jnp.minimum(z, 0.0) - jnp.log(1.0 + jnp.exp(-jnp.abs(z)))


def _fox_gate_fwd(fg, fb, name):
    T = fg.shape[0]
    bt = _pick(T, (256,))

    def body(f_ref, b_ref, c_ref, carry):
        @pl.when(pl.program_id(0) == 0)
        def _():
            carry[...] = jnp.zeros_like(carry)

        xv = _log_sigmoid(f_ref[...] + b_ref[...])
        row = lax.broadcasted_iota(I32, xv.shape, 0)
        s = 1
        while s < bt:
            xv = xv + jnp.where(row >= s, pltpu.roll(xv, s, 0), 0.0)
            s *= 2
        xv = xv + carry[...]
        c_ref[...] = xv
        carry[...] = xv[bt - 1:bt, :]

    blk = pl.BlockSpec((bt, 128), lambda i: (i, 0))
    return pl.pallas_call(
        body, name=name, grid=(T // bt,),
        in_specs=[blk, pl.BlockSpec((1, 128), lambda i: (0, 0))],
        out_specs=blk, out_shape=jax.ShapeDtypeStruct((T, 128), F32),
        scratch_shapes=[pltpu.VMEM((1, 128), F32)],
        compiler_params=_cp(("arbitrary",)),
    )(fg, fb)


def _fox_gate_bwd(dc, fg, fb, name):
    T = fg.shape[0]
    bt = _pick(T, (256,))
    nb = T // bt

    def body(d_ref, f_ref, b_ref, o_ref, db_ref, carry):
        @pl.when(pl.program_id(0) == 0)
        def _():
            carry[...] = jnp.zeros_like(carry)
            db_ref[...] = jnp.zeros_like(db_ref)

        xv = d_ref[...]
        row = lax.broadcasted_iota(I32, xv.shape, 0)
        s = 1
        while s < bt:
            xv = xv + jnp.where(row < bt - s, pltpu.roll(xv, bt - s, 0), 0.0)
            s *= 2
        xv = xv + carry[...]
        carry[...] = xv[0:1, :]
        z = f_ref[...] + b_ref[...]
        dz = xv * (1.0 / (1.0 + jnp.exp(z)))
        o_ref[...] = dz
        db_ref[...] += jnp.sum(dz, axis=0, keepdims=True)

    blk = pl.BlockSpec((bt, 128), lambda i: (nb - 1 - i, 0))
    vec = pl.BlockSpec((1, 128), lambda i: (0, 0))
    return pl.pallas_call(
        body, name=name, grid=(nb,),
        in_specs=[blk, blk, vec], out_specs=[blk, vec],
        out_shape=[jax.ShapeDtypeStruct((T, 128), F32), jax.ShapeDtypeStruct((1, 128), F32)],
        scratch_shapes=[pltpu.VMEM((1, 128), F32)],
        compiler_params=_cp(("arbitrary",)),
    )(dc, fg, fb)


def _lane_lo(shape):
    return lax.broadcasted_iota(I32, shape, 1) < HEAD_DIM


def _put_col(shape, h, col):
    lane = lax.broadcasted_iota(I32, shape, 1)
    return jnp.where(lane == h, col, 0.0)


def _fox_fwd(pm, c_col, c_row, name):
    T = pm.shape[0]
    bq = _pick(T, (512, 256))
    bk = bq
    nq = T // bq

    def body(q_ref, k_ref, v_ref, cq_ref, ck_ref, o_ref, lse_ref, acc, m_s, l_s):
        qi = pl.program_id(0)
        ki = pl.program_id(1)

        @pl.when(ki == 0)
        def _():
            acc[...] = jnp.zeros_like(acc)
            m_s[...] = jnp.full_like(m_s, NEG)
            l_s[...] = jnp.zeros_like(l_s)

        @pl.when(ki <= qi)
        def _():
            row = lax.broadcasted_iota(I32, (bq, bk), 0) + qi * bq
            colv = lax.broadcasted_iota(I32, (bq, bk), 1) + ki * bk
            causal = colv <= row
            klo = _lane_lo((bk, 128))
            qlo = _lane_lo((bq, 128))
            cq = cq_ref[...]
            ck = ck_ref[...]
            for p in range(4):
                sl = slice(128 * p, 128 * p + 128)
                qp = q_ref[:, sl] * ATT_SCALE
                kp = k_ref[:, sl]
                vp = v_ref[:, sl]
                kz = jnp.zeros_like(kp)
                ks = (jnp.where(klo, kp, kz), jnp.where(klo, kz, kp))
                alphas, pvs = [], []
                for j in range(2):
                    h = 2 * p + j
                    s = _dot_nt(qp, ks[j]) + (cq[:, h:h + 1] - ck[h:h + 1, :])
                    s = jnp.where(causal, s, NEG)
                    m_old = m_s[h][:, 0:1]
                    m_new = jnp.maximum(m_old, jnp.max(s, axis=-1, keepdims=True))
                    alpha = jnp.exp(m_old - m_new)
                    pe = jnp.exp(s - m_new)
                    l_new = alpha * l_s[h][:, 0:1] + jnp.sum(pe, axis=-1, keepdims=True)
                    m_s[h] = jnp.broadcast_to(m_new, (bq, 128))
                    l_s[h] = jnp.broadcast_to(l_new, (bq, 128))
                    alphas.append(alpha)
                    pvs.append(_dot_nn(pe.astype(BF16), vp))
                a = jnp.where(qlo, alphas[0], alphas[1])
                acc[:, sl] = a * acc[:, sl] + jnp.where(qlo, pvs[0], pvs[1])

        @pl.when(ki == nq - 1)
        def _():
            qlo = _lane_lo((bq, 128))
            lse = jnp.zeros((bq, 128), F32)
            for p in range(4):
                sl = slice(128 * p, 128 * p + 128)
                l0 = l_s[2 * p][:, 0:1]
                l1 = l_s[2 * p + 1][:, 0:1]
                o_ref[:, sl] = (acc[:, sl] / jnp.where(qlo, l0, l1)).astype(BF16)
                lse = lse + _put_col((bq, 128), 2 * p, m_s[2 * p][:, 0:1] + jnp.log(l0))
                lse = lse + _put_col((bq, 128), 2 * p + 1, m_s[2 * p + 1][:, 0:1] + jnp.log(l1))
            lse_ref[...] = lse

    return pl.pallas_call(
        body, name=name, grid=(nq, nq),
        in_specs=[pl.BlockSpec((bq, BRANCH), lambda i, k: (i, CB_FQ)),
                  pl.BlockSpec((bk, BRANCH), lambda i, k: (jnp.minimum(k, i), CB_FK)),
                  pl.BlockSpec((bk, BRANCH), lambda i, k: (jnp.minimum(k, i), CB_FV)),
                  pl.BlockSpec((bq, 128), lambda i, k: (i, 0)),
                  pl.BlockSpec((8, bk), lambda i, k: (0, jnp.minimum(k, i)))],
        out_specs=[pl.BlockSpec((bq, BRANCH), lambda i, k: (i, 0)),
                   pl.BlockSpec((bq, 128), lambda i, k: (i, 0))],
        out_shape=[jax.ShapeDtypeStruct((T, BRANCH), BF16), jax.ShapeDtypeStruct((T, 128), F32)],
        scratch_shapes=[pltpu.VMEM((bq, BRANCH), F32), pltpu.VMEM((8, bq, 128), F32),
                        pltpu.VMEM((8, bq, 128), F32)],
        compiler_params=_cp(("parallel", "arbitrary")),
    )(pm, pm, pm, c_col, c_row)


def _fox_delta(o, do, name):
    T = o.shape[0]
    bt = _pick(T, (512, 256))

    def body(o_ref, d_ref, out_ref):
        prod = o_ref[...].astype(F32) * d_ref[...].astype(F32)
        out = jnp.zeros((bt, 128), F32)
        for h in range(8):
            out = out + _put_col((bt, 128), h, jnp.sum(prod[:, 64 * h:64 * h + 64], axis=-1, keepdims=True))
        out_ref[...] = out

    blk = pl.BlockSpec((bt, BRANCH), lambda i: (i, 0))
    return pl.pallas_call(
        body, name=name, grid=(T // bt,), in_specs=[blk, blk],
        out_specs=pl.BlockSpec((bt, 128), lambda i: (i, 0)),
        out_shape=jax.ShapeDtypeStruct((T, 128), F32),
        compiler_params=_cp(("parallel",)),
    )(o, do)


def _fox_bwd_dq(pm, do, c_col, c_row, lse, delta, name):
    T = pm.shape[0]
    bq = _pick(T, (512, 256))
    bk = bq
    nq = T // bq

    def body(q_ref, k_ref, v_ref, do_ref, cq_ref, ck_ref, lse_ref, dl_ref, dq_ref, dl2_ref, acc, esum):
        qi = pl.program_id(0)
        ki = pl.program_id(1)

        @pl.when(ki == 0)
        def _():
            acc[...] = jnp.zeros_like(acc)
            esum[...] = jnp.zeros_like(esum)

        @pl.when(ki <= qi)
        def _():
            row = lax.broadcasted_iota(I32, (bq, bk), 0) + qi * bq
            colv = lax.broadcasted_iota(I32, (bq, bk), 1) + ki * bk
            causal = colv <= row
            klo = _lane_lo((bk, 128))
            qlo = _lane_lo((bq, 128))
            cq = cq_ref[...]
            ck = ck_ref[...]
            lse_v = lse_ref[...]
            dl_v = dl_ref[...]
            es = jnp.zeros((bq, 128), F32)
            for p in range(4):
                sl = slice(128 * p, 128 * p + 128)
                qp = q_ref[:, sl] * ATT_SCALE
                kp = k_ref[:, sl]
                vp = v_ref[:, sl]
                dop = do_ref[:, sl]
                kz = jnp.zeros_like(kp)
                ks = (jnp.where(klo, kp, kz), jnp.where(klo, kz, kp))
                vs = (jnp.where(klo, vp, kz), jnp.where(klo, kz, vp))
                dqs = []
                for j in range(2):
                    h = 2 * p + j
                    s = _dot_nt(qp, ks[j]) + (cq[:, h:h + 1] - ck[h:h + 1, :])
                    s = jnp.where(causal, s, NEG)
                    pr = jnp.exp(s - lse_v[:, h:h + 1])
                    dp = _dot_nt(dop, vs[j])
                    ds = pr * (dp - dl_v[:, h:h + 1])
                    es = es + _put_col((bq, 128), h, jnp.sum(ds, axis=-1, keepdims=True))
                    dqs.append(_dot_nn(ds.astype(BF16), kp))
                acc[:, sl] += jnp.where(qlo, dqs[0], dqs[1])
            esum[...] += es

        @pl.when(ki == nq - 1)
        def _():
            dq_ref[...] = (acc[...] * ATT_SCALE).astype(BF16)
            dl2_ref[...] = dl_ref[...] + esum[...]

    qb = pl.BlockSpec((bq, 128), lambda i, k: (i, 0))
    return pl.pallas_call(
        body, name=name, grid=(nq, nq),
        in_specs=[pl.BlockSpec((bq, BRANCH), lambda i, k: (i, CB_FQ)),
                  pl.BlockSpec((bk, BRANCH), lambda i, k: (jnp.minimum(k, i), CB_FK)),
                  pl.BlockSpec((bk, BRANCH), lambda i, k: (jnp.minimum(k, i), CB_FV)),
                  pl.BlockSpec((bq, BRANCH), lambda i, k: (i, 0)),
                  qb, pl.BlockSpec((8, bk), lambda i, k: (0, jnp.minimum(k, i))), qb, qb],
        out_specs=[pl.BlockSpec((bq, BRANCH), lambda i, k: (i, 0)), qb],
        out_shape=[jax.ShapeDtypeStruct((T, BRANCH), BF16), jax.ShapeDtypeStruct((T, 128), F32)],
        scratch_shapes=[pltpu.VMEM((bq, BRANCH), F32), pltpu.VMEM((bq, 128), F32)],
        compiler_params=_cp(("parallel", "arbitrary")),
    )(pm, pm, pm, do, c_col, c_row, lse, delta)


def _fox_bwd_dkv(pm, do, c_col, c_row, lse_row, delta_row, name):
    T = pm.shape[0]
    bk = _pick(T, (512, 256))
    bq = bk
    nk = T // bk

    def body(q_ref, k_ref, v_ref, do_ref, cq_ref, ck_ref, lse_ref, dl_ref,
             dk_ref, dv_ref, dc_ref, dk_acc, dv_acc, dc_acc):
        ki = pl.program_id(0)
        qi = pl.program_id(1)

        @pl.when(qi == 0)
        def _():
            dk_acc[...] = jnp.zeros_like(dk_acc)
            dv_acc[...] = jnp.zeros_like(dv_acc)
            dc_acc[...] = jnp.zeros_like(dc_acc)

        @pl.when(qi >= ki)
        def _():
            krow = lax.broadcasted_iota(I32, (bk, bq), 0) + ki * bk
            qcol = lax.broadcasted_iota(I32, (bk, bq), 1) + qi * bq
            causal = krow <= qcol
            qlo = _lane_lo((bq, 128))
            klo = _lane_lo((bk, 128))
            cq = cq_ref[...]
            ck = ck_ref[...]
            lse_v = lse_ref[...]
            dl_v = dl_ref[...]
            dcs = jnp.zeros((bk, 128), F32)
            for p in range(4):
                sl = slice(128 * p, 128 * p + 128)
                qp = q_ref[:, sl]
                kp = k_ref[:, sl] * ATT_SCALE
                vp = v_ref[:, sl]
                dop = do_ref[:, sl]
                qz = jnp.zeros_like(qp)
                qs = (jnp.where(qlo, qp, qz), jnp.where(qlo, qz, qp))
                dos = (jnp.where(qlo, dop, qz), jnp.where(qlo, qz, dop))
                dks, dvs = [], []
                for j in range(2):
                    h = 2 * p + j
                    st = _dot_nt(kp, qs[j]) + (cq[h:h + 1, :] - ck[:, h:h + 1])
                    st = jnp.where(causal, st, NEG)
                    pt = jnp.exp(st - lse_v[h:h + 1, :])
                    dvs.append(_dot_nn(pt.astype(BF16), dop))
                    dpt = _dot_nt(vp, dos[j])
                    dst = pt * (dpt - dl_v[h:h + 1, :])
                    dks.append(_dot_nn(dst.astype(BF16), qp))
                    dcs = dcs - _put_col((bk, 128), h, jnp.sum(dst, axis=-1, keepdims=True))
                dk_acc[:, sl] += jnp.where(klo, dks[0], dks[1])
                dv_acc[:, sl] += jnp.where(klo, dvs[0], dvs[1])
            dc_acc[...] += dcs

        @pl.when(qi == nk - 1)
        def _():
            dk_ref[...] = (dk_acc[...] * ATT_SCALE).astype(BF16)
            dv_ref[...] = dv_acc[...].astype(BF16)
            dc_ref[...] = dc_acc[...]

    qrow = pl.BlockSpec((8, bq), lambda k, i: (0, jnp.maximum(i, k)))
    kb = pl.BlockSpec((bk, BRANCH), lambda k, i: (k, 0))
    return pl.pallas_call(
        body, name=name, grid=(nk, nk),
        in_specs=[pl.BlockSpec((bq, BRANCH), lambda k, i: (jnp.maximum(i, k), CB_FQ)),
                  pl.BlockSpec((bk, BRANCH), lambda k, i: (k, CB_FK)),
                  pl.BlockSpec((bk, BRANCH), lambda k, i: (k, CB_FV)),
                  pl.BlockSpec((bq, BRANCH), lambda k, i: (jnp.maximum(i, k), 0)),
                  qrow, pl.BlockSpec((bk, 128), lambda k, i: (k, 0)), qrow, qrow],
        out_specs=[kb, kb, pl.BlockSpec((bk, 128), lambda k, i: (k, 0))],
        out_shape=[jax.ShapeDtypeStruct((T, BRANCH), BF16), jax.ShapeDtypeStruct((T, BRANCH), BF16),
                   jax.ShapeDtypeStruct((T, 128), F32)],
        scratch_shapes=[pltpu.VMEM((bk, BRANCH), F32), pltpu.VMEM((bk, BRANCH), F32),
                        pltpu.VMEM((bk, 128), F32)],
        compiler_params=_cp(("parallel", "arbitrary")),
    )(pm, pm, pm, do, c_row, c_col, lse_row, delta_row)


FOX_ROWS = 32


FOX_UNROLL = 16


def _row_start(r, rows):
    return r * rows if isinstance(r, int) else pl.multiple_of(r * rows, rows)


def _chunk_loop(n, chunk):
    if n <= FOX_UNROLL:
        for u in range(n):
            chunk(u, 0)
        return

    def outer(i, carry):
        for u in range(FOX_UNROLL):
            chunk(i * FOX_UNROLL + u, carry)
        return carry

    lax.fori_loop(0, n // FOX_UNROLL, outer, 0)


def _tree(op, xs):
    xs = list(xs)
    while len(xs) > 1:
        xs = [op(xs[i], xs[i + 1]) if i + 1 < len(xs) else xs[i] for i in range(0, len(xs), 2)]
    return xs[0]


def _masked_halves(t):
    lo = _lane_lo(t.shape)
    z = jnp.zeros_like(t)
    return jnp.where(lo, t, z), jnp.where(lo, z, t)


def _fox2_fwd(pm, c_row, name):
    T = pm.shape[0]
    bq = _pick(T, (512, 256))
    bk = bq
    nq = T // bq
    R = FOX_ROWS
    ng = bk // 128

    def body(q_ref, k_ref, v_ref, ck_ref, o_ref, lse_ref, acc, m_s, l_s, a_s, s_scr, p_scr):
        qi = pl.program_id(0)
        ki = pl.program_id(1)

        @pl.when(ki == 0)
        def _():
            acc[...] = jnp.zeros_like(acc)
            m_s[...] = jnp.full_like(m_s, NEG)
            l_s[...] = jnp.zeros_like(l_s)

        def block(masked):
            qlo = _lane_lo((bq, 128))
            for p in range(4):
                sl = slice(128 * p, 128 * p + 128)
                qp = q_ref[:, sl] * ATT_SCALE
                vp = v_ref[:, sl]
                ks = _masked_halves(k_ref[:, sl])
                pvs = []
                for j in range(2):
                    h = 2 * p + j
                    s_scr[j] = _dot_nt(qp, ks[j])

                    def chunk(r, carry, h=h, j=j):
                        r0 = _row_start(r, R)
                        rows = pl.ds(r0, R)
                        sc = [s_scr[j, rows, 128 * g:128 * g + 128] - ck_ref[h:h + 1, 128 * g:128 * g + 128]
                              for g in range(ng)]
                        if masked:
                            rid = lax.broadcasted_iota(I32, (R, 128), 0) + r0
                            cid = lax.broadcasted_iota(I32, (R, 128), 1)
                            sc = [jnp.where(cid + 128 * g <= rid, sc[g], NEG) for g in range(ng)]
                        m_old = m_s[h, rows, :]
                        m_new = jnp.maximum(m_old, jnp.max(_tree(jnp.maximum, sc), axis=-1, keepdims=True))
                        alpha = jnp.exp(m_old - m_new)
                        pe = [jnp.exp(sc[g] - m_new) for g in range(ng)]
                        l_s[h, rows, :] = alpha * l_s[h, rows, :] + _tree(jnp.add, pe)
                        m_s[h, rows, :] = m_new
                        a_s[j, rows, :] = alpha
                        for g in range(ng):
                            p_scr[j, rows, 128 * g:128 * g + 128] = pe[g].astype(BF16)
                        return carry

                    _chunk_loop(bq // R, chunk)
                    pvs.append(_dot_nn(p_scr[j], vp))
                acc[:, sl] = jnp.where(qlo, a_s[0], a_s[1]) * acc[:, sl] + jnp.where(qlo, pvs[0], pvs[1])

        @pl.when(ki < qi)
        def _():
            block(False)

        @pl.when(ki == qi)
        def _():
            block(True)

        @pl.when(ki == nq - 1)
        def _():
            qlo = _lane_lo((bq, 128))
            lse = jnp.zeros((bq, 128), F32)
            for p in range(4):
                sl = slice(128 * p, 128 * p + 128)
                l0 = jnp.sum(l_s[2 * p], axis=-1, keepdims=True)
                l1 = jnp.sum(l_s[2 * p + 1], axis=-1, keepdims=True)
                o_ref[:, sl] = (acc[:, sl] / jnp.where(qlo, l0, l1)).astype(BF16)
                lse = lse + _put_col((bq, 128), 2 * p, m_s[2 * p][:, 0:1] + jnp.log(l0))
                lse = lse + _put_col((bq, 128), 2 * p + 1, m_s[2 * p + 1][:, 0:1] + jnp.log(l1))
            lse_ref[...] = lse

    return pl.pallas_call(
        body, name=name, grid=(nq, nq),
        in_specs=[pl.BlockSpec((bq, BRANCH), lambda i, k: (i, CB_FQ)),
                  pl.BlockSpec((bk, BRANCH), lambda i, k: (jnp.minimum(k, i), CB_FK)),
                  pl.BlockSpec((bk, BRANCH), lambda i, k: (jnp.minimum(k, i), CB_FV)),
                  pl.BlockSpec((8, bk), lambda i, k: (0, jnp.minimum(k, i)))],
        out_specs=[pl.BlockSpec((bq, BRANCH), lambda i, k: (i, 0)),
                   pl.BlockSpec((bq, 128), lambda i, k: (i, 0))],
        out_shape=[jax.ShapeDtypeStruct((T, BRANCH), BF16), jax.ShapeDtypeStruct((T, 128), F32)],
        scratch_shapes=[pltpu.VMEM((bq, BRANCH), F32), pltpu.VMEM((8, bq, 128), F32),
                        pltpu.VMEM((8, bq, 128), F32), pltpu.VMEM((2, bq, 128), F32),
                        pltpu.VMEM((2, bq, bk), F32), pltpu.VMEM((2, bq, bk), BF16)],
        compiler_params=_cp(("parallel", "arbitrary")),
    )(pm, pm, pm, c_row)


def _fox2_bwd_dq(pm, do, c_row, lse, delta, name):
    T = pm.shape[0]
    bq = _pick(T, (512, 256))
    bk = bq
    nq = T // bq
    R = FOX_ROWS
    ng = bk // 128

    def body(q_ref, k_ref, v_ref, do_ref, ck_ref, lse_ref, dl_ref, dq_ref, dl2_ref,
             acc, e_s, s_scr, dp_scr, ds_scr):
        qi = pl.program_id(0)
        ki = pl.program_id(1)

        @pl.when(ki == 0)
        def _():
            acc[...] = jnp.zeros_like(acc)
            e_s[...] = jnp.zeros_like(e_s)

        def block(masked):
            qlo = _lane_lo((bq, 128))
            for p in range(4):
                sl = slice(128 * p, 128 * p + 128)
                qp = q_ref[:, sl] * ATT_SCALE
                kp = k_ref[:, sl]
                dop = do_ref[:, sl]
                ks = _masked_halves(kp)
                vs = _masked_halves(v_ref[:, sl])
                dqs = []
                for j in range(2):
                    h = 2 * p + j
                    s_scr[...] = _dot_nt(qp, ks[j])
                    dp_scr[...] = _dot_nt(dop, vs[j])

                    def chunk(r, carry, h=h):
                        r0 = _row_start(r, R)
                        rows = pl.ds(r0, R)
                        lse_c = lse_ref[rows, h:h + 1]
                        dl_c = dl_ref[rows, h:h + 1]
                        if masked:
                            rid = lax.broadcasted_iota(I32, (R, 128), 0) + r0
                            cid = lax.broadcasted_iota(I32, (R, 128), 1)
                        dss = []
                        for g in range(ng):
                            gs = slice(128 * g, 128 * g + 128)
                            sc = s_scr[rows, gs] - ck_ref[h:h + 1, gs]
                            if masked:
                                sc = jnp.where(cid + 128 * g <= rid, sc, NEG)
                            ds = jnp.exp(sc - lse_c) * (dp_scr[rows, gs] - dl_c)
                            ds_scr[rows, gs] = ds.astype(BF16)
                            dss.append(ds)
                        e_s[h, rows, :] += _tree(jnp.add, dss)
                        return carry

                    _chunk_loop(bq // R, chunk)
                    dqs.append(_dot_nn(ds_scr[...], kp))
                acc[:, sl] += jnp.where(qlo, dqs[0], dqs[1])

        @pl.when(ki < qi)
        def _():
            block(False)

        @pl.when(ki == qi)
        def _():
            block(True)

        @pl.when(ki == nq - 1)
        def _():
            dq_ref[...] = (acc[...] * ATT_SCALE).astype(BF16)
            out = dl_ref[...]
            for h in range(8):
                out = out + _put_col((bq, 128), h, jnp.sum(e_s[h], axis=-1, keepdims=True))
            dl2_ref[...] = out

    qb = pl.BlockSpec((bq, 128), lambda i, k: (i, 0))
    return pl.pallas_call(
        body, name=name, grid=(nq, nq),
        in_specs=[pl.BlockSpec((bq, BRANCH), lambda i, k: (i, CB_FQ)),
                  pl.BlockSpec((bk, BRANCH), lambda i, k: (jnp.minimum(k, i), CB_FK)),
                  pl.BlockSpec((bk, BRANCH), lambda i, k: (jnp.minimum(k, i), CB_FV)),
                  pl.BlockSpec((bq, BRANCH), lambda i, k: (i, 0)),
                  pl.BlockSpec((8, bk), lambda i, k: (0, jnp.minimum(k, i))), qb, qb],
        out_specs=[pl.BlockSpec((bq, BRANCH), lambda i, k: (i, 0)), qb],
        out_shape=[jax.ShapeDtypeStruct((T, BRANCH), BF16), jax.ShapeDtypeStruct((T, 128), F32)],
        scratch_shapes=[pltpu.VMEM((bq, BRANCH), F32), pltpu.VMEM((8, bq, 128), F32),
                        pltpu.VMEM((bq, bk), F32), pltpu.VMEM((bq, bk), F32), pltpu.VMEM((bq, bk), BF16)],
        compiler_params=_cp(("parallel", "arbitrary")),
    )(pm, pm, pm, do, c_row, lse, delta)


def _fox2_bwd_dkv(pm, do, c_col, lse_row, delta_row, name):
    T = pm.shape[0]
    bk = _pick(T, (512, 256))
    bq = bk
    nk = T // bk
    R = FOX_ROWS
    ng = bq // 128

    def body(q_ref, k_ref, v_ref, do_ref, ck_ref, lse_ref, dl_ref, dk_ref, dv_ref, dc_ref,
             dk_acc, dv_acc, dc_s, st_scr, dpt_scr, pt_scr, dst_scr):
        ki = pl.program_id(0)
        qi = pl.program_id(1)

        @pl.when(qi == 0)
        def _():
            dk_acc[...] = jnp.zeros_like(dk_acc)
            dv_acc[...] = jnp.zeros_like(dv_acc)
            dc_s[...] = jnp.zeros_like(dc_s)

        def block(masked):
            klo = _lane_lo((bk, 128))
            for p in range(4):
                sl = slice(128 * p, 128 * p + 128)
                qp = q_ref[:, sl]
                kp = k_ref[:, sl] * ATT_SCALE
                vp = v_ref[:, sl]
                dop = do_ref[:, sl]
                qs = _masked_halves(qp)
                dos = _masked_halves(dop)
                dks, dvs = [], []
                for j in range(2):
                    h = 2 * p + j
                    st_scr[...] = _dot_nt(kp, qs[j])
                    dpt_scr[...] = _dot_nt(vp, dos[j])

                    def chunk(r, carry, h=h):
                        r0 = _row_start(r, R)
                        rows = pl.ds(r0, R)
                        ck_c = ck_ref[rows, h:h + 1]
                        if masked:
                            kid = lax.broadcasted_iota(I32, (R, 128), 0) + r0
                            qid = lax.broadcasted_iota(I32, (R, 128), 1)
                        dss = []
                        for g in range(ng):
                            gs = slice(128 * g, 128 * g + 128)
                            st = st_scr[rows, gs] - (ck_c + lse_ref[h:h + 1, gs])
                            if masked:
                                st = jnp.where(kid <= qid + 128 * g, st, NEG)
                            pt = jnp.exp(st)
                            dst = pt * (dpt_scr[rows, gs] - dl_ref[h:h + 1, gs])
                            pt_scr[rows, gs] = pt.astype(BF16)
                            dst_scr[rows, gs] = dst.astype(BF16)
                            dss.append(dst)
                        dc_s[h, rows, :] -= _tree(jnp.add, dss)
                        return carry

                    _chunk_loop(bk // R, chunk)
                    dvs.append(_dot_nn(pt_scr[...], dop))
                    dks.append(_dot_nn(dst_scr[...], qp))
                dk_acc[:, sl] += jnp.where(klo, dks[0], dks[1])
                dv_acc[:, sl] += jnp.where(klo, dvs[0], dvs[1])

        @pl.when(qi > ki)
        def _():
            block(False)

        @pl.when(qi == ki)
        def _():
            block(True)

        @pl.when(qi == nk - 1)
        def _():
            dk_ref[...] = (dk_acc[...] * ATT_SCALE).astype(BF16)
            dv_ref[...] = dv_acc[...].astype(BF16)
            out = jnp.zeros((bk, 128), F32)
            for h in range(8):
                out = out + _put_col((bk, 128), h, jnp.sum(dc_s[h], axis=-1, keepdims=True))
            dc_ref[...] = out

    qrow = pl.BlockSpec((8, bq), lambda k, i: (0, jnp.maximum(i, k)))
    kb = pl.BlockSpec((bk, BRANCH), lambda k, i: (k, 0))
    return pl.pallas_call(
        body, name=name, grid=(nk, nk),
        in_specs=[pl.BlockSpec((bq, BRANCH), lambda k, i: (jnp.maximum(i, k), CB_FQ)),
                  pl.BlockSpec((bk, BRANCH), lambda k, i: (k, CB_FK)),
                  pl.BlockSpec((bk, BRANCH), lambda k, i: (k, CB_FV)),
                  pl.BlockSpec((bq, BRANCH), lambda k, i: (jnp.maximum(i, k), 0)),
                  pl.BlockSpec((bk, 128), lambda k, i: (k, 0)), qrow, qrow],
        out_specs=[kb, kb, pl.BlockSpec((bk, 128), lambda k, i: (k, 0))],
        out_shape=[jax.ShapeDtypeStruct((T, BRANCH), BF16), jax.ShapeDtypeStruct((T, BRANCH), BF16),
                   jax.ShapeDtypeStruct((T, 128), F32)],
        scratch_shapes=[pltpu.VMEM((bk, BRANCH), F32), pltpu.VMEM((bk, BRANCH), F32),
                        pltpu.VMEM((8, bk, 128), F32), pltpu.VMEM((bk, bq), F32), pltpu.VMEM((bk, bq), F32),
                        pltpu.VMEM((bk, bq), BF16), pltpu.VMEM((bk, bq), BF16)],
        compiler_params=_cp(("parallel", "arbitrary")),
    )(pm, pm, pm, do, c_col, lse_row, delta_row)


def _bucket_table():
    tq = np.arange(WINDOW, dtype=np.int32)[:, None]
    sk = np.arange(2 * WINDOW, dtype=np.int32)[None, :]
    n = np.maximum(WINDOW + tq - sk, 0)
    max_exact = N_BUCKETS // 2
    ratio = np.maximum(n, 1).astype(np.float32) / np.float32(max_exact)
    large = max_exact + (np.log(ratio) / np.float32(math.log(WINDOW / max_exact))
                         * np.float32(N_BUCKETS - max_exact)).astype(np.int32)
    large = np.minimum(large, N_BUCKETS - 1)
    return np.where(n < max_exact, n, large).astype(np.int32)


def _swa_bias(rel_bias, bucket, name):
    def body(rb_ref, bk_ref, o_ref):
        bkt = bk_ref[...]
        for h in range(8):
            def step(b, a):
                return a + jnp.where(bkt == b, rb_ref[b, h], 0.0)
            o_ref[h] = lax.fori_loop(0, N_BUCKETS, step, jnp.zeros(bkt.shape, F32))

    return pl.pallas_call(
        body, name=name,
        in_specs=[pl.BlockSpec(memory_space=pltpu.SMEM), pl.BlockSpec(memory_space=pltpu.VMEM)],
        out_specs=pl.BlockSpec(memory_space=pltpu.VMEM),
        out_shape=jax.ShapeDtypeStruct((8, WINDOW, 2 * WINDOW), F32),
    )(rel_bias, bucket)


def _swa_dbias_reduce(dbias, bucket, name):
    def body(d_ref, bk_ref, o_ref):
        bkt = bk_ref[...]
        rowi = lax.broadcasted_iota(I32, (N_BUCKETS, 128), 0)
        lane = lax.broadcasted_iota(I32, (N_BUCKETS, 128), 1)
        out = jnp.zeros((N_BUCKETS, 128), F32)
        for h in range(8):
            dv = d_ref[h]

            def step(b, a):
                tot = jnp.sum(jnp.where(bkt == b, dv, 0.0), keepdims=True)
                return a + jnp.where((rowi == b) & (lane == h), tot, 0.0)
            out = lax.fori_loop(0, N_BUCKETS, step, out)
        o_ref[...] = out

    return pl.pallas_call(
        body, name=name,
        in_specs=[pl.BlockSpec(memory_space=pltpu.VMEM), pl.BlockSpec(memory_space=pltpu.VMEM)],
        out_specs=pl.BlockSpec(memory_space=pltpu.VMEM),
        out_shape=jax.ShapeDtypeStruct((N_BUCKETS, 128), F32),
    )(dbias, bucket)


def _swap_halves(x):
    return pltpu.roll(x.astype(F32), HEAD_DIM, 1).astype(x.dtype)


def _kv_variants(t):
    lo = _lane_lo(t.shape)
    z = jnp.zeros_like(t)
    a0 = jnp.where(lo, t, z)
    b1 = jnp.where(lo, z, t)
    b0 = _swap_halves(a0)
    a1 = _swap_halves(b1)
    return (a0, a1), (b0, b1), (a0 + b0, a1 + b1)


def _swa_masks(i):
    tq = lax.broadcasted_iota(I32, (WINDOW, WINDOW), 0)
    jj = lax.broadcasted_iota(I32, (WINDOW, WINDOW), 1)
    return (jj > tq) & (i > 0), jj <= tq


def _swa_specs():
    q = pl.BlockSpec((WINDOW, BRANCH), lambda i: (i, CB_SQ))
    kc = pl.BlockSpec((WINDOW, 128), lambda i: (i, CB_SK))
    kp = pl.BlockSpec((WINDOW, 128), lambda i: (jnp.maximum(i - 1, 0), CB_SK))
    vc = pl.BlockSpec((WINDOW, 128), lambda i: (i, CB_SV))
    vp = pl.BlockSpec((WINDOW, 128), lambda i: (jnp.maximum(i - 1, 0), CB_SV))
    bias = pl.BlockSpec((8, WINDOW, 2 * WINDOW), lambda i: (0, 0, 0))
    vec = pl.BlockSpec((1, 128), lambda i: (0, 0))
    return q, kc, kp, vc, vp, bias, vec


def _swa_fwd(pm, bias, sink, name):
    T = pm.shape[0]
    nb = T // WINDOW

    def body(q_ref, kc_ref, kp_ref, vc_ref, vp_ref, b_ref, s_ref, o_ref, m_ref):
        i = pl.program_id(0)
        mprev, mcur = _swa_masks(i)
        kcA, kcB, _ = _kv_variants(kc_ref[...])
        kpA, kpB, _ = _kv_variants(kp_ref[...])
        _, _, vcD = _kv_variants(vc_ref[...])
        _, _, vpD = _kv_variants(vp_ref[...])
        lo = _lane_lo((WINDOW, 128))
        sink_v = s_ref[...]
        mout = jnp.zeros((WINDOW, 128), F32)
        for p in range(4):
            jv = p // 2
            sl = slice(128 * p, 128 * p + 128)
            qp = q_ref[:, sl] * ATT_SCALE
            outs = []
            for par in range(2):
                h = 2 * p + par
                kpx = (kpA, kpB)[par][jv]
                kcx = (kcA, kcB)[par][jv]
                sp = jnp.where(mprev, _dot_nt(qp, kpx) + b_ref[h, :, 0:WINDOW], NEG)
                sc = jnp.where(mcur, _dot_nt(qp, kcx) + b_ref[h, :, WINDOW:2 * WINDOW], NEG)
                sk_h = sink_v[:, h:h + 1]
                m = jnp.maximum(jnp.maximum(jnp.max(sp, axis=-1, keepdims=True),
                                            jnp.max(sc, axis=-1, keepdims=True)), sk_h)
                ep = jnp.exp(sp - m)
                ec = jnp.exp(sc - m)
                den = (jnp.sum(ep, axis=-1, keepdims=True) + jnp.sum(ec, axis=-1, keepdims=True)
                       + jnp.exp(sk_h - m))
                inv = 1.0 / den
                outs.append(_dot_nn((ep * inv).astype(BF16), vpD[jv])
                            + _dot_nn((ec * inv).astype(BF16), vcD[jv]))
                mout = mout + _put_col((WINDOW, 128), h, m + jnp.log(den))
            o_ref[:, sl] = jnp.where(lo, outs[0], outs[1]).astype(BF16)
        m_ref[...] = mout

    q, kc, kp, vc, vp, bs, vec = _swa_specs()
    return pl.pallas_call(
        body, name=name, grid=(nb,),
        in_specs=[q, kc, kp, vc, vp, bs, vec],
        out_specs=[pl.BlockSpec((WINDOW, BRANCH), lambda i: (i, 0)),
                   pl.BlockSpec((WINDOW, 128), lambda i: (i, 0))],
        out_shape=[jax.ShapeDtypeStruct((T, BRANCH), BF16), jax.ShapeDtypeStruct((T, 128), F32)],
        compiler_params=_cp(("parallel",)),
    )(pm, pm, pm, pm, pm, bias, sink)


def _swa_bwd(pm, bias, sink, do, mlse, name):
    T = pm.shape[0]
    nb = T // WINDOW

    def fold(zz):
        return zz + pltpu.roll(zz, HEAD_DIM, 1)

    def body(q_ref, kc_ref, kp_ref, vc_ref, vp_ref, b_ref, s_ref, do_ref, m_ref,
             dq_ref, dkc_ref, dkp_ref, dvc_ref, dvp_ref, db_ref, ds_ref):
        i = pl.program_id(0)

        @pl.when(i == 0)
        def _():
            db_ref[...] = jnp.zeros_like(db_ref)
            ds_ref[...] = jnp.zeros_like(ds_ref)

        mprev, mcur = _swa_masks(i)
        kcA, kcB, kcD = _kv_variants(kc_ref[...])
        kpA, kpB, kpD = _kv_variants(kp_ref[...])
        vcA, vcB, _ = _kv_variants(vc_ref[...])
        vpA, vpB, _ = _kv_variants(vp_ref[...])
        lo = _lane_lo((WINDOW, 128))
        sink_v = s_ref[...]
        mv = m_ref[...]
        zk = jnp.zeros((WINDOW, 128), F32)
        zkp, zkc, zvp, zvc = [zk, zk], [zk, zk], [zk, zk], [zk, zk]
        dsink = jnp.zeros((1, 128), F32)
        for p in range(4):
            jv = p // 2
            sl = slice(128 * p, 128 * p + 128)
            qraw = q_ref[:, sl]
            qp = qraw * ATT_SCALE
            dop = do_ref[:, sl]
            dqs, mkp, mkc, mvp, mvc = [], [], [], [], []
            for par in range(2):
                h = 2 * p + par
                kpx = (kpA, kpB)[par][jv]
                kcx = (kcA, kcB)[par][jv]
                vpx = (vpA, vpB)[par][jv]
                vcx = (vcA, vcB)[par][jv]
                sp = jnp.where(mprev, _dot_nt(qp, kpx) + b_ref[h, :, 0:WINDOW], NEG)
                sc = jnp.where(mcur, _dot_nt(qp, kcx) + b_ref[h, :, WINDOW:2 * WINDOW], NEG)
                m_h = mv[:, h:h + 1]
                pp = jnp.exp(sp - m_h)
                pc = jnp.exp(sc - m_h)
                psink = jnp.exp(sink_v[:, h:h + 1] - m_h)
                dpp = _dot_nt(dop, vpx)
                dpc = _dot_nt(dop, vcx)
                delta = jnp.sum(pp * dpp, axis=-1, keepdims=True) + jnp.sum(pc * dpc, axis=-1, keepdims=True)
                dsp = pp * (dpp - delta)
                dsc = pc * (dpc - delta)
                db_ref[h, :, 0:WINDOW] += dsp
                db_ref[h, :, WINDOW:2 * WINDOW] += dsc
                dsink = dsink - _put_col((1, 128), h, jnp.sum(psink * delta, keepdims=True))
                dsp_b = dsp.astype(BF16)
                dsc_b = dsc.astype(BF16)
                dqs.append(_dot_nn(dsp_b, kpD[jv]) + _dot_nn(dsc_b, kcD[jv]))
                mkp.append(_dot_tn(dsp_b, qraw))
                mkc.append(_dot_tn(dsc_b, qraw))
                mvp.append(_dot_tn(pp.astype(BF16), dop))
                mvc.append(_dot_tn(pc.astype(BF16), dop))
            dq_ref[:, sl] = (jnp.where(lo, dqs[0], dqs[1]) * ATT_SCALE).astype(BF16)
            zkp[jv] = zkp[jv] + jnp.where(lo, mkp[0], mkp[1])
            zkc[jv] = zkc[jv] + jnp.where(lo, mkc[0], mkc[1])
            zvp[jv] = zvp[jv] + jnp.where(lo, mvp[0], mvp[1])
            zvc[jv] = zvc[jv] + jnp.where(lo, mvc[0], mvc[1])
        dkc_ref[...] = jnp.where(lo, fold(zkc[0]), fold(zkc[1])) * ATT_SCALE
        dkp_ref[...] = jnp.where(lo, fold(zkp[0]), fold(zkp[1])) * ATT_SCALE
        dvc_ref[...] = jnp.where(lo, fold(zvc[0]), fold(zvc[1]))
        dvp_ref[...] = jnp.where(lo, fold(zvp[0]), fold(zvp[1]))
        ds_ref[...] += dsink

    q, kc, kp, vc, vp, bs, vec = _swa_specs()
    own = pl.BlockSpec((WINDOW, BRANCH), lambda i: (i, 0))
    sm = pl.BlockSpec((WINDOW, 128), lambda i: (i, 0))
    f128 = jax.ShapeDtypeStruct((T, 128), F32)
    return pl.pallas_call(
        body, name=name, grid=(nb,),
        in_specs=[q, kc, kp, vc, vp, bs, vec, own, sm],
        out_specs=[own, sm, sm, sm, sm, bs, vec],
        out_shape=[jax.ShapeDtypeStruct((T, BRANCH), BF16), f128, f128, f128, f128,
                   jax.ShapeDtypeStruct((8, WINDOW, 2 * WINDOW), F32), jax.ShapeDtypeStruct((1, 128), F32)],
        compiler_params=_cp(("arbitrary",)),
    )(pm, pm, pm, pm, pm, bias, sink, do, mlse)


def _merge_fwd(pm, us, name):
    T = pm.shape[0]
    bt = _pick(T, (512, 256))

    def body(g0, g1, g2, u0, u1, u2, o_ref):
        acc = jax.nn.sigmoid(g0[...].astype(F32)) * u0[...].astype(F32)
        acc = acc + jax.nn.sigmoid(g1[...].astype(F32)) * u1[...].astype(F32)
        acc = acc + jax.nn.sigmoid(g2[...].astype(F32)) * u2[...].astype(F32)
        o_ref[...] = acc.astype(BF16)

    own = pl.BlockSpec((bt, D_MODEL), lambda i: (i, 0))
    gs = [pl.BlockSpec((bt, D_MODEL), lambda i, cb=cb: (i, cb)) for cb in CB_GATE]
    return pl.pallas_call(
        body, name=name, grid=(T // bt,), in_specs=gs + [own, own, own], out_specs=own,
        out_shape=jax.ShapeDtypeStruct((T, D_MODEL), BF16),
        compiler_params=_cp(("parallel",)),
    )(pm, pm, pm, *us)


def _merge_bwd(pm, us, dm, name):
    T = pm.shape[0]
    bt = _pick(T, (256,))

    def body(g0, g1, g2, u0, u1, u2, dm_ref, du0, du1, du2, dg_ref):
        dmv = dm_ref[...].astype(F32)
        for b, (g, u, du) in enumerate(((g0, u0, du0), (g1, u1, du1), (g2, u2, du2))):
            s = jax.nn.sigmoid(g[...].astype(F32))
            du[...] = (dmv * s).astype(BF16)
            dg_ref[:, D_MODEL * b:D_MODEL * (b + 1)] = (dmv * u[...].astype(F32) * s * (1.0 - s)).astype(BF16)

    own = pl.BlockSpec((bt, D_MODEL), lambda i: (i, 0))
    gs = [pl.BlockSpec((bt, D_MODEL), lambda i, cb=cb: (i, cb)) for cb in CB_GATE]
    act = jax.ShapeDtypeStruct((T, D_MODEL), BF16)
    return pl.pallas_call(
        body, name=name, grid=(T // bt,), in_specs=gs + [own, own, own, own],
        out_specs=[own, own, own, pl.BlockSpec((bt, 3 * D_MODEL), lambda i: (i, 0))],
        out_shape=[act, act, act, jax.ShapeDtypeStruct((T, 3 * D_MODEL), BF16)],
        compiler_params=_cp(("parallel",)),
    )(pm, pm, pm, *us, dm)


def _swiglu_fwd(ab, name):
    T = ab.shape[0]
    bt = _pick(T, (512, 256))

    def body(a_ref, b_ref, o_ref):
        a = a_ref[...].astype(F32)
        o_ref[...] = (a * jax.nn.sigmoid(a) * b_ref[...].astype(F32)).astype(BF16)

    return pl.pallas_call(
        body, name=name, grid=(T // bt,),
        in_specs=[pl.BlockSpec((bt, D_FF), lambda i: (i, 0)), pl.BlockSpec((bt, D_FF), lambda i: (i, 1))],
        out_specs=pl.BlockSpec((bt, D_FF), lambda i: (i, 0)),
        out_shape=jax.ShapeDtypeStruct((T, D_FF), BF16),
        compiler_params=_cp(("parallel",)),
    )(ab, ab)


def _swiglu_bwd(ab, dh, name):
    T = ab.shape[0]
    bt = _pick(T, (256,))

    def body(a_ref, b_ref, d_ref, o_ref):
        a = a_ref[...].astype(F32)
        b = b_ref[...].astype(F32)
        d = d_ref[...].astype(F32)
        s = jax.nn.sigmoid(a)
        o_ref[:, 0:D_FF] = (d * b * (s + a * s * (1.0 - s))).astype(BF16)
        o_ref[:, D_FF:2 * D_FF] = (d * a * s).astype(BF16)

    return pl.pallas_call(
        body, name=name, grid=(T // bt,),
        in_specs=[pl.BlockSpec((bt, D_FF), lambda i: (i, 0)), pl.BlockSpec((bt, D_FF), lambda i: (i, 1)),
                  pl.BlockSpec((bt, D_FF), lambda i: (i, 0))],
        out_specs=pl.BlockSpec((bt, 2 * D_FF), lambda i: (i, 0)),
        out_shape=jax.ShapeDtypeStruct((T, 2 * D_FF), BF16),
        compiler_params=_cp(("parallel",)),
    )(ab, ab, dh)


def _xattn_probs(q_ref, kv_ref, h):
    sl = slice(X_HEAD_DIM * h, X_HEAD_DIM * (h + 1))
    qh = q_ref[:, sl]
    kh = kv_ref[:, sl]
    vh = kv_ref[:, D_MODEL + X_HEAD_DIM * h:D_MODEL + X_HEAD_DIM * (h + 1)]
    s = _dot_nt(qh, kh) * X_SCALE
    e = jnp.exp(s - jnp.max(s, axis=-1, keepdims=True))
    return qh, kh, vh, e * (1.0 / jnp.sum(e, axis=-1, keepdims=True))


def _xattn_fwd(q, kv, name):
    T = q.shape[0]
    bq = _pick(T, (512, 256))

    def body(q_ref, kv_ref, o_ref):
        for h in range(X_HEADS):
            _, _, vh, p = _xattn_probs(q_ref, kv_ref, h)
            o_ref[:, X_HEAD_DIM * h:X_HEAD_DIM * (h + 1)] = _dot_nn(p.astype(BF16), vh).astype(BF16)

    own = pl.BlockSpec((bq, D_MODEL), lambda i: (i, 0))
    return pl.pallas_call(
        body, name=name, grid=(T // bq,),
        in_specs=[own, pl.BlockSpec((MEM_LEN, 2 * D_MODEL), lambda i: (0, 0))], out_specs=own,
        out_shape=jax.ShapeDtypeStruct((T, D_MODEL), BF16),
        compiler_params=_cp(("parallel",)),
    )(q, kv)


def _xattn_bwd(q, kv, do, name):
    T = q.shape[0]
    bq = _pick(T, (512, 256))

    def body(q_ref, kv_ref, do_ref, dq_ref, dkv_ref):
        @pl.when(pl.program_id(0) == 0)
        def _():
            dkv_ref[...] = jnp.zeros_like(dkv_ref)

        for h in range(X_HEADS):
            sl = slice(X_HEAD_DIM * h, X_HEAD_DIM * (h + 1))
            qh, kh, vh, p = _xattn_probs(q_ref, kv_ref, h)
            doh = do_ref[:, sl]
            dp = _dot_nt(doh, vh)
            ds = (p * (dp - jnp.sum(p * dp, axis=-1, keepdims=True)) * X_SCALE).astype(BF16)
            dq_ref[:, sl] = _dot_nn(ds, kh).astype(BF16)
            dkv_ref[:, sl] += _dot_tn(ds, qh)
            dkv_ref[:, D_MODEL + X_HEAD_DIM * h:D_MODEL + X_HEAD_DIM * (h + 1)] += _dot_tn(p.astype(BF16), doh)

    own = pl.BlockSpec((bq, D_MODEL), lambda i: (i, 0))
    kvs = pl.BlockSpec((MEM_LEN, 2 * D_MODEL), lambda i: (0, 0))
    return pl.pallas_call(
        body, name=name, grid=(T // bq,), in_specs=[own, kvs, own], out_specs=[own, kvs],
        out_shape=[jax.ShapeDtypeStruct((T, D_MODEL), BF16), jax.ShapeDtypeStruct((MEM_LEN, 2 * D_MODEL), F32)],
        compiler_params=_cp(("arbitrary",)),
    )(q, kv, do)


def _adamw(w, g, m, v, name):
    R, C = w.shape
    cpad = -(-C // 128) * 128
    bt = R
    for cand in (1024, 512, 256, 128, 64, 32, 16, 8):
        if R % cand == 0 and cand * cpad * 4 <= (1 << 20):
            bt = cand
            break

    def body(w_ref, g_ref, m_ref, v_ref, d_ref, nm_ref, nv_ref):
        gv = g_ref[...]
        mn = ADAM_B1 * m_ref[...] + (1.0 - ADAM_B1) * gv
        vn = ADAM_B2 * v_ref[...] + (1.0 - ADAM_B2) * (gv * gv)
        m_hat = mn / (1.0 - ADAM_B1 ** ADAM_STEP)
        v_hat = vn / (1.0 - ADAM_B2 ** ADAM_STEP)
        d_ref[...] = -ADAM_LR * (m_hat / (jnp.sqrt(v_hat) + ADAM_EPS) + ADAM_WD * w_ref[...])
        nm_ref[...] = mn
        nv_ref[...] = vn

    blk = pl.BlockSpec((bt, C), lambda i: (i, 0))
    out = jax.ShapeDtypeStruct((R, C), F32)
    return pl.pallas_call(
        body, name=name, grid=(R // bt,), in_specs=[blk] * 4, out_specs=[blk] * 3,
        out_shape=[out, out, out], compiler_params=_cp(("parallel",)),
    )(w, g, m, v)


ANY = pl.BlockSpec(memory_space=pl.ANY)


def _place():
    x, y, c = lax.axis_index("x"), lax.axis_index("y"), lax.axis_index("c")
    chips = [(1 - x, y), (x, 1 - y), (1 - x, 1 - y)]
    return x, y, c, chips


def _ag_packs(pack):
    R, Wd = pack.shape
    hrows = R // 2

    def body(p_ref, o_ref, send_sems, recv_sems, local_sem):
        x, y, c, chips = _place()
        me = 2 * x + y
        mine = pl.ds(c * hrows, hrows)
        theirs = pl.ds((1 - c) * hrows, hrows)
        local = pltpu.make_async_copy(p_ref, o_ref.at[me], local_sem)
        local.start()

        def copy(k, slab, rows, to, src=None):
            dst = o_ref.at[slab, rows]
            return pltpu.make_async_remote_copy(
                src_ref=dst if src is None else src, dst_ref=dst,
                send_sem=send_sems.at[k], recv_sem=recv_sems.at[k], device_id=to, device_id_type=MESH)

        first = [copy(k, me, mine, (px, py, c), src=p_ref.at[mine]) for k, (px, py) in enumerate(chips)]
        for cp in first:
            cp.start()
        passed = [copy(3 + k, 2 * px + py, mine, (x, y, 1 - c)) for k, (px, py) in enumerate(chips)]
        for k, (px, py) in enumerate(chips):
            copy(k, 2 * px + py, mine, (x, y, c)).wait_recv()
            passed[k].start()
        for k, (px, py) in enumerate(chips):
            copy(3 + k, 2 * px + py, theirs, (x, y, c)).wait_recv()
        for cp in first + passed:
            cp.wait_send()
        local.wait()

    return pl.pallas_call(
        body, name="ag_weights", in_specs=[ANY], out_specs=ANY,
        out_shape=jax.ShapeDtypeStruct((4, R, Wd), pack.dtype),
        scratch_shapes=[pltpu.SemaphoreType.DMA((6,)), pltpu.SemaphoreType.DMA((6,)), pltpu.SemaphoreType.DMA],
    )(pack)


def _rs_sibling(g4):
    _, R, Wd = g4.shape
    hrows = R // 2

    def body(g_ref, o_ref, send_sem, recv_sem):
        x, y, c, _ = _place()
        cp = pltpu.make_async_remote_copy(
            src_ref=g_ref.at[:, pl.ds((1 - c) * hrows, hrows)], dst_ref=o_ref,
            send_sem=send_sem, recv_sem=recv_sem, device_id=(x, y, 1 - c), device_id_type=MESH)
        cp.start()
        cp.wait()

    return pl.pallas_call(
        body, name="rs_sibling", in_specs=[ANY], out_specs=ANY,
        out_shape=jax.ShapeDtypeStruct((4, hrows, Wd), g4.dtype),
        scratch_shapes=[pltpu.SemaphoreType.DMA, pltpu.SemaphoreType.DMA],
    )(g4)


def _rs_add_pair(g4, sib, cidx, tag=""):
    _, R, Wd = g4.shape
    hrows = R // 2
    bt = _pick(hrows, ROW_BLOCKS)
    nb = hrows // bt

    def body(c_ref, a_ref, b_ref, o_ref):
        o_ref[...] = (a_ref[...].astype(F32) + b_ref[...].astype(F32)).astype(o_ref.dtype)

    grid_spec = pltpu.PrefetchScalarGridSpec(
        num_scalar_prefetch=1, grid=(4, nb),
        in_specs=[pl.BlockSpec((1, bt, Wd), lambda j, i, c: (j, c[0] * nb + i, 0)),
                  pl.BlockSpec((1, bt, Wd), lambda j, i, c: (j, i, 0))],
        out_specs=pl.BlockSpec((1, bt, Wd), lambda j, i, c: (j, i, 0)))
    return pl.pallas_call(
        body, name="rs_add_pair" + tag, grid_spec=grid_spec,
        out_shape=jax.ShapeDtypeStruct((4, hrows, Wd), g4.dtype),
        compiler_params=_cp(("parallel", "parallel")),
    )(cidx, g4, sib)


def _rs_chips(r4):
    _, hrows, Wd = r4.shape

    def body(r_ref, o_ref, send_sems, recv_sems, local_sem):
        x, y, c, chips = _place()
        me = 2 * x + y
        local = pltpu.make_async_copy(r_ref.at[me], o_ref.at[me], local_sem)
        local.start()
        sends = []
        for k, (px, py) in enumerate(chips):
            sends.append(pltpu.make_async_remote_copy(
                src_ref=r_ref.at[2 * px + py], dst_ref=o_ref.at[me],
                send_sem=send_sems.at[k], recv_sem=recv_sems.at[k], device_id=(px, py, c), device_id_type=MESH))
        for cp in sends:
            cp.start()
        for k, (px, py) in enumerate(chips):
            pltpu.make_async_remote_copy(
                src_ref=r_ref.at[me], dst_ref=o_ref.at[2 * px + py],
                send_sem=send_sems.at[k], recv_sem=recv_sems.at[k], device_id=(x, y, c),
                device_id_type=MESH).wait_recv()
        for cp in sends:
            cp.wait_send()
        local.wait()

    return pl.pallas_call(
        body, name="rs_chips", in_specs=[ANY], out_specs=ANY,
        out_shape=jax.ShapeDtypeStruct((4, hrows, Wd), r4.dtype),
        scratch_shapes=[pltpu.SemaphoreType.DMA((3,)), pltpu.SemaphoreType.DMA((3,)), pltpu.SemaphoreType.DMA],
    )(r4)


def _rs_add_chips(q4):
    _, hrows, Wd = q4.shape
    bt = _pick(hrows, (240, 120, 16))

    def body(q_ref, o_ref):
        o_ref[...] = ((q_ref[0].astype(F32) + q_ref[1].astype(F32)) + q_ref[2].astype(F32)) + q_ref[3].astype(F32)

    return pl.pallas_call(
        body, name="rs_add_chips", grid=(hrows // bt,),
        in_specs=[pl.BlockSpec((4, bt, Wd), lambda i: (0, i, 0))],
        out_specs=pl.BlockSpec((bt, Wd), lambda i: (i, 0)),
        out_shape=jax.ShapeDtypeStruct((hrows, Wd), F32),
        compiler_params=_cp(("parallel",)),
    )(q4)


def _rs_share(buf):
    R, Wd = buf.shape
    hrows = R // 2

    def body(b_ref, o_ref, send_sem, recv_sem):
        del b_ref
        x, y, c, _ = _place()
        mine = o_ref.at[pl.ds(c * hrows, hrows)]
        cp = pltpu.make_async_remote_copy(
            src_ref=mine, dst_ref=mine, send_sem=send_sem, recv_sem=recv_sem,
            device_id=(x, y, 1 - c), device_id_type=MESH)
        cp.start()
        theirs = o_ref.at[pl.ds((1 - c) * hrows, hrows)]
        pltpu.make_async_remote_copy(
            src_ref=theirs, dst_ref=theirs, send_sem=send_sem, recv_sem=recv_sem,
            device_id=(x, y, c), device_id_type=MESH).wait_recv()
        cp.wait_send()

    return pl.pallas_call(
        body, name="rs_share", in_specs=[ANY], out_specs=ANY, input_output_aliases={0: 0},
        out_shape=jax.ShapeDtypeStruct((R, Wd), buf.dtype),
        scratch_shapes=[pltpu.SemaphoreType.DMA, pltpu.SemaphoreType.DMA],
    )(buf)


def _allreduce_small(v, name="allreduce_small"):
    R, Wd = v.shape

    def body(v_ref, o_ref, buf, send_sems, recv_sems):
        x, y, c, _ = _place()
        me = 4 * x + 2 * y + c
        buf[me] = v_ref[...]
        sends = []
        for k in range(1, 8):
            peer = ((x + (k >> 2)) % 2, (y + ((k >> 1) & 1)) % 2, (c + (k & 1)) % 2)
            sends.append(pltpu.make_async_remote_copy(
                src_ref=v_ref, dst_ref=buf.at[me], send_sem=send_sems.at[k - 1], recv_sem=recv_sems.at[k - 1],
                device_id=peer, device_id_type=MESH))
        for cp in sends:
            cp.start()
        for k in range(1, 8):
            px, py, pc = (x + (k >> 2)) % 2, (y + ((k >> 1) & 1)) % 2, (c + (k & 1)) % 2
            pltpu.make_async_remote_copy(
                src_ref=v_ref, dst_ref=buf.at[4 * px + 2 * py + pc], send_sem=send_sems.at[k - 1],
                recv_sem=recv_sems.at[k - 1], device_id=(x, y, c), device_id_type=MESH).wait_recv()
        acc = buf[0]
        for d in range(1, 8):
            acc = acc + buf[d]
        o_ref[...] = acc
        for cp in sends:
            cp.wait_send()

    vm = pl.BlockSpec(memory_space=pltpu.VMEM)
    return pl.pallas_call(
        body, name=name, in_specs=[vm], out_specs=vm,
        out_shape=jax.ShapeDtypeStruct((R, Wd), F32),
        scratch_shapes=[pltpu.VMEM((8, R, Wd), F32), pltpu.SemaphoreType.DMA((7,)), pltpu.SemaphoreType.DMA((7,))],
    )(v)


def _neighbours():
    x, y, c = lax.axis_index("x"), lax.axis_index("y"), lax.axis_index("c")
    idx = (2 * x + y, 2 * (1 - x) + y, 2 * x + (1 - y), 2 * (1 - x) + (1 - y))
    return idx, (x, y, c), (1 - x, y, c), (x, 1 - y, c), (x, y, 1 - c)


def _place_own(pack, me_idx):
    R, Wd = pack.shape
    bt = _pick(R, (512, 256))

    def body(i_ref, p_ref, o_ref):
        o_ref[0] = p_ref[...]

    grid_spec = pltpu.PrefetchScalarGridSpec(
        num_scalar_prefetch=1, grid=(R // bt,),
        in_specs=[pl.BlockSpec((bt, Wd), lambda i, idx: (i, 0))],
        out_specs=pl.BlockSpec((1, bt, Wd), lambda i, idx: (idx[0], i, 0)))
    return pl.pallas_call(
        body, name="place_own", grid_spec=grid_spec,
        out_shape=jax.ShapeDtypeStruct((4, R, Wd), pack.dtype),
        compiler_params=_cp(("parallel",)),
    )(me_idx, pack)


def _ag_ring(buf):
    _, R, Wd = buf.shape
    hrows = R // 2
    qrows = hrows // 2

    def body(b_ref, o_ref, send_sems, recv_sems):
        del b_ref
        (me, ix, iy, idg), here, xn, yn, sib = _neighbours()
        c = here[2]
        base = c * hrows
        half = pl.ds(base, hrows)
        q0 = pl.ds(base, qrows)
        q1 = pl.ds(base + qrows, qrows)
        obase = (1 - c) * hrows

        def copy(k, slab, rows, to):
            dst = o_ref.at[slab, rows]
            return pltpu.make_async_remote_copy(
                src_ref=dst, dst_ref=dst,
                send_sem=send_sems.at[k], recv_sem=recv_sems.at[k], device_id=to, device_id_type=MESH)

        sends = [copy(0, me, half, xn), copy(1, me, half, yn)]
        for cp in sends:
            cp.start()
        landed = [(0, ix, half), (1, iy, half), (2, idg, q0), (3, idg, q1)]
        onward = {0: copy(2, ix, q0, yn), 1: copy(3, iy, q1, xn)}
        for k, slab, rows in landed:
            copy(k, slab, rows, here).wait_recv()
            if k in onward:
                onward[k].start()
                sends.append(onward[k])
            cp = copy(4 + k, slab, rows, sib)
            cp.start()
            sends.append(cp)
        theirs = [(4, ix, pl.ds(obase, hrows)), (5, iy, pl.ds(obase, hrows)),
                  (6, idg, pl.ds(obase, qrows)), (7, idg, pl.ds(obase + qrows, qrows))]
        for k, slab, rows in theirs:
            copy(k, slab, rows, here).wait_recv()
        for cp in sends:
            cp.wait_send()

    return pl.pallas_call(
        body, name="ag_weights", in_specs=[ANY], out_specs=ANY, input_output_aliases={0: 0},
        out_shape=jax.ShapeDtypeStruct((4, R, Wd), buf.dtype),
        scratch_shapes=[pltpu.SemaphoreType.DMA((8,)), pltpu.SemaphoreType.DMA((8,))],
    )(buf)


def _rs_diag(r4):
    _, hrows, Wd = r4.shape
    qrows = hrows // 2

    def body(r_ref, o_ref, send_sems, recv_sems):
        (me, ix, iy, idg), here, xn, yn, sib = _neighbours()
        pieces = [(0, pl.ds(0, qrows), xn), (1, pl.ds(qrows, qrows), yn)]
        sends = [pltpu.make_async_remote_copy(
            src_ref=r_ref.at[idg, rows], dst_ref=o_ref.at[k], send_sem=send_sems.at[k],
            recv_sem=recv_sems.at[k], device_id=to, device_id_type=MESH) for k, rows, to in pieces]
        for cp in sends:
            cp.start()
        for k, rows, to in pieces:
            pltpu.make_async_remote_copy(
                src_ref=r_ref.at[idg, rows], dst_ref=o_ref.at[k], send_sem=send_sems.at[k],
                recv_sem=recv_sems.at[k], device_id=here, device_id_type=MESH).wait_recv()
        for cp in sends:
            cp.wait_send()

    return pl.pallas_call(
        body, name="rs_diag", in_specs=[ANY], out_specs=ANY,
        out_shape=jax.ShapeDtypeStruct((2, qrows, Wd), r4.dtype),
        scratch_shapes=[pltpu.SemaphoreType.DMA((2,)), pltpu.SemaphoreType.DMA((2,))],
    )(r4)


def _rs_merge(r4, dg, nbr_idx, tag=""):
    _, hrows, Wd = r4.shape
    bt = _pick(hrows // 2, ROW_BLOCKS)
    nb = hrows // bt
    nq = nb // 2

    def body(i_ref, r_ref, d_ref, o_ref):
        w = pl.program_id(0)
        i = pl.program_id(1)
        merged = jnp.where(w == 0, i >= nq, i < nq)
        add = jnp.where(merged, d_ref[...].astype(F32), 0.0)
        o_ref[...] = (r_ref[...].astype(F32) + add).astype(o_ref.dtype)

    grid_spec = pltpu.PrefetchScalarGridSpec(
        num_scalar_prefetch=1, grid=(2, nb),
        in_specs=[pl.BlockSpec((1, bt, Wd), lambda w, i, idx: (idx[w], i, 0)),
                  pl.BlockSpec((1, bt, Wd), lambda w, i, idx: (1 - w, jnp.clip(i - (1 - w) * nq, 0, nq - 1), 0))],
        out_specs=pl.BlockSpec((1, bt, Wd), lambda w, i, idx: (w, i, 0)))
    return pl.pallas_call(
        body, name="rs_merge" + tag, grid_spec=grid_spec,
        out_shape=jax.ShapeDtypeStruct((2, hrows, Wd), r4.dtype),
        compiler_params=_cp(("parallel", "parallel")),
    )(nbr_idx, r4, dg)


def _rs_direct(m2):
    _, hrows, Wd = m2.shape

    def body(m_ref, o_ref, send_sems, recv_sems):
        _, here, xn, yn, sib = _neighbours()
        sends = [pltpu.make_async_remote_copy(
            src_ref=m_ref.at[k], dst_ref=o_ref.at[k], send_sem=send_sems.at[k], recv_sem=recv_sems.at[k],
            device_id=to, device_id_type=MESH) for k, to in ((0, xn), (1, yn))]
        for cp in sends:
            cp.start()
        for k in range(2):
            pltpu.make_async_remote_copy(
                src_ref=m_ref.at[k], dst_ref=o_ref.at[k], send_sem=send_sems.at[k], recv_sem=recv_sems.at[k],
                device_id=here, device_id_type=MESH).wait_recv()
        for cp in sends:
            cp.wait_send()

    return pl.pallas_call(
        body, name="rs_direct", in_specs=[ANY], out_specs=ANY,
        out_shape=jax.ShapeDtypeStruct((2, hrows, Wd), m2.dtype),
        scratch_shapes=[pltpu.SemaphoreType.DMA((2,)), pltpu.SemaphoreType.DMA((2,))],
    )(m2)


def _rs_final(r4, got, me_c, tag=""):
    _, hrows, Wd = r4.shape
    bt = _pick(hrows, ROW_BLOCKS)
    nb = hrows // bt

    def body(i_ref, r_ref, g_ref, o_ref):
        o_ref[...] = (r_ref[0].astype(F32) + g_ref[0].astype(F32)) + g_ref[1].astype(F32)

    grid_spec = pltpu.PrefetchScalarGridSpec(
        num_scalar_prefetch=1, grid=(nb,),
        in_specs=[pl.BlockSpec((1, bt, Wd), lambda i, idx: (idx[0], i, 0)),
                  pl.BlockSpec((2, bt, Wd), lambda i, idx: (0, i, 0))],
        out_specs=pl.BlockSpec((bt, Wd), lambda i, idx: (idx[1] * nb + i, 0)))
    return pl.pallas_call(
        body, name="rs_final" + tag, grid_spec=grid_spec,
        out_shape=jax.ShapeDtypeStruct((2 * hrows, Wd), F32),
        compiler_params=_cp(("parallel",)),
    )(me_c, r4, got)


SHARDED = (
    ("w_in", (2, 1024, 1730), 2),
    ("w_branch", (2, 3, 512, 256), 3),
    ("w_mix_out", (2, 256, 1024), 1),
    ("w_xq", (2, 256, 1024), 1),
    ("w_xkv", (2, 1024, 512), 2),
    ("w_xo", (2, 256, 1024), 1),
    ("w_ffn_gate", (2, 1024, 704), 2),
    ("w_ffn_up", (2, 1024, 704), 2),
    ("w_ffn_down", (2, 704, 1024), 1),
    ("conv_w", (2, 3, 128), 2),
)
PACK_W = 1024
PACK_ELEMS = sum(int(np.prod(s)) for _, s, _ in SHARDED)
PACK_ROWS = -(-PACK_ELEMS // (PACK_W * 1024)) * 1024


def _pack(parts, dtype):
    flat = jnp.concatenate([p.astype(dtype).reshape(-1) for p in parts]
                           + [jnp.zeros((PACK_ROWS * PACK_W - PACK_ELEMS,), dtype)])
    return flat.reshape(PACK_ROWS, PACK_W)


def _unpack(pack):
    flat = pack.reshape(-1)
    out, off = {}, 0
    for name, shape, _ in SHARDED:
        n = int(np.prod(shape))
        out[name] = flat[off:off + n].reshape(shape)
        off += n
    return out


SMALL = (
    ("mix_norm_g", (2, 1024)), ("xattn_norm_g", (2, 1024)), ("mem_norm_g", (2, 1024)),
    ("ffn_norm_g", (2, 1024)), ("final_norm_g", (1024,)),
    ("forget_bias", (2, 8)), ("sink", (2, 8)), ("rel_bias", (32, 8)),
)
SMALL_AND_CONV = SMALL + (("conv_w", (2, 3, 512)),)


def _small_rows(spec):
    rows = sum(int(np.prod(s)) // 128 if s[-1] % 128 == 0 else s[0] for _, s in spec)
    return -(-rows // 8) * 8


def _pack_small(vals, spec=SMALL):
    rows = []
    for name, shape in spec:
        v = vals[name].astype(F32)
        if shape[-1] % 128 == 0:
            rows.append(v.reshape(-1, 128))
        else:
            rows.append(jnp.pad(v, ((0, 0), (0, 120))))
    rows = jnp.concatenate(rows, axis=0)
    return jnp.pad(rows, ((0, _small_rows(spec) - rows.shape[0]), (0, 0)))


def _unpack_small(pack, spec=SMALL):
    out, off = {}, 0
    for name, shape in spec:
        if shape[-1] % 128 == 0:
            n = int(np.prod(shape)) // 128
            out[name] = pack[off:off + n].reshape(shape)
        else:
            n = shape[0]
            out[name] = pack[off:off + n, 0:8]
        off += n
    return out


W_IN_PERM = ((3848, 6920), (0, 3072), (3080, 3848), (3072, 3080))


def _perm_w_in(w):
    parts = [w[:, a:b] for a, b in W_IN_PERM]
    return jnp.concatenate(parts + [jnp.zeros((w.shape[0], PROJ_PAD - IN_COLS), w.dtype)], axis=1)


def _unperm_w_in(p):
    return jnp.concatenate([p[:, 3072:6144], p[:, 6912:6920], p[:, 6144:6912], p[:, 0:3072]], axis=1)


def _pad_row8(v):
    return jnp.pad(v.astype(F32).reshape(1, 8), ((0, 0), (0, 120)))


def _local_step(x, mem, tgt, W, rel_bias):
    T = x.shape[0]
    bucket = jnp.asarray(_bucket_table())
    bias = _swa_bias(rel_bias, bucket, "swa_bias")
    saved = []
    for l in range(DEPTH):
        n = "l%d_" % l
        s = {"x0": x}
        wcat = W["w_in_p"][l]
        h = _rms_fwd(x, W["mix_norm_g"][l:l + 1], n + "mix_norm")
        pm = _mm(h, wcat[:, :PROJ_MAIN], "nn", BF16, n + "proj", bn=768)
        fg = _mm(h, wcat[:, PROJ_MAIN:], "nn", F32, n + "proj_fg")
        fb = _pad_row8(W["forget_bias"][l])
        c_col = _fox_gate_fwd(fg, fb, n + "fox_gate")
        c_row = c_col[:, 0:8].T
        cw = jnp.pad(W["conv_w"][l], ((0, 5), (0, 0)))
        y_conv = _conv_fwd(pm, cw, n + "conv")
        y_fox, lse = _fox2_fwd(pm, c_row, n + "fox")
        sink = _pad_row8(W["sink"][l])
        y_swa, mlse = _swa_fwd(pm, bias, sink, n + "swa")
        ys = (y_conv, y_fox, y_swa)
        us = tuple(_mm(ys[b], W["w_branch"][l][b], "nn", BF16, n + "branch%d" % b) for b in range(3))
        merged = _merge_fwd(pm, us, n + "merge")
        x1 = _mm(merged, W["w_mix_out"][l], "nn", F32, n + "mix_out", res=x)
        xn1 = _rms_fwd(x1, W["xattn_norm_g"][l:l + 1], n + "xattn_norm")
        memn = _rms_fwd(mem, W["mem_norm_g"][l:l + 1], n + "mem_norm")
        qx = _mm(xn1, W["w_xq"][l], "nn", BF16, n + "xq")
        kv = _mm(memn, W["w_xkv"][l], "nn", BF16, n + "xkv")
        ox = _xattn_fwd(qx, kv, n + "xattn")
        x2 = _mm(ox, W["w_xo"][l], "nn", F32, n + "xo", res=x1)
        xn2 = _rms_fwd(x2, W["ffn_norm_g"][l:l + 1], n + "ffn_norm")
        ab = _mm(xn2, W["w_gu"][l], "nn", BF16, n + "ffn_in", bn=512)
        hm = _swiglu_fwd(ab, n + "swiglu")
        x3 = _mm(hm, W["w_ffn_down"][l], "nn", F32, n + "ffn_out", res=x2, bk=1408)
        s.update(h=h, pm=pm, fg=fg, fb=fb, c_col=c_col, c_row=c_row, cw=cw, ys=ys, lse=lse, sink=sink,
                 mlse=mlse, us=us, merged=merged, x1=x1, xn1=xn1, memn=memn, qx=qx, kv=kv, ox=ox,
                 x2=x2, xn2=xn2, ab=ab, hm=hm)
        saved.append(s)
        x = x3

    loss_row, dx, dg_final = _final_loss(x, W["final_norm_g"].reshape(1, D_MODEL), tgt, "final_loss")
    G = {name: [None] * DEPTH for name in
         ("mix_norm_g", "w_in_p", "forget_bias", "conv_w", "sink", "w_branch", "w_mix_out", "xattn_norm_g",
          "mem_norm_g", "w_xq", "w_xkv", "w_xo", "ffn_norm_g", "w_gu", "w_ffn_down")}
    dbias_tot = None
    for l in reversed(range(DEPTH)):
        n = "l%d_" % l
        s = saved[l]
        dhm = _mm(dx, W["w_ffn_down"][l], "nt", BF16, n + "d_hm", bn=1408)
        G["w_ffn_down"][l] = _mm(s["hm"], dx, "tn", BF16, n + "dw_down", bm=1408, bk=1024)
        dab = _swiglu_bwd(s["ab"], dhm, n + "d_swiglu")
        dxn2 = _mm(dab, W["w_gu"][l], "nt", BF16, n + "d_xn2", bk=1408)
        G["w_gu"][l] = _mm(s["xn2"], dab, "tn", BF16, n + "dw_gu", bn=512, bk=2048)
        dx, G["ffn_norm_g"][l] = _rms_bwd(s["x2"], W["ffn_norm_g"][l:l + 1], dxn2, dx, n + "d_ffn_norm")
        dox = _mm(dx, W["w_xo"][l], "nt", BF16, n + "d_ox")
        G["w_xo"][l] = _mm(s["ox"], dx, "tn", BF16, n + "dw_xo", bk=1024)
        dqx, dkv = _xattn_bwd(s["qx"], s["kv"], dox, n + "d_xattn")
        dxn1 = _mm(dqx, W["w_xq"][l], "nt", BF16, n + "d_xn1")
        G["w_xq"][l] = _mm(s["xn1"], dqx, "tn", BF16, n + "dw_xq", bk=2048)
        dmemn = _mm(dkv, W["w_xkv"][l], "nt", BF16, n + "d_memn")
        G["w_xkv"][l] = _mm(s["memn"], dkv, "tn", BF16, n + "dw_xkv")
        _, G["mem_norm_g"][l] = _rms_bwd(mem, W["mem_norm_g"][l:l + 1], dmemn, None, n + "d_mem_norm")
        dx, G["xattn_norm_g"][l] = _rms_bwd(s["x1"], W["xattn_norm_g"][l:l + 1], dxn1, dx, n + "d_xattn_norm")
        dmerged = _mm(dx, W["w_mix_out"][l], "nt", BF16, n + "d_merged")
        G["w_mix_out"][l] = _mm(s["merged"], dx, "tn", BF16, n + "dw_mix_out", bk=1024)
        du0, du1, du2, dgates = _merge_bwd(s["pm"], s["us"], dmerged, n + "d_merge")
        dus = (du0, du1, du2)
        dys = [_mm(dus[b], W["w_branch"][l][b], "nt", BF16, n + "d_y%d" % b) for b in range(3)]
        G["w_branch"][l] = jnp.stack(
            [_mm(s["ys"][b], dus[b], "tn", BF16, n + "dw_branch%d" % b, bk=2048) for b in range(3)])
        dcb, dcc, dcu, dcw = _conv_bwd(s["pm"], s["cw"], dys[0], n + "d_conv")
        G["conv_w"][l] = dcw[0:3]
        delta = _fox_delta(s["ys"][1], dys[1], n + "fox_delta")
        dfq, delta = _fox2_bwd_dq(s["pm"], dys[1], s["c_row"], s["lse"], delta, n + "d_fox_q")
        dfk, dfv, dc = _fox2_bwd_dkv(s["pm"], dys[1], s["c_col"], s["lse"][:, 0:8].T, delta[:, 0:8].T,
                                     n + "d_fox_kv")
        dfg, dfb = _fox_gate_bwd(dc, s["fg"], s["fb"], n + "d_fox_gate")
        G["forget_bias"][l] = dfb[0, 0:8]
        dsq, dkc, dkp, dvc, dvp, dbias, dsink = _swa_bwd(s["pm"], bias, s["sink"], dys[2], s["mlse"],
                                                        n + "d_swa")
        G["sink"][l] = dsink[0, 0:8]
        dbias_tot = dbias if dbias_tot is None else dbias_tot + dbias
        zpad = jnp.zeros((WINDOW, 128), F32)
        dsk = dkc + jnp.concatenate([dkp[WINDOW:], zpad], axis=0)
        dsv = dvc + jnp.concatenate([dvp[WINDOW:], zpad], axis=0)
        dproj = jnp.concatenate([dgates, dcb, dcc, dcu, dfq, dfk, dfv, dsq, dsk.astype(BF16),
                                 dsv.astype(BF16), dfg.astype(BF16)], axis=1)
        dh = _mm(dproj, W["w_in_p"][l], "nt", BF16, n + "d_h", bk=1408)
        G["w_in_p"][l] = _mm(s["h"], dproj, "tn", BF16, n + "dw_in", bn=640, bk=2048)
        dx, G["mix_norm_g"][l] = _rms_bwd(s["x0"], W["mix_norm_g"][l:l + 1], dh, dx, n + "d_mix_norm")
    drb = _swa_dbias_reduce(dbias_tot, bucket, "swa_dbias")
    G["rel_bias"] = drb[:, 0:8]
    G["final_norm_g"] = dg_final.reshape(D_MODEL)
    return loss_row, dx, G


BIG = (
    ("w_in", (2048, 1730)), ("w_branch", (3072, 256)), ("w_mix_out", (512, 1024)), ("w_xq", (512, 1024)),
    ("w_xkv", (2048, 512)), ("w_xo", (512, 1024)), ("w_ffn_gate", (2048, 704)), ("w_ffn_up", (2048, 704)),
    ("w_ffn_down", (1408, 1024)),
)
ROW_BLOCKS = (512, 256, 352, 128, 16)


def _cast_place(w, me_idx, name):
    R, Wd = w.shape
    bt = _pick(R, ROW_BLOCKS)

    def body(i_ref, w_ref, o_ref):
        o_ref[0] = w_ref[...].astype(BF16)

    grid_spec = pltpu.PrefetchScalarGridSpec(
        num_scalar_prefetch=1, grid=(R // bt,),
        in_specs=[pl.BlockSpec((bt, Wd), lambda i, idx: (i, 0))],
        out_specs=pl.BlockSpec((1, bt, Wd), lambda i, idx: (idx[0], i, 0)))
    return pl.pallas_call(
        body, name=name, grid_spec=grid_spec, out_shape=jax.ShapeDtypeStruct((4, R, Wd), BF16),
        compiler_params=_cp(("parallel",)),
    )(me_idx, w)


def _remote(src, dst, sems, k, to):
    send_sems, recv_sems = sems
    return pltpu.make_async_remote_copy(src_ref=src, dst_ref=dst, send_sem=send_sems.at[k], recv_sem=recv_sems.at[k],
                                        device_id=to, device_id_type=MESH)


def _ag_ring_multi(bufs):
    n = len(bufs)

    def body(*refs):
        o = refs[n:2 * n]
        sems = refs[2 * n:]
        (me, ix, iy, idg), here, xn, yn, sib = _neighbours()
        c = here[2]

        def piece(t, k, other):
            h = bufs[t].shape[1] // 2
            q = h // 2
            base = ((1 - c) if other else c) * h
            return [(ix, pl.ds(base, h)), (iy, pl.ds(base, h)), (idg, pl.ds(base, q)), (idg, pl.ds(base + q, q))][k]

        def copy(t, k, slab, rows, to):
            ref = o[t].at[slab, rows]
            return _remote(ref, ref, sems, 8 * t + k, to)

        sends = []

        def go(cp):
            cp.start()
            sends.append(cp)

        for t in range(n):
            h = bufs[t].shape[1] // 2
            go(copy(t, 0, me, pl.ds(c * h, h), xn))
            go(copy(t, 1, me, pl.ds(c * h, h), yn))
        for k in range(4):
            for t in range(n):
                slab, rows = piece(t, k, False)
                copy(t, k, slab, rows, here).wait_recv()
                if k == 0:
                    go(copy(t, 2, ix, piece(t, 2, False)[1], yn))
                if k == 1:
                    go(copy(t, 3, iy, piece(t, 3, False)[1], xn))
                go(copy(t, 4 + k, slab, rows, sib))
        for k in range(4):
            for t in range(n):
                slab, rows = piece(t, k, True)
                copy(t, 4 + k, slab, rows, here).wait_recv()
        for cp in sends:
            cp.wait_send()

    return pl.pallas_call(
        body, name="ag_weights", in_specs=[ANY] * n, out_specs=[ANY] * n,
        input_output_aliases={t: t for t in range(n)},
        out_shape=[jax.ShapeDtypeStruct(b.shape, b.dtype) for b in bufs],
        scratch_shapes=[pltpu.SemaphoreType.DMA((8 * n,)), pltpu.SemaphoreType.DMA((8 * n,))],
    )(*bufs)


def _exchange_multi(srcs, out_shapes, plan, name, aliased=False):
    n = len(srcs)

    def body(*refs):
        ins, outs, sems = refs[:n], refs[n:2 * n], refs[2 * n:]
        places = _neighbours()
        here = places[1]
        per = [plan(t, ins[t], outs[t], places) for t in range(n)]
        width = max(len(p) for p in per)
        started = []
        for t in range(n):
            for k, (src, dst, to, land) in enumerate(per[t]):
                cp = _remote(src, dst, sems, width * t + k, to)
                cp.start()
                started.append(cp)
        for t in range(n):
            for k, (src, dst, to, land) in enumerate(per[t]):
                _remote(land, land, sems, width * t + k, here).wait_recv()
        for cp in started:
            cp.wait_send()

    nsem = 2 * n
    return pl.pallas_call(
        body, name=name, in_specs=[ANY] * n, out_specs=[ANY] * n,
        input_output_aliases={t: t for t in range(n)} if aliased else {},
        out_shape=[jax.ShapeDtypeStruct(s, d) for s, d in out_shapes],
        scratch_shapes=[pltpu.SemaphoreType.DMA((nsem,)), pltpu.SemaphoreType.DMA((nsem,))],
    )(*srcs)


def _rs_sibling_multi(gs):
    def plan(t, g, o, places):
        (_, here, _, _, sib) = places
        h = gs[t].shape[1] // 2
        return [(g.at[:, pl.ds((1 - here[2]) * h, h)], o, sib, o)]

    return _exchange_multi(gs, [((4, g.shape[1] // 2, g.shape[2]), g.dtype) for g in gs], plan, "rs_sibling")


def _rs_diag_multi(rs):
    def plan(t, r, o, places):
        ((_, _, _, idg), _, xn, yn, _) = places
        q = rs[t].shape[1] // 2
        return [(r.at[idg, pl.ds(0, q)], o.at[0], xn, o.at[0]), (r.at[idg, pl.ds(q, q)], o.at[1], yn, o.at[1])]

    return _exchange_multi(rs, [((2, r.shape[1] // 2, r.shape[2]), r.dtype) for r in rs], plan, "rs_diag")


def _rs_direct_multi(ms):
    def plan(t, m, o, places):
        (_, _, xn, yn, _) = places
        return [(m.at[0], o.at[0], xn, o.at[0]), (m.at[1], o.at[1], yn, o.at[1])]

    return _exchange_multi(ms, [(m.shape, m.dtype) for m in ms], plan, "rs_direct")


def _rs_share_multi(bufs):
    def plan(t, b, o, places):
        (_, here, _, _, sib) = places
        h = bufs[t].shape[0] // 2
        mine = o.at[pl.ds(here[2] * h, h)]
        return [(mine, mine, sib, o.at[pl.ds((1 - here[2]) * h, h)])]

    return _exchange_multi(bufs, [(b.shape, b.dtype) for b in bufs], plan, "rs_share", aliased=True)


def kernel(x, mem, mix_norm_g, w_in, forget_bias, conv_w, sink, w_branch, w_mix_out, rel_bias, xattn_norm_g, mem_norm_g, w_xq, w_xkv, w_xo, ffn_norm_g, w_ffn_gate, w_ffn_up, w_ffn_down, final_norm_g, loss_target, m_mix_norm_g, m_w_in, m_forget_bias, m_conv_w, m_sink, m_w_branch, m_w_mix_out, m_rel_bias, m_xattn_norm_g, m_mem_norm_g, m_w_xq, m_w_xkv, m_w_xo, m_ffn_norm_g, m_w_ffn_gate, m_w_ffn_up, m_w_ffn_down, m_final_norm_g, v_mix_norm_g, v_w_in, v_forget_bias, v_conv_w, v_sink, v_w_branch, v_w_mix_out, v_rel_bias, v_xattn_norm_g, v_mem_norm_g, v_w_xq, v_w_xkv, v_w_xo, v_ffn_norm_g, v_w_ffn_gate, v_w_ffn_up, v_w_ffn_down, v_final_norm_g):
    order = ("mix_norm_g", "w_in", "forget_bias", "conv_w", "sink", "w_branch", "w_mix_out", "rel_bias",
             "xattn_norm_g", "mem_norm_g", "w_xq", "w_xkv", "w_xo", "ffn_norm_g", "w_ffn_gate", "w_ffn_up",
             "w_ffn_down", "final_norm_g")
    w_sh = dict(zip(order, (mix_norm_g, w_in, forget_bias, conv_w, sink, w_branch, w_mix_out, rel_bias,
                            xattn_norm_g, mem_norm_g, w_xq, w_xkv, w_xo, ffn_norm_g, w_ffn_gate, w_ffn_up,
                            w_ffn_down, final_norm_g)))
    m_sh = dict(zip(order, (m_mix_norm_g, m_w_in, m_forget_bias, m_conv_w, m_sink, m_w_branch, m_w_mix_out,
                            m_rel_bias, m_xattn_norm_g, m_mem_norm_g, m_w_xq, m_w_xkv, m_w_xo, m_ffn_norm_g,
                            m_w_ffn_gate, m_w_ffn_up, m_w_ffn_down, m_final_norm_g)))
    v_sh = dict(zip(order, (v_mix_norm_g, v_w_in, v_forget_bias, v_conv_w, v_sink, v_w_branch, v_w_mix_out,
                            v_rel_bias, v_xattn_norm_g, v_mem_norm_g, v_w_xq, v_w_xkv, v_w_xo, v_ffn_norm_g,
                            v_w_ffn_gate, v_w_ffn_up, v_w_ffn_down, v_final_norm_g)))

    xi, yi, ci = lax.axis_index("x"), lax.axis_index("y"), lax.axis_index("c")
    as_idx = lambda *v: jnp.stack([jnp.asarray(t, I32) for t in v])
    me = 2 * xi + yi
    big = [name for name, _ in BIG]
    two_d = dict(BIG)
    gathered = dict(zip(big, _ag_ring_multi(
        [_cast_place(w_sh[name].reshape(two_d[name]), as_idx(me), "place_" + name) for name in big])))
    conv_part = lax.dynamic_update_slice_in_dim(jnp.zeros((DEPTH, 3, BRANCH), F32), 0.5 * conv_w, 128 * me, axis=2)
    conv_full = _allreduce_small(conv_part.reshape(-1, 128), "allgather_conv").reshape(DEPTH, 3, BRANCH)

    def lay(name, l):
        g = gathered[name]
        return g.reshape(4, DEPTH, g.shape[1] // DEPTH, g.shape[2])[:, l]

    def by_cols(name, l):
        g = lay(name, l)
        return jnp.moveaxis(g, 0, 1).reshape(g.shape[1], 4 * g.shape[2])

    def by_rows(name, l):
        g = lay(name, l)
        return g.reshape(4 * g.shape[1], g.shape[2])

    W = {k: w_sh[k] for k in ("mix_norm_g", "forget_bias", "sink", "xattn_norm_g", "mem_norm_g",
                              "ffn_norm_g", "final_norm_g")}
    W["conv_w"] = conv_full
    W["w_in_p"] = [_perm_w_in(by_cols("w_in", l)) for l in range(DEPTH)]
    W["w_gu"] = [jnp.concatenate([by_cols("w_ffn_gate", l), by_cols("w_ffn_up", l)], axis=1) for l in range(DEPTH)]
    W["w_xkv"] = [by_cols("w_xkv", l) for l in range(DEPTH)]
    W["w_branch"] = [jnp.transpose(lay("w_branch", l).reshape(4, 3, BRANCH, 256), (1, 2, 0, 3)).reshape(3, BRANCH, D_MODEL)
                     for l in range(DEPTH)]
    for k in ("w_mix_out", "w_xq", "w_xo", "w_ffn_down"):
        W[k] = [by_rows(k, l) for l in range(DEPTH)]
    loss_row, dx, G = _local_step(x[0], mem[0], loss_target[0], W, rel_bias)

    def to_cols(g):
        return jnp.moveaxis(g.reshape(g.shape[0], 4, g.shape[1] // 4), 1, 0)

    def to_rows(g):
        return g.reshape(4, g.shape[0] // 4, g.shape[1])

    per_layer = {
        "w_in": [to_cols(_unperm_w_in(G["w_in_p"][l])) for l in range(DEPTH)],
        "w_branch": [jnp.transpose(G["w_branch"][l].reshape(3, BRANCH, 4, 256), (2, 0, 1, 3)).reshape(4, 3 * BRANCH, 256)
                     for l in range(DEPTH)],
        "w_mix_out": [to_rows(g) for g in G["w_mix_out"]],
        "w_xq": [to_rows(g) for g in G["w_xq"]],
        "w_xkv": [to_cols(g) for g in G["w_xkv"]],
        "w_xo": [to_rows(g) for g in G["w_xo"]],
        "w_ffn_gate": [to_cols(G["w_gu"][l][:, :D_FF]) for l in range(DEPTH)],
        "w_ffn_up": [to_cols(G["w_gu"][l][:, D_FF:]) for l in range(DEPTH)],
        "w_ffn_down": [to_rows(g) for g in G["w_ffn_down"]],
    }
    g4 = [jnp.concatenate(per_layer[name], axis=1).astype(BF16) for name in big]
    sib = _rs_sibling_multi(g4)
    pair = [_rs_add_pair(g4[t], sib[t], as_idx(ci), "_" + big[t]) for t in range(len(big))]
    diag = _rs_diag_multi(pair)
    nbrs = as_idx(2 * (1 - xi) + yi, 2 * xi + (1 - yi))
    merged = [_rs_merge(pair[t], diag[t], nbrs, "_" + big[t]) for t in range(len(big))]
    got = _rs_direct_multi(merged)
    reduced = _rs_share_multi([_rs_final(pair[t], got[t], as_idx(me, ci), "_" + big[t]) for t in range(len(big))])

    small = _unpack_small(_allreduce_small(_pack_small({
        "mix_norm_g": jnp.concatenate(G["mix_norm_g"], axis=0),
        "xattn_norm_g": jnp.concatenate(G["xattn_norm_g"], axis=0),
        "mem_norm_g": jnp.concatenate(G["mem_norm_g"], axis=0),
        "ffn_norm_g": jnp.concatenate(G["ffn_norm_g"], axis=0),
        "final_norm_g": G["final_norm_g"],
        "forget_bias": jnp.stack(G["forget_bias"]),
        "sink": jnp.stack(G["sink"]),
        "rel_bias": G["rel_bias"],
        "conv_w": jnp.stack(G["conv_w"]),
    }, SMALL_AND_CONV)), SMALL_AND_CONV)
    grads = {name: reduced[t].reshape(w_sh[name].shape) for t, name in enumerate(big)}
    grads.update(small)
    grads["conv_w"] = lax.dynamic_slice_in_dim(small["conv_w"], 128 * me, 128, axis=2)

    sm_names = [name for name, _ in SMALL]
    sd, sm_, sv_ = _adamw(_pack_small({k: w_sh[k] for k in sm_names}), _pack_small({k: grads[k] for k in sm_names}),
                          _pack_small({k: m_sh[k] for k in sm_names}), _pack_small({k: v_sh[k] for k in sm_names}),
                          "adamw_small")
    delta, new_m, new_v = _unpack_small(sd), _unpack_small(sm_), _unpack_small(sv_)
    for name, shape in BIG + (("conv_w", (6, 128)),):
        full_shape = w_sh[name].shape
        d, nm, nv = _adamw(w_sh[name].reshape(shape), grads[name].reshape(shape), m_sh[name].reshape(shape),
                           v_sh[name].reshape(shape), "adamw_" + name)
        delta[name], new_m[name], new_v[name] = d.reshape(full_shape), nm.reshape(full_shape), nv.reshape(full_shape)

    loss = lax.psum(loss_row[0, 0], ("x", "y", "c"))
    return (loss, dx[None], *[grads[k] for k in order], *[delta[k] for k in order],
            *[new_m[k] for k in order], *[new_v[k] for k in order])
```

```python
import math

import numpy as np
import jax
import jax.numpy as jnp
from jax import lax
from jax.experimental import pallas as pl
from jax.experimental.pallas import tpu as pltpu

F32 = jnp.float32
BF16 = jnp.bfloat16
I32 = jnp.int32

D_MODEL = 1024
DEPTH = 2
HEAD_DIM = 64
BRANCH = 512
N_BUCKETS = 32
WINDOW = 128
MEM_LEN = 256
X_HEADS = 4
X_HEAD_DIM = 256
D_FF = 2816
IN_COLS = 6920
PROJ_MAIN = 6912
PROJ_PAD = 7040
RMS_EPS = 1e-6
NEG = -1e30
ATT_SCALE = 0.125
X_SCALE = 0.0625

ADAM_LR = 0.001
ADAM_B1 = 0.9
ADAM_B2 = 0.999
ADAM_EPS = 1e-08
ADAM_WD = 0.01
ADAM_STEP = 10

VMEM_LIMIT = 48 * 1024 * 1024
MESH = pl.DeviceIdType.MESH

CB_GATE = (0, 1, 2)
CB_B, CB_C, CB_U, CB_FQ, CB_FK, CB_FV, CB_SQ = 6, 7, 8, 9, 10, 11, 12
CB_SK, CB_SV = 52, 53


def _cp(sem):
    return pltpu.CompilerParams(dimension_semantics=sem, vmem_limit_bytes=VMEM_LIMIT)


def _pick(n, prefs):
    for p in prefs:
        if p <= n and n % p == 0:
            return p
    return n


def _dot(a, b, dims):
    return lax.dot_general(a, b, (dims, ((), ())), preferred_element_type=F32)


def _dot_nn(a, b):
    return _dot(a, b, ((1,), (0,)))


def _dot_nt(a, b):
    return _dot(a, b, ((1,), (1,)))


def _dot_tn(a, b):
    return _dot(a, b, ((0,), (0,)))


def _mm(a, b, mode, out_dtype, name, res=None, bm=1024, bn=1024, bk=1024):
    if mode == "nn":
        (M, K), (K2, N) = a.shape, b.shape
    elif mode == "nt":
        (M, K), (N, K2) = a.shape, b.shape
    else:
        (K, M), (K2, N) = a.shape, b.shape
    assert K == K2, (name, a.shape, b.shape)
    bm = _pick(M, (bm, 1024, 512, 256, 128))
    bn = _pick(N, (bn, 1024, 768, 640, 512, 384, 256, 128))
    bk = _pick(K, (bk, 1024, 768, 640, 512, 384, 256, 128))
    nk = K // bk
    if mode == "tn":
        a_spec = pl.BlockSpec((bk, bm), lambda i, j, k: (k, i))
    else:
        a_spec = pl.BlockSpec((bm, bk), lambda i, j, k: (i, k))
    if mode == "nt":
        b_spec = pl.BlockSpec((bn, bk), lambda i, j, k: (j, k))
    else:
        b_spec = pl.BlockSpec((bk, bn), lambda i, j, k: (k, j))
    dims = {"nn": ((1,), (0,)), "nt": ((1,), (1,)), "tn": ((0,), (0,))}[mode]
    o_spec = pl.BlockSpec((bm, bn), lambda i, j, k: (i, j))
    has_res = res is not None

    def body(*refs):
        if has_res:
            a_ref, b_ref, r_ref, o_ref = refs[:4]
            scr = refs[4:]
        else:
            a_ref, b_ref, o_ref = refs[:3]
            r_ref = None
            scr = refs[3:]
        p = _dot(a_ref[...].astype(BF16), b_ref[...].astype(BF16), dims)
        if nk == 1:
            if has_res:
                p = p + r_ref[...]
            o_ref[...] = p.astype(out_dtype)
        else:
            acc = scr[0]
            k = pl.program_id(2)

            @pl.when(k == 0)
            def _():
                acc[...] = p

            @pl.when(k > 0)
            def _():
                acc[...] += p

            @pl.when(k == nk - 1)
            def _():
                r = acc[...]
                if has_res:
                    r = r + r_ref[...]
                o_ref[...] = r.astype(out_dtype)

    ins = [a, b] + ([res] if has_res else [])
    in_specs = [a_spec, b_spec] + ([o_spec] if has_res else [])
    return pl.pallas_call(
        body, name=name, grid=(M // bm, N // bn, nk),
        in_specs=in_specs, out_specs=o_spec,
        out_shape=jax.ShapeDtypeStruct((M, N), out_dtype),
        scratch_shapes=[pltpu.VMEM((bm, bn), F32)] if nk > 1 else [],
        compiler_params=_cp(("parallel", "parallel", "arbitrary")),
    )(*ins)


def _rms_fwd(x, g, name):
    T, Dm = x.shape
    bt = _pick(T, (512, 256))

    def body(x_ref, g_ref, o_ref):
        xv = x_ref[...]
        r = lax.rsqrt(jnp.mean(xv * xv, axis=-1, keepdims=True) + RMS_EPS)
        o_ref[...] = ((xv * r) * g_ref[...]).astype(BF16)

    return pl.pallas_call(
        body, name=name, grid=(T // bt,),
        in_specs=[pl.BlockSpec((bt, Dm), lambda i: (i, 0)), pl.BlockSpec((1, Dm), lambda i: (0, 0))],
        out_specs=pl.BlockSpec((bt, Dm), lambda i: (i, 0)),
        out_shape=jax.ShapeDtypeStruct((T, Dm), BF16),
        compiler_params=_cp(("parallel",)),
    )(x, g)


def _rms_bwd(x, g, dh, dres, name):
    T, Dm = x.shape
    bt = _pick(T, (512, 256))
    want_dx = dres is not None

    def body(*refs):
        if want_dx:
            x_ref, g_ref, dh_ref, dr_ref, dx_ref, dg_ref = refs
        else:
            x_ref, g_ref, dh_ref, dg_ref = refs
        xv = x_ref[...]
        r = lax.rsqrt(jnp.mean(xv * xv, axis=-1, keepdims=True) + RMS_EPS)
        xh = xv * r
        dhv = dh_ref[...].astype(F32)

        @pl.when(pl.program_id(0) == 0)
        def _():
            dg_ref[...] = jnp.zeros_like(dg_ref)

        dg_ref[...] += jnp.sum(dhv * xh, axis=0, keepdims=True)
        if want_dx:
            dyg = dhv * g_ref[...]
            dx_ref[...] = dr_ref[...] + r * (dyg - xh * jnp.mean(dyg * xh, axis=-1, keepdims=True))

    row = pl.BlockSpec((bt, Dm), lambda i: (i, 0))
    vec = pl.BlockSpec((1, Dm), lambda i: (0, 0))
    if want_dx:
        return pl.pallas_call(
            body, name=name, grid=(T // bt,),
            in_specs=[row, vec, row, row], out_specs=[row, vec],
            out_shape=[jax.ShapeDtypeStruct((T, Dm), F32), jax.ShapeDtypeStruct((1, Dm), F32)],
            compiler_params=_cp(("arbitrary",)),
        )(x, g, dh, dres)
    return None, pl.pallas_call(
        body, name=name, grid=(T // bt,),
        in_specs=[row, vec, row], out_specs=vec,
        out_shape=jax.ShapeDtypeStruct((1, Dm), F32),
        compiler_params=_cp(("arbitrary",)),
    )(x, g, dh)


def _final_loss(x, g, tgt, name):
    T, Dm = x.shape
    bt = _pick(T, (512, 256))

    def body(x_ref, g_ref, t_ref, loss_ref, dx_ref, dg_ref):
        xv = x_ref[...]
        r = lax.rsqrt(jnp.mean(xv * xv, axis=-1, keepdims=True) + RMS_EPS)
        xh = xv * r
        gv = g_ref[...]
        err = xh * gv - t_ref[...]

        @pl.when(pl.program_id(0) == 0)
        def _():
            dg_ref[...] = jnp.zeros_like(dg_ref)
            loss_ref[...] = jnp.zeros_like(loss_ref)

        loss_ref[...] += jnp.sum(err * err) * (0.5 / Dm)
        dy = err * (1.0 / Dm)
        dg_ref[...] += jnp.sum(dy * xh, axis=0, keepdims=True)
        dyg = dy * gv
        dx_ref[...] = r * (dyg - xh * jnp.mean(dyg * xh, axis=-1, keepdims=True))

    row = pl.BlockSpec((bt, Dm), lambda i: (i, 0))
    vec = pl.BlockSpec((1, Dm), lambda i: (0, 0))
    return pl.pallas_call(
        body, name=name, grid=(T // bt,),
        in_specs=[row, vec, row],
        out_specs=[pl.BlockSpec((1, 128), lambda i: (0, 0)), row, vec],
        out_shape=[jax.ShapeDtypeStruct((1, 128), F32), jax.ShapeDtypeStruct((T, Dm), F32),
                   jax.ShapeDtypeStruct((1, Dm), F32)],
        compiler_params=_cp(("arbitrary",)),
    )(x, g, tgt)


HALO = 16


def _shift_down(z, zprev, s):
    rolled = pltpu.roll(z, s, 0)
    hp = pltpu.roll(zprev, s, 0)
    row = lax.broadcasted_iota(I32, hp.shape, 0)
    top = jnp.where(row < s, hp, rolled[:HALO])
    return jnp.concatenate([top, rolled[HALO:]], axis=0)


def _shift_up(z, znext, s):
    n = z.shape[0]
    rolled = pltpu.roll(z, n - s, 0)
    hn = pltpu.roll(znext, HALO - s, 0)
    row = lax.broadcasted_iota(I32, hn.shape, 0)
    bot = jnp.where(row >= HALO - s, hn, rolled[n - HALO:])
    return jnp.concatenate([rolled[:n - HALO], bot], axis=0)


def _conv_fwd(pm, cw, name):
    T = pm.shape[0]
    bt = _pick(T, (512, 256))
    hb = bt // HALO

    def body(b_ref, c_ref, u_ref, cp_ref, up_ref, w_ref, o_ref):
        i = pl.program_id(0)
        z = c_ref[...].astype(F32) * u_ref[...].astype(F32)
        zp = cp_ref[...].astype(F32) * up_ref[...].astype(F32)
        zp = jnp.where(i > 0, zp, 0.0)
        w = w_ref[...]
        y = w[2:3] * z + w[1:2] * _shift_down(z, zp, 1) + w[0:1] * _shift_down(z, zp, 2)
        o_ref[...] = (b_ref[...].astype(F32) * y).astype(BF16)

    def col(cb):
        return pl.BlockSpec((bt, BRANCH), lambda i: (i, cb))

    def prev(cb):
        return pl.BlockSpec((HALO, BRANCH), lambda i: (jnp.maximum(i * hb - 1, 0), cb))

    return pl.pallas_call(
        body, name=name, grid=(T // bt,),
        in_specs=[col(CB_B), col(CB_C), col(CB_U), prev(CB_C), prev(CB_U),
                  pl.BlockSpec((8, BRANCH), lambda i: (0, 0))],
        out_specs=pl.BlockSpec((bt, BRANCH), lambda i: (i, 0)),
        out_shape=jax.ShapeDtypeStruct((T, BRANCH), BF16),
        compiler_params=_cp(("parallel",)),
    )(pm, pm, pm, pm, pm, cw)


def _conv_bwd(pm, cw, dy, name):
    T = pm.shape[0]
    bt = _pick(T, (512, 256))
    hb = bt // HALO
    nb = T // bt
    last_h = T // HALO - 1

    def body(b_ref, c_ref, u_ref, cp_ref, up_ref, bn_ref, dy_ref, dyn_ref, w_ref,
             db_ref, dc_ref, du_ref, dw_ref):
        i = pl.program_id(0)
        cv = c_ref[...].astype(F32)
        uv = u_ref[...].astype(F32)
        bv = b_ref[...].astype(F32)
        z = cv * uv
        zp = jnp.where(i > 0, cp_ref[...].astype(F32) * up_ref[...].astype(F32), 0.0)
        w = w_ref[...]
        z1 = _shift_down(z, zp, 1)
        z2 = _shift_down(z, zp, 2)
        yc = w[2:3] * z + w[1:2] * z1 + w[0:1] * z2
        dyv = dy_ref[...].astype(F32)
        db_ref[...] = (dyv * yc).astype(BF16)
        g = dyv * bv
        gn = jnp.where(i < nb - 1, dyn_ref[...].astype(F32) * bn_ref[...].astype(F32), 0.0)
        dz = w[2:3] * g + w[1:2] * _shift_up(g, gn, 1) + w[0:1] * _shift_up(g, gn, 2)
        dc_ref[...] = (dz * uv).astype(BF16)
        du_ref[...] = (dz * cv).astype(BF16)

        @pl.when(i == 0)
        def _():
            dw_ref[...] = jnp.zeros_like(dw_ref)

        dw_ref[0:1, :] += jnp.sum(g * z2, axis=0, keepdims=True)
        dw_ref[1:2, :] += jnp.sum(g * z1, axis=0, keepdims=True)
        dw_ref[2:3, :] += jnp.sum(g * z, axis=0, keepdims=True)

    def col(cb):
        return pl.BlockSpec((bt, BRANCH), lambda i: (i, cb))

    def prev(cb):
        return pl.BlockSpec((HALO, BRANCH), lambda i: (jnp.maximum(i * hb - 1, 0), cb))

    def nxt(cb):
        return pl.BlockSpec((HALO, BRANCH), lambda i: (jnp.minimum((i + 1) * hb, last_h), cb))

    own = pl.BlockSpec((bt, BRANCH), lambda i: (i, 0))
    w_spec = pl.BlockSpec((8, BRANCH), lambda i: (0, 0))
    act = jax.ShapeDtypeStruct((T, BRANCH), BF16)
    return pl.pallas_call(
        body, name=name, grid=(nb,),
        in_specs=[col(CB_B), col(CB_C), col(CB_U), prev(CB_C), prev(CB_U), nxt(CB_B), own,
                  pl.BlockSpec((HALO, BRANCH), lambda i: (jnp.minimum((i + 1) * hb, last_h), 0)), w_spec],
        out_specs=[own, own, own, w_spec],
        out_shape=[act, act, act, jax.ShapeDtypeStruct((8, BRANCH), F32)],
        compiler_params=_cp(("arbitrary",)),
    )(pm, pm, pm, pm, pm, pm, dy, dy, cw)


def _log_sigmoid(z):
    return jnp.minimum(z, 0.0) - jnp.log(1.0 + jnp.exp(-jnp.abs(z)))


def _fox_gate_fwd(fg, fb, name):
    T = fg.shape[0]
    bt = _pick(T, (256,))

    def body(f_ref, b_ref, c_ref, carry):
        @pl.when(pl.program_id(0) == 0)
        def _():
            carry[...] = jnp.zeros_like(carry)

        xv = _log_sigmoid(f_ref[...] + b_ref[...])
        row = lax.broadcasted_iota(I32, xv.shape, 0)
        s = 1
        while s < bt:
            xv = xv + jnp.where(row >= s, pltpu.roll(xv, s, 0), 0.0)
            s *= 2
        xv = xv + carry[...]
        c_ref[...] = xv
        carry[...] = xv[bt - 1:bt, :]

    blk = pl.BlockSpec((bt, 128), lambda i: (i, 0))
    return pl.pallas_call(
        body, name=name, grid=(T // bt,),
        in_specs=[blk, pl.BlockSpec((1, 128), lambda i: (0, 0))],
        out_specs=blk, out_shape=jax.ShapeDtypeStruct((T, 128), F32),
        scratch_shapes=[pltpu.VMEM((1, 128), F32)],
        compiler_params=_cp(("arbitrary",)),
    )(fg, fb)


def _fox_gate_bwd(dc, fg, fb, name):
    T = fg.shape[0]
    bt = _pick(T, (256,))
    nb = T // bt

    def body(d_ref, f_ref, b_ref, o_ref, db_ref, carry):
        @pl.when(pl.program_id(0) == 0)
        def _():
            carry[...] = jnp.zeros_like(carry)
            db_ref[...] = jnp.zeros_like(db_ref)

        xv = d_ref[...]
        row = lax.broadcasted_iota(I32, xv.shape, 0)
        s = 1
        while s < bt:
            xv = xv + jnp.where(row < bt - s, pltpu.roll(xv, bt - s, 0), 0.0)
            s *= 2
        xv = xv + carry[...]
        carry[...] = xv[0:1, :]
        z = f_ref[...] + b_ref[...]
        dz = xv * (1.0 / (1.0 + jnp.exp(z)))
        o_ref[...] = dz
        db_ref[...] += jnp.sum(dz, axis=0, keepdims=True)

    blk = pl.BlockSpec((bt, 128), lambda i: (nb - 1 - i, 0))
    vec = pl.BlockSpec((1, 128), lambda i: (0, 0))
    return pl.pallas_call(
        body, name=name, grid=(nb,),
        in_specs=[blk, blk, vec], out_specs=[blk, vec],
        out_shape=[jax.ShapeDtypeStruct((T, 128), F32), jax.ShapeDtypeStruct((1, 128), F32)],
        scratch_shapes=[pltpu.VMEM((1, 128), F32)],
        compiler_params=_cp(("arbitrary",)),
    )(dc, fg, fb)


def _lane_lo(shape):
    return lax.broadcasted_iota(I32, shape, 1) < HEAD_DIM


def _put_col(shape, h, col):
    lane = lax.broadcasted_iota(I32, shape, 1)
    return jnp.where(lane == h, col, 0.0)


def _fox_fwd(pm, c_col, c_row, name):
    T = pm.shape[0]
    bq = _pick(T, (512, 256))
    bk = bq
    nq = T // bq

    def body(q_ref, k_ref, v_ref, cq_ref, ck_ref, o_ref, lse_ref, acc, m_s, l_s):
        qi = pl.program_id(0)
        ki = pl.program_id(1)

        @pl.when(ki == 0)
        def _():
            acc[...] = jnp.zeros_like(acc)
            m_s[...] = jnp.full_like(m_s, NEG)
            l_s[...] = jnp.zeros_like(l_s)

        @pl.when(ki <= qi)
        def _():
            row = lax.broadcasted_iota(I32, (bq, bk), 0) + qi * bq
            colv = lax.broadcasted_iota(I32, (bq, bk), 1) + ki * bk
            causal = colv <= row
            klo = _lane_lo((bk, 128))
            qlo = _lane_lo((bq, 128))
            cq = cq_ref[...]
            ck = ck_ref[...]
            for p in range(4):
                sl = slice(128 * p, 128 * p + 128)
                qp = q_ref[:, sl] * ATT_SCALE
                kp = k_ref[:, sl]
                vp = v_ref[:, sl]
                kz = jnp.zeros_like(kp)
                ks = (jnp.where(klo, kp, kz), jnp.where(klo, kz, kp))
                alphas, pvs = [], []
                for j in range(2):
                    h = 2 * p + j
                    s = _dot_nt(qp, ks[j]) + (cq[:, h:h + 1] - ck[h:h + 1, :])
                    s = jnp.where(causal, s, NEG)
                    m_old = m_s[h][:, 0:1]
                    m_new = jnp.maximum(m_old, jnp.max(s, axis=-1, keepdims=True))
                    alpha = jnp.exp(m_old - m_new)
                    pe = jnp.exp(s - m_new)
                    l_new = alpha * l_s[h][:, 0:1] + jnp.sum(pe, axis=-1, keepdims=True)
                    m_s[h] = jnp.broadcast_to(m_new, (bq, 128))
                    l_s[h] = jnp.broadcast_to(l_new, (bq, 128))
                    alphas.append(alpha)
                    pvs.append(_dot_nn(pe.astype(BF16), vp))
                a = jnp.where(qlo, alphas[0], alphas[1])
                acc[:, sl] = a * acc[:, sl] + jnp.where(qlo, pvs[0], pvs[1])

        @pl.when(ki == nq - 1)
        def _():
            qlo = _lane_lo((bq, 128))
            lse = jnp.zeros((bq, 128), F32)
            for p in range(4):
                sl = slice(128 * p, 128 * p + 128)
                l0 = l_s[2 * p][:, 0:1]
                l1 = l_s[2 * p + 1][:, 0:1]
                o_ref[:, sl] = (acc[:, sl] / jnp.where(qlo, l0, l1)).astype(BF16)
                lse = lse + _put_col((bq, 128), 2 * p, m_s[2 * p][:, 0:1] + jnp.log(l0))
                lse = lse + _put_col((bq, 128), 2 * p + 1, m_s[2 * p + 1][:, 0:1] + jnp.log(l1))
            lse_ref[...] = lse

    return pl.pallas_call(
        body, name=name, grid=(nq, nq),
        in_specs=[pl.BlockSpec((bq, BRANCH), lambda i, k: (i, CB_FQ)),
                  pl.BlockSpec((bk, BRANCH), lambda i, k: (jnp.minimum(k, i), CB_FK)),
                  pl.BlockSpec((bk, BRANCH), lambda i, k: (jnp.minimum(k, i), CB_FV)),
                  pl.BlockSpec((bq, 128), lambda i, k: (i, 0)),
                  pl.BlockSpec((8, bk), lambda i, k: (0, jnp.minimum(k, i)))],
        out_specs=[pl.BlockSpec((bq, BRANCH), lambda i, k: (i, 0)),
                   pl.BlockSpec((bq, 128), lambda i, k: (i, 0))],
        out_shape=[jax.ShapeDtypeStruct((T, BRANCH), BF16), jax.ShapeDtypeStruct((T, 128), F32)],
        scratch_shapes=[pltpu.VMEM((bq, BRANCH), F32), pltpu.VMEM((8, bq, 128), F32),
                        pltpu.VMEM((8, bq, 128), F32)],
        compiler_params=_cp(("parallel", "arbitrary")),
    )(pm, pm, pm, c_col, c_row)


def _fox_delta(o, do, name):
    T = o.shape[0]
    bt = _pick(T, (512, 256))

    def body(o_ref, d_ref, out_ref):
        prod = o_ref[...].astype(F32) * d_ref[...].astype(F32)
        out = jnp.zeros((bt, 128), F32)
        for h in range(8):
            out = out + _put_col((bt, 128), h, jnp.sum(prod[:, 64 * h:64 * h + 64], axis=-1, keepdims=True))
        out_ref[...] = out

    blk = pl.BlockSpec((bt, BRANCH), lambda i: (i, 0))
    return pl.pallas_call(
        body, name=name, grid=(T // bt,), in_specs=[blk, blk],
        out_specs=pl.BlockSpec((bt, 128), lambda i: (i, 0)),
        out_shape=jax.ShapeDtypeStruct((T, 128), F32),
        compiler_params=_cp(("parallel",)),
    )(o, do)


def _fox_bwd_dq(pm, do, c_col, c_row, lse, delta, name):
    T = pm.shape[0]
    bq = _pick(T, (512, 256))
    bk = bq
    nq = T // bq

    def body(q_ref, k_ref, v_ref, do_ref, cq_ref, ck_ref, lse_ref, dl_ref, dq_ref, dl2_ref, acc, esum):
        qi = pl.program_id(0)
        ki = pl.program_id(1)

        @pl.when(ki == 0)
        def _():
            acc[...] = jnp.zeros_like(acc)
            esum[...] = jnp.zeros_like(esum)

        @pl.when(ki <= qi)
        def _():
            row = lax.broadcasted_iota(I32, (bq, bk), 0) + qi * bq
            colv = lax.broadcasted_iota(I32, (bq, bk), 1) + ki * bk
            causal = colv <= row
            klo = _lane_lo((bk, 128))
            qlo = _lane_lo((bq, 128))
            cq = cq_ref[...]
            ck = ck_ref[...]
            lse_v = lse_ref[...]
            dl_v = dl_ref[...]
            es = jnp.zeros((bq, 128), F32)
            for p in range(4):
                sl = slice(128 * p, 128 * p + 128)
                qp = q_ref[:, sl] * ATT_SCALE
                kp = k_ref[:, sl]
                vp = v_ref[:, sl]
                dop = do_ref[:, sl]
                kz = jnp.zeros_like(kp)
                ks = (jnp.where(klo, kp, kz), jnp.where(klo, kz, kp))
                vs = (jnp.where(klo, vp, kz), jnp.where(klo, kz, vp))
                dqs = []
                for j in range(2):
                    h = 2 * p + j
                    s = _dot_nt(qp, ks[j]) + (cq[:, h:h + 1] - ck[h:h + 1, :])
                    s = jnp.where(causal, s, NEG)
                    pr = jnp.exp(s - lse_v[:, h:h + 1])
                    dp = _dot_nt(dop, vs[j])
                    ds = pr * (dp - dl_v[:, h:h + 1])
                    es = es + _put_col((bq, 128), h, jnp.sum(ds, axis=-1, keepdims=True))
                    dqs.append(_dot_nn(ds.astype(BF16), kp))
                acc[:, sl] += jnp.where(qlo, dqs[0], dqs[1])
            esum[...] += es

        @pl.when(ki == nq - 1)
        def _():
            dq_ref[...] = (acc[...] * ATT_SCALE).astype(BF16)
            dl2_ref[...] = dl_ref[...] + esum[...]

    qb = pl.BlockSpec((bq, 128), lambda i, k: (i, 0))
    return pl.pallas_call(
        body, name=name, grid=(nq, nq),
        in_specs=[pl.BlockSpec((bq, BRANCH), lambda i, k: (i, CB_FQ)),
                  pl.BlockSpec((bk, BRANCH), lambda i, k: (jnp.minimum(k, i), CB_FK)),
                  pl.BlockSpec((bk, BRANCH), lambda i, k: (jnp.minimum(k, i), CB_FV)),
                  pl.BlockSpec((bq, BRANCH), lambda i, k: (i, 0)),
                  qb, pl.BlockSpec((8, bk), lambda i, k: (0, jnp.minimum(k, i))), qb, qb],
        out_specs=[pl.BlockSpec((bq, BRANCH), lambda i, k: (i, 0)), qb],
        out_shape=[jax.ShapeDtypeStruct((T, BRANCH), BF16), jax.ShapeDtypeStruct((T, 128), F32)],
        scratch_shapes=[pltpu.VMEM((bq, BRANCH), F32), pltpu.VMEM((bq, 128), F32)],
        compiler_params=_cp(("parallel", "arbitrary")),
    )(pm, pm, pm, do, c_col, c_row, lse, delta)


def _fox_bwd_dkv(pm, do, c_col, c_row, lse_row, delta_row, name):
    T = pm.shape[0]
    bk = _pick(T, (512, 256))
    bq = bk
    nk = T // bk

    def body(q_ref, k_ref, v_ref, do_ref, cq_ref, ck_ref, lse_ref, dl_ref,
             dk_ref, dv_ref, dc_ref, dk_acc, dv_acc, dc_acc):
        ki = pl.program_id(0)
        qi = pl.program_id(1)

        @pl.when(qi == 0)
        def _():
            dk_acc[...] = jnp.zeros_like(dk_acc)
            dv_acc[...] = jnp.zeros_like(dv_acc)
            dc_acc[...] = jnp.zeros_like(dc_acc)

        @pl.when(qi >= ki)
        def _():
            krow = lax.broadcasted_iota(I32, (bk, bq), 0) + ki * bk
            qcol = lax.broadcasted_iota(I32, (bk, bq), 1) + qi * bq
            causal = krow <= qcol
            qlo = _lane_lo((bq, 128))
            klo = _lane_lo((bk, 128))
            cq = cq_ref[...]
            ck = ck_ref[...]
            lse_v = lse_ref[...]
            dl_v = dl_ref[...]
            dcs = jnp.zeros((bk, 128), F32)
            for p in range(4):
                sl = slice(128 * p, 128 * p + 128)
                qp = q_ref[:, sl]
                kp = k_ref[:, sl] * ATT_SCALE
                vp = v_ref[:, sl]
                dop = do_ref[:, sl]
                qz = jnp.zeros_like(qp)
                qs = (jnp.where(qlo, qp, qz), jnp.where(qlo, qz, qp))
                dos = (jnp.where(qlo, dop, qz), jnp.where(qlo, qz, dop))
                dks, dvs = [], []
                for j in range(2):
                    h = 2 * p + j
                    st = _dot_nt(kp, qs[j]) + (cq[h:h + 1, :] - ck[:, h:h + 1])
                    st = jnp.where(causal, st, NEG)
                    pt = jnp.exp(st - lse_v[h:h + 1, :])
                    dvs.append(_dot_nn(pt.astype(BF16), dop))
                    dpt = _dot_nt(vp, dos[j])
                    dst = pt * (dpt - dl_v[h:h + 1, :])
                    dks.append(_dot_nn(dst.astype(BF16), qp))
                    dcs = dcs - _put_col((bk, 128), h, jnp.sum(dst, axis=-1, keepdims=True))
                dk_acc[:, sl] += jnp.where(klo, dks[0], dks[1])
                dv_acc[:, sl] += jnp.where(klo, dvs[0], dvs[1])
            dc_acc[...] += dcs

        @pl.when(qi == nk - 1)
        def _():
            dk_ref[...] = (dk_acc[...] * ATT_SCALE).astype(BF16)
            dv_ref[...] = dv_acc[...].astype(BF16)
            dc_ref[...] = dc_acc[...]

    qrow = pl.BlockSpec((8, bq), lambda k, i: (0, jnp.maximum(i, k)))
    kb = pl.BlockSpec((bk, BRANCH), lambda k, i: (k, 0))
    return pl.pallas_call(
        body, name=name, grid=(nk, nk),
        in_specs=[pl.BlockSpec((bq, BRANCH), lambda k, i: (jnp.maximum(i, k), CB_FQ)),
                  pl.BlockSpec((bk, BRANCH), lambda k, i: (k, CB_FK)),
                  pl.BlockSpec((bk, BRANCH), lambda k, i: (k, CB_FV)),
                  pl.BlockSpec((bq, BRANCH), lambda k, i: (jnp.maximum(i, k), 0)),
                  qrow, pl.BlockSpec((bk, 128), lambda k, i: (k, 0)), qrow, qrow],
        out_specs=[kb, kb, pl.BlockSpec((bk, 128), lambda k, i: (k, 0))],
        out_shape=[jax.ShapeDtypeStruct((T, BRANCH), BF16), jax.ShapeDtypeStruct((T, BRANCH), BF16),
                   jax.ShapeDtypeStruct((T, 128), F32)],
        scratch_shapes=[pltpu.VMEM((bk, BRANCH), F32), pltpu.VMEM((bk, BRANCH), F32),
                        pltpu.VMEM((bk, 128), F32)],
        compiler_params=_cp(("parallel", "arbitrary")),
    )(pm, pm, pm, do, c_row, c_col, lse_row, delta_row)


FOX_ROWS = 32


FOX_UNROLL = 16


def _row_start(r, rows):
    return r * rows if isinstance(r, int) else pl.multiple_of(r * rows, rows)


def _chunk_loop(n, chunk):
    if n <= FOX_UNROLL:
        for u in range(n):
            chunk(u, 0)
        return

    def outer(i, carry):
        for u in range(FOX_UNROLL):
            chunk(i * FOX_UNROLL + u, carry)
        return carry

    lax.fori_loop(0, n // FOX_UNROLL, outer, 0)


def _tree(op, xs):
    xs = list(xs)
    while len(xs) > 1:
        xs = [op(xs[i], xs[i + 1]) if i + 1 < len(xs) else xs[i] for i in range(0, len(xs), 2)]
    return xs[0]


def _masked_halves(t):
    lo = _lane_lo(t.shape)
    z = jnp.zeros_like(t)
    return jnp.where(lo, t, z), jnp.where(lo, z, t)


def _fox2_fwd(pm, c_row, name):
    T = pm.shape[0]
    bq = _pick(T, (512, 256))
    bk = bq
    nq = T // bq
    R = FOX_ROWS
    ng = bk // 128

    def body(q_ref, k_ref, v_ref, ck_ref, o_ref, lse_ref, acc, m_s, l_s, a_s, s_scr, p_scr):
        qi = pl.program_id(0)
        ki = pl.program_id(1)

        @pl.when(ki == 0)
        def _():
            acc[...] = jnp.zeros_like(acc)
            m_s[...] = jnp.full_like(m_s, NEG)
            l_s[...] = jnp.zeros_like(l_s)

        def block(masked):
            qlo = _lane_lo((bq, 128))
            for p in range(4):
                sl = slice(128 * p, 128 * p + 128)
                qp = q_ref[:, sl] * ATT_SCALE
                vp = v_ref[:, sl]
                ks = _masked_halves(k_ref[:, sl])
                pvs = []
                for j in range(2):
                    h = 2 * p + j
                    s_scr[j] = _dot_nt(qp, ks[j])

                    def chunk(r, carry, h=h, j=j):
                        r0 = _row_start(r, R)
                        rows = pl.ds(r0, R)
                        sc = [s_scr[j, rows, 128 * g:128 * g + 128] - ck_ref[h:h + 1, 128 * g:128 * g + 128]
                              for g in range(ng)]
                        if masked:
                            rid = lax.broadcasted_iota(I32, (R, 128), 0) + r0
                            cid = lax.broadcasted_iota(I32, (R, 128), 1)
                            sc = [jnp.where(cid + 128 * g <= rid, sc[g], NEG) for g in range(ng)]
                        m_old = m_s[h, rows, :]
                        m_new = jnp.maximum(m_old, jnp.max(_tree(jnp.maximum, sc), axis=-1, keepdims=True))
                        alpha = jnp.exp(m_old - m_new)
                        pe = [jnp.exp(sc[g] - m_new) for g in range(ng)]
                        l_s[h, rows, :] = alpha * l_s[h, rows, :] + _tree(jnp.add, pe)
                        m_s[h, rows, :] = m_new
                        a_s[j, rows, :] = alpha
                        for g in range(ng):
                            p_scr[j, rows, 128 * g:128 * g + 128] = pe[g].astype(BF16)
                        return carry

                    _chunk_loop(bq // R, chunk)
                    pvs.append(_dot_nn(p_scr[j], vp))
                acc[:, sl] = jnp.where(qlo, a_s[0], a_s[1]) * acc[:, sl] + jnp.where(qlo, pvs[0], pvs[1])

        @pl.when(ki < qi)
        def _():
            block(False)

        @pl.when(ki == qi)
        def _():
            block(True)

        @pl.when(ki == nq - 1)
        def _():
            qlo = _lane_lo((bq, 128))
            lse = jnp.zeros((bq, 128), F32)
            for p in range(4):
                sl = slice(128 * p, 128 * p + 128)
                l0 = jnp.sum(l_s[2 * p], axis=-1, keepdims=True)
                l1 = jnp.sum(l_s[2 * p + 1], axis=-1, keepdims=True)
                o_ref[:, sl] = (acc[:, sl] / jnp.where(qlo, l0, l1)).astype(BF16)
                lse = lse + _put_col((bq, 128), 2 * p, m_s[2 * p][:, 0:1] + jnp.log(l0))
                lse = lse + _put_col((bq, 128), 2 * p + 1, m_s[2 * p + 1][:, 0:1] + jnp.log(l1))
            lse_ref[...] = lse

    return pl.pallas_call(
        body, name=name, grid=(nq, nq),
        in_specs=[pl.BlockSpec((bq, BRANCH), lambda i, k: (i, CB_FQ)),
                  pl.BlockSpec((bk, BRANCH), lambda i, k: (jnp.minimum(k, i), CB_FK)),
                  pl.BlockSpec((bk, BRANCH), lambda i, k: (jnp.minimum(k, i), CB_FV)),
                  pl.BlockSpec((8, bk), lambda i, k: (0, jnp.minimum(k, i)))],
        out_specs=[pl.BlockSpec((bq, BRANCH), lambda i, k: (i, 0)),
                   pl.BlockSpec((bq, 128), lambda i, k: (i, 0))],
        out_shape=[jax.ShapeDtypeStruct((T, BRANCH), BF16), jax.ShapeDtypeStruct((T, 128), F32)],
        scratch_shapes=[pltpu.VMEM((bq, BRANCH), F32), pltpu.VMEM((8, bq, 128), F32),
                        pltpu.VMEM((8, bq, 128), F32), pltpu.VMEM((2, bq, 128), F32),
                        pltpu.VMEM((2, bq, bk), F32), pltpu.VMEM((2, bq, bk), BF16)],
        compiler_params=_cp(("parallel", "arbitrary")),
    )(pm, pm, pm, c_row)


def _fox2_bwd_dq(pm, do, c_row, lse, delta, name):
    T = pm.shape[0]
    bq = _pick(T, (512, 256))
    bk = bq
    nq = T // bq
    R = FOX_ROWS
    ng = bk // 128

    def body(q_ref, k_ref, v_ref, do_ref, ck_ref, lse_ref, dl_ref, dq_ref, dl2_ref,
             acc, e_s, s_scr, dp_scr, ds_scr):
        qi = pl.program_id(0)
        ki = pl.program_id(1)

        @pl.when(ki == 0)
        def _():
            acc[...] = jnp.zeros_like(acc)
            e_s[...] = jnp.zeros_like(e_s)

        def block(masked):
            qlo = _lane_lo((bq, 128))
            for p in range(4):
                sl = slice(128 * p, 128 * p + 128)
                qp = q_ref[:, sl] * ATT_SCALE
                kp = k_ref[:, sl]
                dop = do_ref[:, sl]
                ks = _masked_halves(kp)
                vs = _masked_halves(v_ref[:, sl])
                dqs = []
                for j in range(2):
                    h = 2 * p + j
                    s_scr[...] = _dot_nt(qp, ks[j])
                    dp_scr[...] = _dot_nt(dop, vs[j])

                    def chunk(r, carry, h=h):
                        r0 = _row_start(r, R)
                        rows = pl.ds(r0, R)
                        lse_c = lse_ref[rows, h:h + 1]
                        dl_c = dl_ref[rows, h:h + 1]
                        if masked:
                            rid = lax.broadcasted_iota(I32, (R, 128), 0) + r0
                            cid = lax.broadcasted_iota(I32, (R, 128), 1)
                        dss = []
                        for g in range(ng):
                            gs = slice(128 * g, 128 * g + 128)
                            sc = s_scr[rows, gs] - ck_ref[h:h + 1, gs]
                            if masked:
                                sc = jnp.where(cid + 128 * g <= rid, sc, NEG)
                            ds = jnp.exp(sc - lse_c) * (dp_scr[rows, gs] - dl_c)
                            ds_scr[rows, gs] = ds.astype(BF16)
                            dss.append(ds)
                        e_s[h, rows, :] += _tree(jnp.add, dss)
                        return carry

                    _chunk_loop(bq // R, chunk)
                    dqs.append(_dot_nn(ds_scr[...], kp))
                acc[:, sl] += jnp.where(qlo, dqs[0], dqs[1])

        @pl.when(ki < qi)
        def _():
            block(False)

        @pl.when(ki == qi)
        def _():
            block(True)

        @pl.when(ki == nq - 1)
        def _():
            dq_ref[...] = (acc[...] * ATT_SCALE).astype(BF16)
            out = dl_ref[...]
            for h in range(8):
                out = out + _put_col((bq, 128), h, jnp.sum(e_s[h], axis=-1, keepdims=True))
            dl2_ref[...] = out

    qb = pl.BlockSpec((bq, 128), lambda i, k: (i, 0))
    return pl.pallas_call(
        body, name=name, grid=(nq, nq),
        in_specs=[pl.BlockSpec((bq, BRANCH), lambda i, k: (i, CB_FQ)),
                  pl.BlockSpec((bk, BRANCH), lambda i, k: (jnp.minimum(k, i), CB_FK)),
                  pl.BlockSpec((bk, BRANCH), lambda i, k: (jnp.minimum(k, i), CB_FV)),
                  pl.BlockSpec((bq, BRANCH), lambda i, k: (i, 0)),
                  pl.BlockSpec((8, bk), lambda i, k: (0, jnp.minimum(k, i))), qb, qb],
        out_specs=[pl.BlockSpec((bq, BRANCH), lambda i, k: (i, 0)), qb],
        out_shape=[jax.ShapeDtypeStruct((T, BRANCH), BF16), jax.ShapeDtypeStruct((T, 128), F32)],
        scratch_shapes=[pltpu.VMEM((bq, BRANCH), F32), pltpu.VMEM((8, bq, 128), F32),
                        pltpu.VMEM((bq, bk), F32), pltpu.VMEM((bq, bk), F32), pltpu.VMEM((bq, bk), BF16)],
        compiler_params=_cp(("parallel", "arbitrary")),
    )(pm, pm, pm, do, c_row, lse, delta)


def _fox2_bwd_dkv(pm, do, c_col, lse_row, delta_row, name):
    T = pm.shape[0]
    bk = _pick(T, (512, 256))
    bq = bk
    nk = T // bk
    R = FOX_ROWS
    ng = bq // 128

    def body(q_ref, k_ref, v_ref, do_ref, ck_ref, lse_ref, dl_ref, dk_ref, dv_ref, dc_ref,
             dk_acc, dv_acc, dc_s, st_scr, dpt_scr, pt_scr, dst_scr):
        ki = pl.program_id(0)
        qi = pl.program_id(1)

        @pl.when(qi == 0)
        def _():
            dk_acc[...] = jnp.zeros_like(dk_acc)
            dv_acc[...] = jnp.zeros_like(dv_acc)
            dc_s[...] = jnp.zeros_like(dc_s)

        def block(masked):
            klo = _lane_lo((bk, 128))
            for p in range(4):
                sl = slice(128 * p, 128 * p + 128)
                qp = q_ref[:, sl]
                kp = k_ref[:, sl] * ATT_SCALE
                vp = v_ref[:, sl]
                dop = do_ref[:, sl]
                qs = _masked_halves(qp)
                dos = _masked_halves(dop)
                dks, dvs = [], []
                for j in range(2):
                    h = 2 * p + j
                    st_scr[...] = _dot_nt(kp, qs[j])
                    dpt_scr[...] = _dot_nt(vp, dos[j])

                    def chunk(r, carry, h=h):
                        r0 = _row_start(r, R)
                        rows = pl.ds(r0, R)
                        ck_c = ck_ref[rows, h:h + 1]
                        if masked:
                            kid = lax.broadcasted_iota(I32, (R, 128), 0) + r0
                            qid = lax.broadcasted_iota(I32, (R, 128), 1)
                        dss = []
                        for g in range(ng):
                            gs = slice(128 * g, 128 * g + 128)
                            st = st_scr[rows, gs] - (ck_c + lse_ref[h:h + 1, gs])
                            if masked:
                                st = jnp.where(kid <= qid + 128 * g, st, NEG)
                            pt = jnp.exp(st)
                            dst = pt * (dpt_scr[rows, gs] - dl_ref[h:h + 1, gs])
                            pt_scr[rows, gs] = pt.astype(BF16)
                            dst_scr[rows, gs] = dst.astype(BF16)
                            dss.append(dst)
                        dc_s[h, rows, :] -= _tree(jnp.add, dss)
                        return carry

                    _chunk_loop(bk // R, chunk)
                    dvs.append(_dot_nn(pt_scr[...], dop))
                    dks.append(_dot_nn(dst_scr[...], qp))
                dk_acc[:, sl] += jnp.where(klo, dks[0], dks[1])
                dv_acc[:, sl] += jnp.where(klo, dvs[0], dvs[1])

        @pl.when(qi > ki)
        def _():
            block(False)

        @pl.when(qi == ki)
        def _():
            block(True)

        @pl.when(qi == nk - 1)
        def _():
            dk_ref[...] = (dk_acc[...] * ATT_SCALE).astype(BF16)
            dv_ref[...] = dv_acc[...].astype(BF16)
            out = jnp.zeros((bk, 128), F32)
            for h in range(8):
                out = out + _put_col((bk, 128), h, jnp.sum(dc_s[h], axis=-1, keepdims=True))
            dc_ref[...] = out

    qrow = pl.BlockSpec((8, bq), lambda k, i: (0, jnp.maximum(i, k)))
    kb = pl.BlockSpec((bk, BRANCH), lambda k, i: (k, 0))
    return pl.pallas_call(
        body, name=name, grid=(nk, nk),
        in_specs=[pl.BlockSpec((bq, BRANCH), lambda k, i: (jnp.maximum(i, k), CB_FQ)),
                  pl.BlockSpec((bk, BRANCH), lambda k, i: (k, CB_FK)),
                  pl.BlockSpec((bk, BRANCH), lambda k, i: (k, CB_FV)),
                  pl.BlockSpec((bq, BRANCH), lambda k, i: (jnp.maximum(i, k), 0)),
                  pl.BlockSpec((bk, 128), lambda k, i: (k, 0)), qrow, qrow],
        out_specs=[kb, kb, pl.BlockSpec((bk, 128), lambda k, i: (k, 0))],
        out_shape=[jax.ShapeDtypeStruct((T, BRANCH), BF16), jax.ShapeDtypeStruct((T, BRANCH), BF16),
                   jax.ShapeDtypeStruct((T, 128), F32)],
        scratch_shapes=[pltpu.VMEM((bk, BRANCH), F32), pltpu.VMEM((bk, BRANCH), F32),
                        pltpu.VMEM((8, bk, 128), F32), pltpu.VMEM((bk, bq), F32), pltpu.VMEM((bk, bq), F32),
                        pltpu.VMEM((bk, bq), BF16), pltpu.VMEM((bk, bq), BF16)],
        compiler_params=_cp(("parallel", "arbitrary")),
    )(pm, pm, pm, do, c_col, lse_row, delta_row)


def _bucket_table():
    tq = np.arange(WINDOW, dtype=np.int32)[:, None]
    sk = np.arange(2 * WINDOW, dtype=np.int32)[None, :]
    n = np.maximum(WINDOW + tq - sk, 0)
    max_exact = N_BUCKETS // 2
    ratio = np.maximum(n, 1).astype(np.float32) / np.float32(max_exact)
    large = max_exact + (np.log(ratio) / np.float32(math.log(WINDOW / max_exact))
                         * np.float32(N_BUCKETS - max_exact)).astype(np.int32)
    large = np.minimum(large, N_BUCKETS - 1)
    return np.where(n < max_exact, n, large).astype(np.int32)


def _swa_bias(rel_bias, bucket, name):
    def body(rb_ref, bk_ref, o_ref):
        bkt = bk_ref[...]
        for h in range(8):
            def step(b, a):
                return a + jnp.where(bkt == b, rb_ref[b, h], 0.0)
            o_ref[h] = lax.fori_loop(0, N_BUCKETS, step, jnp.zeros(bkt.shape, F32))

    return pl.pallas_call(
        body, name=name,
        in_specs=[pl.BlockSpec(memory_space=pltpu.SMEM), pl.BlockSpec(memory_space=pltpu.VMEM)],
        out_specs=pl.BlockSpec(memory_space=pltpu.VMEM),
        out_shape=jax.ShapeDtypeStruct((8, WINDOW, 2 * WINDOW), F32),
    )(rel_bias, bucket)


def _swa_dbias_reduce(dbias, bucket, name):
    def body(d_ref, bk_ref, o_ref):
        bkt = bk_ref[...]
        rowi = lax.broadcasted_iota(I32, (N_BUCKETS, 128), 0)
        lane = lax.broadcasted_iota(I32, (N_BUCKETS, 128), 1)
        out = jnp.zeros((N_BUCKETS, 128), F32)
        for h in range(8):
            dv = d_ref[h]

            def step(b, a):
                tot = jnp.sum(jnp.where(bkt == b, dv, 0.0), keepdims=True)
                return a + jnp.where((rowi == b) & (lane == h), tot, 0.0)
            out = lax.fori_loop(0, N_BUCKETS, step, out)
        o_ref[...] = out

    return pl.pallas_call(
        body, name=name,
        in_specs=[pl.BlockSpec(memory_space=pltpu.VMEM), pl.BlockSpec(memory_space=pltpu.VMEM)],
        out_specs=pl.BlockSpec(memory_space=pltpu.VMEM),
        out_shape=jax.ShapeDtypeStruct((N_BUCKETS, 128), F32),
    )(dbias, bucket)


def _swap_halves(x):
    return pltpu.roll(x.astype(F32), HEAD_DIM, 1).astype(x.dtype)


def _kv_variants(t):
    lo = _lane_lo(t.shape)
    z = jnp.zeros_like(t)
    a0 = jnp.where(lo, t, z)
    b1 = jnp.where(lo, z, t)
    b0 = _swap_halves(a0)
    a1 = _swap_halves(b1)
    return (a0, a1), (b0, b1), (a0 + b0, a1 + b1)


def _swa_masks(i):
    tq = lax.broadcasted_iota(I32, (WINDOW, WINDOW), 0)
    jj = lax.broadcasted_iota(I32, (WINDOW, WINDOW), 1)
    return (jj > tq) & (i > 0), jj <= tq


def _swa_specs():
    q = pl.BlockSpec((WINDOW, BRANCH), lambda i: (i, CB_SQ))
    kc = pl.BlockSpec((WINDOW, 128), lambda i: (i, CB_SK))
    kp = pl.BlockSpec((WINDOW, 128), lambda i: (jnp.maximum(i - 1, 0), CB_SK))
    vc = pl.BlockSpec((WINDOW, 128), lambda i: (i, CB_SV))
    vp = pl.BlockSpec((WINDOW, 128), lambda i: (jnp.maximum(i - 1, 0), CB_SV))
    bias = pl.BlockSpec((8, WINDOW, 2 * WINDOW), lambda i: (0, 0, 0))
    vec = pl.BlockSpec((1, 128), lambda i: (0, 0))
    return q, kc, kp, vc, vp, bias, vec


def _swa_fwd(pm, bias, sink, name):
    T = pm.shape[0]
    nb = T // WINDOW

    def body(q_ref, kc_ref, kp_ref, vc_ref, vp_ref, b_ref, s_ref, o_ref, m_ref):
        i = pl.program_id(0)
        mprev, mcur = _swa_masks(i)
        kcA, kcB, _ = _kv_variants(kc_ref[...])
        kpA, kpB, _ = _kv_variants(kp_ref[...])
        _, _, vcD = _kv_variants(vc_ref[...])
        _, _, vpD = _kv_variants(vp_ref[...])
        lo = _lane_lo((WINDOW, 128))
        sink_v = s_ref[...]
        mout = jnp.zeros((WINDOW, 128), F32)
        for p in range(4):
            jv = p // 2
            sl = slice(128 * p, 128 * p + 128)
            qp = q_ref[:, sl] * ATT_SCALE
            outs = []
            for par in range(2):
                h = 2 * p + par
                kpx = (kpA, kpB)[par][jv]
                kcx = (kcA, kcB)[par][jv]
                sp = jnp.where(mprev, _dot_nt(qp, kpx) + b_ref[h, :, 0:WINDOW], NEG)
                sc = jnp.where(mcur, _dot_nt(qp, kcx) + b_ref[h, :, WINDOW:2 * WINDOW], NEG)
                sk_h = sink_v[:, h:h + 1]
                m = jnp.maximum(jnp.maximum(jnp.max(sp, axis=-1, keepdims=True),
                                            jnp.max(sc, axis=-1, keepdims=True)), sk_h)
                ep = jnp.exp(sp - m)
                ec = jnp.exp(sc - m)
                den = (jnp.sum(ep, axis=-1, keepdims=True) + jnp.sum(ec, axis=-1, keepdims=True)
                       + jnp.exp(sk_h - m))
                inv = 1.0 / den
                outs.append(_dot_nn((ep * inv).astype(BF16), vpD[jv])
                            + _dot_nn((ec * inv).astype(BF16), vcD[jv]))
                mout = mout + _put_col((WINDOW, 128), h, m + jnp.log(den))
            o_ref[:, sl] = jnp.where(lo, outs[0], outs[1]).astype(BF16)
        m_ref[...] = mout

    q, kc, kp, vc, vp, bs, vec = _swa_specs()
    return pl.pallas_call(
        body, name=name, grid=(nb,),
        in_specs=[q, kc, kp, vc, vp, bs, vec],
        out_specs=[pl.BlockSpec((WINDOW, BRANCH), lambda i: (i, 0)),
                   pl.BlockSpec((WINDOW, 128), lambda i: (i, 0))],
        out_shape=[jax.ShapeDtypeStruct((T, BRANCH), BF16), jax.ShapeDtypeStruct((T, 128), F32)],
        compiler_params=_cp(("parallel",)),
    )(pm, pm, pm, pm, pm, bias, sink)


def _swa_bwd(pm, bias, sink, do, mlse, name):
    T = pm.shape[0]
    nb = T // WINDOW

    def fold(zz):
        return zz + pltpu.roll(zz, HEAD_DIM, 1)

    def body(q_ref, kc_ref, kp_ref, vc_ref, vp_ref, b_ref, s_ref, do_ref, m_ref,
             dq_ref, dkc_ref, dkp_ref, dvc_ref, dvp_ref, db_ref, ds_ref):
        i = pl.program_id(0)

        @pl.when(i == 0)
        def _():
            db_ref[...] = jnp.zeros_like(db_ref)
            ds_ref[...] = jnp.zeros_like(ds_ref)

        mprev, mcur = _swa_masks(i)
        kcA, kcB, kcD = _kv_variants(kc_ref[...])
        kpA, kpB, kpD = _kv_variants(kp_ref[...])
        vcA, vcB, _ = _kv_variants(vc_ref[...])
        vpA, vpB, _ = _kv_variants(vp_ref[...])
        lo = _lane_lo((WINDOW, 128))
        sink_v = s_ref[...]
        mv = m_ref[...]
        zk = jnp.zeros((WINDOW, 128), F32)
        zkp, zkc, zvp, zvc = [zk, zk], [zk, zk], [zk, zk], [zk, zk]
        dsink = jnp.zeros((1, 128), F32)
        for p in range(4):
            jv = p // 2
            sl = slice(128 * p, 128 * p + 128)
            qraw = q_ref[:, sl]
            qp = qraw * ATT_SCALE
            dop = do_ref[:, sl]
            dqs, mkp, mkc, mvp, mvc = [], [], [], [], []
            for par in range(2):
                h = 2 * p + par
                kpx = (kpA, kpB)[par][jv]
                kcx = (kcA, kcB)[par][jv]
                vpx = (vpA, vpB)[par][jv]
                vcx = (vcA, vcB)[par][jv]
                sp = jnp.where(mprev, _dot_nt(qp, kpx) + b_ref[h, :, 0:WINDOW], NEG)
                sc = jnp.where(mcur, _dot_nt(qp, kcx) + b_ref[h, :, WINDOW:2 * WINDOW], NEG)
                m_h = mv[:, h:h + 1]
                pp = jnp.exp(sp - m_h)
                pc = jnp.exp(sc - m_h)
                psink = jnp.exp(sink_v[:, h:h + 1] - m_h)
                dpp = _dot_nt(dop, vpx)
                dpc = _dot_nt(dop, vcx)
                delta = jnp.sum(pp * dpp, axis=-1, keepdims=True) + jnp.sum(pc * dpc, axis=-1, keepdims=True)
                dsp = pp * (dpp - delta)
                dsc = pc * (dpc - delta)
                db_ref[h, :, 0:WINDOW] += dsp
                db_ref[h, :, WINDOW:2 * WINDOW] += dsc
                dsink = dsink - _put_col((1, 128), h, jnp.sum(psink * delta, keepdims=True))
                dsp_b = dsp.astype(BF16)
                dsc_b = dsc.astype(BF16)
                dqs.append(_dot_nn(dsp_b, kpD[jv]) + _dot_nn(dsc_b, kcD[jv]))
                mkp.append(_dot_tn(dsp_b, qraw))
                mkc.append(_dot_tn(dsc_b, qraw))
                mvp.append(_dot_tn(pp.astype(BF16), dop))
                mvc.append(_dot_tn(pc.astype(BF16), dop))
            dq_ref[:, sl] = (jnp.where(lo, dqs[0], dqs[1]) * ATT_SCALE).astype(BF16)
            zkp[jv] = zkp[jv] + jnp.where(lo, mkp[0], mkp[1])
            zkc[jv] = zkc[jv] + jnp.where(lo, mkc[0], mkc[1])
            zvp[jv] = zvp[jv] + jnp.where(lo, mvp[0], mvp[1])
            zvc[jv] = zvc[jv] + jnp.where(lo, mvc[0], mvc[1])
        dkc_ref[...] = jnp.where(lo, fold(zkc[0]), fold(zkc[1])) * ATT_SCALE
        dkp_ref[...] = jnp.where(lo, fold(zkp[0]), fold(zkp[1])) * ATT_SCALE
        dvc_ref[...] = jnp.where(lo, fold(zvc[0]), fold(zvc[1]))
        dvp_ref[...] = jnp.where(lo, fold(zvp[0]), fold(zvp[1]))
        ds_ref[...] += dsink

    q, kc, kp, vc, vp, bs, vec = _swa_specs()
    own = pl.BlockSpec((WINDOW, BRANCH), lambda i: (i, 0))
    sm = pl.BlockSpec((WINDOW, 128), lambda i: (i, 0))
    f128 = jax.ShapeDtypeStruct((T, 128), F32)
    return pl.pallas_call(
        body, name=name, grid=(nb,),
        in_specs=[q, kc, kp, vc, vp, bs, vec, own, sm],
        out_specs=[own, sm, sm, sm, sm, bs, vec],
        out_shape=[jax.ShapeDtypeStruct((T, BRANCH), BF16), f128, f128, f128, f128,
                   jax.ShapeDtypeStruct((8, WINDOW, 2 * WINDOW), F32), jax.ShapeDtypeStruct((1, 128), F32)],
        compiler_params=_cp(("arbitrary",)),
    )(pm, pm, pm, pm, pm, bias, sink, do, mlse)


def _stacked_head(s, r):
    return 4 * (s // 2) + 2 * r + (s % 2)


def _swa2_bias(rel_bias, bucket, name):
    def body(rb_ref, bk_ref, o_ref):
        bkt = bk_ref[...]
        tq = lax.broadcasted_iota(I32, bkt.shape, 0)
        jj = lax.broadcasted_iota(I32, bkt.shape, 1)
        window = ((jj < WINDOW) & (jj > tq)) | ((jj >= WINDOW) & (jj - WINDOW <= tq))
        for s in range(4):
            for r in range(2):
                h = _stacked_head(s, r)

                def step(b, a, h=h):
                    return a + jnp.where(bkt == b, rb_ref[b, h], 0.0)
                val = lax.fori_loop(0, N_BUCKETS, step, jnp.zeros(bkt.shape, F32))
                o_ref[s, WINDOW * r:WINDOW * (r + 1), :] = jnp.where(window, val, NEG)

    return pl.pallas_call(
        body, name=name,
        in_specs=[pl.BlockSpec(memory_space=pltpu.SMEM), pl.BlockSpec(memory_space=pltpu.VMEM)],
        out_specs=pl.BlockSpec(memory_space=pltpu.VMEM),
        out_shape=jax.ShapeDtypeStruct((4, 2 * WINDOW, 2 * WINDOW), F32),
    )(rel_bias, bucket)


def _swa2_dbias_reduce(dbias, bucket, name):
    def body(d_ref, bk_ref, o_ref):
        bkt = bk_ref[...]
        rowi = lax.broadcasted_iota(I32, (N_BUCKETS, 128), 0)
        lane = lax.broadcasted_iota(I32, (N_BUCKETS, 128), 1)
        out = jnp.zeros((N_BUCKETS, 128), F32)
        for s in range(4):
            for r in range(2):
                h = _stacked_head(s, r)
                dv = d_ref[s, WINDOW * r:WINDOW * (r + 1), :]

                def step(b, a, dv=dv, h=h):
                    tot = jnp.sum(jnp.where(bkt == b, dv, 0.0), keepdims=True)
                    return a + jnp.where((rowi == b) & (lane == h), tot, 0.0)
                out = lax.fori_loop(0, N_BUCKETS, step, out)
        o_ref[...] = out

    return pl.pallas_call(
        body, name=name,
        in_specs=[pl.BlockSpec(memory_space=pltpu.VMEM), pl.BlockSpec(memory_space=pltpu.VMEM)],
        out_specs=pl.BlockSpec(memory_space=pltpu.VMEM),
        out_shape=jax.ShapeDtypeStruct((N_BUCKETS, 128), F32),
    )(dbias, bucket)


def _swa2_mask(i):
    tq = jnp.bitwise_and(lax.broadcasted_iota(I32, (2 * WINDOW, 2 * WINDOW), 0), WINDOW - 1)
    jj = lax.broadcasted_iota(I32, (2 * WINDOW, 2 * WINDOW), 1)
    return ((jj < WINDOW) & (jj > tq) & (i > 0)) | ((jj >= WINDOW) & (jj - WINDOW <= tq))


def _swa2_cols(vec, s):
    rows = lax.broadcasted_iota(I32, (2 * WINDOW, 1), 0)
    return jnp.where(rows < WINDOW, vec[:, _stacked_head(s, 0):_stacked_head(s, 0) + 1],
                     vec[:, _stacked_head(s, 1):_stacked_head(s, 1) + 1])


def _swa2_stack(ref, g):
    return jnp.concatenate([ref[:, 256 * g:256 * g + 128], ref[:, 256 * g + 128:256 * g + 256]], axis=0)


def _swa2_specs():
    q, kc, kp, vc, vp, _, vec = _swa_specs()
    bias = pl.BlockSpec((4, 2 * WINDOW, 2 * WINDOW), lambda i: (0, 0, 0))
    return q, kc, kp, vc, vp, bias, vec


def _swa2_fwd(pm, bias, sink, name):
    T = pm.shape[0]
    nb = T // WINDOW

    def body(q_ref, kc_ref, kp_ref, vc_ref, vp_ref, b_ref, s_ref, o_ref, m_ref):
        i = pl.program_id(0)
        mask = _swa2_mask(i)
        kcA, kcB, _ = _kv_variants(kc_ref[...])
        kpA, kpB, _ = _kv_variants(kp_ref[...])
        _, _, vcD = _kv_variants(vc_ref[...])
        _, _, vpD = _kv_variants(vp_ref[...])
        lo = _lane_lo((WINDOW, 128))
        sink_v = s_ref[...]
        mout = jnp.zeros((WINDOW, 128), F32)
        for g in range(2):
            qg = _swa2_stack(q_ref, g) * ATT_SCALE
            vband = jnp.concatenate([vpD[g], vcD[g]], axis=0)
            outs = []
            for par in range(2):
                s = 2 * g + par
                kband = jnp.concatenate([(kpA, kpB)[par][g], (kcA, kcB)[par][g]], axis=0)
                sc = jnp.where(mask, _dot_nt(qg, kband) + b_ref[s], NEG)
                sk = _swa2_cols(sink_v, s)
                m = jnp.maximum(jnp.max(sc, axis=-1, keepdims=True), sk)
                e = jnp.exp(sc - m)
                den = jnp.sum(e, axis=-1, keepdims=True) + jnp.exp(sk - m)
                outs.append(_dot_nn((e * (1.0 / den)).astype(BF16), vband))
                lse = m + jnp.log(den)
                mout = mout + _put_col((WINDOW, 128), _stacked_head(s, 0), lse[:WINDOW])
                mout = mout + _put_col((WINDOW, 128), _stacked_head(s, 1), lse[WINDOW:])
            for r in range(2):
                sl = slice(256 * g + 128 * r, 256 * g + 128 * r + 128)
                o_ref[:, sl] = jnp.where(lo, outs[0][WINDOW * r:WINDOW * (r + 1)],
                                         outs[1][WINDOW * r:WINDOW * (r + 1)]).astype(BF16)
        m_ref[...] = mout

    q, kc, kp, vc, vp, bs, vec = _swa2_specs()
    return pl.pallas_call(
        body, name=name, grid=(nb,),
        in_specs=[q, kc, kp, vc, vp, bs, vec],
        out_specs=[pl.BlockSpec((WINDOW, BRANCH), lambda i: (i, 0)),
                   pl.BlockSpec((WINDOW, 128), lambda i: (i, 0))],
        out_shape=[jax.ShapeDtypeStruct((T, BRANCH), BF16), jax.ShapeDtypeStruct((T, 128), F32)],
        compiler_params=_cp(("parallel",)),
    )(pm, pm, pm, pm, pm, bias, sink)


SWA_ROWS = 32


def _swa3_fwd(pm, bias, sink, name):
    T = pm.shape[0]
    nb = T // WINDOW
    R = SWA_ROWS

    def body(q_ref, kc_ref, kp_ref, vc_ref, vp_ref, b_ref, s_ref, o_ref, m_ref, s_scr, p_scr):
        first = pl.program_id(0) == 0
        kill = (lax.broadcasted_iota(I32, (R, 2 * WINDOW), 1) < WINDOW) & first
        kcA, kcB, _ = _kv_variants(kc_ref[...])
        kpA, kpB, _ = _kv_variants(kp_ref[...])
        _, _, vcD = _kv_variants(vc_ref[...])
        _, _, vpD = _kv_variants(vp_ref[...])
        lo = _lane_lo((WINDOW, 128))
        sink_v = s_ref[...]
        m_ref[...] = jnp.zeros_like(m_ref)
        for g in range(2):
            qg = _swa2_stack(q_ref, g) * ATT_SCALE
            vband = jnp.concatenate([vpD[g], vcD[g]], axis=0)
            outs = []
            for par in range(2):
                s = 2 * g + par
                kband = jnp.concatenate([(kpA, kpB)[par][g], (kcA, kcB)[par][g]], axis=0)
                s_scr[s] = _dot_nt(qg, kband)
                for c in range(2 * WINDOW // R):
                    rows = slice(c * R, (c + 1) * R)
                    h = _stacked_head(s, c * R // WINDOW)
                    loc = slice(c * R % WINDOW, c * R % WINDOW + R)
                    sc = jnp.where(kill, NEG, s_scr[s, rows, :] + b_ref[s, rows, :])
                    skv = sink_v[:, h:h + 1]
                    m = jnp.maximum(jnp.max(sc, axis=-1, keepdims=True), skv)
                    e = jnp.exp(sc - m)
                    den = jnp.sum(e, axis=-1, keepdims=True) + jnp.exp(skv - m)
                    p_scr[s, rows, :] = (e * (1.0 / den)).astype(BF16)
                    m_ref[loc, h:h + 1] = m + jnp.log(den)
                outs.append(_dot_nn(p_scr[s], vband))
            for r in range(2):
                sl = slice(256 * g + 128 * r, 256 * g + 128 * r + 128)
                o_ref[:, sl] = jnp.where(lo, outs[0][WINDOW * r:WINDOW * (r + 1)],
                                         outs[1][WINDOW * r:WINDOW * (r + 1)]).astype(BF16)

    q, kc, kp, vc, vp, bs, vec = _swa2_specs()
    tile = (4, 2 * WINDOW, 2 * WINDOW)
    return pl.pallas_call(
        body, name=name, grid=(nb,),
        in_specs=[q, kc, kp, vc, vp, bs, vec],
        out_specs=[pl.BlockSpec((WINDOW, BRANCH), lambda i: (i, 0)),
                   pl.BlockSpec((WINDOW, 128), lambda i: (i, 0))],
        out_shape=[jax.ShapeDtypeStruct((T, BRANCH), BF16), jax.ShapeDtypeStruct((T, 128), F32)],
        scratch_shapes=[pltpu.VMEM(tile, F32), pltpu.VMEM(tile, BF16)],
        compiler_params=_cp(("parallel",)),
    )(pm, pm, pm, pm, pm, bias, sink)


def _swa3_bwd(pm, bias, sink, do, mlse, name):
    T = pm.shape[0]
    nb = T // WINDOW
    R = SWA_ROWS

    def fold(zz):
        return zz + pltpu.roll(zz, HEAD_DIM, 1)

    def body(q_ref, kc_ref, kp_ref, vc_ref, vp_ref, b_ref, s_ref, do_ref, m_ref,
             dq_ref, dkc_ref, dkp_ref, dvc_ref, dvp_ref, db_ref, ds_ref, s_scr, dp_scr, p_scr, ds_scr):
        first = pl.program_id(0) == 0

        @pl.when(first)
        def _():
            db_ref[...] = jnp.zeros_like(db_ref)
            ds_ref[...] = jnp.zeros_like(ds_ref)

        kill = (lax.broadcasted_iota(I32, (R, 2 * WINDOW), 1) < WINDOW) & first
        kcA, kcB, kcD = _kv_variants(kc_ref[...])
        kpA, kpB, kpD = _kv_variants(kp_ref[...])
        vcA, vcB, _ = _kv_variants(vc_ref[...])
        vpA, vpB, _ = _kv_variants(vp_ref[...])
        lo = _lane_lo((WINDOW, 128))
        lo2 = _lane_lo((2 * WINDOW, 128))
        sink_v = s_ref[...]
        dsink = [jnp.zeros((1, 1), F32) for _ in range(8)]
        zks, zvs = [], []
        for g in range(2):
            qraw = _swa2_stack(q_ref, g)
            qg = qraw * ATT_SCALE
            dog = _swa2_stack(do_ref, g)
            kband_d = jnp.concatenate([kpD[g], kcD[g]], axis=0)
            dqs, mks, mvs = [], [], []
            for par in range(2):
                s = 2 * g + par
                kband = jnp.concatenate([(kpA, kpB)[par][g], (kcA, kcB)[par][g]], axis=0)
                vband = jnp.concatenate([(vpA, vpB)[par][g], (vcA, vcB)[par][g]], axis=0)
                s_scr[s] = _dot_nt(qg, kband)
                dp_scr[s] = _dot_nt(dog, vband)
                for c in range(2 * WINDOW // R):
                    rows = slice(c * R, (c + 1) * R)
                    h = _stacked_head(s, c * R // WINDOW)
                    loc = slice(c * R % WINDOW, c * R % WINDOW + R)
                    sc = jnp.where(kill, NEG, s_scr[s, rows, :] + b_ref[s, rows, :])
                    m_c = m_ref[loc, h:h + 1]
                    pr = jnp.exp(sc - m_c)
                    dp = dp_scr[s, rows, :]
                    delta = jnp.sum(pr * dp, axis=-1, keepdims=True)
                    dsc = pr * (dp - delta)
                    db_ref[s, rows, :] += dsc
                    ds_scr[s, rows, :] = dsc.astype(BF16)
                    p_scr[s, rows, :] = pr.astype(BF16)
                    dsink[h] = dsink[h] - jnp.sum(jnp.exp(sink_v[:, h:h + 1] - m_c) * delta, keepdims=True)
                dqs.append(_dot_nn(ds_scr[s], kband_d))
                mks.append(_dot_tn(ds_scr[s], qraw))
                mvs.append(_dot_tn(p_scr[s], dog))
            for r in range(2):
                sl = slice(256 * g + 128 * r, 256 * g + 128 * r + 128)
                dq_ref[:, sl] = (jnp.where(lo, dqs[0][WINDOW * r:WINDOW * (r + 1)],
                                           dqs[1][WINDOW * r:WINDOW * (r + 1)]) * ATT_SCALE).astype(BF16)
            zks.append(fold(jnp.where(lo2, mks[0], mks[1])))
            zvs.append(fold(jnp.where(lo2, mvs[0], mvs[1])))
        dk = jnp.where(lo2, zks[0], zks[1]) * ATT_SCALE
        dv = jnp.where(lo2, zvs[0], zvs[1])
        dkp_ref[...] = dk[:WINDOW]
        dkc_ref[...] = dk[WINDOW:]
        dvp_ref[...] = dv[:WINDOW]
        dvc_ref[...] = dv[WINDOW:]
        tot = jnp.zeros((1, 128), F32)
        for h in range(8):
            tot = tot + _put_col((1, 128), h, dsink[h])
        ds_ref[...] += tot

    q, kc, kp, vc, vp, bs, vec = _swa2_specs()
    own = pl.BlockSpec((WINDOW, BRANCH), lambda i: (i, 0))
    sm = pl.BlockSpec((WINDOW, 128), lambda i: (i, 0))
    f128 = jax.ShapeDtypeStruct((T, 128), F32)
    tile = (4, 2 * WINDOW, 2 * WINDOW)
    return pl.pallas_call(
        body, name=name, grid=(nb,),
        in_specs=[q, kc, kp, vc, vp, bs, vec, own, sm],
        out_specs=[own, sm, sm, sm, sm, bs, vec],
        out_shape=[jax.ShapeDtypeStruct((T, BRANCH), BF16), f128, f128, f128, f128,
                   jax.ShapeDtypeStruct(tile, F32), jax.ShapeDtypeStruct((1, 128), F32)],
        scratch_shapes=[pltpu.VMEM(tile, F32), pltpu.VMEM(tile, F32), pltpu.VMEM(tile, BF16), pltpu.VMEM(tile, BF16)],
        compiler_params=_cp(("arbitrary",)),
    )(pm, pm, pm, pm, pm, bias, sink, do, mlse)


def _swa2_bwd(pm, bias, sink, do, mlse, name):
    T = pm.shape[0]
    nb = T // WINDOW

    def fold(zz):
        return zz + pltpu.roll(zz, HEAD_DIM, 1)

    def body(q_ref, kc_ref, kp_ref, vc_ref, vp_ref, b_ref, s_ref, do_ref, m_ref,
             dq_ref, dkc_ref, dkp_ref, dvc_ref, dvp_ref, db_ref, ds_ref):
        i = pl.program_id(0)

        @pl.when(i == 0)
        def _():
            db_ref[...] = jnp.zeros_like(db_ref)
            ds_ref[...] = jnp.zeros_like(ds_ref)

        mask = _swa2_mask(i)
        kcA, kcB, kcD = _kv_variants(kc_ref[...])
        kpA, kpB, kpD = _kv_variants(kp_ref[...])
        vcA, vcB, _ = _kv_variants(vc_ref[...])
        vpA, vpB, _ = _kv_variants(vp_ref[...])
        lo = _lane_lo((WINDOW, 128))
        lo2 = _lane_lo((2 * WINDOW, 128))
        sink_v = s_ref[...]
        mv = m_ref[...]
        dsink = jnp.zeros((1, 128), F32)
        zks, zvs = [], []
        for g in range(2):
            qraw = _swa2_stack(q_ref, g)
            qg = qraw * ATT_SCALE
            dog = _swa2_stack(do_ref, g)
            kband_d = jnp.concatenate([kpD[g], kcD[g]], axis=0)
            dqs, mks, mvs = [], [], []
            for par in range(2):
                s = 2 * g + par
                kband = jnp.concatenate([(kpA, kpB)[par][g], (kcA, kcB)[par][g]], axis=0)
                vband = jnp.concatenate([(vpA, vpB)[par][g], (vcA, vcB)[par][g]], axis=0)
                sc = jnp.where(mask, _dot_nt(qg, kband) + b_ref[s], NEG)
                h0, h1 = _stacked_head(s, 0), _stacked_head(s, 1)
                m_c = jnp.concatenate([mv[:, h0:h0 + 1], mv[:, h1:h1 + 1]], axis=0)
                pr = jnp.exp(sc - m_c)
                psink = jnp.exp(_swa2_cols(sink_v, s) - m_c)
                dp = _dot_nt(dog, vband)
                delta = jnp.sum(pr * dp, axis=-1, keepdims=True)
                dsc = pr * (dp - delta)
                db_ref[s] += dsc
                sd = psink * delta
                dsink = dsink - _put_col((1, 128), _stacked_head(s, 0), jnp.sum(sd[:WINDOW], keepdims=True))
                dsink = dsink - _put_col((1, 128), _stacked_head(s, 1), jnp.sum(sd[WINDOW:], keepdims=True))
                dsb = dsc.astype(BF16)
                dqs.append(_dot_nn(dsb, kband_d))
                mks.append(_dot_tn(dsb, qraw))
                mvs.append(_dot_tn(pr.astype(BF16), dog))
            for r in range(2):
                sl = slice(256 * g + 128 * r, 256 * g + 128 * r + 128)
                dq_ref[:, sl] = (jnp.where(lo, dqs[0][WINDOW * r:WINDOW * (r + 1)],
                                           dqs[1][WINDOW * r:WINDOW * (r + 1)]) * ATT_SCALE).astype(BF16)
            zks.append(fold(jnp.where(lo2, mks[0], mks[1])))
            zvs.append(fold(jnp.where(lo2, mvs[0], mvs[1])))
        dk = jnp.where(lo2, zks[0], zks[1]) * ATT_SCALE
        dv = jnp.where(lo2, zvs[0], zvs[1])
        dkp_ref[...] = dk[:WINDOW]
        dkc_ref[...] = dk[WINDOW:]
        dvp_ref[...] = dv[:WINDOW]
        dvc_ref[...] = dv[WINDOW:]
        ds_ref[...] += dsink

    q, kc, kp, vc, vp, bs, vec = _swa2_specs()
    own = pl.BlockSpec((WINDOW, BRANCH), lambda i: (i, 0))
    sm = pl.BlockSpec((WINDOW, 128), lambda i: (i, 0))
    f128 = jax.ShapeDtypeStruct((T, 128), F32)
    return pl.pallas_call(
        body, name=name, grid=(nb,),
        in_specs=[q, kc, kp, vc, vp, bs, vec, own, sm],
        out_specs=[own, sm, sm, sm, sm, bs, vec],
        out_shape=[jax.ShapeDtypeStruct((T, BRANCH), BF16), f128, f128, f128, f128,
                   jax.ShapeDtypeStruct((4, 2 * WINDOW, 2 * WINDOW), F32), jax.ShapeDtypeStruct((1, 128), F32)],
        compiler_params=_cp(("arbitrary",)),
    )(pm, pm, pm, pm, pm, bias, sink, do, mlse)


def _merge_fwd(pm, us, name):
    T = pm.shape[0]
    bt = _pick(T, (512, 256))

    def body(g0, g1, g2, u0, u1, u2, o_ref):
        acc = jax.nn.sigmoid(g0[...].astype(F32)) * u0[...].astype(F32)
        acc = acc + jax.nn.sigmoid(g1[...].astype(F32)) * u1[...].astype(F32)
        acc = acc + jax.nn.sigmoid(g2[...].astype(F32)) * u2[...].astype(F32)
        o_ref[...] = acc.astype(BF16)

    own = pl.BlockSpec((bt, D_MODEL), lambda i: (i, 0))
    gs = [pl.BlockSpec((bt, D_MODEL), lambda i, cb=cb: (i, cb)) for cb in CB_GATE]
    return pl.pallas_call(
        body, name=name, grid=(T // bt,), in_specs=gs + [own, own, own], out_specs=own,
        out_shape=jax.ShapeDtypeStruct((T, D_MODEL), BF16),
        compiler_params=_cp(("parallel",)),
    )(pm, pm, pm, *us)


def _merge_bwd(pm, us, dm, name):
    T = pm.shape[0]
    bt = _pick(T, (256,))

    def body(g0, g1, g2, u0, u1, u2, dm_ref, du0, du1, du2, dg_ref):
        dmv = dm_ref[...].astype(F32)
        for b, (g, u, du) in enumerate(((g0, u0, du0), (g1, u1, du1), (g2, u2, du2))):
            s = jax.nn.sigmoid(g[...].astype(F32))
            du[...] = (dmv * s).astype(BF16)
            dg_ref[:, D_MODEL * b:D_MODEL * (b + 1)] = (dmv * u[...].astype(F32) * s * (1.0 - s)).astype(BF16)

    own = pl.BlockSpec((bt, D_MODEL), lambda i: (i, 0))
    gs = [pl.BlockSpec((bt, D_MODEL), lambda i, cb=cb: (i, cb)) for cb in CB_GATE]
    act = jax.ShapeDtypeStruct((T, D_MODEL), BF16)
    return pl.pallas_call(
        body, name=name, grid=(T // bt,), in_specs=gs + [own, own, own, own],
        out_specs=[own, own, own, pl.BlockSpec((bt, 3 * D_MODEL), lambda i: (i, 0))],
        out_shape=[act, act, act, jax.ShapeDtypeStruct((T, 3 * D_MODEL), BF16)],
        compiler_params=_cp(("parallel",)),
    )(pm, pm, pm, *us, dm)


def _swiglu_fwd(ab, name):
    T = ab.shape[0]
    bt = _pick(T, (512, 256))

    def body(a_ref, b_ref, o_ref):
        a = a_ref[...].astype(F32)
        o_ref[...] = (a * jax.nn.sigmoid(a) * b_ref[...].astype(F32)).astype(BF16)

    return pl.pallas_call(
        body, name=name, grid=(T // bt,),
        in_specs=[pl.BlockSpec((bt, D_FF), lambda i: (i, 0)), pl.BlockSpec((bt, D_FF), lambda i: (i, 1))],
        out_specs=pl.BlockSpec((bt, D_FF), lambda i: (i, 0)),
        out_shape=jax.ShapeDtypeStruct((T, D_FF), BF16),
        compiler_params=_cp(("parallel",)),
    )(ab, ab)


def _swiglu_bwd(ab, dh, name):
    T = ab.shape[0]
    bt = _pick(T, (256,))

    def body(a_ref, b_ref, d_ref, o_ref):
        a = a_ref[...].astype(F32)
        b = b_ref[...].astype(F32)
        d = d_ref[...].astype(F32)
        s = jax.nn.sigmoid(a)
        o_ref[:, 0:D_FF] = (d * b * (s + a * s * (1.0 - s))).astype(BF16)
        o_ref[:, D_FF:2 * D_FF] = (d * a * s).astype(BF16)

    return pl.pallas_call(
        body, name=name, grid=(T // bt,),
        in_specs=[pl.BlockSpec((bt, D_FF), lambda i: (i, 0)), pl.BlockSpec((bt, D_FF), lambda i: (i, 1)),
                  pl.BlockSpec((bt, D_FF), lambda i: (i, 0))],
        out_specs=pl.BlockSpec((bt, 2 * D_FF), lambda i: (i, 0)),
        out_shape=jax.ShapeDtypeStruct((T, 2 * D_FF), BF16),
        compiler_params=_cp(("parallel",)),
    )(ab, ab, dh)


def _xattn_probs(q_ref, kv_ref, h):
    sl = slice(X_HEAD_DIM * h, X_HEAD_DIM * (h + 1))
    qh = q_ref[:, sl]
    kh = kv_ref[:, sl]
    vh = kv_ref[:, D_MODEL + X_HEAD_DIM * h:D_MODEL + X_HEAD_DIM * (h + 1)]
    s = _dot_nt(qh, kh) * X_SCALE
    e = jnp.exp(s - jnp.max(s, axis=-1, keepdims=True))
    return qh, kh, vh, e * (1.0 / jnp.sum(e, axis=-1, keepdims=True))


def _xattn_fwd(q, kv, name):
    T = q.shape[0]
    bq = _pick(T, (512, 256))

    def body(q_ref, kv_ref, o_ref):
        for h in range(X_HEADS):
            _, _, vh, p = _xattn_probs(q_ref, kv_ref, h)
            o_ref[:, X_HEAD_DIM * h:X_HEAD_DIM * (h + 1)] = _dot_nn(p.astype(BF16), vh).astype(BF16)

    own = pl.BlockSpec((bq, D_MODEL), lambda i: (i, 0))
    return pl.pallas_call(
        body, name=name, grid=(T // bq,),
        in_specs=[own, pl.BlockSpec((MEM_LEN, 2 * D_MODEL), lambda i: (0, 0))], out_specs=own,
        out_shape=jax.ShapeDtypeStruct((T, D_MODEL), BF16),
        compiler_params=_cp(("parallel",)),
    )(q, kv)


def _xattn_bwd(q, kv, do, name):
    T = q.shape[0]
    bq = _pick(T, (512, 256))

    def body(q_ref, kv_ref, do_ref, dq_ref, dkv_ref):
        @pl.when(pl.program_id(0) == 0)
        def _():
            dkv_ref[...] = jnp.zeros_like(dkv_ref)

        for h in range(X_HEADS):
            sl = slice(X_HEAD_DIM * h, X_HEAD_DIM * (h + 1))
            qh, kh, vh, p = _xattn_probs(q_ref, kv_ref, h)
            doh = do_ref[:, sl]
            dp = _dot_nt(doh, vh)
            ds = (p * (dp - jnp.sum(p * dp, axis=-1, keepdims=True)) * X_SCALE).astype(BF16)
            dq_ref[:, sl] = _dot_nn(ds, kh).astype(BF16)
            dkv_ref[:, sl] += _dot_tn(ds, qh)
            dkv_ref[:, D_MODEL + X_HEAD_DIM * h:D_MODEL + X_HEAD_DIM * (h + 1)] += _dot_tn(p.astype(BF16), doh)

    own = pl.BlockSpec((bq, D_MODEL), lambda i: (i, 0))
    kvs = pl.BlockSpec((MEM_LEN, 2 * D_MODEL), lambda i: (0, 0))
    return pl.pallas_call(
        body, name=name, grid=(T // bq,), in_specs=[own, kvs, own], out_specs=[own, kvs],
        out_shape=[jax.ShapeDtypeStruct((T, D_MODEL), BF16), jax.ShapeDtypeStruct((MEM_LEN, 2 * D_MODEL), F32)],
        compiler_params=_cp(("arbitrary",)),
    )(q, kv, do)


def _adamw(w, g, m, v, name):
    R, C = w.shape
    cpad = -(-C // 128) * 128
    bt = R
    for cand in (1024, 512, 256, 128, 64, 32, 16, 8):
        if R % cand == 0 and cand * cpad * 4 <= (1 << 20):
            bt = cand
            break

    def body(w_ref, g_ref, m_ref, v_ref, d_ref, nm_ref, nv_ref):
        gv = g_ref[...]
        mn = ADAM_B1 * m_ref[...] + (1.0 - ADAM_B1) * gv
        vn = ADAM_B2 * v_ref[...] + (1.0 - ADAM_B2) * (gv * gv)
        m_hat = mn / (1.0 - ADAM_B1 ** ADAM_STEP)
        v_hat = vn / (1.0 - ADAM_B2 ** ADAM_STEP)
        d_ref[...] = -ADAM_LR * (m_hat / (jnp.sqrt(v_hat) + ADAM_EPS) + ADAM_WD * w_ref[...])
        nm_ref[...] = mn
        nv_ref[...] = vn

    blk = pl.BlockSpec((bt, C), lambda i: (i, 0))
    out = jax.ShapeDtypeStruct((R, C), F32)
    return pl.pallas_call(
        body, name=name, grid=(R // bt,), in_specs=[blk] * 4, out_specs=[blk] * 3,
        out_shape=[out, out, out], compiler_params=_cp(("parallel",)),
    )(w, g, m, v)


ANY = pl.BlockSpec(memory_space=pl.ANY)


def _place():
    x, y, c = lax.axis_index("x"), lax.axis_index("y"), lax.axis_index("c")
    chips = [(1 - x, y), (x, 1 - y), (1 - x, 1 - y)]
    return x, y, c, chips


def _ag_packs(pack):
    R, Wd = pack.shape
    hrows = R // 2

    def body(p_ref, o_ref, send_sems, recv_sems, local_sem):
        x, y, c, chips = _place()
        me = 2 * x + y
        mine = pl.ds(c * hrows, hrows)
        theirs = pl.ds((1 - c) * hrows, hrows)
        local = pltpu.make_async_copy(p_ref, o_ref.at[me], local_sem)
        local.start()

        def copy(k, slab, rows, to, src=None):
            dst = o_ref.at[slab, rows]
            return pltpu.make_async_remote_copy(
                src_ref=dst if src is None else src, dst_ref=dst,
                send_sem=send_sems.at[k], recv_sem=recv_sems.at[k], device_id=to, device_id_type=MESH)

        first = [copy(k, me, mine, (px, py, c), src=p_ref.at[mine]) for k, (px, py) in enumerate(chips)]
        for cp in first:
            cp.start()
        passed = [copy(3 + k, 2 * px + py, mine, (x, y, 1 - c)) for k, (px, py) in enumerate(chips)]
        for k, (px, py) in enumerate(chips):
            copy(k, 2 * px + py, mine, (x, y, c)).wait_recv()
            passed[k].start()
        for k, (px, py) in enumerate(chips):
            copy(3 + k, 2 * px + py, theirs, (x, y, c)).wait_recv()
        for cp in first + passed:
            cp.wait_send()
        local.wait()

    return pl.pallas_call(
        body, name="ag_weights", in_specs=[ANY], out_specs=ANY,
        out_shape=jax.ShapeDtypeStruct((4, R, Wd), pack.dtype),
        scratch_shapes=[pltpu.SemaphoreType.DMA((6,)), pltpu.SemaphoreType.DMA((6,)), pltpu.SemaphoreType.DMA],
    )(pack)


def _rs_sibling(g4):
    _, R, Wd = g4.shape
    hrows = R // 2

    def body(g_ref, o_ref, send_sem, recv_sem):
        x, y, c, _ = _place()
        cp = pltpu.make_async_remote_copy(
            src_ref=g_ref.at[:, pl.ds((1 - c) * hrows, hrows)], dst_ref=o_ref,
            send_sem=send_sem, recv_sem=recv_sem, device_id=(x, y, 1 - c), device_id_type=MESH)
        cp.start()
        cp.wait()

    return pl.pallas_call(
        body, name="rs_sibling", in_specs=[ANY], out_specs=ANY,
        out_shape=jax.ShapeDtypeStruct((4, hrows, Wd), g4.dtype),
        scratch_shapes=[pltpu.SemaphoreType.DMA, pltpu.SemaphoreType.DMA],
    )(g4)


def _rs_add_pair(g4, sib, cidx, tag=""):
    _, R, Wd = g4.shape
    hrows = R // 2
    bt = _pick(hrows, ROW_BLOCKS)
    nb = hrows // bt

    def body(c_ref, a_ref, b_ref, o_ref):
        o_ref[...] = (a_ref[...].astype(F32) + b_ref[...].astype(F32)).astype(o_ref.dtype)

    grid_spec = pltpu.PrefetchScalarGridSpec(
        num_scalar_prefetch=1, grid=(4, nb),
        in_specs=[pl.BlockSpec((1, bt, Wd), lambda j, i, c: (j, c[0] * nb + i, 0)),
                  pl.BlockSpec((1, bt, Wd), lambda j, i, c: (j, i, 0))],
        out_specs=pl.BlockSpec((1, bt, Wd), lambda j, i, c: (j, i, 0)))
    return pl.pallas_call(
        body, name="rs_add_pair" + tag, grid_spec=grid_spec,
        out_shape=jax.ShapeDtypeStruct((4, hrows, Wd), g4.dtype),
        compiler_params=_cp(("parallel", "parallel")),
    )(cidx, g4, sib)


def _rs_chips(r4):
    _, hrows, Wd = r4.shape

    def body(r_ref, o_ref, send_sems, recv_sems, local_sem):
        x, y, c, chips = _place()
        me = 2 * x + y
        local = pltpu.make_async_copy(r_ref.at[me], o_ref.at[me], local_sem)
        local.start()
        sends = []
        for k, (px, py) in enumerate(chips):
            sends.append(pltpu.make_async_remote_copy(
                src_ref=r_ref.at[2 * px + py], dst_ref=o_ref.at[me],
                send_sem=send_sems.at[k], recv_sem=recv_sems.at[k], device_id=(px, py, c), device_id_type=MESH))
        for cp in sends:
            cp.start()
        for k, (px, py) in enumerate(chips):
            pltpu.make_async_remote_copy(
                src_ref=r_ref.at[me], dst_ref=o_ref.at[2 * px + py],
                send_sem=send_sems.at[k], recv_sem=recv_sems.at[k], device_id=(x, y, c),
                device_id_type=MESH).wait_recv()
        for cp in sends:
            cp.wait_send()
        local.wait()

    return pl.pallas_call(
        body, name="rs_chips", in_specs=[ANY], out_specs=ANY,
        out_shape=jax.ShapeDtypeStruct((4, hrows, Wd), r4.dtype),
        scratch_shapes=[pltpu.SemaphoreType.DMA((3,)), pltpu.SemaphoreType.DMA((3,)), pltpu.SemaphoreType.DMA],
    )(r4)


def _rs_add_chips(q4):
    _, hrows, Wd = q4.shape
    bt = _pick(hrows, (240, 120, 16))

    def body(q_ref, o_ref):
        o_ref[...] = ((q_ref[0].astype(F32) + q_ref[1].astype(F32)) + q_ref[2].astype(F32)) + q_ref[3].astype(F32)

    return pl.pallas_call(
        body, name="rs_add_chips", grid=(hrows // bt,),
        in_specs=[pl.BlockSpec((4, bt, Wd), lambda i: (0, i, 0))],
        out_specs=pl.BlockSpec((bt, Wd), lambda i: (i, 0)),
        out_shape=jax.ShapeDtypeStruct((hrows, Wd), F32),
        compiler_params=_cp(("parallel",)),
    )(q4)


def _rs_share(buf):
    R, Wd = buf.shape
    hrows = R // 2

    def body(b_ref, o_ref, send_sem, recv_sem):
        del b_ref
        x, y, c, _ = _place()
        mine = o_ref.at[pl.ds(c * hrows, hrows)]
        cp = pltpu.make_async_remote_copy(
            src_ref=mine, dst_ref=mine, send_sem=send_sem, recv_sem=recv_sem,
            device_id=(x, y, 1 - c), device_id_type=MESH)
        cp.start()
        theirs = o_ref.at[pl.ds((1 - c) * hrows, hrows)]
        pltpu.make_async_remote_copy(
            src_ref=theirs, dst_ref=theirs, send_sem=send_sem, recv_sem=recv_sem,
            device_id=(x, y, c), device_id_type=MESH).wait_recv()
        cp.wait_send()

    return pl.pallas_call(
        body, name="rs_share", in_specs=[ANY], out_specs=ANY, input_output_aliases={0: 0},
        out_shape=jax.ShapeDtypeStruct((R, Wd), buf.dtype),
        scratch_shapes=[pltpu.SemaphoreType.DMA, pltpu.SemaphoreType.DMA],
    )(buf)


def _allreduce_small(v, name="allreduce_small"):
    R, Wd = v.shape

    def body(v_ref, o_ref, buf, send_sems, recv_sems):
        x, y, c, _ = _place()
        me = 4 * x + 2 * y + c
        buf[me] = v_ref[...]
        sends = []
        for k in range(1, 8):
            peer = ((x + (k >> 2)) % 2, (y + ((k >> 1) & 1)) % 2, (c + (k & 1)) % 2)
            sends.append(pltpu.make_async_remote_copy(
                src_ref=v_ref, dst_ref=buf.at[me], send_sem=send_sems.at[k - 1], recv_sem=recv_sems.at[k - 1],
                device_id=peer, device_id_type=MESH))
        for cp in sends:
            cp.start()
        for k in range(1, 8):
            px, py, pc = (x + (k >> 2)) % 2, (y + ((k >> 1) & 1)) % 2, (c + (k & 1)) % 2
            pltpu.make_async_remote_copy(
                src_ref=v_ref, dst_ref=buf.at[4 * px + 2 * py + pc], send_sem=send_sems.at[k - 1],
                recv_sem=recv_sems.at[k - 1], device_id=(x, y, c), device_id_type=MESH).wait_recv()
        acc = buf[0]
        for d in range(1, 8):
            acc = acc + buf[d]
        o_ref[...] = acc
        for cp in sends:
            cp.wait_send()

    vm = pl.BlockSpec(memory_space=pltpu.VMEM)
    return pl.pallas_call(
        body, name=name, in_specs=[vm], out_specs=vm,
        out_shape=jax.ShapeDtypeStruct((R, Wd), F32),
        scratch_shapes=[pltpu.VMEM((8, R, Wd), F32), pltpu.SemaphoreType.DMA((7,)), pltpu.SemaphoreType.DMA((7,))],
    )(v)


def _neighbours():
    x, y, c = lax.axis_index("x"), lax.axis_index("y"), lax.axis_index("c")
    idx = (2 * x + y, 2 * (1 - x) + y, 2 * x + (1 - y), 2 * (1 - x) + (1 - y))
    return idx, (x, y, c), (1 - x, y, c), (x, 1 - y, c), (x, y, 1 - c)


def _place_own(pack, me_idx):
    R, Wd = pack.shape
    bt = _pick(R, (512, 256))

    def body(i_ref, p_ref, o_ref):
        o_ref[0] = p_ref[...]

    grid_spec = pltpu.PrefetchScalarGridSpec(
        num_scalar_prefetch=1, grid=(R // bt,),
        in_specs=[pl.BlockSpec((bt, Wd), lambda i, idx: (i, 0))],
        out_specs=pl.BlockSpec((1, bt, Wd), lambda i, idx: (idx[0], i, 0)))
    return pl.pallas_call(
        body, name="place_own", grid_spec=grid_spec,
        out_shape=jax.ShapeDtypeStruct((4, R, Wd), pack.dtype),
        compiler_params=_cp(("parallel",)),
    )(me_idx, pack)


def _ag_ring(buf):
    _, R, Wd = buf.shape
    hrows = R // 2
    qrows = hrows // 2

    def body(b_ref, o_ref, send_sems, recv_sems):
        del b_ref
        (me, ix, iy, idg), here, xn, yn, sib = _neighbours()
        c = here[2]
        base = c * hrows
        half = pl.ds(base, hrows)
        q0 = pl.ds(base, qrows)
        q1 = pl.ds(base + qrows, qrows)
        obase = (1 - c) * hrows

        def copy(k, slab, rows, to):
            dst = o_ref.at[slab, rows]
            return pltpu.make_async_remote_copy(
                src_ref=dst, dst_ref=dst,
                send_sem=send_sems.at[k], recv_sem=recv_sems.at[k], device_id=to, device_id_type=MESH)

        sends = [copy(0, me, half, xn), copy(1, me, half, yn)]
        for cp in sends:
            cp.start()
        landed = [(0, ix, half), (1, iy, half), (2, idg, q0), (3, idg, q1)]
        onward = {0: copy(2, ix, q0, yn), 1: copy(3, iy, q1, xn)}
        for k, slab, rows in landed:
            copy(k, slab, rows, here).wait_recv()
            if k in onward:
                onward[k].start()
                sends.append(onward[k])
            cp = copy(4 + k, slab, rows, sib)
            cp.start()
            sends.append(cp)
        theirs = [(4, ix, pl.ds(obase, hrows)), (5, iy, pl.ds(obase, hrows)),
                  (6, idg, pl.ds(obase, qrows)), (7, idg, pl.ds(obase + qrows, qrows))]
        for k, slab, rows in theirs:
            copy(k, slab, rows, here).wait_recv()
        for cp in sends:
            cp.wait_send()

    return pl.pallas_call(
        body, name="ag_weights", in_specs=[ANY], out_specs=ANY, input_output_aliases={0: 0},
        out_shape=jax.ShapeDtypeStruct((4, R, Wd), buf.dtype),
        scratch_shapes=[pltpu.SemaphoreType.DMA((8,)), pltpu.SemaphoreType.DMA((8,))],
    )(buf)


def _rs_diag(r4):
    _, hrows, Wd = r4.shape
    qrows = hrows // 2

    def body(r_ref, o_ref, send_sems, recv_sems):
        (me, ix, iy, idg), here, xn, yn, sib = _neighbours()
        pieces = [(0, pl.ds(0, qrows), xn), (1, pl.ds(qrows, qrows), yn)]
        sends = [pltpu.make_async_remote_copy(
            src_ref=r_ref.at[idg, rows], dst_ref=o_ref.at[k], send_sem=send_sems.at[k],
            recv_sem=recv_sems.at[k], device_id=to, device_id_type=MESH) for k, rows, to in pieces]
        for cp in sends:
            cp.start()
        for k, rows, to in pieces:
            pltpu.make_async_remote_copy(
                src_ref=r_ref.at[idg, rows], dst_ref=o_ref.at[k], send_sem=send_sems.at[k],
                recv_sem=recv_sems.at[k], device_id=here, device_id_type=MESH).wait_recv()
        for cp in sends:
            cp.wait_send()

    return pl.pallas_call(
        body, name="rs_diag", in_specs=[ANY], out_specs=ANY,
        out_shape=jax.ShapeDtypeStruct((2, qrows, Wd), r4.dtype),
        scratch_shapes=[pltpu.SemaphoreType.DMA((2,)), pltpu.SemaphoreType.DMA((2,))],
    )(r4)


def _rs_merge(r4, dg, nbr_idx, tag=""):
    _, hrows, Wd = r4.shape
    bt = _pick(hrows // 2, ROW_BLOCKS)
    nb = hrows // bt
    nq = nb // 2

    def body(i_ref, r_ref, d_ref, o_ref):
        w = pl.program_id(0)
        i = pl.program_id(1)
        merged = jnp.where(w == 0, i >= nq, i < nq)
        add = jnp.where(merged, d_ref[...].astype(F32), 0.0)
        o_ref[...] = (r_ref[...].astype(F32) + add).astype(o_ref.dtype)

    grid_spec = pltpu.PrefetchScalarGridSpec(
        num_scalar_prefetch=1, grid=(2, nb),
        in_specs=[pl.BlockSpec((1, bt, Wd), lambda w, i, idx: (idx[w], i, 0)),
                  pl.BlockSpec((1, bt, Wd), lambda w, i, idx: (1 - w, jnp.clip(i - (1 - w) * nq, 0, nq - 1), 0))],
        out_specs=pl.BlockSpec((1, bt, Wd), lambda w, i, idx: (w, i, 0)))
    return pl.pallas_call(
        body, name="rs_merge" + tag, grid_spec=grid_spec,
        out_shape=jax.ShapeDtypeStruct((2, hrows, Wd), r4.dtype),
        compiler_params=_cp(("parallel", "parallel")),
    )(nbr_idx, r4, dg)


def _rs_direct(m2):
    _, hrows, Wd = m2.shape

    def body(m_ref, o_ref, send_sems, recv_sems):
        _, here, xn, yn, sib = _neighbours()
        sends = [pltpu.make_async_remote_copy(
            src_ref=m_ref.at[k], dst_ref=o_ref.at[k], send_sem=send_sems.at[k], recv_sem=recv_sems.at[k],
            device_id=to, device_id_type=MESH) for k, to in ((0, xn), (1, yn))]
        for cp in sends:
            cp.start()
        for k in range(2):
            pltpu.make_async_remote_copy(
                src_ref=m_ref.at[k], dst_ref=o_ref.at[k], send_sem=send_sems.at[k], recv_sem=recv_sems.at[k],
                device_id=here, device_id_type=MESH).wait_recv()
        for cp in sends:
            cp.wait_send()

    return pl.pallas_call(
        body, name="rs_direct", in_specs=[ANY], out_specs=ANY,
        out_shape=jax.ShapeDtypeStruct((2, hrows, Wd), m2.dtype),
        scratch_shapes=[pltpu.SemaphoreType.DMA((2,)), pltpu.SemaphoreType.DMA((2,))],
    )(m2)


def _rs_final(r4, got, me_c, tag=""):
    _, hrows, Wd = r4.shape
    bt = _pick(hrows, ROW_BLOCKS)
    nb = hrows // bt

    def body(i_ref, r_ref, g_ref, o_ref):
        o_ref[...] = (r_ref[0].astype(F32) + g_ref[0].astype(F32)) + g_ref[1].astype(F32)

    grid_spec = pltpu.PrefetchScalarGridSpec(
        num_scalar_prefetch=1, grid=(nb,),
        in_specs=[pl.BlockSpec((1, bt, Wd), lambda i, idx: (idx[0], i, 0)),
                  pl.BlockSpec((2, bt, Wd), lambda i, idx: (0, i, 0))],
        out_specs=pl.BlockSpec((bt, Wd), lambda i, idx: (idx[1] * nb + i, 0)))
    return pl.pallas_call(
        body, name="rs_final" + tag, grid_spec=grid_spec,
        out_shape=jax.ShapeDtypeStruct((2 * hrows, Wd), F32),
        compiler_params=_cp(("parallel",)),
    )(me_c, r4, got)


SHARDED = (
    ("w_in", (2, 1024, 1730), 2),
    ("w_branch", (2, 3, 512, 256), 3),
    ("w_mix_out", (2, 256, 1024), 1),
    ("w_xq", (2, 256, 1024), 1),
    ("w_xkv", (2, 1024, 512), 2),
    ("w_xo", (2, 256, 1024), 1),
    ("w_ffn_gate", (2, 1024, 704), 2),
    ("w_ffn_up", (2, 1024, 704), 2),
    ("w_ffn_down", (2, 704, 1024), 1),
    ("conv_w", (2, 3, 128), 2),
)
PACK_W = 1024
PACK_ELEMS = sum(int(np.prod(s)) for _, s, _ in SHARDED)
PACK_ROWS = -(-PACK_ELEMS // (PACK_W * 1024)) * 1024


def _pack(parts, dtype):
    flat = jnp.concatenate([p.astype(dtype).reshape(-1) for p in parts]
                           + [jnp.zeros((PACK_ROWS * PACK_W - PACK_ELEMS,), dtype)])
    return flat.reshape(PACK_ROWS, PACK_W)


def _unpack(pack):
    flat = pack.reshape(-1)
    out, off = {}, 0
    for name, shape, _ in SHARDED:
        n = int(np.prod(shape))
        out[name] = flat[off:off + n].reshape(shape)
        off += n
    return out


SMALL = (
    ("mix_norm_g", (2, 1024)), ("xattn_norm_g", (2, 1024)), ("mem_norm_g", (2, 1024)),
    ("ffn_norm_g", (2, 1024)), ("final_norm_g", (1024,)),
    ("forget_bias", (2, 8)), ("sink", (2, 8)), ("rel_bias", (32, 8)),
)
SMALL_AND_CONV = SMALL + (("conv_w", (2, 3, 512)),)


def _small_rows(spec):
    rows = sum(int(np.prod(s)) // 128 if s[-1] % 128 == 0 else s[0] for _, s in spec)
    return -(-rows // 8) * 8


def _pack_small(vals, spec=SMALL):
    rows = []
    for name, shape in spec:
        v = vals[name].astype(F32)
        if shape[-1] % 128 == 0:
            rows.append(v.reshape(-1, 128))
        else:
            rows.append(jnp.pad(v, ((0, 0), (0, 120))))
    rows = jnp.concatenate(rows, axis=0)
    return jnp.pad(rows, ((0, _small_rows(spec) - rows.shape[0]), (0, 0)))


def _unpack_small(pack, spec=SMALL):
    out, off = {}, 0
    for name, shape in spec:
        if shape[-1] % 128 == 0:
            n = int(np.prod(shape)) // 128
            out[name] = pack[off:off + n].reshape(shape)
        else:
            n = shape[0]
            out[name] = pack[off:off + n, 0:8]
        off += n
    return out


W_IN_PERM = ((3848, 6920), (0, 3072), (3080, 3848), (3072, 3080))


def _perm_w_in(w):
    parts = [w[:, a:b] for a, b in W_IN_PERM]
    return jnp.concatenate(parts + [jnp.zeros((w.shape[0], PROJ_PAD - IN_COLS), w.dtype)], axis=1)


def _unperm_w_in(p):
    return jnp.concatenate([p[:, 3072:6144], p[:, 6912:6920], p[:, 6144:6912], p[:, 0:3072]], axis=1)


def _pad_row8(v):
    return jnp.pad(v.astype(F32).reshape(1, 8), ((0, 0), (0, 120)))


def _local_step(x, mem, tgt, W, rel_bias):
    T = x.shape[0]
    bucket = jnp.asarray(_bucket_table())
    bias = _swa2_bias(rel_bias, bucket, "swa_bias")
    saved = []
    for l in range(DEPTH):
        n = "l%d_" % l
        s = {"x0": x}
        wcat = W["w_in_p"][l]
        h = _rms_fwd(x, W["mix_norm_g"][l:l + 1], n + "mix_norm")
        pm = _mm(h, wcat[:, :PROJ_MAIN], "nn", BF16, n + "proj", bn=768)
        fg = _mm(h, wcat[:, PROJ_MAIN:], "nn", F32, n + "proj_fg")
        fb = _pad_row8(W["forget_bias"][l])
        c_col = _fox_gate_fwd(fg, fb, n + "fox_gate")
        c_row = c_col[:, 0:8].T
        cw = jnp.pad(W["conv_w"][l], ((0, 5), (0, 0)))
        y_conv = _conv_fwd(pm, cw, n + "conv")
        y_fox, lse = _fox2_fwd(pm, c_row, n + "fox")
        sink = _pad_row8(W["sink"][l])
        y_swa, mlse = _swa2_fwd(pm, bias, sink, n + "swa")
        ys = (y_conv, y_fox, y_swa)
        us = tuple(_mm(ys[b], W["w_branch"][l][b], "nn", BF16, n + "branch%d" % b) for b in range(3))
        merged = _merge_fwd(pm, us, n + "merge")
        x1 = _mm(merged, W["w_mix_out"][l], "nn", F32, n + "mix_out", res=x)
        xn1 = _rms_fwd(x1, W["xattn_norm_g"][l:l + 1], n + "xattn_norm")
        memn = _rms_fwd(mem, W["mem_norm_g"][l:l + 1], n + "mem_norm")
        qx = _mm(xn1, W["w_xq"][l], "nn", BF16, n + "xq")
        kv = _mm(memn, W["w_xkv"][l], "nn", BF16, n + "xkv")
        ox = _xattn_fwd(qx, kv, n + "xattn")
        x2 = _mm(ox, W["w_xo"][l], "nn", F32, n + "xo", res=x1)
        xn2 = _rms_fwd(x2, W["ffn_norm_g"][l:l + 1], n + "ffn_norm")
        ab = _mm(xn2, W["w_gu"][l], "nn", BF16, n + "ffn_in", bn=512)
        hm = _swiglu_fwd(ab, n + "swiglu")
        x3 = _mm(hm, W["w_ffn_down"][l], "nn", F32, n + "ffn_out", res=x2, bk=1408)
        s.update(h=h, pm=pm, fg=fg, fb=fb, c_col=c_col, c_row=c_row, cw=cw, ys=ys, lse=lse, sink=sink,
                 mlse=mlse, us=us, merged=merged, x1=x1, xn1=xn1, memn=memn, qx=qx, kv=kv, ox=ox,
                 x2=x2, xn2=xn2, ab=ab, hm=hm)
        saved.append(s)
        x = x3

    loss_row, dx, dg_final = _final_loss(x, W["final_norm_g"].reshape(1, D_MODEL), tgt, "final_loss")
    G = {name: [None] * DEPTH for name in
         ("mix_norm_g", "w_in_p", "forget_bias", "conv_w", "sink", "w_branch", "w_mix_out", "xattn_norm_g",
          "mem_norm_g", "w_xq", "w_xkv", "w_xo", "ffn_norm_g", "w_gu", "w_ffn_down")}
    dbias_tot = None
    for l in reversed(range(DEPTH)):
        n = "l%d_" % l
        s = saved[l]
        dhm = _mm(dx, W["w_ffn_down"][l], "nt", BF16, n + "d_hm", bn=1408)
        G["w_ffn_down"][l] = _mm(s["hm"], dx, "tn", BF16, n + "dw_down", bm=1408, bk=1024)
        dab = _swiglu_bwd(s["ab"], dhm, n + "d_swiglu")
        dxn2 = _mm(dab, W["w_gu"][l], "nt", BF16, n + "d_xn2", bk=1408)
        G["w_gu"][l] = _mm(s["xn2"], dab, "tn", BF16, n + "dw_gu", bn=512, bk=2048)
        dx, G["ffn_norm_g"][l] = _rms_bwd(s["x2"], W["ffn_norm_g"][l:l + 1], dxn2, dx, n + "d_ffn_norm")
        dox = _mm(dx, W["w_xo"][l], "nt", BF16, n + "d_ox")
        G["w_xo"][l] = _mm(s["ox"], dx, "tn", BF16, n + "dw_xo", bk=1024)
        dqx, dkv = _xattn_bwd(s["qx"], s["kv"], dox, n + "d_xattn")
        dxn1 = _mm(dqx, W["w_xq"][l], "nt", BF16, n + "d_xn1")
        G["w_xq"][l] = _mm(s["xn1"], dqx, "tn", BF16, n + "dw_xq", bk=2048)
        dmemn = _mm(dkv, W["w_xkv"][l], "nt", BF16, n + "d_memn")
        G["w_xkv"][l] = _mm(s["memn"], dkv, "tn", BF16, n + "dw_xkv")
        _, G["mem_norm_g"][l] = _rms_bwd(mem, W["mem_norm_g"][l:l + 1], dmemn, None, n + "d_mem_norm")
        dx, G["xattn_norm_g"][l] = _rms_bwd(s["x1"], W["xattn_norm_g"][l:l + 1], dxn1, dx, n + "d_xattn_norm")
        dmerged = _mm(dx, W["w_mix_out"][l], "nt", BF16, n + "d_merged")
        G["w_mix_out"][l] = _mm(s["merged"], dx, "tn", BF16, n + "dw_mix_out", bk=1024)
        du0, du1, du2, dgates = _merge_bwd(s["pm"], s["us"], dmerged, n + "d_merge")
        dus = (du0, du1, du2)
        dys = [_mm(dus[b], W["w_branch"][l][b], "nt", BF16, n + "d_y%d" % b) for b in range(3)]
        G["w_branch"][l] = jnp.stack(
            [_mm(s["ys"][b], dus[b], "tn", BF16, n + "dw_branch%d" % b, bk=2048) for b in range(3)])
        dcb, dcc, dcu, dcw = _conv_bwd(s["pm"], s["cw"], dys[0], n + "d_conv")
        G["conv_w"][l] = dcw[0:3]
        delta = _fox_delta(s["ys"][1], dys[1], n + "fox_delta")
        dfq, delta = _fox2_bwd_dq(s["pm"], dys[1], s["c_row"], s["lse"], delta, n + "d_fox_q")
        dfk, dfv, dc = _fox2_bwd_dkv(s["pm"], dys[1], s["c_col"], s["lse"][:, 0:8].T, delta[:, 0:8].T,
                                     n + "d_fox_kv")
        dfg, dfb = _fox_gate_bwd(dc, s["fg"], s["fb"], n + "d_fox_gate")
        G["forget_bias"][l] = dfb[0, 0:8]
        dsq, dkc, dkp, dvc, dvp, dbias, dsink = _swa2_bwd(s["pm"], bias, s["sink"], dys[2], s["mlse"],
                                                        n + "d_swa")
        G["sink"][l] = dsink[0, 0:8]
        dbias_tot = dbias if dbias_tot is None else dbias_tot + dbias
        zpad = jnp.zeros((WINDOW, 128), F32)
        dsk = dkc + jnp.concatenate([dkp[WINDOW:], zpad], axis=0)
        dsv = dvc + jnp.concatenate([dvp[WINDOW:], zpad], axis=0)
        dproj = jnp.concatenate([dgates, dcb, dcc, dcu, dfq, dfk, dfv, dsq, dsk.astype(BF16),
                                 dsv.astype(BF16), dfg.astype(BF16)], axis=1)
        dh = _mm(dproj, W["w_in_p"][l], "nt", BF16, n + "d_h", bk=1408)
        G["w_in_p"][l] = _mm(s["h"], dproj, "tn", BF16, n + "dw_in", bn=640, bk=2048)
        dx, G["mix_norm_g"][l] = _rms_bwd(s["x0"], W["mix_norm_g"][l:l + 1], dh, dx, n + "d_mix_norm")
    drb = _swa2_dbias_reduce(dbias_tot, bucket, "swa_dbias")
    G["rel_bias"] = drb[:, 0:8]
    G["final_norm_g"] = dg_final.reshape(D_MODEL)
    return loss_row, dx, G


BIG = (
    ("w_in", (2048, 1730)), ("w_branch", (3072, 256)), ("w_mix_out", (512, 1024)), ("w_xq", (512, 1024)),
    ("w_xkv", (2048, 512)), ("w_xo", (512, 1024)), ("w_ffn_gate", (2048, 704)), ("w_ffn_up", (2048, 704)),
    ("w_ffn_down", (1408, 1024)),
)
ROW_BLOCKS = (512, 256, 352, 128, 16)


def _cast_place(w, me_idx, name):
    R, Wd = w.shape
    bt = _pick(R, ROW_BLOCKS)

    def body(i_ref, w_ref, o_ref):
        o_ref[0] = w_ref[...].astype(BF16)

    grid_spec = pltpu.PrefetchScalarGridSpec(
        num_scalar_prefetch=1, grid=(R // bt,),
        in_specs=[pl.BlockSpec((bt, Wd), lambda i, idx: (i, 0))],
        out_specs=pl.BlockSpec((1, bt, Wd), lambda i, idx: (idx[0], i, 0)))
    return pl.pallas_call(
        body, name=name, grid_spec=grid_spec, out_shape=jax.ShapeDtypeStruct((4, R, Wd), BF16),
        compiler_params=_cp(("parallel",)),
    )(me_idx, w)


def _remote(src, dst, sems, k, to):
    send_sems, recv_sems = sems
    return pltpu.make_async_remote_copy(src_ref=src, dst_ref=dst, send_sem=send_sems.at[k], recv_sem=recv_sems.at[k],
                                        device_id=to, device_id_type=MESH)


def _ag_ring_multi(bufs):
    n = len(bufs)

    def body(*refs):
        o = refs[n:2 * n]
        sems = refs[2 * n:]
        (me, ix, iy, idg), here, xn, yn, sib = _neighbours()
        c = here[2]

        def piece(t, k, other):
            h = bufs[t].shape[1] // 2
            q = h // 2
            base = ((1 - c) if other else c) * h
            return [(ix, pl.ds(base, h)), (iy, pl.ds(base, h)), (idg, pl.ds(base, q)), (idg, pl.ds(base + q, q))][k]

        def copy(t, k, slab, rows, to):
            ref = o[t].at[slab, rows]
            return _remote(ref, ref, sems, 8 * t + k, to)

        sends = []

        def go(cp):
            cp.start()
            sends.append(cp)

        for t in range(n):
            h = bufs[t].shape[1] // 2
            go(copy(t, 0, me, pl.ds(c * h, h), xn))
            go(copy(t, 1, me, pl.ds(c * h, h), yn))
        for k in range(4):
            for t in range(n):
                slab, rows = piece(t, k, False)
                copy(t, k, slab, rows, here).wait_recv()
                if k == 0:
                    go(copy(t, 2, ix, piece(t, 2, False)[1], yn))
                if k == 1:
                    go(copy(t, 3, iy, piece(t, 3, False)[1], xn))
                go(copy(t, 4 + k, slab, rows, sib))
        for k in range(4):
            for t in range(n):
                slab, rows = piece(t, k, True)
                copy(t, 4 + k, slab, rows, here).wait_recv()
        for cp in sends:
            cp.wait_send()

    return pl.pallas_call(
        body, name="ag_weights", in_specs=[ANY] * n, out_specs=[ANY] * n,
        input_output_aliases={t: t for t in range(n)},
        out_shape=[jax.ShapeDtypeStruct(b.shape, b.dtype) for b in bufs],
        scratch_shapes=[pltpu.SemaphoreType.DMA((8 * n,)), pltpu.SemaphoreType.DMA((8 * n,))],
    )(*bufs)


def _exchange_multi(srcs, out_shapes, plan, name, aliased=False):
    n = len(srcs)

    def body(*refs):
        ins, outs, sems = refs[:n], refs[n:2 * n], refs[2 * n:]
        places = _neighbours()
        here = places[1]
        per = [plan(t, ins[t], outs[t], places) for t in range(n)]
        width = max(len(p) for p in per)
        started = []
        for t in range(n):
            for k, (src, dst, to, land) in enumerate(per[t]):
                cp = _remote(src, dst, sems, width * t + k, to)
                cp.start()
                started.append(cp)
        for t in range(n):
            for k, (src, dst, to, land) in enumerate(per[t]):
                _remote(land, land, sems, width * t + k, here).wait_recv()
        for cp in started:
            cp.wait_send()

    nsem = 2 * n
    return pl.pallas_call(
        body, name=name, in_specs=[ANY] * n, out_specs=[ANY] * n,
        input_output_aliases={t: t for t in range(n)} if aliased else {},
        out_shape=[jax.ShapeDtypeStruct(s, d) for s, d in out_shapes],
        scratch_shapes=[pltpu.SemaphoreType.DMA((nsem,)), pltpu.SemaphoreType.DMA((nsem,))],
    )(*srcs)


def _rs_sibling_multi(gs):
    def plan(t, g, o, places):
        (_, here, _, _, sib) = places
        h = gs[t].shape[1] // 2
        return [(g.at[:, pl.ds((1 - here[2]) * h, h)], o, sib, o)]

    return _exchange_multi(gs, [((4, g.shape[1] // 2, g.shape[2]), g.dtype) for g in gs], plan, "rs_sibling")


def _rs_diag_multi(rs):
    def plan(t, r, o, places):
        ((_, _, _, idg), _, xn, yn, _) = places
        q = rs[t].shape[1] // 2
        return [(r.at[idg, pl.ds(0, q)], o.at[0], xn, o.at[0]), (r.at[idg, pl.ds(q, q)], o.at[1], yn, o.at[1])]

    return _exchange_multi(rs, [((2, r.shape[1] // 2, r.shape[2]), r.dtype) for r in rs], plan, "rs_diag")


def _rs_direct_multi(ms):
    def plan(t, m, o, places):
        (_, _, xn, yn, _) = places
        return [(m.at[0], o.at[0], xn, o.at[0]), (m.at[1], o.at[1], yn, o.at[1])]

    return _exchange_multi(ms, [(m.shape, m.dtype) for m in ms], plan, "rs_direct")


def _rs_share_multi(bufs):
    def plan(t, b, o, places):
        (_, here, _, _, sib) = places
        h = bufs[t].shape[0] // 2
        mine = o.at[pl.ds(here[2] * h, h)]
        return [(mine, mine, sib, o.at[pl.ds((1 - here[2]) * h, h)])]

    return _exchange_multi(bufs, [(b.shape, b.dtype) for b in bufs], plan, "rs_share", aliased=True)


def kernel(x, mem, mix_norm_g, w_in, forget_bias, conv_w, sink, w_branch, w_mix_out, rel_bias, xattn_norm_g, mem_norm_g, w_xq, w_xkv, w_xo, ffn_norm_g, w_ffn_gate, w_ffn_up, w_ffn_down, final_norm_g, loss_target, m_mix_norm_g, m_w_in, m_forget_bias, m_conv_w, m_sink, m_w_branch, m_w_mix_out, m_rel_bias, m_xattn_norm_g, m_mem_norm_g, m_w_xq, m_w_xkv, m_w_xo, m_ffn_norm_g, m_w_ffn_gate, m_w_ffn_up, m_w_ffn_down, m_final_norm_g, v_mix_norm_g, v_w_in, v_forget_bias, v_conv_w, v_sink, v_w_branch, v_w_mix_out, v_rel_bias, v_xattn_norm_g, v_mem_norm_g, v_w_xq, v_w_xkv, v_w_xo, v_ffn_norm_g, v_w_ffn_gate, v_w_ffn_up, v_w_ffn_down, v_final_norm_g):
    order = ("mix_norm_g", "w_in", "forget_bias", "conv_w", "sink", "w_branch", "w_mix_out", "rel_bias",
             "xattn_norm_g", "mem_norm_g", "w_xq", "w_xkv", "w_xo", "ffn_norm_g", "w_ffn_gate", "w_ffn_up",
             "w_ffn_down", "final_norm_g")
    w_sh = dict(zip(order, (mix_norm_g, w_in, forget_bias, conv_w, sink, w_branch, w_mix_out, rel_bias,
                            xattn_norm_g, mem_norm_g, w_xq, w_xkv, w_xo, ffn_norm_g, w_ffn_gate, w_ffn_up,
                            w_ffn_down, final_norm_g)))
    m_sh = dict(zip(order, (m_mix_norm_g, m_w_in, m_forget_bias, m_conv_w, m_sink, m_w_branch, m_w_mix_out,
                            m_rel_bias, m_xattn_norm_g, m_mem_norm_g, m_w_xq, m_w_xkv, m_w_xo, m_ffn_norm_g,
                            m_w_ffn_gate, m_w_ffn_up, m_w_ffn_down, m_final_norm_g)))
    v_sh = dict(zip(order, (v_mix_norm_g, v_w_in, v_forget_bias, v_conv_w, v_sink, v_w_branch, v_w_mix_out,
                            v_rel_bias, v_xattn_norm_g, v_mem_norm_g, v_w_xq, v_w_xkv, v_w_xo, v_ffn_norm_g,
                            v_w_ffn_gate, v_w_ffn_up, v_w_ffn_down, v_final_norm_g)))

    xi, yi, ci = lax.axis_index("x"), lax.axis_index("y"), lax.axis_index("c")
    as_idx = lambda *v: jnp.stack([jnp.asarray(t, I32) for t in v])
    me = 2 * xi + yi
    big = [name for name, _ in BIG]
    two_d = dict(BIG)
    gathered = dict(zip(big, _ag_ring_multi(
        [_cast_place(w_sh[name].reshape(two_d[name]), as_idx(me), "place_" + name) for name in big])))
    conv_part = lax.dynamic_update_slice_in_dim(jnp.zeros((DEPTH, 3, BRANCH), F32), 0.5 * conv_w, 128 * me, axis=2)
    conv_full = _allreduce_small(conv_part.reshape(-1, 128), "allgather_conv").reshape(DEPTH, 3, BRANCH)

    def lay(name, l):
        g = gathered[name]
        return g.reshape(4, DEPTH, g.shape[1] // DEPTH, g.shape[2])[:, l]

    def by_cols(name, l):
        g = lay(name, l)
        return jnp.moveaxis(g, 0, 1).reshape(g.shape[1], 4 * g.shape[2])

    def by_rows(name, l):
        g = lay(name, l)
        return g.reshape(4 * g.shape[1], g.shape[2])

    W = {k: w_sh[k] for k in ("mix_norm_g", "forget_bias", "sink", "xattn_norm_g", "mem_norm_g",
                              "ffn_norm_g", "final_norm_g")}
    W["conv_w"] = conv_full
    W["w_in_p"] = [_perm_w_in(by_cols("w_in", l)) for l in range(DEPTH)]
    W["w_gu"] = [jnp.concatenate([by_cols("w_ffn_gate", l), by_cols("w_ffn_up", l)], axis=1) for l in range(DEPTH)]
    W["w_xkv"] = [by_cols("w_xkv", l) for l in range(DEPTH)]
    W["w_branch"] = [jnp.transpose(lay("w_branch", l).reshape(4, 3, BRANCH, 256), (1, 2, 0, 3)).reshape(3, BRANCH, D_MODEL)
                     for l in range(DEPTH)]
    for k in ("w_mix_out", "w_xq", "w_xo", "w_ffn_down"):
        W[k] = [by_rows(k, l) for l in range(DEPTH)]
    loss_row, dx, G = _local_step(x[0], mem[0], loss_target[0], W, rel_bias)

    def to_cols(g):
        return jnp.moveaxis(g.reshape(g.shape[0], 4, g.shape[1] // 4), 1, 0)

    def to_rows(g):
        return g.reshape(4, g.shape[0] // 4, g.shape[1])

    per_layer = {
        "w_in": [to_cols(_unperm_w_in(G["w_in_p"][l])) for l in range(DEPTH)],
        "w_branch": [jnp.transpose(G["w_branch"][l].reshape(3, BRANCH, 4, 256), (2, 0, 1, 3)).reshape(4, 3 * BRANCH, 256)
                     for l in range(DEPTH)],
        "w_mix_out": [to_rows(g) for g in G["w_mix_out"]],
        "w_xq": [to_rows(g) for g in G["w_xq"]],
        "w_xkv": [to_cols(g) for g in G["w_xkv"]],
        "w_xo": [to_rows(g) for g in G["w_xo"]],
        "w_ffn_gate": [to_cols(G["w_gu"][l][:, :D_FF]) for l in range(DEPTH)],
        "w_ffn_up": [to_cols(G["w_gu"][l][:, D_FF:]) for l in range(DEPTH)],
        "w_ffn_down": [to_rows(g) for g in G["w_ffn_down"]],
    }
    g4 = [jnp.concatenate(per_layer[name], axis=1).astype(BF16) for name in big]
    sib = _rs_sibling_multi(g4)
    pair = [_rs_add_pair(g4[t], sib[t], as_idx(ci), "_" + big[t]) for t in range(len(big))]
    diag = _rs_diag_multi(pair)
    nbrs = as_idx(2 * (1 - xi) + yi, 2 * xi + (1 - yi))
    merged = [_rs_merge(pair[t], diag[t], nbrs, "_" + big[t]) for t in range(len(big))]
    got = _rs_direct_multi(merged)
    reduced = _rs_share_multi([_rs_final(pair[t], got[t], as_idx(me, ci), "_" + big[t]) for t in range(len(big))])

    small = _unpack_small(_allreduce_small(_pack_small({
        "mix_norm_g": jnp.concatenate(G["mix_norm_g"], axis=0),
        "xattn_norm_g": jnp.concatenate(G["xattn_norm_g"], axis=0),
        "mem_norm_g": jnp.concatenate(G["mem_norm_g"], axis=0),
        "ffn_norm_g": jnp.concatenate(G["ffn_norm_g"], axis=0),
        "final_norm_g": G["final_norm_g"],
        "forget_bias": jnp.stack(G["forget_bias"]),
        "sink": jnp.stack(G["sink"]),
        "rel_bias": G["rel_bias"],
        "conv_w": jnp.stack(G["conv_w"]),
    }, SMALL_AND_CONV)), SMALL_AND_CONV)
    grads = {name: reduced[t].reshape(w_sh[name].shape) for t, name in enumerate(big)}
    grads.update(small)
    grads["conv_w"] = lax.dynamic_slice_in_dim(small["conv_w"], 128 * me, 128, axis=2)

    sm_names = [name for name, _ in SMALL]
    sd, sm_, sv_ = _adamw(_pack_small({k: w_sh[k] for k in sm_names}), _pack_small({k: grads[k] for k in sm_names}),
                          _pack_small({k: m_sh[k] for k in sm_names}), _pack_small({k: v_sh[k] for k in sm_names}),
                          "adamw_small")
    delta, new_m, new_v = _unpack_small(sd), _unpack_small(sm_), _unpack_small(sv_)
    for name, shape in BIG + (("conv_w", (6, 128)),):
        full_shape = w_sh[name].shape
        d, nm, nv = _adamw(w_sh[name].reshape(shape), grads[name].reshape(shape), m_sh[name].reshape(shape),
                           v_sh[name].reshape(shape), "adamw_" + name)
        delta[name], new_m[name], new_v[name] = d.reshape(full_shape), nm.reshape(full_shape), nv.reshape(full_shape)

    loss = lax.psum(loss_row[0, 0], ("x", "y", "c"))
    return (loss, dx[None], *[grads[k] for k in order], *[delta[k] for k in order],
            *[new_m[k] for k in order], *[new_v[k] for k in order])
```

```python
import math

import numpy as np
import jax
import jax.numpy as jnp
from jax import lax
from jax.experimental import pallas as pl
from jax.experimental.pallas import tpu as pltpu

F32 = jnp.float32
BF16 = jnp.bfloat16
I32 = jnp.int32

D_MODEL = 1024
DEPTH = 2
HEAD_DIM = 64
BRANCH = 512
N_BUCKETS = 32
WINDOW = 128
MEM_LEN = 256
X_HEADS = 4
X_HEAD_DIM = 256
D_FF = 2816
IN_COLS = 6920
PROJ_MAIN = 6912
PROJ_PAD = 7040
RMS_EPS = 1e-6
NEG = -1e30
ATT_SCALE = 0.125
X_SCALE = 0.0625

ADAM_LR = 0.001
ADAM_B1 = 0.9
ADAM_B2 = 0.999
ADAM_EPS = 1e-08
ADAM_WD = 0.01
ADAM_STEP = 10

VMEM_LIMIT = 48 * 1024 * 1024
MESH = pl.DeviceIdType.MESH

CB_GATE = (0, 1, 2)
CB_B, CB_C, CB_U, CB_FQ, CB_FK, CB_FV, CB_SQ = 6, 7, 8, 9, 10, 11, 12
CB_SK, CB_SV = 52, 53


def _cp(sem):
    return pltpu.CompilerParams(dimension_semantics=sem, vmem_limit_bytes=VMEM_LIMIT)


def _pick(n, prefs):
    for p in prefs:
        if p <= n and n % p == 0:
            return p
    return n


def _dot(a, b, dims):
    return lax.dot_general(a, b, (dims, ((), ())), preferred_element_type=F32)


def _dot_nn(a, b):
    return _dot(a, b, ((1,), (0,)))


def _dot_nt(a, b):
    return _dot(a, b, ((1,), (1,)))


def _dot_tn(a, b):
    return _dot(a, b, ((0,), (0,)))


def _mm(a, b, mode, out_dtype, name, res=None, bm=1024, bn=1024, bk=1024):
    if mode == "nn":
        (M, K), (K2, N) = a.shape, b.shape
    elif mode == "nt":
        (M, K), (N, K2) = a.shape, b.shape
    else:
        (K, M), (K2, N) = a.shape, b.shape
    assert K == K2, (name, a.shape, b.shape)
    bm = _pick(M, (bm, 1024, 512, 256, 128))
    bn = _pick(N, (bn, 1024, 768, 640, 512, 384, 256, 128))
    bk = _pick(K, (bk, 1024, 768, 640, 512, 384, 256, 128))
    nk = K // bk
    if mode == "tn":
        a_spec = pl.BlockSpec((bk, bm), lambda i, j, k: (k, i))
    else:
        a_spec = pl.BlockSpec((bm, bk), lambda i, j, k: (i, k))
    if mode == "nt":
        b_spec = pl.BlockSpec((bn, bk), lambda i, j, k: (j, k))
    else:
        b_spec = pl.BlockSpec((bk, bn), lambda i, j, k: (k, j))
    dims = {"nn": ((1,), (0,)), "nt": ((1,), (1,)), "tn": ((0,), (0,))}[mode]
    o_spec = pl.BlockSpec((bm, bn), lambda i, j, k: (i, j))
    has_res = res is not None

    def body(*refs):
        if has_res:
            a_ref, b_ref, r_ref, o_ref = refs[:4]
            scr = refs[4:]
        else:
            a_ref, b_ref, o_ref = refs[:3]
            r_ref = None
            scr = refs[3:]
        p = _dot(a_ref[...].astype(BF16), b_ref[...].astype(BF16), dims)
        if nk == 1:
            if has_res:
                p = p + r_ref[...]
            o_ref[...] = p.astype(out_dtype)
        else:
            acc = scr[0]
            k = pl.program_id(2)

            @pl.when(k == 0)
            def _():
                acc[...] = p

            @pl.when(k > 0)
            def _():
                acc[...] += p

            @pl.when(k == nk - 1)
            def _():
                r = acc[...]
                if has_res:
                    r = r + r_ref[...]
                o_ref[...] = r.astype(out_dtype)

    ins = [a, b] + ([res] if has_res else [])
    in_specs = [a_spec, b_spec] + ([o_spec] if has_res else [])
    return pl.pallas_call(
        body, name=name, grid=(M // bm, N // bn, nk),
        in_specs=in_specs, out_specs=o_spec,
        out_shape=jax.ShapeDtypeStruct((M, N), out_dtype),
        scratch_shapes=[pltpu.VMEM((bm, bn), F32)] if nk > 1 else [],
        compiler_params=_cp(("parallel", "parallel", "arbitrary")),
    )(*ins)


def _rms_fwd(x, g, name):
    T, Dm = x.shape
    bt = _pick(T, (512, 256))

    def body(x_ref, g_ref, o_ref):
        xv = x_ref[...]
        r = lax.rsqrt(jnp.mean(xv * xv, axis=-1, keepdims=True) + RMS_EPS)
        o_ref[...] = ((xv * r) * g_ref[...]).astype(BF16)

    return pl.pallas_call(
        body, name=name, grid=(T // bt,),
        in_specs=[pl.BlockSpec((bt, Dm), lambda i: (i, 0)), pl.BlockSpec((1, Dm), lambda i: (0, 0))],
        out_specs=pl.BlockSpec((bt, Dm), lambda i: (i, 0)),
        out_shape=jax.ShapeDtypeStruct((T, Dm), BF16),
        compiler_params=_cp(("parallel",)),
    )(x, g)


def _rms_bwd(x, g, dh, dres, name):
    T, Dm = x.shape
    bt = _pick(T, (512, 256))
    want_dx = dres is not None

    def body(*refs):
        if want_dx:
            x_ref, g_ref, dh_ref, dr_ref, dx_ref, dg_ref = refs
        else:
            x_ref, g_ref, dh_ref, dg_ref = refs
        xv = x_ref[...]
        r = lax.rsqrt(jnp.mean(xv * xv, axis=-1, keepdims=True) + RMS_EPS)
        xh = xv * r
        dhv = dh_ref[...].astype(F32)

        @pl.when(pl.program_id(0) == 0)
        def _():
            dg_ref[...] = jnp.zeros_like(dg_ref)

        dg_ref[...] += jnp.sum(dhv * xh, axis=0, keepdims=True)
        if want_dx:
            dyg = dhv * g_ref[...]
            dx_ref[...] = dr_ref[...] + r * (dyg - xh * jnp.mean(dyg * xh, axis=-1, keepdims=True))

    row = pl.BlockSpec((bt, Dm), lambda i: (i, 0))
    vec = pl.BlockSpec((1, Dm), lambda i: (0, 0))
    if want_dx:
        return pl.pallas_call(
            body, name=name, grid=(T // bt,),
            in_specs=[row, vec, row, row], out_specs=[row, vec],
            out_shape=[jax.ShapeDtypeStruct((T, Dm), F32), jax.ShapeDtypeStruct((1, Dm), F32)],
            compiler_params=_cp(("arbitrary",)),
        )(x, g, dh, dres)
    return None, pl.pallas_call(
        body, name=name, grid=(T // bt,),
        in_specs=[row, vec, row], out_specs=vec,
        out_shape=jax.ShapeDtypeStruct((1, Dm), F32),
        compiler_params=_cp(("arbitrary",)),
    )(x, g, dh)


def _final_loss(x, g, tgt, name):
    T, Dm = x.shape
    bt = _pick(T, (512, 256))

    def body(x_ref, g_ref, t_ref, loss_ref, dx_ref, dg_ref):
        xv = x_ref[...]
        r = lax.rsqrt(jnp.mean(xv * xv, axis=-1, keepdims=True) + RMS_EPS)
        xh = xv * r
        gv = g_ref[...]
        err = xh * gv - t_ref[...]

        @pl.when(pl.program_id(0) == 0)
        def _():
            dg_ref[...] = jnp.zeros_like(dg_ref)
            loss_ref[...] = jnp.zeros_like(loss_ref)

        loss_ref[...] += jnp.sum(err * err) * (0.5 / Dm)
        dy = err * (1.0 / Dm)
        dg_ref[...] += jnp.sum(dy * xh, axis=0, keepdims=True)
        dyg = dy * gv
        dx_ref[...] = r * (dyg - xh * jnp.mean(dyg * xh, axis=-1, keepdims=True))

    row = pl.BlockSpec((bt, Dm), lambda i: (i, 0))
    vec = pl.BlockSpec((1, Dm), lambda i: (0, 0))
    return pl.pallas_call(
        body, name=name, grid=(T // bt,),
        in_specs=[row, vec, row],
        out_specs=[pl.BlockSpec((1, 128), lambda i: (0, 0)), row, vec],
        out_shape=[jax.ShapeDtypeStruct((1, 128), F32), jax.ShapeDtypeStruct((T, Dm), F32),
                   jax.ShapeDtypeStruct((1, Dm), F32)],
        compiler_params=_cp(("arbitrary",)),
    )(x, g, tgt)


HALO = 16


def _shift_down(z, zprev, s):
    rolled = pltpu.roll(z, s, 0)
    hp = pltpu.roll(zprev, s, 0)
    row = lax.broadcasted_iota(I32, hp.shape, 0)
    top = jnp.where(row < s, hp, rolled[:HALO])
    return jnp.concatenate([top, rolled[HALO:]], axis=0)


def _shift_up(z, znext, s):
    n = z.shape[0]
    rolled = pltpu.roll(z, n - s, 0)
    hn = pltpu.roll(znext, HALO - s, 0)
    row = lax.broadcasted_iota(I32, hn.shape, 0)
    bot = jnp.where(row >= HALO - s, hn, rolled[n - HALO:])
    return jnp.concatenate([rolled[:n - HALO], bot], axis=0)


def _conv_fwd(pm, cw, name):
    T = pm.shape[0]
    bt = _pick(T, (512, 256))
    hb = bt // HALO

    def body(b_ref, c_ref, u_ref, cp_ref, up_ref, w_ref, o_ref):
        i = pl.program_id(0)
        z = c_ref[...].astype(F32) * u_ref[...].astype(F32)
        zp = cp_ref[...].astype(F32) * up_ref[...].astype(F32)
        zp = jnp.where(i > 0, zp, 0.0)
        w = w_ref[...]
        y = w[2:3] * z + w[1:2] * _shift_down(z, zp, 1) + w[0:1] * _shift_down(z, zp, 2)
        o_ref[...] = (b_ref[...].astype(F32) * y).astype(BF16)

    def col(cb):
        return pl.BlockSpec((bt, BRANCH), lambda i: (i, cb))

    def prev(cb):
        return pl.BlockSpec((HALO, BRANCH), lambda i: (jnp.maximum(i * hb - 1, 0), cb))

    return pl.pallas_call(
        body, name=name, grid=(T // bt,),
        in_specs=[col(CB_B), col(CB_C), col(CB_U), prev(CB_C), prev(CB_U),
                  pl.BlockSpec((8, BRANCH), lambda i: (0, 0))],
        out_specs=pl.BlockSpec((bt, BRANCH), lambda i: (i, 0)),
        out_shape=jax.ShapeDtypeStruct((T, BRANCH), BF16),
        compiler_params=_cp(("parallel",)),
    )(pm, pm, pm, pm, pm, cw)


def _conv_bwd(pm, cw, dy, dproj, name):
    T = pm.shape[0]
    bt = _pick(T, (512, 256))
    hb = bt // HALO
    nb = T // bt
    last_h = T // HALO - 1

    def body(b_ref, c_ref, u_ref, cp_ref, up_ref, bn_ref, dy_ref, dyn_ref, w_ref, buf_ref,
             dp_ref, dw_ref):
        del buf_ref
        db_ref = dp_ref.at[:, 0:BRANCH]
        dc_ref = dp_ref.at[:, BRANCH:2 * BRANCH]
        du_ref = dp_ref.at[:, 2 * BRANCH:3 * BRANCH]
        i = pl.program_id(0)
        cv = c_ref[...].astype(F32)
        uv = u_ref[...].astype(F32)
        bv = b_ref[...].astype(F32)
        z = cv * uv
        zp = jnp.where(i > 0, cp_ref[...].astype(F32) * up_ref[...].astype(F32), 0.0)
        w = w_ref[...]
        z1 = _shift_down(z, zp, 1)
        z2 = _shift_down(z, zp, 2)
        yc = w[2:3] * z + w[1:2] * z1 + w[0:1] * z2
        dyv = dy_ref[...].astype(F32)
        db_ref[...] = (dyv * yc).astype(BF16)
        g = dyv * bv
        gn = jnp.where(i < nb - 1, dyn_ref[...].astype(F32) * bn_ref[...].astype(F32), 0.0)
        dz = w[2:3] * g + w[1:2] * _shift_up(g, gn, 1) + w[0:1] * _shift_up(g, gn, 2)
        dc_ref[...] = (dz * uv).astype(BF16)
        du_ref[...] = (dz * cv).astype(BF16)

        @pl.when(i == 0)
        def _():
            dw_ref[...] = jnp.zeros_like(dw_ref)

        dw_ref[0:1, :] += jnp.sum(g * z2, axis=0, keepdims=True)
        dw_ref[1:2, :] += jnp.sum(g * z1, axis=0, keepdims=True)
        dw_ref[2:3, :] += jnp.sum(g * z, axis=0, keepdims=True)

    def col(cb):
        return pl.BlockSpec((bt, BRANCH), lambda i: (i, cb))

    def prev(cb):
        return pl.BlockSpec((HALO, BRANCH), lambda i: (jnp.maximum(i * hb - 1, 0), cb))

    def nxt(cb):
        return pl.BlockSpec((HALO, BRANCH), lambda i: (jnp.minimum((i + 1) * hb, last_h), cb))

    own = pl.BlockSpec((bt, BRANCH), lambda i: (i, 0))
    w_spec = pl.BlockSpec((8, BRANCH), lambda i: (0, 0))
    return pl.pallas_call(
        body, name=name, grid=(nb,),
        in_specs=[col(CB_B), col(CB_C), col(CB_U), prev(CB_C), prev(CB_U), nxt(CB_B), own,
                  pl.BlockSpec((HALO, BRANCH), lambda i: (jnp.minimum((i + 1) * hb, last_h), 0)), w_spec,
                  pl.BlockSpec(memory_space=pl.ANY)],
        out_specs=[pl.BlockSpec((bt, 3 * BRANCH), lambda i: (i, 2)), w_spec],
        out_shape=[jax.ShapeDtypeStruct(dproj.shape, dproj.dtype), jax.ShapeDtypeStruct((8, BRANCH), F32)],
        input_output_aliases={9: 0},
        compiler_params=_cp(("arbitrary",)),
    )(pm, pm, pm, pm, pm, pm, dy, dy, cw, dproj)


def _log_sigmoid(z):
    return jnp.minimum(z, 0.0) - jnp.log(1.0 + jnp.exp(-jnp.abs(z)))


def _fox_gate_fwd(fg, fb, name):
    T = fg.shape[0]
    bt = _pick(T, (256,))

    def body(f_ref, b_ref, c_ref, carry):
        @pl.when(pl.program_id(0) == 0)
        def _():
            carry[...] = jnp.zeros_like(carry)

        xv = _log_sigmoid(f_ref[...] + b_ref[...])
        row = lax.broadcasted_iota(I32, xv.shape, 0)
        s = 1
        while s < bt:
            xv = xv + jnp.where(row >= s, pltpu.roll(xv, s, 0), 0.0)
            s *= 2
        xv = xv + carry[...]
        c_ref[...] = xv
        carry[...] = xv[bt - 1:bt, :]

    blk = pl.BlockSpec((bt, 128), lambda i: (i, 0))
    return pl.pallas_call(
        body, name=name, grid=(T // bt,),
        in_specs=[blk, pl.BlockSpec((1, 128), lambda i: (0, 0))],
        out_specs=blk, out_shape=jax.ShapeDtypeStruct((T, 128), F32),
        scratch_shapes=[pltpu.VMEM((1, 128), F32)],
        compiler_params=_cp(("arbitrary",)),
    )(fg, fb)


def _fox_gate_bwd(dc, fg, fb, name):
    T = fg.shape[0]
    bt = _pick(T, (256,))
    nb = T // bt

    def body(d_ref, f_ref, b_ref, o_ref, db_ref, carry):
        @pl.when(pl.program_id(0) == 0)
        def _():
            carry[...] = jnp.zeros_like(carry)
            db_ref[...] = jnp.zeros_like(db_ref)

        xv = d_ref[...]
        row = lax.broadcasted_iota(I32, xv.shape, 0)
        s = 1
        while s < bt:
            xv = xv + jnp.where(row < bt - s, pltpu.roll(xv, bt - s, 0), 0.0)
            s *= 2
        xv = xv + carry[...]
        carry[...] = xv[0:1, :]
        z = f_ref[...] + b_ref[...]
        dz = xv * (1.0 / (1.0 + jnp.exp(z)))
        o_ref[...] = dz
        db_ref[...] += jnp.sum(dz, axis=0, keepdims=True)

    blk = pl.BlockSpec((bt, 128), lambda i: (nb - 1 - i, 0))
    vec = pl.BlockSpec((1, 128), lambda i: (0, 0))
    return pl.pallas_call(
        body, name=name, grid=(nb,),
        in_specs=[blk, blk, vec], out_specs=[blk, vec],
        out_shape=[jax.ShapeDtypeStruct((T, 128), F32), jax.ShapeDtypeStruct((1, 128), F32)],
        scratch_shapes=[pltpu.VMEM((1, 128), F32)],
        compiler_params=_cp(("arbitrary",)),
    )(dc, fg, fb)


def _lane_lo(shape):
    return lax.broadcasted_iota(I32, shape, 1) < HEAD_DIM


def _put_col(shape, h, col):
    lane = lax.broadcasted_iota(I32, shape, 1)
    return jnp.where(lane == h, col, 0.0)


def _fox_fwd(pm, c_col, c_row, name):
    T = pm.shape[0]
    bq = _pick(T, (512, 256))
    bk = bq
    nq = T // bq

    def body(q_ref, k_ref, v_ref, cq_ref, ck_ref, o_ref, lse_ref, acc, m_s, l_s):
        qi = pl.program_id(0)
        ki = pl.program_id(1)

        @pl.when(ki == 0)
        def _():
            acc[...] = jnp.zeros_like(acc)
            m_s[...] = jnp.full_like(m_s, NEG)
            l_s[...] = jnp.zeros_like(l_s)

        @pl.when(ki <= qi)
        def _():
            row = lax.broadcasted_iota(I32, (bq, bk), 0) + qi * bq
            colv = lax.broadcasted_iota(I32, (bq, bk), 1) + ki * bk
            causal = colv <= row
            klo = _lane_lo((bk, 128))
            qlo = _lane_lo((bq, 128))
            cq = cq_ref[...]
            ck = ck_ref[...]
            for p in range(4):
                sl = slice(128 * p, 128 * p + 128)
                qp = q_ref[:, sl] * ATT_SCALE
                kp = k_ref[:, sl]
                vp = v_ref[:, sl]
                kz = jnp.zeros_like(kp)
                ks = (jnp.where(klo, kp, kz), jnp.where(klo, kz, kp))
                alphas, pvs = [], []
                for j in range(2):
                    h = 2 * p + j
                    s = _dot_nt(qp, ks[j]) + (cq[:, h:h + 1] - ck[h:h + 1, :])
                    s = jnp.where(causal, s, NEG)
                    m_old = m_s[h][:, 0:1]
                    m_new = jnp.maximum(m_old, jnp.max(s, axis=-1, keepdims=True))
                    alpha = jnp.exp(m_old - m_new)
                    pe = jnp.exp(s - m_new)
                    l_new = alpha * l_s[h][:, 0:1] + jnp.sum(pe, axis=-1, keepdims=True)
                    m_s[h] = jnp.broadcast_to(m_new, (bq, 128))
                    l_s[h] = jnp.broadcast_to(l_new, (bq, 128))
                    alphas.append(alpha)
                    pvs.append(_dot_nn(pe.astype(BF16), vp))
                a = jnp.where(qlo, alphas[0], alphas[1])
                acc[:, sl] = a * acc[:, sl] + jnp.where(qlo, pvs[0], pvs[1])

        @pl.when(ki == nq - 1)
        def _():
            qlo = _lane_lo((bq, 128))
            lse = jnp.zeros((bq, 128), F32)
            for p in range(4):
                sl = slice(128 * p, 128 * p + 128)
                l0 = l_s[2 * p][:, 0:1]
                l1 = l_s[2 * p + 1][:, 0:1]
                o_ref[:, sl] = (acc[:, sl] / jnp.where(qlo, l0, l1)).astype(BF16)
                lse = lse + _put_col((bq, 128), 2 * p, m_s[2 * p][:, 0:1] + jnp.log(l0))
                lse = lse + _put_col((bq, 128), 2 * p + 1, m_s[2 * p + 1][:, 0:1] + jnp.log(l1))
            lse_ref[...] = lse

    return pl.pallas_call(
        body, name=name, grid=(nq, nq),
        in_specs=[pl.BlockSpec((bq, BRANCH), lambda i, k: (i, CB_FQ)),
                  pl.BlockSpec((bk, BRANCH), lambda i, k: (jnp.minimum(k, i), CB_FK)),
                  pl.BlockSpec((bk, BRANCH), lambda i, k: (jnp.minimum(k, i), CB_FV)),
                  pl.BlockSpec((bq, 128), lambda i, k: (i, 0)),
                  pl.BlockSpec((8, bk), lambda i, k: (0, jnp.minimum(k, i)))],
        out_specs=[pl.BlockSpec((bq, BRANCH), lambda i, k: (i, 0)),
                   pl.BlockSpec((bq, 128), lambda i, k: (i, 0))],
        out_shape=[jax.ShapeDtypeStruct((T, BRANCH), BF16), jax.ShapeDtypeStruct((T, 128), F32)],
        scratch_shapes=[pltpu.VMEM((bq, BRANCH), F32), pltpu.VMEM((8, bq, 128), F32),
                        pltpu.VMEM((8, bq, 128), F32)],
        compiler_params=_cp(("parallel", "arbitrary")),
    )(pm, pm, pm, c_col, c_row)


def _fox_delta(o, do, name):
    T = o.shape[0]
    bt = _pick(T, (512, 256))

    def body(o_ref, d_ref, out_ref):
        prod = o_ref[...].astype(F32) * d_ref[...].astype(F32)
        out = jnp.zeros((bt, 128), F32)
        for h in range(8):
            out = out + _put_col((bt, 128), h, jnp.sum(prod[:, 64 * h:64 * h + 64], axis=-1, keepdims=True))
        out_ref[...] = out

    blk = pl.BlockSpec((bt, BRANCH), lambda i: (i, 0))
    return pl.pallas_call(
        body, name=name, grid=(T // bt,), in_specs=[blk, blk],
        out_specs=pl.BlockSpec((bt, 128), lambda i: (i, 0)),
        out_shape=jax.ShapeDtypeStruct((T, 128), F32),
        compiler_params=_cp(("parallel",)),
    )(o, do)


def _fox_bwd_dq(pm, do, c_col, c_row, lse, delta, name):
    T = pm.shape[0]
    bq = _pick(T, (512, 256))
    bk = bq
    nq = T // bq

    def body(q_ref, k_ref, v_ref, do_ref, cq_ref, ck_ref, lse_ref, dl_ref, dq_ref, dl2_ref, acc, esum):
        qi = pl.program_id(0)
        ki = pl.program_id(1)

        @pl.when(ki == 0)
        def _():
            acc[...] = jnp.zeros_like(acc)
            esum[...] = jnp.zeros_like(esum)

        @pl.when(ki <= qi)
        def _():
            row = lax.broadcasted_iota(I32, (bq, bk), 0) + qi * bq
            colv = lax.broadcasted_iota(I32, (bq, bk), 1) + ki * bk
            causal = colv <= row
            klo = _lane_lo((bk, 128))
            qlo = _lane_lo((bq, 128))
            cq = cq_ref[...]
            ck = ck_ref[...]
            lse_v = lse_ref[...]
            dl_v = dl_ref[...]
            es = jnp.zeros((bq, 128), F32)
            for p in range(4):
                sl = slice(128 * p, 128 * p + 128)
                qp = q_ref[:, sl] * ATT_SCALE
                kp = k_ref[:, sl]
                vp = v_ref[:, sl]
                dop = do_ref[:, sl]
                kz = jnp.zeros_like(kp)
                ks = (jnp.where(klo, kp, kz), jnp.where(klo, kz, kp))
                vs = (jnp.where(klo, vp, kz), jnp.where(klo, kz, vp))
                dqs = []
                for j in range(2):
                    h = 2 * p + j
                    s = _dot_nt(qp, ks[j]) + (cq[:, h:h + 1] - ck[h:h + 1, :])
                    s = jnp.where(causal, s, NEG)
                    pr = jnp.exp(s - lse_v[:, h:h + 1])
                    dp = _dot_nt(dop, vs[j])
                    ds = pr * (dp - dl_v[:, h:h + 1])
                    es = es + _put_col((bq, 128), h, jnp.sum(ds, axis=-1, keepdims=True))
                    dqs.append(_dot_nn(ds.astype(BF16), kp))
                acc[:, sl] += jnp.where(qlo, dqs[0], dqs[1])
            esum[...] += es

        @pl.when(ki == nq - 1)
        def _():
            dq_ref[...] = (acc[...] * ATT_SCALE).astype(BF16)
            dl2_ref[...] = dl_ref[...] + esum[...]

    qb = pl.BlockSpec((bq, 128), lambda i, k: (i, 0))
    return pl.pallas_call(
        body, name=name, grid=(nq, nq),
        in_specs=[pl.BlockSpec((bq, BRANCH), lambda i, k: (i, CB_FQ)),
                  pl.BlockSpec((bk, BRANCH), lambda i, k: (jnp.minimum(k, i), CB_FK)),
                  pl.BlockSpec((bk, BRANCH), lambda i, k: (jnp.minimum(k, i), CB_FV)),
                  pl.BlockSpec((bq, BRANCH), lambda i, k: (i, 0)),
                  qb, pl.BlockSpec((8, bk), lambda i, k: (0, jnp.minimum(k, i))), qb, qb],
        out_specs=[pl.BlockSpec((bq, BRANCH), lambda i, k: (i, 0)), qb],
        out_shape=[jax.ShapeDtypeStruct((T, BRANCH), BF16), jax.ShapeDtypeStruct((T, 128), F32)],
        scratch_shapes=[pltpu.VMEM((bq, BRANCH), F32), pltpu.VMEM((bq, 128), F32)],
        compiler_params=_cp(("parallel", "arbitrary")),
    )(pm, pm, pm, do, c_col, c_row, lse, delta)


def _fox_bwd_dkv(pm, do, c_col, c_row, lse_row, delta_row, name):
    T = pm.shape[0]
    bk = _pick(T, (512, 256))
    bq = bk
    nk = T // bk

    def body(q_ref, k_ref, v_ref, do_ref, cq_ref, ck_ref, lse_ref, dl_ref,
             dk_ref, dv_ref, dc_ref, dk_acc, dv_acc, dc_acc):
        ki = pl.program_id(0)
        qi = pl.program_id(1)

        @pl.when(qi == 0)
        def _():
            dk_acc[...] = jnp.zeros_like(dk_acc)
            dv_acc[...] = jnp.zeros_like(dv_acc)
            dc_acc[...] = jnp.zeros_like(dc_acc)

        @pl.when(qi >= ki)
        def _():
            krow = lax.broadcasted_iota(I32, (bk, bq), 0) + ki * bk
            qcol = lax.broadcasted_iota(I32, (bk, bq), 1) + qi * bq
            causal = krow <= qcol
            qlo = _lane_lo((bq, 128))
            klo = _lane_lo((bk, 128))
            cq = cq_ref[...]
            ck = ck_ref[...]
            lse_v = lse_ref[...]
            dl_v = dl_ref[...]
            dcs = jnp.zeros((bk, 128), F32)
            for p in range(4):
                sl = slice(128 * p, 128 * p + 128)
                qp = q_ref[:, sl]
                kp = k_ref[:, sl] * ATT_SCALE
                vp = v_ref[:, sl]
                dop = do_ref[:, sl]
                qz = jnp.zeros_like(qp)
                qs = (jnp.where(qlo, qp, qz), jnp.where(qlo, qz, qp))
                dos = (jnp.where(qlo, dop, qz), jnp.where(qlo, qz, dop))
                dks, dvs = [], []
                for j in range(2):
                    h = 2 * p + j
                    st = _dot_nt(kp, qs[j]) + (cq[h:h + 1, :] - ck[:, h:h + 1])
                    st = jnp.where(causal, st, NEG)
                    pt = jnp.exp(st - lse_v[h:h + 1, :])
                    dvs.append(_dot_nn(pt.astype(BF16), dop))
                    dpt = _dot_nt(vp, dos[j])
                    dst = pt * (dpt - dl_v[h:h + 1, :])
                    dks.append(_dot_nn(dst.astype(BF16), qp))
                    dcs = dcs - _put_col((bk, 128), h, jnp.sum(dst, axis=-1, keepdims=True))
                dk_acc[:, sl] += jnp.where(klo, dks[0], dks[1])
                dv_acc[:, sl] += jnp.where(klo, dvs[0], dvs[1])
            dc_acc[...] += dcs

        @pl.when(qi == nk - 1)
        def _():
            dk_ref[...] = (dk_acc[...] * ATT_SCALE).astype(BF16)
            dv_ref[...] = dv_acc[...].astype(BF16)
            dc_ref[...] = dc_acc[...]

    qrow = pl.BlockSpec((8, bq), lambda k, i: (0, jnp.maximum(i, k)))
    kb = pl.BlockSpec((bk, BRANCH), lambda k, i: (k, 0))
    return pl.pallas_call(
        body, name=name, grid=(nk, nk),
        in_specs=[pl.BlockSpec((bq, BRANCH), lambda k, i: (jnp.maximum(i, k), CB_FQ)),
                  pl.BlockSpec((bk, BRANCH), lambda k, i: (k, CB_FK)),
                  pl.BlockSpec((bk, BRANCH), lambda k, i: (k, CB_FV)),
                  pl.BlockSpec((bq, BRANCH), lambda k, i: (jnp.maximum(i, k), 0)),
                  qrow, pl.BlockSpec((bk, 128), lambda k, i: (k, 0)), qrow, qrow],
        out_specs=[kb, kb, pl.BlockSpec((bk, 128), lambda k, i: (k, 0))],
        out_shape=[jax.ShapeDtypeStruct((T, BRANCH), BF16), jax.ShapeDtypeStruct((T, BRANCH), BF16),
                   jax.ShapeDtypeStruct((T, 128), F32)],
        scratch_shapes=[pltpu.VMEM((bk, BRANCH), F32), pltpu.VMEM((bk, BRANCH), F32),
                        pltpu.VMEM((bk, 128), F32)],
        compiler_params=_cp(("parallel", "arbitrary")),
    )(pm, pm, pm, do, c_row, c_col, lse_row, delta_row)


FOX_ROWS = 32


FOX_UNROLL = 16


def _row_start(r, rows):
    return r * rows if isinstance(r, int) else pl.multiple_of(r * rows, rows)


def _chunk_loop(n, chunk):
    if n <= FOX_UNROLL:
        for u in range(n):
            chunk(u, 0)
        return

    def outer(i, carry):
        for u in range(FOX_UNROLL):
            chunk(i * FOX_UNROLL + u, carry)
        return carry

    lax.fori_loop(0, n // FOX_UNROLL, outer, 0)


def _tree(op, xs):
    xs = list(xs)
    while len(xs) > 1:
        xs = [op(xs[i], xs[i + 1]) if i + 1 < len(xs) else xs[i] for i in range(0, len(xs), 2)]
    return xs[0]


def _masked_halves(t):
    lo = _lane_lo(t.shape)
    z = jnp.zeros_like(t)
    return jnp.where(lo, t, z), jnp.where(lo, z, t)


def _fox2_fwd(pm, c_row, name):
    T = pm.shape[0]
    bq = _pick(T, (512, 256))
    bk = bq
    nq = T // bq
    R = FOX_ROWS
    ng = bk // 128

    def body(q_ref, k_ref, v_ref, ck_ref, o_ref, lse_ref, acc, m_s, l_s, a_s, s_scr, p_scr):
        qi = pl.program_id(0)
        ki = pl.program_id(1)

        @pl.when(ki == 0)
        def _():
            acc[...] = jnp.zeros_like(acc)
            m_s[...] = jnp.full_like(m_s, NEG)
            l_s[...] = jnp.zeros_like(l_s)

        def block(masked):
            qlo = _lane_lo((bq, 128))
            for p in range(4):
                sl = slice(128 * p, 128 * p + 128)
                qp = q_ref[:, sl] * ATT_SCALE
                vp = v_ref[:, sl]
                ks = _masked_halves(k_ref[:, sl])
                pvs = []
                for j in range(2):
                    h = 2 * p + j
                    s_scr[j] = _dot_nt(qp, ks[j])

                    def chunk(r, carry, h=h, j=j):
                        r0 = _row_start(r, R)
                        rows = pl.ds(r0, R)
                        sc = [s_scr[j, rows, 128 * g:128 * g + 128] - ck_ref[h:h + 1, 128 * g:128 * g + 128]
                              for g in range(ng)]
                        if masked:
                            rid = lax.broadcasted_iota(I32, (R, 128), 0) + r0
                            cid = lax.broadcasted_iota(I32, (R, 128), 1)
                            sc = [jnp.where(cid + 128 * g <= rid, sc[g], NEG) for g in range(ng)]
                        m_old = m_s[h, rows, :]
                        m_new = jnp.maximum(m_old, jnp.max(_tree(jnp.maximum, sc), axis=-1, keepdims=True))
                        alpha = jnp.exp(m_old - m_new)
                        pe = [jnp.exp(sc[g] - m_new) for g in range(ng)]
                        l_s[h, rows, :] = alpha * l_s[h, rows, :] + _tree(jnp.add, pe)
                        m_s[h, rows, :] = m_new
                        a_s[j, rows, :] = alpha
                        for g in range(ng):
                            p_scr[j, rows, 128 * g:128 * g + 128] = pe[g].astype(BF16)
                        return carry

                    _chunk_loop(bq // R, chunk)
                    pvs.append(_dot_nn(p_scr[j], vp))
                acc[:, sl] = jnp.where(qlo, a_s[0], a_s[1]) * acc[:, sl] + jnp.where(qlo, pvs[0], pvs[1])

        @pl.when(ki < qi)
        def _():
            block(False)

        @pl.when(ki == qi)
        def _():
            block(True)

        @pl.when(ki == nq - 1)
        def _():
            qlo = _lane_lo((bq, 128))
            lse = jnp.zeros((bq, 128), F32)
            for p in range(4):
                sl = slice(128 * p, 128 * p + 128)
                l0 = jnp.sum(l_s[2 * p], axis=-1, keepdims=True)
                l1 = jnp.sum(l_s[2 * p + 1], axis=-1, keepdims=True)
                o_ref[:, sl] = (acc[:, sl] / jnp.where(qlo, l0, l1)).astype(BF16)
                lse = lse + _put_col((bq, 128), 2 * p, m_s[2 * p][:, 0:1] + jnp.log(l0))
                lse = lse + _put_col((bq, 128), 2 * p + 1, m_s[2 * p + 1][:, 0:1] + jnp.log(l1))
            lse_ref[...] = lse

    return pl.pallas_call(
        body, name=name, grid=(nq, nq),
        in_specs=[pl.BlockSpec((bq, BRANCH), lambda i, k: (i, CB_FQ)),
                  pl.BlockSpec((bk, BRANCH), lambda i, k: (jnp.minimum(k, i), CB_FK)),
                  pl.BlockSpec((bk, BRANCH), lambda i, k: (jnp.minimum(k, i), CB_FV)),
                  pl.BlockSpec((8, bk), lambda i, k: (0, jnp.minimum(k, i)))],
        out_specs=[pl.BlockSpec((bq, BRANCH), lambda i, k: (i, 0)),
                   pl.BlockSpec((bq, 128), lambda i, k: (i, 0))],
        out_shape=[jax.ShapeDtypeStruct((T, BRANCH), BF16), jax.ShapeDtypeStruct((T, 128), F32)],
        scratch_shapes=[pltpu.VMEM((bq, BRANCH), F32), pltpu.VMEM((8, bq, 128), F32),
                        pltpu.VMEM((8, bq, 128), F32), pltpu.VMEM((2, bq, 128), F32),
                        pltpu.VMEM((2, bq, bk), F32), pltpu.VMEM((2, bq, bk), BF16)],
        compiler_params=_cp(("parallel", "arbitrary")),
    )(pm, pm, pm, c_row)


def _fox2_bwd_dq(pm, do, c_row, lse, delta, dproj, name):
    T = pm.shape[0]
    bq = _pick(T, (512, 256))
    bk = bq
    nq = T // bq
    R = FOX_ROWS
    ng = bk // 128

    def body(q_ref, k_ref, v_ref, do_ref, ck_ref, lse_ref, dl_ref, buf_ref, dq_ref, dl2_ref,
             acc, e_s, s_scr, dp_scr, ds_scr):
        del buf_ref
        qi = pl.program_id(0)
        ki = pl.program_id(1)

        @pl.when(ki == 0)
        def _():
            acc[...] = jnp.zeros_like(acc)
            e_s[...] = jnp.zeros_like(e_s)

        def block(masked):
            qlo = _lane_lo((bq, 128))
            for p in range(4):
                sl = slice(128 * p, 128 * p + 128)
                qp = q_ref[:, sl] * ATT_SCALE
                kp = k_ref[:, sl]
                dop = do_ref[:, sl]
                ks = _masked_halves(kp)
                vs = _masked_halves(v_ref[:, sl])
                dqs = []
                for j in range(2):
                    h = 2 * p + j
                    s_scr[...] = _dot_nt(qp, ks[j])
                    dp_scr[...] = _dot_nt(dop, vs[j])

                    def chunk(r, carry, h=h):
                        r0 = _row_start(r, R)
                        rows = pl.ds(r0, R)
                        lse_c = lse_ref[rows, h:h + 1]
                        dl_c = dl_ref[rows, h:h + 1]
                        if masked:
                            rid = lax.broadcasted_iota(I32, (R, 128), 0) + r0
                            cid = lax.broadcasted_iota(I32, (R, 128), 1)
                        dss = []
                        for g in range(ng):
                            gs = slice(128 * g, 128 * g + 128)
                            sc = s_scr[rows, gs] - ck_ref[h:h + 1, gs]
                            if masked:
                                sc = jnp.where(cid + 128 * g <= rid, sc, NEG)
                            ds = jnp.exp(sc - lse_c) * (dp_scr[rows, gs] - dl_c)
                            ds_scr[rows, gs] = ds.astype(BF16)
                            dss.append(ds)
                        e_s[h, rows, :] += _tree(jnp.add, dss)
                        return carry

                    _chunk_loop(bq // R, chunk)
                    dqs.append(_dot_nn(ds_scr[...], kp))
                acc[:, sl] += jnp.where(qlo, dqs[0], dqs[1])

        @pl.when(ki < qi)
        def _():
            block(False)

        @pl.when(ki == qi)
        def _():
            block(True)

        @pl.when(ki == nq - 1)
        def _():
            dq_ref[...] = (acc[...] * ATT_SCALE).astype(BF16)
            out = dl_ref[...]
            for h in range(8):
                out = out + _put_col((bq, 128), h, jnp.sum(e_s[h], axis=-1, keepdims=True))
            dl2_ref[...] = out

    qb = pl.BlockSpec((bq, 128), lambda i, k: (i, 0))
    return pl.pallas_call(
        body, name=name, grid=(nq, nq),
        in_specs=[pl.BlockSpec((bq, BRANCH), lambda i, k: (i, CB_FQ)),
                  pl.BlockSpec((bk, BRANCH), lambda i, k: (jnp.minimum(k, i), CB_FK)),
                  pl.BlockSpec((bk, BRANCH), lambda i, k: (jnp.minimum(k, i), CB_FV)),
                  pl.BlockSpec((bq, BRANCH), lambda i, k: (i, 0)),
                  pl.BlockSpec((8, bk), lambda i, k: (0, jnp.minimum(k, i))), qb, qb,
                  pl.BlockSpec(memory_space=pl.ANY)],
        out_specs=[pl.BlockSpec((bq, BRANCH), lambda i, k: (i, CB_FQ)), qb],
        out_shape=[jax.ShapeDtypeStruct(dproj.shape, dproj.dtype), jax.ShapeDtypeStruct((T, 128), F32)],
        input_output_aliases={7: 0},
        scratch_shapes=[pltpu.VMEM((bq, BRANCH), F32), pltpu.VMEM((8, bq, 128), F32),
                        pltpu.VMEM((bq, bk), F32), pltpu.VMEM((bq, bk), F32), pltpu.VMEM((bq, bk), BF16)],
        compiler_params=_cp(("parallel", "arbitrary")),
    )(pm, pm, pm, do, c_row, lse, delta, dproj)


def _fox2_bwd_dkv(pm, do, c_col, lse_row, delta_row, dproj, name):
    T = pm.shape[0]
    bk = _pick(T, (512, 256))
    bq = bk
    nk = T // bk
    R = FOX_ROWS
    ng = bq // 128

    def body(q_ref, k_ref, v_ref, do_ref, ck_ref, lse_ref, dl_ref, buf_ref, dkv_ref, dc_ref,
             dk_acc, dv_acc, dc_s, st_scr, dpt_scr, pt_scr, dst_scr):
        del buf_ref
        dk_ref = dkv_ref.at[:, 0:BRANCH]
        dv_ref = dkv_ref.at[:, BRANCH:2 * BRANCH]
        ki = pl.program_id(0)
        qi = pl.program_id(1)

        @pl.when(qi == 0)
        def _():
            dk_acc[...] = jnp.zeros_like(dk_acc)
            dv_acc[...] = jnp.zeros_like(dv_acc)
            dc_s[...] = jnp.zeros_like(dc_s)

        def block(masked):
            klo = _lane_lo((bk, 128))
            for p in range(4):
                sl = slice(128 * p, 128 * p + 128)
                qp = q_ref[:, sl]
                kp = k_ref[:, sl] * ATT_SCALE
                vp = v_ref[:, sl]
                dop = do_ref[:, sl]
                qs = _masked_halves(qp)
                dos = _masked_halves(dop)
                dks, dvs = [], []
                for j in range(2):
                    h = 2 * p + j
                    st_scr[...] = _dot_nt(kp, qs[j])
                    dpt_scr[...] = _dot_nt(vp, dos[j])

                    def chunk(r, carry, h=h):
                        r0 = _row_start(r, R)
                        rows = pl.ds(r0, R)
                        ck_c = ck_ref[rows, h:h + 1]
                        if masked:
                            kid = lax.broadcasted_iota(I32, (R, 128), 0) + r0
                            qid = lax.broadcasted_iota(I32, (R, 128), 1)
                        dss = []
                        for g in range(ng):
                            gs = slice(128 * g, 128 * g + 128)
                            st = st_scr[rows, gs] - (ck_c + lse_ref[h:h + 1, gs])
                            if masked:
                                st = jnp.where(kid <= qid + 128 * g, st, NEG)
                            pt = jnp.exp(st)
                            dst = pt * (dpt_scr[rows, gs] - dl_ref[h:h + 1, gs])
                            pt_scr[rows, gs] = pt.astype(BF16)
                            dst_scr[rows, gs] = dst.astype(BF16)
                            dss.append(dst)
                        dc_s[h, rows, :] -= _tree(jnp.add, dss)
                        return carry

                    _chunk_loop(bk // R, chunk)
                    dvs.append(_dot_nn(pt_scr[...], dop))
                    dks.append(_dot_nn(dst_scr[...], qp))
                dk_acc[:, sl] += jnp.where(klo, dks[0], dks[1])
                dv_acc[:, sl] += jnp.where(klo, dvs[0], dvs[1])

        @pl.when(qi > ki)
        def _():
            block(False)

        @pl.when(qi == ki)
        def _():
            block(True)

        @pl.when(qi == nk - 1)
        def _():
            dk_ref[...] = (dk_acc[...] * ATT_SCALE).astype(BF16)
            dv_ref[...] = dv_acc[...].astype(BF16)
            out = jnp.zeros((bk, 128), F32)
            for h in range(8):
                out = out + _put_col((bk, 128), h, jnp.sum(dc_s[h], axis=-1, keepdims=True))
            dc_ref[...] = out

    qrow = pl.BlockSpec((8, bq), lambda k, i: (0, jnp.maximum(i, k)))
    kb = pl.BlockSpec((bk, BRANCH), lambda k, i: (k, 0))
    return pl.pallas_call(
        body, name=name, grid=(nk, nk),
        in_specs=[pl.BlockSpec((bq, BRANCH), lambda k, i: (jnp.maximum(i, k), CB_FQ)),
                  pl.BlockSpec((bk, BRANCH), lambda k, i: (k, CB_FK)),
                  pl.BlockSpec((bk, BRANCH), lambda k, i: (k, CB_FV)),
                  pl.BlockSpec((bq, BRANCH), lambda k, i: (jnp.maximum(i, k), 0)),
                  pl.BlockSpec((bk, 128), lambda k, i: (k, 0)), qrow, qrow, pl.BlockSpec(memory_space=pl.ANY)],
        out_specs=[pl.BlockSpec((bk, 2 * BRANCH), lambda k, i: (k, 5)), pl.BlockSpec((bk, 128), lambda k, i: (k, 0))],
        out_shape=[jax.ShapeDtypeStruct(dproj.shape, dproj.dtype), jax.ShapeDtypeStruct((T, 128), F32)],
        input_output_aliases={7: 0},
        scratch_shapes=[pltpu.VMEM((bk, BRANCH), F32), pltpu.VMEM((bk, BRANCH), F32),
                        pltpu.VMEM((8, bk, 128), F32), pltpu.VMEM((bk, bq), F32), pltpu.VMEM((bk, bq), F32),
                        pltpu.VMEM((bk, bq), BF16), pltpu.VMEM((bk, bq), BF16)],
        compiler_params=_cp(("parallel", "arbitrary")),
    )(pm, pm, pm, do, c_col, lse_row, delta_row, dproj)


def _bucket_table():
    tq = np.arange(WINDOW, dtype=np.int32)[:, None]
    sk = np.arange(2 * WINDOW, dtype=np.int32)[None, :]
    n = np.maximum(WINDOW + tq - sk, 0)
    max_exact = N_BUCKETS // 2
    ratio = np.maximum(n, 1).astype(np.float32) / np.float32(max_exact)
    large = max_exact + (np.log(ratio) / np.float32(math.log(WINDOW / max_exact))
                         * np.float32(N_BUCKETS - max_exact)).astype(np.int32)
    large = np.minimum(large, N_BUCKETS - 1)
    return np.where(n < max_exact, n, large).astype(np.int32)


def _swa_bias(rel_bias, bucket, name):
    def body(rb_ref, bk_ref, o_ref):
        bkt = bk_ref[...]
        for h in range(8):
            def step(b, a):
                return a + jnp.where(bkt == b, rb_ref[b, h], 0.0)
            o_ref[h] = lax.fori_loop(0, N_BUCKETS, step, jnp.zeros(bkt.shape, F32))

    return pl.pallas_call(
        body, name=name,
        in_specs=[pl.BlockSpec(memory_space=pltpu.SMEM), pl.BlockSpec(memory_space=pltpu.VMEM)],
        out_specs=pl.BlockSpec(memory_space=pltpu.VMEM),
        out_shape=jax.ShapeDtypeStruct((8, WINDOW, 2 * WINDOW), F32),
    )(rel_bias, bucket)


def _swa_dbias_reduce(dbias, bucket, name):
    def body(d_ref, bk_ref, o_ref):
        bkt = bk_ref[...]
        rowi = lax.broadcasted_iota(I32, (N_BUCKETS, 128), 0)
        lane = lax.broadcasted_iota(I32, (N_BUCKETS, 128), 1)
        out = jnp.zeros((N_BUCKETS, 128), F32)
        for h in range(8):
            dv = d_ref[h]

            def step(b, a):
                tot = jnp.sum(jnp.where(bkt == b, dv, 0.0), keepdims=True)
                return a + jnp.where((rowi == b) & (lane == h), tot, 0.0)
            out = lax.fori_loop(0, N_BUCKETS, step, out)
        o_ref[...] = out

    return pl.pallas_call(
        body, name=name,
        in_specs=[pl.BlockSpec(memory_space=pltpu.VMEM), pl.BlockSpec(memory_space=pltpu.VMEM)],
        out_specs=pl.BlockSpec(memory_space=pltpu.VMEM),
        out_shape=jax.ShapeDtypeStruct((N_BUCKETS, 128), F32),
    )(dbias, bucket)


def _swap_halves(x):
    return pltpu.roll(x.astype(F32), HEAD_DIM, 1).astype(x.dtype)


def _kv_variants(t):
    lo = _lane_lo(t.shape)
    z = jnp.zeros_like(t)
    a0 = jnp.where(lo, t, z)
    b1 = jnp.where(lo, z, t)
    b0 = _swap_halves(a0)
    a1 = _swap_halves(b1)
    return (a0, a1), (b0, b1), (a0 + b0, a1 + b1)


def _swa_masks(i):
    tq = lax.broadcasted_iota(I32, (WINDOW, WINDOW), 0)
    jj = lax.broadcasted_iota(I32, (WINDOW, WINDOW), 1)
    return (jj > tq) & (i > 0), jj <= tq


def _swa_specs():
    q = pl.BlockSpec((WINDOW, BRANCH), lambda i: (i, CB_SQ))
    kc = pl.BlockSpec((WINDOW, 128), lambda i: (i, CB_SK))
    kp = pl.BlockSpec((WINDOW, 128), lambda i: (jnp.maximum(i - 1, 0), CB_SK))
    vc = pl.BlockSpec((WINDOW, 128), lambda i: (i, CB_SV))
    vp = pl.BlockSpec((WINDOW, 128), lambda i: (jnp.maximum(i - 1, 0), CB_SV))
    bias = pl.BlockSpec((8, WINDOW, 2 * WINDOW), lambda i: (0, 0, 0))
    vec = pl.BlockSpec((1, 128), lambda i: (0, 0))
    return q, kc, kp, vc, vp, bias, vec


def _swa_fwd(pm, bias, sink, name):
    T = pm.shape[0]
    nb = T // WINDOW

    def body(q_ref, kc_ref, kp_ref, vc_ref, vp_ref, b_ref, s_ref, o_ref, m_ref):
        i = pl.program_id(0)
        mprev, mcur = _swa_masks(i)
        kcA, kcB, _ = _kv_variants(kc_ref[...])
        kpA, kpB, _ = _kv_variants(kp_ref[...])
        _, _, vcD = _kv_variants(vc_ref[...])
        _, _, vpD = _kv_variants(vp_ref[...])
        lo = _lane_lo((WINDOW, 128))
        sink_v = s_ref[...]
        mout = jnp.zeros((WINDOW, 128), F32)
        for p in range(4):
            jv = p // 2
            sl = slice(128 * p, 128 * p + 128)
            qp = q_ref[:, sl] * ATT_SCALE
            outs = []
            for par in range(2):
                h = 2 * p + par
                kpx = (kpA, kpB)[par][jv]
                kcx = (kcA, kcB)[par][jv]
                sp = jnp.where(mprev, _dot_nt(qp, kpx) + b_ref[h, :, 0:WINDOW], NEG)
                sc = jnp.where(mcur, _dot_nt(qp, kcx) + b_ref[h, :, WINDOW:2 * WINDOW], NEG)
                sk_h = sink_v[:, h:h + 1]
                m = jnp.maximum(jnp.maximum(jnp.max(sp, axis=-1, keepdims=True),
                                            jnp.max(sc, axis=-1, keepdims=True)), sk_h)
                ep = jnp.exp(sp - m)
                ec = jnp.exp(sc - m)
                den = (jnp.sum(ep, axis=-1, keepdims=True) + jnp.sum(ec, axis=-1, keepdims=True)
                       + jnp.exp(sk_h - m))
                inv = 1.0 / den
                outs.append(_dot_nn((ep * inv).astype(BF16), vpD[jv])
                            + _dot_nn((ec * inv).astype(BF16), vcD[jv]))
                mout = mout + _put_col((WINDOW, 128), h, m + jnp.log(den))
            o_ref[:, sl] = jnp.where(lo, outs[0], outs[1]).astype(BF16)
        m_ref[...] = mout

    q, kc, kp, vc, vp, bs, vec = _swa_specs()
    return pl.pallas_call(
        body, name=name, grid=(nb,),
        in_specs=[q, kc, kp, vc, vp, bs, vec],
        out_specs=[pl.BlockSpec((WINDOW, BRANCH), lambda i: (i, 0)),
                   pl.BlockSpec((WINDOW, 128), lambda i: (i, 0))],
        out_shape=[jax.ShapeDtypeStruct((T, BRANCH), BF16), jax.ShapeDtypeStruct((T, 128), F32)],
        compiler_params=_cp(("parallel",)),
    )(pm, pm, pm, pm, pm, bias, sink)


def _swa_bwd(pm, bias, sink, do, mlse, name):
    T = pm.shape[0]
    nb = T // WINDOW

    def fold(zz):
        return zz + pltpu.roll(zz, HEAD_DIM, 1)

    def body(q_ref, kc_ref, kp_ref, vc_ref, vp_ref, b_ref, s_ref, do_ref, m_ref,
             dq_ref, dkc_ref, dkp_ref, dvc_ref, dvp_ref, db_ref, ds_ref):
        i = pl.program_id(0)

        @pl.when(i == 0)
        def _():
            db_ref[...] = jnp.zeros_like(db_ref)
            ds_ref[...] = jnp.zeros_like(ds_ref)

        mprev, mcur = _swa_masks(i)
        kcA, kcB, kcD = _kv_variants(kc_ref[...])
        kpA, kpB, kpD = _kv_variants(kp_ref[...])
        vcA, vcB, _ = _kv_variants(vc_ref[...])
        vpA, vpB, _ = _kv_variants(vp_ref[...])
        lo = _lane_lo((WINDOW, 128))
        sink_v = s_ref[...]
        mv = m_ref[...]
        zk = jnp.zeros((WINDOW, 128), F32)
        zkp, zkc, zvp, zvc = [zk, zk], [zk, zk], [zk, zk], [zk, zk]
        dsink = jnp.zeros((1, 128), F32)
        for p in range(4):
            jv = p // 2
            sl = slice(128 * p, 128 * p + 128)
            qraw = q_ref[:, sl]
            qp = qraw * ATT_SCALE
            dop = do_ref[:, sl]
            dqs, mkp, mkc, mvp, mvc = [], [], [], [], []
            for par in range(2):
                h = 2 * p + par
                kpx = (kpA, kpB)[par][jv]
                kcx = (kcA, kcB)[par][jv]
                vpx = (vpA, vpB)[par][jv]
                vcx = (vcA, vcB)[par][jv]
                sp = jnp.where(mprev, _dot_nt(qp, kpx) + b_ref[h, :, 0:WINDOW], NEG)
                sc = jnp.where(mcur, _dot_nt(qp, kcx) + b_ref[h, :, WINDOW:2 * WINDOW], NEG)
                m_h = mv[:, h:h + 1]
                pp = jnp.exp(sp - m_h)
                pc = jnp.exp(sc - m_h)
                psink = jnp.exp(sink_v[:, h:h + 1] - m_h)
                dpp = _dot_nt(dop, vpx)
                dpc = _dot_nt(dop, vcx)
                delta = jnp.sum(pp * dpp, axis=-1, keepdims=True) + jnp.sum(pc * dpc, axis=-1, keepdims=True)
                dsp = pp * (dpp - delta)
                dsc = pc * (dpc - delta)
                db_ref[h, :, 0:WINDOW] += dsp
                db_ref[h, :, WINDOW:2 * WINDOW] += dsc
                dsink = dsink - _put_col((1, 128), h, jnp.sum(psink * delta, keepdims=True))
                dsp_b = dsp.astype(BF16)
                dsc_b = dsc.astype(BF16)
                dqs.append(_dot_nn(dsp_b, kpD[jv]) + _dot_nn(dsc_b, kcD[jv]))
                mkp.append(_dot_tn(dsp_b, qraw))
                mkc.append(_dot_tn(dsc_b, qraw))
                mvp.append(_dot_tn(pp.astype(BF16), dop))
                mvc.append(_dot_tn(pc.astype(BF16), dop))
            dq_ref[:, sl] = (jnp.where(lo, dqs[0], dqs[1]) * ATT_SCALE).astype(BF16)
            zkp[jv] = zkp[jv] + jnp.where(lo, mkp[0], mkp[1])
            zkc[jv] = zkc[jv] + jnp.where(lo, mkc[0], mkc[1])
            zvp[jv] = zvp[jv] + jnp.where(lo, mvp[0], mvp[1])
            zvc[jv] = zvc[jv] + jnp.where(lo, mvc[0], mvc[1])
        dkc_ref[...] = jnp.where(lo, fold(zkc[0]), fold(zkc[1])) * ATT_SCALE
        dkp_ref[...] = jnp.where(lo, fold(zkp[0]), fold(zkp[1])) * ATT_SCALE
        dvc_ref[...] = jnp.where(lo, fold(zvc[0]), fold(zvc[1]))
        dvp_ref[...] = jnp.where(lo, fold(zvp[0]), fold(zvp[1]))
        ds_ref[...] += dsink

    q, kc, kp, vc, vp, bs, vec = _swa_specs()
    own = pl.BlockSpec((WINDOW, BRANCH), lambda i: (i, 0))
    sm = pl.BlockSpec((WINDOW, 128), lambda i: (i, 0))
    f128 = jax.ShapeDtypeStruct((T, 128), F32)
    return pl.pallas_call(
        body, name=name, grid=(nb,),
        in_specs=[q, kc, kp, vc, vp, bs, vec, own, sm],
        out_specs=[own, sm, sm, sm, sm, bs, vec],
        out_shape=[jax.ShapeDtypeStruct((T, BRANCH), BF16), f128, f128, f128, f128,
                   jax.ShapeDtypeStruct((8, WINDOW, 2 * WINDOW), F32), jax.ShapeDtypeStruct((1, 128), F32)],
        compiler_params=_cp(("arbitrary",)),
    )(pm, pm, pm, pm, pm, bias, sink, do, mlse)


def _stacked_head(s, r):
    return 4 * (s // 2) + 2 * r + (s % 2)


def _swa2_bias(rel_bias, bucket, name):
    def body(rb_ref, bk_ref, o_ref):
        bkt = bk_ref[...]
        tq = lax.broadcasted_iota(I32, bkt.shape, 0)
        jj = lax.broadcasted_iota(I32, bkt.shape, 1)
        window = ((jj < WINDOW) & (jj > tq)) | ((jj >= WINDOW) & (jj - WINDOW <= tq))
        for s in range(4):
            for r in range(2):
                h = _stacked_head(s, r)

                def step(b, a, h=h):
                    return a + jnp.where(bkt == b, rb_ref[b, h], 0.0)
                val = lax.fori_loop(0, N_BUCKETS, step, jnp.zeros(bkt.shape, F32))
                o_ref[s, WINDOW * r:WINDOW * (r + 1), :] = jnp.where(window, val, NEG)

    return pl.pallas_call(
        body, name=name,
        in_specs=[pl.BlockSpec(memory_space=pltpu.SMEM), pl.BlockSpec(memory_space=pltpu.VMEM)],
        out_specs=pl.BlockSpec(memory_space=pltpu.VMEM),
        out_shape=jax.ShapeDtypeStruct((4, 2 * WINDOW, 2 * WINDOW), F32),
    )(rel_bias, bucket)


def _swa2_dbias_reduce(dbias, bucket, name):
    def body(d_ref, bk_ref, o_ref):
        bkt = bk_ref[...]
        rowi = lax.broadcasted_iota(I32, (N_BUCKETS, 128), 0)
        lane = lax.broadcasted_iota(I32, (N_BUCKETS, 128), 1)
        out = jnp.zeros((N_BUCKETS, 128), F32)
        for s in range(4):
            for r in range(2):
                h = _stacked_head(s, r)
                dv = d_ref[s, WINDOW * r:WINDOW * (r + 1), :]

                def step(b, a, dv=dv, h=h):
                    tot = jnp.sum(jnp.where(bkt == b, dv, 0.0), keepdims=True)
                    return a + jnp.where((rowi == b) & (lane == h), tot, 0.0)
                out = lax.fori_loop(0, N_BUCKETS, step, out)
        o_ref[...] = out

    return pl.pallas_call(
        body, name=name,
        in_specs=[pl.BlockSpec(memory_space=pltpu.VMEM), pl.BlockSpec(memory_space=pltpu.VMEM)],
        out_specs=pl.BlockSpec(memory_space=pltpu.VMEM),
        out_shape=jax.ShapeDtypeStruct((N_BUCKETS, 128), F32),
    )(dbias, bucket)


def _swa2_mask(i):
    tq = jnp.bitwise_and(lax.broadcasted_iota(I32, (2 * WINDOW, 2 * WINDOW), 0), WINDOW - 1)
    jj = lax.broadcasted_iota(I32, (2 * WINDOW, 2 * WINDOW), 1)
    return ((jj < WINDOW) & (jj > tq) & (i > 0)) | ((jj >= WINDOW) & (jj - WINDOW <= tq))


def _swa2_cols(vec, s):
    rows = lax.broadcasted_iota(I32, (2 * WINDOW, 1), 0)
    return jnp.where(rows < WINDOW, vec[:, _stacked_head(s, 0):_stacked_head(s, 0) + 1],
                     vec[:, _stacked_head(s, 1):_stacked_head(s, 1) + 1])


def _swa2_stack(ref, g):
    return jnp.concatenate([ref[:, 256 * g:256 * g + 128], ref[:, 256 * g + 128:256 * g + 256]], axis=0)


def _swa2_specs():
    q, kc, kp, vc, vp, _, vec = _swa_specs()
    bias = pl.BlockSpec((4, 2 * WINDOW, 2 * WINDOW), lambda i: (0, 0, 0))
    return q, kc, kp, vc, vp, bias, vec


def _swa2_fwd(pm, bias, sink, name):
    T = pm.shape[0]
    nb = T // WINDOW

    def body(q_ref, kc_ref, kp_ref, vc_ref, vp_ref, b_ref, s_ref, o_ref, m_ref):
        i = pl.program_id(0)
        mask = _swa2_mask(i)
        kcA, kcB, _ = _kv_variants(kc_ref[...])
        kpA, kpB, _ = _kv_variants(kp_ref[...])
        _, _, vcD = _kv_variants(vc_ref[...])
        _, _, vpD = _kv_variants(vp_ref[...])
        lo = _lane_lo((WINDOW, 128))
        sink_v = s_ref[...]
        mout = jnp.zeros((WINDOW, 128), F32)
        for g in range(2):
            qg = _swa2_stack(q_ref, g) * ATT_SCALE
            vband = jnp.concatenate([vpD[g], vcD[g]], axis=0)
            outs = []
            for par in range(2):
                s = 2 * g + par
                kband = jnp.concatenate([(kpA, kpB)[par][g], (kcA, kcB)[par][g]], axis=0)
                sc = jnp.where(mask, _dot_nt(qg, kband) + b_ref[s], NEG)
                sk = _swa2_cols(sink_v, s)
                m = jnp.maximum(jnp.max(sc, axis=-1, keepdims=True), sk)
                e = jnp.exp(sc - m)
                den = jnp.sum(e, axis=-1, keepdims=True) + jnp.exp(sk - m)
                outs.append(_dot_nn((e * (1.0 / den)).astype(BF16), vband))
                lse = m + jnp.log(den)
                mout = mout + _put_col((WINDOW, 128), _stacked_head(s, 0), lse[:WINDOW])
                mout = mout + _put_col((WINDOW, 128), _stacked_head(s, 1), lse[WINDOW:])
            for r in range(2):
                sl = slice(256 * g + 128 * r, 256 * g + 128 * r + 128)
                o_ref[:, sl] = jnp.where(lo, outs[0][WINDOW * r:WINDOW * (r + 1)],
                                         outs[1][WINDOW * r:WINDOW * (r + 1)]).astype(BF16)
        m_ref[...] = mout

    q, kc, kp, vc, vp, bs, vec = _swa2_specs()
    return pl.pallas_call(
        body, name=name, grid=(nb,),
        in_specs=[q, kc, kp, vc, vp, bs, vec],
        out_specs=[pl.BlockSpec((WINDOW, BRANCH), lambda i: (i, 0)),
                   pl.BlockSpec((WINDOW, 128), lambda i: (i, 0))],
        out_shape=[jax.ShapeDtypeStruct((T, BRANCH), BF16), jax.ShapeDtypeStruct((T, 128), F32)],
        compiler_params=_cp(("parallel",)),
    )(pm, pm, pm, pm, pm, bias, sink)


def _swa4_mask(no_prev):
    tq = jnp.bitwise_and(lax.broadcasted_iota(I32, (2 * WINDOW, 2 * WINDOW), 0), WINDOW - 1)
    jj = lax.broadcasted_iota(I32, (2 * WINDOW, 2 * WINDOW), 1)
    prev = (jj < WINDOW) & (jj > tq)
    if no_prev is not False:
        prev = prev & jnp.logical_not(no_prev)
    return prev | ((jj >= WINDOW) & (jj - WINDOW <= tq))


def _swa4_stack(ref, rows, g):
    return jnp.concatenate([ref[rows, 256 * g:256 * g + 128], ref[rows, 256 * g + 128:256 * g + 256]], axis=0)


def _swa4_specs():
    W2 = 2 * WINDOW
    q = pl.BlockSpec((W2, BRANCH), lambda i: (i, CB_SQ))
    kc = pl.BlockSpec((W2, 128), lambda i: (i, CB_SK))
    kp = pl.BlockSpec((WINDOW, 128), lambda i: (jnp.maximum(2 * i - 1, 0), CB_SK))
    vc = pl.BlockSpec((W2, 128), lambda i: (i, CB_SV))
    vp = pl.BlockSpec((WINDOW, 128), lambda i: (jnp.maximum(2 * i - 1, 0), CB_SV))
    bias = pl.BlockSpec((4, W2, W2), lambda i: (0, 0, 0))
    vec = pl.BlockSpec((1, 128), lambda i: (0, 0))
    return q, kc, kp, vc, vp, bias, vec


def _swa4_fwd(pm, bias, sink, name):
    T = pm.shape[0]
    nb = T // (2 * WINDOW)

    def body(q_ref, kc_ref, kp_ref, vc_ref, vp_ref, b_ref, s_ref, o_ref, m_ref):
        i = pl.program_id(0)
        lo = _lane_lo((WINDOW, 128))
        sink_v = s_ref[...]
        for u in range(2):
            rows = slice(WINDOW * u, WINDOW * (u + 1))
            mask = _swa4_mask(i == 0 if u == 0 else False)
            kcur, vcur = kc_ref[rows, :], vc_ref[rows, :]
            kprev = kp_ref[...] if u == 0 else kc_ref[0:WINDOW, :]
            vprev = vp_ref[...] if u == 0 else vc_ref[0:WINDOW, :]
            kcA, kcB, _ = _kv_variants(kcur)
            kpA, kpB, _ = _kv_variants(kprev)
            _, _, vcD = _kv_variants(vcur)
            _, _, vpD = _kv_variants(vprev)
            mout = jnp.zeros((WINDOW, 128), F32)
            for g in range(2):
                qg = _swa4_stack(q_ref, rows, g) * ATT_SCALE
                vband = jnp.concatenate([vpD[g], vcD[g]], axis=0)
                outs = []
                for par in range(2):
                    s = 2 * g + par
                    kband = jnp.concatenate([(kpA, kpB)[par][g], (kcA, kcB)[par][g]], axis=0)
                    sc = jnp.where(mask, _dot_nt(qg, kband) + b_ref[s], NEG)
                    sk = _swa2_cols(sink_v, s)
                    m = jnp.maximum(jnp.max(sc, axis=-1, keepdims=True), sk)
                    e = jnp.exp(sc - m)
                    den = jnp.sum(e, axis=-1, keepdims=True) + jnp.exp(sk - m)
                    outs.append(_dot_nn((e * (1.0 / den)).astype(BF16), vband))
                    lse = m + jnp.log(den)
                    mout = mout + _put_col((WINDOW, 128), _stacked_head(s, 0), lse[:WINDOW])
                    mout = mout + _put_col((WINDOW, 128), _stacked_head(s, 1), lse[WINDOW:])
                for r in range(2):
                    sl = slice(256 * g + 128 * r, 256 * g + 128 * r + 128)
                    o_ref[rows, sl] = jnp.where(lo, outs[0][WINDOW * r:WINDOW * (r + 1)],
                                                outs[1][WINDOW * r:WINDOW * (r + 1)]).astype(BF16)
            m_ref[rows, :] = mout

    q, kc, kp, vc, vp, bs, vec = _swa4_specs()
    return pl.pallas_call(
        body, name=name, grid=(nb,),
        in_specs=[q, kc, kp, vc, vp, bs, vec],
        out_specs=[pl.BlockSpec((2 * WINDOW, BRANCH), lambda i: (i, 0)),
                   pl.BlockSpec((2 * WINDOW, 128), lambda i: (i, 0))],
        out_shape=[jax.ShapeDtypeStruct((T, BRANCH), BF16), jax.ShapeDtypeStruct((T, 128), F32)],
        compiler_params=_cp(("parallel",)),
    )(pm, pm, pm, pm, pm, bias, sink)


def _swa4_bwd(pm, bias, sink, do, mlse, dproj, name):
    T = pm.shape[0]
    nb = T // (2 * WINDOW)

    def fold(zz):
        return zz + pltpu.roll(zz, HEAD_DIM, 1)

    def body(q_ref, kc_ref, kp_ref, vc_ref, vp_ref, b_ref, s_ref, do_ref, m_ref, buf_ref,
             dq_ref, dkc_ref, dkp_ref, dvc_ref, dvp_ref, db_ref, ds_ref):
        del buf_ref
        i = pl.program_id(0)

        @pl.when(i == 0)
        def _():
            db_ref[...] = jnp.zeros_like(db_ref)
            ds_ref[...] = jnp.zeros_like(ds_ref)

        lo = _lane_lo((WINDOW, 128))
        lo2 = _lane_lo((2 * WINDOW, 128))
        sink_v = s_ref[...]
        dsink = jnp.zeros((1, 128), F32)
        for u in range(2):
            rows = slice(WINDOW * u, WINDOW * (u + 1))
            mask = _swa4_mask(i == 0 if u == 0 else False)
            kcur, vcur = kc_ref[rows, :], vc_ref[rows, :]
            kprev = kp_ref[...] if u == 0 else kc_ref[0:WINDOW, :]
            vprev = vp_ref[...] if u == 0 else vc_ref[0:WINDOW, :]
            kcA, kcB, kcD = _kv_variants(kcur)
            kpA, kpB, kpD = _kv_variants(kprev)
            vcA, vcB, _ = _kv_variants(vcur)
            vpA, vpB, _ = _kv_variants(vprev)
            mv = m_ref[rows, :]
            zks, zvs = [], []
            for g in range(2):
                qraw = _swa4_stack(q_ref, rows, g)
                qg = qraw * ATT_SCALE
                dog = _swa4_stack(do_ref, rows, g)
                kband_d = jnp.concatenate([kpD[g], kcD[g]], axis=0)
                dqs, mks, mvs = [], [], []
                for par in range(2):
                    s = 2 * g + par
                    kband = jnp.concatenate([(kpA, kpB)[par][g], (kcA, kcB)[par][g]], axis=0)
                    vband = jnp.concatenate([(vpA, vpB)[par][g], (vcA, vcB)[par][g]], axis=0)
                    sc = jnp.where(mask, _dot_nt(qg, kband) + b_ref[s], NEG)
                    h0, h1 = _stacked_head(s, 0), _stacked_head(s, 1)
                    m_c = jnp.concatenate([mv[:, h0:h0 + 1], mv[:, h1:h1 + 1]], axis=0)
                    pr = jnp.exp(sc - m_c)
                    psink = jnp.exp(_swa2_cols(sink_v, s) - m_c)
                    dp = _dot_nt(dog, vband)
                    delta = jnp.sum(pr * dp, axis=-1, keepdims=True)
                    dsc = pr * (dp - delta)
                    db_ref[s] += dsc
                    sd = psink * delta
                    dsink = dsink - _put_col((1, 128), h0, jnp.sum(sd[:WINDOW], keepdims=True))
                    dsink = dsink - _put_col((1, 128), h1, jnp.sum(sd[WINDOW:], keepdims=True))
                    dsb = dsc.astype(BF16)
                    dqs.append(_dot_nn(dsb, kband_d))
                    mks.append(_dot_tn(dsb, qraw))
                    mvs.append(_dot_tn(pr.astype(BF16), dog))
                for r in range(2):
                    sl = slice(256 * g + 128 * r, 256 * g + 128 * r + 128)
                    dq_ref[rows, sl] = (jnp.where(lo, dqs[0][WINDOW * r:WINDOW * (r + 1)],
                                                  dqs[1][WINDOW * r:WINDOW * (r + 1)]) * ATT_SCALE).astype(BF16)
                zks.append(fold(jnp.where(lo2, mks[0], mks[1])))
                zvs.append(fold(jnp.where(lo2, mvs[0], mvs[1])))
            dk = jnp.where(lo2, zks[0], zks[1]) * ATT_SCALE
            dv = jnp.where(lo2, zvs[0], zvs[1])
            dkp_ref[rows, :] = dk[:WINDOW]
            dkc_ref[rows, :] = dk[WINDOW:]
            dvp_ref[rows, :] = dv[:WINDOW]
            dvc_ref[rows, :] = dv[WINDOW:]
        ds_ref[...] += dsink

    q, kc, kp, vc, vp, bs, vec = _swa4_specs()
    own = pl.BlockSpec((2 * WINDOW, BRANCH), lambda i: (i, 0))
    sm = pl.BlockSpec((2 * WINDOW, 128), lambda i: (i, 0))
    f128 = jax.ShapeDtypeStruct((T, 128), F32)
    return pl.pallas_call(
        body, name=name, grid=(nb,),
        in_specs=[q, kc, kp, vc, vp, bs, vec, own, sm, pl.BlockSpec(memory_space=pl.ANY)],
        out_specs=[pl.BlockSpec((2 * WINDOW, BRANCH), lambda i: (i, CB_SQ)), sm, sm, sm, sm, bs, vec],
        out_shape=[jax.ShapeDtypeStruct(dproj.shape, dproj.dtype), f128, f128, f128, f128,
                   jax.ShapeDtypeStruct((4, 2 * WINDOW, 2 * WINDOW), F32), jax.ShapeDtypeStruct((1, 128), F32)],
        input_output_aliases={9: 0},
        compiler_params=_cp(("arbitrary",)),
    )(pm, pm, pm, pm, pm, bias, sink, do, mlse, dproj)


SWA_ROWS = 32


def _swa3_fwd(pm, bias, sink, name):
    T = pm.shape[0]
    nb = T // WINDOW
    R = SWA_ROWS

    def body(q_ref, kc_ref, kp_ref, vc_ref, vp_ref, b_ref, s_ref, o_ref, m_ref, s_scr, p_scr):
        first = pl.program_id(0) == 0
        kill = (lax.broadcasted_iota(I32, (R, 2 * WINDOW), 1) < WINDOW) & first
        kcA, kcB, _ = _kv_variants(kc_ref[...])
        kpA, kpB, _ = _kv_variants(kp_ref[...])
        _, _, vcD = _kv_variants(vc_ref[...])
        _, _, vpD = _kv_variants(vp_ref[...])
        lo = _lane_lo((WINDOW, 128))
        sink_v = s_ref[...]
        m_ref[...] = jnp.zeros_like(m_ref)
        for g in range(2):
            qg = _swa2_stack(q_ref, g) * ATT_SCALE
            vband = jnp.concatenate([vpD[g], vcD[g]], axis=0)
            outs = []
            for par in range(2):
                s = 2 * g + par
                kband = jnp.concatenate([(kpA, kpB)[par][g], (kcA, kcB)[par][g]], axis=0)
                s_scr[s] = _dot_nt(qg, kband)
                for c in range(2 * WINDOW // R):
                    rows = slice(c * R, (c + 1) * R)
                    h = _stacked_head(s, c * R // WINDOW)
                    loc = slice(c * R % WINDOW, c * R % WINDOW + R)
                    sc = jnp.where(kill, NEG, s_scr[s, rows, :] + b_ref[s, rows, :])
                    skv = sink_v[:, h:h + 1]
                    m = jnp.maximum(jnp.max(sc, axis=-1, keepdims=True), skv)
                    e = jnp.exp(sc - m)
                    den = jnp.sum(e, axis=-1, keepdims=True) + jnp.exp(skv - m)
                    p_scr[s, rows, :] = (e * (1.0 / den)).astype(BF16)
                    m_ref[loc, h:h + 1] = m + jnp.log(den)
                outs.append(_dot_nn(p_scr[s], vband))
            for r in range(2):
                sl = slice(256 * g + 128 * r, 256 * g + 128 * r + 128)
                o_ref[:, sl] = jnp.where(lo, outs[0][WINDOW * r:WINDOW * (r + 1)],
                                         outs[1][WINDOW * r:WINDOW * (r + 1)]).astype(BF16)

    q, kc, kp, vc, vp, bs, vec = _swa2_specs()
    tile = (4, 2 * WINDOW, 2 * WINDOW)
    return pl.pallas_call(
        body, name=name, grid=(nb,),
        in_specs=[q, kc, kp, vc, vp, bs, vec],
        out_specs=[pl.BlockSpec((WINDOW, BRANCH), lambda i: (i, 0)),
                   pl.BlockSpec((WINDOW, 128), lambda i: (i, 0))],
        out_shape=[jax.ShapeDtypeStruct((T, BRANCH), BF16), jax.ShapeDtypeStruct((T, 128), F32)],
        scratch_shapes=[pltpu.VMEM(tile, F32), pltpu.VMEM(tile, BF16)],
        compiler_params=_cp(("parallel",)),
    )(pm, pm, pm, pm, pm, bias, sink)


def _swa3_bwd(pm, bias, sink, do, mlse, name):
    T = pm.shape[0]
    nb = T // WINDOW
    R = SWA_ROWS

    def fold(zz):
        return zz + pltpu.roll(zz, HEAD_DIM, 1)

    def body(q_ref, kc_ref, kp_ref, vc_ref, vp_ref, b_ref, s_ref, do_ref, m_ref,
             dq_ref, dkc_ref, dkp_ref, dvc_ref, dvp_ref, db_ref, ds_ref, s_scr, dp_scr, p_scr, ds_scr):
        first = pl.program_id(0) == 0

        @pl.when(first)
        def _():
            db_ref[...] = jnp.zeros_like(db_ref)
            ds_ref[...] = jnp.zeros_like(ds_ref)

        kill = (lax.broadcasted_iota(I32, (R, 2 * WINDOW), 1) < WINDOW) & first
        kcA, kcB, kcD = _kv_variants(kc_ref[...])
        kpA, kpB, kpD = _kv_variants(kp_ref[...])
        vcA, vcB, _ = _kv_variants(vc_ref[...])
        vpA, vpB, _ = _kv_variants(vp_ref[...])
        lo = _lane_lo((WINDOW, 128))
        lo2 = _lane_lo((2 * WINDOW, 128))
        sink_v = s_ref[...]
        dsink = [jnp.zeros((1, 1), F32) for _ in range(8)]
        zks, zvs = [], []
        for g in range(2):
            qraw = _swa2_stack(q_ref, g)
            qg = qraw * ATT_SCALE
            dog = _swa2_stack(do_ref, g)
            kband_d = jnp.concatenate([kpD[g], kcD[g]], axis=0)
            dqs, mks, mvs = [], [], []
            for par in range(2):
                s = 2 * g + par
                kband = jnp.concatenate([(kpA, kpB)[par][g], (kcA, kcB)[par][g]], axis=0)
                vband = jnp.concatenate([(vpA, vpB)[par][g], (vcA, vcB)[par][g]], axis=0)
                s_scr[s] = _dot_nt(qg, kband)
                dp_scr[s] = _dot_nt(dog, vband)
                for c in range(2 * WINDOW // R):
                    rows = slice(c * R, (c + 1) * R)
                    h = _stacked_head(s, c * R // WINDOW)
                    loc = slice(c * R % WINDOW, c * R % WINDOW + R)
                    sc = jnp.where(kill, NEG, s_scr[s, rows, :] + b_ref[s, rows, :])
                    m_c = m_ref[loc, h:h + 1]
                    pr = jnp.exp(sc - m_c)
                    dp = dp_scr[s, rows, :]
                    delta = jnp.sum(pr * dp, axis=-1, keepdims=True)
                    dsc = pr * (dp - delta)
                    db_ref[s, rows, :] += dsc
                    ds_scr[s, rows, :] = dsc.astype(BF16)
                    p_scr[s, rows, :] = pr.astype(BF16)
                    dsink[h] = dsink[h] - jnp.sum(jnp.exp(sink_v[:, h:h + 1] - m_c) * delta, keepdims=True)
                dqs.append(_dot_nn(ds_scr[s], kband_d))
                mks.append(_dot_tn(ds_scr[s], qraw))
                mvs.append(_dot_tn(p_scr[s], dog))
            for r in range(2):
                sl = slice(256 * g + 128 * r, 256 * g + 128 * r + 128)
                dq_ref[:, sl] = (jnp.where(lo, dqs[0][WINDOW * r:WINDOW * (r + 1)],
                                           dqs[1][WINDOW * r:WINDOW * (r + 1)]) * ATT_SCALE).astype(BF16)
            zks.append(fold(jnp.where(lo2, mks[0], mks[1])))
            zvs.append(fold(jnp.where(lo2, mvs[0], mvs[1])))
        dk = jnp.where(lo2, zks[0], zks[1]) * ATT_SCALE
        dv = jnp.where(lo2, zvs[0], zvs[1])
        dkp_ref[...] = dk[:WINDOW]
        dkc_ref[...] = dk[WINDOW:]
        dvp_ref[...] = dv[:WINDOW]
        dvc_ref[...] = dv[WINDOW:]
        tot = jnp.zeros((1, 128), F32)
        for h in range(8):
            tot = tot + _put_col((1, 128), h, dsink[h])
        ds_ref[...] += tot

    q, kc, kp, vc, vp, bs, vec = _swa2_specs()
    own = pl.BlockSpec((WINDOW, BRANCH), lambda i: (i, 0))
    sm = pl.BlockSpec((WINDOW, 128), lambda i: (i, 0))
    f128 = jax.ShapeDtypeStruct((T, 128), F32)
    tile = (4, 2 * WINDOW, 2 * WINDOW)
    return pl.pallas_call(
        body, name=name, grid=(nb,),
        in_specs=[q, kc, kp, vc, vp, bs, vec, own, sm],
        out_specs=[own, sm, sm, sm, sm, bs, vec],
        out_shape=[jax.ShapeDtypeStruct((T, BRANCH), BF16), f128, f128, f128, f128,
                   jax.ShapeDtypeStruct(tile, F32), jax.ShapeDtypeStruct((1, 128), F32)],
        scratch_shapes=[pltpu.VMEM(tile, F32), pltpu.VMEM(tile, F32), pltpu.VMEM(tile, BF16), pltpu.VMEM(tile, BF16)],
        compiler_params=_cp(("arbitrary",)),
    )(pm, pm, pm, pm, pm, bias, sink, do, mlse)


def _swa2_bwd(pm, bias, sink, do, mlse, name):
    T = pm.shape[0]
    nb = T // WINDOW

    def fold(zz):
        return zz + pltpu.roll(zz, HEAD_DIM, 1)

    def body(q_ref, kc_ref, kp_ref, vc_ref, vp_ref, b_ref, s_ref, do_ref, m_ref,
             dq_ref, dkc_ref, dkp_ref, dvc_ref, dvp_ref, db_ref, ds_ref):
        i = pl.program_id(0)

        @pl.when(i == 0)
        def _():
            db_ref[...] = jnp.zeros_like(db_ref)
            ds_ref[...] = jnp.zeros_like(ds_ref)

        mask = _swa2_mask(i)
        kcA, kcB, kcD = _kv_variants(kc_ref[...])
        kpA, kpB, kpD = _kv_variants(kp_ref[...])
        vcA, vcB, _ = _kv_variants(vc_ref[...])
        vpA, vpB, _ = _kv_variants(vp_ref[...])
        lo = _lane_lo((WINDOW, 128))
        lo2 = _lane_lo((2 * WINDOW, 128))
        sink_v = s_ref[...]
        mv = m_ref[...]
        dsink = jnp.zeros((1, 128), F32)
        zks, zvs = [], []
        for g in range(2):
            qraw = _swa2_stack(q_ref, g)
            qg = qraw * ATT_SCALE
            dog = _swa2_stack(do_ref, g)
            kband_d = jnp.concatenate([kpD[g], kcD[g]], axis=0)
            dqs, mks, mvs = [], [], []
            for par in range(2):
                s = 2 * g + par
                kband = jnp.concatenate([(kpA, kpB)[par][g], (kcA, kcB)[par][g]], axis=0)
                vband = jnp.concatenate([(vpA, vpB)[par][g], (vcA, vcB)[par][g]], axis=0)
                sc = jnp.where(mask, _dot_nt(qg, kband) + b_ref[s], NEG)
                h0, h1 = _stacked_head(s, 0), _stacked_head(s, 1)
                m_c = jnp.concatenate([mv[:, h0:h0 + 1], mv[:, h1:h1 + 1]], axis=0)
                pr = jnp.exp(sc - m_c)
                psink = jnp.exp(_swa2_cols(sink_v, s) - m_c)
                dp = _dot_nt(dog, vband)
                delta = jnp.sum(pr * dp, axis=-1, keepdims=True)
                dsc = pr * (dp - delta)
                db_ref[s] += dsc
                sd = psink * delta
                dsink = dsink - _put_col((1, 128), _stacked_head(s, 0), jnp.sum(sd[:WINDOW], keepdims=True))
                dsink = dsink - _put_col((1, 128), _stacked_head(s, 1), jnp.sum(sd[WINDOW:], keepdims=True))
                dsb = dsc.astype(BF16)
                dqs.append(_dot_nn(dsb, kband_d))
                mks.append(_dot_tn(dsb, qraw))
                mvs.append(_dot_tn(pr.astype(BF16), dog))
            for r in range(2):
                sl = slice(256 * g + 128 * r, 256 * g + 128 * r + 128)
                dq_ref[:, sl] = (jnp.where(lo, dqs[0][WINDOW * r:WINDOW * (r + 1)],
                                           dqs[1][WINDOW * r:WINDOW * (r + 1)]) * ATT_SCALE).astype(BF16)
            zks.append(fold(jnp.where(lo2, mks[0], mks[1])))
            zvs.append(fold(jnp.where(lo2, mvs[0], mvs[1])))
        dk = jnp.where(lo2, zks[0], zks[1]) * ATT_SCALE
        dv = jnp.where(lo2, zvs[0], zvs[1])
        dkp_ref[...] = dk[:WINDOW]
        dkc_ref[...] = dk[WINDOW:]
        dvp_ref[...] = dv[:WINDOW]
        dvc_ref[...] = dv[WINDOW:]
        ds_ref[...] += dsink

    q, kc, kp, vc, vp, bs, vec = _swa2_specs()
    own = pl.BlockSpec((WINDOW, BRANCH), lambda i: (i, 0))
    sm = pl.BlockSpec((WINDOW, 128), lambda i: (i, 0))
    f128 = jax.ShapeDtypeStruct((T, 128), F32)
    return pl.pallas_call(
        body, name=name, grid=(nb,),
        in_specs=[q, kc, kp, vc, vp, bs, vec, own, sm],
        out_specs=[own, sm, sm, sm, sm, bs, vec],
        out_shape=[jax.ShapeDtypeStruct((T, BRANCH), BF16), f128, f128, f128, f128,
                   jax.ShapeDtypeStruct((4, 2 * WINDOW, 2 * WINDOW), F32), jax.ShapeDtypeStruct((1, 128), F32)],
        compiler_params=_cp(("arbitrary",)),
    )(pm, pm, pm, pm, pm, bias, sink, do, mlse)


def _merge_fwd(pm, us, name):
    T = pm.shape[0]
    bt = _pick(T, (512, 256))

    def body(g0, g1, g2, u0, u1, u2, o_ref):
        acc = jax.nn.sigmoid(g0[...].astype(F32)) * u0[...].astype(F32)
        acc = acc + jax.nn.sigmoid(g1[...].astype(F32)) * u1[...].astype(F32)
        acc = acc + jax.nn.sigmoid(g2[...].astype(F32)) * u2[...].astype(F32)
        o_ref[...] = acc.astype(BF16)

    own = pl.BlockSpec((bt, D_MODEL), lambda i: (i, 0))
    gs = [pl.BlockSpec((bt, D_MODEL), lambda i, cb=cb: (i, cb)) for cb in CB_GATE]
    return pl.pallas_call(
        body, name=name, grid=(T // bt,), in_specs=gs + [own, own, own], out_specs=own,
        out_shape=jax.ShapeDtypeStruct((T, D_MODEL), BF16),
        compiler_params=_cp(("parallel",)),
    )(pm, pm, pm, *us)


def _merge_bwd(pm, us, dm, name):
    T = pm.shape[0]
    bt = _pick(T, (256,))

    def body(g0, g1, g2, u0, u1, u2, dm_ref, du0, du1, du2, dg_ref):
        dmv = dm_ref[...].astype(F32)
        for b, (g, u, du) in enumerate(((g0, u0, du0), (g1, u1, du1), (g2, u2, du2))):
            s = jax.nn.sigmoid(g[...].astype(F32))
            du[...] = (dmv * s).astype(BF16)
            dg_ref[:, D_MODEL * b:D_MODEL * (b + 1)] = (dmv * u[...].astype(F32) * s * (1.0 - s)).astype(BF16)

    own = pl.BlockSpec((bt, D_MODEL), lambda i: (i, 0))
    gs = [pl.BlockSpec((bt, D_MODEL), lambda i, cb=cb: (i, cb)) for cb in CB_GATE]
    act = jax.ShapeDtypeStruct((T, D_MODEL), BF16)
    return pl.pallas_call(
        body, name=name, grid=(T // bt,), in_specs=gs + [own, own, own, own],
        out_specs=[own, own, own, pl.BlockSpec((bt, 3 * D_MODEL), lambda i: (i, 0))],
        out_shape=[act, act, act, jax.ShapeDtypeStruct((T, PROJ_PAD), BF16)],
        compiler_params=_cp(("parallel",)),
    )(pm, pm, pm, *us, dm)


def _swiglu_fwd(ab, name):
    T = ab.shape[0]
    bt = _pick(T, (512, 256))

    def body(a_ref, b_ref, o_ref):
        a = a_ref[...].astype(F32)
        o_ref[...] = (a * jax.nn.sigmoid(a) * b_ref[...].astype(F32)).astype(BF16)

    return pl.pallas_call(
        body, name=name, grid=(T // bt,),
        in_specs=[pl.BlockSpec((bt, D_FF), lambda i: (i, 0)), pl.BlockSpec((bt, D_FF), lambda i: (i, 1))],
        out_specs=pl.BlockSpec((bt, D_FF), lambda i: (i, 0)),
        out_shape=jax.ShapeDtypeStruct((T, D_FF), BF16),
        compiler_params=_cp(("parallel",)),
    )(ab, ab)


def _swiglu_bwd(ab, dh, name):
    T = ab.shape[0]
    bt = _pick(T, (256,))

    def body(a_ref, b_ref, d_ref, o_ref):
        a = a_ref[...].astype(F32)
        b = b_ref[...].astype(F32)
        d = d_ref[...].astype(F32)
        s = jax.nn.sigmoid(a)
        o_ref[:, 0:D_FF] = (d * b * (s + a * s * (1.0 - s))).astype(BF16)
        o_ref[:, D_FF:2 * D_FF] = (d * a * s).astype(BF16)

    return pl.pallas_call(
        body, name=name, grid=(T // bt,),
        in_specs=[pl.BlockSpec((bt, D_FF), lambda i: (i, 0)), pl.BlockSpec((bt, D_FF), lambda i: (i, 1)),
                  pl.BlockSpec((bt, D_FF), lambda i: (i, 0))],
        out_specs=pl.BlockSpec((bt, 2 * D_FF), lambda i: (i, 0)),
        out_shape=jax.ShapeDtypeStruct((T, 2 * D_FF), BF16),
        compiler_params=_cp(("parallel",)),
    )(ab, ab, dh)


def _xattn_probs(q_ref, kv_ref, h):
    sl = slice(X_HEAD_DIM * h, X_HEAD_DIM * (h + 1))
    qh = q_ref[:, sl]
    kh = kv_ref[:, sl]
    vh = kv_ref[:, D_MODEL + X_HEAD_DIM * h:D_MODEL + X_HEAD_DIM * (h + 1)]
    s = _dot_nt(qh, kh) * X_SCALE
    e = jnp.exp(s - jnp.max(s, axis=-1, keepdims=True))
    return qh, kh, vh, e * (1.0 / jnp.sum(e, axis=-1, keepdims=True))


def _xattn_fwd(q, kv, name):
    T = q.shape[0]
    bq = _pick(T, (512, 256))

    def body(q_ref, kv_ref, o_ref):
        for h in range(X_HEADS):
            _, _, vh, p = _xattn_probs(q_ref, kv_ref, h)
            o_ref[:, X_HEAD_DIM * h:X_HEAD_DIM * (h + 1)] = _dot_nn(p.astype(BF16), vh).astype(BF16)

    own = pl.BlockSpec((bq, D_MODEL), lambda i: (i, 0))
    return pl.pallas_call(
        body, name=name, grid=(T // bq,),
        in_specs=[own, pl.BlockSpec((MEM_LEN, 2 * D_MODEL), lambda i: (0, 0))], out_specs=own,
        out_shape=jax.ShapeDtypeStruct((T, D_MODEL), BF16),
        compiler_params=_cp(("parallel",)),
    )(q, kv)


def _xattn_bwd(q, kv, do, name):
    T = q.shape[0]
    bq = _pick(T, (512, 256))

    def body(q_ref, kv_ref, do_ref, dq_ref, dkv_ref):
        @pl.when(pl.program_id(0) == 0)
        def _():
            dkv_ref[...] = jnp.zeros_like(dkv_ref)

        for h in range(X_HEADS):
            sl = slice(X_HEAD_DIM * h, X_HEAD_DIM * (h + 1))
            qh, kh, vh, p = _xattn_probs(q_ref, kv_ref, h)
            doh = do_ref[:, sl]
            dp = _dot_nt(doh, vh)
            ds = (p * (dp - jnp.sum(p * dp, axis=-1, keepdims=True)) * X_SCALE).astype(BF16)
            dq_ref[:, sl] = _dot_nn(ds, kh).astype(BF16)
            dkv_ref[:, sl] += _dot_tn(ds, qh)
            dkv_ref[:, D_MODEL + X_HEAD_DIM * h:D_MODEL + X_HEAD_DIM * (h + 1)] += _dot_tn(p.astype(BF16), doh)

    own = pl.BlockSpec((bq, D_MODEL), lambda i: (i, 0))
    kvs = pl.BlockSpec((MEM_LEN, 2 * D_MODEL), lambda i: (0, 0))
    return pl.pallas_call(
        body, name=name, grid=(T // bq,), in_specs=[own, kvs, own], out_specs=[own, kvs],
        out_shape=[jax.ShapeDtypeStruct((T, D_MODEL), BF16), jax.ShapeDtypeStruct((MEM_LEN, 2 * D_MODEL), F32)],
        compiler_params=_cp(("arbitrary",)),
    )(q, kv, do)


def _adamw(w, g, m, v, name):
    R, C = w.shape
    cpad = -(-C // 128) * 128
    bt = R
    for cand in (1024, 512, 256, 128, 64, 32, 16, 8):
        if R % cand == 0 and cand * cpad * 4 <= (1 << 20):
            bt = cand
            break

    def body(w_ref, g_ref, m_ref, v_ref, d_ref, nm_ref, nv_ref):
        gv = g_ref[...]
        mn = ADAM_B1 * m_ref[...] + (1.0 - ADAM_B1) * gv
        vn = ADAM_B2 * v_ref[...] + (1.0 - ADAM_B2) * (gv * gv)
        m_hat = mn / (1.0 - ADAM_B1 ** ADAM_STEP)
        v_hat = vn / (1.0 - ADAM_B2 ** ADAM_STEP)
        d_ref[...] = -ADAM_LR * (m_hat / (jnp.sqrt(v_hat) + ADAM_EPS) + ADAM_WD * w_ref[...])
        nm_ref[...] = mn
        nv_ref[...] = vn

    blk = pl.BlockSpec((bt, C), lambda i: (i, 0))
    out = jax.ShapeDtypeStruct((R, C), F32)
    return pl.pallas_call(
        body, name=name, grid=(R // bt,), in_specs=[blk] * 4, out_specs=[blk] * 3,
        out_shape=[out, out, out], compiler_params=_cp(("parallel",)),
    )(w, g, m, v)


ANY = pl.BlockSpec(memory_space=pl.ANY)


def _place():
    x, y, c = lax.axis_index("x"), lax.axis_index("y"), lax.axis_index("c")
    chips = [(1 - x, y), (x, 1 - y), (1 - x, 1 - y)]
    return x, y, c, chips


def _ag_packs(pack):
    R, Wd = pack.shape
    hrows = R // 2

    def body(p_ref, o_ref, send_sems, recv_sems, local_sem):
        x, y, c, chips = _place()
        me = 2 * x + y
        mine = pl.ds(c * hrows, hrows)
        theirs = pl.ds((1 - c) * hrows, hrows)
        local = pltpu.make_async_copy(p_ref, o_ref.at[me], local_sem)
        local.start()

        def copy(k, slab, rows, to, src=None):
            dst = o_ref.at[slab, rows]
            return pltpu.make_async_remote_copy(
                src_ref=dst if src is None else src, dst_ref=dst,
                send_sem=send_sems.at[k], recv_sem=recv_sems.at[k], device_id=to, device_id_type=MESH)

        first = [copy(k, me, mine, (px, py, c), src=p_ref.at[mine]) for k, (px, py) in enumerate(chips)]
        for cp in first:
            cp.start()
        passed = [copy(3 + k, 2 * px + py, mine, (x, y, 1 - c)) for k, (px, py) in enumerate(chips)]
        for k, (px, py) in enumerate(chips):
            copy(k, 2 * px + py, mine, (x, y, c)).wait_recv()
            passed[k].start()
        for k, (px, py) in enumerate(chips):
            copy(3 + k, 2 * px + py, theirs, (x, y, c)).wait_recv()
        for cp in first + passed:
            cp.wait_send()
        local.wait()

    return pl.pallas_call(
        body, name="ag_weights", in_specs=[ANY], out_specs=ANY,
        out_shape=jax.ShapeDtypeStruct((4, R, Wd), pack.dtype),
        scratch_shapes=[pltpu.SemaphoreType.DMA((6,)), pltpu.SemaphoreType.DMA((6,)), pltpu.SemaphoreType.DMA],
    )(pack)


def _rs_sibling(g4):
    _, R, Wd = g4.shape
    hrows = R // 2

    def body(g_ref, o_ref, send_sem, recv_sem):
        x, y, c, _ = _place()
        cp = pltpu.make_async_remote_copy(
            src_ref=g_ref.at[:, pl.ds((1 - c) * hrows, hrows)], dst_ref=o_ref,
            send_sem=send_sem, recv_sem=recv_sem, device_id=(x, y, 1 - c), device_id_type=MESH)
        cp.start()
        cp.wait()

    return pl.pallas_call(
        body, name="rs_sibling", in_specs=[ANY], out_specs=ANY,
        out_shape=jax.ShapeDtypeStruct((4, hrows, Wd), g4.dtype),
        scratch_shapes=[pltpu.SemaphoreType.DMA, pltpu.SemaphoreType.DMA],
    )(g4)


def _rs_add_pair(g4, sib, cidx, tag=""):
    _, R, Wd = g4.shape
    hrows = R // 2
    bt = _pick(hrows, ROW_BLOCKS)
    nb = hrows // bt

    def body(c_ref, a_ref, b_ref, o_ref):
        o_ref[...] = (a_ref[...].astype(F32) + b_ref[...].astype(F32)).astype(o_ref.dtype)

    grid_spec = pltpu.PrefetchScalarGridSpec(
        num_scalar_prefetch=1, grid=(4, nb),
        in_specs=[pl.BlockSpec((1, bt, Wd), lambda j, i, c: (j, c[0] * nb + i, 0)),
                  pl.BlockSpec((1, bt, Wd), lambda j, i, c: (j, i, 0))],
        out_specs=pl.BlockSpec((1, bt, Wd), lambda j, i, c: (j, i, 0)))
    return pl.pallas_call(
        body, name="rs_add_pair" + tag, grid_spec=grid_spec,
        out_shape=jax.ShapeDtypeStruct((4, hrows, Wd), g4.dtype),
        compiler_params=_cp(("parallel", "parallel")),
    )(cidx, g4, sib)


def _rs_chips(r4):
    _, hrows, Wd = r4.shape

    def body(r_ref, o_ref, send_sems, recv_sems, local_sem):
        x, y, c, chips = _place()
        me = 2 * x + y
        local = pltpu.make_async_copy(r_ref.at[me], o_ref.at[me], local_sem)
        local.start()
        sends = []
        for k, (px, py) in enumerate(chips):
            sends.append(pltpu.make_async_remote_copy(
                src_ref=r_ref.at[2 * px + py], dst_ref=o_ref.at[me],
                send_sem=send_sems.at[k], recv_sem=recv_sems.at[k], device_id=(px, py, c), device_id_type=MESH))
        for cp in sends:
            cp.start()
        for k, (px, py) in enumerate(chips):
            pltpu.make_async_remote_copy(
                src_ref=r_ref.at[me], dst_ref=o_ref.at[2 * px + py],
                send_sem=send_sems.at[k], recv_sem=recv_sems.at[k], device_id=(x, y, c),
                device_id_type=MESH).wait_recv()
        for cp in sends:
            cp.wait_send()
        local.wait()

    return pl.pallas_call(
        body, name="rs_chips", in_specs=[ANY], out_specs=ANY,
        out_shape=jax.ShapeDtypeStruct((4, hrows, Wd), r4.dtype),
        scratch_shapes=[pltpu.SemaphoreType.DMA((3,)), pltpu.SemaphoreType.DMA((3,)), pltpu.SemaphoreType.DMA],
    )(r4)


def _rs_add_chips(q4):
    _, hrows, Wd = q4.shape
    bt = _pick(hrows, (240, 120, 16))

    def body(q_ref, o_ref):
        o_ref[...] = ((q_ref[0].astype(F32) + q_ref[1].astype(F32)) + q_ref[2].astype(F32)) + q_ref[3].astype(F32)

    return pl.pallas_call(
        body, name="rs_add_chips", grid=(hrows // bt,),
        in_specs=[pl.BlockSpec((4, bt, Wd), lambda i: (0, i, 0))],
        out_specs=pl.BlockSpec((bt, Wd), lambda i: (i, 0)),
        out_shape=jax.ShapeDtypeStruct((hrows, Wd), F32),
        compiler_params=_cp(("parallel",)),
    )(q4)


def _rs_share(buf):
    R, Wd = buf.shape
    hrows = R // 2

    def body(b_ref, o_ref, send_sem, recv_sem):
        del b_ref
        x, y, c, _ = _place()
        mine = o_ref.at[pl.ds(c * hrows, hrows)]
        cp = pltpu.make_async_remote_copy(
            src_ref=mine, dst_ref=mine, send_sem=send_sem, recv_sem=recv_sem,
            device_id=(x, y, 1 - c), device_id_type=MESH)
        cp.start()
        theirs = o_ref.at[pl.ds((1 - c) * hrows, hrows)]
        pltpu.make_async_remote_copy(
            src_ref=theirs, dst_ref=theirs, send_sem=send_sem, recv_sem=recv_sem,
            device_id=(x, y, c), device_id_type=MESH).wait_recv()
        cp.wait_send()

    return pl.pallas_call(
        body, name="rs_share", in_specs=[ANY], out_specs=ANY, input_output_aliases={0: 0},
        out_shape=jax.ShapeDtypeStruct((R, Wd), buf.dtype),
        scratch_shapes=[pltpu.SemaphoreType.DMA, pltpu.SemaphoreType.DMA],
    )(buf)


def _allreduce_small(v, name="allreduce_small"):
    R, Wd = v.shape

    def body(v_ref, o_ref, buf, send_sems, recv_sems):
        x, y, c, _ = _place()
        me = 4 * x + 2 * y + c
        buf[me] = v_ref[...]
        sends = []
        for k in range(1, 8):
            peer = ((x + (k >> 2)) % 2, (y + ((k >> 1) & 1)) % 2, (c + (k & 1)) % 2)
            sends.append(pltpu.make_async_remote_copy(
                src_ref=v_ref, dst_ref=buf.at[me], send_sem=send_sems.at[k - 1], recv_sem=recv_sems.at[k - 1],
                device_id=peer, device_id_type=MESH))
        for cp in sends:
            cp.start()
        for k in range(1, 8):
            px, py, pc = (x + (k >> 2)) % 2, (y + ((k >> 1) & 1)) % 2, (c + (k & 1)) % 2
            pltpu.make_async_remote_copy(
                src_ref=v_ref, dst_ref=buf.at[4 * px + 2 * py + pc], send_sem=send_sems.at[k - 1],
                recv_sem=recv_sems.at[k - 1], device_id=(x, y, c), device_id_type=MESH).wait_recv()
        acc = buf[0]
        for d in range(1, 8):
            acc = acc + buf[d]
        o_ref[...] = acc
        for cp in sends:
            cp.wait_send()

    vm = pl.BlockSpec(memory_space=pltpu.VMEM)
    return pl.pallas_call(
        body, name=name, in_specs=[vm], out_specs=vm,
        out_shape=jax.ShapeDtypeStruct((R, Wd), F32),
        scratch_shapes=[pltpu.VMEM((8, R, Wd), F32), pltpu.SemaphoreType.DMA((7,)), pltpu.SemaphoreType.DMA((7,))],
    )(v)


def _neighbours():
    x, y, c = lax.axis_index("x"), lax.axis_index("y"), lax.axis_index("c")
    idx = (2 * x + y, 2 * (1 - x) + y, 2 * x + (1 - y), 2 * (1 - x) + (1 - y))
    return idx, (x, y, c), (1 - x, y, c), (x, 1 - y, c), (x, y, 1 - c)


def _place_own(pack, me_idx):
    R, Wd = pack.shape
    bt = _pick(R, (512, 256))

    def body(i_ref, p_ref, o_ref):
        o_ref[0] = p_ref[...]

    grid_spec = pltpu.PrefetchScalarGridSpec(
        num_scalar_prefetch=1, grid=(R // bt,),
        in_specs=[pl.BlockSpec((bt, Wd), lambda i, idx: (i, 0))],
        out_specs=pl.BlockSpec((1, bt, Wd), lambda i, idx: (idx[0], i, 0)))
    return pl.pallas_call(
        body, name="place_own", grid_spec=grid_spec,
        out_shape=jax.ShapeDtypeStruct((4, R, Wd), pack.dtype),
        compiler_params=_cp(("parallel",)),
    )(me_idx, pack)


def _ag_ring(buf):
    _, R, Wd = buf.shape
    hrows = R // 2
    qrows = hrows // 2

    def body(b_ref, o_ref, send_sems, recv_sems):
        del b_ref
        (me, ix, iy, idg), here, xn, yn, sib = _neighbours()
        c = here[2]
        base = c * hrows
        half = pl.ds(base, hrows)
        q0 = pl.ds(base, qrows)
        q1 = pl.ds(base + qrows, qrows)
        obase = (1 - c) * hrows

        def copy(k, slab, rows, to):
            dst = o_ref.at[slab, rows]
            return pltpu.make_async_remote_copy(
                src_ref=dst, dst_ref=dst,
                send_sem=send_sems.at[k], recv_sem=recv_sems.at[k], device_id=to, device_id_type=MESH)

        sends = [copy(0, me, half, xn), copy(1, me, half, yn)]
        for cp in sends:
            cp.start()
        landed = [(0, ix, half), (1, iy, half), (2, idg, q0), (3, idg, q1)]
        onward = {0: copy(2, ix, q0, yn), 1: copy(3, iy, q1, xn)}
        for k, slab, rows in landed:
            copy(k, slab, rows, here).wait_recv()
            if k in onward:
                onward[k].start()
                sends.append(onward[k])
            cp = copy(4 + k, slab, rows, sib)
            cp.start()
            sends.append(cp)
        theirs = [(4, ix, pl.ds(obase, hrows)), (5, iy, pl.ds(obase, hrows)),
                  (6, idg, pl.ds(obase, qrows)), (7, idg, pl.ds(obase + qrows, qrows))]
        for k, slab, rows in theirs:
            copy(k, slab, rows, here).wait_recv()
        for cp in sends:
            cp.wait_send()

    return pl.pallas_call(
        body, name="ag_weights", in_specs=[ANY], out_specs=ANY, input_output_aliases={0: 0},
        out_shape=jax.ShapeDtypeStruct((4, R, Wd), buf.dtype),
        scratch_shapes=[pltpu.SemaphoreType.DMA((8,)), pltpu.SemaphoreType.DMA((8,))],
    )(buf)


def _rs_diag(r4):
    _, hrows, Wd = r4.shape
    qrows = hrows // 2

    def body(r_ref, o_ref, send_sems, recv_sems):
        (me, ix, iy, idg), here, xn, yn, sib = _neighbours()
        pieces = [(0, pl.ds(0, qrows), xn), (1, pl.ds(qrows, qrows), yn)]
        sends = [pltpu.make_async_remote_copy(
            src_ref=r_ref.at[idg, rows], dst_ref=o_ref.at[k], send_sem=send_sems.at[k],
            recv_sem=recv_sems.at[k], device_id=to, device_id_type=MESH) for k, rows, to in pieces]
        for cp in sends:
            cp.start()
        for k, rows, to in pieces:
            pltpu.make_async_remote_copy(
                src_ref=r_ref.at[idg, rows], dst_ref=o_ref.at[k], send_sem=send_sems.at[k],
                recv_sem=recv_sems.at[k], device_id=here, device_id_type=MESH).wait_recv()
        for cp in sends:
            cp.wait_send()

    return pl.pallas_call(
        body, name="rs_diag", in_specs=[ANY], out_specs=ANY,
        out_shape=jax.ShapeDtypeStruct((2, qrows, Wd), r4.dtype),
        scratch_shapes=[pltpu.SemaphoreType.DMA((2,)), pltpu.SemaphoreType.DMA((2,))],
    )(r4)


def _rs_merge(r4, dg, nbr_idx, tag=""):
    _, hrows, Wd = r4.shape
    bt = _pick(hrows // 2, ROW_BLOCKS)
    nb = hrows // bt
    nq = nb // 2

    def body(i_ref, r_ref, d_ref, o_ref):
        w = pl.program_id(0)
        i = pl.program_id(1)
        merged = jnp.where(w == 0, i >= nq, i < nq)
        add = jnp.where(merged, d_ref[...].astype(F32), 0.0)
        o_ref[...] = (r_ref[...].astype(F32) + add).astype(o_ref.dtype)

    grid_spec = pltpu.PrefetchScalarGridSpec(
        num_scalar_prefetch=1, grid=(2, nb),
        in_specs=[pl.BlockSpec((1, bt, Wd), lambda w, i, idx: (idx[w], i, 0)),
                  pl.BlockSpec((1, bt, Wd), lambda w, i, idx: (1 - w, jnp.clip(i - (1 - w) * nq, 0, nq - 1), 0))],
        out_specs=pl.BlockSpec((1, bt, Wd), lambda w, i, idx: (w, i, 0)))
    return pl.pallas_call(
        body, name="rs_merge" + tag, grid_spec=grid_spec,
        out_shape=jax.ShapeDtypeStruct((2, hrows, Wd), r4.dtype),
        compiler_params=_cp(("parallel", "parallel")),
    )(nbr_idx, r4, dg)


def _rs_direct(m2):
    _, hrows, Wd = m2.shape

    def body(m_ref, o_ref, send_sems, recv_sems):
        _, here, xn, yn, sib = _neighbours()
        sends = [pltpu.make_async_remote_copy(
            src_ref=m_ref.at[k], dst_ref=o_ref.at[k], send_sem=send_sems.at[k], recv_sem=recv_sems.at[k],
            device_id=to, device_id_type=MESH) for k, to in ((0, xn), (1, yn))]
        for cp in sends:
            cp.start()
        for k in range(2):
            pltpu.make_async_remote_copy(
                src_ref=m_ref.at[k], dst_ref=o_ref.at[k], send_sem=send_sems.at[k], recv_sem=recv_sems.at[k],
                device_id=here, device_id_type=MESH).wait_recv()
        for cp in sends:
            cp.wait_send()

    return pl.pallas_call(
        body, name="rs_direct", in_specs=[ANY], out_specs=ANY,
        out_shape=jax.ShapeDtypeStruct((2, hrows, Wd), m2.dtype),
        scratch_shapes=[pltpu.SemaphoreType.DMA((2,)), pltpu.SemaphoreType.DMA((2,))],
    )(m2)


def _rs_final(r4, got, me_c, tag=""):
    _, hrows, Wd = r4.shape
    bt = _pick(hrows, ROW_BLOCKS)
    nb = hrows // bt

    def body(i_ref, r_ref, g_ref, o_ref):
        o_ref[...] = (r_ref[0].astype(F32) + g_ref[0].astype(F32)) + g_ref[1].astype(F32)

    grid_spec = pltpu.PrefetchScalarGridSpec(
        num_scalar_prefetch=1, grid=(nb,),
        in_specs=[pl.BlockSpec((1, bt, Wd), lambda i, idx: (idx[0], i, 0)),
                  pl.BlockSpec((2, bt, Wd), lambda i, idx: (0, i, 0))],
        out_specs=pl.BlockSpec((bt, Wd), lambda i, idx: (idx[1] * nb + i, 0)))
    return pl.pallas_call(
        body, name="rs_final" + tag, grid_spec=grid_spec,
        out_shape=jax.ShapeDtypeStruct((2 * hrows, Wd), F32),
        compiler_params=_cp(("parallel",)),
    )(me_c, r4, got)


SHARDED = (
    ("w_in", (2, 1024, 1730), 2),
    ("w_branch", (2, 3, 512, 256), 3),
    ("w_mix_out", (2, 256, 1024), 1),
    ("w_xq", (2, 256, 1024), 1),
    ("w_xkv", (2, 1024, 512), 2),
    ("w_xo", (2, 256, 1024), 1),
    ("w_ffn_gate", (2, 1024, 704), 2),
    ("w_ffn_up", (2, 1024, 704), 2),
    ("w_ffn_down", (2, 704, 1024), 1),
    ("conv_w", (2, 3, 128), 2),
)
PACK_W = 1024
PACK_ELEMS = sum(int(np.prod(s)) for _, s, _ in SHARDED)
PACK_ROWS = -(-PACK_ELEMS // (PACK_W * 1024)) * 1024


def _pack(parts, dtype):
    flat = jnp.concatenate([p.astype(dtype).reshape(-1) for p in parts]
                           + [jnp.zeros((PACK_ROWS * PACK_W - PACK_ELEMS,), dtype)])
    return flat.reshape(PACK_ROWS, PACK_W)


def _unpack(pack):
    flat = pack.reshape(-1)
    out, off = {}, 0
    for name, shape, _ in SHARDED:
        n = int(np.prod(shape))
        out[name] = flat[off:off + n].reshape(shape)
        off += n
    return out


SMALL = (
    ("mix_norm_g", (2, 1024)), ("xattn_norm_g", (2, 1024)), ("mem_norm_g", (2, 1024)),
    ("ffn_norm_g", (2, 1024)), ("final_norm_g", (1024,)),
    ("forget_bias", (2, 8)), ("sink", (2, 8)), ("rel_bias", (32, 8)),
)
SMALL_AND_CONV = SMALL + (("conv_w", (2, 3, 512)),)


def _small_rows(spec):
    rows = sum(int(np.prod(s)) // 128 if s[-1] % 128 == 0 else s[0] for _, s in spec)
    return -(-rows // 8) * 8


def _pack_small(vals, spec=SMALL):
    rows = []
    for name, shape in spec:
        v = vals[name].astype(F32)
        if shape[-1] % 128 == 0:
            rows.append(v.reshape(-1, 128))
        else:
            rows.append(jnp.pad(v, ((0, 0), (0, 120))))
    rows = jnp.concatenate(rows, axis=0)
    return jnp.pad(rows, ((0, _small_rows(spec) - rows.shape[0]), (0, 0)))


def _unpack_small(pack, spec=SMALL):
    out, off = {}, 0
    for name, shape in spec:
        if shape[-1] % 128 == 0:
            n = int(np.prod(shape)) // 128
            out[name] = pack[off:off + n].reshape(shape)
        else:
            n = shape[0]
            out[name] = pack[off:off + n, 0:8]
        off += n
    return out


W_IN_PERM = ((3848, 6920), (0, 3072), (3080, 3848), (3072, 3080))


def _perm_w_in(w):
    parts = [w[:, a:b] for a, b in W_IN_PERM]
    return jnp.concatenate(parts + [jnp.zeros((w.shape[0], PROJ_PAD - IN_COLS), w.dtype)], axis=1)


def _unperm_w_in(p):
    return jnp.concatenate([p[:, 3072:6144], p[:, 6912:6920], p[:, 6144:6912], p[:, 0:3072]], axis=1)


def _pad_row8(v):
    return jnp.pad(v.astype(F32).reshape(1, 8), ((0, 0), (0, 120)))


def _local_step(x, mem, tgt, W, rel_bias):
    T = x.shape[0]
    bucket = jnp.asarray(_bucket_table())
    bias = _swa2_bias(rel_bias, bucket, "swa_bias")
    saved = []
    for l in range(DEPTH):
        n = "l%d_" % l
        s = {"x0": x}
        wcat = W["w_in_p"][l]
        h = _rms_fwd(x, W["mix_norm_g"][l:l + 1], n + "mix_norm")
        pm = _mm(h, wcat[:, :PROJ_MAIN], "nn", BF16, n + "proj", bn=768)
        fg = _mm(h, wcat[:, PROJ_MAIN:], "nn", F32, n + "proj_fg")
        fb = _pad_row8(W["forget_bias"][l])
        c_col = _fox_gate_fwd(fg, fb, n + "fox_gate")
        c_row = c_col[:, 0:8].T
        cw = jnp.pad(W["conv_w"][l], ((0, 5), (0, 0)))
        y_conv = _conv_fwd(pm, cw, n + "conv")
        y_fox, lse = _fox2_fwd(pm, c_row, n + "fox")
        sink = _pad_row8(W["sink"][l])
        y_swa, mlse = _swa4_fwd(pm, bias, sink, n + "swa")
        ys = (y_conv, y_fox, y_swa)
        us = tuple(_mm(ys[b], W["w_branch"][l][b], "nn", BF16, n + "branch%d" % b) for b in range(3))
        merged = _merge_fwd(pm, us, n + "merge")
        x1 = _mm(merged, W["w_mix_out"][l], "nn", F32, n + "mix_out", res=x)
        xn1 = _rms_fwd(x1, W["xattn_norm_g"][l:l + 1], n + "xattn_norm")
        memn = _rms_fwd(mem, W["mem_norm_g"][l:l + 1], n + "mem_norm")
        qx = _mm(xn1, W["w_xq"][l], "nn", BF16, n + "xq")
        kv = _mm(memn, W["w_xkv"][l], "nn", BF16, n + "xkv")
        ox = _xattn_fwd(qx, kv, n + "xattn")
        x2 = _mm(ox, W["w_xo"][l], "nn", F32, n + "xo", res=x1)
        xn2 = _rms_fwd(x2, W["ffn_norm_g"][l:l + 1], n + "ffn_norm")
        ab = _mm(xn2, W["w_gu"][l], "nn", BF16, n + "ffn_in", bn=512)
        hm = _swiglu_fwd(ab, n + "swiglu")
        x3 = _mm(hm, W["w_ffn_down"][l], "nn", F32, n + "ffn_out", res=x2, bk=1408)
        s.update(h=h, pm=pm, fg=fg, fb=fb, c_col=c_col, c_row=c_row, cw=cw, ys=ys, lse=lse, sink=sink,
                 mlse=mlse, us=us, merged=merged, x1=x1, xn1=xn1, memn=memn, qx=qx, kv=kv, ox=ox,
                 x2=x2, xn2=xn2, ab=ab, hm=hm)
        saved.append(s)
        x = x3

    loss_row, dx, dg_final = _final_loss(x, W["final_norm_g"].reshape(1, D_MODEL), tgt, "final_loss")
    G = {name: [None] * DEPTH for name in
         ("mix_norm_g", "w_in_p", "forget_bias", "conv_w", "sink", "w_branch", "w_mix_out", "xattn_norm_g",
          "mem_norm_g", "w_xq", "w_xkv", "w_xo", "ffn_norm_g", "w_gu", "w_ffn_down")}
    dbias_tot = None
    for l in reversed(range(DEPTH)):
        n = "l%d_" % l
        s = saved[l]
        dhm = _mm(dx, W["w_ffn_down"][l], "nt", BF16, n + "d_hm", bn=1408)
        G["w_ffn_down"][l] = _mm(s["hm"], dx, "tn", BF16, n + "dw_down", bm=1408, bk=1024)
        dab = _swiglu_bwd(s["ab"], dhm, n + "d_swiglu")
        dxn2 = _mm(dab, W["w_gu"][l], "nt", BF16, n + "d_xn2", bk=1408)
        G["w_gu"][l] = _mm(s["xn2"], dab, "tn", BF16, n + "dw_gu", bn=512, bk=2048)
        dx, G["ffn_norm_g"][l] = _rms_bwd(s["x2"], W["ffn_norm_g"][l:l + 1], dxn2, dx, n + "d_ffn_norm")
        dox = _mm(dx, W["w_xo"][l], "nt", BF16, n + "d_ox")
        G["w_xo"][l] = _mm(s["ox"], dx, "tn", BF16, n + "dw_xo", bk=1024)
        dqx, dkv = _xattn_bwd(s["qx"], s["kv"], dox, n + "d_xattn")
        dxn1 = _mm(dqx, W["w_xq"][l], "nt", BF16, n + "d_xn1")
        G["w_xq"][l] = _mm(s["xn1"], dqx, "tn", BF16, n + "dw_xq", bk=2048)
        dmemn = _mm(dkv, W["w_xkv"][l], "nt", BF16, n + "d_memn")
        G["w_xkv"][l] = _mm(s["memn"], dkv, "tn", BF16, n + "dw_xkv")
        _, G["mem_norm_g"][l] = _rms_bwd(mem, W["mem_norm_g"][l:l + 1], dmemn, None, n + "d_mem_norm")
        dx, G["xattn_norm_g"][l] = _rms_bwd(s["x1"], W["xattn_norm_g"][l:l + 1], dxn1, dx, n + "d_xattn_norm")
        dmerged = _mm(dx, W["w_mix_out"][l], "nt", BF16, n + "d_merged")
        G["w_mix_out"][l] = _mm(s["merged"], dx, "tn", BF16, n + "dw_mix_out", bk=1024)
        du0, du1, du2, dproj = _merge_bwd(s["pm"], s["us"], dmerged, n + "d_merge")
        dus = (du0, du1, du2)
        dys = [_mm(dus[b], W["w_branch"][l][b], "nt", BF16, n + "d_y%d" % b) for b in range(3)]
        G["w_branch"][l] = jnp.stack(
            [_mm(s["ys"][b], dus[b], "tn", BF16, n + "dw_branch%d" % b, bk=2048) for b in range(3)])
        dproj, dcw = _conv_bwd(s["pm"], s["cw"], dys[0], dproj, n + "d_conv")
        G["conv_w"][l] = dcw[0:3]
        delta = _fox_delta(s["ys"][1], dys[1], n + "fox_delta")
        dproj, delta = _fox2_bwd_dq(s["pm"], dys[1], s["c_row"], s["lse"], delta, dproj, n + "d_fox_q")
        dproj, dc = _fox2_bwd_dkv(s["pm"], dys[1], s["c_col"], s["lse"][:, 0:8].T, delta[:, 0:8].T, dproj,
                                  n + "d_fox_kv")
        dfg, dfb = _fox_gate_bwd(dc, s["fg"], s["fb"], n + "d_fox_gate")
        G["forget_bias"][l] = dfb[0, 0:8]
        dproj, dkc, dkp, dvc, dvp, dbias, dsink = _swa4_bwd(s["pm"], bias, s["sink"], dys[2], s["mlse"], dproj,
                                                          n + "d_swa")
        G["sink"][l] = dsink[0, 0:8]
        dbias_tot = dbias if dbias_tot is None else dbias_tot + dbias
        zpad = jnp.zeros((WINDOW, 128), F32)
        dsk = dkc + jnp.concatenate([dkp[WINDOW:], zpad], axis=0)
        dsv = dvc + jnp.concatenate([dvp[WINDOW:], zpad], axis=0)
        tail = jnp.concatenate([dsk.astype(BF16), dsv.astype(BF16), dfg.astype(BF16)], axis=1)
        dproj = lax.dynamic_update_slice(dproj, tail, (0, PROJ_MAIN - 256))
        dh = _mm(dproj, W["w_in_p"][l], "nt", BF16, n + "d_h", bk=1408)
        G["w_in_p"][l] = _mm(s["h"], dproj, "tn", BF16, n + "dw_in", bn=640, bk=2048)
        dx, G["mix_norm_g"][l] = _rms_bwd(s["x0"], W["mix_norm_g"][l:l + 1], dh, dx, n + "d_mix_norm")
    drb = _swa2_dbias_reduce(dbias_tot, bucket, "swa_dbias")
    G["rel_bias"] = drb[:, 0:8]
    G["final_norm_g"] = dg_final.reshape(D_MODEL)
    return loss_row, dx, G


BIG = (
    ("w_in", (2048, 1730)), ("w_branch", (3072, 256)), ("w_mix_out", (512, 1024)), ("w_xq", (512, 1024)),
    ("w_xkv", (2048, 512)), ("w_xo", (512, 1024)), ("w_ffn_gate", (2048, 704)), ("w_ffn_up", (2048, 704)),
    ("w_ffn_down", (1408, 1024)),
)
ROW_BLOCKS = (512, 256, 352, 128, 16)


def _cast_place(w, me_idx, name):
    R, Wd = w.shape
    bt = _pick(R, ROW_BLOCKS)

    def body(i_ref, w_ref, o_ref):
        o_ref[0] = w_ref[...].astype(BF16)

    grid_spec = pltpu.PrefetchScalarGridSpec(
        num_scalar_prefetch=1, grid=(R // bt,),
        in_specs=[pl.BlockSpec((bt, Wd), lambda i, idx: (i, 0))],
        out_specs=pl.BlockSpec((1, bt, Wd), lambda i, idx: (idx[0], i, 0)))
    return pl.pallas_call(
        body, name=name, grid_spec=grid_spec, out_shape=jax.ShapeDtypeStruct((4, R, Wd), BF16),
        compiler_params=_cp(("parallel",)),
    )(me_idx, w)


def _remote(src, dst, sems, k, to):
    send_sems, recv_sems = sems
    return pltpu.make_async_remote_copy(src_ref=src, dst_ref=dst, send_sem=send_sems.at[k], recv_sem=recv_sems.at[k],
                                        device_id=to, device_id_type=MESH)


def _ag_ring_multi(bufs):
    n = len(bufs)

    def body(*refs):
        o = refs[n:2 * n]
        sems = refs[2 * n:]
        (me, ix, iy, idg), here, xn, yn, sib = _neighbours()
        c = here[2]

        def piece(t, k, other):
            h = bufs[t].shape[1] // 2
            q = h // 2
            base = ((1 - c) if other else c) * h
            return [(ix, pl.ds(base, h)), (iy, pl.ds(base, h)), (idg, pl.ds(base, q)), (idg, pl.ds(base + q, q))][k]

        def copy(t, k, slab, rows, to):
            ref = o[t].at[slab, rows]
            return _remote(ref, ref, sems, 8 * t + k, to)

        sends = []

        def go(cp):
            cp.start()
            sends.append(cp)

        for t in range(n):
            h = bufs[t].shape[1] // 2
            go(copy(t, 0, me, pl.ds(c * h, h), xn))
            go(copy(t, 1, me, pl.ds(c * h, h), yn))
        for k in range(4):
            for t in range(n):
                slab, rows = piece(t, k, False)
                copy(t, k, slab, rows, here).wait_recv()
                if k == 0:
                    go(copy(t, 2, ix, piece(t, 2, False)[1], yn))
                if k == 1:
                    go(copy(t, 3, iy, piece(t, 3, False)[1], xn))
                go(copy(t, 4 + k, slab, rows, sib))
        for k in range(4):
            for t in range(n):
                slab, rows = piece(t, k, True)
                copy(t, 4 + k, slab, rows, here).wait_recv()
        for cp in sends:
            cp.wait_send()

    return pl.pallas_call(
        body, name="ag_weights", in_specs=[ANY] * n, out_specs=[ANY] * n,
        input_output_aliases={t: t for t in range(n)},
        out_shape=[jax.ShapeDtypeStruct(b.shape, b.dtype) for b in bufs],
        scratch_shapes=[pltpu.SemaphoreType.DMA((8 * n,)), pltpu.SemaphoreType.DMA((8 * n,))],
    )(*bufs)


def _exchange_multi(srcs, out_shapes, plan, name, aliased=False):
    n = len(srcs)

    def body(*refs):
        ins, outs, sems = refs[:n], refs[n:2 * n], refs[2 * n:]
        places = _neighbours()
        here = places[1]
        per = [plan(t, ins[t], outs[t], places) for t in range(n)]
        width = max(len(p) for p in per)
        started = []
        for t in range(n):
            for k, (src, dst, to, land) in enumerate(per[t]):
                cp = _remote(src, dst, sems, width * t + k, to)
                cp.start()
                started.append(cp)
        for t in range(n):
            for k, (src, dst, to, land) in enumerate(per[t]):
                _remote(land, land, sems, width * t + k, here).wait_recv()
        for cp in started:
            cp.wait_send()

    nsem = 2 * n
    return pl.pallas_call(
        body, name=name, in_specs=[ANY] * n, out_specs=[ANY] * n,
        input_output_aliases={t: t for t in range(n)} if aliased else {},
        out_shape=[jax.ShapeDtypeStruct(s, d) for s, d in out_shapes],
        scratch_shapes=[pltpu.SemaphoreType.DMA((nsem,)), pltpu.SemaphoreType.DMA((nsem,))],
    )(*srcs)


def _rs_sibling_multi(gs):
    def plan(t, g, o, places):
        (_, here, _, _, sib) = places
        h = gs[t].shape[1] // 2
        return [(g.at[:, pl.ds((1 - here[2]) * h, h)], o, sib, o)]

    return _exchange_multi(gs, [((4, g.shape[1] // 2, g.shape[2]), g.dtype) for g in gs], plan, "rs_sibling")


def _rs_diag_multi(rs):
    def plan(t, r, o, places):
        ((_, _, _, idg), _, xn, yn, _) = places
        q = rs[t].shape[1] // 2
        return [(r.at[idg, pl.ds(0, q)], o.at[0], xn, o.at[0]), (r.at[idg, pl.ds(q, q)], o.at[1], yn, o.at[1])]

    return _exchange_multi(rs, [((2, r.shape[1] // 2, r.shape[2]), r.dtype) for r in rs], plan, "rs_diag")


def _rs_direct_multi(ms):
    def plan(t, m, o, places):
        (_, _, xn, yn, _) = places
        return [(m.at[0], o.at[0], xn, o.at[0]), (m.at[1], o.at[1], yn, o.at[1])]

    return _exchange_multi(ms, [(m.shape, m.dtype) for m in ms], plan, "rs_direct")


def _rs_share_multi(bufs):
    def plan(t, b, o, places):
        (_, here, _, _, sib) = places
        h = bufs[t].shape[0] // 2
        mine = o.at[pl.ds(here[2] * h, h)]
        return [(mine, mine, sib, o.at[pl.ds((1 - here[2]) * h, h)])]

    return _exchange_multi(bufs, [(b.shape, b.dtype) for b in bufs], plan, "rs_share", aliased=True)


def kernel(x, mem, mix_norm_g, w_in, forget_bias, conv_w, sink, w_branch, w_mix_out, rel_bias, xattn_norm_g, mem_norm_g, w_xq, w_xkv, w_xo, ffn_norm_g, w_ffn_gate, w_ffn_up, w_ffn_down, final_norm_g, loss_target, m_mix_norm_g, m_w_in, m_forget_bias, m_conv_w, m_sink, m_w_branch, m_w_mix_out, m_rel_bias, m_xattn_norm_g, m_mem_norm_g, m_w_xq, m_w_xkv, m_w_xo, m_ffn_norm_g, m_w_ffn_gate, m_w_ffn_up, m_w_ffn_down, m_final_norm_g, v_mix_norm_g, v_w_in, v_forget_bias, v_conv_w, v_sink, v_w_branch, v_w_mix_out, v_rel_bias, v_xattn_norm_g, v_mem_norm_g, v_w_xq, v_w_xkv, v_w_xo, v_ffn_norm_g, v_w_ffn_gate, v_w_ffn_up, v_w_ffn_down, v_final_norm_g):
    order = ("mix_norm_g", "w_in", "forget_bias", "conv_w", "sink", "w_branch", "w_mix_out", "rel_bias",
             "xattn_norm_g", "mem_norm_g", "w_xq", "w_xkv", "w_xo", "ffn_norm_g", "w_ffn_gate", "w_ffn_up",
             "w_ffn_down", "final_norm_g")
    w_sh = dict(zip(order, (mix_norm_g, w_in, forget_bias, conv_w, sink, w_branch, w_mix_out, rel_bias,
                            xattn_norm_g, mem_norm_g, w_xq, w_xkv, w_xo, ffn_norm_g, w_ffn_gate, w_ffn_up,
                            w_ffn_down, final_norm_g)))
    m_sh = dict(zip(order, (m_mix_norm_g, m_w_in, m_forget_bias, m_conv_w, m_sink, m_w_branch, m_w_mix_out,
                            m_rel_bias, m_xattn_norm_g, m_mem_norm_g, m_w_xq, m_w_xkv, m_w_xo, m_ffn_norm_g,
                            m_w_ffn_gate, m_w_ffn_up, m_w_ffn_down, m_final_norm_g)))
    v_sh = dict(zip(order, (v_mix_norm_g, v_w_in, v_forget_bias, v_conv_w, v_sink, v_w_branch, v_w_mix_out,
                            v_rel_bias, v_xattn_norm_g, v_mem_norm_g, v_w_xq, v_w_xkv, v_w_xo, v_ffn_norm_g,
                            v_w_ffn_gate, v_w_ffn_up, v_w_ffn_down, v_final_norm_g)))

    xi, yi, ci = lax.axis_index("x"), lax.axis_index("y"), lax.axis_index("c")
    as_idx = lambda *v: jnp.stack([jnp.asarray(t, I32) for t in v])
    me = 2 * xi + yi
    big = [name for name, _ in BIG]
    two_d = dict(BIG)
    gathered = dict(zip(big, _ag_ring_multi(
        [_cast_place(w_sh[name].reshape(two_d[name]), as_idx(me), "place_" + name) for name in big])))
    conv_part = lax.dynamic_update_slice_in_dim(jnp.zeros((DEPTH, 3, BRANCH), F32), 0.5 * conv_w, 128 * me, axis=2)
    conv_full = _allreduce_small(conv_part.reshape(-1, 128), "allgather_conv").reshape(DEPTH, 3, BRANCH)

    def lay(name, l):
        g = gathered[name]
        return g.reshape(4, DEPTH, g.shape[1] // DEPTH, g.shape[2])[:, l]

    def by_cols(name, l):
        g = lay(name, l)
        return jnp.moveaxis(g, 0, 1).reshape(g.shape[1], 4 * g.shape[2])

    def by_rows(name, l):
        g = lay(name, l)
        return g.reshape(4 * g.shape[1], g.shape[2])

    W = {k: w_sh[k] for k in ("mix_norm_g", "forget_bias", "sink", "xattn_norm_g", "mem_norm_g",
                              "ffn_norm_g", "final_norm_g")}
    W["conv_w"] = conv_full
    W["w_in_p"] = [_perm_w_in(by_cols("w_in", l)) for l in range(DEPTH)]
    W["w_gu"] = [jnp.concatenate([by_cols("w_ffn_gate", l), by_cols("w_ffn_up", l)], axis=1) for l in range(DEPTH)]
    W["w_xkv"] = [by_cols("w_xkv", l) for l in range(DEPTH)]
    W["w_branch"] = [jnp.transpose(lay("w_branch", l).reshape(4, 3, BRANCH, 256), (1, 2, 0, 3)).reshape(3, BRANCH, D_MODEL)
                     for l in range(DEPTH)]
    for k in ("w_mix_out", "w_xq", "w_xo", "w_ffn_down"):
        W[k] = [by_rows(k, l) for l in range(DEPTH)]
    loss_row, dx, G = _local_step(x[0], mem[0], loss_target[0], W, rel_bias)

    def to_cols(g):
        return jnp.moveaxis(g.reshape(g.shape[0], 4, g.shape[1] // 4), 1, 0)

    def to_rows(g):
        return g.reshape(4, g.shape[0] // 4, g.shape[1])

    per_layer = {
        "w_in": [to_cols(_unperm_w_in(G["w_in_p"][l])) for l in range(DEPTH)],
        "w_branch": [jnp.transpose(G["w_branch"][l].reshape(3, BRANCH, 4, 256), (2, 0, 1, 3)).reshape(4, 3 * BRANCH, 256)
                     for l in range(DEPTH)],
        "w_mix_out": [to_rows(g) for g in G["w_mix_out"]],
        "w_xq": [to_rows(g) for g in G["w_xq"]],
        "w_xkv": [to_cols(g) for g in G["w_xkv"]],
        "w_xo": [to_rows(g) for g in G["w_xo"]],
        "w_ffn_gate": [to_cols(G["w_gu"][l][:, :D_FF]) for l in range(DEPTH)],
        "w_ffn_up": [to_cols(G["w_gu"][l][:, D_FF:]) for l in range(DEPTH)],
        "w_ffn_down": [to_rows(g) for g in G["w_ffn_down"]],
    }
    g4 = [jnp.concatenate(per_layer[name], axis=1).astype(BF16) for name in big]
    sib = _rs_sibling_multi(g4)
    pair = [_rs_add_pair(g4[t], sib[t], as_idx(ci), "_" + big[t]) for t in range(len(big))]
    diag = _rs_diag_multi(pair)
    nbrs = as_idx(2 * (1 - xi) + yi, 2 * xi + (1 - yi))
    merged = [_rs_merge(pair[t], diag[t], nbrs, "_" + big[t]) for t in range(len(big))]
    got = _rs_direct_multi(merged)
    reduced = _rs_share_multi([_rs_final(pair[t], got[t], as_idx(me, ci), "_" + big[t]) for t in range(len(big))])

    small = _unpack_small(_allreduce_small(_pack_small({
        "mix_norm_g": jnp.concatenate(G["mix_norm_g"], axis=0),
        "xattn_norm_g": jnp.concatenate(G["xattn_norm_g"], axis=0),
        "mem_norm_g": jnp.concatenate(G["mem_norm_g"], axis=0),
        "ffn_norm_g": jnp.concatenate(G["ffn_norm_g"], axis=0),
        "final_norm_g": G["final_norm_g"],
        "forget_bias": jnp.stack(G["forget_bias"]),
        "sink": jnp.stack(G["sink"]),
        "rel_bias": G["rel_bias"],
        "conv_w": jnp.stack(G["conv_w"]),
    }, SMALL_AND_CONV)), SMALL_AND_CONV)
    grads = {name: reduced[t].reshape(w_sh[name].shape) for t, name in enumerate(big)}
    grads.update(small)
    grads["conv_w"] = lax.dynamic_slice_in_dim(small["conv_w"], 128 * me, 128, axis=2)

    sm_names = [name for name, _ in SMALL]
    sd, sm_, sv_ = _adamw(_pack_small({k: w_sh[k] for k in sm_names}), _pack_small({k: grads[k] for k in sm_names}),
                          _pack_small({k: m_sh[k] for k in sm_names}), _pack_small({k: v_sh[k] for k in sm_names}),
                          "adamw_small")
    delta, new_m, new_v = _unpack_small(sd), _unpack_small(sm_), _unpack_small(sv_)
    for name, shape in BIG + (("conv_w", (6, 128)),):
        full_shape = w_sh[name].shape
        d, nm, nv = _adamw(w_sh[name].reshape(shape), grads[name].reshape(shape), m_sh[name].reshape(shape),
                           v_sh[name].reshape(shape), "adamw_" + name)
        delta[name], new_m[name], new_v[name] = d.reshape(full_shape), nm.reshape(full_shape), nv.reshape(full_shape)

    loss = lax.psum(loss_row[0, 0], ("x", "y", "c"))
    return (loss, dx[None], *[grads[k] for k in order], *[delta[k] for k in order],
            *[new_m[k] for k in order], *[new_v[k] for k in order])
```

```python
import math

import numpy as np
import jax
import jax.numpy as jnp
from jax import lax
from jax.experimental import pallas as pl
from jax.experimental.pallas import tpu as pltpu

F32 = jnp.float32
BF16 = jnp.bfloat16
I32 = jnp.int32

D_MODEL = 1024
DEPTH = 2
HEAD_DIM = 64
BRANCH = 512
N_BUCKETS = 32
WINDOW = 128
MEM_LEN = 256
X_HEADS = 4
X_HEAD_DIM = 256
D_FF = 2816
IN_COLS = 6920
PROJ_MAIN = 6912
PROJ_PAD = 7040
RMS_EPS = 1e-6
NEG = -1e30
ATT_SCALE = 0.125
X_SCALE = 0.0625

ADAM_LR = 0.001
ADAM_B1 = 0.9
ADAM_B2 = 0.999
ADAM_EPS = 1e-08
ADAM_WD = 0.01
ADAM_STEP = 10

VMEM_LIMIT = 48 * 1024 * 1024
MESH = pl.DeviceIdType.MESH

CB_GATE = (0, 1, 2)
CB_B, CB_C, CB_U, CB_FQ, CB_FK, CB_FV, CB_SQ = 6, 7, 8, 9, 10, 11, 12
CB_SK, CB_SV = 52, 53


def _cp(sem):
    return pltpu.CompilerParams(dimension_semantics=sem, vmem_limit_bytes=VMEM_LIMIT)


def _pick(n, prefs):
    for p in prefs:
        if p <= n and n % p == 0:
            return p
    return n


def _dot(a, b, dims):
    return lax.dot_general(a, b, (dims, ((), ())), preferred_element_type=F32)


def _dot_nn(a, b):
    return _dot(a, b, ((1,), (0,)))


def _dot_nt(a, b):
    return _dot(a, b, ((1,), (1,)))


def _dot_tn(a, b):
    return _dot(a, b, ((0,), (0,)))


def _mm(a, b, mode, out_dtype, name, res=None, bm=1024, bn=1024, bk=1024):
    if mode == "nn":
        (M, K), (K2, N) = a.shape, b.shape
    elif mode == "nt":
        (M, K), (N, K2) = a.shape, b.shape
    else:
        (K, M), (K2, N) = a.shape, b.shape
    assert K == K2, (name, a.shape, b.shape)
    bm = _pick(M, (bm, 1024, 512, 256, 128))
    bn = _pick(N, (bn, 1024, 768, 640, 512, 384, 256, 128))
    bk = _pick(K, (bk, 1024, 768, 640, 512, 384, 256, 128))
    nk = K // bk
    if mode == "tn":
        a_spec = pl.BlockSpec((bk, bm), lambda i, j, k: (k, i))
    else:
        a_spec = pl.BlockSpec((bm, bk), lambda i, j, k: (i, k))
    if mode == "nt":
        b_spec = pl.BlockSpec((bn, bk), lambda i, j, k: (j, k))
    else:
        b_spec = pl.BlockSpec((bk, bn), lambda i, j, k: (k, j))
    dims = {"nn": ((1,), (0,)), "nt": ((1,), (1,)), "tn": ((0,), (0,))}[mode]
    o_spec = pl.BlockSpec((bm, bn), lambda i, j, k: (i, j))
    has_res = res is not None

    def body(*refs):
        if has_res:
            a_ref, b_ref, r_ref, o_ref = refs[:4]
            scr = refs[4:]
        else:
            a_ref, b_ref, o_ref = refs[:3]
            r_ref = None
            scr = refs[3:]
        p = _dot(a_ref[...].astype(BF16), b_ref[...].astype(BF16), dims)
        if nk == 1:
            if has_res:
                p = p + r_ref[...]
            o_ref[...] = p.astype(out_dtype)
        else:
            acc = scr[0]
            k = pl.program_id(2)

            @pl.when(k == 0)
            def _():
                acc[...] = p

            @pl.when(k > 0)
            def _():
                acc[...] += p

            @pl.when(k == nk - 1)
            def _():
                r = acc[...]
                if has_res:
                    r = r + r_ref[...]
                o_ref[...] = r.astype(out_dtype)

    ins = [a, b] + ([res] if has_res else [])
    in_specs = [a_spec, b_spec] + ([o_spec] if has_res else [])
    return pl.pallas_call(
        body, name=name, grid=(M // bm, N // bn, nk),
        in_specs=in_specs, out_specs=o_spec,
        out_shape=jax.ShapeDtypeStruct((M, N), out_dtype),
        scratch_shapes=[pltpu.VMEM((bm, bn), F32)] if nk > 1 else [],
        compiler_params=_cp(("parallel", "parallel", "arbitrary")),
    )(*ins)


def _rms_fwd(x, g, name):
    T, Dm = x.shape
    bt = _pick(T, (512, 256))

    def body(x_ref, g_ref, o_ref):
        xv = x_ref[...]
        r = lax.rsqrt(jnp.mean(xv * xv, axis=-1, keepdims=True) + RMS_EPS)
        o_ref[...] = ((xv * r) * g_ref[...]).astype(BF16)

    return pl.pallas_call(
        body, name=name, grid=(T // bt,),
        in_specs=[pl.BlockSpec((bt, Dm), lambda i: (i, 0)), pl.BlockSpec((1, Dm), lambda i: (0, 0))],
        out_specs=pl.BlockSpec((bt, Dm), lambda i: (i, 0)),
        out_shape=jax.ShapeDtypeStruct((T, Dm), BF16),
        compiler_params=_cp(("parallel",)),
    )(x, g)


def _rms_bwd(x, g, dh, dres, name):
    T, Dm = x.shape
    bt = _pick(T, (512, 256))
    want_dx = dres is not None

    def body(*refs):
        if want_dx:
            x_ref, g_ref, dh_ref, dr_ref, dx_ref, dg_ref = refs
        else:
            x_ref, g_ref, dh_ref, dg_ref = refs
        xv = x_ref[...]
        r = lax.rsqrt(jnp.mean(xv * xv, axis=-1, keepdims=True) + RMS_EPS)
        xh = xv * r
        dhv = dh_ref[...].astype(F32)

        @pl.when(pl.program_id(0) == 0)
        def _():
            dg_ref[...] = jnp.zeros_like(dg_ref)

        dg_ref[...] += jnp.sum(dhv * xh, axis=0, keepdims=True)
        if want_dx:
            dyg = dhv * g_ref[...]
            dx_ref[...] = dr_ref[...] + r * (dyg - xh * jnp.mean(dyg * xh, axis=-1, keepdims=True))

    row = pl.BlockSpec((bt, Dm), lambda i: (i, 0))
    vec = pl.BlockSpec((1, Dm), lambda i: (0, 0))
    if want_dx:
        return pl.pallas_call(
            body, name=name, grid=(T // bt,),
            in_specs=[row, vec, row, row], out_specs=[row, vec],
            out_shape=[jax.ShapeDtypeStruct((T, Dm), F32), jax.ShapeDtypeStruct((1, Dm), F32)],
            compiler_params=_cp(("arbitrary",)),
        )(x, g, dh, dres)
    return None, pl.pallas_call(
        body, name=name, grid=(T // bt,),
        in_specs=[row, vec, row], out_specs=vec,
        out_shape=jax.ShapeDtypeStruct((1, Dm), F32),
        compiler_params=_cp(("arbitrary",)),
    )(x, g, dh)


def _final_loss(x, g, tgt, name):
    T, Dm = x.shape
    bt = _pick(T, (512, 256))

    def body(x_ref, g_ref, t_ref, loss_ref, dx_ref, dg_ref):
        xv = x_ref[...]
        r = lax.rsqrt(jnp.mean(xv * xv, axis=-1, keepdims=True) + RMS_EPS)
        xh = xv * r
        gv = g_ref[...]
        err = xh * gv - t_ref[...]

        @pl.when(pl.program_id(0) == 0)
        def _():
            dg_ref[...] = jnp.zeros_like(dg_ref)
            loss_ref[...] = jnp.zeros_like(loss_ref)

        loss_ref[...] += jnp.sum(err * err) * (0.5 / Dm)
        dy = err * (1.0 / Dm)
        dg_ref[...] += jnp.sum(dy * xh, axis=0, keepdims=True)
        dyg = dy * gv
        dx_ref[...] = r * (dyg - xh * jnp.mean(dyg * xh, axis=-1, keepdims=True))

    row = pl.BlockSpec((bt, Dm), lambda i: (i, 0))
    vec = pl.BlockSpec((1, Dm), lambda i: (0, 0))
    return pl.pallas_call(
        body, name=name, grid=(T // bt,),
        in_specs=[row, vec, row],
        out_specs=[pl.BlockSpec((1, 128), lambda i: (0, 0)), row, vec],
        out_shape=[jax.ShapeDtypeStruct((1, 128), F32), jax.ShapeDtypeStruct((T, Dm), F32),
                   jax.ShapeDtypeStruct((1, Dm), F32)],
        compiler_params=_cp(("arbitrary",)),
    )(x, g, tgt)


HALO = 16


def _shift_down(z, zprev, s):
    rolled = pltpu.roll(z, s, 0)
    hp = pltpu.roll(zprev, s, 0)
    row = lax.broadcasted_iota(I32, hp.shape, 0)
    top = jnp.where(row < s, hp, rolled[:HALO])
    return jnp.concatenate([top, rolled[HALO:]], axis=0)


def _shift_up(z, znext, s):
    n = z.shape[0]
    rolled = pltpu.roll(z, n - s, 0)
    hn = pltpu.roll(znext, HALO - s, 0)
    row = lax.broadcasted_iota(I32, hn.shape, 0)
    bot = jnp.where(row >= HALO - s, hn, rolled[n - HALO:])
    return jnp.concatenate([rolled[:n - HALO], bot], axis=0)


def _conv_fwd(pm, cw, name):
    T = pm.shape[0]
    bt = _pick(T, (512, 256))
    hb = bt // HALO

    def body(b_ref, c_ref, u_ref, cp_ref, up_ref, w_ref, o_ref):
        i = pl.program_id(0)
        z = c_ref[...].astype(F32) * u_ref[...].astype(F32)
        zp = cp_ref[...].astype(F32) * up_ref[...].astype(F32)
        zp = jnp.where(i > 0, zp, 0.0)
        w = w_ref[...]
        y = w[2:3] * z + w[1:2] * _shift_down(z, zp, 1) + w[0:1] * _shift_down(z, zp, 2)
        o_ref[...] = (b_ref[...].astype(F32) * y).astype(BF16)

    def col(cb):
        return pl.BlockSpec((bt, BRANCH), lambda i: (i, cb))

    def prev(cb):
        return pl.BlockSpec((HALO, BRANCH), lambda i: (jnp.maximum(i * hb - 1, 0), cb))

    return pl.pallas_call(
        body, name=name, grid=(T // bt,),
        in_specs=[col(CB_B), col(CB_C), col(CB_U), prev(CB_C), prev(CB_U),
                  pl.BlockSpec((8, BRANCH), lambda i: (0, 0))],
        out_specs=pl.BlockSpec((bt, BRANCH), lambda i: (i, 0)),
        out_shape=jax.ShapeDtypeStruct((T, BRANCH), BF16),
        compiler_params=_cp(("parallel",)),
    )(pm, pm, pm, pm, pm, cw)


def _conv_bwd(pm, cw, dy, dproj, name):
    T = pm.shape[0]
    bt = _pick(T, (512, 256))
    hb = bt // HALO
    nb = T // bt
    last_h = T // HALO - 1

    def body(b_ref, c_ref, u_ref, cp_ref, up_ref, bn_ref, dy_ref, dyn_ref, w_ref, buf_ref,
             dp_ref, dw_ref):
        del buf_ref
        db_ref = dp_ref.at[:, 0:BRANCH]
        dc_ref = dp_ref.at[:, BRANCH:2 * BRANCH]
        du_ref = dp_ref.at[:, 2 * BRANCH:3 * BRANCH]
        i = pl.program_id(0)
        cv = c_ref[...].astype(F32)
        uv = u_ref[...].astype(F32)
        bv = b_ref[...].astype(F32)
        z = cv * uv
        zp = jnp.where(i > 0, cp_ref[...].astype(F32) * up_ref[...].astype(F32), 0.0)
        w = w_ref[...]
        z1 = _shift_down(z, zp, 1)
        z2 = _shift_down(z, zp, 2)
        yc = w[2:3] * z + w[1:2] * z1 + w[0:1] * z2
        dyv = dy_ref[...].astype(F32)
        db_ref[...] = (dyv * yc).astype(BF16)
        g = dyv * bv
        gn = jnp.where(i < nb - 1, dyn_ref[...].astype(F32) * bn_ref[...].astype(F32), 0.0)
        dz = w[2:3] * g + w[1:2] * _shift_up(g, gn, 1) + w[0:1] * _shift_up(g, gn, 2)
        dc_ref[...] = (dz * uv).astype(BF16)
        du_ref[...] = (dz * cv).astype(BF16)

        @pl.when(i == 0)
        def _():
            dw_ref[...] = jnp.zeros_like(dw_ref)

        dw_ref[0:1, :] += jnp.sum(g * z2, axis=0, keepdims=True)
        dw_ref[1:2, :] += jnp.sum(g * z1, axis=0, keepdims=True)
        dw_ref[2:3, :] += jnp.sum(g * z, axis=0, keepdims=True)

    def col(cb):
        return pl.BlockSpec((bt, BRANCH), lambda i: (i, cb))

    def prev(cb):
        return pl.BlockSpec((HALO, BRANCH), lambda i: (jnp.maximum(i * hb - 1, 0), cb))

    def nxt(cb):
        return pl.BlockSpec((HALO, BRANCH), lambda i: (jnp.minimum((i + 1) * hb, last_h), cb))

    own = pl.BlockSpec((bt, BRANCH), lambda i: (i, 0))
    w_spec = pl.BlockSpec((8, BRANCH), lambda i: (0, 0))
    return pl.pallas_call(
        body, name=name, grid=(nb,),
        in_specs=[col(CB_B), col(CB_C), col(CB_U), prev(CB_C), prev(CB_U), nxt(CB_B), own,
                  pl.BlockSpec((HALO, BRANCH), lambda i: (jnp.minimum((i + 1) * hb, last_h), 0)), w_spec,
                  pl.BlockSpec(memory_space=pl.ANY)],
        out_specs=[pl.BlockSpec((bt, 3 * BRANCH), lambda i: (i, 2)), w_spec],
        out_shape=[jax.ShapeDtypeStruct(dproj.shape, dproj.dtype), jax.ShapeDtypeStruct((8, BRANCH), F32)],
        input_output_aliases={9: 0},
        compiler_params=_cp(("arbitrary",)),
    )(pm, pm, pm, pm, pm, pm, dy, dy, cw, dproj)


def _log_sigmoid(z):
    return jnp.minimum(z, 0.0) - jnp.log(1.0 + jnp.exp(-jnp.abs(z)))


def _fox_gate_fwd(fg, fb, name):
    T = fg.shape[0]
    bt = _pick(T, (256,))

    def body(f_ref, b_ref, c_ref, carry):
        @pl.when(pl.program_id(0) == 0)
        def _():
            carry[...] = jnp.zeros_like(carry)

        xv = _log_sigmoid(f_ref[...] + b_ref[...])
        row = lax.broadcasted_iota(I32, xv.shape, 0)
        s = 1
        while s < bt:
            xv = xv + jnp.where(row >= s, pltpu.roll(xv, s, 0), 0.0)
            s *= 2
        xv = xv + carry[...]
        c_ref[...] = xv
        carry[...] = xv[bt - 1:bt, :]

    blk = pl.BlockSpec((bt, 128), lambda i: (i, 0))
    return pl.pallas_call(
        body, name=name, grid=(T // bt,),
        in_specs=[blk, pl.BlockSpec((1, 128), lambda i: (0, 0))],
        out_specs=blk, out_shape=jax.ShapeDtypeStruct((T, 128), F32),
        scratch_shapes=[pltpu.VMEM((1, 128), F32)],
        compiler_params=_cp(("arbitrary",)),
    )(fg, fb)


def _fox_gate_bwd(dc, fg, fb, name):
    T = fg.shape[0]
    bt = _pick(T, (256,))
    nb = T // bt

    def body(d_ref, f_ref, b_ref, o_ref, db_ref, carry):
        @pl.when(pl.program_id(0) == 0)
        def _():
            carry[...] = jnp.zeros_like(carry)
            db_ref[...] = jnp.zeros_like(db_ref)

        xv = d_ref[...]
        row = lax.broadcasted_iota(I32, xv.shape, 0)
        s = 1
        while s < bt:
            xv = xv + jnp.where(row < bt - s, pltpu.roll(xv, bt - s, 0), 0.0)
            s *= 2
        xv = xv + carry[...]
        carry[...] = xv[0:1, :]
        z = f_ref[...] + b_ref[...]
        dz = xv * (1.0 / (1.0 + jnp.exp(z)))
        o_ref[...] = dz
        db_ref[...] += jnp.sum(dz, axis=0, keepdims=True)

    blk = pl.BlockSpec((bt, 128), lambda i: (nb - 1 - i, 0))
    vec = pl.BlockSpec((1, 128), lambda i: (0, 0))
    return pl.pallas_call(
        body, name=name, grid=(nb,),
        in_specs=[blk, blk, vec], out_specs=[blk, vec],
        out_shape=[jax.ShapeDtypeStruct((T, 128), F32), jax.ShapeDtypeStruct((1, 128), F32)],
        scratch_shapes=[pltpu.VMEM((1, 128), F32)],
        compiler_params=_cp(("arbitrary",)),
    )(dc, fg, fb)


def _lane_lo(shape):
    return lax.broadcasted_iota(I32, shape, 1) < HEAD_DIM


def _put_col(shape, h, col):
    lane = lax.broadcasted_iota(I32, shape, 1)
    return jnp.where(lane == h, col, 0.0)


def _fox_fwd(pm, c_col, c_row, name):
    T = pm.shape[0]
    bq = _pick(T, (512, 256))
    bk = bq
    nq = T // bq

    def body(q_ref, k_ref, v_ref, cq_ref, ck_ref, o_ref, lse_ref, acc, m_s, l_s):
        qi = pl.program_id(0)
        ki = pl.program_id(1)

        @pl.when(ki == 0)
        def _():
            acc[...] = jnp.zeros_like(acc)
            m_s[...] = jnp.full_like(m_s, NEG)
            l_s[...] = jnp.zeros_like(l_s)

        @pl.when(ki <= qi)
        def _():
            row = lax.broadcasted_iota(I32, (bq, bk), 0) + qi * bq
            colv = lax.broadcasted_iota(I32, (bq, bk), 1) + ki * bk
            causal = colv <= row
            klo = _lane_lo((bk, 128))
            qlo = _lane_lo((bq, 128))
            cq = cq_ref[...]
            ck = ck_ref[...]
            for p in range(4):
                sl = slice(128 * p, 128 * p + 128)
                qp = q_ref[:, sl] * ATT_SCALE
                kp = k_ref[:, sl]
                vp = v_ref[:, sl]
                kz = jnp.zeros_like(kp)
                ks = (jnp.where(klo, kp, kz), jnp.where(klo, kz, kp))
                alphas, pvs = [], []
                for j in range(2):
                    h = 2 * p + j
                    s = _dot_nt(qp, ks[j]) + (cq[:, h:h + 1] - ck[h:h + 1, :])
                    s = jnp.where(causal, s, NEG)
                    m_old = m_s[h][:, 0:1]
                    m_new = jnp.maximum(m_old, jnp.max(s, axis=-1, keepdims=True))
                    alpha = jnp.exp(m_old - m_new)
                    pe = jnp.exp(s - m_new)
                    l_new = alpha * l_s[h][:, 0:1] + jnp.sum(pe, axis=-1, keepdims=True)
                    m_s[h] = jnp.broadcast_to(m_new, (bq, 128))
                    l_s[h] = jnp.broadcast_to(l_new, (bq, 128))
                    alphas.append(alpha)
                    pvs.append(_dot_nn(pe.astype(BF16), vp))
                a = jnp.where(qlo, alphas[0], alphas[1])
                acc[:, sl] = a * acc[:, sl] + jnp.where(qlo, pvs[0], pvs[1])

        @pl.when(ki == nq - 1)
        def _():
            qlo = _lane_lo((bq, 128))
            lse = jnp.zeros((bq, 128), F32)
            for p in range(4):
                sl = slice(128 * p, 128 * p + 128)
                l0 = l_s[2 * p][:, 0:1]
                l1 = l_s[2 * p + 1][:, 0:1]
                o_ref[:, sl] = (acc[:, sl] / jnp.where(qlo, l0, l1)).astype(BF16)
                lse = lse + _put_col((bq, 128), 2 * p, m_s[2 * p][:, 0:1] + jnp.log(l0))
                lse = lse + _put_col((bq, 128), 2 * p + 1, m_s[2 * p + 1][:, 0:1] + jnp.log(l1))
            lse_ref[...] = lse

    return pl.pallas_call(
        body, name=name, grid=(nq, nq),
        in_specs=[pl.BlockSpec((bq, BRANCH), lambda i, k: (i, CB_FQ)),
                  pl.BlockSpec((bk, BRANCH), lambda i, k: (jnp.minimum(k, i), CB_FK)),
                  pl.BlockSpec((bk, BRANCH), lambda i, k: (jnp.minimum(k, i), CB_FV)),
                  pl.BlockSpec((bq, 128), lambda i, k: (i, 0)),
                  pl.BlockSpec((8, bk), lambda i, k: (0, jnp.minimum(k, i)))],
        out_specs=[pl.BlockSpec((bq, BRANCH), lambda i, k: (i, 0)),
                   pl.BlockSpec((bq, 128), lambda i, k: (i, 0))],
        out_shape=[jax.ShapeDtypeStruct((T, BRANCH), BF16), jax.ShapeDtypeStruct((T, 128), F32)],
        scratch_shapes=[pltpu.VMEM((bq, BRANCH), F32), pltpu.VMEM((8, bq, 128), F32),
                        pltpu.VMEM((8, bq, 128), F32)],
        compiler_params=_cp(("parallel", "arbitrary")),
    )(pm, pm, pm, c_col, c_row)


def _fox_delta(o, do, name):
    T = o.shape[0]
    bt = _pick(T, (512, 256))

    def body(o_ref, d_ref, out_ref):
        prod = o_ref[...].astype(F32) * d_ref[...].astype(F32)
        out = jnp.zeros((bt, 128), F32)
        for h in range(8):
            out = out + _put_col((bt, 128), h, jnp.sum(prod[:, 64 * h:64 * h + 64], axis=-1, keepdims=True))
        out_ref[...] = out

    blk = pl.BlockSpec((bt, BRANCH), lambda i: (i, 0))
    return pl.pallas_call(
        body, name=name, grid=(T // bt,), in_specs=[blk, blk],
        out_specs=pl.BlockSpec((bt, 128), lambda i: (i, 0)),
        out_shape=jax.ShapeDtypeStruct((T, 128), F32),
        compiler_params=_cp(("parallel",)),
    )(o, do)


def _fox_bwd_dq(pm, do, c_col, c_row, lse, delta, name):
    T = pm.shape[0]
    bq = _pick(T, (512, 256))
    bk = bq
    nq = T // bq

    def body(q_ref, k_ref, v_ref, do_ref, cq_ref, ck_ref, lse_ref, dl_ref, dq_ref, dl2_ref, acc, esum):
        qi = pl.program_id(0)
        ki = pl.program_id(1)

        @pl.when(ki == 0)
        def _():
            acc[...] = jnp.zeros_like(acc)
            esum[...] = jnp.zeros_like(esum)

        @pl.when(ki <= qi)
        def _():
            row = lax.broadcasted_iota(I32, (bq, bk), 0) + qi * bq
            colv = lax.broadcasted_iota(I32, (bq, bk), 1) + ki * bk
            causal = colv <= row
            klo = _lane_lo((bk, 128))
            qlo = _lane_lo((bq, 128))
            cq = cq_ref[...]
            ck = ck_ref[...]
            lse_v = lse_ref[...]
            dl_v = dl_ref[...]
            es = jnp.zeros((bq, 128), F32)
            for p in range(4):
                sl = slice(128 * p, 128 * p + 128)
                qp = q_ref[:, sl] * ATT_SCALE
                kp = k_ref[:, sl]
                vp = v_ref[:, sl]
                dop = do_ref[:, sl]
                kz = jnp.zeros_like(kp)
                ks = (jnp.where(klo, kp, kz), jnp.where(klo, kz, kp))
                vs = (jnp.where(klo, vp, kz), jnp.where(klo, kz, vp))
                dqs = []
                for j in range(2):
                    h = 2 * p + j
                    s = _dot_nt(qp, ks[j]) + (cq[:, h:h + 1] - ck[h:h + 1, :])
                    s = jnp.where(causal, s, NEG)
                    pr = jnp.exp(s - lse_v[:, h:h + 1])
                    dp = _dot_nt(dop, vs[j])
                    ds = pr * (dp - dl_v[:, h:h + 1])
                    es = es + _put_col((bq, 128), h, jnp.sum(ds, axis=-1, keepdims=True))
                    dqs.append(_dot_nn(ds.astype(BF16), kp))
                acc[:, sl] += jnp.where(qlo, dqs[0], dqs[1])
            esum[...] += es

        @pl.when(ki == nq - 1)
        def _():
            dq_ref[...] = (acc[...] * ATT_SCALE).astype(BF16)
            dl2_ref[...] = dl_ref[...] + esum[...]

    qb = pl.BlockSpec((bq, 128), lambda i, k: (i, 0))
    return pl.pallas_call(
        body, name=name, grid=(nq, nq),
        in_specs=[pl.BlockSpec((bq, BRANCH), lambda i, k: (i, CB_FQ)),
                  pl.BlockSpec((bk, BRANCH), lambda i, k: (jnp.minimum(k, i), CB_FK)),
                  pl.BlockSpec((bk, BRANCH), lambda i, k: (jnp.minimum(k, i), CB_FV)),
                  pl.BlockSpec((bq, BRANCH), lambda i, k: (i, 0)),
                  qb, pl.BlockSpec((8, bk), lambda i, k: (0, jnp.minimum(k, i))), qb, qb],
        out_specs=[pl.BlockSpec((bq, BRANCH), lambda i, k: (i, 0)), qb],
        out_shape=[jax.ShapeDtypeStruct((T, BRANCH), BF16), jax.ShapeDtypeStruct((T, 128), F32)],
        scratch_shapes=[pltpu.VMEM((bq, BRANCH), F32), pltpu.VMEM((bq, 128), F32)],
        compiler_params=_cp(("parallel", "arbitrary")),
    )(pm, pm, pm, do, c_col, c_row, lse, delta)


def _fox_bwd_dkv(pm, do, c_col, c_row, lse_row, delta_row, name):
    T = pm.shape[0]
    bk = _pick(T, (512, 256))
    bq = bk
    nk = T // bk

    def body(q_ref, k_ref, v_ref, do_ref, cq_ref, ck_ref, lse_ref, dl_ref,
             dk_ref, dv_ref, dc_ref, dk_acc, dv_acc, dc_acc):
        ki = pl.program_id(0)
        qi = pl.program_id(1)

        @pl.when(qi == 0)
        def _():
            dk_acc[...] = jnp.zeros_like(dk_acc)
            dv_acc[...] = jnp.zeros_like(dv_acc)
            dc_acc[...] = jnp.zeros_like(dc_acc)

        @pl.when(qi >= ki)
        def _():
            krow = lax.broadcasted_iota(I32, (bk, bq), 0) + ki * bk
            qcol = lax.broadcasted_iota(I32, (bk, bq), 1) + qi * bq
            causal = krow <= qcol
            qlo = _lane_lo((bq, 128))
            klo = _lane_lo((bk, 128))
            cq = cq_ref[...]
            ck = ck_ref[...]
            lse_v = lse_ref[...]
            dl_v = dl_ref[...]
            dcs = jnp.zeros((bk, 128), F32)
            for p in range(4):
                sl = slice(128 * p, 128 * p + 128)
                qp = q_ref[:, sl]
                kp = k_ref[:, sl] * ATT_SCALE
                vp = v_ref[:, sl]
                dop = do_ref[:, sl]
                qz = jnp.zeros_like(qp)
                qs = (jnp.where(qlo, qp, qz), jnp.where(qlo, qz, qp))
                dos = (jnp.where(qlo, dop, qz), jnp.where(qlo, qz, dop))
                dks, dvs = [], []
                for j in range(2):
                    h = 2 * p + j
                    st = _dot_nt(kp, qs[j]) + (cq[h:h + 1, :] - ck[:, h:h + 1])
                    st = jnp.where(causal, st, NEG)
                    pt = jnp.exp(st - lse_v[h:h + 1, :])
                    dvs.append(_dot_nn(pt.astype(BF16), dop))
                    dpt = _dot_nt(vp, dos[j])
                    dst = pt * (dpt - dl_v[h:h + 1, :])
                    dks.append(_dot_nn(dst.astype(BF16), qp))
                    dcs = dcs - _put_col((bk, 128), h, jnp.sum(dst, axis=-1, keepdims=True))
                dk_acc[:, sl] += jnp.where(klo, dks[0], dks[1])
                dv_acc[:, sl] += jnp.where(klo, dvs[0], dvs[1])
            dc_acc[...] += dcs

        @pl.when(qi == nk - 1)
        def _():
            dk_ref[...] = (dk_acc[...] * ATT_SCALE).astype(BF16)
            dv_ref[...] = dv_acc[...].astype(BF16)
            dc_ref[...] = dc_acc[...]

    qrow = pl.BlockSpec((8, bq), lambda k, i: (0, jnp.maximum(i, k)))
    kb = pl.BlockSpec((bk, BRANCH), lambda k, i: (k, 0))
    return pl.pallas_call(
        body, name=name, grid=(nk, nk),
        in_specs=[pl.BlockSpec((bq, BRANCH), lambda k, i: (jnp.maximum(i, k), CB_FQ)),
                  pl.BlockSpec((bk, BRANCH), lambda k, i: (k, CB_FK)),
                  pl.BlockSpec((bk, BRANCH), lambda k, i: (k, CB_FV)),
                  pl.BlockSpec((bq, BRANCH), lambda k, i: (jnp.maximum(i, k), 0)),
                  qrow, pl.BlockSpec((bk, 128), lambda k, i: (k, 0)), qrow, qrow],
        out_specs=[kb, kb, pl.BlockSpec((bk, 128), lambda k, i: (k, 0))],
        out_shape=[jax.ShapeDtypeStruct((T, BRANCH), BF16), jax.ShapeDtypeStruct((T, BRANCH), BF16),
                   jax.ShapeDtypeStruct((T, 128), F32)],
        scratch_shapes=[pltpu.VMEM((bk, BRANCH), F32), pltpu.VMEM((bk, BRANCH), F32),
                        pltpu.VMEM((bk, 128), F32)],
        compiler_params=_cp(("parallel", "arbitrary")),
    )(pm, pm, pm, do, c_row, c_col, lse_row, delta_row)


FOX_ROWS = 32


FOX_UNROLL = 16


def _row_start(r, rows):
    return r * rows if isinstance(r, int) else pl.multiple_of(r * rows, rows)


def _chunk_loop(n, chunk):
    if n <= FOX_UNROLL:
        for u in range(n):
            chunk(u, 0)
        return

    def outer(i, carry):
        for u in range(FOX_UNROLL):
            chunk(i * FOX_UNROLL + u, carry)
        return carry

    lax.fori_loop(0, n // FOX_UNROLL, outer, 0)


def _tree(op, xs):
    xs = list(xs)
    while len(xs) > 1:
        xs = [op(xs[i], xs[i + 1]) if i + 1 < len(xs) else xs[i] for i in range(0, len(xs), 2)]
    return xs[0]


def _masked_halves(t):
    lo = _lane_lo(t.shape)
    z = jnp.zeros_like(t)
    return jnp.where(lo, t, z), jnp.where(lo, z, t)


def _fox2_fwd(pm, c_row, name):
    T = pm.shape[0]
    bq = _pick(T, (512, 256))
    bk = bq
    nq = T // bq
    R = FOX_ROWS
    ng = bk // 128

    def body(q_ref, k_ref, v_ref, ck_ref, o_ref, lse_ref, acc, m_s, l_s, a_s, s_scr, p_scr):
        qi = pl.program_id(0)
        ki = pl.program_id(1)

        @pl.when(ki == 0)
        def _():
            acc[...] = jnp.zeros_like(acc)
            m_s[...] = jnp.full_like(m_s, NEG)
            l_s[...] = jnp.zeros_like(l_s)

        def block(masked):
            qlo = _lane_lo((bq, 128))
            for p in range(4):
                sl = slice(128 * p, 128 * p + 128)
                qp = q_ref[:, sl] * ATT_SCALE
                vp = v_ref[:, sl]
                ks = _masked_halves(k_ref[:, sl])
                pvs = []
                for j in range(2):
                    h = 2 * p + j
                    s_scr[j] = _dot_nt(qp, ks[j])

                    def chunk(r, carry, h=h, j=j):
                        r0 = _row_start(r, R)
                        rows = pl.ds(r0, R)
                        sc = [s_scr[j, rows, 128 * g:128 * g + 128] - ck_ref[h:h + 1, 128 * g:128 * g + 128]
                              for g in range(ng)]
                        if masked:
                            rid = lax.broadcasted_iota(I32, (R, 128), 0) + r0
                            cid = lax.broadcasted_iota(I32, (R, 128), 1)
                            sc = [jnp.where(cid + 128 * g <= rid, sc[g], NEG) for g in range(ng)]
                        m_old = m_s[h, rows, :]
                        m_new = jnp.maximum(m_old, jnp.max(_tree(jnp.maximum, sc), axis=-1, keepdims=True))
                        alpha = jnp.exp(m_old - m_new)
                        pe = [jnp.exp(sc[g] - m_new) for g in range(ng)]
                        l_s[h, rows, :] = alpha * l_s[h, rows, :] + _tree(jnp.add, pe)
                        m_s[h, rows, :] = m_new
                        a_s[j, rows, :] = alpha
                        for g in range(ng):
                            p_scr[j, rows, 128 * g:128 * g + 128] = pe[g].astype(BF16)
                        return carry

                    _chunk_loop(bq // R, chunk)
                    pvs.append(_dot_nn(p_scr[j], vp))
                acc[:, sl] = jnp.where(qlo, a_s[0], a_s[1]) * acc[:, sl] + jnp.where(qlo, pvs[0], pvs[1])

        @pl.when(ki < qi)
        def _():
            block(False)

        @pl.when(ki == qi)
        def _():
            block(True)

        @pl.when(ki == nq - 1)
        def _():
            qlo = _lane_lo((bq, 128))
            lse = jnp.zeros((bq, 128), F32)
            for p in range(4):
                sl = slice(128 * p, 128 * p + 128)
                l0 = jnp.sum(l_s[2 * p], axis=-1, keepdims=True)
                l1 = jnp.sum(l_s[2 * p + 1], axis=-1, keepdims=True)
                o_ref[:, sl] = (acc[:, sl] / jnp.where(qlo, l0, l1)).astype(BF16)
                lse = lse + _put_col((bq, 128), 2 * p, m_s[2 * p][:, 0:1] + jnp.log(l0))
                lse = lse + _put_col((bq, 128), 2 * p + 1, m_s[2 * p + 1][:, 0:1] + jnp.log(l1))
            lse_ref[...] = lse

    return pl.pallas_call(
        body, name=name, grid=(nq, nq),
        in_specs=[pl.BlockSpec((bq, BRANCH), lambda i, k: (i, CB_FQ)),
                  pl.BlockSpec((bk, BRANCH), lambda i, k: (jnp.minimum(k, i), CB_FK)),
                  pl.BlockSpec((bk, BRANCH), lambda i, k: (jnp.minimum(k, i), CB_FV)),
                  pl.BlockSpec((8, bk), lambda i, k: (0, jnp.minimum(k, i)))],
        out_specs=[pl.BlockSpec((bq, BRANCH), lambda i, k: (i, 0)),
                   pl.BlockSpec((bq, 128), lambda i, k: (i, 0))],
        out_shape=[jax.ShapeDtypeStruct((T, BRANCH), BF16), jax.ShapeDtypeStruct((T, 128), F32)],
        scratch_shapes=[pltpu.VMEM((bq, BRANCH), F32), pltpu.VMEM((8, bq, 128), F32),
                        pltpu.VMEM((8, bq, 128), F32), pltpu.VMEM((2, bq, 128), F32),
                        pltpu.VMEM((2, bq, bk), F32), pltpu.VMEM((2, bq, bk), BF16)],
        compiler_params=_cp(("parallel", "arbitrary")),
    )(pm, pm, pm, c_row)


def _fox2_bwd_dq(pm, do, c_row, lse, delta, dproj, name):
    T = pm.shape[0]
    bq = _pick(T, (512, 256))
    bk = bq
    nq = T // bq
    R = FOX_ROWS
    ng = bk // 128

    def body(q_ref, k_ref, v_ref, do_ref, ck_ref, lse_ref, dl_ref, buf_ref, dq_ref, dl2_ref,
             acc, e_s, s_scr, dp_scr, ds_scr):
        del buf_ref
        qi = pl.program_id(0)
        ki = pl.program_id(1)

        @pl.when(ki == 0)
        def _():
            acc[...] = jnp.zeros_like(acc)
            e_s[...] = jnp.zeros_like(e_s)

        def block(masked):
            qlo = _lane_lo((bq, 128))
            for p in range(4):
                sl = slice(128 * p, 128 * p + 128)
                qp = q_ref[:, sl] * ATT_SCALE
                kp = k_ref[:, sl]
                dop = do_ref[:, sl]
                ks = _masked_halves(kp)
                vs = _masked_halves(v_ref[:, sl])
                dqs = []
                for j in range(2):
                    h = 2 * p + j
                    s_scr[...] = _dot_nt(qp, ks[j])
                    dp_scr[...] = _dot_nt(dop, vs[j])

                    def chunk(r, carry, h=h):
                        r0 = _row_start(r, R)
                        rows = pl.ds(r0, R)
                        lse_c = lse_ref[rows, h:h + 1]
                        dl_c = dl_ref[rows, h:h + 1]
                        if masked:
                            rid = lax.broadcasted_iota(I32, (R, 128), 0) + r0
                            cid = lax.broadcasted_iota(I32, (R, 128), 1)
                        dss = []
                        for g in range(ng):
                            gs = slice(128 * g, 128 * g + 128)
                            sc = s_scr[rows, gs] - ck_ref[h:h + 1, gs]
                            if masked:
                                sc = jnp.where(cid + 128 * g <= rid, sc, NEG)
                            ds = jnp.exp(sc - lse_c) * (dp_scr[rows, gs] - dl_c)
                            ds_scr[rows, gs] = ds.astype(BF16)
                            dss.append(ds)
                        e_s[h, rows, :] += _tree(jnp.add, dss)
                        return carry

                    _chunk_loop(bq // R, chunk)
                    dqs.append(_dot_nn(ds_scr[...], kp))
                acc[:, sl] += jnp.where(qlo, dqs[0], dqs[1])

        @pl.when(ki < qi)
        def _():
            block(False)

        @pl.when(ki == qi)
        def _():
            block(True)

        @pl.when(ki == nq - 1)
        def _():
            dq_ref[...] = (acc[...] * ATT_SCALE).astype(BF16)
            out = dl_ref[...]
            for h in range(8):
                out = out + _put_col((bq, 128), h, jnp.sum(e_s[h], axis=-1, keepdims=True))
            dl2_ref[...] = out

    qb = pl.BlockSpec((bq, 128), lambda i, k: (i, 0))
    return pl.pallas_call(
        body, name=name, grid=(nq, nq),
        in_specs=[pl.BlockSpec((bq, BRANCH), lambda i, k: (i, CB_FQ)),
                  pl.BlockSpec((bk, BRANCH), lambda i, k: (jnp.minimum(k, i), CB_FK)),
                  pl.BlockSpec((bk, BRANCH), lambda i, k: (jnp.minimum(k, i), CB_FV)),
                  pl.BlockSpec((bq, BRANCH), lambda i, k: (i, 0)),
                  pl.BlockSpec((8, bk), lambda i, k: (0, jnp.minimum(k, i))), qb, qb,
                  pl.BlockSpec(memory_space=pl.ANY)],
        out_specs=[pl.BlockSpec((bq, BRANCH), lambda i, k: (i, CB_FQ)), qb],
        out_shape=[jax.ShapeDtypeStruct(dproj.shape, dproj.dtype), jax.ShapeDtypeStruct((T, 128), F32)],
        input_output_aliases={7: 0},
        scratch_shapes=[pltpu.VMEM((bq, BRANCH), F32), pltpu.VMEM((8, bq, 128), F32),
                        pltpu.VMEM((bq, bk), F32), pltpu.VMEM((bq, bk), F32), pltpu.VMEM((bq, bk), BF16)],
        compiler_params=_cp(("parallel", "arbitrary")),
    )(pm, pm, pm, do, c_row, lse, delta, dproj)


def _fox2_bwd_dkv(pm, do, c_col, lse_row, delta_row, dproj, name):
    T = pm.shape[0]
    bk = _pick(T, (512, 256))
    bq = bk
    nk = T // bk
    R = FOX_ROWS
    ng = bq // 128

    def body(q_ref, k_ref, v_ref, do_ref, ck_ref, lse_ref, dl_ref, buf_ref, dkv_ref, dc_ref,
             dk_acc, dv_acc, dc_s, st_scr, dpt_scr, pt_scr, dst_scr):
        del buf_ref
        dk_ref = dkv_ref.at[:, 0:BRANCH]
        dv_ref = dkv_ref.at[:, BRANCH:2 * BRANCH]
        ki = pl.program_id(0)
        qi = pl.program_id(1)

        @pl.when(qi == 0)
        def _():
            dk_acc[...] = jnp.zeros_like(dk_acc)
            dv_acc[...] = jnp.zeros_like(dv_acc)
            dc_s[...] = jnp.zeros_like(dc_s)

        def block(masked):
            klo = _lane_lo((bk, 128))
            for p in range(4):
                sl = slice(128 * p, 128 * p + 128)
                qp = q_ref[:, sl]
                kp = k_ref[:, sl] * ATT_SCALE
                vp = v_ref[:, sl]
                dop = do_ref[:, sl]
                qs = _masked_halves(qp)
                dos = _masked_halves(dop)
                dks, dvs = [], []
                for j in range(2):
                    h = 2 * p + j
                    st_scr[...] = _dot_nt(kp, qs[j])
                    dpt_scr[...] = _dot_nt(vp, dos[j])

                    def chunk(r, carry, h=h):
                        r0 = _row_start(r, R)
                        rows = pl.ds(r0, R)
                        ck_c = ck_ref[rows, h:h + 1]
                        if masked:
                            kid = lax.broadcasted_iota(I32, (R, 128), 0) + r0
                            qid = lax.broadcasted_iota(I32, (R, 128), 1)
                        dss = []
                        for g in range(ng):
                            gs = slice(128 * g, 128 * g + 128)
                            st = st_scr[rows, gs] - (ck_c + lse_ref[h:h + 1, gs])
                            if masked:
                                st = jnp.where(kid <= qid + 128 * g, st, NEG)
                            pt = jnp.exp(st)
                            dst = pt * (dpt_scr[rows, gs] - dl_ref[h:h + 1, gs])
                            pt_scr[rows, gs] = pt.astype(BF16)
                            dst_scr[rows, gs] = dst.astype(BF16)
                            dss.append(dst)
                        dc_s[h, rows, :] -= _tree(jnp.add, dss)
                        return carry

                    _chunk_loop(bk // R, chunk)
                    dvs.append(_dot_nn(pt_scr[...], dop))
                    dks.append(_dot_nn(dst_scr[...], qp))
                dk_acc[:, sl] += jnp.where(klo, dks[0], dks[1])
                dv_acc[:, sl] += jnp.where(klo, dvs[0], dvs[1])

        @pl.when(qi > ki)
        def _():
            block(False)

        @pl.when(qi == ki)
        def _():
            block(True)

        @pl.when(qi == nk - 1)
        def _():
            dk_ref[...] = (dk_acc[...] * ATT_SCALE).astype(BF16)
            dv_ref[...] = dv_acc[...].astype(BF16)
            out = jnp.zeros((bk, 128), F32)
            for h in range(8):
                out = out + _put_col((bk, 128), h, jnp.sum(dc_s[h], axis=-1, keepdims=True))
            dc_ref[...] = out

    qrow = pl.BlockSpec((8, bq), lambda k, i: (0, jnp.maximum(i, k)))
    kb = pl.BlockSpec((bk, BRANCH), lambda k, i: (k, 0))
    return pl.pallas_call(
        body, name=name, grid=(nk, nk),
        in_specs=[pl.BlockSpec((bq, BRANCH), lambda k, i: (jnp.maximum(i, k), CB_FQ)),
                  pl.BlockSpec((bk, BRANCH), lambda k, i: (k, CB_FK)),
                  pl.BlockSpec((bk, BRANCH), lambda k, i: (k, CB_FV)),
                  pl.BlockSpec((bq, BRANCH), lambda k, i: (jnp.maximum(i, k), 0)),
                  pl.BlockSpec((bk, 128), lambda k, i: (k, 0)), qrow, qrow, pl.BlockSpec(memory_space=pl.ANY)],
        out_specs=[pl.BlockSpec((bk, 2 * BRANCH), lambda k, i: (k, 5)), pl.BlockSpec((bk, 128), lambda k, i: (k, 0))],
        out_shape=[jax.ShapeDtypeStruct(dproj.shape, dproj.dtype), jax.ShapeDtypeStruct((T, 128), F32)],
        input_output_aliases={7: 0},
        scratch_shapes=[pltpu.VMEM((bk, BRANCH), F32), pltpu.VMEM((bk, BRANCH), F32),
                        pltpu.VMEM((8, bk, 128), F32), pltpu.VMEM((bk, bq), F32), pltpu.VMEM((bk, bq), F32),
                        pltpu.VMEM((bk, bq), BF16), pltpu.VMEM((bk, bq), BF16)],
        compiler_params=_cp(("parallel", "arbitrary")),
    )(pm, pm, pm, do, c_col, lse_row, delta_row, dproj)


def _bucket_table():
    tq = np.arange(WINDOW, dtype=np.int32)[:, None]
    sk = np.arange(2 * WINDOW, dtype=np.int32)[None, :]
    n = np.maximum(WINDOW + tq - sk, 0)
    max_exact = N_BUCKETS // 2
    ratio = np.maximum(n, 1).astype(np.float32) / np.float32(max_exact)
    large = max_exact + (np.log(ratio) / np.float32(math.log(WINDOW / max_exact))
                         * np.float32(N_BUCKETS - max_exact)).astype(np.int32)
    large = np.minimum(large, N_BUCKETS - 1)
    return np.where(n < max_exact, n, large).astype(np.int32)


def _swa_bias(rel_bias, bucket, name):
    def body(rb_ref, bk_ref, o_ref):
        bkt = bk_ref[...]
        for h in range(8):
            def step(b, a):
                return a + jnp.where(bkt == b, rb_ref[b, h], 0.0)
            o_ref[h] = lax.fori_loop(0, N_BUCKETS, step, jnp.zeros(bkt.shape, F32))

    return pl.pallas_call(
        body, name=name,
        in_specs=[pl.BlockSpec(memory_space=pltpu.SMEM), pl.BlockSpec(memory_space=pltpu.VMEM)],
        out_specs=pl.BlockSpec(memory_space=pltpu.VMEM),
        out_shape=jax.ShapeDtypeStruct((8, WINDOW, 2 * WINDOW), F32),
    )(rel_bias, bucket)


def _swa_dbias_reduce(dbias, bucket, name):
    def body(d_ref, bk_ref, o_ref):
        bkt = bk_ref[...]
        rowi = lax.broadcasted_iota(I32, (N_BUCKETS, 128), 0)
        lane = lax.broadcasted_iota(I32, (N_BUCKETS, 128), 1)
        out = jnp.zeros((N_BUCKETS, 128), F32)
        for h in range(8):
            dv = d_ref[h]

            def step(b, a):
                tot = jnp.sum(jnp.where(bkt == b, dv, 0.0), keepdims=True)
                return a + jnp.where((rowi == b) & (lane == h), tot, 0.0)
            out = lax.fori_loop(0, N_BUCKETS, step, out)
        o_ref[...] = out

    return pl.pallas_call(
        body, name=name,
        in_specs=[pl.BlockSpec(memory_space=pltpu.VMEM), pl.BlockSpec(memory_space=pltpu.VMEM)],
        out_specs=pl.BlockSpec(memory_space=pltpu.VMEM),
        out_shape=jax.ShapeDtypeStruct((N_BUCKETS, 128), F32),
    )(dbias, bucket)


def _swap_halves(x):
    return pltpu.roll(x.astype(F32), HEAD_DIM, 1).astype(x.dtype)


def _kv_variants(t):
    lo = _lane_lo(t.shape)
    z = jnp.zeros_like(t)
    a0 = jnp.where(lo, t, z)
    b1 = jnp.where(lo, z, t)
    b0 = _swap_halves(a0)
    a1 = _swap_halves(b1)
    return (a0, a1), (b0, b1), (a0 + b0, a1 + b1)


def _swa_masks(i):
    tq = lax.broadcasted_iota(I32, (WINDOW, WINDOW), 0)
    jj = lax.broadcasted_iota(I32, (WINDOW, WINDOW), 1)
    return (jj > tq) & (i > 0), jj <= tq


def _swa_specs():
    q = pl.BlockSpec((WINDOW, BRANCH), lambda i: (i, CB_SQ))
    kc = pl.BlockSpec((WINDOW, 128), lambda i: (i, CB_SK))
    kp = pl.BlockSpec((WINDOW, 128), lambda i: (jnp.maximum(i - 1, 0), CB_SK))
    vc = pl.BlockSpec((WINDOW, 128), lambda i: (i, CB_SV))
    vp = pl.BlockSpec((WINDOW, 128), lambda i: (jnp.maximum(i - 1, 0), CB_SV))
    bias = pl.BlockSpec((8, WINDOW, 2 * WINDOW), lambda i: (0, 0, 0))
    vec = pl.BlockSpec((1, 128), lambda i: (0, 0))
    return q, kc, kp, vc, vp, bias, vec


def _swa_fwd(pm, bias, sink, name):
    T = pm.shape[0]
    nb = T // WINDOW

    def body(q_ref, kc_ref, kp_ref, vc_ref, vp_ref, b_ref, s_ref, o_ref, m_ref):
        i = pl.program_id(0)
        mprev, mcur = _swa_masks(i)
        kcA, kcB, _ = _kv_variants(kc_ref[...])
        kpA, kpB, _ = _kv_variants(kp_ref[...])
        _, _, vcD = _kv_variants(vc_ref[...])
        _, _, vpD = _kv_variants(vp_ref[...])
        lo = _lane_lo((WINDOW, 128))
        sink_v = s_ref[...]
        mout = jnp.zeros((WINDOW, 128), F32)
        for p in range(4):
            jv = p // 2
            sl = slice(128 * p, 128 * p + 128)
            qp = q_ref[:, sl] * ATT_SCALE
            outs = []
            for par in range(2):
                h = 2 * p + par
                kpx = (kpA, kpB)[par][jv]
                kcx = (kcA, kcB)[par][jv]
                sp = jnp.where(mprev, _dot_nt(qp, kpx) + b_ref[h, :, 0:WINDOW], NEG)
                sc = jnp.where(mcur, _dot_nt(qp, kcx) + b_ref[h, :, WINDOW:2 * WINDOW], NEG)
                sk_h = sink_v[:, h:h + 1]
                m = jnp.maximum(jnp.maximum(jnp.max(sp, axis=-1, keepdims=True),
                                            jnp.max(sc, axis=-1, keepdims=True)), sk_h)
                ep = jnp.exp(sp - m)
                ec = jnp.exp(sc - m)
                den = (jnp.sum(ep, axis=-1, keepdims=True) + jnp.sum(ec, axis=-1, keepdims=True)
                       + jnp.exp(sk_h - m))
                inv = 1.0 / den
                outs.append(_dot_nn((ep * inv).astype(BF16), vpD[jv])
                            + _dot_nn((ec * inv).astype(BF16), vcD[jv]))
                mout = mout + _put_col((WINDOW, 128), h, m + jnp.log(den))
            o_ref[:, sl] = jnp.where(lo, outs[0], outs[1]).astype(BF16)
        m_ref[...] = mout

    q, kc, kp, vc, vp, bs, vec = _swa_specs()
    return pl.pallas_call(
        body, name=name, grid=(nb,),
        in_specs=[q, kc, kp, vc, vp, bs, vec],
        out_specs=[pl.BlockSpec((WINDOW, BRANCH), lambda i: (i, 0)),
                   pl.BlockSpec((WINDOW, 128), lambda i: (i, 0))],
        out_shape=[jax.ShapeDtypeStruct((T, BRANCH), BF16), jax.ShapeDtypeStruct((T, 128), F32)],
        compiler_params=_cp(("parallel",)),
    )(pm, pm, pm, pm, pm, bias, sink)


def _swa_bwd(pm, bias, sink, do, mlse, name):
    T = pm.shape[0]
    nb = T // WINDOW

    def fold(zz):
        return zz + pltpu.roll(zz, HEAD_DIM, 1)

    def body(q_ref, kc_ref, kp_ref, vc_ref, vp_ref, b_ref, s_ref, do_ref, m_ref,
             dq_ref, dkc_ref, dkp_ref, dvc_ref, dvp_ref, db_ref, ds_ref):
        i = pl.program_id(0)

        @pl.when(i == 0)
        def _():
            db_ref[...] = jnp.zeros_like(db_ref)
            ds_ref[...] = jnp.zeros_like(ds_ref)

        mprev, mcur = _swa_masks(i)
        kcA, kcB, kcD = _kv_variants(kc_ref[...])
        kpA, kpB, kpD = _kv_variants(kp_ref[...])
        vcA, vcB, _ = _kv_variants(vc_ref[...])
        vpA, vpB, _ = _kv_variants(vp_ref[...])
        lo = _lane_lo((WINDOW, 128))
        sink_v = s_ref[...]
        mv = m_ref[...]
        zk = jnp.zeros((WINDOW, 128), F32)
        zkp, zkc, zvp, zvc = [zk, zk], [zk, zk], [zk, zk], [zk, zk]
        dsink = jnp.zeros((1, 128), F32)
        for p in range(4):
            jv = p // 2
            sl = slice(128 * p, 128 * p + 128)
            qraw = q_ref[:, sl]
            qp = qraw * ATT_SCALE
            dop = do_ref[:, sl]
            dqs, mkp, mkc, mvp, mvc = [], [], [], [], []
            for par in range(2):
                h = 2 * p + par
                kpx = (kpA, kpB)[par][jv]
                kcx = (kcA, kcB)[par][jv]
                vpx = (vpA, vpB)[par][jv]
                vcx = (vcA, vcB)[par][jv]
                sp = jnp.where(mprev, _dot_nt(qp, kpx) + b_ref[h, :, 0:WINDOW], NEG)
                sc = jnp.where(mcur, _dot_nt(qp, kcx) + b_ref[h, :, WINDOW:2 * WINDOW], NEG)
                m_h = mv[:, h:h + 1]
                pp = jnp.exp(sp - m_h)
                pc = jnp.exp(sc - m_h)
                psink = jnp.exp(sink_v[:, h:h + 1] - m_h)
                dpp = _dot_nt(dop, vpx)
                dpc = _dot_nt(dop, vcx)
                delta = jnp.sum(pp * dpp, axis=-1, keepdims=True) + jnp.sum(pc * dpc, axis=-1, keepdims=True)
                dsp = pp * (dpp - delta)
                dsc = pc * (dpc - delta)
                db_ref[h, :, 0:WINDOW] += dsp
                db_ref[h, :, WINDOW:2 * WINDOW] += dsc
                dsink = dsink - _put_col((1, 128), h, jnp.sum(psink * delta, keepdims=True))
                dsp_b = dsp.astype(BF16)
                dsc_b = dsc.astype(BF16)
                dqs.append(_dot_nn(dsp_b, kpD[jv]) + _dot_nn(dsc_b, kcD[jv]))
                mkp.append(_dot_tn(dsp_b, qraw))
                mkc.append(_dot_tn(dsc_b, qraw))
                mvp.append(_dot_tn(pp.astype(BF16), dop))
                mvc.append(_dot_tn(pc.astype(BF16), dop))
            dq_ref[:, sl] = (jnp.where(lo, dqs[0], dqs[1]) * ATT_SCALE).astype(BF16)
            zkp[jv] = zkp[jv] + jnp.where(lo, mkp[0], mkp[1])
            zkc[jv] = zkc[jv] + jnp.where(lo, mkc[0], mkc[1])
            zvp[jv] = zvp[jv] + jnp.where(lo, mvp[0], mvp[1])
            zvc[jv] = zvc[jv] + jnp.where(lo, mvc[0], mvc[1])
        dkc_ref[...] = jnp.where(lo, fold(zkc[0]), fold(zkc[1])) * ATT_SCALE
        dkp_ref[...] = jnp.where(lo, fold(zkp[0]), fold(zkp[1])) * ATT_SCALE
        dvc_ref[...] = jnp.where(lo, fold(zvc[0]), fold(zvc[1]))
        dvp_ref[...] = jnp.where(lo, fold(zvp[0]), fold(zvp[1]))
        ds_ref[...] += dsink

    q, kc, kp, vc, vp, bs, vec = _swa_specs()
    own = pl.BlockSpec((WINDOW, BRANCH), lambda i: (i, 0))
    sm = pl.BlockSpec((WINDOW, 128), lambda i: (i, 0))
    f128 = jax.ShapeDtypeStruct((T, 128), F32)
    return pl.pallas_call(
        body, name=name, grid=(nb,),
        in_specs=[q, kc, kp, vc, vp, bs, vec, own, sm],
        out_specs=[own, sm, sm, sm, sm, bs, vec],
        out_shape=[jax.ShapeDtypeStruct((T, BRANCH), BF16), f128, f128, f128, f128,
                   jax.ShapeDtypeStruct((8, WINDOW, 2 * WINDOW), F32), jax.ShapeDtypeStruct((1, 128), F32)],
        compiler_params=_cp(("arbitrary",)),
    )(pm, pm, pm, pm, pm, bias, sink, do, mlse)


def _stacked_head(s, r):
    return 4 * (s // 2) + 2 * r + (s % 2)


def _swa2_bias(rel_bias, bucket, name):
    def body(rb_ref, bk_ref, o_ref):
        bkt = bk_ref[...]
        tq = lax.broadcasted_iota(I32, bkt.shape, 0)
        jj = lax.broadcasted_iota(I32, bkt.shape, 1)
        window = ((jj < WINDOW) & (jj > tq)) | ((jj >= WINDOW) & (jj - WINDOW <= tq))
        for s in range(4):
            for r in range(2):
                h = _stacked_head(s, r)

                def step(b, a, h=h):
                    return a + jnp.where(bkt == b, rb_ref[b, h], 0.0)
                val = lax.fori_loop(0, N_BUCKETS, step, jnp.zeros(bkt.shape, F32))
                o_ref[s, WINDOW * r:WINDOW * (r + 1), :] = jnp.where(window, val, NEG)

    return pl.pallas_call(
        body, name=name,
        in_specs=[pl.BlockSpec(memory_space=pltpu.SMEM), pl.BlockSpec(memory_space=pltpu.VMEM)],
        out_specs=pl.BlockSpec(memory_space=pltpu.VMEM),
        out_shape=jax.ShapeDtypeStruct((4, 2 * WINDOW, 2 * WINDOW), F32),
    )(rel_bias, bucket)


def _swa2_dbias_reduce(dbias, bucket, name):
    def body(d_ref, bk_ref, o_ref):
        bkt = bk_ref[...]
        rowi = lax.broadcasted_iota(I32, (N_BUCKETS, 128), 0)
        lane = lax.broadcasted_iota(I32, (N_BUCKETS, 128), 1)
        out = jnp.zeros((N_BUCKETS, 128), F32)
        for s in range(4):
            for r in range(2):
                h = _stacked_head(s, r)
                dv = d_ref[s, WINDOW * r:WINDOW * (r + 1), :]

                def step(b, a, dv=dv, h=h):
                    tot = jnp.sum(jnp.where(bkt == b, dv, 0.0), keepdims=True)
                    return a + jnp.where((rowi == b) & (lane == h), tot, 0.0)
                out = lax.fori_loop(0, N_BUCKETS, step, out)
        o_ref[...] = out

    return pl.pallas_call(
        body, name=name,
        in_specs=[pl.BlockSpec(memory_space=pltpu.VMEM), pl.BlockSpec(memory_space=pltpu.VMEM)],
        out_specs=pl.BlockSpec(memory_space=pltpu.VMEM),
        out_shape=jax.ShapeDtypeStruct((N_BUCKETS, 128), F32),
    )(dbias, bucket)


def _swa2_mask(i):
    tq = jnp.bitwise_and(lax.broadcasted_iota(I32, (2 * WINDOW, 2 * WINDOW), 0), WINDOW - 1)
    jj = lax.broadcasted_iota(I32, (2 * WINDOW, 2 * WINDOW), 1)
    return ((jj < WINDOW) & (jj > tq) & (i > 0)) | ((jj >= WINDOW) & (jj - WINDOW <= tq))


def _swa2_cols(vec, s):
    rows = lax.broadcasted_iota(I32, (2 * WINDOW, 1), 0)
    return jnp.where(rows < WINDOW, vec[:, _stacked_head(s, 0):_stacked_head(s, 0) + 1],
                     vec[:, _stacked_head(s, 1):_stacked_head(s, 1) + 1])


def _swa2_stack(ref, g):
    return jnp.concatenate([ref[:, 256 * g:256 * g + 128], ref[:, 256 * g + 128:256 * g + 256]], axis=0)


def _swa2_specs():
    q, kc, kp, vc, vp, _, vec = _swa_specs()
    bias = pl.BlockSpec((4, 2 * WINDOW, 2 * WINDOW), lambda i: (0, 0, 0))
    return q, kc, kp, vc, vp, bias, vec


def _swa2_fwd(pm, bias, sink, name):
    T = pm.shape[0]
    nb = T // WINDOW

    def body(q_ref, kc_ref, kp_ref, vc_ref, vp_ref, b_ref, s_ref, o_ref, m_ref):
        i = pl.program_id(0)
        mask = _swa2_mask(i)
        kcA, kcB, _ = _kv_variants(kc_ref[...])
        kpA, kpB, _ = _kv_variants(kp_ref[...])
        _, _, vcD = _kv_variants(vc_ref[...])
        _, _, vpD = _kv_variants(vp_ref[...])
        lo = _lane_lo((WINDOW, 128))
        sink_v = s_ref[...]
        mout = jnp.zeros((WINDOW, 128), F32)
        for g in range(2):
            qg = _swa2_stack(q_ref, g) * ATT_SCALE
            vband = jnp.concatenate([vpD[g], vcD[g]], axis=0)
            outs = []
            for par in range(2):
                s = 2 * g + par
                kband = jnp.concatenate([(kpA, kpB)[par][g], (kcA, kcB)[par][g]], axis=0)
                sc = jnp.where(mask, _dot_nt(qg, kband) + b_ref[s], NEG)
                sk = _swa2_cols(sink_v, s)
                m = jnp.maximum(jnp.max(sc, axis=-1, keepdims=True), sk)
                e = jnp.exp(sc - m)
                den = jnp.sum(e, axis=-1, keepdims=True) + jnp.exp(sk - m)
                outs.append(_dot_nn((e * (1.0 / den)).astype(BF16), vband))
                lse = m + jnp.log(den)
                mout = mout + _put_col((WINDOW, 128), _stacked_head(s, 0), lse[:WINDOW])
                mout = mout + _put_col((WINDOW, 128), _stacked_head(s, 1), lse[WINDOW:])
            for r in range(2):
                sl = slice(256 * g + 128 * r, 256 * g + 128 * r + 128)
                o_ref[:, sl] = jnp.where(lo, outs[0][WINDOW * r:WINDOW * (r + 1)],
                                         outs[1][WINDOW * r:WINDOW * (r + 1)]).astype(BF16)
        m_ref[...] = mout

    q, kc, kp, vc, vp, bs, vec = _swa2_specs()
    return pl.pallas_call(
        body, name=name, grid=(nb,),
        in_specs=[q, kc, kp, vc, vp, bs, vec],
        out_specs=[pl.BlockSpec((WINDOW, BRANCH), lambda i: (i, 0)),
                   pl.BlockSpec((WINDOW, 128), lambda i: (i, 0))],
        out_shape=[jax.ShapeDtypeStruct((T, BRANCH), BF16), jax.ShapeDtypeStruct((T, 128), F32)],
        compiler_params=_cp(("parallel",)),
    )(pm, pm, pm, pm, pm, bias, sink)


def _swa4_mask(no_prev):
    tq = jnp.bitwise_and(lax.broadcasted_iota(I32, (2 * WINDOW, 2 * WINDOW), 0), WINDOW - 1)
    jj = lax.broadcasted_iota(I32, (2 * WINDOW, 2 * WINDOW), 1)
    prev = (jj < WINDOW) & (jj > tq)
    if no_prev is not False:
        prev = prev & jnp.logical_not(no_prev)
    return prev | ((jj >= WINDOW) & (jj - WINDOW <= tq))


def _swa4_stack(ref, rows, g):
    return jnp.concatenate([ref[rows, 256 * g:256 * g + 128], ref[rows, 256 * g + 128:256 * g + 256]], axis=0)


def _swa4_specs():
    W2 = 2 * WINDOW
    q = pl.BlockSpec((W2, BRANCH), lambda i: (i, CB_SQ))
    kc = pl.BlockSpec((W2, 128), lambda i: (i, CB_SK))
    kp = pl.BlockSpec((WINDOW, 128), lambda i: (jnp.maximum(2 * i - 1, 0), CB_SK))
    vc = pl.BlockSpec((W2, 128), lambda i: (i, CB_SV))
    vp = pl.BlockSpec((WINDOW, 128), lambda i: (jnp.maximum(2 * i - 1, 0), CB_SV))
    bias = pl.BlockSpec((4, W2, W2), lambda i: (0, 0, 0))
    vec = pl.BlockSpec((1, 128), lambda i: (0, 0))
    return q, kc, kp, vc, vp, bias, vec


def _swa4_fwd(pm, bias, sink, name):
    T = pm.shape[0]
    nb = T // (2 * WINDOW)

    def body(q_ref, kc_ref, kp_ref, vc_ref, vp_ref, b_ref, s_ref, o_ref, m_ref):
        i = pl.program_id(0)
        lo = _lane_lo((WINDOW, 128))
        sink_v = s_ref[...]
        for u in range(2):
            rows = slice(WINDOW * u, WINDOW * (u + 1))
            mask = _swa4_mask(i == 0 if u == 0 else False)
            kcur, vcur = kc_ref[rows, :], vc_ref[rows, :]
            kprev = kp_ref[...] if u == 0 else kc_ref[0:WINDOW, :]
            vprev = vp_ref[...] if u == 0 else vc_ref[0:WINDOW, :]
            kcA, kcB, _ = _kv_variants(kcur)
            kpA, kpB, _ = _kv_variants(kprev)
            _, _, vcD = _kv_variants(vcur)
            _, _, vpD = _kv_variants(vprev)
            mout = jnp.zeros((WINDOW, 128), F32)
            for g in range(2):
                qg = _swa4_stack(q_ref, rows, g) * ATT_SCALE
                vband = jnp.concatenate([vpD[g], vcD[g]], axis=0)
                outs = []
                for par in range(2):
                    s = 2 * g + par
                    kband = jnp.concatenate([(kpA, kpB)[par][g], (kcA, kcB)[par][g]], axis=0)
                    sc = jnp.where(mask, _dot_nt(qg, kband) + b_ref[s], NEG)
                    sk = _swa2_cols(sink_v, s)
                    m = jnp.maximum(jnp.max(sc, axis=-1, keepdims=True), sk)
                    e = jnp.exp(sc - m)
                    den = jnp.sum(e, axis=-1, keepdims=True) + jnp.exp(sk - m)
                    outs.append(_dot_nn((e * (1.0 / den)).astype(BF16), vband))
                    lse = m + jnp.log(den)
                    mout = mout + _put_col((WINDOW, 128), _stacked_head(s, 0), lse[:WINDOW])
                    mout = mout + _put_col((WINDOW, 128), _stacked_head(s, 1), lse[WINDOW:])
                for r in range(2):
                    sl = slice(256 * g + 128 * r, 256 * g + 128 * r + 128)
                    o_ref[rows, sl] = jnp.where(lo, outs[0][WINDOW * r:WINDOW * (r + 1)],
                                                outs[1][WINDOW * r:WINDOW * (r + 1)]).astype(BF16)
            m_ref[rows, :] = mout

    q, kc, kp, vc, vp, bs, vec = _swa4_specs()
    return pl.pallas_call(
        body, name=name, grid=(nb,),
        in_specs=[q, kc, kp, vc, vp, bs, vec],
        out_specs=[pl.BlockSpec((2 * WINDOW, BRANCH), lambda i: (i, 0)),
                   pl.BlockSpec((2 * WINDOW, 128), lambda i: (i, 0))],
        out_shape=[jax.ShapeDtypeStruct((T, BRANCH), BF16), jax.ShapeDtypeStruct((T, 128), F32)],
        compiler_params=_cp(("parallel",)),
    )(pm, pm, pm, pm, pm, bias, sink)


def _swa4_bwd(pm, bias, sink, do, mlse, dproj, name):
    T = pm.shape[0]
    nb = T // (2 * WINDOW)

    def fold(zz):
        return zz + pltpu.roll(zz, HEAD_DIM, 1)

    def body(q_ref, kc_ref, kp_ref, vc_ref, vp_ref, b_ref, s_ref, do_ref, m_ref, buf_ref,
             dq_ref, dkc_ref, dkp_ref, dvc_ref, dvp_ref, db_ref, ds_ref):
        del buf_ref
        i = pl.program_id(0)

        @pl.when(i == 0)
        def _():
            db_ref[...] = jnp.zeros_like(db_ref)
            ds_ref[...] = jnp.zeros_like(ds_ref)

        lo = _lane_lo((WINDOW, 128))
        lo2 = _lane_lo((2 * WINDOW, 128))
        sink_v = s_ref[...]
        dsink = jnp.zeros((1, 128), F32)
        for u in range(2):
            rows = slice(WINDOW * u, WINDOW * (u + 1))
            mask = _swa4_mask(i == 0 if u == 0 else False)
            kcur, vcur = kc_ref[rows, :], vc_ref[rows, :]
            kprev = kp_ref[...] if u == 0 else kc_ref[0:WINDOW, :]
            vprev = vp_ref[...] if u == 0 else vc_ref[0:WINDOW, :]
            kcA, kcB, kcD = _kv_variants(kcur)
            kpA, kpB, kpD = _kv_variants(kprev)
            vcA, vcB, _ = _kv_variants(vcur)
            vpA, vpB, _ = _kv_variants(vprev)
            mv = m_ref[rows, :]
            zks, zvs = [], []
            for g in range(2):
                qraw = _swa4_stack(q_ref, rows, g)
                qg = qraw * ATT_SCALE
                dog = _swa4_stack(do_ref, rows, g)
                kband_d = jnp.concatenate([kpD[g], kcD[g]], axis=0)
                dqs, mks, mvs = [], [], []
                for par in range(2):
                    s = 2 * g + par
                    kband = jnp.concatenate([(kpA, kpB)[par][g], (kcA, kcB)[par][g]], axis=0)
                    vband = jnp.concatenate([(vpA, vpB)[par][g], (vcA, vcB)[par][g]], axis=0)
                    sc = jnp.where(mask, _dot_nt(qg, kband) + b_ref[s], NEG)
                    h0, h1 = _stacked_head(s, 0), _stacked_head(s, 1)
                    m_c = jnp.concatenate([mv[:, h0:h0 + 1], mv[:, h1:h1 + 1]], axis=0)
                    pr = jnp.exp(sc - m_c)
                    psink = jnp.exp(_swa2_cols(sink_v, s) - m_c)
                    dp = _dot_nt(dog, vband)
                    delta = jnp.sum(pr * dp, axis=-1, keepdims=True)
                    dsc = pr * (dp - delta)
                    db_ref[s] += dsc
                    sd = psink * delta
                    dsink = dsink - _put_col((1, 128), h0, jnp.sum(sd[:WINDOW], keepdims=True))
                    dsink = dsink - _put_col((1, 128), h1, jnp.sum(sd[WINDOW:], keepdims=True))
                    dsb = dsc.astype(BF16)
                    dqs.append(_dot_nn(dsb, kband_d))
                    mks.append(_dot_tn(dsb, qraw))
                    mvs.append(_dot_tn(pr.astype(BF16), dog))
                for r in range(2):
                    sl = slice(256 * g + 128 * r, 256 * g + 128 * r + 128)
                    dq_ref[rows, sl] = (jnp.where(lo, dqs[0][WINDOW * r:WINDOW * (r + 1)],
                                                  dqs[1][WINDOW * r:WINDOW * (r + 1)]) * ATT_SCALE).astype(BF16)
                zks.append(fold(jnp.where(lo2, mks[0], mks[1])))
                zvs.append(fold(jnp.where(lo2, mvs[0], mvs[1])))
            dk = jnp.where(lo2, zks[0], zks[1]) * ATT_SCALE
            dv = jnp.where(lo2, zvs[0], zvs[1])
            dkp_ref[rows, :] = dk[:WINDOW]
            dkc_ref[rows, :] = dk[WINDOW:]
            dvp_ref[rows, :] = dv[:WINDOW]
            dvc_ref[rows, :] = dv[WINDOW:]
        ds_ref[...] += dsink

    q, kc, kp, vc, vp, bs, vec = _swa4_specs()
    own = pl.BlockSpec((2 * WINDOW, BRANCH), lambda i: (i, 0))
    sm = pl.BlockSpec((2 * WINDOW, 128), lambda i: (i, 0))
    f128 = jax.ShapeDtypeStruct((T, 128), F32)
    return pl.pallas_call(
        body, name=name, grid=(nb,),
        in_specs=[q, kc, kp, vc, vp, bs, vec, own, sm, pl.BlockSpec(memory_space=pl.ANY)],
        out_specs=[pl.BlockSpec((2 * WINDOW, BRANCH), lambda i: (i, CB_SQ)), sm, sm, sm, sm, bs, vec],
        out_shape=[jax.ShapeDtypeStruct(dproj.shape, dproj.dtype), f128, f128, f128, f128,
                   jax.ShapeDtypeStruct((4, 2 * WINDOW, 2 * WINDOW), F32), jax.ShapeDtypeStruct((1, 128), F32)],
        input_output_aliases={9: 0},
        compiler_params=_cp(("arbitrary",)),
    )(pm, pm, pm, pm, pm, bias, sink, do, mlse, dproj)


SWA_ROWS = 32


def _swa3_fwd(pm, bias, sink, name):
    T = pm.shape[0]
    nb = T // WINDOW
    R = SWA_ROWS

    def body(q_ref, kc_ref, kp_ref, vc_ref, vp_ref, b_ref, s_ref, o_ref, m_ref, s_scr, p_scr):
        first = pl.program_id(0) == 0
        kill = (lax.broadcasted_iota(I32, (R, 2 * WINDOW), 1) < WINDOW) & first
        kcA, kcB, _ = _kv_variants(kc_ref[...])
        kpA, kpB, _ = _kv_variants(kp_ref[...])
        _, _, vcD = _kv_variants(vc_ref[...])
        _, _, vpD = _kv_variants(vp_ref[...])
        lo = _lane_lo((WINDOW, 128))
        sink_v = s_ref[...]
        m_ref[...] = jnp.zeros_like(m_ref)
        for g in range(2):
            qg = _swa2_stack(q_ref, g) * ATT_SCALE
            vband = jnp.concatenate([vpD[g], vcD[g]], axis=0)
            outs = []
            for par in range(2):
                s = 2 * g + par
                kband = jnp.concatenate([(kpA, kpB)[par][g], (kcA, kcB)[par][g]], axis=0)
                s_scr[s] = _dot_nt(qg, kband)
                for c in range(2 * WINDOW // R):
                    rows = slice(c * R, (c + 1) * R)
                    h = _stacked_head(s, c * R // WINDOW)
                    loc = slice(c * R % WINDOW, c * R % WINDOW + R)
                    sc = jnp.where(kill, NEG, s_scr[s, rows, :] + b_ref[s, rows, :])
                    skv = sink_v[:, h:h + 1]
                    m = jnp.maximum(jnp.max(sc, axis=-1, keepdims=True), skv)
                    e = jnp.exp(sc - m)
                    den = jnp.sum(e, axis=-1, keepdims=True) + jnp.exp(skv - m)
                    p_scr[s, rows, :] = (e * (1.0 / den)).astype(BF16)
                    m_ref[loc, h:h + 1] = m + jnp.log(den)
                outs.append(_dot_nn(p_scr[s], vband))
            for r in range(2):
                sl = slice(256 * g + 128 * r, 256 * g + 128 * r + 128)
                o_ref[:, sl] = jnp.where(lo, outs[0][WINDOW * r:WINDOW * (r + 1)],
                                         outs[1][WINDOW * r:WINDOW * (r + 1)]).astype(BF16)

    q, kc, kp, vc, vp, bs, vec = _swa2_specs()
    tile = (4, 2 * WINDOW, 2 * WINDOW)
    return pl.pallas_call(
        body, name=name, grid=(nb,),
        in_specs=[q, kc, kp, vc, vp, bs, vec],
        out_specs=[pl.BlockSpec((WINDOW, BRANCH), lambda i: (i, 0)),
                   pl.BlockSpec((WINDOW, 128), lambda i: (i, 0))],
        out_shape=[jax.ShapeDtypeStruct((T, BRANCH), BF16), jax.ShapeDtypeStruct((T, 128), F32)],
        scratch_shapes=[pltpu.VMEM(tile, F32), pltpu.VMEM(tile, BF16)],
        compiler_params=_cp(("parallel",)),
    )(pm, pm, pm, pm, pm, bias, sink)


def _swa3_bwd(pm, bias, sink, do, mlse, name):
    T = pm.shape[0]
    nb = T // WINDOW
    R = SWA_ROWS

    def fold(zz):
        return zz + pltpu.roll(zz, HEAD_DIM, 1)

    def body(q_ref, kc_ref, kp_ref, vc_ref, vp_ref, b_ref, s_ref, do_ref, m_ref,
             dq_ref, dkc_ref, dkp_ref, dvc_ref, dvp_ref, db_ref, ds_ref, s_scr, dp_scr, p_scr, ds_scr):
        first = pl.program_id(0) == 0

        @pl.when(first)
        def _():
            db_ref[...] = jnp.zeros_like(db_ref)
            ds_ref[...] = jnp.zeros_like(ds_ref)

        kill = (lax.broadcasted_iota(I32, (R, 2 * WINDOW), 1) < WINDOW) & first
        kcA, kcB, kcD = _kv_variants(kc_ref[...])
        kpA, kpB, kpD = _kv_variants(kp_ref[...])
        vcA, vcB, _ = _kv_variants(vc_ref[...])
        vpA, vpB, _ = _kv_variants(vp_ref[...])
        lo = _lane_lo((WINDOW, 128))
        lo2 = _lane_lo((2 * WINDOW, 128))
        sink_v = s_ref[...]
        dsink = [jnp.zeros((1, 1), F32) for _ in range(8)]
        zks, zvs = [], []
        for g in range(2):
            qraw = _swa2_stack(q_ref, g)
            qg = qraw * ATT_SCALE
            dog = _swa2_stack(do_ref, g)
            kband_d = jnp.concatenate([kpD[g], kcD[g]], axis=0)
            dqs, mks, mvs = [], [], []
            for par in range(2):
                s = 2 * g + par
                kband = jnp.concatenate([(kpA, kpB)[par][g], (kcA, kcB)[par][g]], axis=0)
                vband = jnp.concatenate([(vpA, vpB)[par][g], (vcA, vcB)[par][g]], axis=0)
                s_scr[s] = _dot_nt(qg, kband)
                dp_scr[s] = _dot_nt(dog, vband)
                for c in range(2 * WINDOW // R):
                    rows = slice(c * R, (c + 1) * R)
                    h = _stacked_head(s, c * R // WINDOW)
                    loc = slice(c * R % WINDOW, c * R % WINDOW + R)
                    sc = jnp.where(kill, NEG, s_scr[s, rows, :] + b_ref[s, rows, :])
                    m_c = m_ref[loc, h:h + 1]
                    pr = jnp.exp(sc - m_c)
                    dp = dp_scr[s, rows, :]
                    delta = jnp.sum(pr * dp, axis=-1, keepdims=True)
                    dsc = pr * (dp - delta)
                    db_ref[s, rows, :] += dsc
                    ds_scr[s, rows, :] = dsc.astype(BF16)
                    p_scr[s, rows, :] = pr.astype(BF16)
                    dsink[h] = dsink[h] - jnp.sum(jnp.exp(sink_v[:, h:h + 1] - m_c) * delta, keepdims=True)
                dqs.append(_dot_nn(ds_scr[s], kband_d))
                mks.append(_dot_tn(ds_scr[s], qraw))
                mvs.append(_dot_tn(p_scr[s], dog))
            for r in range(2):
                sl = slice(256 * g + 128 * r, 256 * g + 128 * r + 128)
                dq_ref[:, sl] = (jnp.where(lo, dqs[0][WINDOW * r:WINDOW * (r + 1)],
                                           dqs[1][WINDOW * r:WINDOW * (r + 1)]) * ATT_SCALE).astype(BF16)
            zks.append(fold(jnp.where(lo2, mks[0], mks[1])))
            zvs.append(fold(jnp.where(lo2, mvs[0], mvs[1])))
        dk = jnp.where(lo2, zks[0], zks[1]) * ATT_SCALE
        dv = jnp.where(lo2, zvs[0], zvs[1])
        dkp_ref[...] = dk[:WINDOW]
        dkc_ref[...] = dk[WINDOW:]
        dvp_ref[...] = dv[:WINDOW]
        dvc_ref[...] = dv[WINDOW:]
        tot = jnp.zeros((1, 128), F32)
        for h in range(8):
            tot = tot + _put_col((1, 128), h, dsink[h])
        ds_ref[...] += tot

    q, kc, kp, vc, vp, bs, vec = _swa2_specs()
    own = pl.BlockSpec((WINDOW, BRANCH), lambda i: (i, 0))
    sm = pl.BlockSpec((WINDOW, 128), lambda i: (i, 0))
    f128 = jax.ShapeDtypeStruct((T, 128), F32)
    tile = (4, 2 * WINDOW, 2 * WINDOW)
    return pl.pallas_call(
        body, name=name, grid=(nb,),
        in_specs=[q, kc, kp, vc, vp, bs, vec, own, sm],
        out_specs=[own, sm, sm, sm, sm, bs, vec],
        out_shape=[jax.ShapeDtypeStruct((T, BRANCH), BF16), f128, f128, f128, f128,
                   jax.ShapeDtypeStruct(tile, F32), jax.ShapeDtypeStruct((1, 128), F32)],
        scratch_shapes=[pltpu.VMEM(tile, F32), pltpu.VMEM(tile, F32), pltpu.VMEM(tile, BF16), pltpu.VMEM(tile, BF16)],
        compiler_params=_cp(("arbitrary",)),
    )(pm, pm, pm, pm, pm, bias, sink, do, mlse)


def _swa2_bwd(pm, bias, sink, do, mlse, name):
    T = pm.shape[0]
    nb = T // WINDOW

    def fold(zz):
        return zz + pltpu.roll(zz, HEAD_DIM, 1)

    def body(q_ref, kc_ref, kp_ref, vc_ref, vp_ref, b_ref, s_ref, do_ref, m_ref,
             dq_ref, dkc_ref, dkp_ref, dvc_ref, dvp_ref, db_ref, ds_ref):
        i = pl.program_id(0)

        @pl.when(i == 0)
        def _():
            db_ref[...] = jnp.zeros_like(db_ref)
            ds_ref[...] = jnp.zeros_like(ds_ref)

        mask = _swa2_mask(i)
        kcA, kcB, kcD = _kv_variants(kc_ref[...])
        kpA, kpB, kpD = _kv_variants(kp_ref[...])
        vcA, vcB, _ = _kv_variants(vc_ref[...])
        vpA, vpB, _ = _kv_variants(vp_ref[...])
        lo = _lane_lo((WINDOW, 128))
        lo2 = _lane_lo((2 * WINDOW, 128))
        sink_v = s_ref[...]
        mv = m_ref[...]
        dsink = jnp.zeros((1, 128), F32)
        zks, zvs = [], []
        for g in range(2):
            qraw = _swa2_stack(q_ref, g)
            qg = qraw * ATT_SCALE
            dog = _swa2_stack(do_ref, g)
            kband_d = jnp.concatenate([kpD[g], kcD[g]], axis=0)
            dqs, mks, mvs = [], [], []
            for par in range(2):
                s = 2 * g + par
                kband = jnp.concatenate([(kpA, kpB)[par][g], (kcA, kcB)[par][g]], axis=0)
                vband = jnp.concatenate([(vpA, vpB)[par][g], (vcA, vcB)[par][g]], axis=0)
                sc = jnp.where(mask, _dot_nt(qg, kband) + b_ref[s], NEG)
                h0, h1 = _stacked_head(s, 0), _stacked_head(s, 1)
                m_c = jnp.concatenate([mv[:, h0:h0 + 1], mv[:, h1:h1 + 1]], axis=0)
                pr = jnp.exp(sc - m_c)
                psink = jnp.exp(_swa2_cols(sink_v, s) - m_c)
                dp = _dot_nt(dog, vband)
                delta = jnp.sum(pr * dp, axis=-1, keepdims=True)
                dsc = pr * (dp - delta)
                db_ref[s] += dsc
                sd = psink * delta
                dsink = dsink - _put_col((1, 128), _stacked_head(s, 0), jnp.sum(sd[:WINDOW], keepdims=True))
                dsink = dsink - _put_col((1, 128), _stacked_head(s, 1), jnp.sum(sd[WINDOW:], keepdims=True))
                dsb = dsc.astype(BF16)
                dqs.append(_dot_nn(dsb, kband_d))
                mks.append(_dot_tn(dsb, qraw))
                mvs.append(_dot_tn(pr.astype(BF16), dog))
            for r in range(2):
                sl = slice(256 * g + 128 * r, 256 * g + 128 * r + 128)
                dq_ref[:, sl] = (jnp.where(lo, dqs[0][WINDOW * r:WINDOW * (r + 1)],
                                           dqs[1][WINDOW * r:WINDOW * (r + 1)]) * ATT_SCALE).astype(BF16)
            zks.append(fold(jnp.where(lo2, mks[0], mks[1])))
            zvs.append(fold(jnp.where(lo2, mvs[0], mvs[1])))
        dk = jnp.where(lo2, zks[0], zks[1]) * ATT_SCALE
        dv = jnp.where(lo2, zvs[0], zvs[1])
        dkp_ref[...] = dk[:WINDOW]
        dkc_ref[...] = dk[WINDOW:]
        dvp_ref[...] = dv[:WINDOW]
        dvc_ref[...] = dv[WINDOW:]
        ds_ref[...] += dsink

    q, kc, kp, vc, vp, bs, vec = _swa2_specs()
    own = pl.BlockSpec((WINDOW, BRANCH), lambda i: (i, 0))
    sm = pl.BlockSpec((WINDOW, 128), lambda i: (i, 0))
    f128 = jax.ShapeDtypeStruct((T, 128), F32)
    return pl.pallas_call(
        body, name=name, grid=(nb,),
        in_specs=[q, kc, kp, vc, vp, bs, vec, own, sm],
        out_specs=[own, sm, sm, sm, sm, bs, vec],
        out_shape=[jax.ShapeDtypeStruct((T, BRANCH), BF16), f128, f128, f128, f128,
                   jax.ShapeDtypeStruct((4, 2 * WINDOW, 2 * WINDOW), F32), jax.ShapeDtypeStruct((1, 128), F32)],
        compiler_params=_cp(("arbitrary",)),
    )(pm, pm, pm, pm, pm, bias, sink, do, mlse)


def _merge_fwd(pm, us, name):
    T = pm.shape[0]
    bt = _pick(T, (512, 256))

    def body(g0, g1, g2, u0, u1, u2, o_ref):
        acc = jax.nn.sigmoid(g0[...].astype(F32)) * u0[...].astype(F32)
        acc = acc + jax.nn.sigmoid(g1[...].astype(F32)) * u1[...].astype(F32)
        acc = acc + jax.nn.sigmoid(g2[...].astype(F32)) * u2[...].astype(F32)
        o_ref[...] = acc.astype(BF16)

    own = pl.BlockSpec((bt, D_MODEL), lambda i: (i, 0))
    gs = [pl.BlockSpec((bt, D_MODEL), lambda i, cb=cb: (i, cb)) for cb in CB_GATE]
    return pl.pallas_call(
        body, name=name, grid=(T // bt,), in_specs=gs + [own, own, own], out_specs=own,
        out_shape=jax.ShapeDtypeStruct((T, D_MODEL), BF16),
        compiler_params=_cp(("parallel",)),
    )(pm, pm, pm, *us)


def _merge_bwd(pm, us, dm, name):
    T = pm.shape[0]
    bt = _pick(T, (256,))

    def body(g0, g1, g2, u0, u1, u2, dm_ref, du0, du1, du2, dg_ref):
        dmv = dm_ref[...].astype(F32)
        for b, (g, u, du) in enumerate(((g0, u0, du0), (g1, u1, du1), (g2, u2, du2))):
            s = jax.nn.sigmoid(g[...].astype(F32))
            du[...] = (dmv * s).astype(BF16)
            dg_ref[:, D_MODEL * b:D_MODEL * (b + 1)] = (dmv * u[...].astype(F32) * s * (1.0 - s)).astype(BF16)

    own = pl.BlockSpec((bt, D_MODEL), lambda i: (i, 0))
    gs = [pl.BlockSpec((bt, D_MODEL), lambda i, cb=cb: (i, cb)) for cb in CB_GATE]
    act = jax.ShapeDtypeStruct((T, D_MODEL), BF16)
    return pl.pallas_call(
        body, name=name, grid=(T // bt,), in_specs=gs + [own, own, own, own],
        out_specs=[own, own, own, pl.BlockSpec((bt, 3 * D_MODEL), lambda i: (i, 0))],
        out_shape=[act, act, act, jax.ShapeDtypeStruct((T, PROJ_PAD), BF16)],
        compiler_params=_cp(("parallel",)),
    )(pm, pm, pm, *us, dm)


def _swiglu_fwd(ab, name):
    T = ab.shape[0]
    bt = _pick(T, (512, 256))

    def body(a_ref, b_ref, o_ref):
        a = a_ref[...].astype(F32)
        o_ref[...] = (a * jax.nn.sigmoid(a) * b_ref[...].astype(F32)).astype(BF16)

    return pl.pallas_call(
        body, name=name, grid=(T // bt,),
        in_specs=[pl.BlockSpec((bt, D_FF), lambda i: (i, 0)), pl.BlockSpec((bt, D_FF), lambda i: (i, 1))],
        out_specs=pl.BlockSpec((bt, D_FF), lambda i: (i, 0)),
        out_shape=jax.ShapeDtypeStruct((T, D_FF), BF16),
        compiler_params=_cp(("parallel",)),
    )(ab, ab)


def _swiglu_bwd(ab, dh, name):
    T = ab.shape[0]
    bt = _pick(T, (256,))

    def body(a_ref, b_ref, d_ref, o_ref):
        a = a_ref[...].astype(F32)
        b = b_ref[...].astype(F32)
        d = d_ref[...].astype(F32)
        s = jax.nn.sigmoid(a)
        o_ref[:, 0:D_FF] = (d * b * (s + a * s * (1.0 - s))).astype(BF16)
        o_ref[:, D_FF:2 * D_FF] = (d * a * s).astype(BF16)

    return pl.pallas_call(
        body, name=name, grid=(T // bt,),
        in_specs=[pl.BlockSpec((bt, D_FF), lambda i: (i, 0)), pl.BlockSpec((bt, D_FF), lambda i: (i, 1)),
                  pl.BlockSpec((bt, D_FF), lambda i: (i, 0))],
        out_specs=pl.BlockSpec((bt, 2 * D_FF), lambda i: (i, 0)),
        out_shape=jax.ShapeDtypeStruct((T, 2 * D_FF), BF16),
        compiler_params=_cp(("parallel",)),
    )(ab, ab, dh)


def _xattn_probs(q_ref, kv_ref, h):
    sl = slice(X_HEAD_DIM * h, X_HEAD_DIM * (h + 1))
    qh = q_ref[:, sl]
    kh = kv_ref[:, sl]
    vh = kv_ref[:, D_MODEL + X_HEAD_DIM * h:D_MODEL + X_HEAD_DIM * (h + 1)]
    s = _dot_nt(qh, kh) * X_SCALE
    e = jnp.exp(s - jnp.max(s, axis=-1, keepdims=True))
    return qh, kh, vh, e * (1.0 / jnp.sum(e, axis=-1, keepdims=True))


def _xattn_fwd(q, kv, name):
    T = q.shape[0]
    bq = _pick(T, (512, 256))

    def body(q_ref, kv_ref, o_ref):
        for h in range(X_HEADS):
            _, _, vh, p = _xattn_probs(q_ref, kv_ref, h)
            o_ref[:, X_HEAD_DIM * h:X_HEAD_DIM * (h + 1)] = _dot_nn(p.astype(BF16), vh).astype(BF16)

    own = pl.BlockSpec((bq, D_MODEL), lambda i: (i, 0))
    return pl.pallas_call(
        body, name=name, grid=(T // bq,),
        in_specs=[own, pl.BlockSpec((MEM_LEN, 2 * D_MODEL), lambda i: (0, 0))], out_specs=own,
        out_shape=jax.ShapeDtypeStruct((T, D_MODEL), BF16),
        compiler_params=_cp(("parallel",)),
    )(q, kv)


def _xattn_bwd(q, kv, do, name):
    T = q.shape[0]
    bq = _pick(T, (512, 256))

    def body(q_ref, kv_ref, do_ref, dq_ref, dkv_ref):
        @pl.when(pl.program_id(0) == 0)
        def _():
            dkv_ref[...] = jnp.zeros_like(dkv_ref)

        for h in range(X_HEADS):
            sl = slice(X_HEAD_DIM * h, X_HEAD_DIM * (h + 1))
            qh, kh, vh, p = _xattn_probs(q_ref, kv_ref, h)
            doh = do_ref[:, sl]
            dp = _dot_nt(doh, vh)
            ds = (p * (dp - jnp.sum(p * dp, axis=-1, keepdims=True)) * X_SCALE).astype(BF16)
            dq_ref[:, sl] = _dot_nn(ds, kh).astype(BF16)
            dkv_ref[:, sl] += _dot_tn(ds, qh)
            dkv_ref[:, D_MODEL + X_HEAD_DIM * h:D_MODEL + X_HEAD_DIM * (h + 1)] += _dot_tn(p.astype(BF16), doh)

    own = pl.BlockSpec((bq, D_MODEL), lambda i: (i, 0))
    kvs = pl.BlockSpec((MEM_LEN, 2 * D_MODEL), lambda i: (0, 0))
    return pl.pallas_call(
        body, name=name, grid=(T // bq,), in_specs=[own, kvs, own], out_specs=[own, kvs],
        out_shape=[jax.ShapeDtypeStruct((T, D_MODEL), BF16), jax.ShapeDtypeStruct((MEM_LEN, 2 * D_MODEL), F32)],
        compiler_params=_cp(("arbitrary",)),
    )(q, kv, do)


def _adamw(w, g, m, v, name):
    R, C = w.shape[0], w.shape[-1]
    rest = w.shape[1:]
    row_bytes = int(np.prod(rest[:-1], dtype=np.int64)) * (-(-C // 128) * 128) * 4
    cands = (1024, 512, 256, 128, 64, 32, 16, 8) if w.ndim == 2 else range(R, 0, -1)
    bt = R
    for cand in cands:
        if R % cand == 0 and cand * row_bytes <= (3 << 19):
            bt = cand
            break
    zeros = (0,) * len(rest)

    def body(w_ref, g_ref, m_ref, v_ref, d_ref, nm_ref, nv_ref):
        gv = g_ref[...]
        mn = ADAM_B1 * m_ref[...] + (1.0 - ADAM_B1) * gv
        vn = ADAM_B2 * v_ref[...] + (1.0 - ADAM_B2) * (gv * gv)
        m_hat = mn / (1.0 - ADAM_B1 ** ADAM_STEP)
        v_hat = vn / (1.0 - ADAM_B2 ** ADAM_STEP)
        d_ref[...] = -ADAM_LR * (m_hat / (jnp.sqrt(v_hat) + ADAM_EPS) + ADAM_WD * w_ref[...])
        nm_ref[...] = mn
        nv_ref[...] = vn

    blk = pl.BlockSpec((bt,) + tuple(rest), lambda i: (i,) + zeros)
    out = jax.ShapeDtypeStruct(w.shape, F32)
    return pl.pallas_call(
        body, name=name, grid=(R // bt,), in_specs=[blk] * 4, out_specs=[blk] * 3,
        out_shape=[out, out, out], compiler_params=_cp(("parallel",)),
    )(w, g, m, v)


ANY = pl.BlockSpec(memory_space=pl.ANY)


def _place():
    x, y, c = lax.axis_index("x"), lax.axis_index("y"), lax.axis_index("c")
    chips = [(1 - x, y), (x, 1 - y), (1 - x, 1 - y)]
    return x, y, c, chips


def _ag_packs(pack):
    R, Wd = pack.shape
    hrows = R // 2

    def body(p_ref, o_ref, send_sems, recv_sems, local_sem):
        x, y, c, chips = _place()
        me = 2 * x + y
        mine = pl.ds(c * hrows, hrows)
        theirs = pl.ds((1 - c) * hrows, hrows)
        local = pltpu.make_async_copy(p_ref, o_ref.at[me], local_sem)
        local.start()

        def copy(k, slab, rows, to, src=None):
            dst = o_ref.at[slab, rows]
            return pltpu.make_async_remote_copy(
                src_ref=dst if src is None else src, dst_ref=dst,
                send_sem=send_sems.at[k], recv_sem=recv_sems.at[k], device_id=to, device_id_type=MESH)

        first = [copy(k, me, mine, (px, py, c), src=p_ref.at[mine]) for k, (px, py) in enumerate(chips)]
        for cp in first:
            cp.start()
        passed = [copy(3 + k, 2 * px + py, mine, (x, y, 1 - c)) for k, (px, py) in enumerate(chips)]
        for k, (px, py) in enumerate(chips):
            copy(k, 2 * px + py, mine, (x, y, c)).wait_recv()
            passed[k].start()
        for k, (px, py) in enumerate(chips):
            copy(3 + k, 2 * px + py, theirs, (x, y, c)).wait_recv()
        for cp in first + passed:
            cp.wait_send()
        local.wait()

    return pl.pallas_call(
        body, name="ag_weights", in_specs=[ANY], out_specs=ANY,
        out_shape=jax.ShapeDtypeStruct((4, R, Wd), pack.dtype),
        scratch_shapes=[pltpu.SemaphoreType.DMA((6,)), pltpu.SemaphoreType.DMA((6,)), pltpu.SemaphoreType.DMA],
    )(pack)


def _rs_sibling(g4):
    _, R, Wd = g4.shape
    hrows = R // 2

    def body(g_ref, o_ref, send_sem, recv_sem):
        x, y, c, _ = _place()
        cp = pltpu.make_async_remote_copy(
            src_ref=g_ref.at[:, pl.ds((1 - c) * hrows, hrows)], dst_ref=o_ref,
            send_sem=send_sem, recv_sem=recv_sem, device_id=(x, y, 1 - c), device_id_type=MESH)
        cp.start()
        cp.wait()

    return pl.pallas_call(
        body, name="rs_sibling", in_specs=[ANY], out_specs=ANY,
        out_shape=jax.ShapeDtypeStruct((4, hrows, Wd), g4.dtype),
        scratch_shapes=[pltpu.SemaphoreType.DMA, pltpu.SemaphoreType.DMA],
    )(g4)


def _rs_add_pair(g4, sib, cidx, tag=""):
    _, R, Wd = g4.shape
    hrows = R // 2
    bt = _pick(hrows, ROW_BLOCKS)
    nb = hrows // bt

    def body(c_ref, a_ref, b_ref, o_ref):
        o_ref[...] = (a_ref[...].astype(F32) + b_ref[...].astype(F32)).astype(o_ref.dtype)

    grid_spec = pltpu.PrefetchScalarGridSpec(
        num_scalar_prefetch=1, grid=(4, nb),
        in_specs=[pl.BlockSpec((1, bt, Wd), lambda j, i, c: (j, c[0] * nb + i, 0)),
                  pl.BlockSpec((1, bt, Wd), lambda j, i, c: (j, i, 0))],
        out_specs=pl.BlockSpec((1, bt, Wd), lambda j, i, c: (j, i, 0)))
    return pl.pallas_call(
        body, name="rs_add_pair" + tag, grid_spec=grid_spec,
        out_shape=jax.ShapeDtypeStruct((4, hrows, Wd), g4.dtype),
        compiler_params=_cp(("parallel", "parallel")),
    )(cidx, g4, sib)


def _rs_chips(r4):
    _, hrows, Wd = r4.shape

    def body(r_ref, o_ref, send_sems, recv_sems, local_sem):
        x, y, c, chips = _place()
        me = 2 * x + y
        local = pltpu.make_async_copy(r_ref.at[me], o_ref.at[me], local_sem)
        local.start()
        sends = []
        for k, (px, py) in enumerate(chips):
            sends.append(pltpu.make_async_remote_copy(
                src_ref=r_ref.at[2 * px + py], dst_ref=o_ref.at[me],
                send_sem=send_sems.at[k], recv_sem=recv_sems.at[k], device_id=(px, py, c), device_id_type=MESH))
        for cp in sends:
            cp.start()
        for k, (px, py) in enumerate(chips):
            pltpu.make_async_remote_copy(
                src_ref=r_ref.at[me], dst_ref=o_ref.at[2 * px + py],
                send_sem=send_sems.at[k], recv_sem=recv_sems.at[k], device_id=(x, y, c),
                device_id_type=MESH).wait_recv()
        for cp in sends:
            cp.wait_send()
        local.wait()

    return pl.pallas_call(
        body, name="rs_chips", in_specs=[ANY], out_specs=ANY,
        out_shape=jax.ShapeDtypeStruct((4, hrows, Wd), r4.dtype),
        scratch_shapes=[pltpu.SemaphoreType.DMA((3,)), pltpu.SemaphoreType.DMA((3,)), pltpu.SemaphoreType.DMA],
    )(r4)


def _rs_add_chips(q4):
    _, hrows, Wd = q4.shape
    bt = _pick(hrows, (240, 120, 16))

    def body(q_ref, o_ref):
        o_ref[...] = ((q_ref[0].astype(F32) + q_ref[1].astype(F32)) + q_ref[2].astype(F32)) + q_ref[3].astype(F32)

    return pl.pallas_call(
        body, name="rs_add_chips", grid=(hrows // bt,),
        in_specs=[pl.BlockSpec((4, bt, Wd), lambda i: (0, i, 0))],
        out_specs=pl.BlockSpec((bt, Wd), lambda i: (i, 0)),
        out_shape=jax.ShapeDtypeStruct((hrows, Wd), F32),
        compiler_params=_cp(("parallel",)),
    )(q4)


def _rs_share(buf):
    R, Wd = buf.shape
    hrows = R // 2

    def body(b_ref, o_ref, send_sem, recv_sem):
        del b_ref
        x, y, c, _ = _place()
        mine = o_ref.at[pl.ds(c * hrows, hrows)]
        cp = pltpu.make_async_remote_copy(
            src_ref=mine, dst_ref=mine, send_sem=send_sem, recv_sem=recv_sem,
            device_id=(x, y, 1 - c), device_id_type=MESH)
        cp.start()
        theirs = o_ref.at[pl.ds((1 - c) * hrows, hrows)]
        pltpu.make_async_remote_copy(
            src_ref=theirs, dst_ref=theirs, send_sem=send_sem, recv_sem=recv_sem,
            device_id=(x, y, c), device_id_type=MESH).wait_recv()
        cp.wait_send()

    return pl.pallas_call(
        body, name="rs_share", in_specs=[ANY], out_specs=ANY, input_output_aliases={0: 0},
        out_shape=jax.ShapeDtypeStruct((R, Wd), buf.dtype),
        scratch_shapes=[pltpu.SemaphoreType.DMA, pltpu.SemaphoreType.DMA],
    )(buf)


def _allreduce_small(v, name="allreduce_small"):
    R, Wd = v.shape

    def body(v_ref, o_ref, buf, send_sems, recv_sems):
        x, y, c, _ = _place()
        me = 4 * x + 2 * y + c
        buf[me] = v_ref[...]
        sends = []
        for k in range(1, 8):
            peer = ((x + (k >> 2)) % 2, (y + ((k >> 1) & 1)) % 2, (c + (k & 1)) % 2)
            sends.append(pltpu.make_async_remote_copy(
                src_ref=v_ref, dst_ref=buf.at[me], send_sem=send_sems.at[k - 1], recv_sem=recv_sems.at[k - 1],
                device_id=peer, device_id_type=MESH))
        for cp in sends:
            cp.start()
        for k in range(1, 8):
            px, py, pc = (x + (k >> 2)) % 2, (y + ((k >> 1) & 1)) % 2, (c + (k & 1)) % 2
            pltpu.make_async_remote_copy(
                src_ref=v_ref, dst_ref=buf.at[4 * px + 2 * py + pc], send_sem=send_sems.at[k - 1],
                recv_sem=recv_sems.at[k - 1], device_id=(x, y, c), device_id_type=MESH).wait_recv()
        acc = buf[0]
        for d in range(1, 8):
            acc = acc + buf[d]
        o_ref[...] = acc
        for cp in sends:
            cp.wait_send()

    vm = pl.BlockSpec(memory_space=pltpu.VMEM)
    return pl.pallas_call(
        body, name=name, in_specs=[vm], out_specs=vm,
        out_shape=jax.ShapeDtypeStruct((R, Wd), F32),
        scratch_shapes=[pltpu.VMEM((8, R, Wd), F32), pltpu.SemaphoreType.DMA((7,)), pltpu.SemaphoreType.DMA((7,))],
    )(v)


def _neighbours():
    x, y, c = lax.axis_index("x"), lax.axis_index("y"), lax.axis_index("c")
    idx = (2 * x + y, 2 * (1 - x) + y, 2 * x + (1 - y), 2 * (1 - x) + (1 - y))
    return idx, (x, y, c), (1 - x, y, c), (x, 1 - y, c), (x, y, 1 - c)


def _place_own(pack, me_idx):
    R, Wd = pack.shape
    bt = _pick(R, (512, 256))

    def body(i_ref, p_ref, o_ref):
        o_ref[0] = p_ref[...]

    grid_spec = pltpu.PrefetchScalarGridSpec(
        num_scalar_prefetch=1, grid=(R // bt,),
        in_specs=[pl.BlockSpec((bt, Wd), lambda i, idx: (i, 0))],
        out_specs=pl.BlockSpec((1, bt, Wd), lambda i, idx: (idx[0], i, 0)))
    return pl.pallas_call(
        body, name="place_own", grid_spec=grid_spec,
        out_shape=jax.ShapeDtypeStruct((4, R, Wd), pack.dtype),
        compiler_params=_cp(("parallel",)),
    )(me_idx, pack)


def _ag_ring(buf):
    _, R, Wd = buf.shape
    hrows = R // 2
    qrows = hrows // 2

    def body(b_ref, o_ref, send_sems, recv_sems):
        del b_ref
        (me, ix, iy, idg), here, xn, yn, sib = _neighbours()
        c = here[2]
        base = c * hrows
        half = pl.ds(base, hrows)
        q0 = pl.ds(base, qrows)
        q1 = pl.ds(base + qrows, qrows)
        obase = (1 - c) * hrows

        def copy(k, slab, rows, to):
            dst = o_ref.at[slab, rows]
            return pltpu.make_async_remote_copy(
                src_ref=dst, dst_ref=dst,
                send_sem=send_sems.at[k], recv_sem=recv_sems.at[k], device_id=to, device_id_type=MESH)

        sends = [copy(0, me, half, xn), copy(1, me, half, yn)]
        for cp in sends:
            cp.start()
        landed = [(0, ix, half), (1, iy, half), (2, idg, q0), (3, idg, q1)]
        onward = {0: copy(2, ix, q0, yn), 1: copy(3, iy, q1, xn)}
        for k, slab, rows in landed:
            copy(k, slab, rows, here).wait_recv()
            if k in onward:
                onward[k].start()
                sends.append(onward[k])
            cp = copy(4 + k, slab, rows, sib)
            cp.start()
            sends.append(cp)
        theirs = [(4, ix, pl.ds(obase, hrows)), (5, iy, pl.ds(obase, hrows)),
                  (6, idg, pl.ds(obase, qrows)), (7, idg, pl.ds(obase + qrows, qrows))]
        for k, slab, rows in theirs:
            copy(k, slab, rows, here).wait_recv()
        for cp in sends:
            cp.wait_send()

    return pl.pallas_call(
        body, name="ag_weights", in_specs=[ANY], out_specs=ANY, input_output_aliases={0: 0},
        out_shape=jax.ShapeDtypeStruct((4, R, Wd), buf.dtype),
        scratch_shapes=[pltpu.SemaphoreType.DMA((8,)), pltpu.SemaphoreType.DMA((8,))],
    )(buf)


def _rs_diag(r4):
    _, hrows, Wd = r4.shape
    qrows = hrows // 2

    def body(r_ref, o_ref, send_sems, recv_sems):
        (me, ix, iy, idg), here, xn, yn, sib = _neighbours()
        pieces = [(0, pl.ds(0, qrows), xn), (1, pl.ds(qrows, qrows), yn)]
        sends = [pltpu.make_async_remote_copy(
            src_ref=r_ref.at[idg, rows], dst_ref=o_ref.at[k], send_sem=send_sems.at[k],
            recv_sem=recv_sems.at[k], device_id=to, device_id_type=MESH) for k, rows, to in pieces]
        for cp in sends:
            cp.start()
        for k, rows, to in pieces:
            pltpu.make_async_remote_copy(
                src_ref=r_ref.at[idg, rows], dst_ref=o_ref.at[k], send_sem=send_sems.at[k],
                recv_sem=recv_sems.at[k], device_id=here, device_id_type=MESH).wait_recv()
        for cp in sends:
            cp.wait_send()

    return pl.pallas_call(
        body, name="rs_diag", in_specs=[ANY], out_specs=ANY,
        out_shape=jax.ShapeDtypeStruct((2, qrows, Wd), r4.dtype),
        scratch_shapes=[pltpu.SemaphoreType.DMA((2,)), pltpu.SemaphoreType.DMA((2,))],
    )(r4)


def _rs_merge(r4, dg, nbr_idx, tag=""):
    _, hrows, Wd = r4.shape
    bt = _pick(hrows // 2, ROW_BLOCKS)
    nb = hrows // bt
    nq = nb // 2

    def body(i_ref, r_ref, d_ref, o_ref):
        w = pl.program_id(0)
        i = pl.program_id(1)
        merged = jnp.where(w == 0, i >= nq, i < nq)
        add = jnp.where(merged, d_ref[...].astype(F32), 0.0)
        o_ref[...] = (r_ref[...].astype(F32) + add).astype(o_ref.dtype)

    grid_spec = pltpu.PrefetchScalarGridSpec(
        num_scalar_prefetch=1, grid=(2, nb),
        in_specs=[pl.BlockSpec((1, bt, Wd), lambda w, i, idx: (idx[w], i, 0)),
                  pl.BlockSpec((1, bt, Wd), lambda w, i, idx: (1 - w, jnp.clip(i - (1 - w) * nq, 0, nq - 1), 0))],
        out_specs=pl.BlockSpec((1, bt, Wd), lambda w, i, idx: (w, i, 0)))
    return pl.pallas_call(
        body, name="rs_merge" + tag, grid_spec=grid_spec,
        out_shape=jax.ShapeDtypeStruct((2, hrows, Wd), r4.dtype),
        compiler_params=_cp(("parallel", "parallel")),
    )(nbr_idx, r4, dg)


def _rs_direct(m2):
    _, hrows, Wd = m2.shape

    def body(m_ref, o_ref, send_sems, recv_sems):
        _, here, xn, yn, sib = _neighbours()
        sends = [pltpu.make_async_remote_copy(
            src_ref=m_ref.at[k], dst_ref=o_ref.at[k], send_sem=send_sems.at[k], recv_sem=recv_sems.at[k],
            device_id=to, device_id_type=MESH) for k, to in ((0, xn), (1, yn))]
        for cp in sends:
            cp.start()
        for k in range(2):
            pltpu.make_async_remote_copy(
                src_ref=m_ref.at[k], dst_ref=o_ref.at[k], send_sem=send_sems.at[k], recv_sem=recv_sems.at[k],
                device_id=here, device_id_type=MESH).wait_recv()
        for cp in sends:
            cp.wait_send()

    return pl.pallas_call(
        body, name="rs_direct", in_specs=[ANY], out_specs=ANY,
        out_shape=jax.ShapeDtypeStruct((2, hrows, Wd), m2.dtype),
        scratch_shapes=[pltpu.SemaphoreType.DMA((2,)), pltpu.SemaphoreType.DMA((2,))],
    )(m2)


def _rs_final(r4, got, me_c, tag=""):
    _, hrows, Wd = r4.shape
    bt = _pick(hrows, ROW_BLOCKS)
    nb = hrows // bt

    def body(i_ref, r_ref, g_ref, o_ref):
        o_ref[...] = (r_ref[0].astype(F32) + g_ref[0].astype(F32)) + g_ref[1].astype(F32)

    grid_spec = pltpu.PrefetchScalarGridSpec(
        num_scalar_prefetch=1, grid=(nb,),
        in_specs=[pl.BlockSpec((1, bt, Wd), lambda i, idx: (idx[0], i, 0)),
                  pl.BlockSpec((2, bt, Wd), lambda i, idx: (0, i, 0))],
        out_specs=pl.BlockSpec((bt, Wd), lambda i, idx: (idx[1] * nb + i, 0)))
    return pl.pallas_call(
        body, name="rs_final" + tag, grid_spec=grid_spec,
        out_shape=jax.ShapeDtypeStruct((2 * hrows, Wd), F32),
        compiler_params=_cp(("parallel",)),
    )(me_c, r4, got)


SHARDED = (
    ("w_in", (2, 1024, 1730), 2),
    ("w_branch", (2, 3, 512, 256), 3),
    ("w_mix_out", (2, 256, 1024), 1),
    ("w_xq", (2, 256, 1024), 1),
    ("w_xkv", (2, 1024, 512), 2),
    ("w_xo", (2, 256, 1024), 1),
    ("w_ffn_gate", (2, 1024, 704), 2),
    ("w_ffn_up", (2, 1024, 704), 2),
    ("w_ffn_down", (2, 704, 1024), 1),
    ("conv_w", (2, 3, 128), 2),
)
PACK_W = 1024
PACK_ELEMS = sum(int(np.prod(s)) for _, s, _ in SHARDED)
PACK_ROWS = -(-PACK_ELEMS // (PACK_W * 1024)) * 1024


def _pack(parts, dtype):
    flat = jnp.concatenate([p.astype(dtype).reshape(-1) for p in parts]
                           + [jnp.zeros((PACK_ROWS * PACK_W - PACK_ELEMS,), dtype)])
    return flat.reshape(PACK_ROWS, PACK_W)


def _unpack(pack):
    flat = pack.reshape(-1)
    out, off = {}, 0
    for name, shape, _ in SHARDED:
        n = int(np.prod(shape))
        out[name] = flat[off:off + n].reshape(shape)
        off += n
    return out


SMALL = (
    ("mix_norm_g", (2, 1024)), ("xattn_norm_g", (2, 1024)), ("mem_norm_g", (2, 1024)),
    ("ffn_norm_g", (2, 1024)), ("final_norm_g", (1024,)),
    ("forget_bias", (2, 8)), ("sink", (2, 8)), ("rel_bias", (32, 8)),
)
SMALL_AND_CONV = SMALL + (("conv_w", (2, 3, 512)),)


def _small_rows(spec):
    rows = sum(int(np.prod(s)) // 128 if s[-1] % 128 == 0 else s[0] for _, s in spec)
    return -(-rows // 8) * 8


def _pack_small(vals, spec=SMALL):
    rows = []
    for name, shape in spec:
        v = vals[name].astype(F32)
        if shape[-1] % 128 == 0:
            rows.append(v.reshape(-1, 128))
        else:
            rows.append(jnp.pad(v, ((0, 0), (0, 120))))
    rows = jnp.concatenate(rows, axis=0)
    return jnp.pad(rows, ((0, _small_rows(spec) - rows.shape[0]), (0, 0)))


def _unpack_small(pack, spec=SMALL):
    out, off = {}, 0
    for name, shape in spec:
        if shape[-1] % 128 == 0:
            n = int(np.prod(shape)) // 128
            out[name] = pack[off:off + n].reshape(shape)
        else:
            n = shape[0]
            out[name] = pack[off:off + n, 0:8]
        off += n
    return out


W_IN_PERM = ((3848, 6920), (0, 3072), (3080, 3848), (3072, 3080))


def _perm_w_in(w):
    parts = [w[:, a:b] for a, b in W_IN_PERM]
    return jnp.concatenate(parts + [jnp.zeros((w.shape[0], PROJ_PAD - IN_COLS), w.dtype)], axis=1)


def _unperm_w_in(p):
    return jnp.concatenate([p[:, 3072:6144], p[:, 6912:6920], p[:, 6144:6912], p[:, 0:3072]], axis=1)


def _pad_row8(v):
    return jnp.pad(v.astype(F32).reshape(1, 8), ((0, 0), (0, 120)))


def _local_step(x, mem, tgt, W, rel_bias):
    T = x.shape[0]
    bucket = jnp.asarray(_bucket_table())
    bias = _swa2_bias(rel_bias, bucket, "swa_bias")
    saved = []
    for l in range(DEPTH):
        n = "l%d_" % l
        s = {"x0": x}
        wcat = W["w_in_p"][l]
        h = _rms_fwd(x, W["mix_norm_g"][l:l + 1], n + "mix_norm")
        pm = _mm(h, wcat[:, :PROJ_MAIN], "nn", BF16, n + "proj", bn=768)
        fg = _mm(h, wcat[:, PROJ_MAIN:], "nn", F32, n + "proj_fg")
        fb = _pad_row8(W["forget_bias"][l])
        c_col = _fox_gate_fwd(fg, fb, n + "fox_gate")
        c_row = c_col[:, 0:8].T
        cw = jnp.pad(W["conv_w"][l], ((0, 5), (0, 0)))
        y_conv = _conv_fwd(pm, cw, n + "conv")
        y_fox, lse = _fox2_fwd(pm, c_row, n + "fox")
        sink = _pad_row8(W["sink"][l])
        y_swa, mlse = _swa4_fwd(pm, bias, sink, n + "swa")
        ys = (y_conv, y_fox, y_swa)
        us = tuple(_mm(ys[b], W["w_branch"][l][b], "nn", BF16, n + "branch%d" % b) for b in range(3))
        merged = _merge_fwd(pm, us, n + "merge")
        x1 = _mm(merged, W["w_mix_out"][l], "nn", F32, n + "mix_out", res=x)
        xn1 = _rms_fwd(x1, W["xattn_norm_g"][l:l + 1], n + "xattn_norm")
        memn = _rms_fwd(mem, W["mem_norm_g"][l:l + 1], n + "mem_norm")
        qx = _mm(xn1, W["w_xq"][l], "nn", BF16, n + "xq")
        kv = _mm(memn, W["w_xkv"][l], "nn", BF16, n + "xkv")
        ox = _xattn_fwd(qx, kv, n + "xattn")
        x2 = _mm(ox, W["w_xo"][l], "nn", F32, n + "xo", res=x1)
        xn2 = _rms_fwd(x2, W["ffn_norm_g"][l:l + 1], n + "ffn_norm")
        ab = _mm(xn2, W["w_gu"][l], "nt", BF16, n + "ffn_in", bn=512)
        hm = _swiglu_fwd(ab, n + "swiglu")
        x3 = _mm(hm, W["w_ffn_down"][l], "nn", F32, n + "ffn_out", res=x2, bk=1408)
        s.update(h=h, pm=pm, fg=fg, fb=fb, c_col=c_col, c_row=c_row, cw=cw, ys=ys, lse=lse, sink=sink,
                 mlse=mlse, us=us, merged=merged, x1=x1, xn1=xn1, memn=memn, qx=qx, kv=kv, ox=ox,
                 x2=x2, xn2=xn2, ab=ab, hm=hm)
        saved.append(s)
        x = x3

    loss_row, dx, dg_final = _final_loss(x, W["final_norm_g"].reshape(1, D_MODEL), tgt, "final_loss")
    G = {name: [None] * DEPTH for name in
         ("mix_norm_g", "w_in_p", "forget_bias", "conv_w", "sink", "w_branch", "w_mix_out", "xattn_norm_g",
          "mem_norm_g", "w_xq", "w_xkv", "w_xo", "ffn_norm_g", "w_gu", "w_ffn_down")}
    dbias_tot = None
    for l in reversed(range(DEPTH)):
        n = "l%d_" % l
        s = saved[l]
        dhm = _mm(dx, W["w_ffn_down"][l], "nt", BF16, n + "d_hm", bn=1408)
        G["w_ffn_down"][l] = _mm(s["hm"], dx, "tn", BF16, n + "dw_down", bm=1408, bk=1024)
        dab = _swiglu_bwd(s["ab"], dhm, n + "d_swiglu")
        dxn2 = _mm(dab, W["w_gu"][l], "nn", BF16, n + "d_xn2", bk=1408)
        G["w_gu"][l] = _mm(dab, s["xn2"], "tn", BF16, n + "dw_gu", bm=512, bk=2048)
        dx, G["ffn_norm_g"][l] = _rms_bwd(s["x2"], W["ffn_norm_g"][l:l + 1], dxn2, dx, n + "d_ffn_norm")
        dox = _mm(dx, W["w_xo"][l], "nt", BF16, n + "d_ox")
        G["w_xo"][l] = _mm(s["ox"], dx, "tn", BF16, n + "dw_xo", bk=1024)
        dqx, dkv = _xattn_bwd(s["qx"], s["kv"], dox, n + "d_xattn")
        dxn1 = _mm(dqx, W["w_xq"][l], "nt", BF16, n + "d_xn1")
        G["w_xq"][l] = _mm(s["xn1"], dqx, "tn", BF16, n + "dw_xq", bk=2048)
        dmemn = _mm(dkv, W["w_xkv"][l], "nt", BF16, n + "d_memn")
        G["w_xkv"][l] = _mm(s["memn"], dkv, "tn", BF16, n + "dw_xkv")
        _, G["mem_norm_g"][l] = _rms_bwd(mem, W["mem_norm_g"][l:l + 1], dmemn, None, n + "d_mem_norm")
        dx, G["xattn_norm_g"][l] = _rms_bwd(s["x1"], W["xattn_norm_g"][l:l + 1], dxn1, dx, n + "d_xattn_norm")
        dmerged = _mm(dx, W["w_mix_out"][l], "nt", BF16, n + "d_merged")
        G["w_mix_out"][l] = _mm(s["merged"], dx, "tn", BF16, n + "dw_mix_out", bk=1024)
        du0, du1, du2, dproj = _merge_bwd(s["pm"], s["us"], dmerged, n + "d_merge")
        dus = (du0, du1, du2)
        dys = [_mm(dus[b], W["w_branch"][l][b], "nt", BF16, n + "d_y%d" % b) for b in range(3)]
        G["w_branch"][l] = jnp.stack(
            [_mm(s["ys"][b], dus[b], "tn", BF16, n + "dw_branch%d" % b, bk=2048) for b in range(3)])
        dproj, dcw = _conv_bwd(s["pm"], s["cw"], dys[0], dproj, n + "d_conv")
        G["conv_w"][l] = dcw[0:3]
        delta = _fox_delta(s["ys"][1], dys[1], n + "fox_delta")
        dproj, delta = _fox2_bwd_dq(s["pm"], dys[1], s["c_row"], s["lse"], delta, dproj, n + "d_fox_q")
        dproj, dc = _fox2_bwd_dkv(s["pm"], dys[1], s["c_col"], s["lse"][:, 0:8].T, delta[:, 0:8].T, dproj,
                                  n + "d_fox_kv")
        dfg, dfb = _fox_gate_bwd(dc, s["fg"], s["fb"], n + "d_fox_gate")
        G["forget_bias"][l] = dfb[0, 0:8]
        dproj, dkc, dkp, dvc, dvp, dbias, dsink = _swa4_bwd(s["pm"], bias, s["sink"], dys[2], s["mlse"], dproj,
                                                          n + "d_swa")
        G["sink"][l] = dsink[0, 0:8]
        dbias_tot = dbias if dbias_tot is None else dbias_tot + dbias
        zpad = jnp.zeros((WINDOW, 128), F32)
        dsk = dkc + jnp.concatenate([dkp[WINDOW:], zpad], axis=0)
        dsv = dvc + jnp.concatenate([dvp[WINDOW:], zpad], axis=0)
        tail = jnp.concatenate([dsk.astype(BF16), dsv.astype(BF16), dfg.astype(BF16)], axis=1)
        dproj = lax.dynamic_update_slice(dproj, tail, (0, PROJ_MAIN - 256))
        dh = _mm(dproj, W["w_in_p"][l], "nt", BF16, n + "d_h", bk=1408)
        G["w_in_p"][l] = _mm(s["h"], dproj, "tn", BF16, n + "dw_in", bn=640, bk=2048)
        dx, G["mix_norm_g"][l] = _rms_bwd(s["x0"], W["mix_norm_g"][l:l + 1], dh, dx, n + "d_mix_norm")
    drb = _swa2_dbias_reduce(dbias_tot, bucket, "swa_dbias")
    G["rel_bias"] = drb[:, 0:8]
    G["final_norm_g"] = dg_final.reshape(D_MODEL)
    return loss_row, dx, G


BIG = (
    ("w_in", (2048, 1730)), ("w_branch", (3072, 256)), ("w_mix_out", (512, 1024)), ("w_xq", (512, 1024)),
    ("w_xkv", (2048, 512)), ("w_xo", (512, 1024)), ("w_ffn_gate", (1408, 1024)), ("w_ffn_up", (1408, 1024)),
    ("w_ffn_down", (1408, 1024)),
)
TRANSPOSED = ("w_ffn_gate", "w_ffn_up")
ROW_BLOCKS = (512, 256, 352, 128, 16)


def _cast_place(w, me_idx, name):
    R, Wd = w.shape
    bt = _pick(R, ROW_BLOCKS)

    def body(i_ref, w_ref, o_ref):
        o_ref[0] = w_ref[...].astype(BF16)

    grid_spec = pltpu.PrefetchScalarGridSpec(
        num_scalar_prefetch=1, grid=(R // bt,),
        in_specs=[pl.BlockSpec((bt, Wd), lambda i, idx: (i, 0))],
        out_specs=pl.BlockSpec((1, bt, Wd), lambda i, idx: (idx[0], i, 0)))
    return pl.pallas_call(
        body, name=name, grid_spec=grid_spec, out_shape=jax.ShapeDtypeStruct((4, R, Wd), BF16),
        compiler_params=_cp(("parallel",)),
    )(me_idx, w)


def _remote(src, dst, sems, k, to):
    send_sems, recv_sems = sems
    return pltpu.make_async_remote_copy(src_ref=src, dst_ref=dst, send_sem=send_sems.at[k], recv_sem=recv_sems.at[k],
                                        device_id=to, device_id_type=MESH)


def _ag_ring_multi(bufs):
    n = len(bufs)

    def body(*refs):
        o = refs[n:2 * n]
        sems = refs[2 * n:]
        (me, ix, iy, idg), here, xn, yn, sib = _neighbours()
        c = here[2]

        def piece(t, k, other):
            h = bufs[t].shape[1] // 2
            q = h // 2
            base = ((1 - c) if other else c) * h
            return [(ix, pl.ds(base, h)), (iy, pl.ds(base, h)), (idg, pl.ds(base, q)), (idg, pl.ds(base + q, q))][k]

        def copy(t, k, slab, rows, to):
            ref = o[t].at[slab, rows]
            return _remote(ref, ref, sems, 8 * t + k, to)

        sends = []

        def go(cp):
            cp.start()
            sends.append(cp)

        for t in range(n):
            h = bufs[t].shape[1] // 2
            go(copy(t, 0, me, pl.ds(c * h, h), xn))
            go(copy(t, 1, me, pl.ds(c * h, h), yn))
        for k in range(4):
            for t in range(n):
                slab, rows = piece(t, k, False)
                copy(t, k, slab, rows, here).wait_recv()
                if k == 0:
                    go(copy(t, 2, ix, piece(t, 2, False)[1], yn))
                if k == 1:
                    go(copy(t, 3, iy, piece(t, 3, False)[1], xn))
                go(copy(t, 4 + k, slab, rows, sib))
        for k in range(4):
            for t in range(n):
                slab, rows = piece(t, k, True)
                copy(t, 4 + k, slab, rows, here).wait_recv()
        for cp in sends:
            cp.wait_send()

    return pl.pallas_call(
        body, name="ag_weights", in_specs=[ANY] * n, out_specs=[ANY] * n,
        input_output_aliases={t: t for t in range(n)},
        out_shape=[jax.ShapeDtypeStruct(b.shape, b.dtype) for b in bufs],
        scratch_shapes=[pltpu.SemaphoreType.DMA((8 * n,)), pltpu.SemaphoreType.DMA((8 * n,))],
    )(*bufs)


def _exchange_multi(srcs, out_shapes, plan, name, aliased=False):
    n = len(srcs)

    def body(*refs):
        ins, outs, sems = refs[:n], refs[n:2 * n], refs[2 * n:]
        places = _neighbours()
        here = places[1]
        per = [plan(t, ins[t], outs[t], places) for t in range(n)]
        width = max(len(p) for p in per)
        started = []
        for t in range(n):
            for k, (src, dst, to, land) in enumerate(per[t]):
                cp = _remote(src, dst, sems, width * t + k, to)
                cp.start()
                started.append(cp)
        for t in range(n):
            for k, (src, dst, to, land) in enumerate(per[t]):
                _remote(land, land, sems, width * t + k, here).wait_recv()
        for cp in started:
            cp.wait_send()

    nsem = 2 * n
    return pl.pallas_call(
        body, name=name, in_specs=[ANY] * n, out_specs=[ANY] * n,
        input_output_aliases={t: t for t in range(n)} if aliased else {},
        out_shape=[jax.ShapeDtypeStruct(s, d) for s, d in out_shapes],
        scratch_shapes=[pltpu.SemaphoreType.DMA((nsem,)), pltpu.SemaphoreType.DMA((nsem,))],
    )(*srcs)


def _rs_sibling_multi(gs):
    def plan(t, g, o, places):
        (_, here, _, _, sib) = places
        h = gs[t].shape[1] // 2
        return [(g.at[:, pl.ds((1 - here[2]) * h, h)], o, sib, o)]

    return _exchange_multi(gs, [((4, g.shape[1] // 2, g.shape[2]), g.dtype) for g in gs], plan, "rs_sibling")


def _rs_diag_multi(rs):
    def plan(t, r, o, places):
        ((_, _, _, idg), _, xn, yn, _) = places
        q = rs[t].shape[1] // 2
        return [(r.at[idg, pl.ds(0, q)], o.at[0], xn, o.at[0]), (r.at[idg, pl.ds(q, q)], o.at[1], yn, o.at[1])]

    return _exchange_multi(rs, [((2, r.shape[1] // 2, r.shape[2]), r.dtype) for r in rs], plan, "rs_diag")


def _rs_direct_multi(ms):
    def plan(t, m, o, places):
        (_, _, xn, yn, _) = places
        return [(m.at[0], o.at[0], xn, o.at[0]), (m.at[1], o.at[1], yn, o.at[1])]

    return _exchange_multi(ms, [(m.shape, m.dtype) for m in ms], plan, "rs_direct")


def _rs_share_multi(bufs):
    def plan(t, b, o, places):
        (_, here, _, _, sib) = places
        h = bufs[t].shape[0] // 2
        mine = o.at[pl.ds(here[2] * h, h)]
        return [(mine, mine, sib, o.at[pl.ds((1 - here[2]) * h, h)])]

    return _exchange_multi(bufs, [(b.shape, b.dtype) for b in bufs], plan, "rs_share", aliased=True)


def kernel(x, mem, mix_norm_g, w_in, forget_bias, conv_w, sink, w_branch, w_mix_out, rel_bias, xattn_norm_g, mem_norm_g, w_xq, w_xkv, w_xo, ffn_norm_g, w_ffn_gate, w_ffn_up, w_ffn_down, final_norm_g, loss_target, m_mix_norm_g, m_w_in, m_forget_bias, m_conv_w, m_sink, m_w_branch, m_w_mix_out, m_rel_bias, m_xattn_norm_g, m_mem_norm_g, m_w_xq, m_w_xkv, m_w_xo, m_ffn_norm_g, m_w_ffn_gate, m_w_ffn_up, m_w_ffn_down, m_final_norm_g, v_mix_norm_g, v_w_in, v_forget_bias, v_conv_w, v_sink, v_w_branch, v_w_mix_out, v_rel_bias, v_xattn_norm_g, v_mem_norm_g, v_w_xq, v_w_xkv, v_w_xo, v_ffn_norm_g, v_w_ffn_gate, v_w_ffn_up, v_w_ffn_down, v_final_norm_g):
    order = ("mix_norm_g", "w_in", "forget_bias", "conv_w", "sink", "w_branch", "w_mix_out", "rel_bias",
             "xattn_norm_g", "mem_norm_g", "w_xq", "w_xkv", "w_xo", "ffn_norm_g", "w_ffn_gate", "w_ffn_up",
             "w_ffn_down", "final_norm_g")
    w_sh = dict(zip(order, (mix_norm_g, w_in, forget_bias, conv_w, sink, w_branch, w_mix_out, rel_bias,
                            xattn_norm_g, mem_norm_g, w_xq, w_xkv, w_xo, ffn_norm_g, w_ffn_gate, w_ffn_up,
                            w_ffn_down, final_norm_g)))
    m_sh = dict(zip(order, (m_mix_norm_g, m_w_in, m_forget_bias, m_conv_w, m_sink, m_w_branch, m_w_mix_out,
                            m_rel_bias, m_xattn_norm_g, m_mem_norm_g, m_w_xq, m_w_xkv, m_w_xo, m_ffn_norm_g,
                            m_w_ffn_gate, m_w_ffn_up, m_w_ffn_down, m_final_norm_g)))
    v_sh = dict(zip(order, (v_mix_norm_g, v_w_in, v_forget_bias, v_conv_w, v_sink, v_w_branch, v_w_mix_out,
                            v_rel_bias, v_xattn_norm_g, v_mem_norm_g, v_w_xq, v_w_xkv, v_w_xo, v_ffn_norm_g,
                            v_w_ffn_gate, v_w_ffn_up, v_w_ffn_down, v_final_norm_g)))

    xi, yi, ci = lax.axis_index("x"), lax.axis_index("y"), lax.axis_index("c")
    as_idx = lambda *v: jnp.stack([jnp.asarray(t, I32) for t in v])
    me = 2 * xi + yi
    big = [name for name, _ in BIG]
    two_d = dict(BIG)
    two_d["conv_w"] = (6, 128)

    def slab(a, name):
        return (jnp.swapaxes(a, 1, 2) if name in TRANSPOSED else a).reshape(two_d[name])

    def unslab(a, name):
        shape = w_sh[name].shape
        if name in TRANSPOSED:
            return jnp.swapaxes(a.reshape(shape[0], shape[2], shape[1]), 1, 2)
        return a.reshape(shape)

    gathered = dict(zip(big, _ag_ring_multi(
        [_cast_place(slab(w_sh[name], name), as_idx(me), "place_" + name) for name in big])))
    conv_part = lax.dynamic_update_slice_in_dim(jnp.zeros((DEPTH, 3, BRANCH), F32), 0.5 * conv_w, 128 * me, axis=2)
    conv_full = _allreduce_small(conv_part.reshape(-1, 128), "allgather_conv").reshape(DEPTH, 3, BRANCH)

    def lay(name, l):
        g = gathered[name]
        return g.reshape(4, DEPTH, g.shape[1] // DEPTH, g.shape[2])[:, l]

    def by_cols(name, l):
        g = lay(name, l)
        return jnp.moveaxis(g, 0, 1).reshape(g.shape[1], 4 * g.shape[2])

    def by_rows(name, l):
        g = lay(name, l)
        return g.reshape(4 * g.shape[1], g.shape[2])

    W = {k: w_sh[k] for k in ("mix_norm_g", "forget_bias", "sink", "xattn_norm_g", "mem_norm_g",
                              "ffn_norm_g", "final_norm_g")}
    W["conv_w"] = conv_full
    W["w_in_p"] = [_perm_w_in(by_cols("w_in", l)) for l in range(DEPTH)]
    W["w_gu"] = [jnp.concatenate([by_rows("w_ffn_gate", l), by_rows("w_ffn_up", l)], axis=0) for l in range(DEPTH)]
    W["w_xkv"] = [by_cols("w_xkv", l) for l in range(DEPTH)]
    W["w_branch"] = [jnp.transpose(lay("w_branch", l).reshape(4, 3, BRANCH, 256), (1, 2, 0, 3)).reshape(3, BRANCH, D_MODEL)
                     for l in range(DEPTH)]
    for k in ("w_mix_out", "w_xq", "w_xo", "w_ffn_down"):
        W[k] = [by_rows(k, l) for l in range(DEPTH)]
    loss_row, dx, G = _local_step(x[0], mem[0], loss_target[0], W, rel_bias)

    def to_cols(g):
        return jnp.moveaxis(g.reshape(g.shape[0], 4, g.shape[1] // 4), 1, 0)

    def to_rows(g):
        return g.reshape(4, g.shape[0] // 4, g.shape[1])

    per_layer = {
        "w_in": [to_cols(_unperm_w_in(G["w_in_p"][l])) for l in range(DEPTH)],
        "w_branch": [jnp.transpose(G["w_branch"][l].reshape(3, BRANCH, 4, 256), (2, 0, 1, 3)).reshape(4, 3 * BRANCH, 256)
                     for l in range(DEPTH)],
        "w_mix_out": [to_rows(g) for g in G["w_mix_out"]],
        "w_xq": [to_rows(g) for g in G["w_xq"]],
        "w_xkv": [to_cols(g) for g in G["w_xkv"]],
        "w_xo": [to_rows(g) for g in G["w_xo"]],
        "w_ffn_gate": [to_rows(G["w_gu"][l][:D_FF]) for l in range(DEPTH)],
        "w_ffn_up": [to_rows(G["w_gu"][l][D_FF:]) for l in range(DEPTH)],
        "w_ffn_down": [to_rows(g) for g in G["w_ffn_down"]],
    }
    g4 = [jnp.concatenate(per_layer[name], axis=1).astype(BF16) for name in big]
    sib = _rs_sibling_multi(g4)
    pair = [_rs_add_pair(g4[t], sib[t], as_idx(ci), "_" + big[t]) for t in range(len(big))]
    diag = _rs_diag_multi(pair)
    nbrs = as_idx(2 * (1 - xi) + yi, 2 * xi + (1 - yi))
    merged = [_rs_merge(pair[t], diag[t], nbrs, "_" + big[t]) for t in range(len(big))]
    got = _rs_direct_multi(merged)
    reduced = _rs_share_multi([_rs_final(pair[t], got[t], as_idx(me, ci), "_" + big[t]) for t in range(len(big))])

    small = _unpack_small(_allreduce_small(_pack_small({
        "mix_norm_g": jnp.concatenate(G["mix_norm_g"], axis=0),
        "xattn_norm_g": jnp.concatenate(G["xattn_norm_g"], axis=0),
        "mem_norm_g": jnp.concatenate(G["mem_norm_g"], axis=0),
        "ffn_norm_g": jnp.concatenate(G["ffn_norm_g"], axis=0),
        "final_norm_g": G["final_norm_g"],
        "forget_bias": jnp.stack(G["forget_bias"]),
        "sink": jnp.stack(G["sink"]),
        "rel_bias": G["rel_bias"],
        "conv_w": jnp.stack(G["conv_w"]),
    }, SMALL_AND_CONV)), SMALL_AND_CONV)
    grads = {name: unslab(reduced[t], name) for t, name in enumerate(big)}
    grads.update(small)
    grads["conv_w"] = lax.dynamic_slice_in_dim(small["conv_w"], 128 * me, 128, axis=2)

    sm_names = [name for name, _ in SMALL]
    sd, sm_, sv_ = _adamw(_pack_small({k: w_sh[k] for k in sm_names}), _pack_small({k: grads[k] for k in sm_names}),
                          _pack_small({k: m_sh[k] for k in sm_names}), _pack_small({k: v_sh[k] for k in sm_names}),
                          "adamw_small")
    delta, new_m, new_v = _unpack_small(sd), _unpack_small(sm_), _unpack_small(sv_)
    for t, name in enumerate(big + ["conv_w"]):
        if name == "w_in":
            to3 = lambda a: jnp.transpose(a, (2, 0, 1))
            d, nm, nv = _adamw(to3(w_sh[name]), to3(grads[name]), to3(m_sh[name]), to3(v_sh[name]), "adamw_w_in")
            delta[name], new_m[name], new_v[name] = (jnp.transpose(a, (1, 2, 0)) for a in (d, nm, nv))
            continue
        g2 = reduced[t] if t < len(big) else slab(grads[name], name)
        d, nm, nv = _adamw(slab(w_sh[name], name), g2, slab(m_sh[name], name), slab(v_sh[name], name), "adamw_" + name)
        delta[name], new_m[name], new_v[name] = unslab(d, name), unslab(nm, name), unslab(nv, name)

    loss = lax.psum(loss_row[0, 0], ("x", "y", "c"))
    return (loss, dx[None], *[grads[k] for k in order], *[delta[k] for k in order],
            *[new_m[k] for k in order], *[new_v[k] for k in order])
```

```python
import math

import numpy as np
import jax
import jax.numpy as jnp
from jax import lax
from jax.experimental import pallas as pl
from jax.experimental.pallas import tpu as pltpu

F32 = jnp.float32
BF16 = jnp.bfloat16
I32 = jnp.int32

D_MODEL = 1024
DEPTH = 2
HEAD_DIM = 64
BRANCH = 512
N_BUCKETS = 32
WINDOW = 128
MEM_LEN = 256
X_HEADS = 4
X_HEAD_DIM = 256
D_FF = 2816
IN_COLS = 6920
PROJ_MAIN = 6912
PROJ_PAD = 7040
RMS_EPS = 1e-6
NEG = -1e30
ATT_SCALE = 0.125
X_SCALE = 0.0625

ADAM_LR = 0.001
ADAM_B1 = 0.9
ADAM_B2 = 0.999
ADAM_EPS = 1e-08
ADAM_WD = 0.01
ADAM_STEP = 10

VMEM_LIMIT = 48 * 1024 * 1024
MESH = pl.DeviceIdType.MESH

CB_GATE = (0, 1, 2)
CB_B, CB_C, CB_U, CB_FQ, CB_FK, CB_FV, CB_SQ = 6, 7, 8, 9, 10, 11, 12
CB_SK, CB_SV = 52, 53


def _cp(sem):
    return pltpu.CompilerParams(dimension_semantics=sem, vmem_limit_bytes=VMEM_LIMIT)


def _pick(n, prefs):
    for p in prefs:
        if p <= n and n % p == 0:
            return p
    return n


def _dot(a, b, dims):
    return lax.dot_general(a, b, (dims, ((), ())), preferred_element_type=F32)


def _dot_nn(a, b):
    return _dot(a, b, ((1,), (0,)))


def _dot_nt(a, b):
    return _dot(a, b, ((1,), (1,)))


def _dot_tn(a, b):
    return _dot(a, b, ((0,), (0,)))


def _mm(a, b, mode, out_dtype, name, res=None, bm=1024, bn=1024, bk=1024):
    if mode == "nn":
        (M, K), (K2, N) = a.shape, b.shape
    elif mode == "nt":
        (M, K), (N, K2) = a.shape, b.shape
    else:
        (K, M), (K2, N) = a.shape, b.shape
    assert K == K2, (name, a.shape, b.shape)
    bm = _pick(M, (bm, 1024, 512, 256, 128))
    bn = _pick(N, (bn, 1024, 768, 640, 512, 384, 256, 128))
    bk = _pick(K, (bk, 1024, 768, 640, 512, 384, 256, 128))
    nk = K // bk
    if mode == "tn":
        a_spec = pl.BlockSpec((bk, bm), lambda i, j, k: (k, i))
    else:
        a_spec = pl.BlockSpec((bm, bk), lambda i, j, k: (i, k))
    if mode == "nt":
        b_spec = pl.BlockSpec((bn, bk), lambda i, j, k: (j, k))
    else:
        b_spec = pl.BlockSpec((bk, bn), lambda i, j, k: (k, j))
    dims = {"nn": ((1,), (0,)), "nt": ((1,), (1,)), "tn": ((0,), (0,))}[mode]
    o_spec = pl.BlockSpec((bm, bn), lambda i, j, k: (i, j))
    has_res = res is not None

    def body(*refs):
        if has_res:
            a_ref, b_ref, r_ref, o_ref = refs[:4]
            scr = refs[4:]
        else:
            a_ref, b_ref, o_ref = refs[:3]
            r_ref = None
            scr = refs[3:]
        p = _dot(a_ref[...].astype(BF16), b_ref[...].astype(BF16), dims)
        if nk == 1:
            if has_res:
                p = p + r_ref[...]
            o_ref[...] = p.astype(out_dtype)
        else:
            acc = scr[0]
            k = pl.program_id(2)

            @pl.when(k == 0)
            def _():
                acc[...] = p

            @pl.when(k > 0)
            def _():
                acc[...] += p

            @pl.when(k == nk - 1)
            def _():
                r = acc[...]
                if has_res:
                    r = r + r_ref[...]
                o_ref[...] = r.astype(out_dtype)

    ins = [a, b] + ([res] if has_res else [])
    in_specs = [a_spec, b_spec] + ([o_spec] if has_res else [])
    return pl.pallas_call(
        body, name=name, grid=(M // bm, N // bn, nk),
        in_specs=in_specs, out_specs=o_spec,
        out_shape=jax.ShapeDtypeStruct((M, N), out_dtype),
        scratch_shapes=[pltpu.VMEM((bm, bn), F32)] if nk > 1 else [],
        compiler_params=_cp(("parallel", "parallel", "arbitrary")),
    )(*ins)


def _rms_fwd(x, g, name):
    T, Dm = x.shape
    bt = _pick(T, (512, 256))

    def body(x_ref, g_ref, o_ref):
        xv = x_ref[...]
        r = lax.rsqrt(jnp.mean(xv * xv, axis=-1, keepdims=True) + RMS_EPS)
        o_ref[...] = ((xv * r) * g_ref[...]).astype(BF16)

    return pl.pallas_call(
        body, name=name, grid=(T // bt,),
        in_specs=[pl.BlockSpec((bt, Dm), lambda i: (i, 0)), pl.BlockSpec((1, Dm), lambda i: (0, 0))],
        out_specs=pl.BlockSpec((bt, Dm), lambda i: (i, 0)),
        out_shape=jax.ShapeDtypeStruct((T, Dm), BF16),
        compiler_params=_cp(("parallel",)),
    )(x, g)


def _rms_bwd(x, g, dh, dres, name):
    T, Dm = x.shape
    bt = _pick(T, (512, 256))
    want_dx = dres is not None

    def body(*refs):
        if want_dx:
            x_ref, g_ref, dh_ref, dr_ref, dx_ref, dg_ref = refs
        else:
            x_ref, g_ref, dh_ref, dg_ref = refs
        xv = x_ref[...]
        r = lax.rsqrt(jnp.mean(xv * xv, axis=-1, keepdims=True) + RMS_EPS)
        xh = xv * r
        dhv = dh_ref[...].astype(F32)

        @pl.when(pl.program_id(0) == 0)
        def _():
            dg_ref[...] = jnp.zeros_like(dg_ref)

        dg_ref[...] += jnp.sum(dhv * xh, axis=0, keepdims=True)
        if want_dx:
            dyg = dhv * g_ref[...]
            dx_ref[...] = dr_ref[...] + r * (dyg - xh * jnp.mean(dyg * xh, axis=-1, keepdims=True))

    row = pl.BlockSpec((bt, Dm), lambda i: (i, 0))
    vec = pl.BlockSpec((1, Dm), lambda i: (0, 0))
    if want_dx:
        return pl.pallas_call(
            body, name=name, grid=(T // bt,),
            in_specs=[row, vec, row, row], out_specs=[row, vec],
            out_shape=[jax.ShapeDtypeStruct((T, Dm), F32), jax.ShapeDtypeStruct((1, Dm), F32)],
            compiler_params=_cp(("arbitrary",)),
        )(x, g, dh, dres)
    return None, pl.pallas_call(
        body, name=name, grid=(T // bt,),
        in_specs=[row, vec, row], out_specs=vec,
        out_shape=jax.ShapeDtypeStruct((1, Dm), F32),
        compiler_params=_cp(("arbitrary",)),
    )(x, g, dh)


def _final_loss(x, g, tgt, name):
    T, Dm = x.shape
    bt = _pick(T, (512, 256))

    def body(x_ref, g_ref, t_ref, loss_ref, dx_ref, dg_ref):
        xv = x_ref[...]
        r = lax.rsqrt(jnp.mean(xv * xv, axis=-1, keepdims=True) + RMS_EPS)
        xh = xv * r
        gv = g_ref[...]
        err = xh * gv - t_ref[...]

        @pl.when(pl.program_id(0) == 0)
        def _():
            dg_ref[...] = jnp.zeros_like(dg_ref)
            loss_ref[...] = jnp.zeros_like(loss_ref)

        loss_ref[...] += jnp.sum(err * err) * (0.5 / Dm)
        dy = err * (1.0 / Dm)
        dg_ref[...] += jnp.sum(dy * xh, axis=0, keepdims=True)
        dyg = dy * gv
        dx_ref[...] = r * (dyg - xh * jnp.mean(dyg * xh, axis=-1, keepdims=True))

    row = pl.BlockSpec((bt, Dm), lambda i: (i, 0))
    vec = pl.BlockSpec((1, Dm), lambda i: (0, 0))
    return pl.pallas_call(
        body, name=name, grid=(T // bt,),
        in_specs=[row, vec, row],
        out_specs=[pl.BlockSpec((1, 128), lambda i: (0, 0)), row, vec],
        out_shape=[jax.ShapeDtypeStruct((1, 128), F32), jax.ShapeDtypeStruct((T, Dm), F32),
                   jax.ShapeDtypeStruct((1, Dm), F32)],
        compiler_params=_cp(("arbitrary",)),
    )(x, g, tgt)


HALO = 16


def _shift_down(z, zprev, s):
    rolled = pltpu.roll(z, s, 0)
    hp = pltpu.roll(zprev, s, 0)
    row = lax.broadcasted_iota(I32, hp.shape, 0)
    top = jnp.where(row < s, hp, rolled[:HALO])
    return jnp.concatenate([top, rolled[HALO:]], axis=0)


def _shift_up(z, znext, s):
    n = z.shape[0]
    rolled = pltpu.roll(z, n - s, 0)
    hn = pltpu.roll(znext, HALO - s, 0)
    row = lax.broadcasted_iota(I32, hn.shape, 0)
    bot = jnp.where(row >= HALO - s, hn, rolled[n - HALO:])
    return jnp.concatenate([rolled[:n - HALO], bot], axis=0)


def _conv_fwd(pm, cw, name):
    T = pm.shape[0]
    bt = _pick(T, (512, 256))
    hb = bt // HALO

    def body(b_ref, c_ref, u_ref, cp_ref, up_ref, w_ref, o_ref):
        i = pl.program_id(0)
        z = c_ref[...].astype(F32) * u_ref[...].astype(F32)
        zp = cp_ref[...].astype(F32) * up_ref[...].astype(F32)
        zp = jnp.where(i > 0, zp, 0.0)
        w = w_ref[...]
        y = w[2:3] * z + w[1:2] * _shift_down(z, zp, 1) + w[0:1] * _shift_down(z, zp, 2)
        o_ref[...] = (b_ref[...].astype(F32) * y).astype(BF16)

    def col(cb):
        return pl.BlockSpec((bt, BRANCH), lambda i: (i, cb))

    def prev(cb):
        return pl.BlockSpec((HALO, BRANCH), lambda i: (jnp.maximum(i * hb - 1, 0), cb))

    return pl.pallas_call(
        body, name=name, grid=(T // bt,),
        in_specs=[col(CB_B), col(CB_C), col(CB_U), prev(CB_C), prev(CB_U),
                  pl.BlockSpec((8, BRANCH), lambda i: (0, 0))],
        out_specs=pl.BlockSpec((bt, BRANCH), lambda i: (i, 0)),
        out_shape=jax.ShapeDtypeStruct((T, BRANCH), BF16),
        compiler_params=_cp(("parallel",)),
    )(pm, pm, pm, pm, pm, cw)


def _conv_bwd(pm, cw, dy, dproj, name):
    T = pm.shape[0]
    bt = _pick(T, (512, 256))
    hb = bt // HALO
    nb = T // bt
    last_h = T // HALO - 1

    def body(b_ref, c_ref, u_ref, cp_ref, up_ref, bn_ref, dy_ref, dyn_ref, w_ref, buf_ref,
             dp_ref, dw_ref):
        del buf_ref
        db_ref = dp_ref.at[:, 0:BRANCH]
        dc_ref = dp_ref.at[:, BRANCH:2 * BRANCH]
        du_ref = dp_ref.at[:, 2 * BRANCH:3 * BRANCH]
        i = pl.program_id(0)
        cv = c_ref[...].astype(F32)
        uv = u_ref[...].astype(F32)
        bv = b_ref[...].astype(F32)
        z = cv * uv
        zp = jnp.where(i > 0, cp_ref[...].astype(F32) * up_ref[...].astype(F32), 0.0)
        w = w_ref[...]
        z1 = _shift_down(z, zp, 1)
        z2 = _shift_down(z, zp, 2)
        yc = w[2:3] * z + w[1:2] * z1 + w[0:1] * z2
        dyv = dy_ref[...].astype(F32)
        db_ref[...] = (dyv * yc).astype(BF16)
        g = dyv * bv
        gn = jnp.where(i < nb - 1, dyn_ref[...].astype(F32) * bn_ref[...].astype(F32), 0.0)
        dz = w[2:3] * g + w[1:2] * _shift_up(g, gn, 1) + w[0:1] * _shift_up(g, gn, 2)
        dc_ref[...] = (dz * uv).astype(BF16)
        du_ref[...] = (dz * cv).astype(BF16)

        @pl.when(i == 0)
        def _():
            dw_ref[...] = jnp.zeros_like(dw_ref)

        dw_ref[0:1, :] += jnp.sum(g * z2, axis=0, keepdims=True)
        dw_ref[1:2, :] += jnp.sum(g * z1, axis=0, keepdims=True)
        dw_ref[2:3, :] += jnp.sum(g * z, axis=0, keepdims=True)

    def col(cb):
        return pl.BlockSpec((bt, BRANCH), lambda i: (i, cb))

    def prev(cb):
        return pl.BlockSpec((HALO, BRANCH), lambda i: (jnp.maximum(i * hb - 1, 0), cb))

    def nxt(cb):
        return pl.BlockSpec((HALO, BRANCH), lambda i: (jnp.minimum((i + 1) * hb, last_h), cb))

    own = pl.BlockSpec((bt, BRANCH), lambda i: (i, 0))
    w_spec = pl.BlockSpec((8, BRANCH), lambda i: (0, 0))
    return pl.pallas_call(
        body, name=name, grid=(nb,),
        in_specs=[col(CB_B), col(CB_C), col(CB_U), prev(CB_C), prev(CB_U), nxt(CB_B), own,
                  pl.BlockSpec((HALO, BRANCH), lambda i: (jnp.minimum((i + 1) * hb, last_h), 0)), w_spec,
                  pl.BlockSpec(memory_space=pl.ANY)],
        out_specs=[pl.BlockSpec((bt, 3 * BRANCH), lambda i: (i, 2)), w_spec],
        out_shape=[jax.ShapeDtypeStruct(dproj.shape, dproj.dtype), jax.ShapeDtypeStruct((8, BRANCH), F32)],
        input_output_aliases={9: 0},
        compiler_params=_cp(("arbitrary",)),
    )(pm, pm, pm, pm, pm, pm, dy, dy, cw, dproj)


def _log_sigmoid(z):
    return jnp.minimum(z, 0.0) - jnp.log(1.0 + jnp.exp(-jnp.abs(z)))


def _fox_gate_fwd(fg, fb, name):
    T = fg.shape[0]
    bt = _pick(T, (256,))

    def body(f_ref, b_ref, c_ref, carry):
        @pl.when(pl.program_id(0) == 0)
        def _():
            carry[...] = jnp.zeros_like(carry)

        xv = _log_sigmoid(f_ref[...] + b_ref[...])
        row = lax.broadcasted_iota(I32, xv.shape, 0)
        s = 1
        while s < bt:
            xv = xv + jnp.where(row >= s, pltpu.roll(xv, s, 0), 0.0)
            s *= 2
        xv = xv + carry[...]
        c_ref[...] = xv
        carry[...] = xv[bt - 1:bt, :]

    blk = pl.BlockSpec((bt, 128), lambda i: (i, 0))
    return pl.pallas_call(
        body, name=name, grid=(T // bt,),
        in_specs=[blk, pl.BlockSpec((1, 128), lambda i: (0, 0))],
        out_specs=blk, out_shape=jax.ShapeDtypeStruct((T, 128), F32),
        scratch_shapes=[pltpu.VMEM((1, 128), F32)],
        compiler_params=_cp(("arbitrary",)),
    )(fg, fb)


def _fox_gate_bwd(dc, fg, fb, name):
    T = fg.shape[0]
    bt = _pick(T, (256,))
    nb = T // bt

    def body(d_ref, f_ref, b_ref, o_ref, db_ref, carry):
        @pl.when(pl.program_id(0) == 0)
        def _():
            carry[...] = jnp.zeros_like(carry)
            db_ref[...] = jnp.zeros_like(db_ref)

        xv = d_ref[...]
        row = lax.broadcasted_iota(I32, xv.shape, 0)
        s = 1
        while s < bt:
            xv = xv + jnp.where(row < bt - s, pltpu.roll(xv, bt - s, 0), 0.0)
            s *= 2
        xv = xv + carry[...]
        carry[...] = xv[0:1, :]
        z = f_ref[...] + b_ref[...]
        dz = xv * (1.0 / (1.0 + jnp.exp(z)))
        o_ref[...] = dz
        db_ref[...] += jnp.sum(dz, axis=0, keepdims=True)

    blk = pl.BlockSpec((bt, 128), lambda i: (nb - 1 - i, 0))
    vec = pl.BlockSpec((1, 128), lambda i: (0, 0))
    return pl.pallas_call(
        body, name=name, grid=(nb,),
        in_specs=[blk, blk, vec], out_specs=[blk, vec],
        out_shape=[jax.ShapeDtypeStruct((T, 128), F32), jax.ShapeDtypeStruct((1, 128), F32)],
        scratch_shapes=[pltpu.VMEM((1, 128), F32)],
        compiler_params=_cp(("arbitrary",)),
    )(dc, fg, fb)


def _lane_lo(shape):
    return lax.broadcasted_iota(I32, shape, 1) < HEAD_DIM


def _put_col(shape, h, col):
    lane = lax.broadcasted_iota(I32, shape, 1)
    return jnp.where(lane == h, col, 0.0)


def _fox_delta(o, do, name):
    T = o.shape[0]
    bt = _pick(T, (512, 256))

    def body(o_ref, d_ref, out_ref):
        prod = o_ref[...].astype(F32) * d_ref[...].astype(F32)
        out = jnp.zeros((bt, 128), F32)
        for h in range(8):
            out = out + _put_col((bt, 128), h, jnp.sum(prod[:, 64 * h:64 * h + 64], axis=-1, keepdims=True))
        out_ref[...] = out

    blk = pl.BlockSpec((bt, BRANCH), lambda i: (i, 0))
    return pl.pallas_call(
        body, name=name, grid=(T // bt,), in_specs=[blk, blk],
        out_specs=pl.BlockSpec((bt, 128), lambda i: (i, 0)),
        out_shape=jax.ShapeDtypeStruct((T, 128), F32),
        compiler_params=_cp(("parallel",)),
    )(o, do)


FOX_ROWS = 32


def _chunk_loop(n, chunk):
    for r in range(n):
        chunk(r)


def _tree(op, xs):
    xs = list(xs)
    while len(xs) > 1:
        xs = [op(xs[i], xs[i + 1]) if i + 1 < len(xs) else xs[i] for i in range(0, len(xs), 2)]
    return xs[0]


def _masked_halves(t):
    lo = _lane_lo(t.shape)
    z = jnp.zeros_like(t)
    return jnp.where(lo, t, z), jnp.where(lo, z, t)


def _fox2_fwd(pm, c_row, name):
    T = pm.shape[0]
    bq = _pick(T, (512, 256))
    bk = bq
    nq = T // bq
    R = FOX_ROWS
    ng = bk // 128

    def body(q_ref, k_ref, v_ref, ck_ref, o_ref, lse_ref, acc, m_s, l_s, a_s, s_scr, p_scr):
        qi = pl.program_id(0)
        ki = pl.program_id(1)

        @pl.when(ki == 0)
        def _():
            acc[...] = jnp.zeros_like(acc)
            m_s[...] = jnp.full_like(m_s, NEG)
            l_s[...] = jnp.zeros_like(l_s)

        def block(masked):
            qlo = _lane_lo((bq, 128))
            for p in range(4):
                sl = slice(128 * p, 128 * p + 128)
                qp = q_ref[:, sl] * ATT_SCALE
                vp = v_ref[:, sl]
                ks = _masked_halves(k_ref[:, sl])
                pvs = []
                for j in range(2):
                    h = 2 * p + j
                    s_scr[j] = _dot_nt(qp, ks[j])

                    def chunk(r, h=h, j=j):
                        r0 = r * R
                        rows = pl.ds(r0, R)
                        sc = [s_scr[j, rows, 128 * g:128 * g + 128] - ck_ref[h:h + 1, 128 * g:128 * g + 128]
                              for g in range(ng)]
                        if masked:
                            rid = lax.broadcasted_iota(I32, (R, 128), 0) + r0
                            cid = lax.broadcasted_iota(I32, (R, 128), 1)
                            sc = [jnp.where(cid + 128 * g <= rid, sc[g], NEG) for g in range(ng)]
                        m_old = m_s[h, rows, :]
                        m_new = jnp.maximum(m_old, jnp.max(_tree(jnp.maximum, sc), axis=-1, keepdims=True))
                        alpha = jnp.exp(m_old - m_new)
                        pe = [jnp.exp(sc[g] - m_new) for g in range(ng)]
                        l_s[h, rows, :] = alpha * l_s[h, rows, :] + _tree(jnp.add, pe)
                        m_s[h, rows, :] = m_new
                        a_s[j, rows, :] = alpha
                        for g in range(ng):
                            p_scr[j, rows, 128 * g:128 * g + 128] = pe[g].astype(BF16)

                    _chunk_loop(bq // R, chunk)
                    pvs.append(_dot_nn(p_scr[j], vp))
                acc[:, sl] = jnp.where(qlo, a_s[0], a_s[1]) * acc[:, sl] + jnp.where(qlo, pvs[0], pvs[1])

        @pl.when(ki < qi)
        def _():
            block(False)

        @pl.when(ki == qi)
        def _():
            block(True)

        @pl.when(ki == nq - 1)
        def _():
            qlo = _lane_lo((bq, 128))
            lse = jnp.zeros((bq, 128), F32)
            for p in range(4):
                sl = slice(128 * p, 128 * p + 128)
                l0 = jnp.sum(l_s[2 * p], axis=-1, keepdims=True)
                l1 = jnp.sum(l_s[2 * p + 1], axis=-1, keepdims=True)
                o_ref[:, sl] = (acc[:, sl] / jnp.where(qlo, l0, l1)).astype(BF16)
                lse = lse + _put_col((bq, 128), 2 * p, m_s[2 * p][:, 0:1] + jnp.log(l0))
                lse = lse + _put_col((bq, 128), 2 * p + 1, m_s[2 * p + 1][:, 0:1] + jnp.log(l1))
            lse_ref[...] = lse

    return pl.pallas_call(
        body, name=name, grid=(nq, nq),
        in_specs=[pl.BlockSpec((bq, BRANCH), lambda i, k: (i, CB_FQ)),
                  pl.BlockSpec((bk, BRANCH), lambda i, k: (jnp.minimum(k, i), CB_FK)),
                  pl.BlockSpec((bk, BRANCH), lambda i, k: (jnp.minimum(k, i), CB_FV)),
                  pl.BlockSpec((8, bk), lambda i, k: (0, jnp.minimum(k, i)))],
        out_specs=[pl.BlockSpec((bq, BRANCH), lambda i, k: (i, 0)),
                   pl.BlockSpec((bq, 128), lambda i, k: (i, 0))],
        out_shape=[jax.ShapeDtypeStruct((T, BRANCH), BF16), jax.ShapeDtypeStruct((T, 128), F32)],
        scratch_shapes=[pltpu.VMEM((bq, BRANCH), F32), pltpu.VMEM((8, bq, 128), F32),
                        pltpu.VMEM((8, bq, 128), F32), pltpu.VMEM((2, bq, 128), F32),
                        pltpu.VMEM((2, bq, bk), F32), pltpu.VMEM((2, bq, bk), BF16)],
        compiler_params=_cp(("parallel", "arbitrary")),
    )(pm, pm, pm, c_row)


def _fox2_bwd_dq(pm, do, c_row, lse, delta, dproj, name):
    T = pm.shape[0]
    bq = _pick(T, (512, 256))
    bk = bq
    nq = T // bq
    R = FOX_ROWS
    ng = bk // 128

    def body(q_ref, k_ref, v_ref, do_ref, ck_ref, lse_ref, dl_ref, buf_ref, dq_ref, dl2_ref,
             acc, e_s, s_scr, dp_scr, ds_scr):
        del buf_ref
        qi = pl.program_id(0)
        ki = pl.program_id(1)

        @pl.when(ki == 0)
        def _():
            acc[...] = jnp.zeros_like(acc)
            e_s[...] = jnp.zeros_like(e_s)

        def block(masked):
            qlo = _lane_lo((bq, 128))
            for p in range(4):
                sl = slice(128 * p, 128 * p + 128)
                qp = q_ref[:, sl] * ATT_SCALE
                kp = k_ref[:, sl]
                dop = do_ref[:, sl]
                ks = _masked_halves(kp)
                vs = _masked_halves(v_ref[:, sl])
                dqs = []
                for j in range(2):
                    h = 2 * p + j
                    s_scr[...] = _dot_nt(qp, ks[j])
                    dp_scr[...] = _dot_nt(dop, vs[j])

                    def chunk(r, h=h):
                        r0 = r * R
                        rows = pl.ds(r0, R)
                        lse_c = lse_ref[rows, h:h + 1]
                        dl_c = dl_ref[rows, h:h + 1]
                        if masked:
                            rid = lax.broadcasted_iota(I32, (R, 128), 0) + r0
                            cid = lax.broadcasted_iota(I32, (R, 128), 1)
                        dss = []
                        for g in range(ng):
                            gs = slice(128 * g, 128 * g + 128)
                            sc = s_scr[rows, gs] - ck_ref[h:h + 1, gs]
                            if masked:
                                sc = jnp.where(cid + 128 * g <= rid, sc, NEG)
                            ds = jnp.exp(sc - lse_c) * (dp_scr[rows, gs] - dl_c)
                            ds_scr[rows, gs] = ds.astype(BF16)
                            dss.append(ds)
                        e_s[h, rows, :] += _tree(jnp.add, dss)

                    _chunk_loop(bq // R, chunk)
                    dqs.append(_dot_nn(ds_scr[...], kp))
                acc[:, sl] += jnp.where(qlo, dqs[0], dqs[1])

        @pl.when(ki < qi)
        def _():
            block(False)

        @pl.when(ki == qi)
        def _():
            block(True)

        @pl.when(ki == nq - 1)
        def _():
            dq_ref[...] = (acc[...] * ATT_SCALE).astype(BF16)
            out = dl_ref[...]
            for h in range(8):
                out = out + _put_col((bq, 128), h, jnp.sum(e_s[h], axis=-1, keepdims=True))
            dl2_ref[...] = out

    qb = pl.BlockSpec((bq, 128), lambda i, k: (i, 0))
    return pl.pallas_call(
        body, name=name, grid=(nq, nq),
        in_specs=[pl.BlockSpec((bq, BRANCH), lambda i, k: (i, CB_FQ)),
                  pl.BlockSpec((bk, BRANCH), lambda i, k: (jnp.minimum(k, i), CB_FK)),
                  pl.BlockSpec((bk, BRANCH), lambda i, k: (jnp.minimum(k, i), CB_FV)),
                  pl.BlockSpec((bq, BRANCH), lambda i, k: (i, 0)),
                  pl.BlockSpec((8, bk), lambda i, k: (0, jnp.minimum(k, i))), qb, qb,
                  pl.BlockSpec(memory_space=pl.ANY)],
        out_specs=[pl.BlockSpec((bq, BRANCH), lambda i, k: (i, CB_FQ)), qb],
        out_shape=[jax.ShapeDtypeStruct(dproj.shape, dproj.dtype), jax.ShapeDtypeStruct((T, 128), F32)],
        input_output_aliases={7: 0},
        scratch_shapes=[pltpu.VMEM((bq, BRANCH), F32), pltpu.VMEM((8, bq, 128), F32),
                        pltpu.VMEM((bq, bk), F32), pltpu.VMEM((bq, bk), F32), pltpu.VMEM((bq, bk), BF16)],
        compiler_params=_cp(("parallel", "arbitrary")),
    )(pm, pm, pm, do, c_row, lse, delta, dproj)


def _fox2_bwd_dkv(pm, do, c_col, lse_row, delta_row, dproj, name):
    T = pm.shape[0]
    bk = _pick(T, (512, 256))
    bq = bk
    nk = T // bk
    R = FOX_ROWS
    ng = bq // 128

    def body(q_ref, k_ref, v_ref, do_ref, ck_ref, lse_ref, dl_ref, buf_ref, dkv_ref, dc_ref,
             dk_acc, dv_acc, dc_s, st_scr, dpt_scr, pt_scr, dst_scr):
        del buf_ref
        dk_ref = dkv_ref.at[:, 0:BRANCH]
        dv_ref = dkv_ref.at[:, BRANCH:2 * BRANCH]
        ki = pl.program_id(0)
        qi = pl.program_id(1)

        @pl.when(qi == 0)
        def _():
            dk_acc[...] = jnp.zeros_like(dk_acc)
            dv_acc[...] = jnp.zeros_like(dv_acc)
            dc_s[...] = jnp.zeros_like(dc_s)

        def block(masked):
            klo = _lane_lo((bk, 128))
            for p in range(4):
                sl = slice(128 * p, 128 * p + 128)
                qp = q_ref[:, sl]
                kp = k_ref[:, sl] * ATT_SCALE
                vp = v_ref[:, sl]
                dop = do_ref[:, sl]
                qs = _masked_halves(qp)
                dos = _masked_halves(dop)
                dks, dvs = [], []
                for j in range(2):
                    h = 2 * p + j
                    st_scr[...] = _dot_nt(kp, qs[j])
                    dpt_scr[...] = _dot_nt(vp, dos[j])

                    def chunk(r, h=h):
                        r0 = r * R
                        rows = pl.ds(r0, R)
                        ck_c = ck_ref[rows, h:h + 1]
                        if masked:
                            kid = lax.broadcasted_iota(I32, (R, 128), 0) + r0
                            qid = lax.broadcasted_iota(I32, (R, 128), 1)
                        dss = []
                        for g in range(ng):
                            gs = slice(128 * g, 128 * g + 128)
                            st = st_scr[rows, gs] - (ck_c + lse_ref[h:h + 1, gs])
                            if masked:
                                st = jnp.where(kid <= qid + 128 * g, st, NEG)
                            pt = jnp.exp(st)
                            dst = pt * (dpt_scr[rows, gs] - dl_ref[h:h + 1, gs])
                            pt_scr[rows, gs] = pt.astype(BF16)
                            dst_scr[rows, gs] = dst.astype(BF16)
                            dss.append(dst)
                        dc_s[h, rows, :] -= _tree(jnp.add, dss)

                    _chunk_loop(bk // R, chunk)
                    dvs.append(_dot_nn(pt_scr[...], dop))
                    dks.append(_dot_nn(dst_scr[...], qp))
                dk_acc[:, sl] += jnp.where(klo, dks[0], dks[1])
                dv_acc[:, sl] += jnp.where(klo, dvs[0], dvs[1])

        @pl.when(qi > ki)
        def _():
            block(False)

        @pl.when(qi == ki)
        def _():
            block(True)

        @pl.when(qi == nk - 1)
        def _():
            dk_ref[...] = (dk_acc[...] * ATT_SCALE).astype(BF16)
            dv_ref[...] = dv_acc[...].astype(BF16)
            out = jnp.zeros((bk, 128), F32)
            for h in range(8):
                out = out + _put_col((bk, 128), h, jnp.sum(dc_s[h], axis=-1, keepdims=True))
            dc_ref[...] = out

    qrow = pl.BlockSpec((8, bq), lambda k, i: (0, jnp.maximum(i, k)))
    return pl.pallas_call(
        body, name=name, grid=(nk, nk),
        in_specs=[pl.BlockSpec((bq, BRANCH), lambda k, i: (jnp.maximum(i, k), CB_FQ)),
                  pl.BlockSpec((bk, BRANCH), lambda k, i: (k, CB_FK)),
                  pl.BlockSpec((bk, BRANCH), lambda k, i: (k, CB_FV)),
                  pl.BlockSpec((bq, BRANCH), lambda k, i: (jnp.maximum(i, k), 0)),
                  pl.BlockSpec((bk, 128), lambda k, i: (k, 0)), qrow, qrow, pl.BlockSpec(memory_space=pl.ANY)],
        out_specs=[pl.BlockSpec((bk, 2 * BRANCH), lambda k, i: (k, 5)), pl.BlockSpec((bk, 128), lambda k, i: (k, 0))],
        out_shape=[jax.ShapeDtypeStruct(dproj.shape, dproj.dtype), jax.ShapeDtypeStruct((T, 128), F32)],
        input_output_aliases={7: 0},
        scratch_shapes=[pltpu.VMEM((bk, BRANCH), F32), pltpu.VMEM((bk, BRANCH), F32),
                        pltpu.VMEM((8, bk, 128), F32), pltpu.VMEM((bk, bq), F32), pltpu.VMEM((bk, bq), F32),
                        pltpu.VMEM((bk, bq), BF16), pltpu.VMEM((bk, bq), BF16)],
        compiler_params=_cp(("parallel", "arbitrary")),
    )(pm, pm, pm, do, c_col, lse_row, delta_row, dproj)


def _bucket_table():
    tq = np.arange(WINDOW, dtype=np.int32)[:, None]
    sk = np.arange(2 * WINDOW, dtype=np.int32)[None, :]
    n = np.maximum(WINDOW + tq - sk, 0)
    max_exact = N_BUCKETS // 2
    ratio = np.maximum(n, 1).astype(np.float32) / np.float32(max_exact)
    large = max_exact + (np.log(ratio) / np.float32(math.log(WINDOW / max_exact))
                         * np.float32(N_BUCKETS - max_exact)).astype(np.int32)
    large = np.minimum(large, N_BUCKETS - 1)
    return np.where(n < max_exact, n, large).astype(np.int32)


def _swap_halves(x):
    return pltpu.roll(x.astype(F32), HEAD_DIM, 1).astype(x.dtype)


def _kv_variants(t):
    lo = _lane_lo(t.shape)
    z = jnp.zeros_like(t)
    a0 = jnp.where(lo, t, z)
    b1 = jnp.where(lo, z, t)
    b0 = _swap_halves(a0)
    a1 = _swap_halves(b1)
    return (a0, a1), (b0, b1), (a0 + b0, a1 + b1)


def _stacked_head(s, r):
    return 4 * (s // 2) + 2 * r + (s % 2)


def _swa_bias(rel_bias, bucket, name):
    def body(rb_ref, bk_ref, o_ref):
        bkt = bk_ref[...]
        tq = lax.broadcasted_iota(I32, bkt.shape, 0)
        jj = lax.broadcasted_iota(I32, bkt.shape, 1)
        window = ((jj < WINDOW) & (jj > tq)) | ((jj >= WINDOW) & (jj - WINDOW <= tq))
        for s in range(4):
            for r in range(2):
                h = _stacked_head(s, r)

                def step(b, a, h=h):
                    return a + jnp.where(bkt == b, rb_ref[b, h], 0.0)
                val = lax.fori_loop(0, N_BUCKETS, step, jnp.zeros(bkt.shape, F32))
                o_ref[s, WINDOW * r:WINDOW * (r + 1), :] = jnp.where(window, val, NEG)

    return pl.pallas_call(
        body, name=name,
        in_specs=[pl.BlockSpec(memory_space=pltpu.SMEM), pl.BlockSpec(memory_space=pltpu.VMEM)],
        out_specs=pl.BlockSpec(memory_space=pltpu.VMEM),
        out_shape=jax.ShapeDtypeStruct((4, 2 * WINDOW, 2 * WINDOW), F32),
    )(rel_bias, bucket)


def _swa_dbias_reduce(dbias, bucket, name):
    def body(d_ref, bk_ref, o_ref):
        bkt = bk_ref[...]
        rowi = lax.broadcasted_iota(I32, (N_BUCKETS, 128), 0)
        lane = lax.broadcasted_iota(I32, (N_BUCKETS, 128), 1)
        out = jnp.zeros((N_BUCKETS, 128), F32)
        for s in range(4):
            for r in range(2):
                h = _stacked_head(s, r)
                dv = d_ref[s, WINDOW * r:WINDOW * (r + 1), :]

                def step(b, a, dv=dv, h=h):
                    tot = jnp.sum(jnp.where(bkt == b, dv, 0.0), keepdims=True)
                    return a + jnp.where((rowi == b) & (lane == h), tot, 0.0)
                out = lax.fori_loop(0, N_BUCKETS, step, out)
        o_ref[...] = out

    return pl.pallas_call(
        body, name=name,
        in_specs=[pl.BlockSpec(memory_space=pltpu.VMEM), pl.BlockSpec(memory_space=pltpu.VMEM)],
        out_specs=pl.BlockSpec(memory_space=pltpu.VMEM),
        out_shape=jax.ShapeDtypeStruct((N_BUCKETS, 128), F32),
    )(dbias, bucket)


def _swa_cols(vec, s):
    rows = lax.broadcasted_iota(I32, (2 * WINDOW, 1), 0)
    return jnp.where(rows < WINDOW, vec[:, _stacked_head(s, 0):_stacked_head(s, 0) + 1],
                     vec[:, _stacked_head(s, 1):_stacked_head(s, 1) + 1])


def _swa_mask(no_prev):
    tq = jnp.bitwise_and(lax.broadcasted_iota(I32, (2 * WINDOW, 2 * WINDOW), 0), WINDOW - 1)
    jj = lax.broadcasted_iota(I32, (2 * WINDOW, 2 * WINDOW), 1)
    prev = (jj < WINDOW) & (jj > tq)
    if no_prev is not False:
        prev = prev & jnp.logical_not(no_prev)
    return prev | ((jj >= WINDOW) & (jj - WINDOW <= tq))


def _swa_stack(ref, rows, g):
    return jnp.concatenate([ref[rows, 256 * g:256 * g + 128], ref[rows, 256 * g + 128:256 * g + 256]], axis=0)


def _swa_specs():
    W2 = 2 * WINDOW
    q = pl.BlockSpec((W2, BRANCH), lambda i: (i, CB_SQ))
    kc = pl.BlockSpec((W2, 128), lambda i: (i, CB_SK))
    kp = pl.BlockSpec((WINDOW, 128), lambda i: (jnp.maximum(2 * i - 1, 0), CB_SK))
    vc = pl.BlockSpec((W2, 128), lambda i: (i, CB_SV))
    vp = pl.BlockSpec((WINDOW, 128), lambda i: (jnp.maximum(2 * i - 1, 0), CB_SV))
    bias = pl.BlockSpec((4, W2, W2), lambda i: (0, 0, 0))
    vec = pl.BlockSpec((1, 128), lambda i: (0, 0))
    return q, kc, kp, vc, vp, bias, vec


def _swa_fwd(pm, bias, sink, name):
    T = pm.shape[0]
    nb = T // (2 * WINDOW)

    def body(q_ref, kc_ref, kp_ref, vc_ref, vp_ref, b_ref, s_ref, o_ref, m_ref):
        i = pl.program_id(0)
        lo = _lane_lo((WINDOW, 128))
        sink_v = s_ref[...]
        for u in range(2):
            rows = slice(WINDOW * u, WINDOW * (u + 1))
            mask = _swa_mask(i == 0 if u == 0 else False)
            kcur, vcur = kc_ref[rows, :], vc_ref[rows, :]
            kprev = kp_ref[...] if u == 0 else kc_ref[0:WINDOW, :]
            vprev = vp_ref[...] if u == 0 else vc_ref[0:WINDOW, :]
            kcA, kcB, _ = _kv_variants(kcur)
            kpA, kpB, _ = _kv_variants(kprev)
            _, _, vcD = _kv_variants(vcur)
            _, _, vpD = _kv_variants(vprev)
            mout = jnp.zeros((WINDOW, 128), F32)
            for g in range(2):
                qg = _swa_stack(q_ref, rows, g) * ATT_SCALE
                vband = jnp.concatenate([vpD[g], vcD[g]], axis=0)
                outs = []
                for par in range(2):
                    s = 2 * g + par
                    kband = jnp.concatenate([(kpA, kpB)[par][g], (kcA, kcB)[par][g]], axis=0)
                    sc = jnp.where(mask, _dot_nt(qg, kband) + b_ref[s], NEG)
                    sk = _swa_cols(sink_v, s)
                    m = jnp.maximum(jnp.max(sc, axis=-1, keepdims=True), sk)
                    e = jnp.exp(sc - m)
                    den = jnp.sum(e, axis=-1, keepdims=True) + jnp.exp(sk - m)
                    outs.append(_dot_nn((e * (1.0 / den)).astype(BF16), vband))
                    lse = m + jnp.log(den)
                    mout = mout + _put_col((WINDOW, 128), _stacked_head(s, 0), lse[:WINDOW])
                    mout = mout + _put_col((WINDOW, 128), _stacked_head(s, 1), lse[WINDOW:])
                for r in range(2):
                    sl = slice(256 * g + 128 * r, 256 * g + 128 * r + 128)
                    o_ref[rows, sl] = jnp.where(lo, outs[0][WINDOW * r:WINDOW * (r + 1)],
                                                outs[1][WINDOW * r:WINDOW * (r + 1)]).astype(BF16)
            m_ref[rows, :] = mout

    q, kc, kp, vc, vp, bs, vec = _swa_specs()
    return pl.pallas_call(
        body, name=name, grid=(nb,),
        in_specs=[q, kc, kp, vc, vp, bs, vec],
        out_specs=[pl.BlockSpec((2 * WINDOW, BRANCH), lambda i: (i, 0)),
                   pl.BlockSpec((2 * WINDOW, 128), lambda i: (i, 0))],
        out_shape=[jax.ShapeDtypeStruct((T, BRANCH), BF16), jax.ShapeDtypeStruct((T, 128), F32)],
        compiler_params=_cp(("parallel",)),
    )(pm, pm, pm, pm, pm, bias, sink)


def _swa_bwd(pm, bias, sink, do, mlse, dproj, name):
    T = pm.shape[0]
    nb = T // (2 * WINDOW)

    def fold(zz):
        return zz + pltpu.roll(zz, HEAD_DIM, 1)

    def body(q_ref, kc_ref, kp_ref, vc_ref, vp_ref, b_ref, s_ref, do_ref, m_ref, buf_ref,
             dq_ref, dkc_ref, dkp_ref, dvc_ref, dvp_ref, db_ref, ds_ref):
        del buf_ref
        i = pl.program_id(0)

        @pl.when(i == 0)
        def _():
            db_ref[...] = jnp.zeros_like(db_ref)
            ds_ref[...] = jnp.zeros_like(ds_ref)

        lo = _lane_lo((WINDOW, 128))
        lo2 = _lane_lo((2 * WINDOW, 128))
        sink_v = s_ref[...]
        dsink = jnp.zeros((1, 128), F32)
        for u in range(2):
            rows = slice(WINDOW * u, WINDOW * (u + 1))
            mask = _swa_mask(i == 0 if u == 0 else False)
            kcur, vcur = kc_ref[rows, :], vc_ref[rows, :]
            kprev = kp_ref[...] if u == 0 else kc_ref[0:WINDOW, :]
            vprev = vp_ref[...] if u == 0 else vc_ref[0:WINDOW, :]
            kcA, kcB, kcD = _kv_variants(kcur)
            kpA, kpB, kpD = _kv_variants(kprev)
            vcA, vcB, _ = _kv_variants(vcur)
            vpA, vpB, _ = _kv_variants(vprev)
            mv = m_ref[rows, :]
            zks, zvs = [], []
            for g in range(2):
                qraw = _swa_stack(q_ref, rows, g)
                qg = qraw * ATT_SCALE
                dog = _swa_stack(do_ref, rows, g)
                kband_d = jnp.concatenate([kpD[g], kcD[g]], axis=0)
                dqs, mks, mvs = [], [], []
                for par in range(2):
                    s = 2 * g + par
                    kband = jnp.concatenate([(kpA, kpB)[par][g], (kcA, kcB)[par][g]], axis=0)
                    vband = jnp.concatenate([(vpA, vpB)[par][g], (vcA, vcB)[par][g]], axis=0)
                    sc = jnp.where(mask, _dot_nt(qg, kband) + b_ref[s], NEG)
                    h0, h1 = _stacked_head(s, 0), _stacked_head(s, 1)
                    m_c = jnp.concatenate([mv[:, h0:h0 + 1], mv[:, h1:h1 + 1]], axis=0)
                    pr = jnp.exp(sc - m_c)
                    psink = jnp.exp(_swa_cols(sink_v, s) - m_c)
                    dp = _dot_nt(dog, vband)
                    delta = jnp.sum(pr * dp, axis=-1, keepdims=True)
                    dsc = pr * (dp - delta)
                    db_ref[s] += dsc
                    sd = psink * delta
                    dsink = dsink - _put_col((1, 128), h0, jnp.sum(sd[:WINDOW], keepdims=True))
                    dsink = dsink - _put_col((1, 128), h1, jnp.sum(sd[WINDOW:], keepdims=True))
                    dsb = dsc.astype(BF16)
                    dqs.append(_dot_nn(dsb, kband_d))
                    mks.append(_dot_tn(dsb, qraw))
                    mvs.append(_dot_tn(pr.astype(BF16), dog))
                for r in range(2):
                    sl = slice(256 * g + 128 * r, 256 * g + 128 * r + 128)
                    dq_ref[rows, sl] = (jnp.where(lo, dqs[0][WINDOW * r:WINDOW * (r + 1)],
                                                  dqs[1][WINDOW * r:WINDOW * (r + 1)]) * ATT_SCALE).astype(BF16)
                zks.append(fold(jnp.where(lo2, mks[0], mks[1])))
                zvs.append(fold(jnp.where(lo2, mvs[0], mvs[1])))
            dk = jnp.where(lo2, zks[0], zks[1]) * ATT_SCALE
            dv = jnp.where(lo2, zvs[0], zvs[1])
            dkp_ref[rows, :] = dk[:WINDOW]
            dkc_ref[rows, :] = dk[WINDOW:]
            dvp_ref[rows, :] = dv[:WINDOW]
            dvc_ref[rows, :] = dv[WINDOW:]
        ds_ref[...] += dsink

    q, kc, kp, vc, vp, bs, vec = _swa_specs()
    own = pl.BlockSpec((2 * WINDOW, BRANCH), lambda i: (i, 0))
    sm = pl.BlockSpec((2 * WINDOW, 128), lambda i: (i, 0))
    f128 = jax.ShapeDtypeStruct((T, 128), F32)
    return pl.pallas_call(
        body, name=name, grid=(nb,),
        in_specs=[q, kc, kp, vc, vp, bs, vec, own, sm, pl.BlockSpec(memory_space=pl.ANY)],
        out_specs=[pl.BlockSpec((2 * WINDOW, BRANCH), lambda i: (i, CB_SQ)), sm, sm, sm, sm, bs, vec],
        out_shape=[jax.ShapeDtypeStruct(dproj.shape, dproj.dtype), f128, f128, f128, f128,
                   jax.ShapeDtypeStruct((4, 2 * WINDOW, 2 * WINDOW), F32), jax.ShapeDtypeStruct((1, 128), F32)],
        input_output_aliases={9: 0},
        compiler_params=_cp(("arbitrary",)),
    )(pm, pm, pm, pm, pm, bias, sink, do, mlse, dproj)


def _merge_fwd(pm, us, name):
    T = pm.shape[0]
    bt = _pick(T, (512, 256))

    def body(g0, g1, g2, u0, u1, u2, o_ref):
        acc = jax.nn.sigmoid(g0[...].astype(F32)) * u0[...].astype(F32)
        acc = acc + jax.nn.sigmoid(g1[...].astype(F32)) * u1[...].astype(F32)
        acc = acc + jax.nn.sigmoid(g2[...].astype(F32)) * u2[...].astype(F32)
        o_ref[...] = acc.astype(BF16)

    own = pl.BlockSpec((bt, D_MODEL), lambda i: (i, 0))
    gs = [pl.BlockSpec((bt, D_MODEL), lambda i, cb=cb: (i, cb)) for cb in CB_GATE]
    return pl.pallas_call(
        body, name=name, grid=(T // bt,), in_specs=gs + [own, own, own], out_specs=own,
        out_shape=jax.ShapeDtypeStruct((T, D_MODEL), BF16),
        compiler_params=_cp(("parallel",)),
    )(pm, pm, pm, *us)


def _merge_bwd(pm, us, dm, name):
    T = pm.shape[0]
    bt = _pick(T, (512, 256))

    def body(g0, g1, g2, u0, u1, u2, dm_ref, du0, du1, du2, dg_ref):
        dmv = dm_ref[...].astype(F32)
        for b, (g, u, du) in enumerate(((g0, u0, du0), (g1, u1, du1), (g2, u2, du2))):
            s = jax.nn.sigmoid(g[...].astype(F32))
            du[...] = (dmv * s).astype(BF16)
            dg_ref[:, D_MODEL * b:D_MODEL * (b + 1)] = (dmv * u[...].astype(F32) * s * (1.0 - s)).astype(BF16)

    own = pl.BlockSpec((bt, D_MODEL), lambda i: (i, 0))
    gs = [pl.BlockSpec((bt, D_MODEL), lambda i, cb=cb: (i, cb)) for cb in CB_GATE]
    act = jax.ShapeDtypeStruct((T, D_MODEL), BF16)
    return pl.pallas_call(
        body, name=name, grid=(T // bt,), in_specs=gs + [own, own, own, own],
        out_specs=[own, own, own, pl.BlockSpec((bt, 3 * D_MODEL), lambda i: (i, 0))],
        out_shape=[act, act, act, jax.ShapeDtypeStruct((T, PROJ_PAD), BF16)],
        compiler_params=_cp(("parallel",)),
    )(pm, pm, pm, *us, dm)


def _swiglu_fwd(ab, name):
    T = ab.shape[0]
    bt = _pick(T, (512, 256))

    def body(a_ref, b_ref, o_ref):
        a = a_ref[...].astype(F32)
        o_ref[...] = (a * jax.nn.sigmoid(a) * b_ref[...].astype(F32)).astype(BF16)

    return pl.pallas_call(
        body, name=name, grid=(T // bt,),
        in_specs=[pl.BlockSpec((bt, D_FF), lambda i: (i, 0)), pl.BlockSpec((bt, D_FF), lambda i: (i, 1))],
        out_specs=pl.BlockSpec((bt, D_FF), lambda i: (i, 0)),
        out_shape=jax.ShapeDtypeStruct((T, D_FF), BF16),
        compiler_params=_cp(("parallel",)),
    )(ab, ab)


def _swiglu_bwd(ab, dh, name):
    T = ab.shape[0]
    bt = _pick(T, (512, 256))

    def body(a_ref, b_ref, d_ref, o_ref):
        a = a_ref[...].astype(F32)
        b = b_ref[...].astype(F32)
        d = d_ref[...].astype(F32)
        s = jax.nn.sigmoid(a)
        o_ref[:, 0:D_FF] = (d * b * (s + a * s * (1.0 - s))).astype(BF16)
        o_ref[:, D_FF:2 * D_FF] = (d * a * s).astype(BF16)

    return pl.pallas_call(
        body, name=name, grid=(T // bt,),
        in_specs=[pl.BlockSpec((bt, D_FF), lambda i: (i, 0)), pl.BlockSpec((bt, D_FF), lambda i: (i, 1)),
                  pl.BlockSpec((bt, D_FF), lambda i: (i, 0))],
        out_specs=pl.BlockSpec((bt, 2 * D_FF), lambda i: (i, 0)),
        out_shape=jax.ShapeDtypeStruct((T, 2 * D_FF), BF16),
        compiler_params=_cp(("parallel",)),
    )(ab, ab, dh)


def _xattn_probs(q_ref, kv_ref, h):
    sl = slice(X_HEAD_DIM * h, X_HEAD_DIM * (h + 1))
    qh = q_ref[:, sl]
    kh = kv_ref[:, sl]
    vh = kv_ref[:, D_MODEL + X_HEAD_DIM * h:D_MODEL + X_HEAD_DIM * (h + 1)]
    s = _dot_nt(qh, kh) * X_SCALE
    e = jnp.exp(s - jnp.max(s, axis=-1, keepdims=True))
    return qh, kh, vh, e * (1.0 / jnp.sum(e, axis=-1, keepdims=True))


def _xattn_fwd(q, kv, name):
    T = q.shape[0]
    bq = _pick(T, (512, 256))

    def body(q_ref, kv_ref, o_ref):
        for h in range(X_HEADS):
            _, _, vh, p = _xattn_probs(q_ref, kv_ref, h)
            o_ref[:, X_HEAD_DIM * h:X_HEAD_DIM * (h + 1)] = _dot_nn(p.astype(BF16), vh).astype(BF16)

    own = pl.BlockSpec((bq, D_MODEL), lambda i: (i, 0))
    return pl.pallas_call(
        body, name=name, grid=(T // bq,),
        in_specs=[own, pl.BlockSpec((MEM_LEN, 2 * D_MODEL), lambda i: (0, 0))], out_specs=own,
        out_shape=jax.ShapeDtypeStruct((T, D_MODEL), BF16),
        compiler_params=_cp(("parallel",)),
    )(q, kv)


def _xattn_bwd(q, kv, do, name):
    T = q.shape[0]
    bq = _pick(T, (512, 256))

    def body(q_ref, kv_ref, do_ref, dq_ref, dkv_ref):
        @pl.when(pl.program_id(0) == 0)
        def _():
            dkv_ref[...] = jnp.zeros_like(dkv_ref)

        for h in range(X_HEADS):
            sl = slice(X_HEAD_DIM * h, X_HEAD_DIM * (h + 1))
            qh, kh, vh, p = _xattn_probs(q_ref, kv_ref, h)
            doh = do_ref[:, sl]
            dp = _dot_nt(doh, vh)
            ds = (p * (dp - jnp.sum(p * dp, axis=-1, keepdims=True)) * X_SCALE).astype(BF16)
            dq_ref[:, sl] = _dot_nn(ds, kh).astype(BF16)
            dkv_ref[:, sl] += _dot_tn(ds, qh)
            dkv_ref[:, D_MODEL + X_HEAD_DIM * h:D_MODEL + X_HEAD_DIM * (h + 1)] += _dot_tn(p.astype(BF16), doh)

    own = pl.BlockSpec((bq, D_MODEL), lambda i: (i, 0))
    kvs = pl.BlockSpec((MEM_LEN, 2 * D_MODEL), lambda i: (0, 0))
    return pl.pallas_call(
        body, name=name, grid=(T // bq,), in_specs=[own, kvs, own], out_specs=[own, kvs],
        out_shape=[jax.ShapeDtypeStruct((T, D_MODEL), BF16), jax.ShapeDtypeStruct((MEM_LEN, 2 * D_MODEL), F32)],
        compiler_params=_cp(("arbitrary",)),
    )(q, kv, do)


def _adamw(w, g, m, v, name):
    R, C = w.shape[0], w.shape[-1]
    rest = w.shape[1:]
    row_bytes = int(np.prod(rest[:-1], dtype=np.int64)) * (-(-C // 128) * 128) * 4
    cands = (1024, 512, 256, 128, 64, 32, 16, 8) if w.ndim == 2 else range(R, 0, -1)
    bt = R
    for cand in cands:
        if R % cand == 0 and cand * row_bytes <= (3 << 19):
            bt = cand
            break
    zeros = (0,) * len(rest)

    def body(w_ref, g_ref, m_ref, v_ref, d_ref, nm_ref, nv_ref):
        gv = g_ref[...]
        mn = ADAM_B1 * m_ref[...] + (1.0 - ADAM_B1) * gv
        vn = ADAM_B2 * v_ref[...] + (1.0 - ADAM_B2) * (gv * gv)
        m_hat = mn / (1.0 - ADAM_B1 ** ADAM_STEP)
        v_hat = vn / (1.0 - ADAM_B2 ** ADAM_STEP)
        d_ref[...] = -ADAM_LR * (m_hat / (jnp.sqrt(v_hat) + ADAM_EPS) + ADAM_WD * w_ref[...])
        nm_ref[...] = mn
        nv_ref[...] = vn

    blk = pl.BlockSpec((bt,) + tuple(rest), lambda i: (i,) + zeros)
    out = jax.ShapeDtypeStruct(w.shape, F32)
    return pl.pallas_call(
        body, name=name, grid=(R // bt,), in_specs=[blk] * 4, out_specs=[blk] * 3,
        out_shape=[out, out, out], compiler_params=_cp(("parallel",)),
    )(w, g, m, v)


ANY = pl.BlockSpec(memory_space=pl.ANY)

BIG = (
    ("w_in", (2048, 1730)), ("w_branch", (3072, 256)), ("w_mix_out", (512, 1024)), ("w_xq", (512, 1024)),
    ("w_xkv", (2048, 512)), ("w_xo", (512, 1024)), ("w_ffn_gate", (1408, 1024)), ("w_ffn_up", (1408, 1024)),
    ("w_ffn_down", (1408, 1024)),
)
TRANSPOSED = ("w_ffn_gate", "w_ffn_up")
ROW_BLOCKS = (512, 256, 352, 128, 16)


def _neighbours():
    x, y, c = lax.axis_index("x"), lax.axis_index("y"), lax.axis_index("c")
    idx = (2 * x + y, 2 * (1 - x) + y, 2 * x + (1 - y), 2 * (1 - x) + (1 - y))
    return idx, (x, y, c), (1 - x, y, c), (x, 1 - y, c), (x, y, 1 - c)


def _remote(src, dst, sems, k, to):
    send_sems, recv_sems = sems
    return pltpu.make_async_remote_copy(src_ref=src, dst_ref=dst, send_sem=send_sems.at[k], recv_sem=recv_sems.at[k],
                                        device_id=to, device_id_type=MESH)


def _cast_place(w, me_idx, name):
    R, Wd = w.shape
    bt = _pick(R, ROW_BLOCKS)

    def body(i_ref, w_ref, o_ref):
        o_ref[0] = w_ref[...].astype(BF16)

    grid_spec = pltpu.PrefetchScalarGridSpec(
        num_scalar_prefetch=1, grid=(R // bt,),
        in_specs=[pl.BlockSpec((bt, Wd), lambda i, idx: (i, 0))],
        out_specs=pl.BlockSpec((1, bt, Wd), lambda i, idx: (idx[0], i, 0)))
    return pl.pallas_call(
        body, name=name, grid_spec=grid_spec, out_shape=jax.ShapeDtypeStruct((4, R, Wd), BF16),
        compiler_params=_cp(("parallel",)),
    )(me_idx, w)


def _ag_ring_multi(bufs):
    n = len(bufs)

    def body(*refs):
        o = refs[n:2 * n]
        sems = refs[2 * n:]
        (me, ix, iy, idg), here, xn, yn, sib = _neighbours()
        c = here[2]

        def piece(t, k, other):
            h = bufs[t].shape[1] // 2
            q = h // 2
            base = ((1 - c) if other else c) * h
            return [(ix, pl.ds(base, h)), (iy, pl.ds(base, h)), (idg, pl.ds(base, q)), (idg, pl.ds(base + q, q))][k]

        def copy(t, k, slab, rows, to):
            ref = o[t].at[slab, rows]
            return _remote(ref, ref, sems, 8 * t + k, to)

        sends = []

        def go(cp):
            cp.start()
            sends.append(cp)

        for t in range(n):
            h = bufs[t].shape[1] // 2
            go(copy(t, 0, me, pl.ds(c * h, h), xn))
            go(copy(t, 1, me, pl.ds(c * h, h), yn))
        for k in range(4):
            for t in range(n):
                slab, rows = piece(t, k, False)
                copy(t, k, slab, rows, here).wait_recv()
                if k == 0:
                    go(copy(t, 2, ix, piece(t, 2, False)[1], yn))
                if k == 1:
                    go(copy(t, 3, iy, piece(t, 3, False)[1], xn))
                go(copy(t, 4 + k, slab, rows, sib))
        for k in range(4):
            for t in range(n):
                slab, rows = piece(t, k, True)
                copy(t, 4 + k, slab, rows, here).wait_recv()
        for cp in sends:
            cp.wait_send()

    return pl.pallas_call(
        body, name="ag_weights", in_specs=[ANY] * n, out_specs=[ANY] * n,
        input_output_aliases={t: t for t in range(n)},
        out_shape=[jax.ShapeDtypeStruct(b.shape, b.dtype) for b in bufs],
        scratch_shapes=[pltpu.SemaphoreType.DMA((8 * n,)), pltpu.SemaphoreType.DMA((8 * n,))],
    )(*bufs)


def _exchange_multi(srcs, out_shapes, plan, name, aliased=False):
    n = len(srcs)

    def body(*refs):
        ins, outs, sems = refs[:n], refs[n:2 * n], refs[2 * n:]
        places = _neighbours()
        here = places[1]
        per = [plan(t, ins[t], outs[t], places) for t in range(n)]
        width = max(len(p) for p in per)
        started = []
        for t in range(n):
            for k, (src, dst, to, land) in enumerate(per[t]):
                cp = _remote(src, dst, sems, width * t + k, to)
                cp.start()
                started.append(cp)
        for t in range(n):
            for k, (src, dst, to, land) in enumerate(per[t]):
                _remote(land, land, sems, width * t + k, here).wait_recv()
        for cp in started:
            cp.wait_send()

    nsem = 2 * n
    return pl.pallas_call(
        body, name=name, in_specs=[ANY] * n, out_specs=[ANY] * n,
        input_output_aliases={t: t for t in range(n)} if aliased else {},
        out_shape=[jax.ShapeDtypeStruct(s, d) for s, d in out_shapes],
        scratch_shapes=[pltpu.SemaphoreType.DMA((nsem,)), pltpu.SemaphoreType.DMA((nsem,))],
    )(*srcs)


def _rs_sibling_multi(gs):
    def plan(t, g, o, places):
        (_, here, _, _, sib) = places
        h = gs[t].shape[1] // 2
        return [(g.at[:, pl.ds((1 - here[2]) * h, h)], o, sib, o)]

    return _exchange_multi(gs, [((4, g.shape[1] // 2, g.shape[2]), g.dtype) for g in gs], plan, "rs_sibling")


def _rs_add_pair(g4, sib, cidx, tag=""):
    _, R, Wd = g4.shape
    hrows = R // 2
    bt = _pick(hrows, ROW_BLOCKS)
    nb = hrows // bt

    def body(c_ref, a_ref, b_ref, o_ref):
        o_ref[...] = (a_ref[...].astype(F32) + b_ref[...].astype(F32)).astype(o_ref.dtype)

    grid_spec = pltpu.PrefetchScalarGridSpec(
        num_scalar_prefetch=1, grid=(4, nb),
        in_specs=[pl.BlockSpec((1, bt, Wd), lambda j, i, c: (j, c[0] * nb + i, 0)),
                  pl.BlockSpec((1, bt, Wd), lambda j, i, c: (j, i, 0))],
        out_specs=pl.BlockSpec((1, bt, Wd), lambda j, i, c: (j, i, 0)))
    return pl.pallas_call(
        body, name="rs_add_pair" + tag, grid_spec=grid_spec,
        out_shape=jax.ShapeDtypeStruct((4, hrows, Wd), g4.dtype),
        compiler_params=_cp(("parallel", "parallel")),
    )(cidx, g4, sib)


def _rs_diag_multi(rs):
    def plan(t, r, o, places):
        ((_, _, _, idg), _, xn, yn, _) = places
        q = rs[t].shape[1] // 2
        return [(r.at[idg, pl.ds(0, q)], o.at[0], xn, o.at[0]), (r.at[idg, pl.ds(q, q)], o.at[1], yn, o.at[1])]

    return _exchange_multi(rs, [((2, r.shape[1] // 2, r.shape[2]), r.dtype) for r in rs], plan, "rs_diag")


def _rs_merge(r4, dg, nbr_idx, tag=""):
    _, hrows, Wd = r4.shape
    bt = _pick(hrows // 2, ROW_BLOCKS)
    nb = hrows // bt
    nq = nb // 2

    def body(i_ref, r_ref, d_ref, o_ref):
        w = pl.program_id(0)
        i = pl.program_id(1)
        merged = jnp.where(w == 0, i >= nq, i < nq)
        add = jnp.where(merged, d_ref[...].astype(F32), 0.0)
        o_ref[...] = (r_ref[...].astype(F32) + add).astype(o_ref.dtype)

    grid_spec = pltpu.PrefetchScalarGridSpec(
        num_scalar_prefetch=1, grid=(2, nb),
        in_specs=[pl.BlockSpec((1, bt, Wd), lambda w, i, idx: (idx[w], i, 0)),
                  pl.BlockSpec((1, bt, Wd), lambda w, i, idx: (1 - w, jnp.clip(i - (1 - w) * nq, 0, nq - 1), 0))],
        out_specs=pl.BlockSpec((1, bt, Wd), lambda w, i, idx: (w, i, 0)))
    return pl.pallas_call(
        body, name="rs_merge" + tag, grid_spec=grid_spec,
        out_shape=jax.ShapeDtypeStruct((2, hrows, Wd), r4.dtype),
        compiler_params=_cp(("parallel", "parallel")),
    )(nbr_idx, r4, dg)


def _rs_direct_multi(ms):
    def plan(t, m, o, places):
        (_, _, xn, yn, _) = places
        return [(m.at[0], o.at[0], xn, o.at[0]), (m.at[1], o.at[1], yn, o.at[1])]

    return _exchange_multi(ms, [(m.shape, m.dtype) for m in ms], plan, "rs_direct")


def _rs_final(r4, got, me_c, tag=""):
    _, hrows, Wd = r4.shape
    bt = _pick(hrows, ROW_BLOCKS)
    nb = hrows // bt

    def body(i_ref, r_ref, g_ref, o_ref):
        o_ref[...] = (r_ref[0].astype(F32) + g_ref[0].astype(F32)) + g_ref[1].astype(F32)

    grid_spec = pltpu.PrefetchScalarGridSpec(
        num_scalar_prefetch=1, grid=(nb,),
        in_specs=[pl.BlockSpec((1, bt, Wd), lambda i, idx: (idx[0], i, 0)),
                  pl.BlockSpec((2, bt, Wd), lambda i, idx: (0, i, 0))],
        out_specs=pl.BlockSpec((bt, Wd), lambda i, idx: (idx[1] * nb + i, 0)))
    return pl.pallas_call(
        body, name="rs_final" + tag, grid_spec=grid_spec,
        out_shape=jax.ShapeDtypeStruct((2 * hrows, Wd), F32),
        compiler_params=_cp(("parallel",)),
    )(me_c, r4, got)


def _rs_share_multi(bufs):
    def plan(t, b, o, places):
        (_, here, _, _, sib) = places
        h = bufs[t].shape[0] // 2
        mine = o.at[pl.ds(here[2] * h, h)]
        return [(mine, mine, sib, o.at[pl.ds((1 - here[2]) * h, h)])]

    return _exchange_multi(bufs, [(b.shape, b.dtype) for b in bufs], plan, "rs_share", aliased=True)


def _allreduce_small(v, name="allreduce_small"):
    R, Wd = v.shape

    def body(v_ref, o_ref, buf, send_sems, recv_sems):
        x, y, c = lax.axis_index("x"), lax.axis_index("y"), lax.axis_index("c")
        me = 4 * x + 2 * y + c
        buf[me] = v_ref[...]
        sends = []
        for k in range(1, 8):
            peer = ((x + (k >> 2)) % 2, (y + ((k >> 1) & 1)) % 2, (c + (k & 1)) % 2)
            sends.append(pltpu.make_async_remote_copy(
                src_ref=v_ref, dst_ref=buf.at[me], send_sem=send_sems.at[k - 1], recv_sem=recv_sems.at[k - 1],
                device_id=peer, device_id_type=MESH))
        for cp in sends:
            cp.start()
        for k in range(1, 8):
            px, py, pc = (x + (k >> 2)) % 2, (y + ((k >> 1) & 1)) % 2, (c + (k & 1)) % 2
            pltpu.make_async_remote_copy(
                src_ref=v_ref, dst_ref=buf.at[4 * px + 2 * py + pc], send_sem=send_sems.at[k - 1],
                recv_sem=recv_sems.at[k - 1], device_id=(x, y, c), device_id_type=MESH).wait_recv()
        acc = buf[0]
        for d in range(1, 8):
            acc = acc + buf[d]
        o_ref[...] = acc
        for cp in sends:
            cp.wait_send()

    vm = pl.BlockSpec(memory_space=pltpu.VMEM)
    return pl.pallas_call(
        body, name=name, in_specs=[vm], out_specs=vm,
        out_shape=jax.ShapeDtypeStruct((R, Wd), F32),
        scratch_shapes=[pltpu.VMEM((8, R, Wd), F32), pltpu.SemaphoreType.DMA((7,)), pltpu.SemaphoreType.DMA((7,))],
    )(v)


SMALL = (
    ("mix_norm_g", (2, 1024)), ("xattn_norm_g", (2, 1024)), ("mem_norm_g", (2, 1024)),
    ("ffn_norm_g", (2, 1024)), ("final_norm_g", (1024,)),
    ("forget_bias", (2, 8)), ("sink", (2, 8)), ("rel_bias", (32, 8)),
)
SMALL_AND_CONV = SMALL + (("conv_w", (2, 3, 512)),)


def _small_rows(spec):
    rows = sum(int(np.prod(s)) // 128 if s[-1] % 128 == 0 else s[0] for _, s in spec)
    return -(-rows // 8) * 8


def _pack_small(vals, spec=SMALL):
    rows = []
    for name, shape in spec:
        v = vals[name].astype(F32)
        if shape[-1] % 128 == 0:
            rows.append(v.reshape(-1, 128))
        else:
            rows.append(jnp.pad(v, ((0, 0), (0, 120))))
    rows = jnp.concatenate(rows, axis=0)
    return jnp.pad(rows, ((0, _small_rows(spec) - rows.shape[0]), (0, 0)))


def _unpack_small(pack, spec=SMALL):
    out, off = {}, 0
    for name, shape in spec:
        if shape[-1] % 128 == 0:
            n = int(np.prod(shape)) // 128
            out[name] = pack[off:off + n].reshape(shape)
        else:
            n = shape[0]
            out[name] = pack[off:off + n, 0:8]
        off += n
    return out


W_IN_PERM = ((3848, 6920), (0, 3072), (3080, 3848), (3072, 3080))


def _perm_w_in(w):
    parts = [w[:, a:b] for a, b in W_IN_PERM]
    return jnp.concatenate(parts + [jnp.zeros((w.shape[0], PROJ_PAD - IN_COLS), w.dtype)], axis=1)


def _unperm_w_in(p):
    return jnp.concatenate([p[:, 3072:6144], p[:, 6912:6920], p[:, 6144:6912], p[:, 0:3072]], axis=1)


def _pad_row8(v):
    return jnp.pad(v.astype(F32).reshape(1, 8), ((0, 0), (0, 120)))


def _local_step(x, mem, tgt, W, rel_bias):
    bucket = jnp.asarray(_bucket_table())
    bias = _swa_bias(rel_bias, bucket, "swa_bias")
    saved = []
    for l in range(DEPTH):
        n = "l%d_" % l
        s = {"x0": x}
        wcat = W["w_in_p"][l]
        h = _rms_fwd(x, W["mix_norm_g"][l:l + 1], n + "mix_norm")
        pm = _mm(h, wcat[:, :PROJ_MAIN], "nn", BF16, n + "proj", bn=768)
        fg = _mm(h, wcat[:, PROJ_MAIN:], "nn", F32, n + "proj_fg")
        fb = _pad_row8(W["forget_bias"][l])
        c_col = _fox_gate_fwd(fg, fb, n + "fox_gate")
        c_row = c_col[:, 0:8].T
        cw = jnp.pad(W["conv_w"][l], ((0, 5), (0, 0)))
        y_conv = _conv_fwd(pm, cw, n + "conv")
        y_fox, lse = _fox2_fwd(pm, c_row, n + "fox")
        sink = _pad_row8(W["sink"][l])
        y_swa, mlse = _swa_fwd(pm, bias, sink, n + "swa")
        ys = (y_conv, y_fox, y_swa)
        us = tuple(_mm(ys[b], W["w_branch"][l][b], "nn", BF16, n + "branch%d" % b) for b in range(3))
        merged = _merge_fwd(pm, us, n + "merge")
        x1 = _mm(merged, W["w_mix_out"][l], "nn", F32, n + "mix_out", res=x)
        xn1 = _rms_fwd(x1, W["xattn_norm_g"][l:l + 1], n + "xattn_norm")
        memn = _rms_fwd(mem, W["mem_norm_g"][l:l + 1], n + "mem_norm")
        qx = _mm(xn1, W["w_xq"][l], "nn", BF16, n + "xq")
        kv = _mm(memn, W["w_xkv"][l], "nn", BF16, n + "xkv")
        ox = _xattn_fwd(qx, kv, n + "xattn")
        x2 = _mm(ox, W["w_xo"][l], "nn", F32, n + "xo", res=x1)
        xn2 = _rms_fwd(x2, W["ffn_norm_g"][l:l + 1], n + "ffn_norm")
        ab = _mm(xn2, W["w_gu"][l], "nt", BF16, n + "ffn_in", bn=512)
        hm = _swiglu_fwd(ab, n + "swiglu")
        x3 = _mm(hm, W["w_ffn_down"][l], "nn", F32, n + "ffn_out", res=x2, bk=1408)
        s.update(h=h, pm=pm, fg=fg, fb=fb, c_col=c_col, c_row=c_row, cw=cw, ys=ys, lse=lse, sink=sink,
                 mlse=mlse, us=us, merged=merged, x1=x1, xn1=xn1, memn=memn, qx=qx, kv=kv, ox=ox,
                 x2=x2, xn2=xn2, ab=ab, hm=hm)
        saved.append(s)
        x = x3

    loss_row, dx, dg_final = _final_loss(x, W["final_norm_g"].reshape(1, D_MODEL), tgt, "final_loss")
    G = {name: [None] * DEPTH for name in
         ("mix_norm_g", "w_in_p", "forget_bias", "conv_w", "sink", "w_branch", "w_mix_out", "xattn_norm_g",
          "mem_norm_g", "w_xq", "w_xkv", "w_xo", "ffn_norm_g", "w_gu", "w_ffn_down")}
    dbias_tot = None
    for l in reversed(range(DEPTH)):
        n = "l%d_" % l
        s = saved[l]
        dhm = _mm(dx, W["w_ffn_down"][l], "nt", BF16, n + "d_hm", bn=1408)
        G["w_ffn_down"][l] = _mm(s["hm"], dx, "tn", BF16, n + "dw_down", bm=1408, bk=1024)
        dab = _swiglu_bwd(s["ab"], dhm, n + "d_swiglu")
        dxn2 = _mm(dab, W["w_gu"][l], "nn", BF16, n + "d_xn2", bk=1408)
        G["w_gu"][l] = _mm(dab, s["xn2"], "tn", BF16, n + "dw_gu", bm=512, bk=2048)
        dx, G["ffn_norm_g"][l] = _rms_bwd(s["x2"], W["ffn_norm_g"][l:l + 1], dxn2, dx, n + "d_ffn_norm")
        dox = _mm(dx, W["w_xo"][l], "nt", BF16, n + "d_ox")
        G["w_xo"][l] = _mm(s["ox"], dx, "tn", BF16, n + "dw_xo", bk=1024)
        dqx, dkv = _xattn_bwd(s["qx"], s["kv"], dox, n + "d_xattn")
        dxn1 = _mm(dqx, W["w_xq"][l], "nt", BF16, n + "d_xn1")
        G["w_xq"][l] = _mm(s["xn1"], dqx, "tn", BF16, n + "dw_xq", bk=2048)
        dmemn = _mm(dkv, W["w_xkv"][l], "nt", BF16, n + "d_memn")
        G["w_xkv"][l] = _mm(s["memn"], dkv, "tn", BF16, n + "dw_xkv")
        _, G["mem_norm_g"][l] = _rms_bwd(mem, W["mem_norm_g"][l:l + 1], dmemn, None, n + "d_mem_norm")
        dx, G["xattn_norm_g"][l] = _rms_bwd(s["x1"], W["xattn_norm_g"][l:l + 1], dxn1, dx, n + "d_xattn_norm")
        dmerged = _mm(dx, W["w_mix_out"][l], "nt", BF16, n + "d_merged")
        G["w_mix_out"][l] = _mm(s["merged"], dx, "tn", BF16, n + "dw_mix_out", bk=1024)
        du0, du1, du2, dproj = _merge_bwd(s["pm"], s["us"], dmerged, n + "d_merge")
        dus = (du0, du1, du2)
        dys = [_mm(dus[b], W["w_branch"][l][b], "nt", BF16, n + "d_y%d" % b) for b in range(3)]
        G["w_branch"][l] = jnp.stack(
            [_mm(s["ys"][b], dus[b], "tn", BF16, n + "dw_branch%d" % b, bk=2048) for b in range(3)])
        dproj, dcw = _conv_bwd(s["pm"], s["cw"], dys[0], dproj, n + "d_conv")
        G["conv_w"][l] = dcw[0:3]
        delta = _fox_delta(s["ys"][1], dys[1], n + "fox_delta")
        dproj, delta = _fox2_bwd_dq(s["pm"], dys[1], s["c_row"], s["lse"], delta, dproj, n + "d_fox_q")
        dproj, dc = _fox2_bwd_dkv(s["pm"], dys[1], s["c_col"], s["lse"][:, 0:8].T, delta[:, 0:8].T, dproj,
                                  n + "d_fox_kv")
        dfg, dfb = _fox_gate_bwd(dc, s["fg"], s["fb"], n + "d_fox_gate")
        G["forget_bias"][l] = dfb[0, 0:8]
        dproj, dkc, dkp, dvc, dvp, dbias, dsink = _swa_bwd(s["pm"], bias, s["sink"], dys[2], s["mlse"], dproj,
                                                         n + "d_swa")
        G["sink"][l] = dsink[0, 0:8]
        dbias_tot = dbias if dbias_tot is None else dbias_tot + dbias
        zpad = jnp.zeros((WINDOW, 128), F32)
        dsk = dkc + jnp.concatenate([dkp[WINDOW:], zpad], axis=0)
        dsv = dvc + jnp.concatenate([dvp[WINDOW:], zpad], axis=0)
        tail = jnp.concatenate([dsk.astype(BF16), dsv.astype(BF16), dfg.astype(BF16)], axis=1)
        dproj = lax.dynamic_update_slice(dproj, tail, (0, PROJ_MAIN - 256))
        dh = _mm(dproj, W["w_in_p"][l], "nt", BF16, n + "d_h", bk=1408)
        G["w_in_p"][l] = _mm(s["h"], dproj, "tn", BF16, n + "dw_in", bn=640, bk=2048)
        dx, G["mix_norm_g"][l] = _rms_bwd(s["x0"], W["mix_norm_g"][l:l + 1], dh, dx, n + "d_mix_norm")
    drb = _swa_dbias_reduce(dbias_tot, bucket, "swa_dbias")
    G["rel_bias"] = drb[:, 0:8]
    G["final_norm_g"] = dg_final.reshape(D_MODEL)
    return loss_row, dx, G


def kernel(x, mem, mix_norm_g, w_in, forget_bias, conv_w, sink, w_branch, w_mix_out, rel_bias, xattn_norm_g, mem_norm_g, w_xq, w_xkv, w_xo, ffn_norm_g, w_ffn_gate, w_ffn_up, w_ffn_down, final_norm_g, loss_target, m_mix_norm_g, m_w_in, m_forget_bias, m_conv_w, m_sink, m_w_branch, m_w_mix_out, m_rel_bias, m_xattn_norm_g, m_mem_norm_g, m_w_xq, m_w_xkv, m_w_xo, m_ffn_norm_g, m_w_ffn_gate, m_w_ffn_up, m_w_ffn_down, m_final_norm_g, v_mix_norm_g, v_w_in, v_forget_bias, v_conv_w, v_sink, v_w_branch, v_w_mix_out, v_rel_bias, v_xattn_norm_g, v_mem_norm_g, v_w_xq, v_w_xkv, v_w_xo, v_ffn_norm_g, v_w_ffn_gate, v_w_ffn_up, v_w_ffn_down, v_final_norm_g):
    order = ("mix_norm_g", "w_in", "forget_bias", "conv_w", "sink", "w_branch", "w_mix_out", "rel_bias",
             "xattn_norm_g", "mem_norm_g", "w_xq", "w_xkv", "w_xo", "ffn_norm_g", "w_ffn_gate", "w_ffn_up",
             "w_ffn_down", "final_norm_g")
    w_sh = dict(zip(order, (mix_norm_g, w_in, forget_bias, conv_w, sink, w_branch, w_mix_out, rel_bias,
                            xattn_norm_g, mem_norm_g, w_xq, w_xkv, w_xo, ffn_norm_g, w_ffn_gate, w_ffn_up,
                            w_ffn_down, final_norm_g)))
    m_sh = dict(zip(order, (m_mix_norm_g, m_w_in, m_forget_bias, m_conv_w, m_sink, m_w_branch, m_w_mix_out,
                            m_rel_bias, m_xattn_norm_g, m_mem_norm_g, m_w_xq, m_w_xkv, m_w_xo, m_ffn_norm_g,
                            m_w_ffn_gate, m_w_ffn_up, m_w_ffn_down, m_final_norm_g)))
    v_sh = dict(zip(order, (v_mix_norm_g, v_w_in, v_forget_bias, v_conv_w, v_sink, v_w_branch, v_w_mix_out,
                            v_rel_bias, v_xattn_norm_g, v_mem_norm_g, v_w_xq, v_w_xkv, v_w_xo, v_ffn_norm_g,
                            v_w_ffn_gate, v_w_ffn_up, v_w_ffn_down, v_final_norm_g)))

    xi, yi, ci = lax.axis_index("x"), lax.axis_index("y"), lax.axis_index("c")
    as_idx = lambda *v: jnp.stack([jnp.asarray(t, I32) for t in v])
    me = 2 * xi + yi
    big = [name for name, _ in BIG]
    two_d = dict(BIG)
    two_d["conv_w"] = (6, 128)

    def slab(a, name):
        return (jnp.swapaxes(a, 1, 2) if name in TRANSPOSED else a).reshape(two_d[name])

    def unslab(a, name):
        shape = w_sh[name].shape
        if name in TRANSPOSED:
            return jnp.swapaxes(a.reshape(shape[0], shape[2], shape[1]), 1, 2)
        return a.reshape(shape)

    gathered = dict(zip(big, _ag_ring_multi(
        [_cast_place(slab(w_sh[name], name), as_idx(me), "place_" + name) for name in big])))
    conv_part = lax.dynamic_update_slice_in_dim(jnp.zeros((DEPTH, 3, BRANCH), F32), 0.5 * conv_w, 128 * me, axis=2)
    conv_full = _allreduce_small(conv_part.reshape(-1, 128), "allgather_conv").reshape(DEPTH, 3, BRANCH)

    def lay(name, l):
        g = gathered[name]
        return g.reshape(4, DEPTH, g.shape[1] // DEPTH, g.shape[2])[:, l]

    def by_cols(name, l):
        g = lay(name, l)
        return jnp.moveaxis(g, 0, 1).reshape(g.shape[1], 4 * g.shape[2])

    def by_rows(name, l):
        g = lay(name, l)
        return g.reshape(4 * g.shape[1], g.shape[2])

    W = {k: w_sh[k] for k in ("mix_norm_g", "forget_bias", "sink", "xattn_norm_g", "mem_norm_g",
                              "ffn_norm_g", "final_norm_g")}
    W["conv_w"] = conv_full
    W["w_in_p"] = [_perm_w_in(by_cols("w_in", l)) for l in range(DEPTH)]
    W["w_gu"] = [jnp.concatenate([by_rows("w_ffn_gate", l), by_rows("w_ffn_up", l)], axis=0) for l in range(DEPTH)]
    W["w_xkv"] = [by_cols("w_xkv", l) for l in range(DEPTH)]
    W["w_branch"] = [jnp.transpose(lay("w_branch", l).reshape(4, 3, BRANCH, 256), (1, 2, 0, 3)).reshape(3, BRANCH, D_MODEL)
                     for l in range(DEPTH)]
    for k in ("w_mix_out", "w_xq", "w_xo", "w_ffn_down"):
        W[k] = [by_rows(k, l) for l in range(DEPTH)]
    loss_row, dx, G = _local_step(x[0], mem[0], loss_target[0], W, rel_bias)

    def to_cols(g):
        return jnp.moveaxis(g.reshape(g.shape[0], 4, g.shape[1] // 4), 1, 0)

    def to_rows(g):
        return g.reshape(4, g.shape[0] // 4, g.shape[1])

    per_layer = {
        "w_in": [to_cols(_unperm_w_in(G["w_in_p"][l])) for l in range(DEPTH)],
        "w_branch": [jnp.transpose(G["w_branch"][l].reshape(3, BRANCH, 4, 256), (2, 0, 1, 3)).reshape(4, 3 * BRANCH, 256)
                     for l in range(DEPTH)],
        "w_mix_out": [to_rows(g) for g in G["w_mix_out"]],
        "w_xq": [to_rows(g) for g in G["w_xq"]],
        "w_xkv": [to_cols(g) for g in G["w_xkv"]],
        "w_xo": [to_rows(g) for g in G["w_xo"]],
        "w_ffn_gate": [to_rows(G["w_gu"][l][:D_FF]) for l in range(DEPTH)],
        "w_ffn_up": [to_rows(G["w_gu"][l][D_FF:]) for l in range(DEPTH)],
        "w_ffn_down": [to_rows(g) for g in G["w_ffn_down"]],
    }
    g4 = [jnp.concatenate(per_layer[name], axis=1).astype(BF16) for name in big]
    sib = _rs_sibling_multi(g4)
    pair = [_rs_add_pair(g4[t], sib[t], as_idx(ci), "_" + big[t]) for t in range(len(big))]
    diag = _rs_diag_multi(pair)
    nbrs = as_idx(2 * (1 - xi) + yi, 2 * xi + (1 - yi))
    merged = [_rs_merge(pair[t], diag[t], nbrs, "_" + big[t]) for t in range(len(big))]
    got = _rs_direct_multi(merged)
    reduced = _rs_share_multi([_rs_final(pair[t], got[t], as_idx(me, ci), "_" + big[t]) for t in range(len(big))])

    small = _unpack_small(_allreduce_small(_pack_small({
        "mix_norm_g": jnp.concatenate(G["mix_norm_g"], axis=0),
        "xattn_norm_g": jnp.concatenate(G["xattn_norm_g"], axis=0),
        "mem_norm_g": jnp.concatenate(G["mem_norm_g"], axis=0),
        "ffn_norm_g": jnp.concatenate(G["ffn_norm_g"], axis=0),
        "final_norm_g": G["final_norm_g"],
        "forget_bias": jnp.stack(G["forget_bias"]),
        "sink": jnp.stack(G["sink"]),
        "rel_bias": G["rel_bias"],
        "conv_w": jnp.stack(G["conv_w"]),
    }, SMALL_AND_CONV)), SMALL_AND_CONV)
    grads = {name: unslab(reduced[t], name) for t, name in enumerate(big)}
    grads.update(small)
    grads["conv_w"] = lax.dynamic_slice_in_dim(small["conv_w"], 128 * me, 128, axis=2)

    sm_names = [name for name, _ in SMALL]
    sd, sm_, sv_ = _adamw(_pack_small({k: w_sh[k] for k in sm_names}), _pack_small({k: grads[k] for k in sm_names}),
                          _pack_small({k: m_sh[k] for k in sm_names}), _pack_small({k: v_sh[k] for k in sm_names}),
                          "adamw_small")
    delta, new_m, new_v = _unpack_small(sd), _unpack_small(sm_), _unpack_small(sv_)
    for t, name in enumerate(big + ["conv_w"]):
        if name == "w_in":
            to3 = lambda a: jnp.transpose(a, (2, 0, 1))
            d, nm, nv = _adamw(to3(w_sh[name]), to3(grads[name]), to3(m_sh[name]), to3(v_sh[name]), "adamw_w_in")
            delta[name], new_m[name], new_v[name] = (jnp.transpose(a, (1, 2, 0)) for a in (d, nm, nv))
            continue
        g2 = reduced[t] if t < len(big) else slab(grads[name], name)
        d, nm, nv = _adamw(slab(w_sh[name], name), g2, slab(m_sh[name], name), slab(v_sh[name], name), "adamw_" + name)
        delta[name], new_m[name], new_v[name] = unslab(d, name), unslab(nm, name), unslab(nv, name)

    loss = lax.psum(loss_row[0, 0], ("x", "y", "c"))
    return (loss, dx[None], *[grads[k] for k in order], *[delta[k] for k in order],
            *[new_m[k] for k in order], *[new_v[k] for k in order])
```

```python
import math

import numpy as np
import jax
import jax.numpy as jnp
from jax import lax
from jax.experimental import pallas as pl
from jax.experimental.pallas import tpu as pltpu

F32 = jnp.float32
BF16 = jnp.bfloat16
I32 = jnp.int32

D_MODEL = 1024
DEPTH = 2
HEAD_DIM = 64
BRANCH = 512
N_BUCKETS = 32
WINDOW = 128
MEM_LEN = 256
X_HEADS = 4
X_HEAD_DIM = 256
D_FF = 2816
IN_COLS = 6920
PROJ_MAIN = 6912
PROJ_PAD = 7040
RMS_EPS = 1e-6
NEG = -1e30
ATT_SCALE = 0.125
X_SCALE = 0.0625

ADAM_LR = 0.001
ADAM_B1 = 0.9
ADAM_B2 = 0.999
ADAM_EPS = 1e-08
ADAM_WD = 0.01
ADAM_STEP = 10

VMEM_LIMIT = 48 * 1024 * 1024
MESH = pl.DeviceIdType.MESH

CB_GATE = (0, 1, 2)
CB_B, CB_C, CB_U, CB_FQ, CB_FK, CB_FV, CB_SQ = 6, 7, 8, 9, 10, 11, 12
CB_SK, CB_SV = 52, 53


def _cp(sem):
    return pltpu.CompilerParams(dimension_semantics=sem, vmem_limit_bytes=VMEM_LIMIT)


def _pick(n, prefs):
    for p in prefs:
        if p <= n and n % p == 0:
            return p
    return n


def _dot(a, b, dims):
    return lax.dot_general(a, b, (dims, ((), ())), preferred_element_type=F32)


def _dot_nn(a, b):
    return _dot(a, b, ((1,), (0,)))


def _dot_nt(a, b):
    return _dot(a, b, ((1,), (1,)))


def _dot_tn(a, b):
    return _dot(a, b, ((0,), (0,)))


def _mm(a, b, mode, out_dtype, name, res=None, bm=1024, bn=1024, bk=1024):
    if mode == "nn":
        (M, K), (K2, N) = a.shape, b.shape
    elif mode == "nt":
        (M, K), (N, K2) = a.shape, b.shape
    else:
        (K, M), (K2, N) = a.shape, b.shape
    assert K == K2, (name, a.shape, b.shape)
    bm = _pick(M, (bm, 1024, 512, 256, 128))
    bn = _pick(N, (bn, 1024, 768, 640, 512, 384, 256, 128))
    bk = _pick(K, (bk, 1024, 768, 640, 512, 384, 256, 128))
    nk = K // bk
    if mode == "tn":
        a_spec = pl.BlockSpec((bk, bm), lambda i, j, k: (k, i))
    else:
        a_spec = pl.BlockSpec((bm, bk), lambda i, j, k: (i, k))
    if mode == "nt":
        b_spec = pl.BlockSpec((bn, bk), lambda i, j, k: (j, k))
    else:
        b_spec = pl.BlockSpec((bk, bn), lambda i, j, k: (k, j))
    dims = {"nn": ((1,), (0,)), "nt": ((1,), (1,)), "tn": ((0,), (0,))}[mode]
    o_spec = pl.BlockSpec((bm, bn), lambda i, j, k: (i, j))
    has_res = res is not None

    def body(*refs):
        if has_res:
            a_ref, b_ref, r_ref, o_ref = refs[:4]
            scr = refs[4:]
        else:
            a_ref, b_ref, o_ref = refs[:3]
            r_ref = None
            scr = refs[3:]
        p = _dot(a_ref[...].astype(BF16), b_ref[...].astype(BF16), dims)
        if nk == 1:
            if has_res:
                p = p + r_ref[...]
            o_ref[...] = p.astype(out_dtype)
        else:
            acc = scr[0]
            k = pl.program_id(2)

            @pl.when(k == 0)
            def _():
                acc[...] = p

            @pl.when(k > 0)
            def _():
                acc[...] += p

            @pl.when(k == nk - 1)
            def _():
                r = acc[...]
                if has_res:
                    r = r + r_ref[...]
                o_ref[...] = r.astype(out_dtype)

    ins = [a, b] + ([res] if has_res else [])
    in_specs = [a_spec, b_spec] + ([o_spec] if has_res else [])
    return pl.pallas_call(
        body, name=name, grid=(M // bm, N // bn, nk),
        in_specs=in_specs, out_specs=o_spec,
        out_shape=jax.ShapeDtypeStruct((M, N), out_dtype),
        scratch_shapes=[pltpu.VMEM((bm, bn), F32)] if nk > 1 else [],
        compiler_params=_cp(("parallel", "parallel", "arbitrary")),
    )(*ins)


def _rms_fwd(x, g, name):
    T, Dm = x.shape
    bt = _pick(T, (512, 256))

    def body(x_ref, g_ref, o_ref):
        xv = x_ref[...]
        r = lax.rsqrt(jnp.mean(xv * xv, axis=-1, keepdims=True) + RMS_EPS)
        o_ref[...] = ((xv * r) * g_ref[...]).astype(BF16)

    return pl.pallas_call(
        body, name=name, grid=(T // bt,),
        in_specs=[pl.BlockSpec((bt, Dm), lambda i: (i, 0)), pl.BlockSpec((1, Dm), lambda i: (0, 0))],
        out_specs=pl.BlockSpec((bt, Dm), lambda i: (i, 0)),
        out_shape=jax.ShapeDtypeStruct((T, Dm), BF16),
        compiler_params=_cp(("parallel",)),
    )(x, g)


def _rms_bwd(x, g, dh, dres, name):
    T, Dm = x.shape
    bt = _pick(T, (512, 256))
    want_dx = dres is not None

    def body(*refs):
        if want_dx:
            x_ref, g_ref, dh_ref, dr_ref, dx_ref, dg_ref = refs
        else:
            x_ref, g_ref, dh_ref, dg_ref = refs
        xv = x_ref[...]
        r = lax.rsqrt(jnp.mean(xv * xv, axis=-1, keepdims=True) + RMS_EPS)
        xh = xv * r
        dhv = dh_ref[...].astype(F32)

        @pl.when(pl.program_id(0) == 0)
        def _():
            dg_ref[...] = jnp.zeros_like(dg_ref)

        dg_ref[...] += jnp.sum(dhv * xh, axis=0, keepdims=True)
        if want_dx:
            dyg = dhv * g_ref[...]
            dx_ref[...] = dr_ref[...] + r * (dyg - xh * jnp.mean(dyg * xh, axis=-1, keepdims=True))

    row = pl.BlockSpec((bt, Dm), lambda i: (i, 0))
    vec = pl.BlockSpec((1, Dm), lambda i: (0, 0))
    if want_dx:
        return pl.pallas_call(
            body, name=name, grid=(T // bt,),
            in_specs=[row, vec, row, row], out_specs=[row, vec],
            out_shape=[jax.ShapeDtypeStruct((T, Dm), F32), jax.ShapeDtypeStruct((1, Dm), F32)],
            compiler_params=_cp(("arbitrary",)),
        )(x, g, dh, dres)
    return None, pl.pallas_call(
        body, name=name, grid=(T // bt,),
        in_specs=[row, vec, row], out_specs=vec,
        out_shape=jax.ShapeDtypeStruct((1, Dm), F32),
        compiler_params=_cp(("arbitrary",)),
    )(x, g, dh)


def _final_loss(x, g, tgt, name):
    T, Dm = x.shape
    bt = _pick(T, (512, 256))

    def body(x_ref, g_ref, t_ref, loss_ref, dx_ref, dg_ref):
        xv = x_ref[...]
        r = lax.rsqrt(jnp.mean(xv * xv, axis=-1, keepdims=True) + RMS_EPS)
        xh = xv * r
        gv = g_ref[...]
        err = xh * gv - t_ref[...]

        @pl.when(pl.program_id(0) == 0)
        def _():
            dg_ref[...] = jnp.zeros_like(dg_ref)
            loss_ref[...] = jnp.zeros_like(loss_ref)

        loss_ref[...] += jnp.sum(err * err) * (0.5 / Dm)
        dy = err * (1.0 / Dm)
        dg_ref[...] += jnp.sum(dy * xh, axis=0, keepdims=True)
        dyg = dy * gv
        dx_ref[...] = r * (dyg - xh * jnp.mean(dyg * xh, axis=-1, keepdims=True))

    row = pl.BlockSpec((bt, Dm), lambda i: (i, 0))
    vec = pl.BlockSpec((1, Dm), lambda i: (0, 0))
    return pl.pallas_call(
        body, name=name, grid=(T // bt,),
        in_specs=[row, vec, row],
        out_specs=[pl.BlockSpec((1, 128), lambda i: (0, 0)), row, vec],
        out_shape=[jax.ShapeDtypeStruct((1, 128), F32), jax.ShapeDtypeStruct((T, Dm), F32),
                   jax.ShapeDtypeStruct((1, Dm), F32)],
        compiler_params=_cp(("arbitrary",)),
    )(x, g, tgt)


HALO = 16


def _shift_down(z, zprev, s):
    rolled = pltpu.roll(z, s, 0)
    hp = pltpu.roll(zprev, s, 0)
    row = lax.broadcasted_iota(I32, hp.shape, 0)
    top = jnp.where(row < s, hp, rolled[:HALO])
    return jnp.concatenate([top, rolled[HALO:]], axis=0)


def _shift_up(z, znext, s):
    n = z.shape[0]
    rolled = pltpu.roll(z, n - s, 0)
    hn = pltpu.roll(znext, HALO - s, 0)
    row = lax.broadcasted_iota(I32, hn.shape, 0)
    bot = jnp.where(row >= HALO - s, hn, rolled[n - HALO:])
    return jnp.concatenate([rolled[:n - HALO], bot], axis=0)


def _conv_fwd(pm, cw, name):
    T = pm.shape[0]
    bt = _pick(T, (512, 256))
    hb = bt // HALO

    def body(b_ref, c_ref, u_ref, cp_ref, up_ref, w_ref, o_ref):
        i = pl.program_id(0)
        z = c_ref[...].astype(F32) * u_ref[...].astype(F32)
        zp = cp_ref[...].astype(F32) * up_ref[...].astype(F32)
        zp = jnp.where(i > 0, zp, 0.0)
        w = w_ref[...]
        y = w[2:3] * z + w[1:2] * _shift_down(z, zp, 1) + w[0:1] * _shift_down(z, zp, 2)
        o_ref[...] = (b_ref[...].astype(F32) * y).astype(BF16)

    def col(cb):
        return pl.BlockSpec((bt, BRANCH), lambda i: (i, cb))

    def prev(cb):
        return pl.BlockSpec((HALO, BRANCH), lambda i: (jnp.maximum(i * hb - 1, 0), cb))

    return pl.pallas_call(
        body, name=name, grid=(T // bt,),
        in_specs=[col(CB_B), col(CB_C), col(CB_U), prev(CB_C), prev(CB_U),
                  pl.BlockSpec((8, BRANCH), lambda i: (0, 0))],
        out_specs=pl.BlockSpec((bt, BRANCH), lambda i: (i, 0)),
        out_shape=jax.ShapeDtypeStruct((T, BRANCH), BF16),
        compiler_params=_cp(("parallel",)),
    )(pm, pm, pm, pm, pm, cw)


def _conv_bwd(pm, cw, dy, dproj, name):
    T = pm.shape[0]
    bt = _pick(T, (512, 256))
    hb = bt // HALO
    nb = T // bt
    last_h = T // HALO - 1

    def body(b_ref, c_ref, u_ref, cp_ref, up_ref, bn_ref, dy_ref, dyn_ref, w_ref, buf_ref,
             dp_ref, dw_ref):
        del buf_ref
        db_ref = dp_ref.at[:, 0:BRANCH]
        dc_ref = dp_ref.at[:, BRANCH:2 * BRANCH]
        du_ref = dp_ref.at[:, 2 * BRANCH:3 * BRANCH]
        i = pl.program_id(0)
        cv = c_ref[...].astype(F32)
        uv = u_ref[...].astype(F32)
        bv = b_ref[...].astype(F32)
        z = cv * uv
        zp = jnp.where(i > 0, cp_ref[...].astype(F32) * up_ref[...].astype(F32), 0.0)
        w = w_ref[...]
        z1 = _shift_down(z, zp, 1)
        z2 = _shift_down(z, zp, 2)
        yc = w[2:3] * z + w[1:2] * z1 + w[0:1] * z2
        dyv = dy_ref[...].astype(F32)
        db_ref[...] = (dyv * yc).astype(BF16)
        g = dyv * bv
        gn = jnp.where(i < nb - 1, dyn_ref[...].astype(F32) * bn_ref[...].astype(F32), 0.0)
        dz = w[2:3] * g + w[1:2] * _shift_up(g, gn, 1) + w[0:1] * _shift_up(g, gn, 2)
        dc_ref[...] = (dz * uv).astype(BF16)
        du_ref[...] = (dz * cv).astype(BF16)

        @pl.when(i == 0)
        def _():
            dw_ref[...] = jnp.zeros_like(dw_ref)

        dw_ref[0:1, :] += jnp.sum(g * z2, axis=0, keepdims=True)
        dw_ref[1:2, :] += jnp.sum(g * z1, axis=0, keepdims=True)
        dw_ref[2:3, :] += jnp.sum(g * z, axis=0, keepdims=True)

    def col(cb):
        return pl.BlockSpec((bt, BRANCH), lambda i: (i, cb))

    def prev(cb):
        return pl.BlockSpec((HALO, BRANCH), lambda i: (jnp.maximum(i * hb - 1, 0), cb))

    def nxt(cb):
        return pl.BlockSpec((HALO, BRANCH), lambda i: (jnp.minimum((i + 1) * hb, last_h), cb))

    own = pl.BlockSpec((bt, BRANCH), lambda i: (i, 0))
    w_spec = pl.BlockSpec((8, BRANCH), lambda i: (0, 0))
    return pl.pallas_call(
        body, name=name, grid=(nb,),
        in_specs=[col(CB_B), col(CB_C), col(CB_U), prev(CB_C), prev(CB_U), nxt(CB_B), own,
                  pl.BlockSpec((HALO, BRANCH), lambda i: (jnp.minimum((i + 1) * hb, last_h), 0)), w_spec,
                  pl.BlockSpec(memory_space=pl.ANY)],
        out_specs=[pl.BlockSpec((bt, 3 * BRANCH), lambda i: (i, 2)), w_spec],
        out_shape=[jax.ShapeDtypeStruct(dproj.shape, dproj.dtype), jax.ShapeDtypeStruct((8, BRANCH), F32)],
        input_output_aliases={9: 0},
        compiler_params=_cp(("arbitrary",)),
    )(pm, pm, pm, pm, pm, pm, dy, dy, cw, dproj)


def _log_sigmoid(z):
    return jnp.minimum(z, 0.0) - jnp.log(1.0 + jnp.exp(-jnp.abs(z)))


def _fox_gate_fwd(fg, fb, name):
    T = fg.shape[0]
    bt = _pick(T, (256,))

    def body(f_ref, b_ref, c_ref, carry):
        @pl.when(pl.program_id(0) == 0)
        def _():
            carry[...] = jnp.zeros_like(carry)

        xv = _log_sigmoid(f_ref[...] + b_ref[...])
        row = lax.broadcasted_iota(I32, xv.shape, 0)
        s = 1
        while s < bt:
            xv = xv + jnp.where(row >= s, pltpu.roll(xv, s, 0), 0.0)
            s *= 2
        xv = xv + carry[...]
        c_ref[...] = xv
        carry[...] = xv[bt - 1:bt, :]

    blk = pl.BlockSpec((bt, 128), lambda i: (i, 0))
    return pl.pallas_call(
        body, name=name, grid=(T // bt,),
        in_specs=[blk, pl.BlockSpec((1, 128), lambda i: (0, 0))],
        out_specs=blk, out_shape=jax.ShapeDtypeStruct((T, 128), F32),
        scratch_shapes=[pltpu.VMEM((1, 128), F32)],
        compiler_params=_cp(("arbitrary",)),
    )(fg, fb)


def _fox_gate_bwd(dc, fg, fb, name):
    T = fg.shape[0]
    bt = _pick(T, (256,))
    nb = T // bt

    def body(d_ref, f_ref, b_ref, o_ref, db_ref, carry):
        @pl.when(pl.program_id(0) == 0)
        def _():
            carry[...] = jnp.zeros_like(carry)
            db_ref[...] = jnp.zeros_like(db_ref)

        xv = d_ref[...]
        row = lax.broadcasted_iota(I32, xv.shape, 0)
        s = 1
        while s < bt:
            xv = xv + jnp.where(row < bt - s, pltpu.roll(xv, bt - s, 0), 0.0)
            s *= 2
        xv = xv + carry[...]
        carry[...] = xv[0:1, :]
        z = f_ref[...] + b_ref[...]
        dz = xv * (1.0 / (1.0 + jnp.exp(z)))
        o_ref[...] = dz
        db_ref[...] += jnp.sum(dz, axis=0, keepdims=True)

    blk = pl.BlockSpec((bt, 128), lambda i: (nb - 1 - i, 0))
    vec = pl.BlockSpec((1, 128), lambda i: (0, 0))
    return pl.pallas_call(
        body, name=name, grid=(nb,),
        in_specs=[blk, blk, vec], out_specs=[blk, vec],
        out_shape=[jax.ShapeDtypeStruct((T, 128), F32), jax.ShapeDtypeStruct((1, 128), F32)],
        scratch_shapes=[pltpu.VMEM((1, 128), F32)],
        compiler_params=_cp(("arbitrary",)),
    )(dc, fg, fb)


def _lane_lo(shape):
    return lax.broadcasted_iota(I32, shape, 1) < HEAD_DIM


def _put_col(shape, h, col):
    lane = lax.broadcasted_iota(I32, shape, 1)
    return jnp.where(lane == h, col, 0.0)


def _fox_delta(o, do, name):
    T = o.shape[0]
    bt = _pick(T, (512, 256))

    def body(o_ref, d_ref, out_ref):
        prod = o_ref[...].astype(F32) * d_ref[...].astype(F32)
        out = jnp.zeros((bt, 128), F32)
        for h in range(8):
            out = out + _put_col((bt, 128), h, jnp.sum(prod[:, 64 * h:64 * h + 64], axis=-1, keepdims=True))
        out_ref[...] = out

    blk = pl.BlockSpec((bt, BRANCH), lambda i: (i, 0))
    return pl.pallas_call(
        body, name=name, grid=(T // bt,), in_specs=[blk, blk],
        out_specs=pl.BlockSpec((bt, 128), lambda i: (i, 0)),
        out_shape=jax.ShapeDtypeStruct((T, 128), F32),
        compiler_params=_cp(("parallel",)),
    )(o, do)


FOX_ROWS = 32


def _chunk_loop(n, chunk):
    for r in range(n):
        chunk(r)


def _tree(op, xs):
    xs = list(xs)
    while len(xs) > 1:
        xs = [op(xs[i], xs[i + 1]) if i + 1 < len(xs) else xs[i] for i in range(0, len(xs), 2)]
    return xs[0]


def _masked_halves(t):
    lo = _lane_lo(t.shape)
    z = jnp.zeros_like(t)
    return jnp.where(lo, t, z), jnp.where(lo, z, t)


def _fox2_fwd(pm, c_row, name):
    T = pm.shape[0]
    bq = _pick(T, (512, 256))
    bk = bq
    nq = T // bq
    R = FOX_ROWS
    ng = bk // 128

    def body(q_ref, k_ref, v_ref, ck_ref, o_ref, lse_ref, acc, m_s, l_s, a_s, s_scr, p_scr):
        qi = pl.program_id(0)
        ki = pl.program_id(1)

        @pl.when(ki == 0)
        def _():
            acc[...] = jnp.zeros_like(acc)
            m_s[...] = jnp.full_like(m_s, NEG)
            l_s[...] = jnp.zeros_like(l_s)

        def block(masked):
            qlo = _lane_lo((bq, 128))
            for p in range(4):
                sl = slice(128 * p, 128 * p + 128)
                qp = q_ref[:, sl] * ATT_SCALE
                vp = v_ref[:, sl]
                ks = _masked_halves(k_ref[:, sl])
                pvs = []
                for j in range(2):
                    h = 2 * p + j
                    s_scr[j] = _dot_nt(qp, ks[j])

                    def chunk(r, h=h, j=j):
                        r0 = r * R
                        rows = pl.ds(r0, R)
                        sc = [s_scr[j, rows, 128 * g:128 * g + 128] - ck_ref[h:h + 1, 128 * g:128 * g + 128]
                              for g in range(ng)]
                        if masked:
                            rid = lax.broadcasted_iota(I32, (R, 128), 0) + r0
                            cid = lax.broadcasted_iota(I32, (R, 128), 1)
                            sc = [jnp.where(cid + 128 * g <= rid, sc[g], NEG) for g in range(ng)]
                        m_old = m_s[h, rows, :]
                        m_new = jnp.maximum(m_old, jnp.max(_tree(jnp.maximum, sc), axis=-1, keepdims=True))
                        alpha = jnp.exp(m_old - m_new)
                        pe = [jnp.exp(sc[g] - m_new) for g in range(ng)]
                        l_s[h, rows, :] = alpha * l_s[h, rows, :] + _tree(jnp.add, pe)
                        m_s[h, rows, :] = m_new
                        a_s[j, rows, :] = alpha
                        for g in range(ng):
                            p_scr[j, rows, 128 * g:128 * g + 128] = pe[g].astype(BF16)

                    _chunk_loop(bq // R, chunk)
                    pvs.append(_dot_nn(p_scr[j], vp))
                acc[:, sl] = jnp.where(qlo, a_s[0], a_s[1]) * acc[:, sl] + jnp.where(qlo, pvs[0], pvs[1])

        @pl.when(ki < qi)
        def _():
            block(False)

        @pl.when(ki == qi)
        def _():
            block(True)

        @pl.when(ki == nq - 1)
        def _():
            qlo = _lane_lo((bq, 128))
            lse = jnp.zeros((bq, 128), F32)
            for p in range(4):
                sl = slice(128 * p, 128 * p + 128)
                l0 = jnp.sum(l_s[2 * p], axis=-1, keepdims=True)
                l1 = jnp.sum(l_s[2 * p + 1], axis=-1, keepdims=True)
                o_ref[:, sl] = (acc[:, sl] / jnp.where(qlo, l0, l1)).astype(BF16)
                lse = lse + _put_col((bq, 128), 2 * p, m_s[2 * p][:, 0:1] + jnp.log(l0))
                lse = lse + _put_col((bq, 128), 2 * p + 1, m_s[2 * p + 1][:, 0:1] + jnp.log(l1))
            lse_ref[...] = lse

    return pl.pallas_call(
        body, name=name, grid=(nq, nq),
        in_specs=[pl.BlockSpec((bq, BRANCH), lambda i, k: (i, CB_FQ)),
                  pl.BlockSpec((bk, BRANCH), lambda i, k: (jnp.minimum(k, i), CB_FK)),
                  pl.BlockSpec((bk, BRANCH), lambda i, k: (jnp.minimum(k, i), CB_FV)),
                  pl.BlockSpec((8, bk), lambda i, k: (0, jnp.minimum(k, i)))],
        out_specs=[pl.BlockSpec((bq, BRANCH), lambda i, k: (i, 0)),
                   pl.BlockSpec((bq, 128), lambda i, k: (i, 0))],
        out_shape=[jax.ShapeDtypeStruct((T, BRANCH), BF16), jax.ShapeDtypeStruct((T, 128), F32)],
        scratch_shapes=[pltpu.VMEM((bq, BRANCH), F32), pltpu.VMEM((8, bq, 128), F32),
                        pltpu.VMEM((8, bq, 128), F32), pltpu.VMEM((2, bq, 128), F32),
                        pltpu.VMEM((2, bq, bk), F32), pltpu.VMEM((2, bq, bk), BF16)],
        compiler_params=_cp(("parallel", "arbitrary")),
    )(pm, pm, pm, c_row)


def _fox2_bwd_dq(pm, do, c_row, lse, delta, dproj, name):
    T = pm.shape[0]
    bq = _pick(T, (512, 256))
    bk = bq
    nq = T // bq
    R = FOX_ROWS
    ng = bk // 128

    def body(q_ref, k_ref, v_ref, do_ref, ck_ref, lse_ref, dl_ref, buf_ref, dq_ref, dl2_ref,
             acc, e_s, s_scr, dp_scr, ds_scr):
        del buf_ref
        qi = pl.program_id(0)
        ki = pl.program_id(1)

        @pl.when(ki == 0)
        def _():
            acc[...] = jnp.zeros_like(acc)
            e_s[...] = jnp.zeros_like(e_s)

        def block(masked):
            qlo = _lane_lo((bq, 128))
            for p in range(4):
                sl = slice(128 * p, 128 * p + 128)
                qp = q_ref[:, sl] * ATT_SCALE
                kp = k_ref[:, sl]
                dop = do_ref[:, sl]
                ks = _masked_halves(kp)
                vs = _masked_halves(v_ref[:, sl])
                dqs = []
                for j in range(2):
                    h = 2 * p + j
                    s_scr[...] = _dot_nt(qp, ks[j])
                    dp_scr[...] = _dot_nt(dop, vs[j])

                    def chunk(r, h=h):
                        r0 = r * R
                        rows = pl.ds(r0, R)
                        lse_c = lse_ref[rows, h:h + 1]
                        dl_c = dl_ref[rows, h:h + 1]
                        if masked:
                            rid = lax.broadcasted_iota(I32, (R, 128), 0) + r0
                            cid = lax.broadcasted_iota(I32, (R, 128), 1)
                        dss = []
                        for g in range(ng):
                            gs = slice(128 * g, 128 * g + 128)
                            sc = s_scr[rows, gs] - ck_ref[h:h + 1, gs]
                            if masked:
                                sc = jnp.where(cid + 128 * g <= rid, sc, NEG)
                            ds = jnp.exp(sc - lse_c) * (dp_scr[rows, gs] - dl_c)
                            ds_scr[rows, gs] = ds.astype(BF16)
                            dss.append(ds)
                        e_s[h, rows, :] += _tree(jnp.add, dss)

                    _chunk_loop(bq // R, chunk)
                    dqs.append(_dot_nn(ds_scr[...], kp))
                acc[:, sl] += jnp.where(qlo, dqs[0], dqs[1])

        @pl.when(ki < qi)
        def _():
            block(False)

        @pl.when(ki == qi)
        def _():
            block(True)

        @pl.when(ki == nq - 1)
        def _():
            dq_ref[...] = (acc[...] * ATT_SCALE).astype(BF16)
            out = dl_ref[...]
            for h in range(8):
                out = out + _put_col((bq, 128), h, jnp.sum(e_s[h], axis=-1, keepdims=True))
            dl2_ref[...] = out

    qb = pl.BlockSpec((bq, 128), lambda i, k: (i, 0))
    return pl.pallas_call(
        body, name=name, grid=(nq, nq),
        in_specs=[pl.BlockSpec((bq, BRANCH), lambda i, k: (i, CB_FQ)),
                  pl.BlockSpec((bk, BRANCH), lambda i, k: (jnp.minimum(k, i), CB_FK)),
                  pl.BlockSpec((bk, BRANCH), lambda i, k: (jnp.minimum(k, i), CB_FV)),
                  pl.BlockSpec((bq, BRANCH), lambda i, k: (i, 0)),
                  pl.BlockSpec((8, bk), lambda i, k: (0, jnp.minimum(k, i))), qb, qb,
                  pl.BlockSpec(memory_space=pl.ANY)],
        out_specs=[pl.BlockSpec((bq, BRANCH), lambda i, k: (i, CB_FQ)), qb],
        out_shape=[jax.ShapeDtypeStruct(dproj.shape, dproj.dtype), jax.ShapeDtypeStruct((T, 128), F32)],
        input_output_aliases={7: 0},
        scratch_shapes=[pltpu.VMEM((bq, BRANCH), F32), pltpu.VMEM((8, bq, 128), F32),
                        pltpu.VMEM((bq, bk), F32), pltpu.VMEM((bq, bk), F32), pltpu.VMEM((bq, bk), BF16)],
        compiler_params=_cp(("parallel", "arbitrary")),
    )(pm, pm, pm, do, c_row, lse, delta, dproj)


def _fox2_bwd_dkv(pm, do, c_col, lse_row, delta_row, dproj, name):
    T = pm.shape[0]
    bk = _pick(T, (512, 256))
    bq = bk
    nk = T // bk
    R = FOX_ROWS
    ng = bq // 128

    def body(q_ref, k_ref, v_ref, do_ref, ck_ref, lse_ref, dl_ref, buf_ref, dkv_ref, dc_ref,
             dk_acc, dv_acc, dc_s, st_scr, dpt_scr, pt_scr, dst_scr):
        del buf_ref
        dk_ref = dkv_ref.at[:, 0:BRANCH]
        dv_ref = dkv_ref.at[:, BRANCH:2 * BRANCH]
        ki = pl.program_id(0)
        qi = pl.program_id(1)

        @pl.when(qi == 0)
        def _():
            dk_acc[...] = jnp.zeros_like(dk_acc)
            dv_acc[...] = jnp.zeros_like(dv_acc)
            dc_s[...] = jnp.zeros_like(dc_s)

        def block(masked):
            klo = _lane_lo((bk, 128))
            for p in range(4):
                sl = slice(128 * p, 128 * p + 128)
                qp = q_ref[:, sl]
                kp = k_ref[:, sl] * ATT_SCALE
                vp = v_ref[:, sl]
                dop = do_ref[:, sl]
                qs = _masked_halves(qp)
                dos = _masked_halves(dop)
                dks, dvs = [], []
                for j in range(2):
                    h = 2 * p + j
                    st_scr[...] = _dot_nt(kp, qs[j])
                    dpt_scr[...] = _dot_nt(vp, dos[j])

                    def chunk(r, h=h):
                        r0 = r * R
                        rows = pl.ds(r0, R)
                        ck_c = ck_ref[rows, h:h + 1]
                        if masked:
                            kid = lax.broadcasted_iota(I32, (R, 128), 0) + r0
                            qid = lax.broadcasted_iota(I32, (R, 128), 1)
                        dss = []
                        for g in range(ng):
                            gs = slice(128 * g, 128 * g + 128)
                            st = st_scr[rows, gs] - (ck_c + lse_ref[h:h + 1, gs])
                            if masked:
                                st = jnp.where(kid <= qid + 128 * g, st, NEG)
                            pt = jnp.exp(st)
                            dst = pt * (dpt_scr[rows, gs] - dl_ref[h:h + 1, gs])
                            pt_scr[rows, gs] = pt.astype(BF16)
                            dst_scr[rows, gs] = dst.astype(BF16)
                            dss.append(dst)
                        dc_s[h, rows, :] -= _tree(jnp.add, dss)

                    _chunk_loop(bk // R, chunk)
                    dvs.append(_dot_nn(pt_scr[...], dop))
                    dks.append(_dot_nn(dst_scr[...], qp))
                dk_acc[:, sl] += jnp.where(klo, dks[0], dks[1])
                dv_acc[:, sl] += jnp.where(klo, dvs[0], dvs[1])

        @pl.when(qi > ki)
        def _():
            block(False)

        @pl.when(qi == ki)
        def _():
            block(True)

        @pl.when(qi == nk - 1)
        def _():
            dk_ref[...] = (dk_acc[...] * ATT_SCALE).astype(BF16)
            dv_ref[...] = dv_acc[...].astype(BF16)
            out = jnp.zeros((bk, 128), F32)
            for h in range(8):
                out = out + _put_col((bk, 128), h, jnp.sum(dc_s[h], axis=-1, keepdims=True))
            dc_ref[...] = out

    qrow = pl.BlockSpec((8, bq), lambda k, i: (0, jnp.maximum(i, k)))
    return pl.pallas_call(
        body, name=name, grid=(nk, nk),
        in_specs=[pl.BlockSpec((bq, BRANCH), lambda k, i: (jnp.maximum(i, k), CB_FQ)),
                  pl.BlockSpec((bk, BRANCH), lambda k, i: (k, CB_FK)),
                  pl.BlockSpec((bk, BRANCH), lambda k, i: (k, CB_FV)),
                  pl.BlockSpec((bq, BRANCH), lambda k, i: (jnp.maximum(i, k), 0)),
                  pl.BlockSpec((bk, 128), lambda k, i: (k, 0)), qrow, qrow, pl.BlockSpec(memory_space=pl.ANY)],
        out_specs=[pl.BlockSpec((bk, 2 * BRANCH), lambda k, i: (k, 5)), pl.BlockSpec((bk, 128), lambda k, i: (k, 0))],
        out_shape=[jax.ShapeDtypeStruct(dproj.shape, dproj.dtype), jax.ShapeDtypeStruct((T, 128), F32)],
        input_output_aliases={7: 0},
        scratch_shapes=[pltpu.VMEM((bk, BRANCH), F32), pltpu.VMEM((bk, BRANCH), F32),
                        pltpu.VMEM((8, bk, 128), F32), pltpu.VMEM((bk, bq), F32), pltpu.VMEM((bk, bq), F32),
                        pltpu.VMEM((bk, bq), BF16), pltpu.VMEM((bk, bq), BF16)],
        compiler_params=_cp(("parallel", "arbitrary")),
    )(pm, pm, pm, do, c_col, lse_row, delta_row, dproj)


def _bucket_table():
    tq = np.arange(WINDOW, dtype=np.int32)[:, None]
    sk = np.arange(2 * WINDOW, dtype=np.int32)[None, :]
    n = np.maximum(WINDOW + tq - sk, 0)
    max_exact = N_BUCKETS // 2
    ratio = np.maximum(n, 1).astype(np.float32) / np.float32(max_exact)
    large = max_exact + (np.log(ratio) / np.float32(math.log(WINDOW / max_exact))
                         * np.float32(N_BUCKETS - max_exact)).astype(np.int32)
    large = np.minimum(large, N_BUCKETS - 1)
    return np.where(n < max_exact, n, large).astype(np.int32)


def _swap_halves(x):
    return pltpu.roll(x.astype(F32), HEAD_DIM, 1).astype(x.dtype)


def _kv_variants(t):
    lo = _lane_lo(t.shape)
    z = jnp.zeros_like(t)
    a0 = jnp.where(lo, t, z)
    b1 = jnp.where(lo, z, t)
    b0 = _swap_halves(a0)
    a1 = _swap_halves(b1)
    return (a0, a1), (b0, b1), (a0 + b0, a1 + b1)


def _stacked_head(s, r):
    return 4 * (s // 2) + 2 * r + (s % 2)


def _swa_bias(rel_bias, bucket, name):
    def body(rb_ref, bk_ref, o_ref):
        bkt = bk_ref[...]
        tq = lax.broadcasted_iota(I32, bkt.shape, 0)
        jj = lax.broadcasted_iota(I32, bkt.shape, 1)
        window = ((jj < WINDOW) & (jj > tq)) | ((jj >= WINDOW) & (jj - WINDOW <= tq))
        for s in range(4):
            for r in range(2):
                h = _stacked_head(s, r)

                def step(b, a, h=h):
                    return a + jnp.where(bkt == b, rb_ref[b, h], 0.0)
                val = lax.fori_loop(0, N_BUCKETS, step, jnp.zeros(bkt.shape, F32))
                o_ref[s, WINDOW * r:WINDOW * (r + 1), :] = jnp.where(window, val, NEG)

    return pl.pallas_call(
        body, name=name,
        in_specs=[pl.BlockSpec(memory_space=pltpu.SMEM), pl.BlockSpec(memory_space=pltpu.VMEM)],
        out_specs=pl.BlockSpec(memory_space=pltpu.VMEM),
        out_shape=jax.ShapeDtypeStruct((4, 2 * WINDOW, 2 * WINDOW), F32),
    )(rel_bias, bucket)


def _swa_dbias_reduce(dbias, bucket, name):
    def body(d_ref, bk_ref, o_ref):
        bkt = bk_ref[...]
        rowi = lax.broadcasted_iota(I32, (N_BUCKETS, 128), 0)
        lane = lax.broadcasted_iota(I32, (N_BUCKETS, 128), 1)
        out = jnp.zeros((N_BUCKETS, 128), F32)
        for s in range(4):
            for r in range(2):
                h = _stacked_head(s, r)
                dv = d_ref[s, WINDOW * r:WINDOW * (r + 1), :]

                def step(b, a, dv=dv, h=h):
                    tot = jnp.sum(jnp.where(bkt == b, dv, 0.0), keepdims=True)
                    return a + jnp.where((rowi == b) & (lane == h), tot, 0.0)
                out = lax.fori_loop(0, N_BUCKETS, step, out)
        o_ref[...] = out

    return pl.pallas_call(
        body, name=name,
        in_specs=[pl.BlockSpec(memory_space=pltpu.VMEM), pl.BlockSpec(memory_space=pltpu.VMEM)],
        out_specs=pl.BlockSpec(memory_space=pltpu.VMEM),
        out_shape=jax.ShapeDtypeStruct((N_BUCKETS, 128), F32),
    )(dbias, bucket)


def _swa_cols(vec, s):
    rows = lax.broadcasted_iota(I32, (2 * WINDOW, 1), 0)
    return jnp.where(rows < WINDOW, vec[:, _stacked_head(s, 0):_stacked_head(s, 0) + 1],
                     vec[:, _stacked_head(s, 1):_stacked_head(s, 1) + 1])


def _swa_mask(no_prev):
    tq = jnp.bitwise_and(lax.broadcasted_iota(I32, (2 * WINDOW, 2 * WINDOW), 0), WINDOW - 1)
    jj = lax.broadcasted_iota(I32, (2 * WINDOW, 2 * WINDOW), 1)
    prev = (jj < WINDOW) & (jj > tq)
    if no_prev is not False:
        prev = prev & jnp.logical_not(no_prev)
    return prev | ((jj >= WINDOW) & (jj - WINDOW <= tq))


def _swa_stack(ref, rows, g):
    return jnp.concatenate([ref[rows, 256 * g:256 * g + 128], ref[rows, 256 * g + 128:256 * g + 256]], axis=0)


def _swa_specs():
    W2 = 2 * WINDOW
    q = pl.BlockSpec((W2, BRANCH), lambda i: (i, CB_SQ))
    kc = pl.BlockSpec((W2, 128), lambda i: (i, CB_SK))
    kp = pl.BlockSpec((WINDOW, 128), lambda i: (jnp.maximum(2 * i - 1, 0), CB_SK))
    vc = pl.BlockSpec((W2, 128), lambda i: (i, CB_SV))
    vp = pl.BlockSpec((WINDOW, 128), lambda i: (jnp.maximum(2 * i - 1, 0), CB_SV))
    bias = pl.BlockSpec((4, W2, W2), lambda i: (0, 0, 0))
    vec = pl.BlockSpec((1, 128), lambda i: (0, 0))
    return q, kc, kp, vc, vp, bias, vec


def _swa_fwd(pm, bias, sink, name):
    T = pm.shape[0]
    nb = T // (2 * WINDOW)

    def body(q_ref, kc_ref, kp_ref, vc_ref, vp_ref, b_ref, s_ref, o_ref, m_ref):
        i = pl.program_id(0)
        lo = _lane_lo((WINDOW, 128))
        sink_v = s_ref[...]
        for u in range(2):
            rows = slice(WINDOW * u, WINDOW * (u + 1))
            mask = _swa_mask(i == 0 if u == 0 else False)
            kcur, vcur = kc_ref[rows, :], vc_ref[rows, :]
            kprev = kp_ref[...] if u == 0 else kc_ref[0:WINDOW, :]
            vprev = vp_ref[...] if u == 0 else vc_ref[0:WINDOW, :]
            kcA, kcB, _ = _kv_variants(kcur)
            kpA, kpB, _ = _kv_variants(kprev)
            _, _, vcD = _kv_variants(vcur)
            _, _, vpD = _kv_variants(vprev)
            mout = jnp.zeros((WINDOW, 128), F32)
            for g in range(2):
                qg = _swa_stack(q_ref, rows, g) * ATT_SCALE
                vband = jnp.concatenate([vpD[g], vcD[g]], axis=0)
                outs = []
                for par in range(2):
                    s = 2 * g + par
                    kband = jnp.concatenate([(kpA, kpB)[par][g], (kcA, kcB)[par][g]], axis=0)
                    sc = jnp.where(mask, _dot_nt(qg, kband) + b_ref[s], NEG)
                    sk = _swa_cols(sink_v, s)
                    m = jnp.maximum(jnp.max(sc, axis=-1, keepdims=True), sk)
                    e = jnp.exp(sc - m)
                    den = jnp.sum(e, axis=-1, keepdims=True) + jnp.exp(sk - m)
                    outs.append(_dot_nn((e * (1.0 / den)).astype(BF16), vband))
                    lse = m + jnp.log(den)
                    mout = mout + _put_col((WINDOW, 128), _stacked_head(s, 0), lse[:WINDOW])
                    mout = mout + _put_col((WINDOW, 128), _stacked_head(s, 1), lse[WINDOW:])
                for r in range(2):
                    sl = slice(256 * g + 128 * r, 256 * g + 128 * r + 128)
                    o_ref[rows, sl] = jnp.where(lo, outs[0][WINDOW * r:WINDOW * (r + 1)],
                                                outs[1][WINDOW * r:WINDOW * (r + 1)]).astype(BF16)
            m_ref[rows, :] = mout

    q, kc, kp, vc, vp, bs, vec = _swa_specs()
    return pl.pallas_call(
        body, name=name, grid=(nb,),
        in_specs=[q, kc, kp, vc, vp, bs, vec],
        out_specs=[pl.BlockSpec((2 * WINDOW, BRANCH), lambda i: (i, 0)),
                   pl.BlockSpec((2 * WINDOW, 128), lambda i: (i, 0))],
        out_shape=[jax.ShapeDtypeStruct((T, BRANCH), BF16), jax.ShapeDtypeStruct((T, 128), F32)],
        compiler_params=_cp(("parallel",)),
    )(pm, pm, pm, pm, pm, bias, sink)


def _swa_bwd(pm, bias, sink, do, mlse, dproj, name):
    T = pm.shape[0]
    nb = T // (2 * WINDOW)

    def fold(zz):
        return zz + pltpu.roll(zz, HEAD_DIM, 1)

    def body(q_ref, kc_ref, kp_ref, vc_ref, vp_ref, b_ref, s_ref, do_ref, m_ref, buf_ref,
             dq_ref, dkc_ref, dkp_ref, dvc_ref, dvp_ref, db_ref, ds_ref):
        del buf_ref
        i = pl.program_id(0)

        @pl.when(i == 0)
        def _():
            db_ref[...] = jnp.zeros_like(db_ref)
            ds_ref[...] = jnp.zeros_like(ds_ref)

        lo = _lane_lo((WINDOW, 128))
        lo2 = _lane_lo((2 * WINDOW, 128))
        sink_v = s_ref[...]
        dsink = jnp.zeros((1, 128), F32)
        for u in range(2):
            rows = slice(WINDOW * u, WINDOW * (u + 1))
            mask = _swa_mask(i == 0 if u == 0 else False)
            kcur, vcur = kc_ref[rows, :], vc_ref[rows, :]
            kprev = kp_ref[...] if u == 0 else kc_ref[0:WINDOW, :]
            vprev = vp_ref[...] if u == 0 else vc_ref[0:WINDOW, :]
            kcA, kcB, kcD = _kv_variants(kcur)
            kpA, kpB, kpD = _kv_variants(kprev)
            vcA, vcB, _ = _kv_variants(vcur)
            vpA, vpB, _ = _kv_variants(vprev)
            mv = m_ref[rows, :]
            zks, zvs = [], []
            for g in range(2):
                qraw = _swa_stack(q_ref, rows, g)
                qg = qraw * ATT_SCALE
                dog = _swa_stack(do_ref, rows, g)
                kband_d = jnp.concatenate([kpD[g], kcD[g]], axis=0)
                dqs, mks, mvs = [], [], []
                for par in range(2):
                    s = 2 * g + par
                    kband = jnp.concatenate([(kpA, kpB)[par][g], (kcA, kcB)[par][g]], axis=0)
                    vband = jnp.concatenate([(vpA, vpB)[par][g], (vcA, vcB)[par][g]], axis=0)
                    sc = jnp.where(mask, _dot_nt(qg, kband) + b_ref[s], NEG)
                    h0, h1 = _stacked_head(s, 0), _stacked_head(s, 1)
                    m_c = jnp.concatenate([mv[:, h0:h0 + 1], mv[:, h1:h1 + 1]], axis=0)
                    pr = jnp.exp(sc - m_c)
                    psink = jnp.exp(_swa_cols(sink_v, s) - m_c)
                    dp = _dot_nt(dog, vband)
                    delta = jnp.sum(pr * dp, axis=-1, keepdims=True)
                    dsc = pr * (dp - delta)
                    db_ref[s] += dsc
                    sd = psink * delta
                    dsink = dsink - _put_col((1, 128), h0, jnp.sum(sd[:WINDOW], keepdims=True))
                    dsink = dsink - _put_col((1, 128), h1, jnp.sum(sd[WINDOW:], keepdims=True))
                    dsb = dsc.astype(BF16)
                    dqs.append(_dot_nn(dsb, kband_d))
                    mks.append(_dot_tn(dsb, qraw))
                    mvs.append(_dot_tn(pr.astype(BF16), dog))
                for r in range(2):
                    sl = slice(256 * g + 128 * r, 256 * g + 128 * r + 128)
                    dq_ref[rows, sl] = (jnp.where(lo, dqs[0][WINDOW * r:WINDOW * (r + 1)],
                                                  dqs[1][WINDOW * r:WINDOW * (r + 1)]) * ATT_SCALE).astype(BF16)
                zks.append(fold(jnp.where(lo2, mks[0], mks[1])))
                zvs.append(fold(jnp.where(lo2, mvs[0], mvs[1])))
            dk = jnp.where(lo2, zks[0], zks[1]) * ATT_SCALE
            dv = jnp.where(lo2, zvs[0], zvs[1])
            dkp_ref[rows, :] = dk[:WINDOW]
            dkc_ref[rows, :] = dk[WINDOW:]
            dvp_ref[rows, :] = dv[:WINDOW]
            dvc_ref[rows, :] = dv[WINDOW:]
        ds_ref[...] += dsink

    q, kc, kp, vc, vp, bs, vec = _swa_specs()
    own = pl.BlockSpec((2 * WINDOW, BRANCH), lambda i: (i, 0))
    sm = pl.BlockSpec((2 * WINDOW, 128), lambda i: (i, 0))
    f128 = jax.ShapeDtypeStruct((T, 128), F32)
    return pl.pallas_call(
        body, name=name, grid=(nb,),
        in_specs=[q, kc, kp, vc, vp, bs, vec, own, sm, pl.BlockSpec(memory_space=pl.ANY)],
        out_specs=[pl.BlockSpec((2 * WINDOW, BRANCH), lambda i: (i, CB_SQ)), sm, sm, sm, sm, bs, vec],
        out_shape=[jax.ShapeDtypeStruct(dproj.shape, dproj.dtype), f128, f128, f128, f128,
                   jax.ShapeDtypeStruct((4, 2 * WINDOW, 2 * WINDOW), F32), jax.ShapeDtypeStruct((1, 128), F32)],
        input_output_aliases={9: 0},
        compiler_params=_cp(("arbitrary",)),
    )(pm, pm, pm, pm, pm, bias, sink, do, mlse, dproj)


def _merge_fwd(pm, us, name):
    T = pm.shape[0]
    bt = _pick(T, (512, 256))

    def body(g0, g1, g2, u0, u1, u2, o_ref):
        acc = jax.nn.sigmoid(g0[...].astype(F32)) * u0[...].astype(F32)
        acc = acc + jax.nn.sigmoid(g1[...].astype(F32)) * u1[...].astype(F32)
        acc = acc + jax.nn.sigmoid(g2[...].astype(F32)) * u2[...].astype(F32)
        o_ref[...] = acc.astype(BF16)

    own = pl.BlockSpec((bt, D_MODEL), lambda i: (i, 0))
    gs = [pl.BlockSpec((bt, D_MODEL), lambda i, cb=cb: (i, cb)) for cb in CB_GATE]
    return pl.pallas_call(
        body, name=name, grid=(T // bt,), in_specs=gs + [own, own, own], out_specs=own,
        out_shape=jax.ShapeDtypeStruct((T, D_MODEL), BF16),
        compiler_params=_cp(("parallel",)),
    )(pm, pm, pm, *us)


def _merge_bwd(pm, us, dm, name):
    T = pm.shape[0]
    bt = _pick(T, (512, 256))

    def body(g0, g1, g2, u0, u1, u2, dm_ref, du0, du1, du2, dg_ref):
        dmv = dm_ref[...].astype(F32)
        for b, (g, u, du) in enumerate(((g0, u0, du0), (g1, u1, du1), (g2, u2, du2))):
            s = jax.nn.sigmoid(g[...].astype(F32))
            du[...] = (dmv * s).astype(BF16)
            dg_ref[:, D_MODEL * b:D_MODEL * (b + 1)] = (dmv * u[...].astype(F32) * s * (1.0 - s)).astype(BF16)

    own = pl.BlockSpec((bt, D_MODEL), lambda i: (i, 0))
    gs = [pl.BlockSpec((bt, D_MODEL), lambda i, cb=cb: (i, cb)) for cb in CB_GATE]
    act = jax.ShapeDtypeStruct((T, D_MODEL), BF16)
    return pl.pallas_call(
        body, name=name, grid=(T // bt,), in_specs=gs + [own, own, own, own],
        out_specs=[own, own, own, pl.BlockSpec((bt, 3 * D_MODEL), lambda i: (i, 0))],
        out_shape=[act, act, act, jax.ShapeDtypeStruct((T, PROJ_PAD), BF16)],
        compiler_params=_cp(("parallel",)),
    )(pm, pm, pm, *us, dm)


def _swiglu_fwd(ab, name):
    T = ab.shape[0]
    bt = _pick(T, (512, 256))

    def body(a_ref, b_ref, o_ref):
        a = a_ref[...].astype(F32)
        o_ref[...] = (a * jax.nn.sigmoid(a) * b_ref[...].astype(F32)).astype(BF16)

    return pl.pallas_call(
        body, name=name, grid=(T // bt,),
        in_specs=[pl.BlockSpec((bt, D_FF), lambda i: (i, 0)), pl.BlockSpec((bt, D_FF), lambda i: (i, 1))],
        out_specs=pl.BlockSpec((bt, D_FF), lambda i: (i, 0)),
        out_shape=jax.ShapeDtypeStruct((T, D_FF), BF16),
        compiler_params=_cp(("parallel",)),
    )(ab, ab)


def _swiglu_bwd(ab, dh, name):
    T = ab.shape[0]
    bt = _pick(T, (512, 256))

    def body(a_ref, b_ref, d_ref, o_ref):
        a = a_ref[...].astype(F32)
        b = b_ref[...].astype(F32)
        d = d_ref[...].astype(F32)
        s = jax.nn.sigmoid(a)
        o_ref[:, 0:D_FF] = (d * b * (s + a * s * (1.0 - s))).astype(BF16)
        o_ref[:, D_FF:2 * D_FF] = (d * a * s).astype(BF16)

    return pl.pallas_call(
        body, name=name, grid=(T // bt,),
        in_specs=[pl.BlockSpec((bt, D_FF), lambda i: (i, 0)), pl.BlockSpec((bt, D_FF), lambda i: (i, 1)),
                  pl.BlockSpec((bt, D_FF), lambda i: (i, 0))],
        out_specs=pl.BlockSpec((bt, 2 * D_FF), lambda i: (i, 0)),
        out_shape=jax.ShapeDtypeStruct((T, 2 * D_FF), BF16),
        compiler_params=_cp(("parallel",)),
    )(ab, ab, dh)


def _xattn_probs(q_ref, kv_ref, h):
    sl = slice(X_HEAD_DIM * h, X_HEAD_DIM * (h + 1))
    qh = q_ref[:, sl]
    kh = kv_ref[:, sl]
    vh = kv_ref[:, D_MODEL + X_HEAD_DIM * h:D_MODEL + X_HEAD_DIM * (h + 1)]
    s = _dot_nt(qh, kh) * X_SCALE
    e = jnp.exp(s - jnp.max(s, axis=-1, keepdims=True))
    return qh, kh, vh, e * (1.0 / jnp.sum(e, axis=-1, keepdims=True))


def _xattn_fwd(q, kv, name):
    T = q.shape[0]
    bq = _pick(T, (512, 256))

    def body(q_ref, kv_ref, o_ref):
        for h in range(X_HEADS):
            _, _, vh, p = _xattn_probs(q_ref, kv_ref, h)
            o_ref[:, X_HEAD_DIM * h:X_HEAD_DIM * (h + 1)] = _dot_nn(p.astype(BF16), vh).astype(BF16)

    own = pl.BlockSpec((bq, D_MODEL), lambda i: (i, 0))
    return pl.pallas_call(
        body, name=name, grid=(T // bq,),
        in_specs=[own, pl.BlockSpec((MEM_LEN, 2 * D_MODEL), lambda i: (0, 0))], out_specs=own,
        out_shape=jax.ShapeDtypeStruct((T, D_MODEL), BF16),
        compiler_params=_cp(("parallel",)),
    )(q, kv)


def _xattn_bwd(q, kv, do, name):
    T = q.shape[0]
    bq = _pick(T, (512, 256))

    def body(q_ref, kv_ref, do_ref, dq_ref, dkv_ref):
        @pl.when(pl.program_id(0) == 0)
        def _():
            dkv_ref[...] = jnp.zeros_like(dkv_ref)

        for h in range(X_HEADS):
            sl = slice(X_HEAD_DIM * h, X_HEAD_DIM * (h + 1))
            qh, kh, vh, p = _xattn_probs(q_ref, kv_ref, h)
            doh = do_ref[:, sl]
            dp = _dot_nt(doh, vh)
            ds = (p * (dp - jnp.sum(p * dp, axis=-1, keepdims=True)) * X_SCALE).astype(BF16)
            dq_ref[:, sl] = _dot_nn(ds, kh).astype(BF16)
            dkv_ref[:, sl] += _dot_tn(ds, qh)
            dkv_ref[:, D_MODEL + X_HEAD_DIM * h:D_MODEL + X_HEAD_DIM * (h + 1)] += _dot_tn(p.astype(BF16), doh)

    own = pl.BlockSpec((bq, D_MODEL), lambda i: (i, 0))
    kvs = pl.BlockSpec((MEM_LEN, 2 * D_MODEL), lambda i: (0, 0))
    return pl.pallas_call(
        body, name=name, grid=(T // bq,), in_specs=[own, kvs, own], out_specs=[own, kvs],
        out_shape=[jax.ShapeDtypeStruct((T, D_MODEL), BF16), jax.ShapeDtypeStruct((MEM_LEN, 2 * D_MODEL), F32)],
        compiler_params=_cp(("arbitrary",)),
    )(q, kv, do)


def _adamw(w, g, m, v, name):
    R, C = w.shape[0], w.shape[-1]
    rest = w.shape[1:]
    row_bytes = int(np.prod(rest[:-1], dtype=np.int64)) * (-(-C // 128) * 128) * 4
    cands = (1024, 512, 256, 128, 64, 32, 16, 8) if w.ndim == 2 else range(R, 0, -1)
    bt = R
    for cand in cands:
        if R % cand == 0 and cand * row_bytes <= (3 << 19):
            bt = cand
            break
    zeros = (0,) * len(rest)

    def body(w_ref, g_ref, m_ref, v_ref, d_ref, nm_ref, nv_ref):
        gv = g_ref[...]
        mn = ADAM_B1 * m_ref[...] + (1.0 - ADAM_B1) * gv
        vn = ADAM_B2 * v_ref[...] + (1.0 - ADAM_B2) * (gv * gv)
        m_hat = mn / (1.0 - ADAM_B1 ** ADAM_STEP)
        v_hat = vn / (1.0 - ADAM_B2 ** ADAM_STEP)
        d_ref[...] = -ADAM_LR * (m_hat / (jnp.sqrt(v_hat) + ADAM_EPS) + ADAM_WD * w_ref[...])
        nm_ref[...] = mn
        nv_ref[...] = vn

    blk = pl.BlockSpec((bt,) + tuple(rest), lambda i: (i,) + zeros)
    out = jax.ShapeDtypeStruct(w.shape, F32)
    return pl.pallas_call(
        body, name=name, grid=(R // bt,), in_specs=[blk] * 4, out_specs=[blk] * 3,
        out_shape=[out, out, out], compiler_params=_cp(("parallel",)),
    )(w, g, m, v)


ANY = pl.BlockSpec(memory_space=pl.ANY)

BIG = (
    ("w_in", (2048, 1730)), ("w_branch", (3072, 256)), ("w_mix_out", (512, 1024)), ("w_xq", (512, 1024)),
    ("w_xkv", (2048, 512)), ("w_xo", (512, 1024)), ("w_ffn_gate", (1408, 1024)), ("w_ffn_up", (1408, 1024)),
    ("w_ffn_down", (1408, 1024)),
)
TRANSPOSED = ("w_ffn_gate", "w_ffn_up")
ROW_BLOCKS = (512, 256, 352, 128, 16)


def _neighbours():
    x, y, c = lax.axis_index("x"), lax.axis_index("y"), lax.axis_index("c")
    idx = (2 * x + y, 2 * (1 - x) + y, 2 * x + (1 - y), 2 * (1 - x) + (1 - y))
    return idx, (x, y, c), (1 - x, y, c), (x, 1 - y, c), (x, y, 1 - c)


def _remote(src, dst, sems, k, to):
    send_sems, recv_sems = sems
    return pltpu.make_async_remote_copy(src_ref=src, dst_ref=dst, send_sem=send_sems.at[k], recv_sem=recv_sems.at[k],
                                        device_id=to, device_id_type=MESH)


def _cast_place(w, me_idx, name):
    R, Wd = w.shape
    bt = _pick(R, ROW_BLOCKS)

    def body(i_ref, w_ref, o_ref):
        o_ref[0] = w_ref[...].astype(BF16)

    grid_spec = pltpu.PrefetchScalarGridSpec(
        num_scalar_prefetch=1, grid=(R // bt,),
        in_specs=[pl.BlockSpec((bt, Wd), lambda i, idx: (i, 0))],
        out_specs=pl.BlockSpec((1, bt, Wd), lambda i, idx: (idx[0], i, 0)))
    return pl.pallas_call(
        body, name=name, grid_spec=grid_spec, out_shape=jax.ShapeDtypeStruct((4, R, Wd), BF16),
        compiler_params=_cp(("parallel",)),
    )(me_idx, w)


def _ag_ring_multi(bufs):
    n = len(bufs)

    def body(*refs):
        o = refs[n:2 * n]
        sems = refs[2 * n:]
        (me, ix, iy, idg), here, xn, yn, sib = _neighbours()
        c = here[2]

        def piece(t, k, other):
            h = bufs[t].shape[1] // 2
            q = h // 2
            base = ((1 - c) if other else c) * h
            return [(ix, pl.ds(base, h)), (iy, pl.ds(base, h)), (idg, pl.ds(base, q)), (idg, pl.ds(base + q, q))][k]

        def copy(t, k, slab, rows, to):
            ref = o[t].at[slab, rows]
            return _remote(ref, ref, sems, 8 * t + k, to)

        sends = []

        def go(cp):
            cp.start()
            sends.append(cp)

        for t in range(n):
            h = bufs[t].shape[1] // 2
            go(copy(t, 0, me, pl.ds(c * h, h), xn))
            go(copy(t, 1, me, pl.ds(c * h, h), yn))
        for k in range(4):
            for t in range(n):
                slab, rows = piece(t, k, False)
                copy(t, k, slab, rows, here).wait_recv()
                if k == 0:
                    go(copy(t, 2, ix, piece(t, 2, False)[1], yn))
                if k == 1:
                    go(copy(t, 3, iy, piece(t, 3, False)[1], xn))
                go(copy(t, 4 + k, slab, rows, sib))
        for k in range(4):
            for t in range(n):
                slab, rows = piece(t, k, True)
                copy(t, 4 + k, slab, rows, here).wait_recv()
        for cp in sends:
            cp.wait_send()

    return pl.pallas_call(
        body, name="ag_weights", in_specs=[ANY] * n, out_specs=[ANY] * n,
        input_output_aliases={t: t for t in range(n)},
        out_shape=[jax.ShapeDtypeStruct(b.shape, b.dtype) for b in bufs],
        scratch_shapes=[pltpu.SemaphoreType.DMA((8 * n,)), pltpu.SemaphoreType.DMA((8 * n,))],
    )(*bufs)


def _exchange_multi(srcs, out_shapes, plan, name, aliased=False):
    n = len(srcs)

    def body(*refs):
        ins, outs, sems = refs[:n], refs[n:2 * n], refs[2 * n:]
        places = _neighbours()
        here = places[1]
        per = [plan(t, ins[t], outs[t], places) for t in range(n)]
        width = max(len(p) for p in per)
        started = []
        for t in range(n):
            for k, (src, dst, to, land) in enumerate(per[t]):
                cp = _remote(src, dst, sems, width * t + k, to)
                cp.start()
                started.append(cp)
        for t in range(n):
            for k, (src, dst, to, land) in enumerate(per[t]):
                _remote(land, land, sems, width * t + k, here).wait_recv()
        for cp in started:
            cp.wait_send()

    nsem = 2 * n
    return pl.pallas_call(
        body, name=name, in_specs=[ANY] * n, out_specs=[ANY] * n,
        input_output_aliases={t: t for t in range(n)} if aliased else {},
        out_shape=[jax.ShapeDtypeStruct(s, d) for s, d in out_shapes],
        scratch_shapes=[pltpu.SemaphoreType.DMA((nsem,)), pltpu.SemaphoreType.DMA((nsem,))],
    )(*srcs)


def _rs_sibling_multi(gs):
    def plan(t, g, o, places):
        (_, here, _, _, sib) = places
        h = gs[t].shape[1] // 2
        return [(g.at[:, pl.ds((1 - here[2]) * h, h)], o, sib, o)]

    return _exchange_multi(gs, [((4, g.shape[1] // 2, g.shape[2]), g.dtype) for g in gs], plan, "rs_sibling")


def _rs_add_pair(g4, sib, cidx, tag=""):
    _, R, Wd = g4.shape
    hrows = R // 2
    bt = _pick(hrows, ROW_BLOCKS)
    nb = hrows // bt

    def body(c_ref, a_ref, b_ref, o_ref):
        o_ref[...] = (a_ref[...].astype(F32) + b_ref[...].astype(F32)).astype(o_ref.dtype)

    grid_spec = pltpu.PrefetchScalarGridSpec(
        num_scalar_prefetch=1, grid=(4, nb),
        in_specs=[pl.BlockSpec((1, bt, Wd), lambda j, i, c: (j, c[0] * nb + i, 0)),
                  pl.BlockSpec((1, bt, Wd), lambda j, i, c: (j, i, 0))],
        out_specs=pl.BlockSpec((1, bt, Wd), lambda j, i, c: (j, i, 0)))
    return pl.pallas_call(
        body, name="rs_add_pair" + tag, grid_spec=grid_spec,
        out_shape=jax.ShapeDtypeStruct((4, hrows, Wd), g4.dtype),
        compiler_params=_cp(("parallel", "parallel")),
    )(cidx, g4, sib)


def _rs_diag_multi(rs):
    def plan(t, r, o, places):
        ((_, _, _, idg), _, xn, yn, _) = places
        q = rs[t].shape[1] // 2
        return [(r.at[idg, pl.ds(0, q)], o.at[0], xn, o.at[0]), (r.at[idg, pl.ds(q, q)], o.at[1], yn, o.at[1])]

    return _exchange_multi(rs, [((2, r.shape[1] // 2, r.shape[2]), r.dtype) for r in rs], plan, "rs_diag")


def _rs_merge(r4, dg, nbr_idx, tag=""):
    _, hrows, Wd = r4.shape
    bt = _pick(hrows // 2, ROW_BLOCKS)
    nb = hrows // bt
    nq = nb // 2

    def body(i_ref, r_ref, d_ref, o_ref):
        w = pl.program_id(0)
        i = pl.program_id(1)
        merged = jnp.where(w == 0, i >= nq, i < nq)
        add = jnp.where(merged, d_ref[...].astype(F32), 0.0)
        o_ref[...] = (r_ref[...].astype(F32) + add).astype(o_ref.dtype)

    grid_spec = pltpu.PrefetchScalarGridSpec(
        num_scalar_prefetch=1, grid=(2, nb),
        in_specs=[pl.BlockSpec((1, bt, Wd), lambda w, i, idx: (idx[w], i, 0)),
                  pl.BlockSpec((1, bt, Wd), lambda w, i, idx: (1 - w, jnp.clip(i - (1 - w) * nq, 0, nq - 1), 0))],
        out_specs=pl.BlockSpec((1, bt, Wd), lambda w, i, idx: (w, i, 0)))
    return pl.pallas_call(
        body, name="rs_merge" + tag, grid_spec=grid_spec,
        out_shape=jax.ShapeDtypeStruct((2, hrows, Wd), r4.dtype),
        compiler_params=_cp(("parallel", "parallel")),
    )(nbr_idx, r4, dg)


def _rs_direct_multi(ms):
    def plan(t, m, o, places):
        (_, _, xn, yn, _) = places
        return [(m.at[0], o.at[0], xn, o.at[0]), (m.at[1], o.at[1], yn, o.at[1])]

    return _exchange_multi(ms, [(m.shape, m.dtype) for m in ms], plan, "rs_direct")


def _rs_final(r4, got, me_c, tag=""):
    _, hrows, Wd = r4.shape
    bt = _pick(hrows, ROW_BLOCKS)
    nb = hrows // bt

    def body(i_ref, r_ref, g_ref, o_ref):
        o_ref[...] = (r_ref[0].astype(F32) + g_ref[0].astype(F32)) + g_ref[1].astype(F32)

    grid_spec = pltpu.PrefetchScalarGridSpec(
        num_scalar_prefetch=1, grid=(nb,),
        in_specs=[pl.BlockSpec((1, bt, Wd), lambda i, idx: (idx[0], i, 0)),
                  pl.BlockSpec((2, bt, Wd), lambda i, idx: (0, i, 0))],
        out_specs=pl.BlockSpec((bt, Wd), lambda i, idx: (idx[1] * nb + i, 0)))
    return pl.pallas_call(
        body, name="rs_final" + tag, grid_spec=grid_spec,
        out_shape=jax.ShapeDtypeStruct((2 * hrows, Wd), F32),
        compiler_params=_cp(("parallel",)),
    )(me_c, r4, got)


def _rs_share_multi(bufs):
    def plan(t, b, o, places):
        (_, here, _, _, sib) = places
        h = bufs[t].shape[0] // 2
        mine = o.at[pl.ds(here[2] * h, h)]
        return [(mine, mine, sib, o.at[pl.ds((1 - here[2]) * h, h)])]

    return _exchange_multi(bufs, [(b.shape, b.dtype) for b in bufs], plan, "rs_share", aliased=True)


def _allreduce_small(v, name="allreduce_small"):
    R, Wd = v.shape

    def body(v_ref, o_ref, buf, send_sems, recv_sems):
        x, y, c = lax.axis_index("x"), lax.axis_index("y"), lax.axis_index("c")
        me = 4 * x + 2 * y + c
        buf[me] = v_ref[...]
        sends = []
        for k in range(1, 8):
            peer = ((x + (k >> 2)) % 2, (y + ((k >> 1) & 1)) % 2, (c + (k & 1)) % 2)
            sends.append(pltpu.make_async_remote_copy(
                src_ref=v_ref, dst_ref=buf.at[me], send_sem=send_sems.at[k - 1], recv_sem=recv_sems.at[k - 1],
                device_id=peer, device_id_type=MESH))
        for cp in sends:
            cp.start()
        for k in range(1, 8):
            px, py, pc = (x + (k >> 2)) % 2, (y + ((k >> 1) & 1)) % 2, (c + (k & 1)) % 2
            pltpu.make_async_remote_copy(
                src_ref=v_ref, dst_ref=buf.at[4 * px + 2 * py + pc], send_sem=send_sems.at[k - 1],
                recv_sem=recv_sems.at[k - 1], device_id=(x, y, c), device_id_type=MESH).wait_recv()
        acc = buf[0]
        for d in range(1, 8):
            acc = acc + buf[d]
        o_ref[...] = acc
        for cp in sends:
            cp.wait_send()

    vm = pl.BlockSpec(memory_space=pltpu.VMEM)
    return pl.pallas_call(
        body, name=name, in_specs=[vm], out_specs=vm,
        out_shape=jax.ShapeDtypeStruct((R, Wd), F32),
        scratch_shapes=[pltpu.VMEM((8, R, Wd), F32), pltpu.SemaphoreType.DMA((7,)), pltpu.SemaphoreType.DMA((7,))],
    )(v)


SMALL = (
    ("mix_norm_g", (2, 1024)), ("xattn_norm_g", (2, 1024)), ("mem_norm_g", (2, 1024)),
    ("ffn_norm_g", (2, 1024)), ("final_norm_g", (1024,)),
    ("forget_bias", (2, 8)), ("sink", (2, 8)), ("rel_bias", (32, 8)),
)
SMALL_AND_CONV = SMALL + (("conv_w", (2, 3, 512)),)


def _small_rows(spec):
    rows = sum(int(np.prod(s)) // 128 if s[-1] % 128 == 0 else s[0] for _, s in spec)
    return -(-rows // 8) * 8


def _pack_small(vals, spec=SMALL):
    rows = []
    for name, shape in spec:
        v = vals[name].astype(F32)
        if shape[-1] % 128 == 0:
            rows.append(v.reshape(-1, 128))
        else:
            rows.append(jnp.pad(v, ((0, 0), (0, 120))))
    rows = jnp.concatenate(rows, axis=0)
    return jnp.pad(rows, ((0, _small_rows(spec) - rows.shape[0]), (0, 0)))


def _unpack_small(pack, spec=SMALL):
    out, off = {}, 0
    for name, shape in spec:
        if shape[-1] % 128 == 0:
            n = int(np.prod(shape)) // 128
            out[name] = pack[off:off + n].reshape(shape)
        else:
            n = shape[0]
            out[name] = pack[off:off + n, 0:8]
        off += n
    return out


W_IN_PERM = ((3848, 6920), (0, 3072), (3080, 3848), (3072, 3080))


def _perm_w_in(w):
    parts = [w[:, a:b] for a, b in W_IN_PERM]
    return jnp.concatenate(parts + [jnp.zeros((w.shape[0], PROJ_PAD - IN_COLS), w.dtype)], axis=1)


def _unperm_w_in(p):
    return jnp.concatenate([p[:, 3072:6144], p[:, 6912:6920], p[:, 6144:6912], p[:, 0:3072]], axis=1)


def _pad_row8(v):
    return jnp.pad(v.astype(F32).reshape(1, 8), ((0, 0), (0, 120)))


def _local_step(x, mem, tgt, W, rel_bias):
    bucket = jnp.asarray(_bucket_table())
    bias = _swa_bias(rel_bias, bucket, "swa_bias")
    saved = []
    for l in range(DEPTH):
        n = "l%d_" % l
        s = {"x0": x}
        wcat = W["w_in_p"][l]
        h = _rms_fwd(x, W["mix_norm_g"][l:l + 1], n + "mix_norm")
        pm = _mm(h, wcat[:, :PROJ_MAIN], "nn", BF16, n + "proj", bn=768)
        fg = _mm(h, wcat[:, PROJ_MAIN:], "nn", F32, n + "proj_fg")
        fb = _pad_row8(W["forget_bias"][l])
        c_col = _fox_gate_fwd(fg, fb, n + "fox_gate")
        c_row = c_col[:, 0:8].T
        cw = jnp.pad(W["conv_w"][l], ((0, 5), (0, 0)))
        y_conv = _conv_fwd(pm, cw, n + "conv")
        y_fox, lse = _fox2_fwd(pm, c_row, n + "fox")
        sink = _pad_row8(W["sink"][l])
        y_swa, mlse = _swa_fwd(pm, bias, sink, n + "swa")
        ys = (y_conv, y_fox, y_swa)
        us = tuple(_mm(ys[b], W["w_branch"][l][b], "nn", BF16, n + "branch%d" % b) for b in range(3))
        merged = _merge_fwd(pm, us, n + "merge")
        x1 = _mm(merged, W["w_mix_out"][l], "nn", F32, n + "mix_out", res=x)
        xn1 = _rms_fwd(x1, W["xattn_norm_g"][l:l + 1], n + "xattn_norm")
        memn = _rms_fwd(mem, W["mem_norm_g"][l:l + 1], n + "mem_norm")
        qx = _mm(xn1, W["w_xq"][l], "nn", BF16, n + "xq")
        kv = _mm(memn, W["w_xkv"][l], "nn", BF16, n + "xkv")
        ox = _xattn_fwd(qx, kv, n + "xattn")
        x2 = _mm(ox, W["w_xo"][l], "nn", F32, n + "xo", res=x1)
        xn2 = _rms_fwd(x2, W["ffn_norm_g"][l:l + 1], n + "ffn_norm")
        ab = _mm(xn2, W["w_gu"][l], "nt", BF16, n + "ffn_in", bn=512)
        hm = _swiglu_fwd(ab, n + "swiglu")
        x3 = _mm(hm, W["w_ffn_down"][l], "nn", F32, n + "ffn_out", res=x2, bk=1408)
        s.update(h=h, pm=pm, fg=fg, fb=fb, c_col=c_col, c_row=c_row, cw=cw, ys=ys, lse=lse, sink=sink,
                 mlse=mlse, us=us, merged=merged, x1=x1, xn1=xn1, memn=memn, qx=qx, kv=kv, ox=ox,
                 x2=x2, xn2=xn2, ab=ab, hm=hm)
        saved.append(s)
        x = x3

    loss_row, dx, dg_final = _final_loss(x, W["final_norm_g"].reshape(1, D_MODEL), tgt, "final_loss")
    G = {name: [None] * DEPTH for name in
         ("mix_norm_g", "w_in_p", "forget_bias", "conv_w", "sink", "w_branch", "w_mix_out", "xattn_norm_g",
          "mem_norm_g", "w_xq", "w_xkv", "w_xo", "ffn_norm_g", "w_gu", "w_ffn_down")}
    dbias_tot = None
    for l in reversed(range(DEPTH)):
        n = "l%d_" % l
        s = saved[l]
        dhm = _mm(dx, W["w_ffn_down"][l], "nt", BF16, n + "d_hm", bn=1408)
        G["w_ffn_down"][l] = _mm(s["hm"], dx, "tn", BF16, n + "dw_down", bm=1408, bk=1024)
        dab = _swiglu_bwd(s["ab"], dhm, n + "d_swiglu")
        dxn2 = _mm(dab, W["w_gu"][l], "nn", BF16, n + "d_xn2", bk=1408)
        G["w_gu"][l] = _mm(dab, s["xn2"], "tn", BF16, n + "dw_gu", bm=512, bk=2048)
        dx, G["ffn_norm_g"][l] = _rms_bwd(s["x2"], W["ffn_norm_g"][l:l + 1], dxn2, dx, n + "d_ffn_norm")
        dox = _mm(dx, W["w_xo"][l], "nt", BF16, n + "d_ox")
        G["w_xo"][l] = _mm(s["ox"], dx, "tn", BF16, n + "dw_xo", bk=1024)
        dqx, dkv = _xattn_bwd(s["qx"], s["kv"], dox, n + "d_xattn")
        dxn1 = _mm(dqx, W["w_xq"][l], "nt", BF16, n + "d_xn1")
        G["w_xq"][l] = _mm(s["xn1"], dqx, "tn", BF16, n + "dw_xq", bk=2048)
        dmemn = _mm(dkv, W["w_xkv"][l], "nt", BF16, n + "d_memn")
        G["w_xkv"][l] = _mm(s["memn"], dkv, "tn", BF16, n + "dw_xkv")
        _, G["mem_norm_g"][l] = _rms_bwd(mem, W["mem_norm_g"][l:l + 1], dmemn, None, n + "d_mem_norm")
        dx, G["xattn_norm_g"][l] = _rms_bwd(s["x1"], W["xattn_norm_g"][l:l + 1], dxn1, dx, n + "d_xattn_norm")
        dmerged = _mm(dx, W["w_mix_out"][l], "nt", BF16, n + "d_merged")
        G["w_mix_out"][l] = _mm(s["merged"], dx, "tn", BF16, n + "dw_mix_out", bk=1024)
        du0, du1, du2, dproj = _merge_bwd(s["pm"], s["us"], dmerged, n + "d_merge")
        dus = (du0, du1, du2)
        dys = [_mm(dus[b], W["w_branch"][l][b], "nt", BF16, n + "d_y%d" % b) for b in range(3)]
        G["w_branch"][l] = [_mm(s["ys"][b], dus[b], "tn", BF16, n + "dw_branch%d" % b, bk=2048) for b in range(3)]
        dproj, dcw = _conv_bwd(s["pm"], s["cw"], dys[0], dproj, n + "d_conv")
        G["conv_w"][l] = dcw[0:3]
        delta = _fox_delta(s["ys"][1], dys[1], n + "fox_delta")
        dproj, delta = _fox2_bwd_dq(s["pm"], dys[1], s["c_row"], s["lse"], delta, dproj, n + "d_fox_q")
        dproj, dc = _fox2_bwd_dkv(s["pm"], dys[1], s["c_col"], s["lse"][:, 0:8].T, delta[:, 0:8].T, dproj,
                                  n + "d_fox_kv")
        dfg, dfb = _fox_gate_bwd(dc, s["fg"], s["fb"], n + "d_fox_gate")
        G["forget_bias"][l] = dfb[0, 0:8]
        dproj, dkc, dkp, dvc, dvp, dbias, dsink = _swa_bwd(s["pm"], bias, s["sink"], dys[2], s["mlse"], dproj,
                                                         n + "d_swa")
        G["sink"][l] = dsink[0, 0:8]
        dbias_tot = dbias if dbias_tot is None else dbias_tot + dbias
        zpad = jnp.zeros((WINDOW, 128), F32)
        dsk = dkc + jnp.concatenate([dkp[WINDOW:], zpad], axis=0)
        dsv = dvc + jnp.concatenate([dvp[WINDOW:], zpad], axis=0)
        tail = jnp.concatenate([dsk.astype(BF16), dsv.astype(BF16), dfg.astype(BF16)], axis=1)
        dproj = lax.dynamic_update_slice(dproj, tail, (0, PROJ_MAIN - 256))
        dh = _mm(dproj, W["w_in_p"][l], "nt", BF16, n + "d_h", bk=1408)
        G["w_in_p"][l] = _mm(s["h"], dproj, "tn", BF16, n + "dw_in", bn=1408, bk=2048)
        dx, G["mix_norm_g"][l] = _rms_bwd(s["x0"], W["mix_norm_g"][l:l + 1], dh, dx, n + "d_mix_norm")
    drb = _swa_dbias_reduce(dbias_tot, bucket, "swa_dbias")
    G["rel_bias"] = drb[:, 0:8]
    G["final_norm_g"] = dg_final.reshape(D_MODEL)
    return loss_row, dx, G


def kernel(x, mem, mix_norm_g, w_in, forget_bias, conv_w, sink, w_branch, w_mix_out, rel_bias, xattn_norm_g, mem_norm_g, w_xq, w_xkv, w_xo, ffn_norm_g, w_ffn_gate, w_ffn_up, w_ffn_down, final_norm_g, loss_target, m_mix_norm_g, m_w_in, m_forget_bias, m_conv_w, m_sink, m_w_branch, m_w_mix_out, m_rel_bias, m_xattn_norm_g, m_mem_norm_g, m_w_xq, m_w_xkv, m_w_xo, m_ffn_norm_g, m_w_ffn_gate, m_w_ffn_up, m_w_ffn_down, m_final_norm_g, v_mix_norm_g, v_w_in, v_forget_bias, v_conv_w, v_sink, v_w_branch, v_w_mix_out, v_rel_bias, v_xattn_norm_g, v_mem_norm_g, v_w_xq, v_w_xkv, v_w_xo, v_ffn_norm_g, v_w_ffn_gate, v_w_ffn_up, v_w_ffn_down, v_final_norm_g):
    order = ("mix_norm_g", "w_in", "forget_bias", "conv_w", "sink", "w_branch", "w_mix_out", "rel_bias",
             "xattn_norm_g", "mem_norm_g", "w_xq", "w_xkv", "w_xo", "ffn_norm_g", "w_ffn_gate", "w_ffn_up",
             "w_ffn_down", "final_norm_g")
    w_sh = dict(zip(order, (mix_norm_g, w_in, forget_bias, conv_w, sink, w_branch, w_mix_out, rel_bias,
                            xattn_norm_g, mem_norm_g, w_xq, w_xkv, w_xo, ffn_norm_g, w_ffn_gate, w_ffn_up,
                            w_ffn_down, final_norm_g)))
    m_sh = dict(zip(order, (m_mix_norm_g, m_w_in, m_forget_bias, m_conv_w, m_sink, m_w_branch, m_w_mix_out,
                            m_rel_bias, m_xattn_norm_g, m_mem_norm_g, m_w_xq, m_w_xkv, m_w_xo, m_ffn_norm_g,
                            m_w_ffn_gate, m_w_ffn_up, m_w_ffn_down, m_final_norm_g)))
    v_sh = dict(zip(order, (v_mix_norm_g, v_w_in, v_forget_bias, v_conv_w, v_sink, v_w_branch, v_w_mix_out,
                            v_rel_bias, v_xattn_norm_g, v_mem_norm_g, v_w_xq, v_w_xkv, v_w_xo, v_ffn_norm_g,
                            v_w_ffn_gate, v_w_ffn_up, v_w_ffn_down, v_final_norm_g)))

    xi, yi, ci = lax.axis_index("x"), lax.axis_index("y"), lax.axis_index("c")
    as_idx = lambda *v: jnp.stack([jnp.asarray(t, I32) for t in v])
    me = 2 * xi + yi
    big = [name for name, _ in BIG]
    two_d = dict(BIG)
    two_d["conv_w"] = (6, 128)

    def slab(a, name):
        return (jnp.swapaxes(a, 1, 2) if name in TRANSPOSED else a).reshape(two_d[name])

    def unslab(a, name):
        shape = w_sh[name].shape
        if name in TRANSPOSED:
            return jnp.swapaxes(a.reshape(shape[0], shape[2], shape[1]), 1, 2)
        return a.reshape(shape)

    gathered = dict(zip(big, _ag_ring_multi(
        [_cast_place(slab(w_sh[name], name), as_idx(me), "place_" + name) for name in big])))
    conv_part = lax.dynamic_update_slice_in_dim(jnp.zeros((DEPTH, 3, BRANCH), F32), 0.5 * conv_w, 128 * me, axis=2)
    conv_full = _allreduce_small(conv_part.reshape(-1, 128), "allgather_conv").reshape(DEPTH, 3, BRANCH)

    def lay(name, l):
        g = gathered[name]
        return g.reshape(4, DEPTH, g.shape[1] // DEPTH, g.shape[2])[:, l]

    def by_cols(name, l):
        g = lay(name, l)
        return jnp.moveaxis(g, 0, 1).reshape(g.shape[1], 4 * g.shape[2])

    def by_rows(name, l):
        g = lay(name, l)
        return g.reshape(4 * g.shape[1], g.shape[2])

    W = {k: w_sh[k] for k in ("mix_norm_g", "forget_bias", "sink", "xattn_norm_g", "mem_norm_g",
                              "ffn_norm_g", "final_norm_g")}
    W["conv_w"] = conv_full
    W["w_in_p"] = [_perm_w_in(by_cols("w_in", l)) for l in range(DEPTH)]
    W["w_gu"] = [jnp.concatenate([by_rows("w_ffn_gate", l), by_rows("w_ffn_up", l)], axis=0) for l in range(DEPTH)]
    W["w_xkv"] = [by_cols("w_xkv", l) for l in range(DEPTH)]
    W["w_branch"] = [[jnp.moveaxis(lay("w_branch", l)[:, BRANCH * b:BRANCH * (b + 1)], 0, 1).reshape(BRANCH, D_MODEL)
                      for b in range(3)] for l in range(DEPTH)]
    for k in ("w_mix_out", "w_xq", "w_xo", "w_ffn_down"):
        W[k] = [by_rows(k, l) for l in range(DEPTH)]
    loss_row, dx, G = _local_step(x[0], mem[0], loss_target[0], W, rel_bias)

    def to_cols(g):
        return jnp.moveaxis(g.reshape(g.shape[0], 4, g.shape[1] // 4), 1, 0)

    def to_rows(g):
        return g.reshape(4, g.shape[0] // 4, g.shape[1])

    per_layer = {
        "w_in": [to_cols(_unperm_w_in(G["w_in_p"][l])) for l in range(DEPTH)],
        "w_branch": [jnp.concatenate([to_cols(g) for g in G["w_branch"][l]], axis=1) for l in range(DEPTH)],
        "w_mix_out": [to_rows(g) for g in G["w_mix_out"]],
        "w_xq": [to_rows(g) for g in G["w_xq"]],
        "w_xkv": [to_cols(g) for g in G["w_xkv"]],
        "w_xo": [to_rows(g) for g in G["w_xo"]],
        "w_ffn_gate": [to_rows(G["w_gu"][l][:D_FF]) for l in range(DEPTH)],
        "w_ffn_up": [to_rows(G["w_gu"][l][D_FF:]) for l in range(DEPTH)],
        "w_ffn_down": [to_rows(g) for g in G["w_ffn_down"]],
    }
    g4 = [jnp.concatenate(per_layer[name], axis=1).astype(BF16) for name in big]
    sib = _rs_sibling_multi(g4)
    pair = [_rs_add_pair(g4[t], sib[t], as_idx(ci), "_" + big[t]) for t in range(len(big))]
    diag = _rs_diag_multi(pair)
    nbrs = as_idx(2 * (1 - xi) + yi, 2 * xi + (1 - yi))
    merged = [_rs_merge(pair[t], diag[t], nbrs, "_" + big[t]) for t in range(len(big))]
    got = _rs_direct_multi(merged)
    reduced = _rs_share_multi([_rs_final(pair[t], got[t], as_idx(me, ci), "_" + big[t]) for t in range(len(big))])

    small = _unpack_small(_allreduce_small(_pack_small({
        "mix_norm_g": jnp.concatenate(G["mix_norm_g"], axis=0),
        "xattn_norm_g": jnp.concatenate(G["xattn_norm_g"], axis=0),
        "mem_norm_g": jnp.concatenate(G["mem_norm_g"], axis=0),
        "ffn_norm_g": jnp.concatenate(G["ffn_norm_g"], axis=0),
        "final_norm_g": G["final_norm_g"],
        "forget_bias": jnp.stack(G["forget_bias"]),
        "sink": jnp.stack(G["sink"]),
        "rel_bias": G["rel_bias"],
        "conv_w": jnp.stack(G["conv_w"]),
    }, SMALL_AND_CONV)), SMALL_AND_CONV)
    grads = {name: unslab(reduced[t], name) for t, name in enumerate(big)}
    grads.update(small)
    grads["conv_w"] = lax.dynamic_slice_in_dim(small["conv_w"], 128 * me, 128, axis=2)

    sm_names = [name for name, _ in SMALL]
    sd, sm_, sv_ = _adamw(_pack_small({k: w_sh[k] for k in sm_names}), _pack_small({k: grads[k] for k in sm_names}),
                          _pack_small({k: m_sh[k] for k in sm_names}), _pack_small({k: v_sh[k] for k in sm_names}),
                          "adamw_small")
    delta, new_m, new_v = _unpack_small(sd), _unpack_small(sm_), _unpack_small(sv_)
    for t, name in enumerate(big + ["conv_w"]):
        if name == "w_in":
            to3 = lambda a: jnp.transpose(a, (2, 0, 1))
            d, nm, nv = _adamw(to3(w_sh[name]), to3(grads[name]), to3(m_sh[name]), to3(v_sh[name]), "adamw_w_in")
            delta[name], new_m[name], new_v[name] = (jnp.transpose(a, (1, 2, 0)) for a in (d, nm, nv))
            continue
        g2 = reduced[t] if t < len(big) else slab(grads[name], name)
        d, nm, nv = _adamw(slab(w_sh[name], name), g2, slab(m_sh[name], name), slab(v_sh[name], name), "adamw_" + name)
        delta[name], new_m[name], new_v[name] = unslab(d, name), unslab(nm, name), unslab(nv, name)

    loss = lax.psum(loss_row[0, 0], ("x", "y", "c"))
    return (loss, dx[None], *[grads[k] for k in order], *[delta[k] for k in order],
            *[new_m[k] for k in order], *[new_v[k] for k in order])
```

```python
import math

import numpy as np
import jax
import jax.numpy as jnp
from jax import lax
from jax.experimental import pallas as pl
from jax.experimental.pallas import tpu as pltpu

F32 = jnp.float32
BF16 = jnp.bfloat16
I32 = jnp.int32

D_MODEL = 1024
DEPTH = 2
HEAD_DIM = 64
BRANCH = 512
N_BUCKETS = 32
WINDOW = 128
MEM_LEN = 256
X_HEADS = 4
X_HEAD_DIM = 256
D_FF = 2816
IN_COLS = 6920
PROJ_MAIN = 6912
PROJ_PAD = 7040
RMS_EPS = 1e-6
NEG = -1e30
ATT_SCALE = 0.125
X_SCALE = 0.0625

ADAM_LR = 0.001
ADAM_B1 = 0.9
ADAM_B2 = 0.999
ADAM_EPS = 1e-08
ADAM_WD = 0.01
ADAM_STEP = 10

VMEM_LIMIT = 48 * 1024 * 1024
MESH = pl.DeviceIdType.MESH

CB_GATE = (0, 1, 2)
CB_B, CB_C, CB_U, CB_FQ, CB_FK, CB_FV, CB_SQ = 6, 7, 8, 9, 10, 11, 12
CB_SK, CB_SV = 52, 53


def _cp(sem):
    return pltpu.CompilerParams(dimension_semantics=sem, vmem_limit_bytes=VMEM_LIMIT)


def _pick(n, prefs):
    for p in prefs:
        if p <= n and n % p == 0:
            return p
    return n


def _dot(a, b, dims):
    return lax.dot_general(a, b, (dims, ((), ())), preferred_element_type=F32)


def _dot_nn(a, b):
    return _dot(a, b, ((1,), (0,)))


def _dot_nt(a, b):
    return _dot(a, b, ((1,), (1,)))


def _dot_tn(a, b):
    return _dot(a, b, ((0,), (0,)))


def _mm(a, b, mode, out_dtype, name, res=None, bm=1024, bn=1024, bk=1024):
    if mode == "nn":
        (M, K), (K2, N) = a.shape, b.shape
    elif mode == "nt":
        (M, K), (N, K2) = a.shape, b.shape
    else:
        (K, M), (K2, N) = a.shape, b.shape
    assert K == K2, (name, a.shape, b.shape)
    bm = _pick(M, (bm, 1024, 512, 256, 128))
    bn = _pick(N, (bn, 1024, 768, 640, 512, 384, 256, 128))
    bk = _pick(K, (bk, 1024, 768, 640, 512, 384, 256, 128))
    nk = K // bk
    if mode == "tn":
        a_spec = pl.BlockSpec((bk, bm), lambda i, j, k: (k, i))
    else:
        a_spec = pl.BlockSpec((bm, bk), lambda i, j, k: (i, k))
    if mode == "nt":
        b_spec = pl.BlockSpec((bn, bk), lambda i, j, k: (j, k))
    else:
        b_spec = pl.BlockSpec((bk, bn), lambda i, j, k: (k, j))
    dims = {"nn": ((1,), (0,)), "nt": ((1,), (1,)), "tn": ((0,), (0,))}[mode]
    o_spec = pl.BlockSpec((bm, bn), lambda i, j, k: (i, j))
    has_res = res is not None

    def body(*refs):
        if has_res:
            a_ref, b_ref, r_ref, o_ref = refs[:4]
            scr = refs[4:]
        else:
            a_ref, b_ref, o_ref = refs[:3]
            r_ref = None
            scr = refs[3:]
        p = _dot(a_ref[...].astype(BF16), b_ref[...].astype(BF16), dims)
        if nk == 1:
            if has_res:
                p = p + r_ref[...]
            o_ref[...] = p.astype(out_dtype)
        else:
            acc = scr[0]
            k = pl.program_id(2)

            @pl.when(k == 0)
            def _():
                acc[...] = p

            @pl.when(k > 0)
            def _():
                acc[...] += p

            @pl.when(k == nk - 1)
            def _():
                r = acc[...]
                if has_res:
                    r = r + r_ref[...]
                o_ref[...] = r.astype(out_dtype)

    ins = [a, b] + ([res] if has_res else [])
    in_specs = [a_spec, b_spec] + ([o_spec] if has_res else [])
    return pl.pallas_call(
        body, name=name, grid=(M // bm, N // bn, nk),
        in_specs=in_specs, out_specs=o_spec,
        out_shape=jax.ShapeDtypeStruct((M, N), out_dtype),
        scratch_shapes=[pltpu.VMEM((bm, bn), F32)] if nk > 1 else [],
        compiler_params=_cp(("parallel", "parallel", "arbitrary")),
    )(*ins)


def _rms_fwd(x, g, name):
    T, Dm = x.shape
    bt = _pick(T, (512, 256))

    def body(x_ref, g_ref, o_ref):
        xv = x_ref[...]
        r = lax.rsqrt(jnp.mean(xv * xv, axis=-1, keepdims=True) + RMS_EPS)
        o_ref[...] = ((xv * r) * g_ref[...]).astype(BF16)

    return pl.pallas_call(
        body, name=name, grid=(T // bt,),
        in_specs=[pl.BlockSpec((bt, Dm), lambda i: (i, 0)), pl.BlockSpec((1, Dm), lambda i: (0, 0))],
        out_specs=pl.BlockSpec((bt, Dm), lambda i: (i, 0)),
        out_shape=jax.ShapeDtypeStruct((T, Dm), BF16),
        compiler_params=_cp(("parallel",)),
    )(x, g)


def _rms_bwd(x, g, dh, dres, name):
    T, Dm = x.shape
    bt = _pick(T, (512, 256))
    want_dx = dres is not None

    def body(*refs):
        if want_dx:
            x_ref, g_ref, dh_ref, dr_ref, dx_ref, dg_ref = refs
        else:
            x_ref, g_ref, dh_ref, dg_ref = refs
        xv = x_ref[...]
        r = lax.rsqrt(jnp.mean(xv * xv, axis=-1, keepdims=True) + RMS_EPS)
        xh = xv * r
        dhv = dh_ref[...].astype(F32)

        @pl.when(pl.program_id(0) == 0)
        def _():
            dg_ref[...] = jnp.zeros_like(dg_ref)

        dg_ref[...] += jnp.sum(dhv * xh, axis=0, keepdims=True)
        if want_dx:
            dyg = dhv * g_ref[...]
            dx_ref[...] = dr_ref[...] + r * (dyg - xh * jnp.mean(dyg * xh, axis=-1, keepdims=True))

    row = pl.BlockSpec((bt, Dm), lambda i: (i, 0))
    vec = pl.BlockSpec((1, Dm), lambda i: (0, 0))
    if want_dx:
        return pl.pallas_call(
            body, name=name, grid=(T // bt,),
            in_specs=[row, vec, row, row], out_specs=[row, vec],
            out_shape=[jax.ShapeDtypeStruct((T, Dm), F32), jax.ShapeDtypeStruct((1, Dm), F32)],
            compiler_params=_cp(("arbitrary",)),
        )(x, g, dh, dres)
    return None, pl.pallas_call(
        body, name=name, grid=(T // bt,),
        in_specs=[row, vec, row], out_specs=vec,
        out_shape=jax.ShapeDtypeStruct((1, Dm), F32),
        compiler_params=_cp(("arbitrary",)),
    )(x, g, dh)


def _final_loss(x, g, tgt, name):
    T, Dm = x.shape
    bt = _pick(T, (512, 256))

    def body(x_ref, g_ref, t_ref, loss_ref, dx_ref, dg_ref):
        xv = x_ref[...]
        r = lax.rsqrt(jnp.mean(xv * xv, axis=-1, keepdims=True) + RMS_EPS)
        xh = xv * r
        gv = g_ref[...]
        err = xh * gv - t_ref[...]

        @pl.when(pl.program_id(0) == 0)
        def _():
            dg_ref[...] = jnp.zeros_like(dg_ref)
            loss_ref[...] = jnp.zeros_like(loss_ref)

        loss_ref[...] += jnp.sum(err * err) * (0.5 / Dm)
        dy = err * (1.0 / Dm)
        dg_ref[...] += jnp.sum(dy * xh, axis=0, keepdims=True)
        dyg = dy * gv
        dx_ref[...] = r * (dyg - xh * jnp.mean(dyg * xh, axis=-1, keepdims=True))

    row = pl.BlockSpec((bt, Dm), lambda i: (i, 0))
    vec = pl.BlockSpec((1, Dm), lambda i: (0, 0))
    return pl.pallas_call(
        body, name=name, grid=(T // bt,),
        in_specs=[row, vec, row],
        out_specs=[pl.BlockSpec((1, 128), lambda i: (0, 0)), row, vec],
        out_shape=[jax.ShapeDtypeStruct((1, 128), F32), jax.ShapeDtypeStruct((T, Dm), F32),
                   jax.ShapeDtypeStruct((1, Dm), F32)],
        compiler_params=_cp(("arbitrary",)),
    )(x, g, tgt)


HALO = 16


def _shift_down(z, zprev, s):
    rolled = pltpu.roll(z, s, 0)
    hp = pltpu.roll(zprev, s, 0)
    row = lax.broadcasted_iota(I32, hp.shape, 0)
    top = jnp.where(row < s, hp, rolled[:HALO])
    return jnp.concatenate([top, rolled[HALO:]], axis=0)


def _shift_up(z, znext, s):
    n = z.shape[0]
    rolled = pltpu.roll(z, n - s, 0)
    hn = pltpu.roll(znext, HALO - s, 0)
    row = lax.broadcasted_iota(I32, hn.shape, 0)
    bot = jnp.where(row >= HALO - s, hn, rolled[n - HALO:])
    return jnp.concatenate([rolled[:n - HALO], bot], axis=0)


def _conv_fwd(pm, cw, name):
    T = pm.shape[0]
    bt = _pick(T, (512, 256))
    hb = bt // HALO

    def body(b_ref, c_ref, u_ref, cp_ref, up_ref, w_ref, o_ref):
        i = pl.program_id(0)
        z = c_ref[...].astype(F32) * u_ref[...].astype(F32)
        zp = cp_ref[...].astype(F32) * up_ref[...].astype(F32)
        zp = jnp.where(i > 0, zp, 0.0)
        w = w_ref[...]
        y = w[2:3] * z + w[1:2] * _shift_down(z, zp, 1) + w[0:1] * _shift_down(z, zp, 2)
        o_ref[...] = (b_ref[...].astype(F32) * y).astype(BF16)

    def col(cb):
        return pl.BlockSpec((bt, BRANCH), lambda i: (i, cb))

    def prev(cb):
        return pl.BlockSpec((HALO, BRANCH), lambda i: (jnp.maximum(i * hb - 1, 0), cb))

    return pl.pallas_call(
        body, name=name, grid=(T // bt,),
        in_specs=[col(CB_B), col(CB_C), col(CB_U), prev(CB_C), prev(CB_U),
                  pl.BlockSpec((8, BRANCH), lambda i: (0, 0))],
        out_specs=pl.BlockSpec((bt, BRANCH), lambda i: (i, 0)),
        out_shape=jax.ShapeDtypeStruct((T, BRANCH), BF16),
        compiler_params=_cp(("parallel",)),
    )(pm, pm, pm, pm, pm, cw)


def _conv_bwd(pm, cw, dy, dproj, name):
    T = pm.shape[0]
    bt = _pick(T, (512, 256))
    hb = bt // HALO
    nb = T // bt
    last_h = T // HALO - 1

    def body(b_ref, c_ref, u_ref, cp_ref, up_ref, bn_ref, dy_ref, dyn_ref, w_ref, buf_ref,
             dp_ref, dw_ref):
        del buf_ref
        db_ref = dp_ref.at[:, 0:BRANCH]
        dc_ref = dp_ref.at[:, BRANCH:2 * BRANCH]
        du_ref = dp_ref.at[:, 2 * BRANCH:3 * BRANCH]
        i = pl.program_id(0)
        cv = c_ref[...].astype(F32)
        uv = u_ref[...].astype(F32)
        bv = b_ref[...].astype(F32)
        z = cv * uv
        zp = jnp.where(i > 0, cp_ref[...].astype(F32) * up_ref[...].astype(F32), 0.0)
        w = w_ref[...]
        z1 = _shift_down(z, zp, 1)
        z2 = _shift_down(z, zp, 2)
        yc = w[2:3] * z + w[1:2] * z1 + w[0:1] * z2
        dyv = dy_ref[...].astype(F32)
        db_ref[...] = (dyv * yc).astype(BF16)
        g = dyv * bv
        gn = jnp.where(i < nb - 1, dyn_ref[...].astype(F32) * bn_ref[...].astype(F32), 0.0)
        dz = w[2:3] * g + w[1:2] * _shift_up(g, gn, 1) + w[0:1] * _shift_up(g, gn, 2)
        dc_ref[...] = (dz * uv).astype(BF16)
        du_ref[...] = (dz * cv).astype(BF16)

        @pl.when(i == 0)
        def _():
            dw_ref[...] = jnp.zeros_like(dw_ref)

        dw_ref[0:1, :] += jnp.sum(g * z2, axis=0, keepdims=True)
        dw_ref[1:2, :] += jnp.sum(g * z1, axis=0, keepdims=True)
        dw_ref[2:3, :] += jnp.sum(g * z, axis=0, keepdims=True)

    def col(cb):
        return pl.BlockSpec((bt, BRANCH), lambda i: (i, cb))

    def prev(cb):
        return pl.BlockSpec((HALO, BRANCH), lambda i: (jnp.maximum(i * hb - 1, 0), cb))

    def nxt(cb):
        return pl.BlockSpec((HALO, BRANCH), lambda i: (jnp.minimum((i + 1) * hb, last_h), cb))

    own = pl.BlockSpec((bt, BRANCH), lambda i: (i, 0))
    w_spec = pl.BlockSpec((8, BRANCH), lambda i: (0, 0))
    return pl.pallas_call(
        body, name=name, grid=(nb,),
        in_specs=[col(CB_B), col(CB_C), col(CB_U), prev(CB_C), prev(CB_U), nxt(CB_B), own,
                  pl.BlockSpec((HALO, BRANCH), lambda i: (jnp.minimum((i + 1) * hb, last_h), 0)), w_spec,
                  pl.BlockSpec(memory_space=pl.ANY)],
        out_specs=[pl.BlockSpec((bt, 3 * BRANCH), lambda i: (i, 2)), w_spec],
        out_shape=[jax.ShapeDtypeStruct(dproj.shape, dproj.dtype), jax.ShapeDtypeStruct((8, BRANCH), F32)],
        input_output_aliases={9: 0},
        compiler_params=_cp(("arbitrary",)),
    )(pm, pm, pm, pm, pm, pm, dy, dy, cw, dproj)


def _log_sigmoid(z):
    return jnp.minimum(z, 0.0) - jnp.log(1.0 + jnp.exp(-jnp.abs(z)))


def _fox_gate_fwd(fg, fb, name):
    T = fg.shape[0]
    bt = _pick(T, (256,))

    def body(f_ref, b_ref, c_ref, carry):
        @pl.when(pl.program_id(0) == 0)
        def _():
            carry[...] = jnp.zeros_like(carry)

        xv = _log_sigmoid(f_ref[...] + b_ref[...])
        row = lax.broadcasted_iota(I32, xv.shape, 0)
        s = 1
        while s < bt:
            xv = xv + jnp.where(row >= s, pltpu.roll(xv, s, 0), 0.0)
            s *= 2
        xv = xv + carry[...]
        c_ref[...] = xv
        carry[...] = xv[bt - 1:bt, :]

    blk = pl.BlockSpec((bt, 128), lambda i: (i, 0))
    return pl.pallas_call(
        body, name=name, grid=(T // bt,),
        in_specs=[blk, pl.BlockSpec((1, 128), lambda i: (0, 0))],
        out_specs=blk, out_shape=jax.ShapeDtypeStruct((T, 128), F32),
        scratch_shapes=[pltpu.VMEM((1, 128), F32)],
        compiler_params=_cp(("arbitrary",)),
    )(fg, fb)


def _fox_gate_bwd(dc, fg, fb, name):
    T = fg.shape[0]
    bt = _pick(T, (256,))
    nb = T // bt

    def body(d_ref, f_ref, b_ref, o_ref, db_ref, carry):
        @pl.when(pl.program_id(0) == 0)
        def _():
            carry[...] = jnp.zeros_like(carry)
            db_ref[...] = jnp.zeros_like(db_ref)

        xv = d_ref[...]
        row = lax.broadcasted_iota(I32, xv.shape, 0)
        s = 1
        while s < bt:
            xv = xv + jnp.where(row < bt - s, pltpu.roll(xv, bt - s, 0), 0.0)
            s *= 2
        xv = xv + carry[...]
        carry[...] = xv[0:1, :]
        z = f_ref[...] + b_ref[...]
        dz = xv * (1.0 / (1.0 + jnp.exp(z)))
        o_ref[...] = dz
        db_ref[...] += jnp.sum(dz, axis=0, keepdims=True)

    blk = pl.BlockSpec((bt, 128), lambda i: (nb - 1 - i, 0))
    vec = pl.BlockSpec((1, 128), lambda i: (0, 0))
    return pl.pallas_call(
        body, name=name, grid=(nb,),
        in_specs=[blk, blk, vec], out_specs=[blk, vec],
        out_shape=[jax.ShapeDtypeStruct((T, 128), F32), jax.ShapeDtypeStruct((1, 128), F32)],
        scratch_shapes=[pltpu.VMEM((1, 128), F32)],
        compiler_params=_cp(("arbitrary",)),
    )(dc, fg, fb)


def _lane_lo(shape):
    return lax.broadcasted_iota(I32, shape, 1) < HEAD_DIM


def _put_col(shape, h, col):
    lane = lax.broadcasted_iota(I32, shape, 1)
    return jnp.where(lane == h, col, 0.0)


def _fox_delta(o, do, name):
    T = o.shape[0]
    bt = _pick(T, (512, 256))

    def body(o_ref, d_ref, out_ref):
        prod = o_ref[...].astype(F32) * d_ref[...].astype(F32)
        out = jnp.zeros((bt, 128), F32)
        for h in range(8):
            out = out + _put_col((bt, 128), h, jnp.sum(prod[:, 64 * h:64 * h + 64], axis=-1, keepdims=True))
        out_ref[...] = out

    blk = pl.BlockSpec((bt, BRANCH), lambda i: (i, 0))
    return pl.pallas_call(
        body, name=name, grid=(T // bt,), in_specs=[blk, blk],
        out_specs=pl.BlockSpec((bt, 128), lambda i: (i, 0)),
        out_shape=jax.ShapeDtypeStruct((T, 128), F32),
        compiler_params=_cp(("parallel",)),
    )(o, do)


FOX_ROWS = 32


def _chunk_loop(n, chunk):
    for r in range(n):
        chunk(r)


def _tree(op, xs):
    xs = list(xs)
    while len(xs) > 1:
        xs = [op(xs[i], xs[i + 1]) if i + 1 < len(xs) else xs[i] for i in range(0, len(xs), 2)]
    return xs[0]


def _masked_halves(t):
    lo = _lane_lo(t.shape)
    z = jnp.zeros_like(t)
    return jnp.where(lo, t, z), jnp.where(lo, z, t)


def _fox2_fwd(pm, c_row, name):
    T = pm.shape[0]
    bq = _pick(T, (512, 256))
    bk = bq
    nq = T // bq
    R = FOX_ROWS
    ng = bk // 128

    def body(q_ref, k_ref, v_ref, ck_ref, o_ref, lse_ref, acc, m_s, l_s, a_s, s_scr, p_scr):
        qi = pl.program_id(0)
        ki = pl.program_id(1)

        @pl.when(ki == 0)
        def _():
            acc[...] = jnp.zeros_like(acc)
            m_s[...] = jnp.full_like(m_s, NEG)
            l_s[...] = jnp.zeros_like(l_s)

        def block(masked):
            qlo = _lane_lo((bq, 128))
            for p in range(4):
                sl = slice(128 * p, 128 * p + 128)
                qp = q_ref[:, sl] * ATT_SCALE
                vp = v_ref[:, sl]
                ks = _masked_halves(k_ref[:, sl])
                pvs = []
                for j in range(2):
                    h = 2 * p + j
                    s_scr[j] = _dot_nt(qp, ks[j])

                    def chunk(r, h=h, j=j):
                        r0 = r * R
                        rows = pl.ds(r0, R)
                        sc = [s_scr[j, rows, 128 * g:128 * g + 128] - ck_ref[h:h + 1, 128 * g:128 * g + 128]
                              for g in range(ng)]
                        if masked:
                            rid = lax.broadcasted_iota(I32, (R, 128), 0) + r0
                            cid = lax.broadcasted_iota(I32, (R, 128), 1)
                            sc = [jnp.where(cid + 128 * g <= rid, sc[g], NEG) for g in range(ng)]
                        m_old = m_s[h, rows, :]
                        m_new = jnp.maximum(m_old, jnp.max(_tree(jnp.maximum, sc), axis=-1, keepdims=True))
                        alpha = jnp.exp(m_old - m_new)
                        pe = [jnp.exp(sc[g] - m_new) for g in range(ng)]
                        l_s[h, rows, :] = alpha * l_s[h, rows, :] + _tree(jnp.add, pe)
                        m_s[h, rows, :] = m_new
                        a_s[j, rows, :] = alpha
                        for g in range(ng):
                            p_scr[j, rows, 128 * g:128 * g + 128] = pe[g].astype(BF16)

                    _chunk_loop(bq // R, chunk)
                    pvs.append(_dot_nn(p_scr[j], vp))
                acc[:, sl] = jnp.where(qlo, a_s[0], a_s[1]) * acc[:, sl] + jnp.where(qlo, pvs[0], pvs[1])

        @pl.when(ki < qi)
        def _():
            block(False)

        @pl.when(ki == qi)
        def _():
            block(True)

        @pl.when(ki == nq - 1)
        def _():
            qlo = _lane_lo((bq, 128))
            lse = jnp.zeros((bq, 128), F32)
            for p in range(4):
                sl = slice(128 * p, 128 * p + 128)
                l0 = jnp.sum(l_s[2 * p], axis=-1, keepdims=True)
                l1 = jnp.sum(l_s[2 * p + 1], axis=-1, keepdims=True)
                o_ref[:, sl] = (acc[:, sl] / jnp.where(qlo, l0, l1)).astype(BF16)
                lse = lse + _put_col((bq, 128), 2 * p, m_s[2 * p][:, 0:1] + jnp.log(l0))
                lse = lse + _put_col((bq, 128), 2 * p + 1, m_s[2 * p + 1][:, 0:1] + jnp.log(l1))
            lse_ref[...] = lse

    return pl.pallas_call(
        body, name=name, grid=(nq, nq),
        in_specs=[pl.BlockSpec((bq, BRANCH), lambda i, k: (i, CB_FQ)),
                  pl.BlockSpec((bk, BRANCH), lambda i, k: (jnp.minimum(k, i), CB_FK)),
                  pl.BlockSpec((bk, BRANCH), lambda i, k: (jnp.minimum(k, i), CB_FV)),
                  pl.BlockSpec((8, bk), lambda i, k: (0, jnp.minimum(k, i)))],
        out_specs=[pl.BlockSpec((bq, BRANCH), lambda i, k: (i, 0)),
                   pl.BlockSpec((bq, 128), lambda i, k: (i, 0))],
        out_shape=[jax.ShapeDtypeStruct((T, BRANCH), BF16), jax.ShapeDtypeStruct((T, 128), F32)],
        scratch_shapes=[pltpu.VMEM((bq, BRANCH), F32), pltpu.VMEM((8, bq, 128), F32),
                        pltpu.VMEM((8, bq, 128), F32), pltpu.VMEM((2, bq, 128), F32),
                        pltpu.VMEM((2, bq, bk), F32), pltpu.VMEM((2, bq, bk), BF16)],
        compiler_params=_cp(("parallel", "arbitrary")),
    )(pm, pm, pm, c_row)


def _fox2_bwd_dq(pm, do, c_row, lse, delta, dproj, name):
    T = pm.shape[0]
    bq = _pick(T, (512, 256))
    bk = bq
    nq = T // bq
    R = FOX_ROWS
    ng = bk // 128

    def body(q_ref, k_ref, v_ref, do_ref, ck_ref, lse_ref, dl_ref, buf_ref, dq_ref, dl2_ref,
             acc, e_s, s_scr, dp_scr, ds_scr):
        del buf_ref
        qi = pl.program_id(0)
        ki = pl.program_id(1)

        @pl.when(ki == 0)
        def _():
            acc[...] = jnp.zeros_like(acc)
            e_s[...] = jnp.zeros_like(e_s)

        def block(masked):
            qlo = _lane_lo((bq, 128))
            for p in range(4):
                sl = slice(128 * p, 128 * p + 128)
                qp = q_ref[:, sl] * ATT_SCALE
                kp = k_ref[:, sl]
                dop = do_ref[:, sl]
                ks = _masked_halves(kp)
                vs = _masked_halves(v_ref[:, sl])
                dqs = []
                for j in range(2):
                    h = 2 * p + j
                    s_scr[...] = _dot_nt(qp, ks[j])
                    dp_scr[...] = _dot_nt(dop, vs[j])

                    def chunk(r, h=h):
                        r0 = r * R
                        rows = pl.ds(r0, R)
                        lse_c = lse_ref[rows, h:h + 1]
                        dl_c = dl_ref[rows, h:h + 1]
                        if masked:
                            rid = lax.broadcasted_iota(I32, (R, 128), 0) + r0
                            cid = lax.broadcasted_iota(I32, (R, 128), 1)
                        dss = []
                        for g in range(ng):
                            gs = slice(128 * g, 128 * g + 128)
                            sc = s_scr[rows, gs] - ck_ref[h:h + 1, gs]
                            if masked:
                                sc = jnp.where(cid + 128 * g <= rid, sc, NEG)
                            ds = jnp.exp(sc - lse_c) * (dp_scr[rows, gs] - dl_c)
                            ds_scr[rows, gs] = ds.astype(BF16)
                            dss.append(ds)
                        e_s[h, rows, :] += _tree(jnp.add, dss)

                    _chunk_loop(bq // R, chunk)
                    dqs.append(_dot_nn(ds_scr[...], kp))
                acc[:, sl] += jnp.where(qlo, dqs[0], dqs[1])

        @pl.when(ki < qi)
        def _():
            block(False)

        @pl.when(ki == qi)
        def _():
            block(True)

        @pl.when(ki == nq - 1)
        def _():
            dq_ref[...] = (acc[...] * ATT_SCALE).astype(BF16)
            out = dl_ref[...]
            for h in range(8):
                out = out + _put_col((bq, 128), h, jnp.sum(e_s[h], axis=-1, keepdims=True))
            dl2_ref[...] = out

    qb = pl.BlockSpec((bq, 128), lambda i, k: (i, 0))
    return pl.pallas_call(
        body, name=name, grid=(nq, nq),
        in_specs=[pl.BlockSpec((bq, BRANCH), lambda i, k: (i, CB_FQ)),
                  pl.BlockSpec((bk, BRANCH), lambda i, k: (jnp.minimum(k, i), CB_FK)),
                  pl.BlockSpec((bk, BRANCH), lambda i, k: (jnp.minimum(k, i), CB_FV)),
                  pl.BlockSpec((bq, BRANCH), lambda i, k: (i, 0)),
                  pl.BlockSpec((8, bk), lambda i, k: (0, jnp.minimum(k, i))), qb, qb,
                  pl.BlockSpec(memory_space=pl.ANY)],
        out_specs=[pl.BlockSpec((bq, BRANCH), lambda i, k: (i, CB_FQ)), qb],
        out_shape=[jax.ShapeDtypeStruct(dproj.shape, dproj.dtype), jax.ShapeDtypeStruct((T, 128), F32)],
        input_output_aliases={7: 0},
        scratch_shapes=[pltpu.VMEM((bq, BRANCH), F32), pltpu.VMEM((8, bq, 128), F32),
                        pltpu.VMEM((bq, bk), F32), pltpu.VMEM((bq, bk), F32), pltpu.VMEM((bq, bk), BF16)],
        compiler_params=_cp(("parallel", "arbitrary")),
    )(pm, pm, pm, do, c_row, lse, delta, dproj)


def _fox2_bwd_dkv(pm, do, c_col, lse_row, delta_row, dproj, name):
    T = pm.shape[0]
    bk = _pick(T, (512, 256))
    bq = bk
    nk = T // bk
    R = FOX_ROWS
    ng = bq // 128

    def body(q_ref, k_ref, v_ref, do_ref, ck_ref, lse_ref, dl_ref, buf_ref, dkv_ref, dc_ref,
             dk_acc, dv_acc, dc_s, st_scr, dpt_scr, pt_scr, dst_scr):
        del buf_ref
        dk_ref = dkv_ref.at[:, 0:BRANCH]
        dv_ref = dkv_ref.at[:, BRANCH:2 * BRANCH]
        ki = pl.program_id(0)
        qi = pl.program_id(1)

        @pl.when(qi == 0)
        def _():
            dk_acc[...] = jnp.zeros_like(dk_acc)
            dv_acc[...] = jnp.zeros_like(dv_acc)
            dc_s[...] = jnp.zeros_like(dc_s)

        def block(masked):
            klo = _lane_lo((bk, 128))
            for p in range(4):
                sl = slice(128 * p, 128 * p + 128)
                qp = q_ref[:, sl]
                kp = k_ref[:, sl] * ATT_SCALE
                vp = v_ref[:, sl]
                dop = do_ref[:, sl]
                qs = _masked_halves(qp)
                dos = _masked_halves(dop)
                dks, dvs = [], []
                for j in range(2):
                    h = 2 * p + j
                    st_scr[...] = _dot_nt(kp, qs[j])
                    dpt_scr[...] = _dot_nt(vp, dos[j])

                    def chunk(r, h=h):
                        r0 = r * R
                        rows = pl.ds(r0, R)
                        ck_c = ck_ref[rows, h:h + 1]
                        if masked:
                            kid = lax.broadcasted_iota(I32, (R, 128), 0) + r0
                            qid = lax.broadcasted_iota(I32, (R, 128), 1)
                        dss = []
                        for g in range(ng):
                            gs = slice(128 * g, 128 * g + 128)
                            st = st_scr[rows, gs] - (ck_c + lse_ref[h:h + 1, gs])
                            if masked:
                                st = jnp.where(kid <= qid + 128 * g, st, NEG)
                            pt = jnp.exp(st)
                            dst = pt * (dpt_scr[rows, gs] - dl_ref[h:h + 1, gs])
                            pt_scr[rows, gs] = pt.astype(BF16)
                            dst_scr[rows, gs] = dst.astype(BF16)
                            dss.append(dst)
                        dc_s[h, rows, :] -= _tree(jnp.add, dss)

                    _chunk_loop(bk // R, chunk)
                    dvs.append(_dot_nn(pt_scr[...], dop))
                    dks.append(_dot_nn(dst_scr[...], qp))
                dk_acc[:, sl] += jnp.where(klo, dks[0], dks[1])
                dv_acc[:, sl] += jnp.where(klo, dvs[0], dvs[1])

        @pl.when(qi > ki)
        def _():
            block(False)

        @pl.when(qi == ki)
        def _():
            block(True)

        @pl.when(qi == nk - 1)
        def _():
            dk_ref[...] = (dk_acc[...] * ATT_SCALE).astype(BF16)
            dv_ref[...] = dv_acc[...].astype(BF16)
            out = jnp.zeros((bk, 128), F32)
            for h in range(8):
                out = out + _put_col((bk, 128), h, jnp.sum(dc_s[h], axis=-1, keepdims=True))
            dc_ref[...] = out

    qrow = pl.BlockSpec((8, bq), lambda k, i: (0, jnp.maximum(i, k)))
    return pl.pallas_call(
        body, name=name, grid=(nk, nk),
        in_specs=[pl.BlockSpec((bq, BRANCH), lambda k, i: (jnp.maximum(i, k), CB_FQ)),
                  pl.BlockSpec((bk, BRANCH), lambda k, i: (k, CB_FK)),
                  pl.BlockSpec((bk, BRANCH), lambda k, i: (k, CB_FV)),
                  pl.BlockSpec((bq, BRANCH), lambda k, i: (jnp.maximum(i, k), 0)),
                  pl.BlockSpec((bk, 128), lambda k, i: (k, 0)), qrow, qrow, pl.BlockSpec(memory_space=pl.ANY)],
        out_specs=[pl.BlockSpec((bk, 2 * BRANCH), lambda k, i: (k, 5)), pl.BlockSpec((bk, 128), lambda k, i: (k, 0))],
        out_shape=[jax.ShapeDtypeStruct(dproj.shape, dproj.dtype), jax.ShapeDtypeStruct((T, 128), F32)],
        input_output_aliases={7: 0},
        scratch_shapes=[pltpu.VMEM((bk, BRANCH), F32), pltpu.VMEM((bk, BRANCH), F32),
                        pltpu.VMEM((8, bk, 128), F32), pltpu.VMEM((bk, bq), F32), pltpu.VMEM((bk, bq), F32),
                        pltpu.VMEM((bk, bq), BF16), pltpu.VMEM((bk, bq), BF16)],
        compiler_params=_cp(("parallel", "arbitrary")),
    )(pm, pm, pm, do, c_col, lse_row, delta_row, dproj)


def _bucket_table():
    tq = np.arange(WINDOW, dtype=np.int32)[:, None]
    sk = np.arange(2 * WINDOW, dtype=np.int32)[None, :]
    n = np.maximum(WINDOW + tq - sk, 0)
    max_exact = N_BUCKETS // 2
    ratio = np.maximum(n, 1).astype(np.float32) / np.float32(max_exact)
    large = max_exact + (np.log(ratio) / np.float32(math.log(WINDOW / max_exact))
                         * np.float32(N_BUCKETS - max_exact)).astype(np.int32)
    large = np.minimum(large, N_BUCKETS - 1)
    return np.where(n < max_exact, n, large).astype(np.int32)


def _swap_halves(x):
    return pltpu.roll(x.astype(F32), HEAD_DIM, 1).astype(x.dtype)


def _kv_variants(t):
    lo = _lane_lo(t.shape)
    z = jnp.zeros_like(t)
    a0 = jnp.where(lo, t, z)
    b1 = jnp.where(lo, z, t)
    b0 = _swap_halves(a0)
    a1 = _swap_halves(b1)
    return (a0, a1), (b0, b1), (a0 + b0, a1 + b1)


def _stacked_head(s, r):
    return 4 * (s // 2) + 2 * r + (s % 2)


def _swa_bias(rel_bias, bucket, name):
    def body(rb_ref, bk_ref, o_ref):
        bkt = bk_ref[...]
        tq = lax.broadcasted_iota(I32, bkt.shape, 0)
        jj = lax.broadcasted_iota(I32, bkt.shape, 1)
        window = ((jj < WINDOW) & (jj > tq)) | ((jj >= WINDOW) & (jj - WINDOW <= tq))
        for s in range(4):
            for r in range(2):
                h = _stacked_head(s, r)

                def step(b, a, h=h):
                    return a + jnp.where(bkt == b, rb_ref[b, h], 0.0)
                val = lax.fori_loop(0, N_BUCKETS, step, jnp.zeros(bkt.shape, F32))
                o_ref[s, WINDOW * r:WINDOW * (r + 1), :] = jnp.where(window, val, NEG)

    return pl.pallas_call(
        body, name=name,
        in_specs=[pl.BlockSpec(memory_space=pltpu.SMEM), pl.BlockSpec(memory_space=pltpu.VMEM)],
        out_specs=pl.BlockSpec(memory_space=pltpu.VMEM),
        out_shape=jax.ShapeDtypeStruct((4, 2 * WINDOW, 2 * WINDOW), F32),
    )(rel_bias, bucket)


def _swa_dbias_reduce(dbias, bucket, name):
    def body(d_ref, bk_ref, o_ref):
        bkt = bk_ref[...]
        rowi = lax.broadcasted_iota(I32, (N_BUCKETS, 128), 0)
        lane = lax.broadcasted_iota(I32, (N_BUCKETS, 128), 1)
        out = jnp.zeros((N_BUCKETS, 128), F32)
        for s in range(4):
            for r in range(2):
                h = _stacked_head(s, r)
                dv = d_ref[s, WINDOW * r:WINDOW * (r + 1), :]

                def step(b, a, dv=dv, h=h):
                    tot = jnp.sum(jnp.where(bkt == b, dv, 0.0), keepdims=True)
                    return a + jnp.where((rowi == b) & (lane == h), tot, 0.0)
                out = lax.fori_loop(0, N_BUCKETS, step, out)
        o_ref[...] = out

    return pl.pallas_call(
        body, name=name,
        in_specs=[pl.BlockSpec(memory_space=pltpu.VMEM), pl.BlockSpec(memory_space=pltpu.VMEM)],
        out_specs=pl.BlockSpec(memory_space=pltpu.VMEM),
        out_shape=jax.ShapeDtypeStruct((N_BUCKETS, 128), F32),
    )(dbias, bucket)


def _swa_cols(vec, s):
    rows = lax.broadcasted_iota(I32, (2 * WINDOW, 1), 0)
    return jnp.where(rows < WINDOW, vec[:, _stacked_head(s, 0):_stacked_head(s, 0) + 1],
                     vec[:, _stacked_head(s, 1):_stacked_head(s, 1) + 1])


def _swa_scores(qg, kband, bias_tile, first):
    sc = _dot_nt(qg, kband) + bias_tile
    if first is not None:
        jj = lax.broadcasted_iota(I32, sc.shape, 1)
        sc = jnp.where((jj < WINDOW) & first, NEG, sc)
    return sc


def _swa_stack(ref, rows, g):
    return jnp.concatenate([ref[rows, 256 * g:256 * g + 128], ref[rows, 256 * g + 128:256 * g + 256]], axis=0)


def _swa_specs():
    W2 = 2 * WINDOW
    q = pl.BlockSpec((W2, BRANCH), lambda i: (i, CB_SQ))
    kc = pl.BlockSpec((W2, 128), lambda i: (i, CB_SK))
    kp = pl.BlockSpec((WINDOW, 128), lambda i: (jnp.maximum(2 * i - 1, 0), CB_SK))
    vc = pl.BlockSpec((W2, 128), lambda i: (i, CB_SV))
    vp = pl.BlockSpec((WINDOW, 128), lambda i: (jnp.maximum(2 * i - 1, 0), CB_SV))
    bias = pl.BlockSpec((4, W2, W2), lambda i: (0, 0, 0))
    vec = pl.BlockSpec((1, 128), lambda i: (0, 0))
    return q, kc, kp, vc, vp, bias, vec


def _swa_fwd(pm, bias, sink, name):
    T = pm.shape[0]
    nb = T // (2 * WINDOW)

    def body(q_ref, kc_ref, kp_ref, vc_ref, vp_ref, b_ref, s_ref, o_ref, m_ref):
        i = pl.program_id(0)
        lo = _lane_lo((WINDOW, 128))
        sink_v = s_ref[...]
        for u in range(2):
            rows = slice(WINDOW * u, WINDOW * (u + 1))
            first = (i == 0) if u == 0 else None
            kcur, vcur = kc_ref[rows, :], vc_ref[rows, :]
            kprev = kp_ref[...] if u == 0 else kc_ref[0:WINDOW, :]
            vprev = vp_ref[...] if u == 0 else vc_ref[0:WINDOW, :]
            kcA, kcB, _ = _kv_variants(kcur)
            kpA, kpB, _ = _kv_variants(kprev)
            _, _, vcD = _kv_variants(vcur)
            _, _, vpD = _kv_variants(vprev)
            mout = jnp.zeros((WINDOW, 128), F32)
            for g in range(2):
                qg = _swa_stack(q_ref, rows, g) * ATT_SCALE
                vband = jnp.concatenate([vpD[g], vcD[g]], axis=0)
                outs = []
                for par in range(2):
                    s = 2 * g + par
                    kband = jnp.concatenate([(kpA, kpB)[par][g], (kcA, kcB)[par][g]], axis=0)
                    sc = _swa_scores(qg, kband, b_ref[s], first)
                    sk = _swa_cols(sink_v, s)
                    m = jnp.maximum(jnp.max(sc, axis=-1, keepdims=True), sk)
                    e = jnp.exp(sc - m)
                    den = jnp.sum(e, axis=-1, keepdims=True) + jnp.exp(sk - m)
                    outs.append(_dot_nn((e * (1.0 / den)).astype(BF16), vband))
                    lse = m + jnp.log(den)
                    mout = mout + _put_col((WINDOW, 128), _stacked_head(s, 0), lse[:WINDOW])
                    mout = mout + _put_col((WINDOW, 128), _stacked_head(s, 1), lse[WINDOW:])
                for r in range(2):
                    sl = slice(256 * g + 128 * r, 256 * g + 128 * r + 128)
                    o_ref[rows, sl] = jnp.where(lo, outs[0][WINDOW * r:WINDOW * (r + 1)],
                                                outs[1][WINDOW * r:WINDOW * (r + 1)]).astype(BF16)
            m_ref[rows, :] = mout

    q, kc, kp, vc, vp, bs, vec = _swa_specs()
    return pl.pallas_call(
        body, name=name, grid=(nb,),
        in_specs=[q, kc, kp, vc, vp, bs, vec],
        out_specs=[pl.BlockSpec((2 * WINDOW, BRANCH), lambda i: (i, 0)),
                   pl.BlockSpec((2 * WINDOW, 128), lambda i: (i, 0))],
        out_shape=[jax.ShapeDtypeStruct((T, BRANCH), BF16), jax.ShapeDtypeStruct((T, 128), F32)],
        compiler_params=_cp(("parallel",)),
    )(pm, pm, pm, pm, pm, bias, sink)


def _swa_bwd(pm, bias, sink, do, mlse, dproj, name):
    T = pm.shape[0]
    nb = T // (2 * WINDOW)

    def fold(zz):
        return zz + pltpu.roll(zz, HEAD_DIM, 1)

    def body(q_ref, kc_ref, kp_ref, vc_ref, vp_ref, b_ref, s_ref, do_ref, m_ref, buf_ref,
             dq_ref, dkc_ref, dkp_ref, dvc_ref, dvp_ref, db_ref, ds_ref):
        del buf_ref
        i = pl.program_id(0)

        @pl.when(i == 0)
        def _():
            db_ref[...] = jnp.zeros_like(db_ref)
            ds_ref[...] = jnp.zeros_like(ds_ref)

        lo = _lane_lo((WINDOW, 128))
        lo2 = _lane_lo((2 * WINDOW, 128))
        sink_v = s_ref[...]
        dsink = jnp.zeros((1, 128), F32)
        for u in range(2):
            rows = slice(WINDOW * u, WINDOW * (u + 1))
            first = (i == 0) if u == 0 else None
            kcur, vcur = kc_ref[rows, :], vc_ref[rows, :]
            kprev = kp_ref[...] if u == 0 else kc_ref[0:WINDOW, :]
            vprev = vp_ref[...] if u == 0 else vc_ref[0:WINDOW, :]
            kcA, kcB, kcD = _kv_variants(kcur)
            kpA, kpB, kpD = _kv_variants(kprev)
            vcA, vcB, _ = _kv_variants(vcur)
            vpA, vpB, _ = _kv_variants(vprev)
            mv = m_ref[rows, :]
            zks, zvs = [], []
            for g in range(2):
                qraw = _swa_stack(q_ref, rows, g)
                qg = qraw * ATT_SCALE
                dog = _swa_stack(do_ref, rows, g)
                kband_d = jnp.concatenate([kpD[g], kcD[g]], axis=0)
                dqs, mks, mvs = [], [], []
                for par in range(2):
                    s = 2 * g + par
                    kband = jnp.concatenate([(kpA, kpB)[par][g], (kcA, kcB)[par][g]], axis=0)
                    vband = jnp.concatenate([(vpA, vpB)[par][g], (vcA, vcB)[par][g]], axis=0)
                    sc = _swa_scores(qg, kband, b_ref[s], first)
                    h0, h1 = _stacked_head(s, 0), _stacked_head(s, 1)
                    m_c = jnp.concatenate([mv[:, h0:h0 + 1], mv[:, h1:h1 + 1]], axis=0)
                    pr = jnp.exp(sc - m_c)
                    psink = jnp.exp(_swa_cols(sink_v, s) - m_c)
                    dp = _dot_nt(dog, vband)
                    delta = jnp.sum(pr * dp, axis=-1, keepdims=True)
                    dsc = pr * (dp - delta)
                    db_ref[s] += dsc
                    sd = psink * delta
                    dsink = dsink - _put_col((1, 128), h0, jnp.sum(sd[:WINDOW], keepdims=True))
                    dsink = dsink - _put_col((1, 128), h1, jnp.sum(sd[WINDOW:], keepdims=True))
                    dsb = dsc.astype(BF16)
                    dqs.append(_dot_nn(dsb, kband_d))
                    mks.append(_dot_tn(dsb, qraw))
                    mvs.append(_dot_tn(pr.astype(BF16), dog))
                for r in range(2):
                    sl = slice(256 * g + 128 * r, 256 * g + 128 * r + 128)
                    dq_ref[rows, sl] = (jnp.where(lo, dqs[0][WINDOW * r:WINDOW * (r + 1)],
                                                  dqs[1][WINDOW * r:WINDOW * (r + 1)]) * ATT_SCALE).astype(BF16)
                zks.append(fold(jnp.where(lo2, mks[0], mks[1])))
                zvs.append(fold(jnp.where(lo2, mvs[0], mvs[1])))
            dk = jnp.where(lo2, zks[0], zks[1]) * ATT_SCALE
            dv = jnp.where(lo2, zvs[0], zvs[1])
            dkp_ref[rows, :] = dk[:WINDOW]
            dkc_ref[rows, :] = dk[WINDOW:]
            dvp_ref[rows, :] = dv[:WINDOW]
            dvc_ref[rows, :] = dv[WINDOW:]
        ds_ref[...] += dsink

    q, kc, kp, vc, vp, bs, vec = _swa_specs()
    own = pl.BlockSpec((2 * WINDOW, BRANCH), lambda i: (i, 0))
    sm = pl.BlockSpec((2 * WINDOW, 128), lambda i: (i, 0))
    f128 = jax.ShapeDtypeStruct((T, 128), F32)
    return pl.pallas_call(
        body, name=name, grid=(nb,),
        in_specs=[q, kc, kp, vc, vp, bs, vec, own, sm, pl.BlockSpec(memory_space=pl.ANY)],
        out_specs=[pl.BlockSpec((2 * WINDOW, BRANCH), lambda i: (i, CB_SQ)), sm, sm, sm, sm, bs, vec],
        out_shape=[jax.ShapeDtypeStruct(dproj.shape, dproj.dtype), f128, f128, f128, f128,
                   jax.ShapeDtypeStruct((4, 2 * WINDOW, 2 * WINDOW), F32), jax.ShapeDtypeStruct((1, 128), F32)],
        input_output_aliases={9: 0},
        compiler_params=_cp(("arbitrary",)),
    )(pm, pm, pm, pm, pm, bias, sink, do, mlse, dproj)


def _merge_fwd(pm, us, name):
    T = pm.shape[0]
    bt = _pick(T, (512, 256))

    def body(g0, g1, g2, u0, u1, u2, o_ref):
        acc = jax.nn.sigmoid(g0[...].astype(F32)) * u0[...].astype(F32)
        acc = acc + jax.nn.sigmoid(g1[...].astype(F32)) * u1[...].astype(F32)
        acc = acc + jax.nn.sigmoid(g2[...].astype(F32)) * u2[...].astype(F32)
        o_ref[...] = acc.astype(BF16)

    own = pl.BlockSpec((bt, D_MODEL), lambda i: (i, 0))
    gs = [pl.BlockSpec((bt, D_MODEL), lambda i, cb=cb: (i, cb)) for cb in CB_GATE]
    return pl.pallas_call(
        body, name=name, grid=(T // bt,), in_specs=gs + [own, own, own], out_specs=own,
        out_shape=jax.ShapeDtypeStruct((T, D_MODEL), BF16),
        compiler_params=_cp(("parallel",)),
    )(pm, pm, pm, *us)


def _merge_bwd(pm, us, dm, name):
    T = pm.shape[0]
    bt = _pick(T, (512, 256))

    def body(g0, g1, g2, u0, u1, u2, dm_ref, du0, du1, du2, dg_ref):
        dmv = dm_ref[...].astype(F32)
        for b, (g, u, du) in enumerate(((g0, u0, du0), (g1, u1, du1), (g2, u2, du2))):
            s = jax.nn.sigmoid(g[...].astype(F32))
            du[...] = (dmv * s).astype(BF16)
            dg_ref[:, D_MODEL * b:D_MODEL * (b + 1)] = (dmv * u[...].astype(F32) * s * (1.0 - s)).astype(BF16)

    own = pl.BlockSpec((bt, D_MODEL), lambda i: (i, 0))
    gs = [pl.BlockSpec((bt, D_MODEL), lambda i, cb=cb: (i, cb)) for cb in CB_GATE]
    act = jax.ShapeDtypeStruct((T, D_MODEL), BF16)
    return pl.pallas_call(
        body, name=name, grid=(T // bt,), in_specs=gs + [own, own, own, own],
        out_specs=[own, own, own, pl.BlockSpec((bt, 3 * D_MODEL), lambda i: (i, 0))],
        out_shape=[act, act, act, jax.ShapeDtypeStruct((T, PROJ_PAD), BF16)],
        compiler_params=_cp(("parallel",)),
    )(pm, pm, pm, *us, dm)


def _swiglu_fwd(ab, name):
    T = ab.shape[0]
    bt = _pick(T, (512, 256))

    def body(a_ref, b_ref, o_ref):
        a = a_ref[...].astype(F32)
        o_ref[...] = (a * jax.nn.sigmoid(a) * b_ref[...].astype(F32)).astype(BF16)

    return pl.pallas_call(
        body, name=name, grid=(T // bt,),
        in_specs=[pl.BlockSpec((bt, D_FF), lambda i: (i, 0)), pl.BlockSpec((bt, D_FF), lambda i: (i, 1))],
        out_specs=pl.BlockSpec((bt, D_FF), lambda i: (i, 0)),
        out_shape=jax.ShapeDtypeStruct((T, D_FF), BF16),
        compiler_params=_cp(("parallel",)),
    )(ab, ab)


def _swiglu_bwd(ab, dh, name):
    T = ab.shape[0]
    bt = _pick(T, (512, 256))

    def body(a_ref, b_ref, d_ref, o_ref):
        a = a_ref[...].astype(F32)
        b = b_ref[...].astype(F32)
        d = d_ref[...].astype(F32)
        s = jax.nn.sigmoid(a)
        o_ref[:, 0:D_FF] = (d * b * (s + a * s * (1.0 - s))).astype(BF16)
        o_ref[:, D_FF:2 * D_FF] = (d * a * s).astype(BF16)

    return pl.pallas_call(
        body, name=name, grid=(T // bt,),
        in_specs=[pl.BlockSpec((bt, D_FF), lambda i: (i, 0)), pl.BlockSpec((bt, D_FF), lambda i: (i, 1)),
                  pl.BlockSpec((bt, D_FF), lambda i: (i, 0))],
        out_specs=pl.BlockSpec((bt, 2 * D_FF), lambda i: (i, 0)),
        out_shape=jax.ShapeDtypeStruct((T, 2 * D_FF), BF16),
        compiler_params=_cp(("parallel",)),
    )(ab, ab, dh)


def _xattn_probs(q_ref, kv_ref, h):
    sl = slice(X_HEAD_DIM * h, X_HEAD_DIM * (h + 1))
    qh = q_ref[:, sl]
    kh = kv_ref[:, sl]
    vh = kv_ref[:, D_MODEL + X_HEAD_DIM * h:D_MODEL + X_HEAD_DIM * (h + 1)]
    s = _dot_nt(qh, kh) * X_SCALE
    e = jnp.exp(s - jnp.max(s, axis=-1, keepdims=True))
    return qh, kh, vh, e * (1.0 / jnp.sum(e, axis=-1, keepdims=True))


def _xattn_fwd(q, kv, name):
    T = q.shape[0]
    bq = _pick(T, (512, 256))

    def body(q_ref, kv_ref, o_ref):
        for h in range(X_HEADS):
            _, _, vh, p = _xattn_probs(q_ref, kv_ref, h)
            o_ref[:, X_HEAD_DIM * h:X_HEAD_DIM * (h + 1)] = _dot_nn(p.astype(BF16), vh).astype(BF16)

    own = pl.BlockSpec((bq, D_MODEL), lambda i: (i, 0))
    return pl.pallas_call(
        body, name=name, grid=(T // bq,),
        in_specs=[own, pl.BlockSpec((MEM_LEN, 2 * D_MODEL), lambda i: (0, 0))], out_specs=own,
        out_shape=jax.ShapeDtypeStruct((T, D_MODEL), BF16),
        compiler_params=_cp(("parallel",)),
    )(q, kv)


def _xattn_bwd(q, kv, do, name):
    T = q.shape[0]
    bq = _pick(T, (512, 256))

    def body(q_ref, kv_ref, do_ref, dq_ref, dkv_ref):
        @pl.when(pl.program_id(0) == 0)
        def _():
            dkv_ref[...] = jnp.zeros_like(dkv_ref)

        for h in range(X_HEADS):
            sl = slice(X_HEAD_DIM * h, X_HEAD_DIM * (h + 1))
            qh, kh, vh, p = _xattn_probs(q_ref, kv_ref, h)
            doh = do_ref[:, sl]
            dp = _dot_nt(doh, vh)
            ds = (p * (dp - jnp.sum(p * dp, axis=-1, keepdims=True)) * X_SCALE).astype(BF16)
            dq_ref[:, sl] = _dot_nn(ds, kh).astype(BF16)
            dkv_ref[:, sl] += _dot_tn(ds, qh)
            dkv_ref[:, D_MODEL + X_HEAD_DIM * h:D_MODEL + X_HEAD_DIM * (h + 1)] += _dot_tn(p.astype(BF16), doh)

    own = pl.BlockSpec((bq, D_MODEL), lambda i: (i, 0))
    kvs = pl.BlockSpec((MEM_LEN, 2 * D_MODEL), lambda i: (0, 0))
    return pl.pallas_call(
        body, name=name, grid=(T // bq,), in_specs=[own, kvs, own], out_specs=[own, kvs],
        out_shape=[jax.ShapeDtypeStruct((T, D_MODEL), BF16), jax.ShapeDtypeStruct((MEM_LEN, 2 * D_MODEL), F32)],
        compiler_params=_cp(("arbitrary",)),
    )(q, kv, do)


def _adamw(w, g, m, v, name):
    R, C = w.shape[0], w.shape[-1]
    rest = w.shape[1:]
    row_bytes = int(np.prod(rest[:-1], dtype=np.int64)) * (-(-C // 128) * 128) * 4
    cands = (1024, 512, 256, 128, 64, 32, 16, 8) if w.ndim == 2 else range(R, 0, -1)
    bt = R
    for cand in cands:
        if R % cand == 0 and cand * row_bytes <= (3 << 19):
            bt = cand
            break
    zeros = (0,) * len(rest)

    def body(w_ref, g_ref, m_ref, v_ref, d_ref, nm_ref, nv_ref):
        gv = g_ref[...]
        mn = ADAM_B1 * m_ref[...] + (1.0 - ADAM_B1) * gv
        vn = ADAM_B2 * v_ref[...] + (1.0 - ADAM_B2) * (gv * gv)
        m_hat = mn / (1.0 - ADAM_B1 ** ADAM_STEP)
        v_hat = vn / (1.0 - ADAM_B2 ** ADAM_STEP)
        d_ref[...] = -ADAM_LR * (m_hat / (jnp.sqrt(v_hat) + ADAM_EPS) + ADAM_WD * w_ref[...])
        nm_ref[...] = mn
        nv_ref[...] = vn

    blk = pl.BlockSpec((bt,) + tuple(rest), lambda i: (i,) + zeros)
    out = jax.ShapeDtypeStruct(w.shape, F32)
    return pl.pallas_call(
        body, name=name, grid=(R // bt,), in_specs=[blk] * 4, out_specs=[blk] * 3,
        out_shape=[out, out, out], compiler_params=_cp(("parallel",)),
    )(w, g, m, v)


ANY = pl.BlockSpec(memory_space=pl.ANY)

BIG = (
    ("w_in", (2048, 1730)), ("w_branch", (3072, 256)), ("w_mix_out", (512, 1024)), ("w_xq", (512, 1024)),
    ("w_xkv", (2048, 512)), ("w_xo", (512, 1024)), ("w_ffn_gate", (1408, 1024)), ("w_ffn_up", (1408, 1024)),
    ("w_ffn_down", (1408, 1024)),
)
TRANSPOSED = ("w_ffn_gate", "w_ffn_up")
ROW_BLOCKS = (512, 256, 352, 128, 16)


def _neighbours():
    x, y, c = lax.axis_index("x"), lax.axis_index("y"), lax.axis_index("c")
    idx = (2 * x + y, 2 * (1 - x) + y, 2 * x + (1 - y), 2 * (1 - x) + (1 - y))
    return idx, (x, y, c), (1 - x, y, c), (x, 1 - y, c), (x, y, 1 - c)


def _remote(src, dst, sems, k, to):
    send_sems, recv_sems = sems
    return pltpu.make_async_remote_copy(src_ref=src, dst_ref=dst, send_sem=send_sems.at[k], recv_sem=recv_sems.at[k],
                                        device_id=to, device_id_type=MESH)


def _cast_place(w, me_idx, name):
    R, Wd = w.shape
    bt = _pick(R, ROW_BLOCKS)

    def body(i_ref, w_ref, o_ref):
        o_ref[0] = w_ref[...].astype(BF16)

    grid_spec = pltpu.PrefetchScalarGridSpec(
        num_scalar_prefetch=1, grid=(R // bt,),
        in_specs=[pl.BlockSpec((bt, Wd), lambda i, idx: (i, 0))],
        out_specs=pl.BlockSpec((1, bt, Wd), lambda i, idx: (idx[0], i, 0)))
    return pl.pallas_call(
        body, name=name, grid_spec=grid_spec, out_shape=jax.ShapeDtypeStruct((4, R, Wd), BF16),
        compiler_params=_cp(("parallel",)),
    )(me_idx, w)


def _ag_ring_multi(bufs):
    n = len(bufs)

    def body(*refs):
        o = refs[n:2 * n]
        sems = refs[2 * n:]
        (me, ix, iy, idg), here, xn, yn, sib = _neighbours()
        c = here[2]

        def piece(t, k, other):
            h = bufs[t].shape[1] // 2
            q = h // 2
            base = ((1 - c) if other else c) * h
            return [(ix, pl.ds(base, h)), (iy, pl.ds(base, h)), (idg, pl.ds(base, q)), (idg, pl.ds(base + q, q))][k]

        def copy(t, k, slab, rows, to):
            ref = o[t].at[slab, rows]
            return _remote(ref, ref, sems, 8 * t + k, to)

        sends = []

        def go(cp):
            cp.start()
            sends.append(cp)

        for t in range(n):
            h = bufs[t].shape[1] // 2
            go(copy(t, 0, me, pl.ds(c * h, h), xn))
            go(copy(t, 1, me, pl.ds(c * h, h), yn))
        for k in range(4):
            for t in range(n):
                slab, rows = piece(t, k, False)
                copy(t, k, slab, rows, here).wait_recv()
                if k == 0:
                    go(copy(t, 2, ix, piece(t, 2, False)[1], yn))
                if k == 1:
                    go(copy(t, 3, iy, piece(t, 3, False)[1], xn))
                go(copy(t, 4 + k, slab, rows, sib))
        for k in range(4):
            for t in range(n):
                slab, rows = piece(t, k, True)
                copy(t, 4 + k, slab, rows, here).wait_recv()
        for cp in sends:
            cp.wait_send()

    return pl.pallas_call(
        body, name="ag_weights", in_specs=[ANY] * n, out_specs=[ANY] * n,
        input_output_aliases={t: t for t in range(n)},
        out_shape=[jax.ShapeDtypeStruct(b.shape, b.dtype) for b in bufs],
        scratch_shapes=[pltpu.SemaphoreType.DMA((8 * n,)), pltpu.SemaphoreType.DMA((8 * n,))],
    )(*bufs)


def _exchange_multi(srcs, out_shapes, plan, name, aliased=False):
    n = len(srcs)

    def body(*refs):
        ins, outs, sems = refs[:n], refs[n:2 * n], refs[2 * n:]
        places = _neighbours()
        here = places[1]
        per = [plan(t, ins[t], outs[t], places) for t in range(n)]
        width = max(len(p) for p in per)
        started = []
        for t in range(n):
            for k, (src, dst, to, land) in enumerate(per[t]):
                cp = _remote(src, dst, sems, width * t + k, to)
                cp.start()
                started.append(cp)
        for t in range(n):
            for k, (src, dst, to, land) in enumerate(per[t]):
                _remote(land, land, sems, width * t + k, here).wait_recv()
        for cp in started:
            cp.wait_send()

    nsem = 2 * n
    return pl.pallas_call(
        body, name=name, in_specs=[ANY] * n, out_specs=[ANY] * n,
        input_output_aliases={t: t for t in range(n)} if aliased else {},
        out_shape=[jax.ShapeDtypeStruct(s, d) for s, d in out_shapes],
        scratch_shapes=[pltpu.SemaphoreType.DMA((nsem,)), pltpu.SemaphoreType.DMA((nsem,))],
    )(*srcs)


def _rs_sibling_multi(gs):
    def plan(t, g, o, places):
        (_, here, _, _, sib) = places
        h = gs[t].shape[1] // 2
        return [(g.at[:, pl.ds((1 - here[2]) * h, h)], o, sib, o)]

    return _exchange_multi(gs, [((4, g.shape[1] // 2, g.shape[2]), g.dtype) for g in gs], plan, "rs_sibling")


def _rs_add_pair(g4, sib, cidx, tag=""):
    _, R, Wd = g4.shape
    hrows = R // 2
    bt = _pick(hrows, ROW_BLOCKS)
    nb = hrows // bt

    def body(c_ref, a_ref, b_ref, o_ref):
        o_ref[...] = (a_ref[...].astype(F32) + b_ref[...].astype(F32)).astype(o_ref.dtype)

    grid_spec = pltpu.PrefetchScalarGridSpec(
        num_scalar_prefetch=1, grid=(4, nb),
        in_specs=[pl.BlockSpec((1, bt, Wd), lambda j, i, c: (j, c[0] * nb + i, 0)),
                  pl.BlockSpec((1, bt, Wd), lambda j, i, c: (j, i, 0))],
        out_specs=pl.BlockSpec((1, bt, Wd), lambda j, i, c: (j, i, 0)))
    return pl.pallas_call(
        body, name="rs_add_pair" + tag, grid_spec=grid_spec,
        out_shape=jax.ShapeDtypeStruct((4, hrows, Wd), g4.dtype),
        compiler_params=_cp(("parallel", "parallel")),
    )(cidx, g4, sib)


def _rs_diag_multi(rs):
    def plan(t, r, o, places):
        ((_, _, _, idg), _, xn, yn, _) = places
        q = rs[t].shape[1] // 2
        return [(r.at[idg, pl.ds(0, q)], o.at[0], xn, o.at[0]), (r.at[idg, pl.ds(q, q)], o.at[1], yn, o.at[1])]

    return _exchange_multi(rs, [((2, r.shape[1] // 2, r.shape[2]), r.dtype) for r in rs], plan, "rs_diag")


def _rs_merge(r4, dg, nbr_idx, tag=""):
    _, hrows, Wd = r4.shape
    bt = _pick(hrows // 2, ROW_BLOCKS)
    nb = hrows // bt
    nq = nb // 2

    def body(i_ref, r_ref, d_ref, o_ref):
        w = pl.program_id(0)
        i = pl.program_id(1)
        merged = jnp.where(w == 0, i >= nq, i < nq)
        add = jnp.where(merged, d_ref[...].astype(F32), 0.0)
        o_ref[...] = (r_ref[...].astype(F32) + add).astype(o_ref.dtype)

    grid_spec = pltpu.PrefetchScalarGridSpec(
        num_scalar_prefetch=1, grid=(2, nb),
        in_specs=[pl.BlockSpec((1, bt, Wd), lambda w, i, idx: (idx[w], i, 0)),
                  pl.BlockSpec((1, bt, Wd), lambda w, i, idx: (1 - w, jnp.clip(i - (1 - w) * nq, 0, nq - 1), 0))],
        out_specs=pl.BlockSpec((1, bt, Wd), lambda w, i, idx: (w, i, 0)))
    return pl.pallas_call(
        body, name="rs_merge" + tag, grid_spec=grid_spec,
        out_shape=jax.ShapeDtypeStruct((2, hrows, Wd), r4.dtype),
        compiler_params=_cp(("parallel", "parallel")),
    )(nbr_idx, r4, dg)


def _rs_direct_multi(ms):
    def plan(t, m, o, places):
        (_, _, xn, yn, _) = places
        return [(m.at[0], o.at[0], xn, o.at[0]), (m.at[1], o.at[1], yn, o.at[1])]

    return _exchange_multi(ms, [(m.shape, m.dtype) for m in ms], plan, "rs_direct")


def _rs_final(r4, got, me_c, tag=""):
    _, hrows, Wd = r4.shape
    bt = _pick(hrows, ROW_BLOCKS)
    nb = hrows // bt

    def body(i_ref, r_ref, g_ref, o_ref):
        o_ref[...] = (r_ref[0].astype(F32) + g_ref[0].astype(F32)) + g_ref[1].astype(F32)

    grid_spec = pltpu.PrefetchScalarGridSpec(
        num_scalar_prefetch=1, grid=(nb,),
        in_specs=[pl.BlockSpec((1, bt, Wd), lambda i, idx: (idx[0], i, 0)),
                  pl.BlockSpec((2, bt, Wd), lambda i, idx: (0, i, 0))],
        out_specs=pl.BlockSpec((bt, Wd), lambda i, idx: (idx[1] * nb + i, 0)))
    return pl.pallas_call(
        body, name="rs_final" + tag, grid_spec=grid_spec,
        out_shape=jax.ShapeDtypeStruct((2 * hrows, Wd), F32),
        compiler_params=_cp(("parallel",)),
    )(me_c, r4, got)


def _rs_share_multi(bufs):
    def plan(t, b, o, places):
        (_, here, _, _, sib) = places
        h = bufs[t].shape[0] // 2
        mine = o.at[pl.ds(here[2] * h, h)]
        return [(mine, mine, sib, o.at[pl.ds((1 - here[2]) * h, h)])]

    return _exchange_multi(bufs, [(b.shape, b.dtype) for b in bufs], plan, "rs_share", aliased=True)


def _allreduce_small(v, name="allreduce_small"):
    R, Wd = v.shape

    def body(v_ref, o_ref, buf, send_sems, recv_sems):
        x, y, c = lax.axis_index("x"), lax.axis_index("y"), lax.axis_index("c")
        me = 4 * x + 2 * y + c
        buf[me] = v_ref[...]
        sends = []
        for k in range(1, 8):
            peer = ((x + (k >> 2)) % 2, (y + ((k >> 1) & 1)) % 2, (c + (k & 1)) % 2)
            sends.append(pltpu.make_async_remote_copy(
                src_ref=v_ref, dst_ref=buf.at[me], send_sem=send_sems.at[k - 1], recv_sem=recv_sems.at[k - 1],
                device_id=peer, device_id_type=MESH))
        for cp in sends:
            cp.start()
        for k in range(1, 8):
            px, py, pc = (x + (k >> 2)) % 2, (y + ((k >> 1) & 1)) % 2, (c + (k & 1)) % 2
            pltpu.make_async_remote_copy(
                src_ref=v_ref, dst_ref=buf.at[4 * px + 2 * py + pc], send_sem=send_sems.at[k - 1],
                recv_sem=recv_sems.at[k - 1], device_id=(x, y, c), device_id_type=MESH).wait_recv()
        acc = buf[0]
        for d in range(1, 8):
            acc = acc + buf[d]
        o_ref[...] = acc
        for cp in sends:
            cp.wait_send()

    vm = pl.BlockSpec(memory_space=pltpu.VMEM)
    return pl.pallas_call(
        body, name=name, in_specs=[vm], out_specs=vm,
        out_shape=jax.ShapeDtypeStruct((R, Wd), F32),
        scratch_shapes=[pltpu.VMEM((8, R, Wd), F32), pltpu.SemaphoreType.DMA((7,)), pltpu.SemaphoreType.DMA((7,))],
    )(v)


SMALL = (
    ("mix_norm_g", (2, 1024)), ("xattn_norm_g", (2, 1024)), ("mem_norm_g", (2, 1024)),
    ("ffn_norm_g", (2, 1024)), ("final_norm_g", (1024,)),
    ("forget_bias", (2, 8)), ("sink", (2, 8)), ("rel_bias", (32, 8)),
)
SMALL_AND_CONV = SMALL + (("conv_w", (2, 3, 512)),)


def _small_rows(spec):
    rows = sum(int(np.prod(s)) // 128 if s[-1] % 128 == 0 else s[0] for _, s in spec)
    return -(-rows // 8) * 8


def _pack_small(vals, spec=SMALL):
    rows = []
    for name, shape in spec:
        v = vals[name].astype(F32)
        if shape[-1] % 128 == 0:
            rows.append(v.reshape(-1, 128))
        else:
            rows.append(jnp.pad(v, ((0, 0), (0, 120))))
    rows = jnp.concatenate(rows, axis=0)
    return jnp.pad(rows, ((0, _small_rows(spec) - rows.shape[0]), (0, 0)))


def _unpack_small(pack, spec=SMALL):
    out, off = {}, 0
    for name, shape in spec:
        if shape[-1] % 128 == 0:
            n = int(np.prod(shape)) // 128
            out[name] = pack[off:off + n].reshape(shape)
        else:
            n = shape[0]
            out[name] = pack[off:off + n, 0:8]
        off += n
    return out


W_IN_PERM = ((3848, 6920), (0, 3072), (3080, 3848), (3072, 3080))


def _perm_w_in(w):
    parts = [w[:, a:b] for a, b in W_IN_PERM]
    return jnp.concatenate(parts + [jnp.zeros((w.shape[0], PROJ_PAD - IN_COLS), w.dtype)], axis=1)


def _unperm_w_in(p):
    return jnp.concatenate([p[:, 3072:6144], p[:, 6912:6920], p[:, 6144:6912], p[:, 0:3072]], axis=1)


def _pad_row8(v):
    return jnp.pad(v.astype(F32).reshape(1, 8), ((0, 0), (0, 120)))


def _local_step(x, mem, tgt, W, rel_bias):
    bucket = jnp.asarray(_bucket_table())
    bias = _swa_bias(rel_bias, bucket, "swa_bias")
    saved = []
    for l in range(DEPTH):
        n = "l%d_" % l
        s = {"x0": x}
        wcat = W["w_in_p"][l]
        h = _rms_fwd(x, W["mix_norm_g"][l:l + 1], n + "mix_norm")
        pm = _mm(h, wcat[:, :PROJ_MAIN], "nn", BF16, n + "proj", bn=768)
        fg = _mm(h, wcat[:, PROJ_MAIN:], "nn", F32, n + "proj_fg")
        fb = _pad_row8(W["forget_bias"][l])
        c_col = _fox_gate_fwd(fg, fb, n + "fox_gate")
        c_row = c_col[:, 0:8].T
        cw = jnp.pad(W["conv_w"][l], ((0, 5), (0, 0)))
        y_conv = _conv_fwd(pm, cw, n + "conv")
        y_fox, lse = _fox2_fwd(pm, c_row, n + "fox")
        sink = _pad_row8(W["sink"][l])
        y_swa, mlse = _swa_fwd(pm, bias, sink, n + "swa")
        ys = (y_conv, y_fox, y_swa)
        us = tuple(_mm(ys[b], W["w_branch"][l][b], "nn", BF16, n + "branch%d" % b) for b in range(3))
        merged = _merge_fwd(pm, us, n + "merge")
        x1 = _mm(merged, W["w_mix_out"][l], "nn", F32, n + "mix_out", res=x)
        xn1 = _rms_fwd(x1, W["xattn_norm_g"][l:l + 1], n + "xattn_norm")
        memn = _rms_fwd(mem, W["mem_norm_g"][l:l + 1], n + "mem_norm")
        qx = _mm(xn1, W["w_xq"][l], "nn", BF16, n + "xq")
        kv = _mm(memn, W["w_xkv"][l], "nn", BF16, n + "xkv")
        ox = _xattn_fwd(qx, kv, n + "xattn")
        x2 = _mm(ox, W["w_xo"][l], "nn", F32, n + "xo", res=x1)
        xn2 = _rms_fwd(x2, W["ffn_norm_g"][l:l + 1], n + "ffn_norm")
        ab = _mm(xn2, W["w_gu"][l], "nt", BF16, n + "ffn_in", bn=512)
        hm = _swiglu_fwd(ab, n + "swiglu")
        x3 = _mm(hm, W["w_ffn_down"][l], "nn", F32, n + "ffn_out", res=x2, bk=1408)
        s.update(h=h, pm=pm, fg=fg, fb=fb, c_col=c_col, c_row=c_row, cw=cw, ys=ys, lse=lse, sink=sink,
                 mlse=mlse, us=us, merged=merged, x1=x1, xn1=xn1, memn=memn, qx=qx, kv=kv, ox=ox,
                 x2=x2, xn2=xn2, ab=ab, hm=hm)
        saved.append(s)
        x = x3

    loss_row, dx, dg_final = _final_loss(x, W["final_norm_g"].reshape(1, D_MODEL), tgt, "final_loss")
    G = {name: [None] * DEPTH for name in
         ("mix_norm_g", "w_in_p", "forget_bias", "conv_w", "sink", "w_branch", "w_mix_out", "xattn_norm_g",
          "mem_norm_g", "w_xq", "w_xkv", "w_xo", "ffn_norm_g", "w_gu", "w_ffn_down")}
    dbias_tot = None
    for l in reversed(range(DEPTH)):
        n = "l%d_" % l
        s = saved[l]
        dhm = _mm(dx, W["w_ffn_down"][l], "nt", BF16, n + "d_hm", bn=1408)
        G["w_ffn_down"][l] = _mm(s["hm"], dx, "tn", BF16, n + "dw_down", bm=1408, bk=1024)
        dab = _swiglu_bwd(s["ab"], dhm, n + "d_swiglu")
        dxn2 = _mm(dab, W["w_gu"][l], "nn", BF16, n + "d_xn2", bk=1408)
        G["w_gu"][l] = _mm(dab, s["xn2"], "tn", BF16, n + "dw_gu", bm=512, bk=2048)
        dx, G["ffn_norm_g"][l] = _rms_bwd(s["x2"], W["ffn_norm_g"][l:l + 1], dxn2, dx, n + "d_ffn_norm")
        dox = _mm(dx, W["w_xo"][l], "nt", BF16, n + "d_ox")
        G["w_xo"][l] = _mm(s["ox"], dx, "tn", BF16, n + "dw_xo", bk=1024)
        dqx, dkv = _xattn_bwd(s["qx"], s["kv"], dox, n + "d_xattn")
        dxn1 = _mm(dqx, W["w_xq"][l], "nt", BF16, n + "d_xn1")
        G["w_xq"][l] = _mm(s["xn1"], dqx, "tn", BF16, n + "dw_xq", bk=2048)
        dmemn = _mm(dkv, W["w_xkv"][l], "nt", BF16, n + "d_memn")
        G["w_xkv"][l] = _mm(s["memn"], dkv, "tn", BF16, n + "dw_xkv")
        _, G["mem_norm_g"][l] = _rms_bwd(mem, W["mem_norm_g"][l:l + 1], dmemn, None, n + "d_mem_norm")
        dx, G["xattn_norm_g"][l] = _rms_bwd(s["x1"], W["xattn_norm_g"][l:l + 1], dxn1, dx, n + "d_xattn_norm")
        dmerged = _mm(dx, W["w_mix_out"][l], "nt", BF16, n + "d_merged")
        G["w_mix_out"][l] = _mm(s["merged"], dx, "tn", BF16, n + "dw_mix_out", bk=1024)
        du0, du1, du2, dproj = _merge_bwd(s["pm"], s["us"], dmerged, n + "d_merge")
        dus = (du0, du1, du2)
        dys = [_mm(dus[b], W["w_branch"][l][b], "nt", BF16, n + "d_y%d" % b) for b in range(3)]
        G["w_branch"][l] = [_mm(s["ys"][b], dus[b], "tn", BF16, n + "dw_branch%d" % b, bk=2048) for b in range(3)]
        dproj, dcw = _conv_bwd(s["pm"], s["cw"], dys[0], dproj, n + "d_conv")
        G["conv_w"][l] = dcw[0:3]
        delta = _fox_delta(s["ys"][1], dys[1], n + "fox_delta")
        dproj, delta = _fox2_bwd_dq(s["pm"], dys[1], s["c_row"], s["lse"], delta, dproj, n + "d_fox_q")
        dproj, dc = _fox2_bwd_dkv(s["pm"], dys[1], s["c_col"], s["lse"][:, 0:8].T, delta[:, 0:8].T, dproj,
                                  n + "d_fox_kv")
        dfg, dfb = _fox_gate_bwd(dc, s["fg"], s["fb"], n + "d_fox_gate")
        G["forget_bias"][l] = dfb[0, 0:8]
        dproj, dkc, dkp, dvc, dvp, dbias, dsink = _swa_bwd(s["pm"], bias, s["sink"], dys[2], s["mlse"], dproj,
                                                         n + "d_swa")
        G["sink"][l] = dsink[0, 0:8]
        dbias_tot = dbias if dbias_tot is None else dbias_tot + dbias
        zpad = jnp.zeros((WINDOW, 128), F32)
        dsk = dkc + jnp.concatenate([dkp[WINDOW:], zpad], axis=0)
        dsv = dvc + jnp.concatenate([dvp[WINDOW:], zpad], axis=0)
        tail = jnp.concatenate([dsk.astype(BF16), dsv.astype(BF16), dfg.astype(BF16)], axis=1)
        dproj = lax.dynamic_update_slice(dproj, tail, (0, PROJ_MAIN - 256))
        dh = _mm(dproj, W["w_in_p"][l], "nt", BF16, n + "d_h", bk=1408)
        G["w_in_p"][l] = _mm(s["h"], dproj, "tn", BF16, n + "dw_in", bn=1408, bk=2048)
        dx, G["mix_norm_g"][l] = _rms_bwd(s["x0"], W["mix_norm_g"][l:l + 1], dh, dx, n + "d_mix_norm")
    drb = _swa_dbias_reduce(dbias_tot, bucket, "swa_dbias")
    G["rel_bias"] = drb[:, 0:8]
    G["final_norm_g"] = dg_final.reshape(D_MODEL)
    return loss_row, dx, G


def kernel(x, mem, mix_norm_g, w_in, forget_bias, conv_w, sink, w_branch, w_mix_out, rel_bias, xattn_norm_g, mem_norm_g, w_xq, w_xkv, w_xo, ffn_norm_g, w_ffn_gate, w_ffn_up, w_ffn_down, final_norm_g, loss_target, m_mix_norm_g, m_w_in, m_forget_bias, m_conv_w, m_sink, m_w_branch, m_w_mix_out, m_rel_bias, m_xattn_norm_g, m_mem_norm_g, m_w_xq, m_w_xkv, m_w_xo, m_ffn_norm_g, m_w_ffn_gate, m_w_ffn_up, m_w_ffn_down, m_final_norm_g, v_mix_norm_g, v_w_in, v_forget_bias, v_conv_w, v_sink, v_w_branch, v_w_mix_out, v_rel_bias, v_xattn_norm_g, v_mem_norm_g, v_w_xq, v_w_xkv, v_w_xo, v_ffn_norm_g, v_w_ffn_gate, v_w_ffn_up, v_w_ffn_down, v_final_norm_g):
    order = ("mix_norm_g", "w_in", "forget_bias", "conv_w", "sink", "w_branch", "w_mix_out", "rel_bias",
             "xattn_norm_g", "mem_norm_g", "w_xq", "w_xkv", "w_xo", "ffn_norm_g", "w_ffn_gate", "w_ffn_up",
             "w_ffn_down", "final_norm_g")
    w_sh = dict(zip(order, (mix_norm_g, w_in, forget_bias, conv_w, sink, w_branch, w_mix_out, rel_bias,
                            xattn_norm_g, mem_norm_g, w_xq, w_xkv, w_xo, ffn_norm_g, w_ffn_gate, w_ffn_up,
                            w_ffn_down, final_norm_g)))
    m_sh = dict(zip(order, (m_mix_norm_g, m_w_in, m_forget_bias, m_conv_w, m_sink, m_w_branch, m_w_mix_out,
                            m_rel_bias, m_xattn_norm_g, m_mem_norm_g, m_w_xq, m_w_xkv, m_w_xo, m_ffn_norm_g,
                            m_w_ffn_gate, m_w_ffn_up, m_w_ffn_down, m_final_norm_g)))
    v_sh = dict(zip(order, (v_mix_norm_g, v_w_in, v_forget_bias, v_conv_w, v_sink, v_w_branch, v_w_mix_out,
                            v_rel_bias, v_xattn_norm_g, v_mem_norm_g, v_w_xq, v_w_xkv, v_w_xo, v_ffn_norm_g,
                            v_w_ffn_gate, v_w_ffn_up, v_w_ffn_down, v_final_norm_g)))

    xi, yi, ci = lax.axis_index("x"), lax.axis_index("y"), lax.axis_index("c")
    as_idx = lambda *v: jnp.stack([jnp.asarray(t, I32) for t in v])
    me = 2 * xi + yi
    big = [name for name, _ in BIG]
    two_d = dict(BIG)
    two_d["conv_w"] = (6, 128)

    def slab(a, name):
        return (jnp.swapaxes(a, 1, 2) if name in TRANSPOSED else a).reshape(two_d[name])

    def unslab(a, name):
        shape = w_sh[name].shape
        if name in TRANSPOSED:
            return jnp.swapaxes(a.reshape(shape[0], shape[2], shape[1]), 1, 2)
        return a.reshape(shape)

    gathered = dict(zip(big, _ag_ring_multi(
        [_cast_place(slab(w_sh[name], name), as_idx(me), "place_" + name) for name in big])))
    conv_part = lax.dynamic_update_slice_in_dim(jnp.zeros((DEPTH, 3, BRANCH), F32), 0.5 * conv_w, 128 * me, axis=2)
    conv_full = _allreduce_small(conv_part.reshape(-1, 128), "allgather_conv").reshape(DEPTH, 3, BRANCH)

    def lay(name, l):
        g = gathered[name]
        return g.reshape(4, DEPTH, g.shape[1] // DEPTH, g.shape[2])[:, l]

    def by_cols(name, l):
        g = lay(name, l)
        return jnp.moveaxis(g, 0, 1).reshape(g.shape[1], 4 * g.shape[2])

    def by_rows(name, l):
        g = lay(name, l)
        return g.reshape(4 * g.shape[1], g.shape[2])

    W = {k: w_sh[k] for k in ("mix_norm_g", "forget_bias", "sink", "xattn_norm_g", "mem_norm_g",
                              "ffn_norm_g", "final_norm_g")}
    W["conv_w"] = conv_full
    W["w_in_p"] = [_perm_w_in(by_cols("w_in", l)) for l in range(DEPTH)]
    W["w_gu"] = [jnp.concatenate([by_rows("w_ffn_gate", l), by_rows("w_ffn_up", l)], axis=0) for l in range(DEPTH)]
    W["w_xkv"] = [by_cols("w_xkv", l) for l in range(DEPTH)]
    W["w_branch"] = [[jnp.moveaxis(lay("w_branch", l)[:, BRANCH * b:BRANCH * (b + 1)], 0, 1).reshape(BRANCH, D_MODEL)
                      for b in range(3)] for l in range(DEPTH)]
    for k in ("w_mix_out", "w_xq", "w_xo", "w_ffn_down"):
        W[k] = [by_rows(k, l) for l in range(DEPTH)]
    loss_row, dx, G = _local_step(x[0], mem[0], loss_target[0], W, rel_bias)

    def to_cols(g):
        return jnp.moveaxis(g.reshape(g.shape[0], 4, g.shape[1] // 4), 1, 0)

    def to_rows(g):
        return g.reshape(4, g.shape[0] // 4, g.shape[1])

    per_layer = {
        "w_in": [to_cols(_unperm_w_in(G["w_in_p"][l])) for l in range(DEPTH)],
        "w_branch": [jnp.concatenate([to_cols(g) for g in G["w_branch"][l]], axis=1) for l in range(DEPTH)],
        "w_mix_out": [to_rows(g) for g in G["w_mix_out"]],
        "w_xq": [to_rows(g) for g in G["w_xq"]],
        "w_xkv": [to_cols(g) for g in G["w_xkv"]],
        "w_xo": [to_rows(g) for g in G["w_xo"]],
        "w_ffn_gate": [to_rows(G["w_gu"][l][:D_FF]) for l in range(DEPTH)],
        "w_ffn_up": [to_rows(G["w_gu"][l][D_FF:]) for l in range(DEPTH)],
        "w_ffn_down": [to_rows(g) for g in G["w_ffn_down"]],
    }
    g4 = [jnp.concatenate(per_layer[name], axis=1).astype(BF16) for name in big]
    sib = _rs_sibling_multi(g4)
    pair = [_rs_add_pair(g4[t], sib[t], as_idx(ci), "_" + big[t]) for t in range(len(big))]
    diag = _rs_diag_multi(pair)
    nbrs = as_idx(2 * (1 - xi) + yi, 2 * xi + (1 - yi))
    merged = [_rs_merge(pair[t], diag[t], nbrs, "_" + big[t]) for t in range(len(big))]
    got = _rs_direct_multi(merged)
    reduced = _rs_share_multi([_rs_final(pair[t], got[t], as_idx(me, ci), "_" + big[t]) for t in range(len(big))])

    small = _unpack_small(_allreduce_small(_pack_small({
        "mix_norm_g": jnp.concatenate(G["mix_norm_g"], axis=0),
        "xattn_norm_g": jnp.concatenate(G["xattn_norm_g"], axis=0),
        "mem_norm_g": jnp.concatenate(G["mem_norm_g"], axis=0),
        "ffn_norm_g": jnp.concatenate(G["ffn_norm_g"], axis=0),
        "final_norm_g": G["final_norm_g"],
        "forget_bias": jnp.stack(G["forget_bias"]),
        "sink": jnp.stack(G["sink"]),
        "rel_bias": G["rel_bias"],
        "conv_w": jnp.stack(G["conv_w"]),
    }, SMALL_AND_CONV)), SMALL_AND_CONV)
    grads = {name: unslab(reduced[t], name) for t, name in enumerate(big)}
    grads.update(small)
    grads["conv_w"] = lax.dynamic_slice_in_dim(small["conv_w"], 128 * me, 128, axis=2)

    sm_names = [name for name, _ in SMALL]
    sd, sm_, sv_ = _adamw(_pack_small({k: w_sh[k] for k in sm_names}), _pack_small({k: grads[k] for k in sm_names}),
                          _pack_small({k: m_sh[k] for k in sm_names}), _pack_small({k: v_sh[k] for k in sm_names}),
                          "adamw_small")
    delta, new_m, new_v = _unpack_small(sd), _unpack_small(sm_), _unpack_small(sv_)
    for t, name in enumerate(big + ["conv_w"]):
        if name == "w_in":
            to3 = lambda a: jnp.transpose(a, (2, 0, 1))
            d, nm, nv = _adamw(to3(w_sh[name]), to3(grads[name]), to3(m_sh[name]), to3(v_sh[name]), "adamw_w_in")
            delta[name], new_m[name], new_v[name] = (jnp.transpose(a, (1, 2, 0)) for a in (d, nm, nv))
            continue
        g2 = reduced[t] if t < len(big) else slab(grads[name], name)
        d, nm, nv = _adamw(slab(w_sh[name], name), g2, slab(m_sh[name], name), slab(v_sh[name], name), "adamw_" + name)
        delta[name], new_m[name], new_v[name] = unslab(d, name), unslab(nm, name), unslab(nv, name)

    loss = lax.psum(loss_row[0, 0], ("x", "y", "c"))
    return (loss, dx[None], *[grads[k] for k in order], *[delta[k] for k in order],
            *[new_m[k] for k in order], *[new_v[k] for k in order])
```

```python
import math

import numpy as np
import jax
import jax.numpy as jnp
from jax import lax
from jax.experimental import pallas as pl
from jax.experimental.pallas import tpu as pltpu

F32 = jnp.float32
BF16 = jnp.bfloat16
I32 = jnp.int32

D_MODEL = 1024
DEPTH = 2
HEAD_DIM = 64
BRANCH = 512
N_BUCKETS = 32
WINDOW = 128
MEM_LEN = 256
X_HEADS = 4
X_HEAD_DIM = 256
D_FF = 2816
IN_COLS = 6920
PROJ_MAIN = 6912
PROJ_PAD = 7040
RMS_EPS = 1e-6
NEG = -1e30
ATT_SCALE = 0.125
X_SCALE = 0.0625

ADAM_LR = 0.001
ADAM_B1 = 0.9
ADAM_B2 = 0.999
ADAM_EPS = 1e-08
ADAM_WD = 0.01
ADAM_STEP = 10

VMEM_LIMIT = 48 * 1024 * 1024
MESH = pl.DeviceIdType.MESH

CB_GATE = (0, 1, 2)
CB_B, CB_C, CB_U, CB_FQ, CB_FK, CB_FV, CB_SQ = 6, 7, 8, 9, 10, 11, 12
CB_SK, CB_SV = 52, 53


def _cp(sem):
    return pltpu.CompilerParams(dimension_semantics=sem, vmem_limit_bytes=VMEM_LIMIT)


def _pick(n, prefs):
    for p in prefs:
        if p <= n and n % p == 0:
            return p
    return n


def _dot(a, b, dims):
    return lax.dot_general(a, b, (dims, ((), ())), preferred_element_type=F32)


def _dot_nn(a, b):
    return _dot(a, b, ((1,), (0,)))


def _dot_nt(a, b):
    return _dot(a, b, ((1,), (1,)))


def _dot_tn(a, b):
    return _dot(a, b, ((0,), (0,)))


def _mm(a, b, mode, out_dtype, name, res=None, bm=1024, bn=1024, bk=1024):
    if mode == "nn":
        (M, K), (K2, N) = a.shape, b.shape
    elif mode == "nt":
        (M, K), (N, K2) = a.shape, b.shape
    else:
        (K, M), (K2, N) = a.shape, b.shape
    assert K == K2, (name, a.shape, b.shape)
    bm = _pick(M, (bm, 1024, 512, 256, 128))
    bn = _pick(N, (bn, 1024, 768, 640, 512, 384, 256, 128))
    bk = _pick(K, (bk, 1024, 768, 640, 512, 384, 256, 128))
    nk = K // bk
    if mode == "tn":
        a_spec = pl.BlockSpec((bk, bm), lambda i, j, k: (k, i))
    else:
        a_spec = pl.BlockSpec((bm, bk), lambda i, j, k: (i, k))
    if mode == "nt":
        b_spec = pl.BlockSpec((bn, bk), lambda i, j, k: (j, k))
    else:
        b_spec = pl.BlockSpec((bk, bn), lambda i, j, k: (k, j))
    dims = {"nn": ((1,), (0,)), "nt": ((1,), (1,)), "tn": ((0,), (0,))}[mode]
    o_spec = pl.BlockSpec((bm, bn), lambda i, j, k: (i, j))
    has_res = res is not None

    def body(*refs):
        if has_res:
            a_ref, b_ref, r_ref, o_ref = refs[:4]
            scr = refs[4:]
        else:
            a_ref, b_ref, o_ref = refs[:3]
            r_ref = None
            scr = refs[3:]
        p = _dot(a_ref[...].astype(BF16), b_ref[...].astype(BF16), dims)
        if nk == 1:
            if has_res:
                p = p + r_ref[...]
            o_ref[...] = p.astype(out_dtype)
        else:
            acc = scr[0]
            k = pl.program_id(2)

            @pl.when(k == 0)
            def _():
                acc[...] = p

            @pl.when(k > 0)
            def _():
                acc[...] += p

            @pl.when(k == nk - 1)
            def _():
                r = acc[...]
                if has_res:
                    r = r + r_ref[...]
                o_ref[...] = r.astype(out_dtype)

    ins = [a, b] + ([res] if has_res else [])
    in_specs = [a_spec, b_spec] + ([o_spec] if has_res else [])
    return pl.pallas_call(
        body, name=name, grid=(M // bm, N // bn, nk),
        in_specs=in_specs, out_specs=o_spec,
        out_shape=jax.ShapeDtypeStruct((M, N), out_dtype),
        scratch_shapes=[pltpu.VMEM((bm, bn), F32)] if nk > 1 else [],
        compiler_params=_cp(("parallel", "parallel", "arbitrary")),
    )(*ins)


def _rms_fwd(x, g, name):
    T, Dm = x.shape
    bt = _pick(T, (512, 256))

    def body(x_ref, g_ref, o_ref):
        xv = x_ref[...]
        r = lax.rsqrt(jnp.mean(xv * xv, axis=-1, keepdims=True) + RMS_EPS)
        o_ref[...] = ((xv * r) * g_ref[...]).astype(BF16)

    return pl.pallas_call(
        body, name=name, grid=(T // bt,),
        in_specs=[pl.BlockSpec((bt, Dm), lambda i: (i, 0)), pl.BlockSpec((1, Dm), lambda i: (0, 0))],
        out_specs=pl.BlockSpec((bt, Dm), lambda i: (i, 0)),
        out_shape=jax.ShapeDtypeStruct((T, Dm), BF16),
        compiler_params=_cp(("parallel",)),
    )(x, g)


def _rms_bwd(x, g, dh, dres, name):
    T, Dm = x.shape
    bt = _pick(T, (512, 256))
    want_dx = dres is not None

    def body(*refs):
        if want_dx:
            x_ref, g_ref, dh_ref, dr_ref, dx_ref, dg_ref = refs
        else:
            x_ref, g_ref, dh_ref, dg_ref = refs
        xv = x_ref[...]
        r = lax.rsqrt(jnp.mean(xv * xv, axis=-1, keepdims=True) + RMS_EPS)
        xh = xv * r
        dhv = dh_ref[...].astype(F32)

        @pl.when(pl.program_id(0) == 0)
        def _():
            dg_ref[...] = jnp.zeros_like(dg_ref)

        dg_ref[...] += jnp.sum(dhv * xh, axis=0, keepdims=True)
        if want_dx:
            dyg = dhv * g_ref[...]
            dx_ref[...] = dr_ref[...] + r * (dyg - xh * jnp.mean(dyg * xh, axis=-1, keepdims=True))

    row = pl.BlockSpec((bt, Dm), lambda i: (i, 0))
    vec = pl.BlockSpec((1, Dm), lambda i: (0, 0))
    if want_dx:
        return pl.pallas_call(
            body, name=name, grid=(T // bt,),
            in_specs=[row, vec, row, row], out_specs=[row, vec],
            out_shape=[jax.ShapeDtypeStruct((T, Dm), F32), jax.ShapeDtypeStruct((1, Dm), F32)],
            compiler_params=_cp(("arbitrary",)),
        )(x, g, dh, dres)
    return None, pl.pallas_call(
        body, name=name, grid=(T // bt,),
        in_specs=[row, vec, row], out_specs=vec,
        out_shape=jax.ShapeDtypeStruct((1, Dm), F32),
        compiler_params=_cp(("arbitrary",)),
    )(x, g, dh)


def _final_loss(x, g, tgt, name):
    T, Dm = x.shape
    bt = _pick(T, (512, 256))

    def body(x_ref, g_ref, t_ref, loss_ref, dx_ref, dg_ref):
        xv = x_ref[...]
        r = lax.rsqrt(jnp.mean(xv * xv, axis=-1, keepdims=True) + RMS_EPS)
        xh = xv * r
        gv = g_ref[...]
        err = xh * gv - t_ref[...]

        @pl.when(pl.program_id(0) == 0)
        def _():
            dg_ref[...] = jnp.zeros_like(dg_ref)
            loss_ref[...] = jnp.zeros_like(loss_ref)

        loss_ref[...] += jnp.sum(err * err) * (0.5 / Dm)
        dy = err * (1.0 / Dm)
        dg_ref[...] += jnp.sum(dy * xh, axis=0, keepdims=True)
        dyg = dy * gv
        dx_ref[...] = r * (dyg - xh * jnp.mean(dyg * xh, axis=-1, keepdims=True))

    row = pl.BlockSpec((bt, Dm), lambda i: (i, 0))
    vec = pl.BlockSpec((1, Dm), lambda i: (0, 0))
    return pl.pallas_call(
        body, name=name, grid=(T // bt,),
        in_specs=[row, vec, row],
        out_specs=[pl.BlockSpec((1, 128), lambda i: (0, 0)), row, vec],
        out_shape=[jax.ShapeDtypeStruct((1, 128), F32), jax.ShapeDtypeStruct((T, Dm), F32),
                   jax.ShapeDtypeStruct((1, Dm), F32)],
        compiler_params=_cp(("arbitrary",)),
    )(x, g, tgt)


HALO = 16


def _shift_down(z, zprev, s):
    rolled = pltpu.roll(z, s, 0)
    hp = pltpu.roll(zprev, s, 0)
    row = lax.broadcasted_iota(I32, hp.shape, 0)
    top = jnp.where(row < s, hp, rolled[:HALO])
    return jnp.concatenate([top, rolled[HALO:]], axis=0)


def _shift_up(z, znext, s):
    n = z.shape[0]
    rolled = pltpu.roll(z, n - s, 0)
    hn = pltpu.roll(znext, HALO - s, 0)
    row = lax.broadcasted_iota(I32, hn.shape, 0)
    bot = jnp.where(row >= HALO - s, hn, rolled[n - HALO:])
    return jnp.concatenate([rolled[:n - HALO], bot], axis=0)


def _conv_fwd(pm, cw, name):
    T = pm.shape[0]
    bt = _pick(T, (512, 256))
    hb = bt // HALO

    def body(b_ref, c_ref, u_ref, cp_ref, up_ref, w_ref, o_ref):
        i = pl.program_id(0)
        z = c_ref[...].astype(F32) * u_ref[...].astype(F32)
        zp = cp_ref[...].astype(F32) * up_ref[...].astype(F32)
        zp = jnp.where(i > 0, zp, 0.0)
        w = w_ref[...]
        y = w[2:3] * z + w[1:2] * _shift_down(z, zp, 1) + w[0:1] * _shift_down(z, zp, 2)
        o_ref[...] = (b_ref[...].astype(F32) * y).astype(BF16)

    def col(cb):
        return pl.BlockSpec((bt, BRANCH), lambda i: (i, cb))

    def prev(cb):
        return pl.BlockSpec((HALO, BRANCH), lambda i: (jnp.maximum(i * hb - 1, 0), cb))

    return pl.pallas_call(
        body, name=name, grid=(T // bt,),
        in_specs=[col(CB_B), col(CB_C), col(CB_U), prev(CB_C), prev(CB_U),
                  pl.BlockSpec((8, BRANCH), lambda i: (0, 0))],
        out_specs=pl.BlockSpec((bt, BRANCH), lambda i: (i, 0)),
        out_shape=jax.ShapeDtypeStruct((T, BRANCH), BF16),
        compiler_params=_cp(("parallel",)),
    )(pm, pm, pm, pm, pm, cw)


def _conv_bwd(pm, cw, dy, dproj, name):
    T = pm.shape[0]
    bt = _pick(T, (512, 256))
    hb = bt // HALO
    nb = T // bt
    last_h = T // HALO - 1

    def body(b_ref, c_ref, u_ref, cp_ref, up_ref, bn_ref, dy_ref, dyn_ref, w_ref, buf_ref,
             dp_ref, dw_ref):
        del buf_ref
        db_ref = dp_ref.at[:, 0:BRANCH]
        dc_ref = dp_ref.at[:, BRANCH:2 * BRANCH]
        du_ref = dp_ref.at[:, 2 * BRANCH:3 * BRANCH]
        i = pl.program_id(0)
        cv = c_ref[...].astype(F32)
        uv = u_ref[...].astype(F32)
        bv = b_ref[...].astype(F32)
        z = cv * uv
        zp = jnp.where(i > 0, cp_ref[...].astype(F32) * up_ref[...].astype(F32), 0.0)
        w = w_ref[...]
        z1 = _shift_down(z, zp, 1)
        z2 = _shift_down(z, zp, 2)
        yc = w[2:3] * z + w[1:2] * z1 + w[0:1] * z2
        dyv = dy_ref[...].astype(F32)
        db_ref[...] = (dyv * yc).astype(BF16)
        g = dyv * bv
        gn = jnp.where(i < nb - 1, dyn_ref[...].astype(F32) * bn_ref[...].astype(F32), 0.0)
        dz = w[2:3] * g + w[1:2] * _shift_up(g, gn, 1) + w[0:1] * _shift_up(g, gn, 2)
        dc_ref[...] = (dz * uv).astype(BF16)
        du_ref[...] = (dz * cv).astype(BF16)

        @pl.when(i == 0)
        def _():
            dw_ref[...] = jnp.zeros_like(dw_ref)

        dw_ref[0:1, :] += jnp.sum(g * z2, axis=0, keepdims=True)
        dw_ref[1:2, :] += jnp.sum(g * z1, axis=0, keepdims=True)
        dw_ref[2:3, :] += jnp.sum(g * z, axis=0, keepdims=True)

    def col(cb):
        return pl.BlockSpec((bt, BRANCH), lambda i: (i, cb))

    def prev(cb):
        return pl.BlockSpec((HALO, BRANCH), lambda i: (jnp.maximum(i * hb - 1, 0), cb))

    def nxt(cb):
        return pl.BlockSpec((HALO, BRANCH), lambda i: (jnp.minimum((i + 1) * hb, last_h), cb))

    own = pl.BlockSpec((bt, BRANCH), lambda i: (i, 0))
    w_spec = pl.BlockSpec((8, BRANCH), lambda i: (0, 0))
    return pl.pallas_call(
        body, name=name, grid=(nb,),
        in_specs=[col(CB_B), col(CB_C), col(CB_U), prev(CB_C), prev(CB_U), nxt(CB_B), own,
                  pl.BlockSpec((HALO, BRANCH), lambda i: (jnp.minimum((i + 1) * hb, last_h), 0)), w_spec,
                  pl.BlockSpec(memory_space=pl.ANY)],
        out_specs=[pl.BlockSpec((bt, 3 * BRANCH), lambda i: (i, 2)), w_spec],
        out_shape=[jax.ShapeDtypeStruct(dproj.shape, dproj.dtype), jax.ShapeDtypeStruct((8, BRANCH), F32)],
        input_output_aliases={9: 0},
        compiler_params=_cp(("arbitrary",)),
    )(pm, pm, pm, pm, pm, pm, dy, dy, cw, dproj)


def _log_sigmoid(z):
    return jnp.minimum(z, 0.0) - jnp.log(1.0 + jnp.exp(-jnp.abs(z)))


def _fox_gate_fwd(fg, fb, name):
    T = fg.shape[0]
    bt = _pick(T, (256,))

    def body(f_ref, b_ref, c_ref, carry):
        @pl.when(pl.program_id(0) == 0)
        def _():
            carry[...] = jnp.zeros_like(carry)

        xv = _log_sigmoid(f_ref[...] + b_ref[...])
        row = lax.broadcasted_iota(I32, xv.shape, 0)
        s = 1
        while s < bt:
            xv = xv + jnp.where(row >= s, pltpu.roll(xv, s, 0), 0.0)
            s *= 2
        xv = xv + carry[...]
        c_ref[...] = xv
        carry[...] = xv[bt - 1:bt, :]

    blk = pl.BlockSpec((bt, 128), lambda i: (i, 0))
    return pl.pallas_call(
        body, name=name, grid=(T // bt,),
        in_specs=[blk, pl.BlockSpec((1, 128), lambda i: (0, 0))],
        out_specs=blk, out_shape=jax.ShapeDtypeStruct((T, 128), F32),
        scratch_shapes=[pltpu.VMEM((1, 128), F32)],
        compiler_params=_cp(("arbitrary",)),
    )(fg, fb)


def _fox_gate_bwd(dc, fg, fb, name):
    T = fg.shape[0]
    bt = _pick(T, (256,))
    nb = T // bt

    def body(d_ref, f_ref, b_ref, o_ref, db_ref, carry):
        @pl.when(pl.program_id(0) == 0)
        def _():
            carry[...] = jnp.zeros_like(carry)
            db_ref[...] = jnp.zeros_like(db_ref)

        xv = d_ref[...]
        row = lax.broadcasted_iota(I32, xv.shape, 0)
        s = 1
        while s < bt:
            xv = xv + jnp.where(row < bt - s, pltpu.roll(xv, bt - s, 0), 0.0)
            s *= 2
        xv = xv + carry[...]
        carry[...] = xv[0:1, :]
        z = f_ref[...] + b_ref[...]
        dz = xv * (1.0 / (1.0 + jnp.exp(z)))
        o_ref[...] = dz
        db_ref[...] += jnp.sum(dz, axis=0, keepdims=True)

    blk = pl.BlockSpec((bt, 128), lambda i: (nb - 1 - i, 0))
    vec = pl.BlockSpec((1, 128), lambda i: (0, 0))
    return pl.pallas_call(
        body, name=name, grid=(nb,),
        in_specs=[blk, blk, vec], out_specs=[blk, vec],
        out_shape=[jax.ShapeDtypeStruct((T, 128), F32), jax.ShapeDtypeStruct((1, 128), F32)],
        scratch_shapes=[pltpu.VMEM((1, 128), F32)],
        compiler_params=_cp(("arbitrary",)),
    )(dc, fg, fb)


def _lane_lo(shape):
    return lax.broadcasted_iota(I32, shape, 1) < HEAD_DIM


def _put_col(shape, h, col):
    lane = lax.broadcasted_iota(I32, shape, 1)
    return jnp.where(lane == h, col, 0.0)


def _fox_delta(o, do, name):
    T = o.shape[0]
    bt = _pick(T, (512, 256))

    def body(o_ref, d_ref, out_ref):
        prod = o_ref[...].astype(F32) * d_ref[...].astype(F32)
        out = jnp.zeros((bt, 128), F32)
        for h in range(8):
            out = out + _put_col((bt, 128), h, jnp.sum(prod[:, 64 * h:64 * h + 64], axis=-1, keepdims=True))
        out_ref[...] = out

    blk = pl.BlockSpec((bt, BRANCH), lambda i: (i, 0))
    return pl.pallas_call(
        body, name=name, grid=(T // bt,), in_specs=[blk, blk],
        out_specs=pl.BlockSpec((bt, 128), lambda i: (i, 0)),
        out_shape=jax.ShapeDtypeStruct((T, 128), F32),
        compiler_params=_cp(("parallel",)),
    )(o, do)


FOX_ROWS = 32


def _chunk_loop(n, chunk):
    for r in range(n):
        chunk(r)


def _tree(op, xs):
    xs = list(xs)
    while len(xs) > 1:
        xs = [op(xs[i], xs[i + 1]) if i + 1 < len(xs) else xs[i] for i in range(0, len(xs), 2)]
    return xs[0]


def _masked_halves(t):
    lo = _lane_lo(t.shape)
    z = jnp.zeros_like(t)
    return jnp.where(lo, t, z), jnp.where(lo, z, t)


def _fox2_fwd(pm, c_row, name):
    T = pm.shape[0]
    bq = _pick(T, (512, 256))
    bk = bq
    nq = T // bq
    R = FOX_ROWS
    ng = bk // 128

    def body(q_ref, k_ref, v_ref, ck_ref, o_ref, lse_ref, acc, m_s, l_s, a_s, s_scr, p_scr):
        qi = pl.program_id(0)
        ki = pl.program_id(1)

        @pl.when(ki == 0)
        def _():
            acc[...] = jnp.zeros_like(acc)
            m_s[...] = jnp.full_like(m_s, NEG)
            l_s[...] = jnp.zeros_like(l_s)

        def block(masked):
            qlo = _lane_lo((bq, 128))
            for p in range(4):
                sl = slice(128 * p, 128 * p + 128)
                qp = q_ref[:, sl] * ATT_SCALE
                vp = v_ref[:, sl]
                ks = _masked_halves(k_ref[:, sl])
                pvs = []
                for j in range(2):
                    h = 2 * p + j
                    s_scr[j] = _dot_nt(qp, ks[j])

                    def chunk(r, h=h, j=j):
                        r0 = r * R
                        rows = pl.ds(r0, R)
                        sc = [s_scr[j, rows, 128 * g:128 * g + 128] - ck_ref[h:h + 1, 128 * g:128 * g + 128]
                              for g in range(ng)]
                        if masked:
                            rid = lax.broadcasted_iota(I32, (R, 128), 0) + r0
                            cid = lax.broadcasted_iota(I32, (R, 128), 1)
                            sc = [jnp.where(cid + 128 * g <= rid, sc[g], NEG) for g in range(ng)]
                        m_old = m_s[h, rows, :]
                        m_new = jnp.maximum(m_old, jnp.max(_tree(jnp.maximum, sc), axis=-1, keepdims=True))
                        alpha = jnp.exp(m_old - m_new)
                        pe = [jnp.exp(sc[g] - m_new) for g in range(ng)]
                        l_s[h, rows, :] = alpha * l_s[h, rows, :] + _tree(jnp.add, pe)
                        m_s[h, rows, :] = m_new
                        a_s[j, rows, :] = alpha
                        for g in range(ng):
                            p_scr[j, rows, 128 * g:128 * g + 128] = pe[g].astype(BF16)

                    _chunk_loop(bq // R, chunk)
                    pvs.append(_dot_nn(p_scr[j], vp))
                acc[:, sl] = jnp.where(qlo, a_s[0], a_s[1]) * acc[:, sl] + jnp.where(qlo, pvs[0], pvs[1])

        @pl.when(ki < qi)
        def _():
            block(False)

        @pl.when(ki == qi)
        def _():
            block(True)

        @pl.when(ki == nq - 1)
        def _():
            qlo = _lane_lo((bq, 128))
            lse = jnp.zeros((bq, 128), F32)
            for p in range(4):
                sl = slice(128 * p, 128 * p + 128)
                l0 = jnp.sum(l_s[2 * p], axis=-1, keepdims=True)
                l1 = jnp.sum(l_s[2 * p + 1], axis=-1, keepdims=True)
                o_ref[:, sl] = (acc[:, sl] / jnp.where(qlo, l0, l1)).astype(BF16)
                lse = lse + _put_col((bq, 128), 2 * p, m_s[2 * p][:, 0:1] + jnp.log(l0))
                lse = lse + _put_col((bq, 128), 2 * p + 1, m_s[2 * p + 1][:, 0:1] + jnp.log(l1))
            lse_ref[...] = lse

    return pl.pallas_call(
        body, name=name, grid=(nq, nq),
        in_specs=[pl.BlockSpec((bq, BRANCH), lambda i, k: (i, CB_FQ)),
                  pl.BlockSpec((bk, BRANCH), lambda i, k: (jnp.minimum(k, i), CB_FK)),
                  pl.BlockSpec((bk, BRANCH), lambda i, k: (jnp.minimum(k, i), CB_FV)),
                  pl.BlockSpec((8, bk), lambda i, k: (0, jnp.minimum(k, i)))],
        out_specs=[pl.BlockSpec((bq, BRANCH), lambda i, k: (i, 0)),
                   pl.BlockSpec((bq, 128), lambda i, k: (i, 0))],
        out_shape=[jax.ShapeDtypeStruct((T, BRANCH), BF16), jax.ShapeDtypeStruct((T, 128), F32)],
        scratch_shapes=[pltpu.VMEM((bq, BRANCH), F32), pltpu.VMEM((8, bq, 128), F32),
                        pltpu.VMEM((8, bq, 128), F32), pltpu.VMEM((2, bq, 128), F32),
                        pltpu.VMEM((2, bq, bk), F32), pltpu.VMEM((2, bq, bk), BF16)],
        compiler_params=_cp(("parallel", "arbitrary")),
    )(pm, pm, pm, c_row)


def _fox2_bwd_dq(pm, do, c_row, lse, delta, dproj, name):
    T = pm.shape[0]
    bq = _pick(T, (512, 256))
    bk = bq
    nq = T // bq
    R = FOX_ROWS
    ng = bk // 128

    def body(q_ref, k_ref, v_ref, do_ref, ck_ref, lse_ref, dl_ref, buf_ref, dq_ref, dl2_ref,
             acc, e_s, s_scr, dp_scr, ds_scr):
        del buf_ref
        qi = pl.program_id(0)
        ki = pl.program_id(1)

        @pl.when(ki == 0)
        def _():
            acc[...] = jnp.zeros_like(acc)
            e_s[...] = jnp.zeros_like(e_s)

        def block(masked):
            qlo = _lane_lo((bq, 128))
            for p in range(4):
                sl = slice(128 * p, 128 * p + 128)
                qp = q_ref[:, sl] * ATT_SCALE
                kp = k_ref[:, sl]
                dop = do_ref[:, sl]
                ks = _masked_halves(kp)
                vs = _masked_halves(v_ref[:, sl])
                dqs = []
                for j in range(2):
                    h = 2 * p + j
                    s_scr[...] = _dot_nt(qp, ks[j])
                    dp_scr[...] = _dot_nt(dop, vs[j])

                    def chunk(r, h=h):
                        r0 = r * R
                        rows = pl.ds(r0, R)
                        lse_c = lse_ref[rows, h:h + 1]
                        dl_c = dl_ref[rows, h:h + 1]
                        if masked:
                            rid = lax.broadcasted_iota(I32, (R, 128), 0) + r0
                            cid = lax.broadcasted_iota(I32, (R, 128), 1)
                        dss = []
                        for g in range(ng):
                            gs = slice(128 * g, 128 * g + 128)
                            sc = s_scr[rows, gs] - ck_ref[h:h + 1, gs]
                            if masked:
                                sc = jnp.where(cid + 128 * g <= rid, sc, NEG)
                            ds = jnp.exp(sc - lse_c) * (dp_scr[rows, gs] - dl_c)
                            ds_scr[rows, gs] = ds.astype(BF16)
                            dss.append(ds)
                        e_s[h, rows, :] += _tree(jnp.add, dss)

                    _chunk_loop(bq // R, chunk)
                    dqs.append(_dot_nn(ds_scr[...], kp))
                acc[:, sl] += jnp.where(qlo, dqs[0], dqs[1])

        @pl.when(ki < qi)
        def _():
            block(False)

        @pl.when(ki == qi)
        def _():
            block(True)

        @pl.when(ki == nq - 1)
        def _():
            dq_ref[...] = (acc[...] * ATT_SCALE).astype(BF16)
            out = dl_ref[...]
            for h in range(8):
                out = out + _put_col((bq, 128), h, jnp.sum(e_s[h], axis=-1, keepdims=True))
            dl2_ref[...] = out

    qb = pl.BlockSpec((bq, 128), lambda i, k: (i, 0))
    return pl.pallas_call(
        body, name=name, grid=(nq, nq),
        in_specs=[pl.BlockSpec((bq, BRANCH), lambda i, k: (i, CB_FQ)),
                  pl.BlockSpec((bk, BRANCH), lambda i, k: (jnp.minimum(k, i), CB_FK)),
                  pl.BlockSpec((bk, BRANCH), lambda i, k: (jnp.minimum(k, i), CB_FV)),
                  pl.BlockSpec((bq, BRANCH), lambda i, k: (i, 0)),
                  pl.BlockSpec((8, bk), lambda i, k: (0, jnp.minimum(k, i))), qb, qb,
                  pl.BlockSpec(memory_space=pl.ANY)],
        out_specs=[pl.BlockSpec((bq, BRANCH), lambda i, k: (i, CB_FQ)), qb],
        out_shape=[jax.ShapeDtypeStruct(dproj.shape, dproj.dtype), jax.ShapeDtypeStruct((T, 128), F32)],
        input_output_aliases={7: 0},
        scratch_shapes=[pltpu.VMEM((bq, BRANCH), F32), pltpu.VMEM((8, bq, 128), F32),
                        pltpu.VMEM((bq, bk), F32), pltpu.VMEM((bq, bk), F32), pltpu.VMEM((bq, bk), BF16)],
        compiler_params=_cp(("parallel", "arbitrary")),
    )(pm, pm, pm, do, c_row, lse, delta, dproj)


def _fox2_bwd_dkv(pm, do, c_col, lse_row, delta_row, dproj, name):
    T = pm.shape[0]
    bk = _pick(T, (512, 256))
    bq = bk
    nk = T // bk
    R = FOX_ROWS
    ng = bq // 128

    def body(q_ref, k_ref, v_ref, do_ref, ck_ref, lse_ref, dl_ref, buf_ref, dkv_ref, dc_ref,
             dk_acc, dv_acc, dc_s, st_scr, dpt_scr, pt_scr, dst_scr):
        del buf_ref
        dk_ref = dkv_ref.at[:, 0:BRANCH]
        dv_ref = dkv_ref.at[:, BRANCH:2 * BRANCH]
        ki = pl.program_id(0)
        qi = pl.program_id(1)

        @pl.when(qi == 0)
        def _():
            dk_acc[...] = jnp.zeros_like(dk_acc)
            dv_acc[...] = jnp.zeros_like(dv_acc)
            dc_s[...] = jnp.zeros_like(dc_s)

        def block(masked):
            klo = _lane_lo((bk, 128))
            for p in range(4):
                sl = slice(128 * p, 128 * p + 128)
                qp = q_ref[:, sl]
                kp = k_ref[:, sl] * ATT_SCALE
                vp = v_ref[:, sl]
                dop = do_ref[:, sl]
                qs = _masked_halves(qp)
                dos = _masked_halves(dop)
                dks, dvs = [], []
                for j in range(2):
                    h = 2 * p + j
                    st_scr[...] = _dot_nt(kp, qs[j])
                    dpt_scr[...] = _dot_nt(vp, dos[j])

                    def chunk(r, h=h):
                        r0 = r * R
                        rows = pl.ds(r0, R)
                        ck_c = ck_ref[rows, h:h + 1]
                        if masked:
                            kid = lax.broadcasted_iota(I32, (R, 128), 0) + r0
                            qid = lax.broadcasted_iota(I32, (R, 128), 1)
                        dss = []
                        for g in range(ng):
                            gs = slice(128 * g, 128 * g + 128)
                            st = st_scr[rows, gs] - (ck_c + lse_ref[h:h + 1, gs])
                            if masked:
                                st = jnp.where(kid <= qid + 128 * g, st, NEG)
                            pt = jnp.exp(st)
                            dst = pt * (dpt_scr[rows, gs] - dl_ref[h:h + 1, gs])
                            pt_scr[rows, gs] = pt.astype(BF16)
                            dst_scr[rows, gs] = dst.astype(BF16)
                            dss.append(dst)
                        dc_s[h, rows, :] -= _tree(jnp.add, dss)

                    _chunk_loop(bk // R, chunk)
                    dvs.append(_dot_nn(pt_scr[...], dop))
                    dks.append(_dot_nn(dst_scr[...], qp))
                dk_acc[:, sl] += jnp.where(klo, dks[0], dks[1])
                dv_acc[:, sl] += jnp.where(klo, dvs[0], dvs[1])

        @pl.when(qi > ki)
        def _():
            block(False)

        @pl.when(qi == ki)
        def _():
            block(True)

        @pl.when(qi == nk - 1)
        def _():
            dk_ref[...] = (dk_acc[...] * ATT_SCALE).astype(BF16)
            dv_ref[...] = dv_acc[...].astype(BF16)
            out = jnp.zeros((bk, 128), F32)
            for h in range(8):
                out = out + _put_col((bk, 128), h, jnp.sum(dc_s[h], axis=-1, keepdims=True))
            dc_ref[...] = out

    qrow = pl.BlockSpec((8, bq), lambda k, i: (0, jnp.maximum(i, k)))
    return pl.pallas_call(
        body, name=name, grid=(nk, nk),
        in_specs=[pl.BlockSpec((bq, BRANCH), lambda k, i: (jnp.maximum(i, k), CB_FQ)),
                  pl.BlockSpec((bk, BRANCH), lambda k, i: (k, CB_FK)),
                  pl.BlockSpec((bk, BRANCH), lambda k, i: (k, CB_FV)),
                  pl.BlockSpec((bq, BRANCH), lambda k, i: (jnp.maximum(i, k), 0)),
                  pl.BlockSpec((bk, 128), lambda k, i: (k, 0)), qrow, qrow, pl.BlockSpec(memory_space=pl.ANY)],
        out_specs=[pl.BlockSpec((bk, 2 * BRANCH), lambda k, i: (k, 5)), pl.BlockSpec((bk, 128), lambda k, i: (k, 0))],
        out_shape=[jax.ShapeDtypeStruct(dproj.shape, dproj.dtype), jax.ShapeDtypeStruct((T, 128), F32)],
        input_output_aliases={7: 0},
        scratch_shapes=[pltpu.VMEM((bk, BRANCH), F32), pltpu.VMEM((bk, BRANCH), F32),
                        pltpu.VMEM((8, bk, 128), F32), pltpu.VMEM((bk, bq), F32), pltpu.VMEM((bk, bq), F32),
                        pltpu.VMEM((bk, bq), BF16), pltpu.VMEM((bk, bq), BF16)],
        compiler_params=_cp(("parallel", "arbitrary")),
    )(pm, pm, pm, do, c_col, lse_row, delta_row, dproj)


def _bucket_table():
    tq = np.arange(WINDOW, dtype=np.int32)[:, None]
    sk = np.arange(2 * WINDOW, dtype=np.int32)[None, :]
    n = np.maximum(WINDOW + tq - sk, 0)
    max_exact = N_BUCKETS // 2
    ratio = np.maximum(n, 1).astype(np.float32) / np.float32(max_exact)
    large = max_exact + (np.log(ratio) / np.float32(math.log(WINDOW / max_exact))
                         * np.float32(N_BUCKETS - max_exact)).astype(np.int32)
    large = np.minimum(large, N_BUCKETS - 1)
    return np.where(n < max_exact, n, large).astype(np.int32)


def _swap_halves(x):
    return pltpu.roll(x.astype(F32), HEAD_DIM, 1).astype(x.dtype)


def _kv_variants(t):
    lo = _lane_lo(t.shape)
    z = jnp.zeros_like(t)
    a0 = jnp.where(lo, t, z)
    b1 = jnp.where(lo, z, t)
    b0 = _swap_halves(a0)
    a1 = _swap_halves(b1)
    return (a0, a1), (b0, b1), (a0 + b0, a1 + b1)


def _stacked_head(s, r):
    return 4 * (s // 2) + 2 * r + (s % 2)


def _swa_bias(rel_bias, bucket, name):
    def body(rb_ref, bk_ref, o_ref):
        bkt = bk_ref[...]
        tq = lax.broadcasted_iota(I32, bkt.shape, 0)
        jj = lax.broadcasted_iota(I32, bkt.shape, 1)
        window = ((jj < WINDOW) & (jj > tq)) | ((jj >= WINDOW) & (jj - WINDOW <= tq))
        for s in range(4):
            for r in range(2):
                h = _stacked_head(s, r)

                def step(b, a, h=h):
                    return a + jnp.where(bkt == b, rb_ref[b, h], 0.0)
                val = lax.fori_loop(0, N_BUCKETS, step, jnp.zeros(bkt.shape, F32))
                o_ref[s, WINDOW * r:WINDOW * (r + 1), :] = jnp.where(window, val, NEG)

    return pl.pallas_call(
        body, name=name,
        in_specs=[pl.BlockSpec(memory_space=pltpu.SMEM), pl.BlockSpec(memory_space=pltpu.VMEM)],
        out_specs=pl.BlockSpec(memory_space=pltpu.VMEM),
        out_shape=jax.ShapeDtypeStruct((4, 2 * WINDOW, 2 * WINDOW), F32),
    )(rel_bias, bucket)


def _swa_dbias_reduce(dbias, bucket, name):
    def body(d_ref, bk_ref, o_ref):
        bkt = bk_ref[...]
        rowi = lax.broadcasted_iota(I32, (N_BUCKETS, 128), 0)
        lane = lax.broadcasted_iota(I32, (N_BUCKETS, 128), 1)
        out = jnp.zeros((N_BUCKETS, 128), F32)
        for s in range(4):
            for r in range(2):
                h = _stacked_head(s, r)
                dv = d_ref[s, WINDOW * r:WINDOW * (r + 1), :]

                def step(b, a, dv=dv, h=h):
                    tot = jnp.sum(jnp.where(bkt == b, dv, 0.0), keepdims=True)
                    return a + jnp.where((rowi == b) & (lane == h), tot, 0.0)
                out = lax.fori_loop(0, N_BUCKETS, step, out)
        o_ref[...] = out

    return pl.pallas_call(
        body, name=name,
        in_specs=[pl.BlockSpec(memory_space=pltpu.VMEM), pl.BlockSpec(memory_space=pltpu.VMEM)],
        out_specs=pl.BlockSpec(memory_space=pltpu.VMEM),
        out_shape=jax.ShapeDtypeStruct((N_BUCKETS, 128), F32),
    )(dbias, bucket)


def _swa_cols(vec, s):
    rows = lax.broadcasted_iota(I32, (2 * WINDOW, 1), 0)
    return jnp.where(rows < WINDOW, vec[:, _stacked_head(s, 0):_stacked_head(s, 0) + 1],
                     vec[:, _stacked_head(s, 1):_stacked_head(s, 1) + 1])


SWA_FWD_BLOCKS = 4
SWA_BWD_BLOCKS = 2

def _swa_scores(qg, kband, bias_tile, first):
    sc = _dot_nt(qg, kband) + bias_tile
    if first is not None:
        jj = lax.broadcasted_iota(I32, sc.shape, 1)
        sc = jnp.where((jj < WINDOW) & first, NEG, sc)
    return sc


def _swa_stack(ref, rows, g):
    return jnp.concatenate([ref[rows, 256 * g:256 * g + 128], ref[rows, 256 * g + 128:256 * g + 256]], axis=0)


def _swa_specs(blocks):
    step = blocks * WINDOW
    q = pl.BlockSpec((step, BRANCH), lambda i: (i, CB_SQ))
    kc = pl.BlockSpec((step, 128), lambda i: (i, CB_SK))
    kp = pl.BlockSpec((WINDOW, 128), lambda i: (jnp.maximum(blocks * i - 1, 0), CB_SK))
    vc = pl.BlockSpec((step, 128), lambda i: (i, CB_SV))
    vp = pl.BlockSpec((WINDOW, 128), lambda i: (jnp.maximum(blocks * i - 1, 0), CB_SV))
    bias = pl.BlockSpec((4, 2 * WINDOW, 2 * WINDOW), lambda i: (0, 0, 0))
    vec = pl.BlockSpec((1, 128), lambda i: (0, 0))
    return q, kc, kp, vc, vp, bias, vec


def _swa_fwd(pm, bias, sink, name):
    T = pm.shape[0]
    blocks = SWA_FWD_BLOCKS
    step = blocks * WINDOW
    nb = T // step

    def body(q_ref, kc_ref, kp_ref, vc_ref, vp_ref, b_ref, s_ref, o_ref, m_ref):
        i = pl.program_id(0)
        lo = _lane_lo((WINDOW, 128))
        sink_v = s_ref[...]
        for u in range(blocks):
            rows = slice(WINDOW * u, WINDOW * (u + 1))
            before = slice(WINDOW * (u - 1), WINDOW * u)
            first = (i == 0) if u == 0 else None
            kcur, vcur = kc_ref[rows, :], vc_ref[rows, :]
            kprev = kp_ref[...] if u == 0 else kc_ref[before, :]
            vprev = vp_ref[...] if u == 0 else vc_ref[before, :]
            kcA, kcB, _ = _kv_variants(kcur)
            kpA, kpB, _ = _kv_variants(kprev)
            _, _, vcD = _kv_variants(vcur)
            _, _, vpD = _kv_variants(vprev)
            mout = jnp.zeros((WINDOW, 128), F32)
            for g in range(2):
                qg = _swa_stack(q_ref, rows, g) * ATT_SCALE
                vband = jnp.concatenate([vpD[g], vcD[g]], axis=0)
                outs = []
                for par in range(2):
                    s = 2 * g + par
                    kband = jnp.concatenate([(kpA, kpB)[par][g], (kcA, kcB)[par][g]], axis=0)
                    sc = _swa_scores(qg, kband, b_ref[s], first)
                    sk = _swa_cols(sink_v, s)
                    m = jnp.maximum(jnp.max(sc, axis=-1, keepdims=True), sk)
                    e = jnp.exp(sc - m)
                    den = jnp.sum(e, axis=-1, keepdims=True) + jnp.exp(sk - m)
                    outs.append(_dot_nn((e * (1.0 / den)).astype(BF16), vband))
                    lse = m + jnp.log(den)
                    mout = mout + _put_col((WINDOW, 128), _stacked_head(s, 0), lse[:WINDOW])
                    mout = mout + _put_col((WINDOW, 128), _stacked_head(s, 1), lse[WINDOW:])
                for r in range(2):
                    sl = slice(256 * g + 128 * r, 256 * g + 128 * r + 128)
                    o_ref[rows, sl] = jnp.where(lo, outs[0][WINDOW * r:WINDOW * (r + 1)],
                                                outs[1][WINDOW * r:WINDOW * (r + 1)]).astype(BF16)
            m_ref[rows, :] = mout

    q, kc, kp, vc, vp, bs, vec = _swa_specs(blocks)
    return pl.pallas_call(
        body, name=name, grid=(nb,),
        in_specs=[q, kc, kp, vc, vp, bs, vec],
        out_specs=[pl.BlockSpec((step, BRANCH), lambda i: (i, 0)),
                   pl.BlockSpec((step, 128), lambda i: (i, 0))],
        out_shape=[jax.ShapeDtypeStruct((T, BRANCH), BF16), jax.ShapeDtypeStruct((T, 128), F32)],
        compiler_params=_cp(("parallel",)),
    )(pm, pm, pm, pm, pm, bias, sink)


def _swa_bwd(pm, bias, sink, do, mlse, dproj, name):
    T = pm.shape[0]
    blocks = SWA_BWD_BLOCKS
    step = blocks * WINDOW
    nb = T // step

    def fold(zz):
        return zz + pltpu.roll(zz, HEAD_DIM, 1)

    def body(q_ref, kc_ref, kp_ref, vc_ref, vp_ref, b_ref, s_ref, do_ref, m_ref, buf_ref,
             dq_ref, dkc_ref, dkp_ref, dvc_ref, dvp_ref, db_ref, ds_ref):
        del buf_ref
        i = pl.program_id(0)

        @pl.when(i == 0)
        def _():
            db_ref[...] = jnp.zeros_like(db_ref)
            ds_ref[...] = jnp.zeros_like(ds_ref)

        lo = _lane_lo((WINDOW, 128))
        lo2 = _lane_lo((2 * WINDOW, 128))
        sink_v = s_ref[...]
        dsink = jnp.zeros((1, 128), F32)
        for u in range(blocks):
            rows = slice(WINDOW * u, WINDOW * (u + 1))
            before = slice(WINDOW * (u - 1), WINDOW * u)
            first = (i == 0) if u == 0 else None
            kcur, vcur = kc_ref[rows, :], vc_ref[rows, :]
            kprev = kp_ref[...] if u == 0 else kc_ref[before, :]
            vprev = vp_ref[...] if u == 0 else vc_ref[before, :]
            kcA, kcB, kcD = _kv_variants(kcur)
            kpA, kpB, kpD = _kv_variants(kprev)
            vcA, vcB, _ = _kv_variants(vcur)
            vpA, vpB, _ = _kv_variants(vprev)
            mv = m_ref[rows, :]
            zks, zvs = [], []
            for g in range(2):
                qraw = _swa_stack(q_ref, rows, g)
                qg = qraw * ATT_SCALE
                dog = _swa_stack(do_ref, rows, g)
                kband_d = jnp.concatenate([kpD[g], kcD[g]], axis=0)
                dqs, mks, mvs = [], [], []
                for par in range(2):
                    s = 2 * g + par
                    kband = jnp.concatenate([(kpA, kpB)[par][g], (kcA, kcB)[par][g]], axis=0)
                    vband = jnp.concatenate([(vpA, vpB)[par][g], (vcA, vcB)[par][g]], axis=0)
                    sc = _swa_scores(qg, kband, b_ref[s], first)
                    h0, h1 = _stacked_head(s, 0), _stacked_head(s, 1)
                    m_c = jnp.concatenate([mv[:, h0:h0 + 1], mv[:, h1:h1 + 1]], axis=0)
                    pr = jnp.exp(sc - m_c)
                    psink = jnp.exp(_swa_cols(sink_v, s) - m_c)
                    dp = _dot_nt(dog, vband)
                    delta = jnp.sum(pr * dp, axis=-1, keepdims=True)
                    dsc = pr * (dp - delta)
                    db_ref[s] += dsc
                    sd = psink * delta
                    dsink = dsink - _put_col((1, 128), h0, jnp.sum(sd[:WINDOW], keepdims=True))
                    dsink = dsink - _put_col((1, 128), h1, jnp.sum(sd[WINDOW:], keepdims=True))
                    dsb = dsc.astype(BF16)
                    dqs.append(_dot_nn(dsb, kband_d))
                    mks.append(_dot_tn(dsb, qraw))
                    mvs.append(_dot_tn(pr.astype(BF16), dog))
                for r in range(2):
                    sl = slice(256 * g + 128 * r, 256 * g + 128 * r + 128)
                    dq_ref[rows, sl] = (jnp.where(lo, dqs[0][WINDOW * r:WINDOW * (r + 1)],
                                                  dqs[1][WINDOW * r:WINDOW * (r + 1)]) * ATT_SCALE).astype(BF16)
                zks.append(fold(jnp.where(lo2, mks[0], mks[1])))
                zvs.append(fold(jnp.where(lo2, mvs[0], mvs[1])))
            dk = jnp.where(lo2, zks[0], zks[1]) * ATT_SCALE
            dv = jnp.where(lo2, zvs[0], zvs[1])
            dkp_ref[rows, :] = dk[:WINDOW]
            dkc_ref[rows, :] = dk[WINDOW:]
            dvp_ref[rows, :] = dv[:WINDOW]
            dvc_ref[rows, :] = dv[WINDOW:]
        ds_ref[...] += dsink

    q, kc, kp, vc, vp, bs, vec = _swa_specs(blocks)
    own = pl.BlockSpec((step, BRANCH), lambda i: (i, 0))
    sm = pl.BlockSpec((step, 128), lambda i: (i, 0))
    f128 = jax.ShapeDtypeStruct((T, 128), F32)
    return pl.pallas_call(
        body, name=name, grid=(nb,),
        in_specs=[q, kc, kp, vc, vp, bs, vec, own, sm, pl.BlockSpec(memory_space=pl.ANY)],
        out_specs=[pl.BlockSpec((step, BRANCH), lambda i: (i, CB_SQ)), sm, sm, sm, sm, bs, vec],
        out_shape=[jax.ShapeDtypeStruct(dproj.shape, dproj.dtype), f128, f128, f128, f128,
                   jax.ShapeDtypeStruct((4, 2 * WINDOW, 2 * WINDOW), F32), jax.ShapeDtypeStruct((1, 128), F32)],
        input_output_aliases={9: 0},
        compiler_params=_cp(("arbitrary",)),
    )(pm, pm, pm, pm, pm, bias, sink, do, mlse, dproj)


def _merge_fwd(pm, us, name):
    T = pm.shape[0]
    bt = _pick(T, (512, 256))

    def body(g0, g1, g2, u0, u1, u2, o_ref):
        acc = jax.nn.sigmoid(g0[...].astype(F32)) * u0[...].astype(F32)
        acc = acc + jax.nn.sigmoid(g1[...].astype(F32)) * u1[...].astype(F32)
        acc = acc + jax.nn.sigmoid(g2[...].astype(F32)) * u2[...].astype(F32)
        o_ref[...] = acc.astype(BF16)

    own = pl.BlockSpec((bt, D_MODEL), lambda i: (i, 0))
    gs = [pl.BlockSpec((bt, D_MODEL), lambda i, cb=cb: (i, cb)) for cb in CB_GATE]
    return pl.pallas_call(
        body, name=name, grid=(T // bt,), in_specs=gs + [own, own, own], out_specs=own,
        out_shape=jax.ShapeDtypeStruct((T, D_MODEL), BF16),
        compiler_params=_cp(("parallel",)),
    )(pm, pm, pm, *us)


def _merge_bwd(pm, us, dm, name):
    T = pm.shape[0]
    bt = _pick(T, (512, 256))

    def body(g0, g1, g2, u0, u1, u2, dm_ref, du0, du1, du2, dg_ref):
        dmv = dm_ref[...].astype(F32)
        for b, (g, u, du) in enumerate(((g0, u0, du0), (g1, u1, du1), (g2, u2, du2))):
            s = jax.nn.sigmoid(g[...].astype(F32))
            du[...] = (dmv * s).astype(BF16)
            dg_ref[:, D_MODEL * b:D_MODEL * (b + 1)] = (dmv * u[...].astype(F32) * s * (1.0 - s)).astype(BF16)

    own = pl.BlockSpec((bt, D_MODEL), lambda i: (i, 0))
    gs = [pl.BlockSpec((bt, D_MODEL), lambda i, cb=cb: (i, cb)) for cb in CB_GATE]
    act = jax.ShapeDtypeStruct((T, D_MODEL), BF16)
    return pl.pallas_call(
        body, name=name, grid=(T // bt,), in_specs=gs + [own, own, own, own],
        out_specs=[own, own, own, pl.BlockSpec((bt, 3 * D_MODEL), lambda i: (i, 0))],
        out_shape=[act, act, act, jax.ShapeDtypeStruct((T, PROJ_PAD), BF16)],
        compiler_params=_cp(("parallel",)),
    )(pm, pm, pm, *us, dm)


def _swiglu_fwd(ab, name):
    T = ab.shape[0]
    bt = _pick(T, (512, 256))

    def body(a_ref, b_ref, o_ref):
        a = a_ref[...].astype(F32)
        o_ref[...] = (a * jax.nn.sigmoid(a) * b_ref[...].astype(F32)).astype(BF16)

    return pl.pallas_call(
        body, name=name, grid=(T // bt,),
        in_specs=[pl.BlockSpec((bt, D_FF), lambda i: (i, 0)), pl.BlockSpec((bt, D_FF), lambda i: (i, 1))],
        out_specs=pl.BlockSpec((bt, D_FF), lambda i: (i, 0)),
        out_shape=jax.ShapeDtypeStruct((T, D_FF), BF16),
        compiler_params=_cp(("parallel",)),
    )(ab, ab)


def _swiglu_bwd(ab, dh, name):
    T = ab.shape[0]
    bt = _pick(T, (512, 256))

    def body(a_ref, b_ref, d_ref, o_ref):
        a = a_ref[...].astype(F32)
        b = b_ref[...].astype(F32)
        d = d_ref[...].astype(F32)
        s = jax.nn.sigmoid(a)
        o_ref[:, 0:D_FF] = (d * b * (s + a * s * (1.0 - s))).astype(BF16)
        o_ref[:, D_FF:2 * D_FF] = (d * a * s).astype(BF16)

    return pl.pallas_call(
        body, name=name, grid=(T // bt,),
        in_specs=[pl.BlockSpec((bt, D_FF), lambda i: (i, 0)), pl.BlockSpec((bt, D_FF), lambda i: (i, 1)),
                  pl.BlockSpec((bt, D_FF), lambda i: (i, 0))],
        out_specs=pl.BlockSpec((bt, 2 * D_FF), lambda i: (i, 0)),
        out_shape=jax.ShapeDtypeStruct((T, 2 * D_FF), BF16),
        compiler_params=_cp(("parallel",)),
    )(ab, ab, dh)


def _xattn_probs(q_ref, kv_ref, h):
    sl = slice(X_HEAD_DIM * h, X_HEAD_DIM * (h + 1))
    qh = q_ref[:, sl]
    kh = kv_ref[:, sl]
    vh = kv_ref[:, D_MODEL + X_HEAD_DIM * h:D_MODEL + X_HEAD_DIM * (h + 1)]
    s = _dot_nt(qh, kh) * X_SCALE
    e = jnp.exp(s - jnp.max(s, axis=-1, keepdims=True))
    return qh, kh, vh, e * (1.0 / jnp.sum(e, axis=-1, keepdims=True))


def _xattn_fwd(q, kv, name):
    T = q.shape[0]
    bq = _pick(T, (512, 256))

    def body(q_ref, kv_ref, o_ref):
        for h in range(X_HEADS):
            _, _, vh, p = _xattn_probs(q_ref, kv_ref, h)
            o_ref[:, X_HEAD_DIM * h:X_HEAD_DIM * (h + 1)] = _dot_nn(p.astype(BF16), vh).astype(BF16)

    own = pl.BlockSpec((bq, D_MODEL), lambda i: (i, 0))
    return pl.pallas_call(
        body, name=name, grid=(T // bq,),
        in_specs=[own, pl.BlockSpec((MEM_LEN, 2 * D_MODEL), lambda i: (0, 0))], out_specs=own,
        out_shape=jax.ShapeDtypeStruct((T, D_MODEL), BF16),
        compiler_params=_cp(("parallel",)),
    )(q, kv)


def _xattn_bwd(q, kv, do, name):
    T = q.shape[0]
    bq = _pick(T, (512, 256))

    def body(q_ref, kv_ref, do_ref, dq_ref, dkv_ref):
        @pl.when(pl.program_id(0) == 0)
        def _():
            dkv_ref[...] = jnp.zeros_like(dkv_ref)

        for h in range(X_HEADS):
            sl = slice(X_HEAD_DIM * h, X_HEAD_DIM * (h + 1))
            qh, kh, vh, p = _xattn_probs(q_ref, kv_ref, h)
            doh = do_ref[:, sl]
            dp = _dot_nt(doh, vh)
            ds = (p * (dp - jnp.sum(p * dp, axis=-1, keepdims=True)) * X_SCALE).astype(BF16)
            dq_ref[:, sl] = _dot_nn(ds, kh).astype(BF16)
            dkv_ref[:, sl] += _dot_tn(ds, qh)
            dkv_ref[:, D_MODEL + X_HEAD_DIM * h:D_MODEL + X_HEAD_DIM * (h + 1)] += _dot_tn(p.astype(BF16), doh)

    own = pl.BlockSpec((bq, D_MODEL), lambda i: (i, 0))
    kvs = pl.BlockSpec((MEM_LEN, 2 * D_MODEL), lambda i: (0, 0))
    return pl.pallas_call(
        body, name=name, grid=(T // bq,), in_specs=[own, kvs, own], out_specs=[own, kvs],
        out_shape=[jax.ShapeDtypeStruct((T, D_MODEL), BF16), jax.ShapeDtypeStruct((MEM_LEN, 2 * D_MODEL), F32)],
        compiler_params=_cp(("arbitrary",)),
    )(q, kv, do)


def _adamw(w, g, m, v, name):
    R, C = w.shape[0], w.shape[-1]
    rest = w.shape[1:]
    row_bytes = int(np.prod(rest[:-1], dtype=np.int64)) * (-(-C // 128) * 128) * 4
    cands = (1024, 512, 256, 128, 64, 32, 16, 8) if w.ndim == 2 else range(R, 0, -1)
    bt = R
    for cand in cands:
        if R % cand == 0 and cand * row_bytes <= (3 << 19):
            bt = cand
            break
    zeros = (0,) * len(rest)

    def body(w_ref, g_ref, m_ref, v_ref, d_ref, nm_ref, nv_ref):
        gv = g_ref[...]
        mn = ADAM_B1 * m_ref[...] + (1.0 - ADAM_B1) * gv
        vn = ADAM_B2 * v_ref[...] + (1.0 - ADAM_B2) * (gv * gv)
        m_hat = mn / (1.0 - ADAM_B1 ** ADAM_STEP)
        v_hat = vn / (1.0 - ADAM_B2 ** ADAM_STEP)
        d_ref[...] = -ADAM_LR * (m_hat / (jnp.sqrt(v_hat) + ADAM_EPS) + ADAM_WD * w_ref[...])
        nm_ref[...] = mn
        nv_ref[...] = vn

    blk = pl.BlockSpec((bt,) + tuple(rest), lambda i: (i,) + zeros)
    out = jax.ShapeDtypeStruct(w.shape, F32)
    return pl.pallas_call(
        body, name=name, grid=(R // bt,), in_specs=[blk] * 4, out_specs=[blk] * 3,
        out_shape=[out, out, out], compiler_params=_cp(("parallel",)),
    )(w, g, m, v)


ANY = pl.BlockSpec(memory_space=pl.ANY)

BIG = (
    ("w_in", (2048, 1730)), ("w_branch", (3072, 256)), ("w_mix_out", (512, 1024)), ("w_xq", (512, 1024)),
    ("w_xkv", (2048, 512)), ("w_xo", (512, 1024)), ("w_ffn_gate", (1408, 1024)), ("w_ffn_up", (1408, 1024)),
    ("w_ffn_down", (1408, 1024)),
)
TRANSPOSED = ("w_ffn_gate", "w_ffn_up")
ROW_BLOCKS = (512, 256, 352, 128, 16)


def _neighbours():
    x, y, c = lax.axis_index("x"), lax.axis_index("y"), lax.axis_index("c")
    idx = (2 * x + y, 2 * (1 - x) + y, 2 * x + (1 - y), 2 * (1 - x) + (1 - y))
    return idx, (x, y, c), (1 - x, y, c), (x, 1 - y, c), (x, y, 1 - c)


def _remote(src, dst, sems, k, to):
    send_sems, recv_sems = sems
    return pltpu.make_async_remote_copy(src_ref=src, dst_ref=dst, send_sem=send_sems.at[k], recv_sem=recv_sems.at[k],
                                        device_id=to, device_id_type=MESH)


def _cast_place(w, me_idx, name):
    R, Wd = w.shape
    bt = _pick(R, ROW_BLOCKS)

    def body(i_ref, w_ref, o_ref):
        o_ref[0] = w_ref[...].astype(BF16)

    grid_spec = pltpu.PrefetchScalarGridSpec(
        num_scalar_prefetch=1, grid=(R // bt,),
        in_specs=[pl.BlockSpec((bt, Wd), lambda i, idx: (i, 0))],
        out_specs=pl.BlockSpec((1, bt, Wd), lambda i, idx: (idx[0], i, 0)))
    return pl.pallas_call(
        body, name=name, grid_spec=grid_spec, out_shape=jax.ShapeDtypeStruct((4, R, Wd), BF16),
        compiler_params=_cp(("parallel",)),
    )(me_idx, w)


def _ag_ring_multi(bufs):
    n = len(bufs)

    def body(*refs):
        o = refs[n:2 * n]
        sems = refs[2 * n:]
        (me, ix, iy, idg), here, xn, yn, sib = _neighbours()
        c = here[2]

        def piece(t, k, other):
            h = bufs[t].shape[1] // 2
            q = h // 2
            base = ((1 - c) if other else c) * h
            return [(ix, pl.ds(base, h)), (iy, pl.ds(base, h)), (idg, pl.ds(base, q)), (idg, pl.ds(base + q, q))][k]

        def copy(t, k, slab, rows, to):
            ref = o[t].at[slab, rows]
            return _remote(ref, ref, sems, 8 * t + k, to)

        sends = []

        def go(cp):
            cp.start()
            sends.append(cp)

        for t in range(n):
            h = bufs[t].shape[1] // 2
            go(copy(t, 0, me, pl.ds(c * h, h), xn))
            go(copy(t, 1, me, pl.ds(c * h, h), yn))
        for k in range(4):
            for t in range(n):
                slab, rows = piece(t, k, False)
                copy(t, k, slab, rows, here).wait_recv()
                if k == 0:
                    go(copy(t, 2, ix, piece(t, 2, False)[1], yn))
                if k == 1:
                    go(copy(t, 3, iy, piece(t, 3, False)[1], xn))
                go(copy(t, 4 + k, slab, rows, sib))
        for k in range(4):
            for t in range(n):
                slab, rows = piece(t, k, True)
                copy(t, 4 + k, slab, rows, here).wait_recv()
        for cp in sends:
            cp.wait_send()

    return pl.pallas_call(
        body, name="ag_weights", in_specs=[ANY] * n, out_specs=[ANY] * n,
        input_output_aliases={t: t for t in range(n)},
        out_shape=[jax.ShapeDtypeStruct(b.shape, b.dtype) for b in bufs],
        scratch_shapes=[pltpu.SemaphoreType.DMA((8 * n,)), pltpu.SemaphoreType.DMA((8 * n,))],
    )(*bufs)


def _exchange_multi(srcs, out_shapes, plan, name, aliased=False):
    n = len(srcs)

    def body(*refs):
        ins, outs, sems = refs[:n], refs[n:2 * n], refs[2 * n:]
        places = _neighbours()
        here = places[1]
        per = [plan(t, ins[t], outs[t], places) for t in range(n)]
        width = max(len(p) for p in per)
        started = []
        for t in range(n):
            for k, (src, dst, to, land) in enumerate(per[t]):
                cp = _remote(src, dst, sems, width * t + k, to)
                cp.start()
                started.append(cp)
        for t in range(n):
            for k, (src, dst, to, land) in enumerate(per[t]):
                _remote(land, land, sems, width * t + k, here).wait_recv()
        for cp in started:
            cp.wait_send()

    nsem = 2 * n
    return pl.pallas_call(
        body, name=name, in_specs=[ANY] * n, out_specs=[ANY] * n,
        input_output_aliases={t: t for t in range(n)} if aliased else {},
        out_shape=[jax.ShapeDtypeStruct(s, d) for s, d in out_shapes],
        scratch_shapes=[pltpu.SemaphoreType.DMA((nsem,)), pltpu.SemaphoreType.DMA((nsem,))],
    )(*srcs)


def _rs_sibling_multi(gs):
    def plan(t, g, o, places):
        (_, here, _, _, sib) = places
        h = gs[t].shape[1] // 2
        return [(g.at[:, pl.ds((1 - here[2]) * h, h)], o, sib, o)]

    return _exchange_multi(gs, [((4, g.shape[1] // 2, g.shape[2]), g.dtype) for g in gs], plan, "rs_sibling")


def _rs_add_pair(g4, sib, cidx, tag=""):
    _, R, Wd = g4.shape
    hrows = R // 2
    bt = _pick(hrows, ROW_BLOCKS)
    nb = hrows // bt

    def body(c_ref, a_ref, b_ref, o_ref):
        o_ref[...] = (a_ref[...].astype(F32) + b_ref[...].astype(F32)).astype(o_ref.dtype)

    grid_spec = pltpu.PrefetchScalarGridSpec(
        num_scalar_prefetch=1, grid=(4, nb),
        in_specs=[pl.BlockSpec((1, bt, Wd), lambda j, i, c: (j, c[0] * nb + i, 0)),
                  pl.BlockSpec((1, bt, Wd), lambda j, i, c: (j, i, 0))],
        out_specs=pl.BlockSpec((1, bt, Wd), lambda j, i, c: (j, i, 0)))
    return pl.pallas_call(
        body, name="rs_add_pair" + tag, grid_spec=grid_spec,
        out_shape=jax.ShapeDtypeStruct((4, hrows, Wd), g4.dtype),
        compiler_params=_cp(("parallel", "parallel")),
    )(cidx, g4, sib)


def _rs_diag_multi(rs):
    def plan(t, r, o, places):
        ((_, _, _, idg), _, xn, yn, _) = places
        q = rs[t].shape[1] // 2
        return [(r.at[idg, pl.ds(0, q)], o.at[0], xn, o.at[0]), (r.at[idg, pl.ds(q, q)], o.at[1], yn, o.at[1])]

    return _exchange_multi(rs, [((2, r.shape[1] // 2, r.shape[2]), r.dtype) for r in rs], plan, "rs_diag")


def _rs_merge(r4, dg, nbr_idx, tag=""):
    _, hrows, Wd = r4.shape
    bt = _pick(hrows // 2, ROW_BLOCKS)
    nb = hrows // bt
    nq = nb // 2

    def body(i_ref, r_ref, d_ref, o_ref):
        w = pl.program_id(0)
        i = pl.program_id(1)
        merged = jnp.where(w == 0, i >= nq, i < nq)
        add = jnp.where(merged, d_ref[...].astype(F32), 0.0)
        o_ref[...] = (r_ref[...].astype(F32) + add).astype(o_ref.dtype)

    grid_spec = pltpu.PrefetchScalarGridSpec(
        num_scalar_prefetch=1, grid=(2, nb),
        in_specs=[pl.BlockSpec((1, bt, Wd), lambda w, i, idx: (idx[w], i, 0)),
                  pl.BlockSpec((1, bt, Wd), lambda w, i, idx: (1 - w, jnp.clip(i - (1 - w) * nq, 0, nq - 1), 0))],
        out_specs=pl.BlockSpec((1, bt, Wd), lambda w, i, idx: (w, i, 0)))
    return pl.pallas_call(
        body, name="rs_merge" + tag, grid_spec=grid_spec,
        out_shape=jax.ShapeDtypeStruct((2, hrows, Wd), r4.dtype),
        compiler_params=_cp(("parallel", "parallel")),
    )(nbr_idx, r4, dg)


def _rs_direct_multi(ms):
    def plan(t, m, o, places):
        (_, _, xn, yn, _) = places
        return [(m.at[0], o.at[0], xn, o.at[0]), (m.at[1], o.at[1], yn, o.at[1])]

    return _exchange_multi(ms, [(m.shape, m.dtype) for m in ms], plan, "rs_direct")


def _rs_final(r4, got, me_c, tag=""):
    _, hrows, Wd = r4.shape
    bt = _pick(hrows, ROW_BLOCKS)
    nb = hrows // bt

    def body(i_ref, r_ref, g_ref, o_ref):
        o_ref[...] = (r_ref[0].astype(F32) + g_ref[0].astype(F32)) + g_ref[1].astype(F32)

    grid_spec = pltpu.PrefetchScalarGridSpec(
        num_scalar_prefetch=1, grid=(nb,),
        in_specs=[pl.BlockSpec((1, bt, Wd), lambda i, idx: (idx[0], i, 0)),
                  pl.BlockSpec((2, bt, Wd), lambda i, idx: (0, i, 0))],
        out_specs=pl.BlockSpec((bt, Wd), lambda i, idx: (idx[1] * nb + i, 0)))
    return pl.pallas_call(
        body, name="rs_final" + tag, grid_spec=grid_spec,
        out_shape=jax.ShapeDtypeStruct((2 * hrows, Wd), F32),
        compiler_params=_cp(("parallel",)),
    )(me_c, r4, got)


def _rs_share_multi(bufs):
    def plan(t, b, o, places):
        (_, here, _, _, sib) = places
        h = bufs[t].shape[0] // 2
        mine = o.at[pl.ds(here[2] * h, h)]
        return [(mine, mine, sib, o.at[pl.ds((1 - here[2]) * h, h)])]

    return _exchange_multi(bufs, [(b.shape, b.dtype) for b in bufs], plan, "rs_share", aliased=True)


def _allreduce_small(v, name="allreduce_small"):
    R, Wd = v.shape

    def body(v_ref, o_ref, buf, send_sems, recv_sems):
        x, y, c = lax.axis_index("x"), lax.axis_index("y"), lax.axis_index("c")
        me = 4 * x + 2 * y + c
        buf[me] = v_ref[...]
        sends = []
        for k in range(1, 8):
            peer = ((x + (k >> 2)) % 2, (y + ((k >> 1) & 1)) % 2, (c + (k & 1)) % 2)
            sends.append(pltpu.make_async_remote_copy(
                src_ref=v_ref, dst_ref=buf.at[me], send_sem=send_sems.at[k - 1], recv_sem=recv_sems.at[k - 1],
                device_id=peer, device_id_type=MESH))
        for cp in sends:
            cp.start()
        for k in range(1, 8):
            px, py, pc = (x + (k >> 2)) % 2, (y + ((k >> 1) & 1)) % 2, (c + (k & 1)) % 2
            pltpu.make_async_remote_copy(
                src_ref=v_ref, dst_ref=buf.at[4 * px + 2 * py + pc], send_sem=send_sems.at[k - 1],
                recv_sem=recv_sems.at[k - 1], device_id=(x, y, c), device_id_type=MESH).wait_recv()
        acc = buf[0]
        for d in range(1, 8):
            acc = acc + buf[d]
        o_ref[...] = acc
        for cp in sends:
            cp.wait_send()

    vm = pl.BlockSpec(memory_space=pltpu.VMEM)
    return pl.pallas_call(
        body, name=name, in_specs=[vm], out_specs=vm,
        out_shape=jax.ShapeDtypeStruct((R, Wd), F32),
        scratch_shapes=[pltpu.VMEM((8, R, Wd), F32), pltpu.SemaphoreType.DMA((7,)), pltpu.SemaphoreType.DMA((7,))],
    )(v)


SMALL = (
    ("mix_norm_g", (2, 1024)), ("xattn_norm_g", (2, 1024)), ("mem_norm_g", (2, 1024)),
    ("ffn_norm_g", (2, 1024)), ("final_norm_g", (1024,)),
    ("forget_bias", (2, 8)), ("sink", (2, 8)), ("rel_bias", (32, 8)),
)
SMALL_AND_CONV = SMALL + (("conv_w", (2, 3, 512)),)


def _small_rows(spec):
    rows = sum(int(np.prod(s)) // 128 if s[-1] % 128 == 0 else s[0] for _, s in spec)
    return -(-rows // 8) * 8


def _pack_small(vals, spec=SMALL):
    rows = []
    for name, shape in spec:
        v = vals[name].astype(F32)
        if shape[-1] % 128 == 0:
            rows.append(v.reshape(-1, 128))
        else:
            rows.append(jnp.pad(v, ((0, 0), (0, 120))))
    rows = jnp.concatenate(rows, axis=0)
    return jnp.pad(rows, ((0, _small_rows(spec) - rows.shape[0]), (0, 0)))


def _unpack_small(pack, spec=SMALL):
    out, off = {}, 0
    for name, shape in spec:
        if shape[-1] % 128 == 0:
            n = int(np.prod(shape)) // 128
            out[name] = pack[off:off + n].reshape(shape)
        else:
            n = shape[0]
            out[name] = pack[off:off + n, 0:8]
        off += n
    return out


W_IN_PERM = ((3848, 6920), (0, 3072), (3080, 3848), (3072, 3080))


def _perm_w_in(w):
    parts = [w[:, a:b] for a, b in W_IN_PERM]
    return jnp.concatenate(parts + [jnp.zeros((w.shape[0], PROJ_PAD - IN_COLS), w.dtype)], axis=1)


def _unperm_w_in(p):
    return jnp.concatenate([p[:, 3072:6144], p[:, 6912:6920], p[:, 6144:6912], p[:, 0:3072]], axis=1)


def _pad_row8(v):
    return jnp.pad(v.astype(F32).reshape(1, 8), ((0, 0), (0, 120)))


def _local_step(x, mem, tgt, W, rel_bias):
    bucket = jnp.asarray(_bucket_table())
    bias = _swa_bias(rel_bias, bucket, "swa_bias")
    saved = []
    for l in range(DEPTH):
        n = "l%d_" % l
        s = {"x0": x}
        wcat = W["w_in_p"][l]
        h = _rms_fwd(x, W["mix_norm_g"][l:l + 1], n + "mix_norm")
        pm = _mm(h, wcat[:, :PROJ_MAIN], "nn", BF16, n + "proj", bn=768)
        fg = _mm(h, wcat[:, PROJ_MAIN:], "nn", F32, n + "proj_fg")
        fb = _pad_row8(W["forget_bias"][l])
        c_col = _fox_gate_fwd(fg, fb, n + "fox_gate")
        c_row = c_col[:, 0:8].T
        cw = jnp.pad(W["conv_w"][l], ((0, 5), (0, 0)))
        y_conv = _conv_fwd(pm, cw, n + "conv")
        y_fox, lse = _fox2_fwd(pm, c_row, n + "fox")
        sink = _pad_row8(W["sink"][l])
        y_swa, mlse = _swa_fwd(pm, bias, sink, n + "swa")
        ys = (y_conv, y_fox, y_swa)
        us = tuple(_mm(ys[b], W["w_branch"][l][b], "nn", BF16, n + "branch%d" % b) for b in range(3))
        merged = _merge_fwd(pm, us, n + "merge")
        x1 = _mm(merged, W["w_mix_out"][l], "nn", F32, n + "mix_out", res=x)
        xn1 = _rms_fwd(x1, W["xattn_norm_g"][l:l + 1], n + "xattn_norm")
        memn = _rms_fwd(mem, W["mem_norm_g"][l:l + 1], n + "mem_norm")
        qx = _mm(xn1, W["w_xq"][l], "nn", BF16, n + "xq")
        kv = _mm(memn, W["w_xkv"][l], "nn", BF16, n + "xkv")
        ox = _xattn_fwd(qx, kv, n + "xattn")
        x2 = _mm(ox, W["w_xo"][l], "nn", F32, n + "xo", res=x1)
        xn2 = _rms_fwd(x2, W["ffn_norm_g"][l:l + 1], n + "ffn_norm")
        ab = _mm(xn2, W["w_gu"][l], "nt", BF16, n + "ffn_in", bn=512)
        hm = _swiglu_fwd(ab, n + "swiglu")
        x3 = _mm(hm, W["w_ffn_down"][l], "nn", F32, n + "ffn_out", res=x2, bk=1408)
        s.update(h=h, pm=pm, fg=fg, fb=fb, c_col=c_col, c_row=c_row, cw=cw, ys=ys, lse=lse, sink=sink,
                 mlse=mlse, us=us, merged=merged, x1=x1, xn1=xn1, memn=memn, qx=qx, kv=kv, ox=ox,
                 x2=x2, xn2=xn2, ab=ab, hm=hm)
        saved.append(s)
        x = x3

    loss_row, dx, dg_final = _final_loss(x, W["final_norm_g"].reshape(1, D_MODEL), tgt, "final_loss")
    G = {name: [None] * DEPTH for name in
         ("mix_norm_g", "w_in_p", "forget_bias", "conv_w", "sink", "w_branch", "w_mix_out", "xattn_norm_g",
          "mem_norm_g", "w_xq", "w_xkv", "w_xo", "ffn_norm_g", "w_gu", "w_ffn_down")}
    dbias_tot = None
    for l in reversed(range(DEPTH)):
        n = "l%d_" % l
        s = saved[l]
        dhm = _mm(dx, W["w_ffn_down"][l], "nt", BF16, n + "d_hm", bn=1408)
        G["w_ffn_down"][l] = _mm(s["hm"], dx, "tn", BF16, n + "dw_down", bm=1408, bk=1024)
        dab = _swiglu_bwd(s["ab"], dhm, n + "d_swiglu")
        dxn2 = _mm(dab, W["w_gu"][l], "nn", BF16, n + "d_xn2", bk=1408)
        G["w_gu"][l] = _mm(dab, s["xn2"], "tn", BF16, n + "dw_gu", bm=512, bk=2048)
        dx, G["ffn_norm_g"][l] = _rms_bwd(s["x2"], W["ffn_norm_g"][l:l + 1], dxn2, dx, n + "d_ffn_norm")
        dox = _mm(dx, W["w_xo"][l], "nt", BF16, n + "d_ox")
        G["w_xo"][l] = _mm(s["ox"], dx, "tn", BF16, n + "dw_xo", bk=1024)
        dqx, dkv = _xattn_bwd(s["qx"], s["kv"], dox, n + "d_xattn")
        dxn1 = _mm(dqx, W["w_xq"][l], "nt", BF16, n + "d_xn1")
        G["w_xq"][l] = _mm(s["xn1"], dqx, "tn", BF16, n + "dw_xq", bk=2048)
        dmemn = _mm(dkv, W["w_xkv"][l], "nt", BF16, n + "d_memn")
        G["w_xkv"][l] = _mm(s["memn"], dkv, "tn", BF16, n + "dw_xkv")
        _, G["mem_norm_g"][l] = _rms_bwd(mem, W["mem_norm_g"][l:l + 1], dmemn, None, n + "d_mem_norm")
        dx, G["xattn_norm_g"][l] = _rms_bwd(s["x1"], W["xattn_norm_g"][l:l + 1], dxn1, dx, n + "d_xattn_norm")
        dmerged = _mm(dx, W["w_mix_out"][l], "nt", BF16, n + "d_merged")
        G["w_mix_out"][l] = _mm(s["merged"], dx, "tn", BF16, n + "dw_mix_out", bk=1024)
        du0, du1, du2, dproj = _merge_bwd(s["pm"], s["us"], dmerged, n + "d_merge")
        dus = (du0, du1, du2)
        dys = [_mm(dus[b], W["w_branch"][l][b], "nt", BF16, n + "d_y%d" % b) for b in range(3)]
        G["w_branch"][l] = [_mm(s["ys"][b], dus[b], "tn", BF16, n + "dw_branch%d" % b, bk=2048) for b in range(3)]
        dproj, dcw = _conv_bwd(s["pm"], s["cw"], dys[0], dproj, n + "d_conv")
        G["conv_w"][l] = dcw[0:3]
        delta = _fox_delta(s["ys"][1], dys[1], n + "fox_delta")
        dproj, delta = _fox2_bwd_dq(s["pm"], dys[1], s["c_row"], s["lse"], delta, dproj, n + "d_fox_q")
        dproj, dc = _fox2_bwd_dkv(s["pm"], dys[1], s["c_col"], s["lse"][:, 0:8].T, delta[:, 0:8].T, dproj,
                                  n + "d_fox_kv")
        dfg, dfb = _fox_gate_bwd(dc, s["fg"], s["fb"], n + "d_fox_gate")
        G["forget_bias"][l] = dfb[0, 0:8]
        dproj, dkc, dkp, dvc, dvp, dbias, dsink = _swa_bwd(s["pm"], bias, s["sink"], dys[2], s["mlse"], dproj,
                                                         n + "d_swa")
        G["sink"][l] = dsink[0, 0:8]
        dbias_tot = dbias if dbias_tot is None else dbias_tot + dbias
        zpad = jnp.zeros((WINDOW, 128), F32)
        dsk = dkc + jnp.concatenate([dkp[WINDOW:], zpad], axis=0)
        dsv = dvc + jnp.concatenate([dvp[WINDOW:], zpad], axis=0)
        tail = jnp.concatenate([dsk.astype(BF16), dsv.astype(BF16), dfg.astype(BF16)], axis=1)
        dproj = lax.dynamic_update_slice(dproj, tail, (0, PROJ_MAIN - 256))
        dh = _mm(dproj, W["w_in_p"][l], "nt", BF16, n + "d_h", bk=1408)
        G["w_in_p"][l] = _mm(s["h"], dproj, "tn", BF16, n + "dw_in", bn=1408, bk=2048)
        dx, G["mix_norm_g"][l] = _rms_bwd(s["x0"], W["mix_norm_g"][l:l + 1], dh, dx, n + "d_mix_norm")
    drb = _swa_dbias_reduce(dbias_tot, bucket, "swa_dbias")
    G["rel_bias"] = drb[:, 0:8]
    G["final_norm_g"] = dg_final.reshape(D_MODEL)
    return loss_row, dx, G


def kernel(x, mem, mix_norm_g, w_in, forget_bias, conv_w, sink, w_branch, w_mix_out, rel_bias, xattn_norm_g, mem_norm_g, w_xq, w_xkv, w_xo, ffn_norm_g, w_ffn_gate, w_ffn_up, w_ffn_down, final_norm_g, loss_target, m_mix_norm_g, m_w_in, m_forget_bias, m_conv_w, m_sink, m_w_branch, m_w_mix_out, m_rel_bias, m_xattn_norm_g, m_mem_norm_g, m_w_xq, m_w_xkv, m_w_xo, m_ffn_norm_g, m_w_ffn_gate, m_w_ffn_up, m_w_ffn_down, m_final_norm_g, v_mix_norm_g, v_w_in, v_forget_bias, v_conv_w, v_sink, v_w_branch, v_w_mix_out, v_rel_bias, v_xattn_norm_g, v_mem_norm_g, v_w_xq, v_w_xkv, v_w_xo, v_ffn_norm_g, v_w_ffn_gate, v_w_ffn_up, v_w_ffn_down, v_final_norm_g):
    order = ("mix_norm_g", "w_in", "forget_bias", "conv_w", "sink", "w_branch", "w_mix_out", "rel_bias",
             "xattn_norm_g", "mem_norm_g", "w_xq", "w_xkv", "w_xo", "ffn_norm_g", "w_ffn_gate", "w_ffn_up",
             "w_ffn_down", "final_norm_g")
    w_sh = dict(zip(order, (mix_norm_g, w_in, forget_bias, conv_w, sink, w_branch, w_mix_out, rel_bias,
                            xattn_norm_g, mem_norm_g, w_xq, w_xkv, w_xo, ffn_norm_g, w_ffn_gate, w_ffn_up,
                            w_ffn_down, final_norm_g)))
    m_sh = dict(zip(order, (m_mix_norm_g, m_w_in, m_forget_bias, m_conv_w, m_sink, m_w_branch, m_w_mix_out,
                            m_rel_bias, m_xattn_norm_g, m_mem_norm_g, m_w_xq, m_w_xkv, m_w_xo, m_ffn_norm_g,
                            m_w_ffn_gate, m_w_ffn_up, m_w_ffn_down, m_final_norm_g)))
    v_sh = dict(zip(order, (v_mix_norm_g, v_w_in, v_forget_bias, v_conv_w, v_sink, v_w_branch, v_w_mix_out,
                            v_rel_bias, v_xattn_norm_g, v_mem_norm_g, v_w_xq, v_w_xkv, v_w_xo, v_ffn_norm_g,
                            v_w_ffn_gate, v_w_ffn_up, v_w_ffn_down, v_final_norm_g)))

    xi, yi, ci = lax.axis_index("x"), lax.axis_index("y"), lax.axis_index("c")
    as_idx = lambda *v: jnp.stack([jnp.asarray(t, I32) for t in v])
    me = 2 * xi + yi
    big = [name for name, _ in BIG]
    two_d = dict(BIG)
    two_d["conv_w"] = (6, 128)

    def slab(a, name):
        return (jnp.swapaxes(a, 1, 2) if name in TRANSPOSED else a).reshape(two_d[name])

    def unslab(a, name):
        shape = w_sh[name].shape
        if name in TRANSPOSED:
            return jnp.swapaxes(a.reshape(shape[0], shape[2], shape[1]), 1, 2)
        return a.reshape(shape)

    gathered = dict(zip(big, _ag_ring_multi(
        [_cast_place(slab(w_sh[name], name), as_idx(me), "place_" + name) for name in big])))
    conv_part = lax.dynamic_update_slice_in_dim(jnp.zeros((DEPTH, 3, BRANCH), F32), 0.5 * conv_w, 128 * me, axis=2)
    conv_full = _allreduce_small(conv_part.reshape(-1, 128), "allgather_conv").reshape(DEPTH, 3, BRANCH)

    def lay(name, l):
        g = gathered[name]
        return g.reshape(4, DEPTH, g.shape[1] // DEPTH, g.shape[2])[:, l]

    def by_cols(name, l):
        g = lay(name, l)
        return jnp.moveaxis(g, 0, 1).reshape(g.shape[1], 4 * g.shape[2])

    def by_rows(name, l):
        g = lay(name, l)
        return g.reshape(4 * g.shape[1], g.shape[2])

    W = {k: w_sh[k] for k in ("mix_norm_g", "forget_bias", "sink", "xattn_norm_g", "mem_norm_g",
                              "ffn_norm_g", "final_norm_g")}
    W["conv_w"] = conv_full
    W["w_in_p"] = [_perm_w_in(by_cols("w_in", l)) for l in range(DEPTH)]
    W["w_gu"] = [jnp.concatenate([by_rows("w_ffn_gate", l), by_rows("w_ffn_up", l)], axis=0) for l in range(DEPTH)]
    W["w_xkv"] = [by_cols("w_xkv", l) for l in range(DEPTH)]
    W["w_branch"] = [[jnp.moveaxis(lay("w_branch", l)[:, BRANCH * b:BRANCH * (b + 1)], 0, 1).reshape(BRANCH, D_MODEL)
                      for b in range(3)] for l in range(DEPTH)]
    for k in ("w_mix_out", "w_xq", "w_xo", "w_ffn_down"):
        W[k] = [by_rows(k, l) for l in range(DEPTH)]
    loss_row, dx, G = _local_step(x[0], mem[0], loss_target[0], W, rel_bias)

    def to_cols(g):
        return jnp.moveaxis(g.reshape(g.shape[0], 4, g.shape[1] // 4), 1, 0)

    def to_rows(g):
        return g.reshape(4, g.shape[0] // 4, g.shape[1])

    per_layer = {
        "w_in": [to_cols(_unperm_w_in(G["w_in_p"][l])) for l in range(DEPTH)],
        "w_branch": [jnp.concatenate([to_cols(g) for g in G["w_branch"][l]], axis=1) for l in range(DEPTH)],
        "w_mix_out": [to_rows(g) for g in G["w_mix_out"]],
        "w_xq": [to_rows(g) for g in G["w_xq"]],
        "w_xkv": [to_cols(g) for g in G["w_xkv"]],
        "w_xo": [to_rows(g) for g in G["w_xo"]],
        "w_ffn_gate": [to_rows(G["w_gu"][l][:D_FF]) for l in range(DEPTH)],
        "w_ffn_up": [to_rows(G["w_gu"][l][D_FF:]) for l in range(DEPTH)],
        "w_ffn_down": [to_rows(g) for g in G["w_ffn_down"]],
    }
    g4 = [jnp.concatenate(per_layer[name], axis=1).astype(BF16) for name in big]
    sib = _rs_sibling_multi(g4)
    pair = [_rs_add_pair(g4[t], sib[t], as_idx(ci), "_" + big[t]) for t in range(len(big))]
    diag = _rs_diag_multi(pair)
    nbrs = as_idx(2 * (1 - xi) + yi, 2 * xi + (1 - yi))
    merged = [_rs_merge(pair[t], diag[t], nbrs, "_" + big[t]) for t in range(len(big))]
    got = _rs_direct_multi(merged)
    reduced = _rs_share_multi([_rs_final(pair[t], got[t], as_idx(me, ci), "_" + big[t]) for t in range(len(big))])

    small = _unpack_small(_allreduce_small(_pack_small({
        "mix_norm_g": jnp.concatenate(G["mix_norm_g"], axis=0),
        "xattn_norm_g": jnp.concatenate(G["xattn_norm_g"], axis=0),
        "mem_norm_g": jnp.concatenate(G["mem_norm_g"], axis=0),
        "ffn_norm_g": jnp.concatenate(G["ffn_norm_g"], axis=0),
        "final_norm_g": G["final_norm_g"],
        "forget_bias": jnp.stack(G["forget_bias"]),
        "sink": jnp.stack(G["sink"]),
        "rel_bias": G["rel_bias"],
        "conv_w": jnp.stack(G["conv_w"]),
    }, SMALL_AND_CONV)), SMALL_AND_CONV)
    grads = {name: unslab(reduced[t], name) for t, name in enumerate(big)}
    grads.update(small)
    grads["conv_w"] = lax.dynamic_slice_in_dim(small["conv_w"], 128 * me, 128, axis=2)

    sm_names = [name for name, _ in SMALL]
    sd, sm_, sv_ = _adamw(_pack_small({k: w_sh[k] for k in sm_names}), _pack_small({k: grads[k] for k in sm_names}),
                          _pack_small({k: m_sh[k] for k in sm_names}), _pack_small({k: v_sh[k] for k in sm_names}),
                          "adamw_small")
    delta, new_m, new_v = _unpack_small(sd), _unpack_small(sm_), _unpack_small(sv_)
    for t, name in enumerate(big + ["conv_w"]):
        if name == "w_in":
            to3 = lambda a: jnp.transpose(a, (2, 0, 1))
            d, nm, nv = _adamw(to3(w_sh[name]), to3(grads[name]), to3(m_sh[name]), to3(v_sh[name]), "adamw_w_in")
            delta[name], new_m[name], new_v[name] = (jnp.transpose(a, (1, 2, 0)) for a in (d, nm, nv))
            continue
        g2 = reduced[t] if t < len(big) else slab(grads[name], name)
        d, nm, nv = _adamw(slab(w_sh[name], name), g2, slab(m_sh[name], name), slab(v_sh[name], name), "adamw_" + name)
        delta[name], new_m[name], new_v[name] = unslab(d, name), unslab(nm, name), unslab(nv, name)

    loss = lax.psum(loss_row[0, 0], ("x", "y", "c"))
    return (loss, dx[None], *[grads[k] for k in order], *[delta[k] for k in order],
            *[new_m[k] for k in order], *[new_v[k] for k in order])
```

```python
import math

import numpy as np
import jax
import jax.numpy as jnp
from jax import lax
from jax.experimental import pallas as pl
from jax.experimental.pallas import tpu as pltpu

F32 = jnp.float32
BF16 = jnp.bfloat16
I32 = jnp.int32

D_MODEL = 1024
DEPTH = 2
HEAD_DIM = 64
BRANCH = 512
N_BUCKETS = 32
WINDOW = 128
MEM_LEN = 256
X_HEADS = 4
X_HEAD_DIM = 256
D_FF = 2816
IN_COLS = 6920
PROJ_MAIN = 6912
PROJ_PAD = 7040
RMS_EPS = 1e-6
NEG = -1e30
ATT_SCALE = 0.125
X_SCALE = 0.0625

ADAM_LR = 0.001
ADAM_B1 = 0.9
ADAM_B2 = 0.999
ADAM_EPS = 1e-08
ADAM_WD = 0.01
ADAM_STEP = 10

VMEM_LIMIT = 48 * 1024 * 1024
MESH = pl.DeviceIdType.MESH

CB_GATE = (0, 1, 2)
CB_B, CB_C, CB_U, CB_FQ, CB_FK, CB_FV, CB_SQ = 6, 7, 8, 9, 10, 11, 12
CB_SK, CB_SV = 52, 53


def _cp(sem):
    return pltpu.CompilerParams(dimension_semantics=sem, vmem_limit_bytes=VMEM_LIMIT)


def _pick(n, prefs):
    for p in prefs:
        if p <= n and n % p == 0:
            return p
    return n


def _dot(a, b, dims):
    return lax.dot_general(a, b, (dims, ((), ())), preferred_element_type=F32)


def _dot_nn(a, b):
    return _dot(a, b, ((1,), (0,)))


def _dot_nt(a, b):
    return _dot(a, b, ((1,), (1,)))


def _dot_tn(a, b):
    return _dot(a, b, ((0,), (0,)))


def _mm(a, b, mode, out_dtype, name, res=None, bm=1024, bn=1024, bk=1024):
    if mode == "nn":
        (M, K), (K2, N) = a.shape, b.shape
    elif mode == "nt":
        (M, K), (N, K2) = a.shape, b.shape
    else:
        (K, M), (K2, N) = a.shape, b.shape
    assert K == K2, (name, a.shape, b.shape)
    bm = _pick(M, (bm, 1024, 512, 256, 128))
    bn = _pick(N, (bn, 1024, 768, 640, 512, 384, 256, 128))
    bk = _pick(K, (bk, 1024, 768, 640, 512, 384, 256, 128))
    nk = K // bk
    if mode == "tn":
        a_spec = pl.BlockSpec((bk, bm), lambda i, j, k: (k, i))
    else:
        a_spec = pl.BlockSpec((bm, bk), lambda i, j, k: (i, k))
    if mode == "nt":
        b_spec = pl.BlockSpec((bn, bk), lambda i, j, k: (j, k))
    else:
        b_spec = pl.BlockSpec((bk, bn), lambda i, j, k: (k, j))
    dims = {"nn": ((1,), (0,)), "nt": ((1,), (1,)), "tn": ((0,), (0,))}[mode]
    o_spec = pl.BlockSpec((bm, bn), lambda i, j, k: (i, j))
    has_res = res is not None

    def body(*refs):
        if has_res:
            a_ref, b_ref, r_ref, o_ref = refs[:4]
            scr = refs[4:]
        else:
            a_ref, b_ref, o_ref = refs[:3]
            r_ref = None
            scr = refs[3:]
        p = _dot(a_ref[...].astype(BF16), b_ref[...].astype(BF16), dims)
        if nk == 1:
            if has_res:
                p = p + r_ref[...]
            o_ref[...] = p.astype(out_dtype)
        else:
            acc = scr[0]
            k = pl.program_id(2)

            @pl.when(k == 0)
            def _():
                acc[...] = p

            @pl.when(k > 0)
            def _():
                acc[...] += p

            @pl.when(k == nk - 1)
            def _():
                r = acc[...]
                if has_res:
                    r = r + r_ref[...]
                o_ref[...] = r.astype(out_dtype)

    ins = [a, b] + ([res] if has_res else [])
    in_specs = [a_spec, b_spec] + ([o_spec] if has_res else [])
    return pl.pallas_call(
        body, name=name, grid=(M // bm, N // bn, nk),
        in_specs=in_specs, out_specs=o_spec,
        out_shape=jax.ShapeDtypeStruct((M, N), out_dtype),
        scratch_shapes=[pltpu.VMEM((bm, bn), F32)] if nk > 1 else [],
        compiler_params=_cp(("parallel", "parallel", "arbitrary")),
    )(*ins)


def _rms_fwd(x, g, name):
    T, Dm = x.shape
    bt = _pick(T, (512, 256))

    def body(x_ref, g_ref, o_ref):
        xv = x_ref[...]
        r = lax.rsqrt(jnp.mean(xv * xv, axis=-1, keepdims=True) + RMS_EPS)
        o_ref[...] = ((xv * r) * g_ref[...]).astype(BF16)

    return pl.pallas_call(
        body, name=name, grid=(T // bt,),
        in_specs=[pl.BlockSpec((bt, Dm), lambda i: (i, 0)), pl.BlockSpec((1, Dm), lambda i: (0, 0))],
        out_specs=pl.BlockSpec((bt, Dm), lambda i: (i, 0)),
        out_shape=jax.ShapeDtypeStruct((T, Dm), BF16),
        compiler_params=_cp(("parallel",)),
    )(x, g)


def _rms_bwd(x, g, dh, dres, name):
    T, Dm = x.shape
    bt = _pick(T, (512, 256))
    want_dx = dres is not None

    def body(*refs):
        if want_dx:
            x_ref, g_ref, dh_ref, dr_ref, dx_ref, dxb_ref, dg_ref = refs
        else:
            x_ref, g_ref, dh_ref, dg_ref = refs
        xv = x_ref[...]
        r = lax.rsqrt(jnp.mean(xv * xv, axis=-1, keepdims=True) + RMS_EPS)
        xh = xv * r
        dhv = dh_ref[...].astype(F32)

        @pl.when(pl.program_id(0) == 0)
        def _():
            dg_ref[...] = jnp.zeros_like(dg_ref)

        dg_ref[...] += jnp.sum(dhv * xh, axis=0, keepdims=True)
        if want_dx:
            dyg = dhv * g_ref[...]
            dxv = dr_ref[...] + r * (dyg - xh * jnp.mean(dyg * xh, axis=-1, keepdims=True))
            dx_ref[...] = dxv
            dxb_ref[...] = dxv.astype(BF16)

    row = pl.BlockSpec((bt, Dm), lambda i: (i, 0))
    vec = pl.BlockSpec((1, Dm), lambda i: (0, 0))
    if want_dx:
        return pl.pallas_call(
            body, name=name, grid=(T // bt,),
            in_specs=[row, vec, row, row], out_specs=[row, row, vec],
            out_shape=[jax.ShapeDtypeStruct((T, Dm), F32), jax.ShapeDtypeStruct((T, Dm), BF16),
                       jax.ShapeDtypeStruct((1, Dm), F32)],
            compiler_params=_cp(("arbitrary",)),
        )(x, g, dh, dres)
    return None, None, pl.pallas_call(
        body, name=name, grid=(T // bt,),
        in_specs=[row, vec, row], out_specs=vec,
        out_shape=jax.ShapeDtypeStruct((1, Dm), F32),
        compiler_params=_cp(("arbitrary",)),
    )(x, g, dh)


def _final_loss(x, g, tgt, name):
    T, Dm = x.shape
    bt = _pick(T, (512, 256))

    def body(x_ref, g_ref, t_ref, loss_ref, dx_ref, dxb_ref, dg_ref):
        xv = x_ref[...]
        r = lax.rsqrt(jnp.mean(xv * xv, axis=-1, keepdims=True) + RMS_EPS)
        xh = xv * r
        gv = g_ref[...]
        err = xh * gv - t_ref[...]

        @pl.when(pl.program_id(0) == 0)
        def _():
            dg_ref[...] = jnp.zeros_like(dg_ref)
            loss_ref[...] = jnp.zeros_like(loss_ref)

        loss_ref[...] += jnp.sum(err * err) * (0.5 / Dm)
        dy = err * (1.0 / Dm)
        dg_ref[...] += jnp.sum(dy * xh, axis=0, keepdims=True)
        dyg = dy * gv
        dxv = r * (dyg - xh * jnp.mean(dyg * xh, axis=-1, keepdims=True))
        dx_ref[...] = dxv
        dxb_ref[...] = dxv.astype(BF16)

    row = pl.BlockSpec((bt, Dm), lambda i: (i, 0))
    vec = pl.BlockSpec((1, Dm), lambda i: (0, 0))
    return pl.pallas_call(
        body, name=name, grid=(T // bt,),
        in_specs=[row, vec, row],
        out_specs=[pl.BlockSpec((1, 128), lambda i: (0, 0)), row, row, vec],
        out_shape=[jax.ShapeDtypeStruct((1, 128), F32), jax.ShapeDtypeStruct((T, Dm), F32),
                   jax.ShapeDtypeStruct((T, Dm), BF16), jax.ShapeDtypeStruct((1, Dm), F32)],
        compiler_params=_cp(("arbitrary",)),
    )(x, g, tgt)


HALO = 16


def _shift_down(z, zprev, s):
    rolled = pltpu.roll(z, s, 0)
    hp = pltpu.roll(zprev, s, 0)
    row = lax.broadcasted_iota(I32, hp.shape, 0)
    top = jnp.where(row < s, hp, rolled[:HALO])
    return jnp.concatenate([top, rolled[HALO:]], axis=0)


def _shift_up(z, znext, s):
    n = z.shape[0]
    rolled = pltpu.roll(z, n - s, 0)
    hn = pltpu.roll(znext, HALO - s, 0)
    row = lax.broadcasted_iota(I32, hn.shape, 0)
    bot = jnp.where(row >= HALO - s, hn, rolled[n - HALO:])
    return jnp.concatenate([rolled[:n - HALO], bot], axis=0)


def _conv_fwd(pm, cw, name):
    T = pm.shape[0]
    bt = _pick(T, (512, 256))
    hb = bt // HALO

    def body(b_ref, c_ref, u_ref, cp_ref, up_ref, w_ref, o_ref):
        i = pl.program_id(0)
        z = c_ref[...].astype(F32) * u_ref[...].astype(F32)
        zp = cp_ref[...].astype(F32) * up_ref[...].astype(F32)
        zp = jnp.where(i > 0, zp, 0.0)
        w = w_ref[...]
        y = w[2:3] * z + w[1:2] * _shift_down(z, zp, 1) + w[0:1] * _shift_down(z, zp, 2)
        o_ref[...] = (b_ref[...].astype(F32) * y).astype(BF16)

    def col(cb):
        return pl.BlockSpec((bt, BRANCH), lambda i: (i, cb))

    def prev(cb):
        return pl.BlockSpec((HALO, BRANCH), lambda i: (jnp.maximum(i * hb - 1, 0), cb))

    return pl.pallas_call(
        body, name=name, grid=(T // bt,),
        in_specs=[col(CB_B), col(CB_C), col(CB_U), prev(CB_C), prev(CB_U),
                  pl.BlockSpec((8, BRANCH), lambda i: (0, 0))],
        out_specs=pl.BlockSpec((bt, BRANCH), lambda i: (i, 0)),
        out_shape=jax.ShapeDtypeStruct((T, BRANCH), BF16),
        compiler_params=_cp(("parallel",)),
    )(pm, pm, pm, pm, pm, cw)


def _conv_bwd(pm, cw, dy, dproj, name):
    T = pm.shape[0]
    bt = _pick(T, (512, 256))
    hb = bt // HALO
    nb = T // bt
    last_h = T // HALO - 1

    def body(b_ref, c_ref, u_ref, cp_ref, up_ref, bn_ref, dy_ref, dyn_ref, w_ref, buf_ref,
             dp_ref, dw_ref):
        del buf_ref
        db_ref = dp_ref.at[:, 0:BRANCH]
        dc_ref = dp_ref.at[:, BRANCH:2 * BRANCH]
        du_ref = dp_ref.at[:, 2 * BRANCH:3 * BRANCH]
        i = pl.program_id(0)
        cv = c_ref[...].astype(F32)
        uv = u_ref[...].astype(F32)
        bv = b_ref[...].astype(F32)
        z = cv * uv
        zp = jnp.where(i > 0, cp_ref[...].astype(F32) * up_ref[...].astype(F32), 0.0)
        w = w_ref[...]
        z1 = _shift_down(z, zp, 1)
        z2 = _shift_down(z, zp, 2)
        yc = w[2:3] * z + w[1:2] * z1 + w[0:1] * z2
        dyv = dy_ref[...].astype(F32)
        db_ref[...] = (dyv * yc).astype(BF16)
        g = dyv * bv
        gn = jnp.where(i < nb - 1, dyn_ref[...].astype(F32) * bn_ref[...].astype(F32), 0.0)
        dz = w[2:3] * g + w[1:2] * _shift_up(g, gn, 1) + w[0:1] * _shift_up(g, gn, 2)
        dc_ref[...] = (dz * uv).astype(BF16)
        du_ref[...] = (dz * cv).astype(BF16)

        @pl.when(i == 0)
        def _():
            dw_ref[...] = jnp.zeros_like(dw_ref)

        dw_ref[0:1, :] += jnp.sum(g * z2, axis=0, keepdims=True)
        dw_ref[1:2, :] += jnp.sum(g * z1, axis=0, keepdims=True)
        dw_ref[2:3, :] += jnp.sum(g * z, axis=0, keepdims=True)

    def col(cb):
        return pl.BlockSpec((bt, BRANCH), lambda i: (i, cb))

    def prev(cb):
        return pl.BlockSpec((HALO, BRANCH), lambda i: (jnp.maximum(i * hb - 1, 0), cb))

    def nxt(cb):
        return pl.BlockSpec((HALO, BRANCH), lambda i: (jnp.minimum((i + 1) * hb, last_h), cb))

    own = pl.BlockSpec((bt, BRANCH), lambda i: (i, 0))
    w_spec = pl.BlockSpec((8, BRANCH), lambda i: (0, 0))
    return pl.pallas_call(
        body, name=name, grid=(nb,),
        in_specs=[col(CB_B), col(CB_C), col(CB_U), prev(CB_C), prev(CB_U), nxt(CB_B), own,
                  pl.BlockSpec((HALO, BRANCH), lambda i: (jnp.minimum((i + 1) * hb, last_h), 0)), w_spec,
                  pl.BlockSpec(memory_space=pl.ANY)],
        out_specs=[pl.BlockSpec((bt, 3 * BRANCH), lambda i: (i, 2)), w_spec],
        out_shape=[jax.ShapeDtypeStruct(dproj.shape, dproj.dtype), jax.ShapeDtypeStruct((8, BRANCH), F32)],
        input_output_aliases={9: 0},
        compiler_params=_cp(("arbitrary",)),
    )(pm, pm, pm, pm, pm, pm, dy, dy, cw, dproj)


def _log_sigmoid(z):
    return jnp.minimum(z, 0.0) - jnp.log(1.0 + jnp.exp(-jnp.abs(z)))


def _fox_gate_fwd(fg, fb, name):
    T = fg.shape[0]
    bt = _pick(T, (256,))

    def body(f_ref, b_ref, c_ref, carry):
        @pl.when(pl.program_id(0) == 0)
        def _():
            carry[...] = jnp.zeros_like(carry)

        xv = _log_sigmoid(f_ref[...] + b_ref[...])
        row = lax.broadcasted_iota(I32, xv.shape, 0)
        s = 1
        while s < bt:
            xv = xv + jnp.where(row >= s, pltpu.roll(xv, s, 0), 0.0)
            s *= 2
        xv = xv + carry[...]
        c_ref[...] = xv
        carry[...] = xv[bt - 1:bt, :]

    blk = pl.BlockSpec((bt, 128), lambda i: (i, 0))
    return pl.pallas_call(
        body, name=name, grid=(T // bt,),
        in_specs=[blk, pl.BlockSpec((1, 128), lambda i: (0, 0))],
        out_specs=blk, out_shape=jax.ShapeDtypeStruct((T, 128), F32),
        scratch_shapes=[pltpu.VMEM((1, 128), F32)],
        compiler_params=_cp(("arbitrary",)),
    )(fg, fb)


def _fox_gate_bwd(dc, fg, fb, name):
    T = fg.shape[0]
    bt = _pick(T, (256,))
    nb = T // bt

    def body(d_ref, f_ref, b_ref, o_ref, db_ref, carry):
        @pl.when(pl.program_id(0) == 0)
        def _():
            carry[...] = jnp.zeros_like(carry)
            db_ref[...] = jnp.zeros_like(db_ref)

        xv = d_ref[...]
        row = lax.broadcasted_iota(I32, xv.shape, 0)
        s = 1
        while s < bt:
            xv = xv + jnp.where(row < bt - s, pltpu.roll(xv, bt - s, 0), 0.0)
            s *= 2
        xv = xv + carry[...]
        carry[...] = xv[0:1, :]
        z = f_ref[...] + b_ref[...]
        dz = xv * (1.0 / (1.0 + jnp.exp(z)))
        o_ref[...] = dz
        db_ref[...] += jnp.sum(dz, axis=0, keepdims=True)

    blk = pl.BlockSpec((bt, 128), lambda i: (nb - 1 - i, 0))
    vec = pl.BlockSpec((1, 128), lambda i: (0, 0))
    return pl.pallas_call(
        body, name=name, grid=(nb,),
        in_specs=[blk, blk, vec], out_specs=[blk, vec],
        out_shape=[jax.ShapeDtypeStruct((T, 128), F32), jax.ShapeDtypeStruct((1, 128), F32)],
        scratch_shapes=[pltpu.VMEM((1, 128), F32)],
        compiler_params=_cp(("arbitrary",)),
    )(dc, fg, fb)


def _lane_lo(shape):
    return lax.broadcasted_iota(I32, shape, 1) < HEAD_DIM


def _put_col(shape, h, col):
    lane = lax.broadcasted_iota(I32, shape, 1)
    return jnp.where(lane == h, col, 0.0)


def _fox_delta(o, do, name):
    T = o.shape[0]
    bt = _pick(T, (512, 256))

    def body(o_ref, d_ref, out_ref):
        prod = o_ref[...].astype(F32) * d_ref[...].astype(F32)
        out = jnp.zeros((bt, 128), F32)
        for h in range(8):
            out = out + _put_col((bt, 128), h, jnp.sum(prod[:, 64 * h:64 * h + 64], axis=-1, keepdims=True))
        out_ref[...] = out

    blk = pl.BlockSpec((bt, BRANCH), lambda i: (i, 0))
    return pl.pallas_call(
        body, name=name, grid=(T // bt,), in_specs=[blk, blk],
        out_specs=pl.BlockSpec((bt, 128), lambda i: (i, 0)),
        out_shape=jax.ShapeDtypeStruct((T, 128), F32),
        compiler_params=_cp(("parallel",)),
    )(o, do)


FOX_ROWS = 32


def _chunk_loop(n, chunk):
    for r in range(n):
        chunk(r)


def _tree(op, xs):
    xs = list(xs)
    while len(xs) > 1:
        xs = [op(xs[i], xs[i + 1]) if i + 1 < len(xs) else xs[i] for i in range(0, len(xs), 2)]
    return xs[0]


def _masked_halves(t):
    lo = _lane_lo(t.shape)
    z = jnp.zeros_like(t)
    return jnp.where(lo, t, z), jnp.where(lo, z, t)


def _fox2_fwd(pm, c_row, name):
    T = pm.shape[0]
    bq = _pick(T, (512, 256))
    bk = bq
    nq = T // bq
    R = FOX_ROWS
    ng = bk // 128

    def body(q_ref, k_ref, v_ref, ck_ref, o_ref, lse_ref, acc, m_s, l_s, a_s, s_scr, p_scr):
        qi = pl.program_id(0)
        ki = pl.program_id(1)

        @pl.when(ki == 0)
        def _():
            acc[...] = jnp.zeros_like(acc)
            m_s[...] = jnp.full_like(m_s, NEG)
            l_s[...] = jnp.zeros_like(l_s)

        def block(masked):
            qlo = _lane_lo((bq, 128))
            for p in range(4):
                sl = slice(128 * p, 128 * p + 128)
                qp = q_ref[:, sl] * ATT_SCALE
                vp = v_ref[:, sl]
                ks = _masked_halves(k_ref[:, sl])
                pvs = []
                for j in range(2):
                    h = 2 * p + j
                    s_scr[j] = _dot_nt(qp, ks[j])

                    def chunk(r, h=h, j=j):
                        r0 = r * R
                        rows = pl.ds(r0, R)
                        sc = [s_scr[j, rows, 128 * g:128 * g + 128] - ck_ref[h:h + 1, 128 * g:128 * g + 128]
                              for g in range(ng)]
                        if masked:
                            rid = lax.broadcasted_iota(I32, (R, 128), 0) + r0
                            cid = lax.broadcasted_iota(I32, (R, 128), 1)
                            sc = [jnp.where(cid + 128 * g <= rid, sc[g], NEG) for g in range(ng)]
                        m_old = m_s[h, rows, :]
                        m_new = jnp.maximum(m_old, jnp.max(_tree(jnp.maximum, sc), axis=-1, keepdims=True))
                        alpha = jnp.exp(m_old - m_new)
                        pe = [jnp.exp(sc[g] - m_new) for g in range(ng)]
                        l_s[h, rows, :] = alpha * l_s[h, rows, :] + _tree(jnp.add, pe)
                        m_s[h, rows, :] = m_new
                        a_s[j, rows, :] = alpha
                        for g in range(ng):
                            p_scr[j, rows, 128 * g:128 * g + 128] = pe[g].astype(BF16)

                    _chunk_loop(bq // R, chunk)
                    pvs.append(_dot_nn(p_scr[j], vp))
                acc[:, sl] = jnp.where(qlo, a_s[0], a_s[1]) * acc[:, sl] + jnp.where(qlo, pvs[0], pvs[1])

        @pl.when(ki < qi)
        def _():
            block(False)

        @pl.when(ki == qi)
        def _():
            block(True)

        @pl.when(ki == nq - 1)
        def _():
            qlo = _lane_lo((bq, 128))
            lse = jnp.zeros((bq, 128), F32)
            for p in range(4):
                sl = slice(128 * p, 128 * p + 128)
                l0 = jnp.sum(l_s[2 * p], axis=-1, keepdims=True)
                l1 = jnp.sum(l_s[2 * p + 1], axis=-1, keepdims=True)
                o_ref[:, sl] = (acc[:, sl] / jnp.where(qlo, l0, l1)).astype(BF16)
                lse = lse + _put_col((bq, 128), 2 * p, m_s[2 * p][:, 0:1] + jnp.log(l0))
                lse = lse + _put_col((bq, 128), 2 * p + 1, m_s[2 * p + 1][:, 0:1] + jnp.log(l1))
            lse_ref[...] = lse

    return pl.pallas_call(
        body, name=name, grid=(nq, nq),
        in_specs=[pl.BlockSpec((bq, BRANCH), lambda i, k: (i, CB_FQ)),
                  pl.BlockSpec((bk, BRANCH), lambda i, k: (jnp.minimum(k, i), CB_FK)),
                  pl.BlockSpec((bk, BRANCH), lambda i, k: (jnp.minimum(k, i), CB_FV)),
                  pl.BlockSpec((8, bk), lambda i, k: (0, jnp.minimum(k, i)))],
        out_specs=[pl.BlockSpec((bq, BRANCH), lambda i, k: (i, 0)),
                   pl.BlockSpec((bq, 128), lambda i, k: (i, 0))],
        out_shape=[jax.ShapeDtypeStruct((T, BRANCH), BF16), jax.ShapeDtypeStruct((T, 128), F32)],
        scratch_shapes=[pltpu.VMEM((bq, BRANCH), F32), pltpu.VMEM((8, bq, 128), F32),
                        pltpu.VMEM((8, bq, 128), F32), pltpu.VMEM((2, bq, 128), F32),
                        pltpu.VMEM((2, bq, bk), F32), pltpu.VMEM((2, bq, bk), BF16)],
        compiler_params=_cp(("parallel", "arbitrary")),
    )(pm, pm, pm, c_row)


def _fox2_bwd_dq(pm, do, c_row, lse, delta, dproj, name):
    T = pm.shape[0]
    bq = _pick(T, (512, 256))
    bk = bq
    nq = T // bq
    R = FOX_ROWS
    ng = bk // 128

    def body(q_ref, k_ref, v_ref, do_ref, ck_ref, lse_ref, dl_ref, buf_ref, dq_ref, dl2_ref,
             acc, e_s, s_scr, dp_scr, ds_scr):
        del buf_ref
        qi = pl.program_id(0)
        ki = pl.program_id(1)

        @pl.when(ki == 0)
        def _():
            acc[...] = jnp.zeros_like(acc)
            e_s[...] = jnp.zeros_like(e_s)

        def block(masked):
            qlo = _lane_lo((bq, 128))
            for p in range(4):
                sl = slice(128 * p, 128 * p + 128)
                qp = q_ref[:, sl] * ATT_SCALE
                kp = k_ref[:, sl]
                dop = do_ref[:, sl]
                ks = _masked_halves(kp)
                vs = _masked_halves(v_ref[:, sl])
                dqs = []
                for j in range(2):
                    h = 2 * p + j
                    s_scr[...] = _dot_nt(qp, ks[j])
                    dp_scr[...] = _dot_nt(dop, vs[j])

                    def chunk(r, h=h):
                        r0 = r * R
                        rows = pl.ds(r0, R)
                        lse_c = lse_ref[rows, h:h + 1]
                        dl_c = dl_ref[rows, h:h + 1]
                        if masked:
                            rid = lax.broadcasted_iota(I32, (R, 128), 0) + r0
                            cid = lax.broadcasted_iota(I32, (R, 128), 1)
                        dss = []
                        for g in range(ng):
                            gs = slice(128 * g, 128 * g + 128)
                            sc = s_scr[rows, gs] - ck_ref[h:h + 1, gs]
                            if masked:
                                sc = jnp.where(cid + 128 * g <= rid, sc, NEG)
                            ds = jnp.exp(sc - lse_c) * (dp_scr[rows, gs] - dl_c)
                            ds_scr[rows, gs] = ds.astype(BF16)
                            dss.append(ds)
                        e_s[h, rows, :] += _tree(jnp.add, dss)

                    _chunk_loop(bq // R, chunk)
                    dqs.append(_dot_nn(ds_scr[...], kp))
                acc[:, sl] += jnp.where(qlo, dqs[0], dqs[1])

        @pl.when(ki < qi)
        def _():
            block(False)

        @pl.when(ki == qi)
        def _():
            block(True)

        @pl.when(ki == nq - 1)
        def _():
            dq_ref[...] = (acc[...] * ATT_SCALE).astype(BF16)
            out = dl_ref[...]
            for h in range(8):
                out = out + _put_col((bq, 128), h, jnp.sum(e_s[h], axis=-1, keepdims=True))
            dl2_ref[...] = out

    qb = pl.BlockSpec((bq, 128), lambda i, k: (i, 0))
    return pl.pallas_call(
        body, name=name, grid=(nq, nq),
        in_specs=[pl.BlockSpec((bq, BRANCH), lambda i, k: (i, CB_FQ)),
                  pl.BlockSpec((bk, BRANCH), lambda i, k: (jnp.minimum(k, i), CB_FK)),
                  pl.BlockSpec((bk, BRANCH), lambda i, k: (jnp.minimum(k, i), CB_FV)),
                  pl.BlockSpec((bq, BRANCH), lambda i, k: (i, 0)),
                  pl.BlockSpec((8, bk), lambda i, k: (0, jnp.minimum(k, i))), qb, qb,
                  pl.BlockSpec(memory_space=pl.ANY)],
        out_specs=[pl.BlockSpec((bq, BRANCH), lambda i, k: (i, CB_FQ)), qb],
        out_shape=[jax.ShapeDtypeStruct(dproj.shape, dproj.dtype), jax.ShapeDtypeStruct((T, 128), F32)],
        input_output_aliases={7: 0},
        scratch_shapes=[pltpu.VMEM((bq, BRANCH), F32), pltpu.VMEM((8, bq, 128), F32),
                        pltpu.VMEM((bq, bk), F32), pltpu.VMEM((bq, bk), F32), pltpu.VMEM((bq, bk), BF16)],
        compiler_params=_cp(("parallel", "arbitrary")),
    )(pm, pm, pm, do, c_row, lse, delta, dproj)


def _fox2_bwd_dkv(pm, do, c_col, lse_row, delta_row, dproj, name):
    T = pm.shape[0]
    bk = _pick(T, (512, 256))
    bq = bk
    nk = T // bk
    R = FOX_ROWS
    ng = bq // 128

    def body(q_ref, k_ref, v_ref, do_ref, ck_ref, lse_ref, dl_ref, buf_ref, dkv_ref, dc_ref,
             dk_acc, dv_acc, dc_s, st_scr, dpt_scr, pt_scr, dst_scr):
        del buf_ref
        dk_ref = dkv_ref.at[:, 0:BRANCH]
        dv_ref = dkv_ref.at[:, BRANCH:2 * BRANCH]
        ki = pl.program_id(0)
        qi = pl.program_id(1)

        @pl.when(qi == 0)
        def _():
            dk_acc[...] = jnp.zeros_like(dk_acc)
            dv_acc[...] = jnp.zeros_like(dv_acc)
            dc_s[...] = jnp.zeros_like(dc_s)

        def block(masked):
            klo = _lane_lo((bk, 128))
            for p in range(4):
                sl = slice(128 * p, 128 * p + 128)
                qp = q_ref[:, sl]
                kp = k_ref[:, sl] * ATT_SCALE
                vp = v_ref[:, sl]
                dop = do_ref[:, sl]
                qs = _masked_halves(qp)
                dos = _masked_halves(dop)
                dks, dvs = [], []
                for j in range(2):
                    h = 2 * p + j
                    st_scr[...] = _dot_nt(kp, qs[j])
                    dpt_scr[...] = _dot_nt(vp, dos[j])

                    def chunk(r, h=h):
                        r0 = r * R
                        rows = pl.ds(r0, R)
                        ck_c = ck_ref[rows, h:h + 1]
                        if masked:
                            kid = lax.broadcasted_iota(I32, (R, 128), 0) + r0
                            qid = lax.broadcasted_iota(I32, (R, 128), 1)
                        dss = []
                        for g in range(ng):
                            gs = slice(128 * g, 128 * g + 128)
                            st = st_scr[rows, gs] - (ck_c + lse_ref[h:h + 1, gs])
                            if masked:
                                st = jnp.where(kid <= qid + 128 * g, st, NEG)
                            pt = jnp.exp(st)
                            dst = pt * (dpt_scr[rows, gs] - dl_ref[h:h + 1, gs])
                            pt_scr[rows, gs] = pt.astype(BF16)
                            dst_scr[rows, gs] = dst.astype(BF16)
                            dss.append(dst)
                        dc_s[h, rows, :] -= _tree(jnp.add, dss)

                    _chunk_loop(bk // R, chunk)
                    dvs.append(_dot_nn(pt_scr[...], dop))
                    dks.append(_dot_nn(dst_scr[...], qp))
                dk_acc[:, sl] += jnp.where(klo, dks[0], dks[1])
                dv_acc[:, sl] += jnp.where(klo, dvs[0], dvs[1])

        @pl.when(qi > ki)
        def _():
            block(False)

        @pl.when(qi == ki)
        def _():
            block(True)

        @pl.when(qi == nk - 1)
        def _():
            dk_ref[...] = (dk_acc[...] * ATT_SCALE).astype(BF16)
            dv_ref[...] = dv_acc[...].astype(BF16)
            out = jnp.zeros((bk, 128), F32)
            for h in range(8):
                out = out + _put_col((bk, 128), h, jnp.sum(dc_s[h], axis=-1, keepdims=True))
            dc_ref[...] = out

    qrow = pl.BlockSpec((8, bq), lambda k, i: (0, jnp.maximum(i, k)))
    return pl.pallas_call(
        body, name=name, grid=(nk, nk),
        in_specs=[pl.BlockSpec((bq, BRANCH), lambda k, i: (jnp.maximum(i, k), CB_FQ)),
                  pl.BlockSpec((bk, BRANCH), lambda k, i: (k, CB_FK)),
                  pl.BlockSpec((bk, BRANCH), lambda k, i: (k, CB_FV)),
                  pl.BlockSpec((bq, BRANCH), lambda k, i: (jnp.maximum(i, k), 0)),
                  pl.BlockSpec((bk, 128), lambda k, i: (k, 0)), qrow, qrow, pl.BlockSpec(memory_space=pl.ANY)],
        out_specs=[pl.BlockSpec((bk, 2 * BRANCH), lambda k, i: (k, 5)), pl.BlockSpec((bk, 128), lambda k, i: (k, 0))],
        out_shape=[jax.ShapeDtypeStruct(dproj.shape, dproj.dtype), jax.ShapeDtypeStruct((T, 128), F32)],
        input_output_aliases={7: 0},
        scratch_shapes=[pltpu.VMEM((bk, BRANCH), F32), pltpu.VMEM((bk, BRANCH), F32),
                        pltpu.VMEM((8, bk, 128), F32), pltpu.VMEM((bk, bq), F32), pltpu.VMEM((bk, bq), F32),
                        pltpu.VMEM((bk, bq), BF16), pltpu.VMEM((bk, bq), BF16)],
        compiler_params=_cp(("parallel", "arbitrary")),
    )(pm, pm, pm, do, c_col, lse_row, delta_row, dproj)


def _bucket_table():
    tq = np.arange(WINDOW, dtype=np.int32)[:, None]
    sk = np.arange(2 * WINDOW, dtype=np.int32)[None, :]
    n = np.maximum(WINDOW + tq - sk, 0)
    max_exact = N_BUCKETS // 2
    ratio = np.maximum(n, 1).astype(np.float32) / np.float32(max_exact)
    large = max_exact + (np.log(ratio) / np.float32(math.log(WINDOW / max_exact))
                         * np.float32(N_BUCKETS - max_exact)).astype(np.int32)
    large = np.minimum(large, N_BUCKETS - 1)
    return np.where(n < max_exact, n, large).astype(np.int32)


def _swap_halves(x):
    return pltpu.roll(x.astype(F32), HEAD_DIM, 1).astype(x.dtype)


def _kv_variants(t):
    lo = _lane_lo(t.shape)
    z = jnp.zeros_like(t)
    a0 = jnp.where(lo, t, z)
    b1 = jnp.where(lo, z, t)
    b0 = _swap_halves(a0)
    a1 = _swap_halves(b1)
    return (a0, a1), (b0, b1), (a0 + b0, a1 + b1)


def _stacked_head(s, r):
    return 4 * (s // 2) + 2 * r + (s % 2)


def _swa_bias(rel_bias, bucket, name):
    def body(rb_ref, bk_ref, o_ref):
        bkt = bk_ref[...]
        tq = lax.broadcasted_iota(I32, bkt.shape, 0)
        jj = lax.broadcasted_iota(I32, bkt.shape, 1)
        window = ((jj < WINDOW) & (jj > tq)) | ((jj >= WINDOW) & (jj - WINDOW <= tq))
        for s in range(4):
            for r in range(2):
                h = _stacked_head(s, r)

                def step(b, a, h=h):
                    return a + jnp.where(bkt == b, rb_ref[b, h], 0.0)
                val = lax.fori_loop(0, N_BUCKETS, step, jnp.zeros(bkt.shape, F32))
                o_ref[s, WINDOW * r:WINDOW * (r + 1), :] = jnp.where(window, val, NEG)

    return pl.pallas_call(
        body, name=name,
        in_specs=[pl.BlockSpec(memory_space=pltpu.SMEM), pl.BlockSpec(memory_space=pltpu.VMEM)],
        out_specs=pl.BlockSpec(memory_space=pltpu.VMEM),
        out_shape=jax.ShapeDtypeStruct((4, 2 * WINDOW, 2 * WINDOW), F32),
    )(rel_bias, bucket)


def _swa_dbias_reduce(dbias, bucket, name):
    def body(d_ref, bk_ref, o_ref):
        bkt = bk_ref[...]
        rowi = lax.broadcasted_iota(I32, (N_BUCKETS, 128), 0)
        lane = lax.broadcasted_iota(I32, (N_BUCKETS, 128), 1)
        out = jnp.zeros((N_BUCKETS, 128), F32)
        for s in range(4):
            for r in range(2):
                h = _stacked_head(s, r)
                dv = d_ref[s, WINDOW * r:WINDOW * (r + 1), :]

                def step(b, a, dv=dv, h=h):
                    tot = jnp.sum(jnp.where(bkt == b, dv, 0.0), keepdims=True)
                    return a + jnp.where((rowi == b) & (lane == h), tot, 0.0)
                out = lax.fori_loop(0, N_BUCKETS, step, out)
        o_ref[...] = out

    return pl.pallas_call(
        body, name=name,
        in_specs=[pl.BlockSpec(memory_space=pltpu.VMEM), pl.BlockSpec(memory_space=pltpu.VMEM)],
        out_specs=pl.BlockSpec(memory_space=pltpu.VMEM),
        out_shape=jax.ShapeDtypeStruct((N_BUCKETS, 128), F32),
    )(dbias, bucket)


def _swa_cols(vec, s):
    rows = lax.broadcasted_iota(I32, (2 * WINDOW, 1), 0)
    return jnp.where(rows < WINDOW, vec[:, _stacked_head(s, 0):_stacked_head(s, 0) + 1],
                     vec[:, _stacked_head(s, 1):_stacked_head(s, 1) + 1])


SWA_FWD_BLOCKS = 4
SWA_BWD_BLOCKS = 2

def _swa_scores(qg, kband, bias_tile, first):
    sc = _dot_nt(qg, kband) + bias_tile
    if first is not None:
        jj = lax.broadcasted_iota(I32, sc.shape, 1)
        sc = jnp.where((jj < WINDOW) & first, NEG, sc)
    return sc


def _swa_stack(ref, rows, g):
    return jnp.concatenate([ref[rows, 256 * g:256 * g + 128], ref[rows, 256 * g + 128:256 * g + 256]], axis=0)


def _swa_specs(blocks):
    step = blocks * WINDOW
    q = pl.BlockSpec((step, BRANCH), lambda i: (i, CB_SQ))
    kc = pl.BlockSpec((step, 128), lambda i: (i, CB_SK))
    kp = pl.BlockSpec((WINDOW, 128), lambda i: (jnp.maximum(blocks * i - 1, 0), CB_SK))
    vc = pl.BlockSpec((step, 128), lambda i: (i, CB_SV))
    vp = pl.BlockSpec((WINDOW, 128), lambda i: (jnp.maximum(blocks * i - 1, 0), CB_SV))
    bias = pl.BlockSpec((4, 2 * WINDOW, 2 * WINDOW), lambda i: (0, 0, 0))
    vec = pl.BlockSpec((1, 128), lambda i: (0, 0))
    return q, kc, kp, vc, vp, bias, vec


def _swa_fwd(pm, bias, sink, name):
    T = pm.shape[0]
    blocks = SWA_FWD_BLOCKS
    step = blocks * WINDOW
    nb = T // step

    def body(q_ref, kc_ref, kp_ref, vc_ref, vp_ref, b_ref, s_ref, o_ref, m_ref):
        i = pl.program_id(0)
        lo = _lane_lo((WINDOW, 128))
        sink_v = s_ref[...]
        for u in range(blocks):
            rows = slice(WINDOW * u, WINDOW * (u + 1))
            before = slice(WINDOW * (u - 1), WINDOW * u)
            first = (i == 0) if u == 0 else None
            kcur, vcur = kc_ref[rows, :], vc_ref[rows, :]
            kprev = kp_ref[...] if u == 0 else kc_ref[before, :]
            vprev = vp_ref[...] if u == 0 else vc_ref[before, :]
            kcA, kcB, _ = _kv_variants(kcur)
            kpA, kpB, _ = _kv_variants(kprev)
            _, _, vcD = _kv_variants(vcur)
            _, _, vpD = _kv_variants(vprev)
            mout = jnp.zeros((WINDOW, 128), F32)
            for g in range(2):
                qg = _swa_stack(q_ref, rows, g) * ATT_SCALE
                vband = jnp.concatenate([vpD[g], vcD[g]], axis=0)
                outs = []
                for par in range(2):
                    s = 2 * g + par
                    kband = jnp.concatenate([(kpA, kpB)[par][g], (kcA, kcB)[par][g]], axis=0)
                    sc = _swa_scores(qg, kband, b_ref[s], first)
                    sk = _swa_cols(sink_v, s)
                    m = jnp.maximum(jnp.max(sc, axis=-1, keepdims=True), sk)
                    e = jnp.exp(sc - m)
                    den = jnp.sum(e, axis=-1, keepdims=True) + jnp.exp(sk - m)
                    outs.append(_dot_nn((e * (1.0 / den)).astype(BF16), vband))
                    lse = m + jnp.log(den)
                    mout = mout + _put_col((WINDOW, 128), _stacked_head(s, 0), lse[:WINDOW])
                    mout = mout + _put_col((WINDOW, 128), _stacked_head(s, 1), lse[WINDOW:])
                for r in range(2):
                    sl = slice(256 * g + 128 * r, 256 * g + 128 * r + 128)
                    o_ref[rows, sl] = jnp.where(lo, outs[0][WINDOW * r:WINDOW * (r + 1)],
                                                outs[1][WINDOW * r:WINDOW * (r + 1)]).astype(BF16)
            m_ref[rows, :] = mout

    q, kc, kp, vc, vp, bs, vec = _swa_specs(blocks)
    return pl.pallas_call(
        body, name=name, grid=(nb,),
        in_specs=[q, kc, kp, vc, vp, bs, vec],
        out_specs=[pl.BlockSpec((step, BRANCH), lambda i: (i, 0)),
                   pl.BlockSpec((step, 128), lambda i: (i, 0))],
        out_shape=[jax.ShapeDtypeStruct((T, BRANCH), BF16), jax.ShapeDtypeStruct((T, 128), F32)],
        compiler_params=_cp(("parallel",)),
    )(pm, pm, pm, pm, pm, bias, sink)


def _swa_bwd(pm, bias, sink, do, mlse, dproj, name):
    T = pm.shape[0]
    blocks = SWA_BWD_BLOCKS
    step = blocks * WINDOW
    nb = T // step

    def fold(zz):
        return zz + pltpu.roll(zz, HEAD_DIM, 1)

    def body(q_ref, kc_ref, kp_ref, vc_ref, vp_ref, b_ref, s_ref, do_ref, m_ref, buf_ref,
             dq_ref, dkc_ref, dkp_ref, dvc_ref, dvp_ref, db_ref, ds_ref):
        del buf_ref
        i = pl.program_id(0)

        @pl.when(i == 0)
        def _():
            db_ref[...] = jnp.zeros_like(db_ref)
            ds_ref[...] = jnp.zeros_like(ds_ref)

        lo = _lane_lo((WINDOW, 128))
        lo2 = _lane_lo((2 * WINDOW, 128))
        sink_v = s_ref[...]
        dsink = jnp.zeros((1, 128), F32)
        for u in range(blocks):
            rows = slice(WINDOW * u, WINDOW * (u + 1))
            before = slice(WINDOW * (u - 1), WINDOW * u)
            first = (i == 0) if u == 0 else None
            kcur, vcur = kc_ref[rows, :], vc_ref[rows, :]
            kprev = kp_ref[...] if u == 0 else kc_ref[before, :]
            vprev = vp_ref[...] if u == 0 else vc_ref[before, :]
            kcA, kcB, kcD = _kv_variants(kcur)
            kpA, kpB, kpD = _kv_variants(kprev)
            vcA, vcB, _ = _kv_variants(vcur)
            vpA, vpB, _ = _kv_variants(vprev)
            mv = m_ref[rows, :]
            zks, zvs = [], []
            for g in range(2):
                qraw = _swa_stack(q_ref, rows, g)
                qg = qraw * ATT_SCALE
                dog = _swa_stack(do_ref, rows, g)
                kband_d = jnp.concatenate([kpD[g], kcD[g]], axis=0)
                dqs, mks, mvs = [], [], []
                for par in range(2):
                    s = 2 * g + par
                    kband = jnp.concatenate([(kpA, kpB)[par][g], (kcA, kcB)[par][g]], axis=0)
                    vband = jnp.concatenate([(vpA, vpB)[par][g], (vcA, vcB)[par][g]], axis=0)
                    sc = _swa_scores(qg, kband, b_ref[s], first)
                    h0, h1 = _stacked_head(s, 0), _stacked_head(s, 1)
                    m_c = jnp.concatenate([mv[:, h0:h0 + 1], mv[:, h1:h1 + 1]], axis=0)
                    pr = jnp.exp(sc - m_c)
                    psink = jnp.exp(_swa_cols(sink_v, s) - m_c)
                    dp = _dot_nt(dog, vband)
                    delta = jnp.sum(pr * dp, axis=-1, keepdims=True)
                    dsc = pr * (dp - delta)
                    db_ref[s] += dsc
                    sd = psink * delta
                    dsink = dsink - _put_col((1, 128), h0, jnp.sum(sd[:WINDOW], keepdims=True))
                    dsink = dsink - _put_col((1, 128), h1, jnp.sum(sd[WINDOW:], keepdims=True))
                    dsb = dsc.astype(BF16)
                    dqs.append(_dot_nn(dsb, kband_d))
                    mks.append(_dot_tn(dsb, qraw))
                    mvs.append(_dot_tn(pr.astype(BF16), dog))
                for r in range(2):
                    sl = slice(256 * g + 128 * r, 256 * g + 128 * r + 128)
                    dq_ref[rows, sl] = (jnp.where(lo, dqs[0][WINDOW * r:WINDOW * (r + 1)],
                                                  dqs[1][WINDOW * r:WINDOW * (r + 1)]) * ATT_SCALE).astype(BF16)
                zks.append(fold(jnp.where(lo2, mks[0], mks[1])))
                zvs.append(fold(jnp.where(lo2, mvs[0], mvs[1])))
            dk = jnp.where(lo2, zks[0], zks[1]) * ATT_SCALE
            dv = jnp.where(lo2, zvs[0], zvs[1])
            dkp_ref[rows, :] = dk[:WINDOW]
            dkc_ref[rows, :] = dk[WINDOW:]
            dvp_ref[rows, :] = dv[:WINDOW]
            dvc_ref[rows, :] = dv[WINDOW:]
        ds_ref[...] += dsink

    q, kc, kp, vc, vp, bs, vec = _swa_specs(blocks)
    own = pl.BlockSpec((step, BRANCH), lambda i: (i, 0))
    sm = pl.BlockSpec((step, 128), lambda i: (i, 0))
    f128 = jax.ShapeDtypeStruct((T, 128), F32)
    return pl.pallas_call(
        body, name=name, grid=(nb,),
        in_specs=[q, kc, kp, vc, vp, bs, vec, own, sm, pl.BlockSpec(memory_space=pl.ANY)],
        out_specs=[pl.BlockSpec((step, BRANCH), lambda i: (i, CB_SQ)), sm, sm, sm, sm, bs, vec],
        out_shape=[jax.ShapeDtypeStruct(dproj.shape, dproj.dtype), f128, f128, f128, f128,
                   jax.ShapeDtypeStruct((4, 2 * WINDOW, 2 * WINDOW), F32), jax.ShapeDtypeStruct((1, 128), F32)],
        input_output_aliases={9: 0},
        compiler_params=_cp(("arbitrary",)),
    )(pm, pm, pm, pm, pm, bias, sink, do, mlse, dproj)


def _merge_fwd(pm, us, name):
    T = pm.shape[0]
    bt = _pick(T, (512, 256))

    def body(g0, g1, g2, u0, u1, u2, o_ref):
        acc = jax.nn.sigmoid(g0[...].astype(F32)) * u0[...].astype(F32)
        acc = acc + jax.nn.sigmoid(g1[...].astype(F32)) * u1[...].astype(F32)
        acc = acc + jax.nn.sigmoid(g2[...].astype(F32)) * u2[...].astype(F32)
        o_ref[...] = acc.astype(BF16)

    own = pl.BlockSpec((bt, D_MODEL), lambda i: (i, 0))
    gs = [pl.BlockSpec((bt, D_MODEL), lambda i, cb=cb: (i, cb)) for cb in CB_GATE]
    return pl.pallas_call(
        body, name=name, grid=(T // bt,), in_specs=gs + [own, own, own], out_specs=own,
        out_shape=jax.ShapeDtypeStruct((T, D_MODEL), BF16),
        compiler_params=_cp(("parallel",)),
    )(pm, pm, pm, *us)


def _merge_bwd(pm, us, dm, name):
    T = pm.shape[0]
    bt = _pick(T, (512, 256))

    def body(g0, g1, g2, u0, u1, u2, dm_ref, du0, du1, du2, dg_ref):
        dmv = dm_ref[...].astype(F32)
        for b, (g, u, du) in enumerate(((g0, u0, du0), (g1, u1, du1), (g2, u2, du2))):
            s = jax.nn.sigmoid(g[...].astype(F32))
            du[...] = (dmv * s).astype(BF16)
            dg_ref[:, D_MODEL * b:D_MODEL * (b + 1)] = (dmv * u[...].astype(F32) * s * (1.0 - s)).astype(BF16)

    own = pl.BlockSpec((bt, D_MODEL), lambda i: (i, 0))
    gs = [pl.BlockSpec((bt, D_MODEL), lambda i, cb=cb: (i, cb)) for cb in CB_GATE]
    act = jax.ShapeDtypeStruct((T, D_MODEL), BF16)
    return pl.pallas_call(
        body, name=name, grid=(T // bt,), in_specs=gs + [own, own, own, own],
        out_specs=[own, own, own, pl.BlockSpec((bt, 3 * D_MODEL), lambda i: (i, 0))],
        out_shape=[act, act, act, jax.ShapeDtypeStruct((T, PROJ_PAD), BF16)],
        compiler_params=_cp(("parallel",)),
    )(pm, pm, pm, *us, dm)


def _swiglu_fwd(ab, name):
    T = ab.shape[0]
    bt = _pick(T, (512, 256))

    def body(a_ref, b_ref, o_ref):
        a = a_ref[...].astype(F32)
        o_ref[...] = (a * jax.nn.sigmoid(a) * b_ref[...].astype(F32)).astype(BF16)

    return pl.pallas_call(
        body, name=name, grid=(T // bt,),
        in_specs=[pl.BlockSpec((bt, D_FF), lambda i: (i, 0)), pl.BlockSpec((bt, D_FF), lambda i: (i, 1))],
        out_specs=pl.BlockSpec((bt, D_FF), lambda i: (i, 0)),
        out_shape=jax.ShapeDtypeStruct((T, D_FF), BF16),
        compiler_params=_cp(("parallel",)),
    )(ab, ab)


def _swiglu_bwd(ab, dh, name):
    T = ab.shape[0]
    bt = _pick(T, (512, 256))

    def body(a_ref, b_ref, d_ref, o_ref):
        a = a_ref[...].astype(F32)
        b = b_ref[...].astype(F32)
        d = d_ref[...].astype(F32)
        s = jax.nn.sigmoid(a)
        o_ref[:, 0:D_FF] = (d * b * (s + a * s * (1.0 - s))).astype(BF16)
        o_ref[:, D_FF:2 * D_FF] = (d * a * s).astype(BF16)

    return pl.pallas_call(
        body, name=name, grid=(T // bt,),
        in_specs=[pl.BlockSpec((bt, D_FF), lambda i: (i, 0)), pl.BlockSpec((bt, D_FF), lambda i: (i, 1)),
                  pl.BlockSpec((bt, D_FF), lambda i: (i, 0))],
        out_specs=pl.BlockSpec((bt, 2 * D_FF), lambda i: (i, 0)),
        out_shape=jax.ShapeDtypeStruct((T, 2 * D_FF), BF16),
        compiler_params=_cp(("parallel",)),
    )(ab, ab, dh)


def _xattn_probs(q_ref, kv_ref, h):
    sl = slice(X_HEAD_DIM * h, X_HEAD_DIM * (h + 1))
    qh = q_ref[:, sl]
    kh = kv_ref[:, sl]
    vh = kv_ref[:, D_MODEL + X_HEAD_DIM * h:D_MODEL + X_HEAD_DIM * (h + 1)]
    s = _dot_nt(qh, kh) * X_SCALE
    e = jnp.exp(s - jnp.max(s, axis=-1, keepdims=True))
    return qh, kh, vh, e * (1.0 / jnp.sum(e, axis=-1, keepdims=True))


def _xattn_fwd(q, kv, name):
    T = q.shape[0]
    bq = _pick(T, (512, 256))

    def body(q_ref, kv_ref, o_ref):
        for h in range(X_HEADS):
            _, _, vh, p = _xattn_probs(q_ref, kv_ref, h)
            o_ref[:, X_HEAD_DIM * h:X_HEAD_DIM * (h + 1)] = _dot_nn(p.astype(BF16), vh).astype(BF16)

    own = pl.BlockSpec((bq, D_MODEL), lambda i: (i, 0))
    return pl.pallas_call(
        body, name=name, grid=(T // bq,),
        in_specs=[own, pl.BlockSpec((MEM_LEN, 2 * D_MODEL), lambda i: (0, 0))], out_specs=own,
        out_shape=jax.ShapeDtypeStruct((T, D_MODEL), BF16),
        compiler_params=_cp(("parallel",)),
    )(q, kv)


def _xattn_bwd(q, kv, do, name):
    T = q.shape[0]
    bq = _pick(T, (512, 256))

    def body(q_ref, kv_ref, do_ref, dq_ref, dkv_ref):
        @pl.when(pl.program_id(0) == 0)
        def _():
            dkv_ref[...] = jnp.zeros_like(dkv_ref)

        for h in range(X_HEADS):
            sl = slice(X_HEAD_DIM * h, X_HEAD_DIM * (h + 1))
            qh, kh, vh, p = _xattn_probs(q_ref, kv_ref, h)
            doh = do_ref[:, sl]
            dp = _dot_nt(doh, vh)
            ds = (p * (dp - jnp.sum(p * dp, axis=-1, keepdims=True)) * X_SCALE).astype(BF16)
            dq_ref[:, sl] = _dot_nn(ds, kh).astype(BF16)
            dkv_ref[:, sl] += _dot_tn(ds, qh)
            dkv_ref[:, D_MODEL + X_HEAD_DIM * h:D_MODEL + X_HEAD_DIM * (h + 1)] += _dot_tn(p.astype(BF16), doh)

    own = pl.BlockSpec((bq, D_MODEL), lambda i: (i, 0))
    kvs = pl.BlockSpec((MEM_LEN, 2 * D_MODEL), lambda i: (0, 0))
    return pl.pallas_call(
        body, name=name, grid=(T // bq,), in_specs=[own, kvs, own], out_specs=[own, kvs],
        out_shape=[jax.ShapeDtypeStruct((T, D_MODEL), BF16), jax.ShapeDtypeStruct((MEM_LEN, 2 * D_MODEL), F32)],
        compiler_params=_cp(("arbitrary",)),
    )(q, kv, do)


def _adamw(w, g, m, v, name):
    R, C = w.shape[0], w.shape[-1]
    rest = w.shape[1:]
    row_bytes = int(np.prod(rest[:-1], dtype=np.int64)) * (-(-C // 128) * 128) * 4
    cands = (1024, 512, 256, 128, 64, 32, 16, 8) if w.ndim == 2 else range(R, 0, -1)
    bt = R
    for cand in cands:
        if R % cand == 0 and cand * row_bytes <= (3 << 19):
            bt = cand
            break
    zeros = (0,) * len(rest)

    def body(w_ref, g_ref, m_ref, v_ref, d_ref, nm_ref, nv_ref):
        gv = g_ref[...]
        mn = ADAM_B1 * m_ref[...] + (1.0 - ADAM_B1) * gv
        vn = ADAM_B2 * v_ref[...] + (1.0 - ADAM_B2) * (gv * gv)
        m_hat = mn / (1.0 - ADAM_B1 ** ADAM_STEP)
        v_hat = vn / (1.0 - ADAM_B2 ** ADAM_STEP)
        d_ref[...] = -ADAM_LR * (m_hat / (jnp.sqrt(v_hat) + ADAM_EPS) + ADAM_WD * w_ref[...])
        nm_ref[...] = mn
        nv_ref[...] = vn

    blk = pl.BlockSpec((bt,) + tuple(rest), lambda i: (i,) + zeros)
    out = jax.ShapeDtypeStruct(w.shape, F32)
    return pl.pallas_call(
        body, name=name, grid=(R // bt,), in_specs=[blk] * 4, out_specs=[blk] * 3,
        out_shape=[out, out, out], compiler_params=_cp(("parallel",)),
    )(w, g, m, v)


ANY = pl.BlockSpec(memory_space=pl.ANY)

BIG = (
    ("w_in", (2048, 1730)), ("w_branch", (3072, 256)), ("w_mix_out", (512, 1024)), ("w_xq", (512, 1024)),
    ("w_xkv", (2048, 512)), ("w_xo", (512, 1024)), ("w_ffn_gate", (1408, 1024)), ("w_ffn_up", (1408, 1024)),
    ("w_ffn_down", (1408, 1024)),
)
TRANSPOSED = ("w_ffn_gate", "w_ffn_up")
ROW_BLOCKS = (512, 256, 352, 128, 16)


def _neighbours():
    x, y, c = lax.axis_index("x"), lax.axis_index("y"), lax.axis_index("c")
    idx = (2 * x + y, 2 * (1 - x) + y, 2 * x + (1 - y), 2 * (1 - x) + (1 - y))
    return idx, (x, y, c), (1 - x, y, c), (x, 1 - y, c), (x, y, 1 - c)


def _remote(src, dst, sems, k, to):
    send_sems, recv_sems = sems
    return pltpu.make_async_remote_copy(src_ref=src, dst_ref=dst, send_sem=send_sems.at[k], recv_sem=recv_sems.at[k],
                                        device_id=to, device_id_type=MESH)


def _cast_place(w, me_idx, name):
    R, Wd = w.shape
    bt = _pick(R, ROW_BLOCKS)

    def body(i_ref, w_ref, o_ref):
        o_ref[0] = w_ref[...].astype(BF16)

    grid_spec = pltpu.PrefetchScalarGridSpec(
        num_scalar_prefetch=1, grid=(R // bt,),
        in_specs=[pl.BlockSpec((bt, Wd), lambda i, idx: (i, 0))],
        out_specs=pl.BlockSpec((1, bt, Wd), lambda i, idx: (idx[0], i, 0)))
    return pl.pallas_call(
        body, name=name, grid_spec=grid_spec, out_shape=jax.ShapeDtypeStruct((4, R, Wd), BF16),
        compiler_params=_cp(("parallel",)),
    )(me_idx, w)


def _ag_ring_multi(bufs):
    n = len(bufs)

    def body(*refs):
        o = refs[n:2 * n]
        sems = refs[2 * n:]
        (me, ix, iy, idg), here, xn, yn, sib = _neighbours()
        c = here[2]

        def piece(t, k, other):
            h = bufs[t].shape[1] // 2
            q = h // 2
            base = ((1 - c) if other else c) * h
            return [(ix, pl.ds(base, h)), (iy, pl.ds(base, h)), (idg, pl.ds(base, q)), (idg, pl.ds(base + q, q))][k]

        def copy(t, k, slab, rows, to):
            ref = o[t].at[slab, rows]
            return _remote(ref, ref, sems, 8 * t + k, to)

        sends = []

        def go(cp):
            cp.start()
            sends.append(cp)

        for t in range(n):
            h = bufs[t].shape[1] // 2
            go(copy(t, 0, me, pl.ds(c * h, h), xn))
            go(copy(t, 1, me, pl.ds(c * h, h), yn))
        for k in range(4):
            for t in range(n):
                slab, rows = piece(t, k, False)
                copy(t, k, slab, rows, here).wait_recv()
                if k == 0:
                    go(copy(t, 2, ix, piece(t, 2, False)[1], yn))
                if k == 1:
                    go(copy(t, 3, iy, piece(t, 3, False)[1], xn))
                go(copy(t, 4 + k, slab, rows, sib))
        for k in range(4):
            for t in range(n):
                slab, rows = piece(t, k, True)
                copy(t, 4 + k, slab, rows, here).wait_recv()
        for cp in sends:
            cp.wait_send()

    return pl.pallas_call(
        body, name="ag_weights", in_specs=[ANY] * n, out_specs=[ANY] * n,
        input_output_aliases={t: t for t in range(n)},
        out_shape=[jax.ShapeDtypeStruct(b.shape, b.dtype) for b in bufs],
        scratch_shapes=[pltpu.SemaphoreType.DMA((8 * n,)), pltpu.SemaphoreType.DMA((8 * n,))],
    )(*bufs)


def _exchange_multi(srcs, out_shapes, plan, name, aliased=False):
    n = len(srcs)

    def body(*refs):
        ins, outs, sems = refs[:n], refs[n:2 * n], refs[2 * n:]
        places = _neighbours()
        here = places[1]
        per = [plan(t, ins[t], outs[t], places) for t in range(n)]
        width = max(len(p) for p in per)
        started = []
        for t in range(n):
            for k, (src, dst, to, land) in enumerate(per[t]):
                cp = _remote(src, dst, sems, width * t + k, to)
                cp.start()
                started.append(cp)
        for t in range(n):
            for k, (src, dst, to, land) in enumerate(per[t]):
                _remote(land, land, sems, width * t + k, here).wait_recv()
        for cp in started:
            cp.wait_send()

    nsem = 2 * n
    return pl.pallas_call(
        body, name=name, in_specs=[ANY] * n, out_specs=[ANY] * n,
        input_output_aliases={t: t for t in range(n)} if aliased else {},
        out_shape=[jax.ShapeDtypeStruct(s, d) for s, d in out_shapes],
        scratch_shapes=[pltpu.SemaphoreType.DMA((nsem,)), pltpu.SemaphoreType.DMA((nsem,))],
    )(*srcs)


def _rs_sibling_multi(gs):
    def plan(t, g, o, places):
        (_, here, _, _, sib) = places
        h = gs[t].shape[1] // 2
        return [(g.at[:, pl.ds((1 - here[2]) * h, h)], o, sib, o)]

    return _exchange_multi(gs, [((4, g.shape[1] // 2, g.shape[2]), g.dtype) for g in gs], plan, "rs_sibling")


def _rs_add_pair(g4, sib, cidx, tag=""):
    _, R, Wd = g4.shape
    hrows = R // 2
    bt = _pick(hrows, ROW_BLOCKS)
    nb = hrows // bt

    def body(c_ref, a_ref, b_ref, o_ref):
        o_ref[...] = (a_ref[...].astype(F32) + b_ref[...].astype(F32)).astype(o_ref.dtype)

    grid_spec = pltpu.PrefetchScalarGridSpec(
        num_scalar_prefetch=1, grid=(4, nb),
        in_specs=[pl.BlockSpec((1, bt, Wd), lambda j, i, c: (j, c[0] * nb + i, 0)),
                  pl.BlockSpec((1, bt, Wd), lambda j, i, c: (j, i, 0))],
        out_specs=pl.BlockSpec((1, bt, Wd), lambda j, i, c: (j, i, 0)))
    return pl.pallas_call(
        body, name="rs_add_pair" + tag, grid_spec=grid_spec,
        out_shape=jax.ShapeDtypeStruct((4, hrows, Wd), g4.dtype),
        compiler_params=_cp(("parallel", "parallel")),
    )(cidx, g4, sib)


def _rs_diag_multi(rs):
    def plan(t, r, o, places):
        ((_, _, _, idg), _, xn, yn, _) = places
        q = rs[t].shape[1] // 2
        return [(r.at[idg, pl.ds(0, q)], o.at[0], xn, o.at[0]), (r.at[idg, pl.ds(q, q)], o.at[1], yn, o.at[1])]

    return _exchange_multi(rs, [((2, r.shape[1] // 2, r.shape[2]), r.dtype) for r in rs], plan, "rs_diag")


def _rs_merge(r4, dg, nbr_idx, tag=""):
    _, hrows, Wd = r4.shape
    bt = _pick(hrows // 2, ROW_BLOCKS)
    nb = hrows // bt
    nq = nb // 2

    def body(i_ref, r_ref, d_ref, o_ref):
        w = pl.program_id(0)
        i = pl.program_id(1)
        merged = jnp.where(w == 0, i >= nq, i < nq)
        add = jnp.where(merged, d_ref[...].astype(F32), 0.0)
        o_ref[...] = (r_ref[...].astype(F32) + add).astype(o_ref.dtype)

    grid_spec = pltpu.PrefetchScalarGridSpec(
        num_scalar_prefetch=1, grid=(2, nb),
        in_specs=[pl.BlockSpec((1, bt, Wd), lambda w, i, idx: (idx[w], i, 0)),
                  pl.BlockSpec((1, bt, Wd), lambda w, i, idx: (1 - w, jnp.clip(i - (1 - w) * nq, 0, nq - 1), 0))],
        out_specs=pl.BlockSpec((1, bt, Wd), lambda w, i, idx: (w, i, 0)))
    return pl.pallas_call(
        body, name="rs_merge" + tag, grid_spec=grid_spec,
        out_shape=jax.ShapeDtypeStruct((2, hrows, Wd), r4.dtype),
        compiler_params=_cp(("parallel", "parallel")),
    )(nbr_idx, r4, dg)


def _rs_direct_multi(ms):
    def plan(t, m, o, places):
        (_, _, xn, yn, _) = places
        return [(m.at[0], o.at[0], xn, o.at[0]), (m.at[1], o.at[1], yn, o.at[1])]

    return _exchange_multi(ms, [(m.shape, m.dtype) for m in ms], plan, "rs_direct")


def _rs_final(r4, got, me_c, tag=""):
    _, hrows, Wd = r4.shape
    bt = _pick(hrows, ROW_BLOCKS)
    nb = hrows // bt

    def body(i_ref, r_ref, g_ref, o_ref):
        o_ref[...] = (r_ref[0].astype(F32) + g_ref[0].astype(F32)) + g_ref[1].astype(F32)

    grid_spec = pltpu.PrefetchScalarGridSpec(
        num_scalar_prefetch=1, grid=(nb,),
        in_specs=[pl.BlockSpec((1, bt, Wd), lambda i, idx: (idx[0], i, 0)),
                  pl.BlockSpec((2, bt, Wd), lambda i, idx: (0, i, 0))],
        out_specs=pl.BlockSpec((bt, Wd), lambda i, idx: (idx[1] * nb + i, 0)))
    return pl.pallas_call(
        body, name="rs_final" + tag, grid_spec=grid_spec,
        out_shape=jax.ShapeDtypeStruct((2 * hrows, Wd), F32),
        compiler_params=_cp(("parallel",)),
    )(me_c, r4, got)


def _rs_share_multi(bufs):
    def plan(t, b, o, places):
        (_, here, _, _, sib) = places
        h = bufs[t].shape[0] // 2
        mine = o.at[pl.ds(here[2] * h, h)]
        return [(mine, mine, sib, o.at[pl.ds((1 - here[2]) * h, h)])]

    return _exchange_multi(bufs, [(b.shape, b.dtype) for b in bufs], plan, "rs_share", aliased=True)


def _allreduce_small(v, name="allreduce_small"):
    R, Wd = v.shape

    def body(v_ref, o_ref, buf, send_sems, recv_sems):
        x, y, c = lax.axis_index("x"), lax.axis_index("y"), lax.axis_index("c")
        me = 4 * x + 2 * y + c
        buf[me] = v_ref[...]
        sends = []
        for k in range(1, 8):
            peer = ((x + (k >> 2)) % 2, (y + ((k >> 1) & 1)) % 2, (c + (k & 1)) % 2)
            sends.append(pltpu.make_async_remote_copy(
                src_ref=v_ref, dst_ref=buf.at[me], send_sem=send_sems.at[k - 1], recv_sem=recv_sems.at[k - 1],
                device_id=peer, device_id_type=MESH))
        for cp in sends:
            cp.start()
        for k in range(1, 8):
            px, py, pc = (x + (k >> 2)) % 2, (y + ((k >> 1) & 1)) % 2, (c + (k & 1)) % 2
            pltpu.make_async_remote_copy(
                src_ref=v_ref, dst_ref=buf.at[4 * px + 2 * py + pc], send_sem=send_sems.at[k - 1],
                recv_sem=recv_sems.at[k - 1], device_id=(x, y, c), device_id_type=MESH).wait_recv()
        acc = buf[0]
        for d in range(1, 8):
            acc = acc + buf[d]
        o_ref[...] = acc
        for cp in sends:
            cp.wait_send()

    vm = pl.BlockSpec(memory_space=pltpu.VMEM)
    return pl.pallas_call(
        body, name=name, in_specs=[vm], out_specs=vm,
        out_shape=jax.ShapeDtypeStruct((R, Wd), F32),
        scratch_shapes=[pltpu.VMEM((8, R, Wd), F32), pltpu.SemaphoreType.DMA((7,)), pltpu.SemaphoreType.DMA((7,))],
    )(v)


SMALL = (
    ("mix_norm_g", (2, 1024)), ("xattn_norm_g", (2, 1024)), ("mem_norm_g", (2, 1024)),
    ("ffn_norm_g", (2, 1024)), ("final_norm_g", (1024,)),
    ("forget_bias", (2, 8)), ("sink", (2, 8)), ("rel_bias", (32, 8)),
)
SMALL_AND_CONV = SMALL + (("conv_w", (2, 3, 512)),)


def _small_rows(spec):
    rows = sum(int(np.prod(s)) // 128 if s[-1] % 128 == 0 else s[0] for _, s in spec)
    return -(-rows // 8) * 8


def _pack_small(vals, spec=SMALL):
    rows = []
    for name, shape in spec:
        v = vals[name].astype(F32)
        if shape[-1] % 128 == 0:
            rows.append(v.reshape(-1, 128))
        else:
            rows.append(jnp.pad(v, ((0, 0), (0, 120))))
    rows = jnp.concatenate(rows, axis=0)
    return jnp.pad(rows, ((0, _small_rows(spec) - rows.shape[0]), (0, 0)))


def _unpack_small(pack, spec=SMALL):
    out, off = {}, 0
    for name, shape in spec:
        if shape[-1] % 128 == 0:
            n = int(np.prod(shape)) // 128
            out[name] = pack[off:off + n].reshape(shape)
        else:
            n = shape[0]
            out[name] = pack[off:off + n, 0:8]
        off += n
    return out


W_IN_PERM = ((3848, 6920), (0, 3072), (3080, 3848), (3072, 3080))


def _perm_w_in(w):
    parts = [w[:, a:b] for a, b in W_IN_PERM]
    return jnp.concatenate(parts + [jnp.zeros((w.shape[0], PROJ_PAD - IN_COLS), w.dtype)], axis=1)


def _unperm_w_in(p):
    return jnp.concatenate([p[:, 3072:6144], p[:, 6912:6920], p[:, 6144:6912], p[:, 0:3072]], axis=1)


def _pad_row8(v):
    return jnp.pad(v.astype(F32).reshape(1, 8), ((0, 0), (0, 120)))


def _local_step(x, mem, tgt, W, rel_bias):
    bucket = jnp.asarray(_bucket_table())
    bias = _swa_bias(rel_bias, bucket, "swa_bias")
    saved = []
    for l in range(DEPTH):
        n = "l%d_" % l
        s = {"x0": x}
        wcat = W["w_in_p"][l]
        h = _rms_fwd(x, W["mix_norm_g"][l:l + 1], n + "mix_norm")
        pm = _mm(h, wcat[:, :PROJ_MAIN], "nn", BF16, n + "proj", bn=768)
        fg = _mm(h, wcat[:, PROJ_MAIN:], "nn", F32, n + "proj_fg")
        fb = _pad_row8(W["forget_bias"][l])
        c_col = _fox_gate_fwd(fg, fb, n + "fox_gate")
        c_row = c_col[:, 0:8].T
        cw = jnp.pad(W["conv_w"][l], ((0, 5), (0, 0)))
        y_conv = _conv_fwd(pm, cw, n + "conv")
        y_fox, lse = _fox2_fwd(pm, c_row, n + "fox")
        sink = _pad_row8(W["sink"][l])
        y_swa, mlse = _swa_fwd(pm, bias, sink, n + "swa")
        ys = (y_conv, y_fox, y_swa)
        us = tuple(_mm(ys[b], W["w_branch"][l][b], "nn", BF16, n + "branch%d" % b) for b in range(3))
        merged = _merge_fwd(pm, us, n + "merge")
        x1 = _mm(merged, W["w_mix_out"][l], "nn", F32, n + "mix_out", res=x)
        xn1 = _rms_fwd(x1, W["xattn_norm_g"][l:l + 1], n + "xattn_norm")
        memn = _rms_fwd(mem, W["mem_norm_g"][l:l + 1], n + "mem_norm")
        qx = _mm(xn1, W["w_xq"][l], "nn", BF16, n + "xq")
        kv = _mm(memn, W["w_xkv"][l], "nn", BF16, n + "xkv")
        ox = _xattn_fwd(qx, kv, n + "xattn")
        x2 = _mm(ox, W["w_xo"][l], "nn", F32, n + "xo", res=x1)
        xn2 = _rms_fwd(x2, W["ffn_norm_g"][l:l + 1], n + "ffn_norm")
        ab = _mm(xn2, W["w_gu"][l], "nt", BF16, n + "ffn_in", bn=512)
        hm = _swiglu_fwd(ab, n + "swiglu")
        x3 = _mm(hm, W["w_ffn_down"][l], "nn", F32, n + "ffn_out", res=x2, bk=1408)
        s.update(h=h, pm=pm, fg=fg, fb=fb, c_col=c_col, c_row=c_row, cw=cw, ys=ys, lse=lse, sink=sink,
                 mlse=mlse, us=us, merged=merged, x1=x1, xn1=xn1, memn=memn, qx=qx, kv=kv, ox=ox,
                 x2=x2, xn2=xn2, ab=ab, hm=hm)
        saved.append(s)
        x = x3

    loss_row, dx, dxb, dg_final = _final_loss(x, W["final_norm_g"].reshape(1, D_MODEL), tgt, "final_loss")
    G = {name: [None] * DEPTH for name in
         ("mix_norm_g", "w_in_p", "forget_bias", "conv_w", "sink", "w_branch", "w_mix_out", "xattn_norm_g",
          "mem_norm_g", "w_xq", "w_xkv", "w_xo", "ffn_norm_g", "w_gu", "w_ffn_down")}
    dbias_tot = None
    for l in reversed(range(DEPTH)):
        n = "l%d_" % l
        s = saved[l]
        dhm = _mm(dxb, W["w_ffn_down"][l], "nt", BF16, n + "d_hm", bn=1408)
        G["w_ffn_down"][l] = _mm(s["hm"], dxb, "tn", BF16, n + "dw_down", bm=1408, bk=1024)
        dab = _swiglu_bwd(s["ab"], dhm, n + "d_swiglu")
        dxn2 = _mm(dab, W["w_gu"][l], "nn", BF16, n + "d_xn2", bk=1408)
        G["w_gu"][l] = _mm(dab, s["xn2"], "tn", BF16, n + "dw_gu", bm=512, bk=2048)
        dx, dxb, G["ffn_norm_g"][l] = _rms_bwd(s["x2"], W["ffn_norm_g"][l:l + 1], dxn2, dx, n + "d_ffn_norm")
        dox = _mm(dxb, W["w_xo"][l], "nt", BF16, n + "d_ox")
        G["w_xo"][l] = _mm(s["ox"], dxb, "tn", BF16, n + "dw_xo", bk=1024)
        dqx, dkv = _xattn_bwd(s["qx"], s["kv"], dox, n + "d_xattn")
        dxn1 = _mm(dqx, W["w_xq"][l], "nt", BF16, n + "d_xn1")
        G["w_xq"][l] = _mm(s["xn1"], dqx, "tn", BF16, n + "dw_xq", bk=2048)
        dmemn = _mm(dkv, W["w_xkv"][l], "nt", BF16, n + "d_memn")
        G["w_xkv"][l] = _mm(s["memn"], dkv, "tn", BF16, n + "dw_xkv")
        _, _, G["mem_norm_g"][l] = _rms_bwd(mem, W["mem_norm_g"][l:l + 1], dmemn, None, n + "d_mem_norm")
        dx, dxb, G["xattn_norm_g"][l] = _rms_bwd(s["x1"], W["xattn_norm_g"][l:l + 1], dxn1, dx, n + "d_xattn_norm")
        dmerged = _mm(dxb, W["w_mix_out"][l], "nt", BF16, n + "d_merged")
        G["w_mix_out"][l] = _mm(s["merged"], dxb, "tn", BF16, n + "dw_mix_out", bk=1024)
        du0, du1, du2, dproj = _merge_bwd(s["pm"], s["us"], dmerged, n + "d_merge")
        dus = (du0, du1, du2)
        dys = [_mm(dus[b], W["w_branch"][l][b], "nt", BF16, n + "d_y%d" % b) for b in range(3)]
        G["w_branch"][l] = [_mm(s["ys"][b], dus[b], "tn", BF16, n + "dw_branch%d" % b, bk=2048) for b in range(3)]
        dproj, dcw = _conv_bwd(s["pm"], s["cw"], dys[0], dproj, n + "d_conv")
        G["conv_w"][l] = dcw[0:3]
        delta = _fox_delta(s["ys"][1], dys[1], n + "fox_delta")
        dproj, delta = _fox2_bwd_dq(s["pm"], dys[1], s["c_row"], s["lse"], delta, dproj, n + "d_fox_q")
        dproj, dc = _fox2_bwd_dkv(s["pm"], dys[1], s["c_col"], s["lse"][:, 0:8].T, delta[:, 0:8].T, dproj,
                                  n + "d_fox_kv")
        dfg, dfb = _fox_gate_bwd(dc, s["fg"], s["fb"], n + "d_fox_gate")
        G["forget_bias"][l] = dfb[0, 0:8]
        dproj, dkc, dkp, dvc, dvp, dbias, dsink = _swa_bwd(s["pm"], bias, s["sink"], dys[2], s["mlse"], dproj,
                                                         n + "d_swa")
        G["sink"][l] = dsink[0, 0:8]
        dbias_tot = dbias if dbias_tot is None else dbias_tot + dbias
        zpad = jnp.zeros((WINDOW, 128), F32)
        dsk = dkc + jnp.concatenate([dkp[WINDOW:], zpad], axis=0)
        dsv = dvc + jnp.concatenate([dvp[WINDOW:], zpad], axis=0)
        tail = jnp.concatenate([dsk.astype(BF16), dsv.astype(BF16), dfg.astype(BF16)], axis=1)
        dproj = lax.dynamic_update_slice(dproj, tail, (0, PROJ_MAIN - 256))
        dh = _mm(dproj, W["w_in_p"][l], "nt", BF16, n + "d_h", bk=1408)
        G["w_in_p"][l] = _mm(s["h"], dproj, "tn", BF16, n + "dw_in", bn=1408, bk=2048)
        dx, dxb, G["mix_norm_g"][l] = _rms_bwd(s["x0"], W["mix_norm_g"][l:l + 1], dh, dx, n + "d_mix_norm")
    drb = _swa_dbias_reduce(dbias_tot, bucket, "swa_dbias")
    G["rel_bias"] = drb[:, 0:8]
    G["final_norm_g"] = dg_final.reshape(D_MODEL)
    return loss_row, dx, G


def kernel(x, mem, mix_norm_g, w_in, forget_bias, conv_w, sink, w_branch, w_mix_out, rel_bias, xattn_norm_g, mem_norm_g, w_xq, w_xkv, w_xo, ffn_norm_g, w_ffn_gate, w_ffn_up, w_ffn_down, final_norm_g, loss_target, m_mix_norm_g, m_w_in, m_forget_bias, m_conv_w, m_sink, m_w_branch, m_w_mix_out, m_rel_bias, m_xattn_norm_g, m_mem_norm_g, m_w_xq, m_w_xkv, m_w_xo, m_ffn_norm_g, m_w_ffn_gate, m_w_ffn_up, m_w_ffn_down, m_final_norm_g, v_mix_norm_g, v_w_in, v_forget_bias, v_conv_w, v_sink, v_w_branch, v_w_mix_out, v_rel_bias, v_xattn_norm_g, v_mem_norm_g, v_w_xq, v_w_xkv, v_w_xo, v_ffn_norm_g, v_w_ffn_gate, v_w_ffn_up, v_w_ffn_down, v_final_norm_g):
    order = ("mix_norm_g", "w_in", "forget_bias", "conv_w", "sink", "w_branch", "w_mix_out", "rel_bias",
             "xattn_norm_g", "mem_norm_g", "w_xq", "w_xkv", "w_xo", "ffn_norm_g", "w_ffn_gate", "w_ffn_up",
             "w_ffn_down", "final_norm_g")
    w_sh = dict(zip(order, (mix_norm_g, w_in, forget_bias, conv_w, sink, w_branch, w_mix_out, rel_bias,
                            xattn_norm_g, mem_norm_g, w_xq, w_xkv, w_xo, ffn_norm_g, w_ffn_gate, w_ffn_up,
                            w_ffn_down, final_norm_g)))
    m_sh = dict(zip(order, (m_mix_norm_g, m_w_in, m_forget_bias, m_conv_w, m_sink, m_w_branch, m_w_mix_out,
                            m_rel_bias, m_xattn_norm_g, m_mem_norm_g, m_w_xq, m_w_xkv, m_w_xo, m_ffn_norm_g,
                            m_w_ffn_gate, m_w_ffn_up, m_w_ffn_down, m_final_norm_g)))
    v_sh = dict(zip(order, (v_mix_norm_g, v_w_in, v_forget_bias, v_conv_w, v_sink, v_w_branch, v_w_mix_out,
                            v_rel_bias, v_xattn_norm_g, v_mem_norm_g, v_w_xq, v_w_xkv, v_w_xo, v_ffn_norm_g,
                            v_w_ffn_gate, v_w_ffn_up, v_w_ffn_down, v_final_norm_g)))

    xi, yi, ci = lax.axis_index("x"), lax.axis_index("y"), lax.axis_index("c")
    as_idx = lambda *v: jnp.stack([jnp.asarray(t, I32) for t in v])
    me = 2 * xi + yi
    big = [name for name, _ in BIG]
    two_d = dict(BIG)
    two_d["conv_w"] = (6, 128)

    def slab(a, name):
        return (jnp.swapaxes(a, 1, 2) if name in TRANSPOSED else a).reshape(two_d[name])

    def unslab(a, name):
        shape = w_sh[name].shape
        if name in TRANSPOSED:
            return jnp.swapaxes(a.reshape(shape[0], shape[2], shape[1]), 1, 2)
        return a.reshape(shape)

    gathered = dict(zip(big, _ag_ring_multi(
        [_cast_place(slab(w_sh[name], name), as_idx(me), "place_" + name) for name in big])))
    conv_part = lax.dynamic_update_slice_in_dim(jnp.zeros((DEPTH, 3, BRANCH), F32), 0.5 * conv_w, 128 * me, axis=2)
    conv_full = _allreduce_small(conv_part.reshape(-1, 128), "allgather_conv").reshape(DEPTH, 3, BRANCH)

    def lay(name, l):
        g = gathered[name]
        return g.reshape(4, DEPTH, g.shape[1] // DEPTH, g.shape[2])[:, l]

    def by_cols(name, l):
        g = lay(name, l)
        return jnp.moveaxis(g, 0, 1).reshape(g.shape[1], 4 * g.shape[2])

    def by_rows(name, l):
        g = lay(name, l)
        return g.reshape(4 * g.shape[1], g.shape[2])

    W = {k: w_sh[k] for k in ("mix_norm_g", "forget_bias", "sink", "xattn_norm_g", "mem_norm_g",
                              "ffn_norm_g", "final_norm_g")}
    W["conv_w"] = conv_full
    W["w_in_p"] = [_perm_w_in(by_cols("w_in", l)) for l in range(DEPTH)]
    W["w_gu"] = [jnp.concatenate([by_rows("w_ffn_gate", l), by_rows("w_ffn_up", l)], axis=0) for l in range(DEPTH)]
    W["w_xkv"] = [by_cols("w_xkv", l) for l in range(DEPTH)]
    W["w_branch"] = [[jnp.moveaxis(lay("w_branch", l)[:, BRANCH * b:BRANCH * (b + 1)], 0, 1).reshape(BRANCH, D_MODEL)
                      for b in range(3)] for l in range(DEPTH)]
    for k in ("w_mix_out", "w_xq", "w_xo", "w_ffn_down"):
        W[k] = [by_rows(k, l) for l in range(DEPTH)]
    loss_row, dx, G = _local_step(x[0], mem[0], loss_target[0], W, rel_bias)

    def to_cols(g):
        return jnp.moveaxis(g.reshape(g.shape[0], 4, g.shape[1] // 4), 1, 0)

    def to_rows(g):
        return g.reshape(4, g.shape[0] // 4, g.shape[1])

    per_layer = {
        "w_in": [to_cols(_unperm_w_in(G["w_in_p"][l])) for l in range(DEPTH)],
        "w_branch": [jnp.concatenate([to_cols(g) for g in G["w_branch"][l]], axis=1) for l in range(DEPTH)],
        "w_mix_out": [to_rows(g) for g in G["w_mix_out"]],
        "w_xq": [to_rows(g) for g in G["w_xq"]],
        "w_xkv": [to_cols(g) for g in G["w_xkv"]],
        "w_xo": [to_rows(g) for g in G["w_xo"]],
        "w_ffn_gate": [to_rows(G["w_gu"][l][:D_FF]) for l in range(DEPTH)],
        "w_ffn_up": [to_rows(G["w_gu"][l][D_FF:]) for l in range(DEPTH)],
        "w_ffn_down": [to_rows(g) for g in G["w_ffn_down"]],
    }
    g4 = [jnp.concatenate(per_layer[name], axis=1).astype(BF16) for name in big]
    sib = _rs_sibling_multi(g4)
    pair = [_rs_add_pair(g4[t], sib[t], as_idx(ci), "_" + big[t]) for t in range(len(big))]
    diag = _rs_diag_multi(pair)
    nbrs = as_idx(2 * (1 - xi) + yi, 2 * xi + (1 - yi))
    merged = [_rs_merge(pair[t], diag[t], nbrs, "_" + big[t]) for t in range(len(big))]
    got = _rs_direct_multi(merged)
    reduced = _rs_share_multi([_rs_final(pair[t], got[t], as_idx(me, ci), "_" + big[t]) for t in range(len(big))])

    small = _unpack_small(_allreduce_small(_pack_small({
        "mix_norm_g": jnp.concatenate(G["mix_norm_g"], axis=0),
        "xattn_norm_g": jnp.concatenate(G["xattn_norm_g"], axis=0),
        "mem_norm_g": jnp.concatenate(G["mem_norm_g"], axis=0),
        "ffn_norm_g": jnp.concatenate(G["ffn_norm_g"], axis=0),
        "final_norm_g": G["final_norm_g"],
        "forget_bias": jnp.stack(G["forget_bias"]),
        "sink": jnp.stack(G["sink"]),
        "rel_bias": G["rel_bias"],
        "conv_w": jnp.stack(G["conv_w"]),
    }, SMALL_AND_CONV)), SMALL_AND_CONV)
    grads = {name: unslab(reduced[t], name) for t, name in enumerate(big)}
    grads.update(small)
    grads["conv_w"] = lax.dynamic_slice_in_dim(small["conv_w"], 128 * me, 128, axis=2)

    sm_names = [name for name, _ in SMALL]
    sd, sm_, sv_ = _adamw(_pack_small({k: w_sh[k] for k in sm_names}), _pack_small({k: grads[k] for k in sm_names}),
                          _pack_small({k: m_sh[k] for k in sm_names}), _pack_small({k: v_sh[k] for k in sm_names}),
                          "adamw_small")
    delta, new_m, new_v = _unpack_small(sd), _unpack_small(sm_), _unpack_small(sv_)
    for t, name in enumerate(big + ["conv_w"]):
        if name == "w_in":
            to3 = lambda a: jnp.transpose(a, (2, 0, 1))
            d, nm, nv = _adamw(to3(w_sh[name]), to3(grads[name]), to3(m_sh[name]), to3(v_sh[name]), "adamw_w_in")
            delta[name], new_m[name], new_v[name] = (jnp.transpose(a, (1, 2, 0)) for a in (d, nm, nv))
            continue
        g2 = reduced[t] if t < len(big) else slab(grads[name], name)
        d, nm, nv = _adamw(slab(w_sh[name], name), g2, slab(m_sh[name], name), slab(v_sh[name], name), "adamw_" + name)
        delta[name], new_m[name], new_v[name] = unslab(d, name), unslab(nm, name), unslab(nv, name)

    loss = lax.psum(loss_row[0, 0], ("x", "y", "c"))
    return (loss, dx[None], *[grads[k] for k in order], *[delta[k] for k in order],
            *[new_m[k] for k in order], *[new_v[k] for k in order])
```
